```python
import math
import jax, jax.numpy as jnp
from jax import lax
import numpy as np

D_MODEL = 1024
BATCH = 32
SEQ = 2048
DEPTH = 1

D_FF = 2816
CONV_WIDTH = 1024
CONV_HEADS = 16
CONV_K = 3
SSM_WIDTH = 512
SSM_GROUP = 16
SSM_GROUPS = SSM_WIDTH // SSM_GROUP
SSM_STATE = 64
DT_MIN = 0.001
DT_MAX = 0.1
N_MOD = 9
EPS = 1e-6
IN_COLS = 3 * CONV_WIDTH + SSM_WIDTH + 2 * D_MODEL

kernel_name = "hybrid_shortconv_s5_macaron_block"


def rms_norm(x, g):
    xf = x.astype(jnp.float32)
    y = xf * lax.rsqrt(jnp.mean(xf * xf, axis=-1, keepdims=True) + EPS)
    return (y * g.astype(jnp.float32)).astype(x.dtype)


def modulate(h, shift, scale):
    return h * (1.0 + scale[:, None, :]) + shift[:, None, :]


def swiglu(h, w1, w3, w2):
    return (jax.nn.silu(h @ w1) * (h @ w3)) @ w2


def short_conv(v, w):
    return lax.conv_general_dilated(
        v, w[:, None, :].astype(v.dtype), window_strides=(1,),
        padding=((CONV_K - 1, 0),),
        dimension_numbers=("NWC", "WIO", "NWC"),
        feature_group_count=v.shape[-1])


def s5_ssm(u, a_re, a_im, b_re, b_im, c_re, c_im, log_dt):
    bsz, seq, _ = u.shape
    f32 = jnp.float32
    uf = u.astype(f32).reshape(bsz, seq, SSM_GROUPS, SSM_GROUP).transpose(1, 0, 2, 3)
    lam = lax.complex(a_re.astype(f32), a_im.astype(f32))
    dt = jnp.exp(log_dt.astype(f32))[:, None]
    a_bar = jnp.exp(lam * dt)
    b_mat = lax.complex(b_re.astype(f32), b_im.astype(f32))
    b_bar = ((a_bar - 1.0) / lam)[..., None] * b_mat
    c_mat = lax.complex(c_re.astype(f32), c_im.astype(f32))
    bu = jnp.einsum("sbgh,gph->sbgp", uf.astype(jnp.complex64), b_bar)
    a_seq = jnp.broadcast_to(a_bar, (seq, 1) + a_bar.shape)

    def combine(left, right):
        a_l, b_l = left
        a_r, b_r = right
        return (a_r * a_l, a_r * b_l + b_r)

    _, states = lax.associative_scan(combine, (a_seq, bu), axis=0)
    y = jnp.einsum("sbgp,ghp->sbgh", states, c_mat).real
    return y.transpose(1, 0, 2, 3).reshape(bsz, seq, SSM_WIDTH)


def hybrid_mixer(h, w_in, conv_w, w_conv_out, a_re, a_im, b_re, b_im, c_re, c_im,
                 log_dt, d_skip, w_glu, w_ssm_out, w_out):
    proj = h @ w_in
    cuts = [CONV_WIDTH, 2 * CONV_WIDTH, 3 * CONV_WIDTH,
            3 * CONV_WIDTH + SSM_WIDTH, 3 * CONV_WIDTH + SSM_WIDTH + D_MODEL]
    gate_b, gate_c, v, u, glog_a, glog_b = jnp.split(proj, cuts, axis=-1)
    y_a = (gate_b * short_conv(gate_c * v, conv_w)) @ w_conv_out
    s = s5_ssm(u, a_re, a_im, b_re, b_im, c_re, c_im, log_dt).astype(u.dtype) + d_skip * u
    s = jax.nn.gelu(s)
    s = s * jax.nn.sigmoid(s @ w_glu)
    y_b = s @ w_ssm_out
    merged = jax.nn.sigmoid(glog_a) * y_a + jax.nn.sigmoid(glog_b) * y_b
    return merged @ w_out


def _fwd_setup_inputs(seed: int = 0) -> dict:
    key = jax.random.key(seed)
    ks = jax.random.split(key, 32)
    L, D, G, P, H = DEPTH, D_MODEL, SSM_GROUPS, SSM_STATE, SSM_GROUP
    nrm = lambda k, shape, s: jax.random.normal(k, shape, jnp.float32) * s
    gain = lambda k: 1.0 + 0.02 * jax.random.normal(k, (L, D), jnp.float32)
    a_im_base = jnp.pi * jnp.arange(P, dtype=jnp.float32)
    return {
        "x": nrm(ks[0], (BATCH, SEQ, D), 1.0),
        "c": nrm(ks[1], (BATCH, D), 1.0),
        "w_ada": nrm(ks[2], (L, D, N_MOD * D), 0.5 * D ** -0.5),
        "b_ada": nrm(ks[3], (L, N_MOD * D), 0.01),
        "g_ffn1": gain(ks[4]),
        "w1_a": nrm(ks[5], (L, D, D_FF), D ** -0.5),
        "w3_a": nrm(ks[6], (L, D, D_FF), D ** -0.5),
        "w2_a": nrm(ks[7], (L, D_FF, D), D_FF ** -0.5),
        "g_mix": gain(ks[8]),
        "w_in": nrm(ks[9], (L, D, IN_COLS), D ** -0.5),
        "conv_w": nrm(ks[10], (L, CONV_K, CONV_WIDTH), CONV_K ** -0.5),
        "w_conv_out": nrm(ks[11], (L, CONV_WIDTH, D), CONV_WIDTH ** -0.5),
        "a_re": -0.5 + nrm(ks[12], (L, G, P), 0.01),
        "a_im": a_im_base + nrm(ks[13], (L, G, P), 0.01),
        "b_re": nrm(ks[14], (L, G, P, H), (2.0 * H) ** -0.5),
        "b_im": nrm(ks[15], (L, G, P, H), (2.0 * H) ** -0.5),
        "c_re": nrm(ks[16], (L, G, H, P), P ** -0.5),
        "c_im": nrm(ks[17], (L, G, H, P), P ** -0.5),
        "log_dt": jax.random.uniform(ks[18], (L, G), jnp.float32,
                                     math.log(DT_MIN), math.log(DT_MAX)),
        "d_skip": nrm(ks[19], (L, SSM_WIDTH), 1.0),
        "w_glu": nrm(ks[20], (L, SSM_WIDTH, SSM_WIDTH), SSM_WIDTH ** -0.5),
        "w_ssm_out": nrm(ks[21], (L, SSM_WIDTH, D), SSM_WIDTH ** -0.5),
        "w_out": nrm(ks[22], (L, D, D), D ** -0.5),
        "g_ffn2": gain(ks[23]),
        "w1_b": nrm(ks[24], (L, D, D_FF), D ** -0.5),
        "w3_b": nrm(ks[25], (L, D, D_FF), D ** -0.5),
        "w2_b": nrm(ks[26], (L, D_FF, D), D_FF ** -0.5),
        "g_final": 1.0 + 0.02 * jax.random.normal(ks[27], (D,), jnp.float32),
    }


def _fwd_reference(x, c, w_ada, b_ada, g_ffn1, w1_a, w3_a, w2_a, g_mix, w_in, conv_w,
              w_conv_out, a_re, a_im, b_re, b_im, c_re, c_im, log_dt, d_skip, w_glu,
              w_ssm_out, w_out, g_ffn2, w1_b, w3_b, w2_b, g_final):
    cond = jax.nn.silu(c)
    for l in range(DEPTH):
        mod = cond @ w_ada[l] + b_ada[l]
        sh1, sc1, gt1, sh2, sc2, gt2, sh3, sc3, gt3 = jnp.split(mod, N_MOD, axis=-1)
        h = modulate(rms_norm(x, g_ffn1[l]), sh1, sc1)
        x = x + 0.5 * gt1[:, None, :] * swiglu(h, w1_a[l], w3_a[l], w2_a[l])
        h = modulate(rms_norm(x, g_mix[l]), sh2, sc2)
        x = x + gt2[:, None, :] * hybrid_mixer(
            h, w_in[l], conv_w[l], w_conv_out[l], a_re[l], a_im[l], b_re[l], b_im[l],
            c_re[l], c_im[l], log_dt[l], d_skip[l], w_glu[l], w_ssm_out[l], w_out[l])
        h = modulate(rms_norm(x, g_ffn2[l]), sh3, sc3)
        x = x + 0.5 * gt3[:, None, :] * swiglu(h, w1_b[l], w3_b[l], w2_b[l])
    return rms_norm(x, g_final)


import jax as _jax
import jax.numpy as _jnp

TWIN_FORMAT = 'train_step'
FWD_PARAMS = ['x', 'c', 'w_ada', 'b_ada', 'g_ffn1', 'w1_a', 'w3_a', 'w2_a', 'g_mix', 'w_in', 'conv_w', 'w_conv_out', 'a_re', 'a_im', 'b_re', 'b_im', 'c_re', 'c_im', 'log_dt', 'd_skip', 'w_glu', 'w_ssm_out', 'w_out', 'g_ffn2', 'w1_b', 'w3_b', 'w2_b', 'g_final']
TWIN_WEIGHTS = ['w_ada', 'b_ada', 'g_ffn1', 'w1_a', 'w3_a', 'w2_a', 'g_mix', 'w_in', 'conv_w', 'w_conv_out', 'a_re', 'a_im', 'b_re', 'b_im', 'c_re', 'c_im', 'log_dt', 'd_skip', 'w_glu', 'w_ssm_out', 'w_out', 'g_ffn2', 'w1_b', 'w3_b', 'w2_b', 'g_final']
TWIN_DIFF_INPUT = 'x'
TWIN_INPUTS = ['x', 'c', 'w_ada', 'b_ada', 'g_ffn1', 'w1_a', 'w3_a', 'w2_a', 'g_mix', 'w_in', 'conv_w', 'w_conv_out', 'a_re', 'a_im', 'b_re', 'b_im', 'c_re', 'c_im', 'log_dt', 'd_skip', 'w_glu', 'w_ssm_out', 'w_out', 'g_ffn2', 'w1_b', 'w3_b', 'w2_b', 'g_final', 'loss_target', 'm_w_ada', 'm_b_ada', 'm_g_ffn1', 'm_w1_a', 'm_w3_a', 'm_w2_a', 'm_g_mix', 'm_w_in', 'm_conv_w', 'm_w_conv_out', 'm_a_re', 'm_a_im', 'm_b_re', 'm_b_im', 'm_c_re', 'm_c_im', 'm_log_dt', 'm_d_skip', 'm_w_glu', 'm_w_ssm_out', 'm_w_out', 'm_g_ffn2', 'm_w1_b', 'm_w3_b', 'm_w2_b', 'm_g_final', 'v_w_ada', 'v_b_ada', 'v_g_ffn1', 'v_w1_a', 'v_w3_a', 'v_w2_a', 'v_g_mix', 'v_w_in', 'v_conv_w', 'v_w_conv_out', 'v_a_re', 'v_a_im', 'v_b_re', 'v_b_im', 'v_c_re', 'v_c_im', 'v_log_dt', 'v_d_skip', 'v_w_glu', 'v_w_ssm_out', 'v_w_out', 'v_g_ffn2', 'v_w1_b', 'v_w3_b', 'v_w2_b', 'v_g_final']
TWIN_OUTPUTS = ['loss', 'grad_x', 'grad_w_ada', 'grad_b_ada', 'grad_g_ffn1', 'grad_w1_a', 'grad_w3_a', 'grad_w2_a', 'grad_g_mix', 'grad_w_in', 'grad_conv_w', 'grad_w_conv_out', 'grad_a_re', 'grad_a_im', 'grad_b_re', 'grad_b_im', 'grad_c_re', 'grad_c_im', 'grad_log_dt', 'grad_d_skip', 'grad_w_glu', 'grad_w_ssm_out', 'grad_w_out', 'grad_g_ffn2', 'grad_w1_b', 'grad_w3_b', 'grad_w2_b', 'grad_g_final', 'delta_w_ada', 'delta_b_ada', 'delta_g_ffn1', 'delta_w1_a', 'delta_w3_a', 'delta_w2_a', 'delta_g_mix', 'delta_w_in', 'delta_conv_w', 'delta_w_conv_out', 'delta_a_re', 'delta_a_im', 'delta_b_re', 'delta_b_im', 'delta_c_re', 'delta_c_im', 'delta_log_dt', 'delta_d_skip', 'delta_w_glu', 'delta_w_ssm_out', 'delta_w_out', 'delta_g_ffn2', 'delta_w1_b', 'delta_w3_b', 'delta_w2_b', 'delta_g_final', 'new_m_w_ada', 'new_m_b_ada', 'new_m_g_ffn1', 'new_m_w1_a', 'new_m_w3_a', 'new_m_w2_a', 'new_m_g_mix', 'new_m_w_in', 'new_m_conv_w', 'new_m_w_conv_out', 'new_m_a_re', 'new_m_a_im', 'new_m_b_re', 'new_m_b_im', 'new_m_c_re', 'new_m_c_im', 'new_m_log_dt', 'new_m_d_skip', 'new_m_w_glu', 'new_m_w_ssm_out', 'new_m_w_out', 'new_m_g_ffn2', 'new_m_w1_b', 'new_m_w3_b', 'new_m_w2_b', 'new_m_g_final', 'new_v_w_ada', 'new_v_b_ada', 'new_v_g_ffn1', 'new_v_w1_a', 'new_v_w3_a', 'new_v_w2_a', 'new_v_g_mix', 'new_v_w_in', 'new_v_conv_w', 'new_v_w_conv_out', 'new_v_a_re', 'new_v_a_im', 'new_v_b_re', 'new_v_b_im', 'new_v_c_re', 'new_v_c_im', 'new_v_log_dt', 'new_v_d_skip', 'new_v_w_glu', 'new_v_w_ssm_out', 'new_v_w_out', 'new_v_g_ffn2', 'new_v_w1_b', 'new_v_w3_b', 'new_v_w2_b', 'new_v_g_final']
TWIN_LEAF_KINDS = {'loss': 'loss', 'grad_x': 'grad_x', 'grad_w_ada': 'grad_w', 'grad_b_ada': 'grad_w', 'grad_g_ffn1': 'grad_w', 'grad_w1_a': 'grad_w', 'grad_w3_a': 'grad_w', 'grad_w2_a': 'grad_w', 'grad_g_mix': 'grad_w', 'grad_w_in': 'grad_w', 'grad_conv_w': 'grad_w', 'grad_w_conv_out': 'grad_w', 'grad_a_re': 'grad_w', 'grad_a_im': 'grad_w', 'grad_b_re': 'grad_w', 'grad_b_im': 'grad_w', 'grad_c_re': 'grad_w', 'grad_c_im': 'grad_w', 'grad_log_dt': 'grad_w', 'grad_d_skip': 'grad_w', 'grad_w_glu': 'grad_w', 'grad_w_ssm_out': 'grad_w', 'grad_w_out': 'grad_w', 'grad_g_ffn2': 'grad_w', 'grad_w1_b': 'grad_w', 'grad_w3_b': 'grad_w', 'grad_w2_b': 'grad_w', 'grad_g_final': 'grad_w', 'delta_w_ada': 'delta_w', 'delta_b_ada': 'delta_w', 'delta_g_ffn1': 'delta_w', 'delta_w1_a': 'delta_w', 'delta_w3_a': 'delta_w', 'delta_w2_a': 'delta_w', 'delta_g_mix': 'delta_w', 'delta_w_in': 'delta_w', 'delta_conv_w': 'delta_w', 'delta_w_conv_out': 'delta_w', 'delta_a_re': 'delta_w', 'delta_a_im': 'delta_w', 'delta_b_re': 'delta_w', 'delta_b_im': 'delta_w', 'delta_c_re': 'delta_w', 'delta_c_im': 'delta_w', 'delta_log_dt': 'delta_w', 'delta_d_skip': 'delta_w', 'delta_w_glu': 'delta_w', 'delta_w_ssm_out': 'delta_w', 'delta_w_out': 'delta_w', 'delta_g_ffn2': 'delta_w', 'delta_w1_b': 'delta_w', 'delta_w3_b': 'delta_w', 'delta_w2_b': 'delta_w', 'delta_g_final': 'delta_w', 'new_m_w_ada': 'new_m', 'new_m_b_ada': 'new_m', 'new_m_g_ffn1': 'new_m', 'new_m_w1_a': 'new_m', 'new_m_w3_a': 'new_m', 'new_m_w2_a': 'new_m', 'new_m_g_mix': 'new_m', 'new_m_w_in': 'new_m', 'new_m_conv_w': 'new_m', 'new_m_w_conv_out': 'new_m', 'new_m_a_re': 'new_m', 'new_m_a_im': 'new_m', 'new_m_b_re': 'new_m', 'new_m_b_im': 'new_m', 'new_m_c_re': 'new_m', 'new_m_c_im': 'new_m', 'new_m_log_dt': 'new_m', 'new_m_d_skip': 'new_m', 'new_m_w_glu': 'new_m', 'new_m_w_ssm_out': 'new_m', 'new_m_w_out': 'new_m', 'new_m_g_ffn2': 'new_m', 'new_m_w1_b': 'new_m', 'new_m_w3_b': 'new_m', 'new_m_w2_b': 'new_m', 'new_m_g_final': 'new_m', 'new_v_w_ada': 'new_v', 'new_v_b_ada': 'new_v', 'new_v_g_ffn1': 'new_v', 'new_v_w1_a': 'new_v', 'new_v_w3_a': 'new_v', 'new_v_w2_a': 'new_v', 'new_v_g_mix': 'new_v', 'new_v_w_in': 'new_v', 'new_v_conv_w': 'new_v', 'new_v_w_conv_out': 'new_v', 'new_v_a_re': 'new_v', 'new_v_a_im': 'new_v', 'new_v_b_re': 'new_v', 'new_v_b_im': 'new_v', 'new_v_c_re': 'new_v', 'new_v_c_im': 'new_v', 'new_v_log_dt': 'new_v', 'new_v_d_skip': 'new_v', 'new_v_w_glu': 'new_v', 'new_v_w_ssm_out': 'new_v', 'new_v_w_out': 'new_v', 'new_v_g_ffn2': 'new_v', 'new_v_w1_b': 'new_v', 'new_v_w3_b': 'new_v', 'new_v_w2_b': 'new_v', 'new_v_g_final': 'new_v'}


def _forward(args):
    return _fwd_reference(*[args[k] for k in FWD_PARAMS])


def _output_shape():
    out = _jax.eval_shape(lambda: _forward(_fwd_setup_inputs(0)))
    return out.shape, out.dtype

N_MICROBATCH = 1
ADAM_LR = 0.001
ADAM_B1 = 0.9
ADAM_B2 = 0.999
ADAM_EPS = 1e-08
ADAM_WD = 0.01
ADAM_STEP = 10
PER_EXAMPLE_BATCH_AXIS = {'x': 0, 'c': 0, 'loss_target': 0}
SHARED_INPUTS = []
_WEIGHT_DTYPES = {'w_ada': _jnp.float32, 'b_ada': _jnp.float32, 'g_ffn1': _jnp.float32, 'w1_a': _jnp.float32, 'w3_a': _jnp.float32, 'w2_a': _jnp.float32, 'g_mix': _jnp.float32, 'w_in': _jnp.float32, 'conv_w': _jnp.float32, 'w_conv_out': _jnp.float32, 'a_re': _jnp.float32, 'a_im': _jnp.float32, 'b_re': _jnp.float32, 'b_im': _jnp.float32, 'c_re': _jnp.float32, 'c_im': _jnp.float32, 'log_dt': _jnp.float32, 'd_skip': _jnp.float32, 'w_glu': _jnp.float32, 'w_ssm_out': _jnp.float32, 'w_out': _jnp.float32, 'g_ffn2': _jnp.float32, 'w1_b': _jnp.float32, 'w3_b': _jnp.float32, 'w2_b': _jnp.float32, 'g_final': _jnp.float32}
MOMENT_SCALE = {'w_ada': 5.532355e-02, 'b_ada': 9.669099e-02, 'g_ffn1': 3.991673e-02, 'w1_a': 1.774775e-02, 'w3_a': 1.713916e-02, 'w2_a': 2.846330e-02, 'g_mix': 9.466806e-02, 'w_in': 4.176324e-02, 'conv_w': 5.373278e-02, 'w_conv_out': 5.347745e-02, 'a_re': 2.509799e-03, 'a_im': 3.609838e-03, 'b_re': 1.534291e-03, 'b_im': 1.512671e-03, 'c_re': 2.067714e-03, 'c_im': 2.172946e-03, 'log_dt': 7.374715e-01, 'd_skip': 2.709403e-02, 'w_glu': 8.006086e-03, 'w_ssm_out': 1.583873e-02, 'w_out': 5.607005e-02, 'g_ffn2': 4.204337e-02, 'w1_b': 1.674707e-02, 'w3_b': 1.608665e-02, 'w2_b': 2.670388e-02, 'g_final': 6.393377e+01}


def _to_microbatches(a, axis):
    t = _jnp.moveaxis(a, axis, 0)
    t = t.reshape((N_MICROBATCH, t.shape[0] // N_MICROBATCH) + t.shape[1:])
    return _jnp.moveaxis(t, 1, axis + 1)


def setup_inputs(seed: int = 0) -> dict:
    inp = _fwd_setup_inputs(seed)
    key = _jax.random.fold_in(_jax.random.key(seed), 7919)
    shape, _ = _output_shape()
    out = dict(inp)
    out["loss_target"] = _jax.random.normal(_jax.random.fold_in(key, 0), shape, _jnp.float32)
    for i, name in enumerate(TWIN_WEIGHTS):
        w = inp[name].astype(_jnp.float32)
        if MOMENT_SCALE is None:
            s = _jnp.sqrt(_jnp.mean(_jnp.square(w)) + 1e-30)
        else:
            s = MOMENT_SCALE[name]
        km, kv = _jax.random.split(_jax.random.fold_in(key, i + 1))
        out[name] = w
        out["m_" + name] = s * _jax.random.normal(km, w.shape, _jnp.float32)
        out["v_" + name] = (s * s) * _jax.random.uniform(kv, w.shape, _jnp.float32, 0.5, 1.5)
    if N_MICROBATCH > 1:
        for name, axis in PER_EXAMPLE_BATCH_AXIS.items():
            out[name] = _to_microbatches(out[name], axis)
    return {'x': out['x'], 'c': out['c'], 'w_ada': out['w_ada'], 'b_ada': out['b_ada'], 'g_ffn1': out['g_ffn1'], 'w1_a': out['w1_a'], 'w3_a': out['w3_a'], 'w2_a': out['w2_a'], 'g_mix': out['g_mix'], 'w_in': out['w_in'], 'conv_w': out['conv_w'], 'w_conv_out': out['w_conv_out'], 'a_re': out['a_re'], 'a_im': out['a_im'], 'b_re': out['b_re'], 'b_im': out['b_im'], 'c_re': out['c_re'], 'c_im': out['c_im'], 'log_dt': out['log_dt'], 'd_skip': out['d_skip'], 'w_glu': out['w_glu'], 'w_ssm_out': out['w_ssm_out'], 'w_out': out['w_out'], 'g_ffn2': out['g_ffn2'], 'w1_b': out['w1_b'], 'w3_b': out['w3_b'], 'w2_b': out['w2_b'], 'g_final': out['g_final'], 'loss_target': out['loss_target'], 'm_w_ada': out['m_w_ada'], 'm_b_ada': out['m_b_ada'], 'm_g_ffn1': out['m_g_ffn1'], 'm_w1_a': out['m_w1_a'], 'm_w3_a': out['m_w3_a'], 'm_w2_a': out['m_w2_a'], 'm_g_mix': out['m_g_mix'], 'm_w_in': out['m_w_in'], 'm_conv_w': out['m_conv_w'], 'm_w_conv_out': out['m_w_conv_out'], 'm_a_re': out['m_a_re'], 'm_a_im': out['m_a_im'], 'm_b_re': out['m_b_re'], 'm_b_im': out['m_b_im'], 'm_c_re': out['m_c_re'], 'm_c_im': out['m_c_im'], 'm_log_dt': out['m_log_dt'], 'm_d_skip': out['m_d_skip'], 'm_w_glu': out['m_w_glu'], 'm_w_ssm_out': out['m_w_ssm_out'], 'm_w_out': out['m_w_out'], 'm_g_ffn2': out['m_g_ffn2'], 'm_w1_b': out['m_w1_b'], 'm_w3_b': out['m_w3_b'], 'm_w2_b': out['m_w2_b'], 'm_g_final': out['m_g_final'], 'v_w_ada': out['v_w_ada'], 'v_b_ada': out['v_b_ada'], 'v_g_ffn1': out['v_g_ffn1'], 'v_w1_a': out['v_w1_a'], 'v_w3_a': out['v_w3_a'], 'v_w2_a': out['v_w2_a'], 'v_g_mix': out['v_g_mix'], 'v_w_in': out['v_w_in'], 'v_conv_w': out['v_conv_w'], 'v_w_conv_out': out['v_w_conv_out'], 'v_a_re': out['v_a_re'], 'v_a_im': out['v_a_im'], 'v_b_re': out['v_b_re'], 'v_b_im': out['v_b_im'], 'v_c_re': out['v_c_re'], 'v_c_im': out['v_c_im'], 'v_log_dt': out['v_log_dt'], 'v_d_skip': out['v_d_skip'], 'v_w_glu': out['v_w_glu'], 'v_w_ssm_out': out['v_w_ssm_out'], 'v_w_out': out['v_w_out'], 'v_g_ffn2': out['v_g_ffn2'], 'v_w1_b': out['v_w1_b'], 'v_w3_b': out['v_w3_b'], 'v_w2_b': out['v_w2_b'], 'v_g_final': out['v_g_final']}


def _loss(weights, diff, rest, loss_target):
    with _jax.named_scope("forward"):
        args = {**rest, TWIN_DIFF_INPUT: diff, **{k: w.astype(_WEIGHT_DTYPES[k]) for k, w in weights.items()}}
        y = _forward(args)
    with _jax.named_scope("loss_head"):
        err = _jnp.square(y.astype(_jnp.float32) - loss_target)
        return 0.5 * _jnp.sum(_jnp.mean(err, axis=-1)) if err.ndim else 0.5 * err


def _adamw(w, g, m, v):
    m = ADAM_B1 * m + (1.0 - ADAM_B1) * g
    v = ADAM_B2 * v + (1.0 - ADAM_B2) * _jnp.square(g)
    m_hat = m / (1.0 - ADAM_B1 ** ADAM_STEP)
    v_hat = v / (1.0 - ADAM_B2 ** ADAM_STEP)
    delta = -ADAM_LR * (m_hat / (_jnp.sqrt(v_hat) + ADAM_EPS) + ADAM_WD * w)
    return delta, m, v


def reference(x, c, w_ada, b_ada, g_ffn1, w1_a, w3_a, w2_a, g_mix, w_in, conv_w, w_conv_out, a_re, a_im, b_re, b_im, c_re, c_im, log_dt, d_skip, w_glu, w_ssm_out, w_out, g_ffn2, w1_b, w3_b, w2_b, g_final, loss_target, m_w_ada, m_b_ada, m_g_ffn1, m_w1_a, m_w3_a, m_w2_a, m_g_mix, m_w_in, m_conv_w, m_w_conv_out, m_a_re, m_a_im, m_b_re, m_b_im, m_c_re, m_c_im, m_log_dt, m_d_skip, m_w_glu, m_w_ssm_out, m_w_out, m_g_ffn2, m_w1_b, m_w3_b, m_w2_b, m_g_final, v_w_ada, v_b_ada, v_g_ffn1, v_w1_a, v_w3_a, v_w2_a, v_g_mix, v_w_in, v_conv_w, v_w_conv_out, v_a_re, v_a_im, v_b_re, v_b_im, v_c_re, v_c_im, v_log_dt, v_d_skip, v_w_glu, v_w_ssm_out, v_w_out, v_g_ffn2, v_w1_b, v_w3_b, v_w2_b, v_g_final):
    given = dict(x=x, c=c, w_ada=w_ada, b_ada=b_ada, g_ffn1=g_ffn1, w1_a=w1_a, w3_a=w3_a, w2_a=w2_a, g_mix=g_mix, w_in=w_in, conv_w=conv_w, w_conv_out=w_conv_out, a_re=a_re, a_im=a_im, b_re=b_re, b_im=b_im, c_re=c_re, c_im=c_im, log_dt=log_dt, d_skip=d_skip, w_glu=w_glu, w_ssm_out=w_ssm_out, w_out=w_out, g_ffn2=g_ffn2, w1_b=w1_b, w3_b=w3_b, w2_b=w2_b, g_final=g_final, loss_target=loss_target, m_w_ada=m_w_ada, m_b_ada=m_b_ada, m_g_ffn1=m_g_ffn1, m_w1_a=m_w1_a, m_w3_a=m_w3_a, m_w2_a=m_w2_a, m_g_mix=m_g_mix, m_w_in=m_w_in, m_conv_w=m_conv_w, m_w_conv_out=m_w_conv_out, m_a_re=m_a_re, m_a_im=m_a_im, m_b_re=m_b_re, m_b_im=m_b_im, m_c_re=m_c_re, m_c_im=m_c_im, m_log_dt=m_log_dt, m_d_skip=m_d_skip, m_w_glu=m_w_glu, m_w_ssm_out=m_w_ssm_out, m_w_out=m_w_out, m_g_ffn2=m_g_ffn2, m_w1_b=m_w1_b, m_w3_b=m_w3_b, m_w2_b=m_w2_b, m_g_final=m_g_final, v_w_ada=v_w_ada, v_b_ada=v_b_ada, v_g_ffn1=v_g_ffn1, v_w1_a=v_w1_a, v_w3_a=v_w3_a, v_w2_a=v_w2_a, v_g_mix=v_g_mix, v_w_in=v_w_in, v_conv_w=v_conv_w, v_w_conv_out=v_w_conv_out, v_a_re=v_a_re, v_a_im=v_a_im, v_b_re=v_b_re, v_b_im=v_b_im, v_c_re=v_c_re, v_c_im=v_c_im, v_log_dt=v_log_dt, v_d_skip=v_d_skip, v_w_glu=v_w_glu, v_w_ssm_out=v_w_ssm_out, v_w_out=v_w_out, v_g_ffn2=v_g_ffn2, v_w1_b=v_w1_b, v_w3_b=v_w3_b, v_w2_b=v_w2_b, v_g_final=v_g_final)
    weights = {n: given[n] for n in TWIN_WEIGHTS}
    shared = {n: given[n] for n in SHARED_INPUTS}
    per_example = {n: given[n] for n in ['x', 'c']}
    grad_fn = _jax.value_and_grad(_loss, argnums=(0, 1))

    def one_microbatch(ex, loss_target):
        ex = dict(ex)
        diff = ex.pop(TWIN_DIFF_INPUT)
        return grad_fn(weights, diff, {**shared, **ex}, loss_target)

    if N_MICROBATCH == 1:
        loss, (grad_w, grad_x) = one_microbatch(per_example, given["loss_target"])
    else:
        def body(carry, xs):
            loss_sum, grad_sum = carry
            l_k, (gw_k, gx_k) = one_microbatch(xs[0], xs[1])
            with _jax.named_scope("update"):
                return (loss_sum + l_k, _jax.tree.map(_jnp.add, grad_sum, gw_k)), gx_k

        init = (_jnp.zeros((), _jnp.float32), _jax.tree.map(_jnp.zeros_like, weights))
        (loss, grad_w), grad_x = _jax.lax.scan(body, init, (per_example, given["loss_target"]))
    with _jax.named_scope("update"):
        delta_w, new_m, new_v = {}, {}, {}
        for n in TWIN_WEIGHTS:
            delta_w[n], new_m[n], new_v[n] = _adamw(weights[n], grad_w[n], given["m_" + n], given["v_" + n])
    return (loss, grad_x, *[grad_w[n] for n in TWIN_WEIGHTS], *[delta_w[n] for n in TWIN_WEIGHTS],
            *[new_m[n] for n in TWIN_WEIGHTS], *[new_v[n] for n in TWIN_WEIGHTS])
```

```python
import functools
import math

import jax
import jax.numpy as jnp
from jax import lax
from jax.experimental import pallas as pl
from jax.experimental.pallas import tpu as pltpu

F32 = jnp.float32
BF16 = jnp.bfloat16
N_DEV = 8
N_MOD = 9
EPS = 1e-6
CONV_K = 3
ADAM_LR = 0.001
ADAM_B1 = 0.9
ADAM_B2 = 0.999
ADAM_EPS = 1e-08
ADAM_WD = 0.01
ADAM_STEP = 10
GELU_C0 = math.sqrt(2.0 / math.pi)
GELU_C1 = 0.044715
V7X_VMEM_LIMIT = 56 * 1024 * 1024
MESH_ID = pl.DeviceIdType.MESH
NT = (((1,), (1,)), ((), ()))
TN = (((0,), (0,)), ((), ()))


def _dot(a, b, dims=None):
    if dims is None:
        return jnp.dot(a, b, preferred_element_type=F32)
    return lax.dot_general(a, b, dims, preferred_element_type=F32)


def _params(sem=None, vmem=V7X_VMEM_LIMIT):
    return pltpu.CompilerParams(dimension_semantics=sem, vmem_limit_bytes=vmem)


def _full(shape):
    return pl.BlockSpec(shape, lambda *_: (0,) * len(shape))


def _const(shape):
    return pl.BlockSpec(shape, lambda *_: (0,) * len(shape), pipeline_mode=pl.Buffered(1))


def _tile(n, want):
    t = min(n, want)
    while n % t:
        t //= 2
    return t


def all_gather_rows(shards, name):
    n = len(shards)

    def body(*refs):
        ins, outs = refs[:n], refs[n:2 * n]
        send_sems, recv_sems, local_sems = refs[2 * n:]
        x, y, c = lax.axis_index("x"), lax.axis_index("y"), lax.axis_index("c")
        me, sibling = (x, y, c), (x, y, 1 - c)
        chips = [(1 - x, y), (x, 1 - y), (1 - x, 1 - y)]

        def rows(k, px, py, pc):
            r = ins[k].shape[0]
            return outs[k].at[pl.ds((4 * px + 2 * py + pc) * r, r), :]

        def copy(k, j, block, to, src=None):
            return pltpu.make_async_remote_copy(
                src_ref=rows(k, *block) if src is None else src, dst_ref=rows(k, *block),
                send_sem=send_sems.at[7 * k + j], recv_sem=recv_sems.at[7 * k + j],
                device_id=to, device_id_type=MESH_ID)

        mine = [pltpu.make_async_copy(ins[k], rows(k, *me), local_sems.at[k]) for k in range(n)]
        for cp in mine:
            cp.start()
        first = []
        for k in range(n):
            first.append(copy(k, 0, me, sibling, src=ins[k]))
            first += [copy(k, 1 + j, me, (*chip, c), src=ins[k]) for j, chip in enumerate(chips)]
        for cp in first:
            cp.start()
        passed = []
        for j, chip in enumerate(chips):
            for k in range(n):
                copy(k, 1 + j, (*chip, c), me).wait_recv()
                cp = copy(k, 4 + j, (*chip, c), sibling)
                cp.start()
                passed.append(cp)
        for k in range(n):
            copy(k, 0, sibling, me).wait_recv()
            for j, chip in enumerate(chips):
                copy(k, 4 + j, (*chip, 1 - c), me).wait_recv()
        for cp in first + passed:
            cp.wait_send()
        for cp in mine:
            cp.wait()

    any_spec = pl.BlockSpec(memory_space=pl.ANY)
    return pl.pallas_call(
        body, name=name,
        out_shape=[jax.ShapeDtypeStruct((N_DEV * s.shape[0], s.shape[1]), s.dtype) for s in shards],
        in_specs=[any_spec] * n, out_specs=[any_spec] * n,
        scratch_shapes=[pltpu.SemaphoreType.DMA((7 * n,)), pltpu.SemaphoreType.DMA((7 * n,)),
                        pltpu.SemaphoreType.DMA((n,))],
    )(*shards)


def all_to_all_slots(bufs, name):
    n = len(bufs)

    def body(*refs):
        ins, outs = refs[:n], refs[n:2 * n]
        send_sems, recv_sems, local_sems = refs[2 * n:]
        x, y, c = lax.axis_index("x"), lax.axis_index("y"), lax.axis_index("c")
        me = 4 * x + 2 * y + c
        mine = [pltpu.make_async_copy(ins[k].at[me], outs[k].at[me], local_sems.at[k]) for k in range(n)]
        for cp in mine:
            cp.start()
        copies = []
        for mask in range(1, N_DEV):
            px, py, pc = x ^ (mask >> 2), y ^ ((mask >> 1) & 1), c ^ (mask & 1)
            peer = 4 * px + 2 * py + pc
            for k in range(n):
                copies.append(pltpu.make_async_remote_copy(
                    src_ref=ins[k].at[peer], dst_ref=outs[k].at[me],
                    send_sem=send_sems.at[7 * k + mask - 1], recv_sem=recv_sems.at[7 * k + mask - 1],
                    device_id=(px, py, pc), device_id_type=MESH_ID))
        for cp in copies:
            cp.start()
        for cp in copies:
            cp.wait_recv()
        for cp in copies:
            cp.wait_send()
        for cp in mine:
            cp.wait()

    any_spec = pl.BlockSpec(memory_space=pl.ANY)
    return pl.pallas_call(
        body, name=name,
        out_shape=[jax.ShapeDtypeStruct(b.shape, b.dtype) for b in bufs],
        in_specs=[any_spec] * n, out_specs=[any_spec] * n,
        scratch_shapes=[pltpu.SemaphoreType.DMA((7 * n,)), pltpu.SemaphoreType.DMA((7 * n,)),
                        pltpu.SemaphoreType.DMA((n,))],
    )(*bufs)


def _norm_mod(x, g, shift, scale):
    r = lax.rsqrt(jnp.mean(x * x, axis=-1, keepdims=True) + EPS)
    n = x * r
    return (n * g) * (1.0 + scale) + shift, n, r


def _norm_mod_bwd(dh, n, r, g, scale):
    dsh = jnp.sum(dh, axis=0, keepdims=True)
    dsc = jnp.sum(dh * (n * g), axis=0, keepdims=True)
    dg = jnp.sum(dh * (1.0 + scale) * n, axis=0, keepdims=True)
    dn = dh * ((1.0 + scale) * g)
    dx = r * (dn - n * jnp.mean(n * dn, axis=-1, keepdims=True))
    return dx, dsh, dsc, dg


def _mod_rows(mod_ref, sub):
    m = mod_ref[0]
    return m[3 * sub:3 * sub + 1], m[3 * sub + 1:3 * sub + 2], m[3 * sub + 2:3 * sub + 3]


def _gelu(x):
    t = jnp.tanh(GELU_C0 * (x + GELU_C1 * x * x * x))
    return 0.5 * x * (1.0 + t), t


def _gelu_grad(x, t):
    return 0.5 * (1.0 + t) + 0.5 * x * (1.0 - t * t) * (GELU_C0 * (1.0 + 3.0 * GELU_C1 * x * x))


def _accumulate(ref, val, first):
    @pl.when(first)
    def _():
        ref[...] = val

    @pl.when(jnp.logical_not(first))
    def _():
        ref[...] += val


def ada_forward(c_all, w_ada, b_ada_cols):
    def body(c_ref, w_ref, b_ref, o_ref):
        c = c_ref[...]
        cond = (c * jax.nn.sigmoid(c)).astype(BF16)
        o_ref[...] = _dot(cond, w_ref[...].astype(BF16)) + b_ref[...]

    nb, d = c_all.shape
    cols = w_ada.shape[1]
    tn = _tile(cols, 384)
    return pl.pallas_call(
        body, name="ada_forward", grid=(cols // tn,),
        out_shape=jax.ShapeDtypeStruct((nb, cols), F32),
        in_specs=[_full((nb, d)), pl.BlockSpec((d, tn), lambda j: (0, j)), pl.BlockSpec((1, tn), lambda j: (0, j))],
        out_specs=pl.BlockSpec((nb, tn), lambda j: (0, j)),
        compiler_params=_params(("arbitrary",)),
    )(c_all, w_ada, b_ada_cols)


def _adamw(w, g, m, v):
    m = ADAM_B1 * m + (1.0 - ADAM_B1) * g
    v = ADAM_B2 * v + (1.0 - ADAM_B2) * (g * g)
    m_hat = m / (1.0 - ADAM_B1 ** ADAM_STEP)
    v_hat = v / (1.0 - ADAM_B2 ** ADAM_STEP)
    delta = -ADAM_LR * (m_hat / (jnp.sqrt(v_hat) + ADAM_EPS) + ADAM_WD * w)
    return delta, m, v


def ada_backward_update(c_all, gmod_cols, w, m, v):
    def body(c_ref, g_ref, w_ref, m_ref, v_ref, go_ref, d_ref, mo_ref, vo_ref):
        c = c_ref[...]
        cond = (c * jax.nn.sigmoid(c)).astype(BF16)
        g = _dot(cond, g_ref[...].astype(BF16), TN)
        go_ref[...] = g
        d_ref[...], mo_ref[...], vo_ref[...] = _adamw(w_ref[...], g, m_ref[...], v_ref[...])

    nb, d = c_all.shape
    cols = w.shape[1]
    tn = _tile(cols, 128)
    col = pl.BlockSpec((d, tn), lambda j: (0, j))
    return pl.pallas_call(
        body, name="ada_backward_update", grid=(cols // tn,),
        out_shape=[jax.ShapeDtypeStruct(w.shape, F32)] * 4,
        in_specs=[_full((nb, d)), pl.BlockSpec((nb, tn), lambda j: (0, j)), col, col, col],
        out_specs=[col] * 4,
        compiler_params=_params(("arbitrary",)),
    )(c_all, gmod_cols, w, m, v)


def _row_spec(tm, width, tiles_per_seq):
    return pl.BlockSpec((tm, width), lambda b, i: (b * tiles_per_seq + i, 0))


def _mod_spec(d):
    return pl.BlockSpec((1, N_MOD, d), lambda b, i: (b, 0, 0))


def ffn_forward(x, mod, g, w1t, w3t, w2, sub, nb, name):
    t, d = x.shape
    f = w2.shape[0]
    s = t // nb
    tm = _tile(s, 512)
    fc = _tile(f, 256)

    def body(x_ref, mod_ref, g_ref, w1_ref, w3_ref, w2_ref, xo_ref, a_ref, b_ref, f_ref):
        xv = x_ref[...]
        sh, sc, gt = _mod_rows(mod_ref, sub)
        h, _, _ = _norm_mod(xv, g_ref[...], sh, sc)
        hb = h.astype(BF16)
        acc = jnp.zeros((tm, d), F32)
        for k in range(f // fc):
            rows = slice(k * fc, (k + 1) * fc)
            a = _dot(hb, w1_ref[rows, :], NT)
            b = _dot(hb, w3_ref[rows, :], NT)
            a_ref[:, rows] = a.astype(BF16)
            b_ref[:, rows] = b.astype(BF16)
            sw = (a * jax.nn.sigmoid(a)) * b
            acc = acc + _dot(sw.astype(BF16), w2_ref[rows, :])
        f_ref[...] = acc.astype(BF16)
        xo_ref[...] = xv + (0.5 * gt) * acc

    wspec = _const((f, d))
    return pl.pallas_call(
        body, name=name, grid=(nb, s // tm),
        out_shape=[jax.ShapeDtypeStruct((t, d), F32), jax.ShapeDtypeStruct((t, f), BF16),
                   jax.ShapeDtypeStruct((t, f), BF16), jax.ShapeDtypeStruct((t, d), BF16)],
        in_specs=[_row_spec(tm, d, s // tm), _mod_spec(d), _const((1, d)), wspec, wspec, wspec],
        out_specs=[_row_spec(tm, d, s // tm), _row_spec(tm, f, s // tm), _row_spec(tm, f, s // tm),
                   _row_spec(tm, d, s // tm)],
        compiler_params=_params(("arbitrary", "arbitrary")),
    )(x, mod, g, w1t, w3t, w2)


def ffn_backward(dxo, x, a, b, fo, mod, g, w1t, w3t, w2, sub, nb, name):
    t, d = x.shape
    f = w2.shape[0]
    s = t // nb
    tm = _tile(s, 256)
    fc = _tile(f, 256)

    def body(dxo_ref, x_ref, a_ref, b_ref, f_ref, mod_ref, g_ref, w1_ref, w3_ref, w2_ref,
             dx_ref, da_ref, db_ref, h_ref, df_ref, dmod_ref, dg_ref):
        bi, i = pl.program_id(0), pl.program_id(1)
        dxo = dxo_ref[...]
        sh, sc, gt = _mod_rows(mod_ref, sub)
        gv = g_ref[...]
        h, n, r = _norm_mod(x_ref[...], gv, sh, sc)
        hb = h.astype(BF16)
        h_ref[...] = hb
        dfb = ((0.5 * gt) * dxo).astype(BF16)
        df_ref[...] = dfb
        dgt = 0.5 * jnp.sum(dxo * f_ref[...].astype(F32), axis=0, keepdims=True)
        dh = jnp.zeros((tm, d), F32)
        for k in range(f // fc):
            rows = slice(k * fc, (k + 1) * fc)
            ds = _dot(dfb, w2_ref[rows, :], NT)
            av = a_ref[:, rows].astype(F32)
            bv = b_ref[:, rows].astype(F32)
            sig = jax.nn.sigmoid(av)
            dab = (ds * bv * (sig * (1.0 + av * (1.0 - sig)))).astype(BF16)
            dbb = (ds * (av * sig)).astype(BF16)
            da_ref[:, rows] = dab
            db_ref[:, rows] = dbb
            dh = dh + _dot(dab, w1_ref[rows, :]) + _dot(dbb, w3_ref[rows, :])
        dxn, dsh, dsc, dg = _norm_mod_bwd(dh, n, r, gv, sc)
        dx_ref[...] = dxo + dxn
        _accumulate(dmod_ref, jnp.concatenate([dsh, dsc, dgt], axis=0)[None], i == 0)
        _accumulate(dg_ref, dg, jnp.logical_and(bi == 0, i == 0))

    wspec = _const((f, d))
    rd, rf = _row_spec(tm, d, s // tm), _row_spec(tm, f, s // tm)
    return pl.pallas_call(
        body, name=name, grid=(nb, s // tm),
        out_shape=[jax.ShapeDtypeStruct((t, d), F32), jax.ShapeDtypeStruct((t, f), BF16),
                   jax.ShapeDtypeStruct((t, f), BF16), jax.ShapeDtypeStruct((t, d), BF16),
                   jax.ShapeDtypeStruct((t, d), BF16), jax.ShapeDtypeStruct((nb, 3, d), F32),
                   jax.ShapeDtypeStruct((1, d), F32)],
        in_specs=[rd, rd, rf, rf, rd, _mod_spec(d), _const((1, d)), wspec, wspec, wspec],
        out_specs=[rd, rf, rf, rd, rd, pl.BlockSpec((1, 3, d), lambda b, i: (b, 0, 0)), _full((1, d))],
        compiler_params=_params(("arbitrary", "arbitrary")),
    )(dxo, x, a, b, fo, mod, g, w1t, w3t, w2)


def tn_matmul(lhs, rhs, name, lhs2=None):
    t, m = lhs.shape
    n = rhs.shape[1]
    tk = _tile(t, 512)
    tmm = m if m <= 1536 else m // 2
    swiglu = lhs2 is not None

    def body(*refs):
        if swiglu:
            a_ref, a2_ref, b_ref, o_ref, acc_ref = refs
            av = a_ref[...].astype(F32)
            lv = ((av * jax.nn.sigmoid(av)) * a2_ref[...].astype(F32)).astype(BF16)
        else:
            a_ref, b_ref, o_ref, acc_ref = refs
            lv = a_ref[...]
        k = pl.program_id(1)
        _accumulate(acc_ref, _dot(lv, b_ref[...], TN), k == 0)

        @pl.when(k == pl.num_programs(1) - 1)
        def _():
            o_ref[...] = acc_ref[...].astype(BF16)

    lspec = pl.BlockSpec((tk, tmm), lambda j, k: (k, j))
    ins = [lhs, lhs2, rhs] if swiglu else [lhs, rhs]
    return pl.pallas_call(
        body, name=name, grid=(m // tmm, t // tk),
        out_shape=jax.ShapeDtypeStruct((m, n), BF16),
        in_specs=([lspec, lspec] if swiglu else [lspec]) + [pl.BlockSpec((tk, n), lambda j, k: (k, 0))],
        out_specs=pl.BlockSpec((tmm, n), lambda j, k: (j, 0)),
        scratch_shapes=[pltpu.VMEM((tmm, n), F32)],
        compiler_params=_params(("arbitrary", "arbitrary")),
    )(*ins)


def mixer_proj_forward(x, mod, g, w_in_t, cw, sw, nb):
    t, d = x.shape
    s = t // nb
    tm = _tile(s, 512)
    pieces = [(0, cw, "bf16"), (cw, cw, "bf16"), (2 * cw, cw, "bf16"), (3 * cw, sw, "f32"),
              (3 * cw + sw, d, "sig"), (3 * cw + sw + d, d, "sig")]

    def body(x_ref, mod_ref, g_ref, w_ref, *outs):
        h_ref = outs[-1]
        sh, sc, _ = _mod_rows(mod_ref, 1)
        h, _, _ = _norm_mod(x_ref[...], g_ref[...], sh, sc)
        hb = h.astype(BF16)
        h_ref[...] = hb
        for (off, width, kind), o_ref in zip(pieces, outs[:-1]):
            ck = _tile(width, 512)
            for j in range(width // ck):
                p = _dot(hb, w_ref[off + j * ck:off + (j + 1) * ck, :], NT)
                if kind == "sig":
                    p = jax.nn.sigmoid(p)
                o_ref[:, j * ck:(j + 1) * ck] = p.astype(o_ref.dtype)

    tps = s // tm
    widths = [(cw, BF16), (cw, BF16), (cw, BF16), (sw, F32), (d, BF16), (d, BF16), (d, BF16)]
    return pl.pallas_call(
        body, name="mixer_proj_forward", grid=(nb, tps),
        out_shape=[jax.ShapeDtypeStruct((t, w), dt) for w, dt in widths],
        in_specs=[_row_spec(tm, d, tps), _mod_spec(d), _const((1, d)), _const(w_in_t.shape)],
        out_specs=[_row_spec(tm, w, tps) for w, _ in widths],
        compiler_params=_params(("arbitrary", "arbitrary")),
    )(x, mod, g, w_in_t)


def mixer_proj_backward(dgb, dgc, dv, du, dgla, dglb, dxo, x, mod, g, w_in_t, nb):
    t, d = x.shape
    s = t // nb
    tm = _tile(s, 512)
    parts = [dgb, dgc, dv, du, dgla, dglb]
    offs = [0]
    for p in parts:
        offs.append(offs[-1] + p.shape[1])

    def body(*refs):
        p_refs = refs[:6]
        dxo_ref, x_ref, mod_ref, g_ref, w_ref, dx_ref, dmod_ref, dg_ref = refs[6:]
        bi, i = pl.program_id(0), pl.program_id(1)
        dh = jnp.zeros((tm, d), F32)
        for p_ref, off in zip(p_refs, offs):
            width = p_ref.shape[1]
            ck = _tile(width, 512)
            for j in range(width // ck):
                dh = dh + _dot(p_ref[:, j * ck:(j + 1) * ck], w_ref[off + j * ck:off + (j + 1) * ck, :])
        sh, sc, _ = _mod_rows(mod_ref, 1)
        gv = g_ref[...]
        _, n, r = _norm_mod(x_ref[...], gv, sh, sc)
        dxn, dsh, dsc, dg = _norm_mod_bwd(dh, n, r, gv, sc)
        dx_ref[...] = dxo_ref[...] + dxn
        _accumulate(dmod_ref, jnp.concatenate([dsh, dsc], axis=0)[None], i == 0)
        _accumulate(dg_ref, dg, jnp.logical_and(bi == 0, i == 0))

    tps = s // tm
    rd = _row_spec(tm, d, tps)
    return pl.pallas_call(
        body, name="mixer_proj_backward", grid=(nb, tps),
        out_shape=[jax.ShapeDtypeStruct((t, d), F32), jax.ShapeDtypeStruct((nb, 2, d), F32),
                   jax.ShapeDtypeStruct((1, d), F32)],
        in_specs=[_row_spec(tm, p.shape[1], tps) for p in parts]
        + [rd, rd, _mod_spec(d), _const((1, d)), _const(w_in_t.shape)],
        out_specs=[rd, pl.BlockSpec((1, 2, d), lambda b, i: (b, 0, 0)), _full((1, d))],
        compiler_params=_params(("arbitrary", "arbitrary")),
    )(*parts, dxo, x, mod, g, w_in_t)


GROUPS_PER_BLOCK = 8
ROWS = 8


def _scan_rows(xr, xi, masks, shifts):
    for (mr, mi), sft in zip(masks, shifts):
        sr, si = pltpu.roll(xr, sft, 0), pltpu.roll(xi, sft, 0)
        xr, xi = xr + mr * sr - mi * si, xi + mr * si + mi * sr
    return xr, xi


def _ssm_scan_forward(xr_s, xi_s, row0, n_groups, lanes, ml_ref, ap_ref, carry):
    masks = [(ml_ref[d, 0, :, lanes], ml_ref[d, 1, :, lanes]) for d in range(3)]
    apr, api = ap_ref[0, :, lanes], ap_ref[1, :, lanes]

    def group(k, c):
        cr, ci = c
        rows = pl.ds(pl.multiple_of(row0 + k * ROWS, ROWS), ROWS)
        xr, xi = _scan_rows(xr_s[rows, lanes], xi_s[rows, lanes], masks, (1, 2, 4))
        xr, xi = xr + apr * cr - api * ci, xi + apr * ci + api * cr
        xr_s[rows, lanes] = xr
        xi_s[rows, lanes] = xi
        return jnp.broadcast_to(xr[ROWS - 1:ROWS], xr.shape), jnp.broadcast_to(xi[ROWS - 1:ROWS], xi.shape)

    return lax.fori_loop(0, n_groups, group, carry)


def ssm_forward(u, bd, cd, ml, ap, nb):
    t, w = u.shape
    s = t // nb
    tc = _tile(s, 256)
    nq, ub, lq = bd.shape[1], bd.shape[2], bd.shape[3]
    nl = nq * lq
    nch = s // tc

    def body(u_ref, bd_ref, cd_ref, ml_ref, ap_ref, y_ref, st_ref, xr_s, xi_s, car_s):
        i = pl.program_id(1)

        @pl.when(i == 0)
        def _():
            car_s[...] = jnp.zeros_like(car_s)

        st_ref[0] = car_s[...]
        ub16 = u_ref[...].astype(BF16)
        for q in range(nq):
            lanes = slice(q * lq, (q + 1) * lq)
            uq = ub16[:, q * ub:(q + 1) * ub]
            xr_s[:, lanes] = _dot(uq, bd_ref[0, q])
            xi_s[:, lanes] = _dot(uq, bd_ref[1, q])
        for q in range(nq):
            lanes = slice(q * lq, (q + 1) * lq)
            cr, ci = _ssm_scan_forward(xr_s, xi_s, 0, tc // ROWS, lanes, ml_ref, ap_ref,
                                       (car_s[0, :, lanes], car_s[1, :, lanes]))
            car_s[0, :, lanes] = cr
            car_s[1, :, lanes] = ci
        for q in range(nq):
            lanes = slice(q * lq, (q + 1) * lq)
            y_ref[:, q * ub:(q + 1) * ub] = (_dot(xr_s[:, lanes].astype(BF16), cd_ref[0, q])
                                             + _dot(xi_s[:, lanes].astype(BF16), cd_ref[1, q]))

    return pl.pallas_call(
        body, name="ssm_forward", grid=(nb, nch),
        out_shape=[jax.ShapeDtypeStruct((t, w), F32), jax.ShapeDtypeStruct((nb * nch, 2, ROWS, nl), F32)],
        in_specs=[pl.BlockSpec((tc, w), lambda b, i: (b * nch + i, 0)), _const(bd.shape), _const(cd.shape),
                  _const(ml.shape), _const(ap.shape)],
        out_specs=[pl.BlockSpec((tc, w), lambda b, i: (b * nch + i, 0)),
                   pl.BlockSpec((1, 2, ROWS, nl), lambda b, i: (b * nch + i, 0, 0, 0))],
        scratch_shapes=[pltpu.VMEM((tc, nl), F32), pltpu.VMEM((tc, nl), F32), pltpu.VMEM((2, ROWS, nl), F32)],
        compiler_params=_params(("arbitrary", "arbitrary")),
    )(u, bd, cd, ml, ap)


def ssm_backward(u, dy, st, bd, cd, ml, ap, mlb, apb, dskip, nb):
    t, w = u.shape
    s = t // nb
    tc = _tile(s, 256)
    nq, ub, lq = bd.shape[1], bd.shape[2], bd.shape[3]
    nl = nq * lq
    nch = s // tc
    ng = tc // ROWS

    def body(u_ref, dy_ref, st_ref, bd_ref, cd_ref, ml_ref, ap_ref, mlb_ref, apb_ref, dsk_ref,
             du_ref, dab_ref, dbd_ref, dcd_ref, xr_s, xi_s, lr_s, li_s, car_s):
        bi, i = pl.program_id(0), pl.program_id(1)
        first = jnp.logical_and(bi == 0, i == 0)

        @pl.when(i == 0)
        def _():
            car_s[...] = jnp.zeros_like(car_s)

        @pl.when(first)
        def _():
            dab_ref[...] = jnp.zeros_like(dab_ref)
            dbd_ref[...] = jnp.zeros_like(dbd_ref)
            dcd_ref[...] = jnp.zeros_like(dcd_ref)

        uv = u_ref[...]
        dyv = dy_ref[...]
        ub16 = uv.astype(BF16)
        dyb16 = dyv.astype(BF16)
        xr_s[0:ROWS, :] = st_ref[0, 0]
        xi_s[0:ROWS, :] = st_ref[0, 1]
        for q in range(nq):
            lanes = slice(q * lq, (q + 1) * lq)
            uq = ub16[:, q * ub:(q + 1) * ub]
            dq = dyb16[:, q * ub:(q + 1) * ub]
            xr_s[ROWS:, lanes] = _dot(uq, bd_ref[0, q])
            xi_s[ROWS:, lanes] = _dot(uq, bd_ref[1, q])
            lr_s[:, lanes] = _dot(dq, cd_ref[0, q], NT)
            li_s[:, lanes] = _dot(dq, cd_ref[1, q], NT)
        for q in range(nq):
            lanes = slice(q * lq, (q + 1) * lq)
            _ssm_scan_forward(xr_s, xi_s, ROWS, ng, lanes, ml_ref, ap_ref,
                              (st_ref[0, 0, :, lanes], st_ref[0, 1, :, lanes]))
            masks = [(mlb_ref[d, 0, :, lanes], mlb_ref[d, 1, :, lanes]) for d in range(3)]
            apr, api = apb_ref[0, :, lanes], apb_ref[1, :, lanes]
            row_is_0 = lax.broadcasted_iota(jnp.int32, (ROWS, lq), 0) == 0

            def group(kk, c):
                cr, ci, accr, acci = c
                k = ng - 1 - kk
                rows = pl.ds(pl.multiple_of(k * ROWS, ROWS), ROWS)
                lr, li = _scan_rows(lr_s[rows, lanes], li_s[rows, lanes], masks, (7, 6, 4))
                lr, li = lr + apr * cr - api * ci, li + apr * ci + api * cr
                lr_s[rows, lanes] = lr
                li_s[rows, lanes] = li
                prev = pl.ds(pl.multiple_of(k * ROWS, ROWS), ROWS)
                cur = pl.ds(pl.multiple_of(k * ROWS + ROWS, ROWS), ROWS)
                pr = jnp.broadcast_to(xr_s[prev, lanes][ROWS - 1:ROWS], (ROWS, lq))
                pi = jnp.broadcast_to(xi_s[prev, lanes][ROWS - 1:ROWS], (ROWS, lq))
                xpr = jnp.where(row_is_0, pr, pltpu.roll(xr_s[cur, lanes], 1, 0))
                xpi = jnp.where(row_is_0, pi, pltpu.roll(xi_s[cur, lanes], 1, 0))
                accr = accr + lr * xpr + li * xpi
                acci = acci + li * xpr - lr * xpi
                return (jnp.broadcast_to(lr[0:1], lr.shape), jnp.broadcast_to(li[0:1], li.shape), accr, acci)

            zero = jnp.zeros((ROWS, lq), F32)
            cr, ci, accr, acci = lax.fori_loop(0, ng, group, (car_s[0, :, lanes], car_s[1, :, lanes], zero, zero))
            car_s[0, :, lanes] = cr
            car_s[1, :, lanes] = ci
            dab_ref[0, :, lanes] += accr
            dab_ref[1, :, lanes] += acci
        dsk = dsk_ref[...]
        for q in range(nq):
            lanes = slice(q * lq, (q + 1) * lq)
            cols = slice(q * ub, (q + 1) * ub)
            lrb, lib = lr_s[:, lanes].astype(BF16), li_s[:, lanes].astype(BF16)
            uq, dq = ub16[:, cols], dyb16[:, cols]
            du = _dot(lrb, bd_ref[0, q], NT) + _dot(lib, bd_ref[1, q], NT) + dsk[:, cols] * dyv[:, cols]
            du_ref[:, cols] = du.astype(BF16)
            dbd_ref[0, q] += _dot(uq, lrb, TN)
            dbd_ref[1, q] += _dot(uq, lib, TN)
            dcd_ref[0, q] += _dot(xr_s[ROWS:, lanes].astype(BF16), dq, TN)
            dcd_ref[1, q] += _dot(xi_s[ROWS:, lanes].astype(BF16), dq, TN)

    rev = lambda b, i: (b * nch + nch - 1 - i, 0)
    return pl.pallas_call(
        body, name="ssm_backward", grid=(nb, nch),
        out_shape=[jax.ShapeDtypeStruct((t, w), BF16), jax.ShapeDtypeStruct((2, ROWS, nl), F32),
                   jax.ShapeDtypeStruct(bd.shape, F32), jax.ShapeDtypeStruct(cd.shape, F32)],
        in_specs=[pl.BlockSpec((tc, w), rev), pl.BlockSpec((tc, w), rev),
                  pl.BlockSpec((1, 2, ROWS, nl), lambda b, i: (b * nch + nch - 1 - i, 0, 0, 0)),
                  _const(bd.shape), _const(cd.shape), _const(ml.shape), _const(ap.shape), _const(mlb.shape),
                  _const(apb.shape), _const((1, w))],
        out_specs=[pl.BlockSpec((tc, w), rev), _full((2, ROWS, nl)), _full(bd.shape), _full(cd.shape)],
        scratch_shapes=[pltpu.VMEM((tc + ROWS, nl), F32), pltpu.VMEM((tc + ROWS, nl), F32),
                        pltpu.VMEM((tc, nl), F32), pltpu.VMEM((tc, nl), F32), pltpu.VMEM((2, ROWS, nl), F32)],
        compiler_params=_params(("arbitrary", "arbitrary")),
    )(u, dy, st, bd, cd, ml, ap, mlb, apb, dskip)


def ssm_discretise(a_re, a_im, b_re, b_im, log_dt):
    dt = jnp.exp(log_dt)[:, None]
    er = jnp.exp(a_re * dt)
    abr, abi = er * jnp.cos(a_im * dt), er * jnp.sin(a_im * dt)
    den = a_re * a_re + a_im * a_im
    nr, ni = abr - 1.0, abi
    fr = ((nr * a_re + ni * a_im) / den)[..., None]
    fi = ((ni * a_re - nr * a_im) / den)[..., None]
    return abr, abi, fr * b_re - fi * b_im, fr * b_im + fi * b_re


def _complex_powers(abr, abi, n):
    pr, pi = [abr.reshape(-1)], [abi.reshape(-1)]
    for _ in range(n - 1):
        pr, pi = pr + [pr[-1] * pr[0] - pi[-1] * pi[0]], pi + [pr[-1] * pi[0] + pi[-1] * pr[0]]
    return jnp.stack(pr), jnp.stack(pi)


def ssm_tables(abr, abi, bbr, bbi, c_re, c_im):
    g, p, h = bbr.shape
    nq = g // GROUPS_PER_BLOCK
    pw_r, pw_i = _complex_powers(abr, abi, ROWS)
    row = jnp.arange(ROWS)[:, None]
    ml, mlb = [], []
    for d in (1, 2, 4):
        fr, fi = pw_r[d - 1][None], pw_i[d - 1][None]
        ml.append(jnp.stack([jnp.where(row >= d, fr, 0.0), jnp.where(row >= d, fi, 0.0)]))
        mlb.append(jnp.stack([jnp.where(row + d < ROWS, fr, 0.0), jnp.where(row + d < ROWS, -fi, 0.0)]))
    ap = jnp.stack([pw_r, pw_i])
    apb = jnp.stack([pw_r[::-1], -pw_i[::-1]])
    eye = jnp.eye(GROUPS_PER_BLOCK, dtype=F32)

    def block_diag_in(bb):
        bq = bb.reshape(nq, GROUPS_PER_BLOCK, p, h)
        return jnp.einsum("qaph,ab->qahbp", bq, eye).reshape(nq, GROUPS_PER_BLOCK * h, GROUPS_PER_BLOCK * p)

    def block_diag_out(cc):
        cq = cc.reshape(nq, GROUPS_PER_BLOCK, h, p)
        return jnp.einsum("qahp,ab->qapbh", cq, eye).reshape(nq, GROUPS_PER_BLOCK * p, GROUPS_PER_BLOCK * h)

    bd = jnp.stack([block_diag_in(bbr), block_diag_in(bbi)]).astype(BF16)
    cd = jnp.stack([block_diag_out(c_re), block_diag_out(-c_im)]).astype(BF16)
    return bd, cd, jnp.stack(ml), ap, jnp.stack(mlb), apb


def ssm_table_grads(dab, dbd, dcd, g, p, h):
    nq = g // GROUPS_PER_BLOCK
    dabr, dabi = dab[0].sum(0).reshape(g, p), dab[1].sum(0).reshape(g, p)
    b5 = dbd.reshape(2, nq, GROUPS_PER_BLOCK, h, GROUPS_PER_BLOCK, p)
    dbb = jnp.einsum("rqahap->rqaph", b5).reshape(2, g, p, h)
    c5 = dcd.reshape(2, nq, GROUPS_PER_BLOCK, p, GROUPS_PER_BLOCK, h)
    dcc = jnp.einsum("rqapah->rqahp", c5).reshape(2, g, h, p)
    return dabr, dabi, dbb[0], dbb[1], dcc[0], -dcc[1]


HALO = 16


def _conv_inputs(gc_ref, v_ref, gch_ref, vh_ref, cv_s, i, tm):
    cv = gc_ref[...].astype(F32) * v_ref[...].astype(F32)
    halo = gch_ref[...].astype(F32) * vh_ref[...].astype(F32)
    cv_s[0:HALO, :] = jnp.where(i == 0, 0.0, halo)
    cv_s[HALO:, :] = cv
    return cv, cv_s[HALO - 1:HALO - 1 + tm, :], cv_s[HALO - 2:HALO - 2 + tm, :]


def _halo_spec(tm, width, tiles_per_seq):
    per = tm // HALO
    return pl.BlockSpec((HALO, width), lambda b, i: (jnp.maximum((b * tiles_per_seq + i) * per - 1, 0), 0))


def mixer_merge_forward(x, gb, gc, v, sga, sgb, yssm, u, mod, conv_w, dskip, wco, wglu, wso_t, wout, nb):
    t, d = x.shape
    cw, sw = gb.shape[1], u.shape[1]
    s = t // nb
    tm = _tile(s, 256)
    tps = s // tm

    def body(x_ref, gb_ref, gc_ref, v_ref, gch_ref, vh_ref, sga_ref, sgb_ref, ys_ref, u_ref, mod_ref, cw_ref,
             dsk_ref, wco_ref, wglu_ref, wso_ref, wout_ref, xo_ref, ya_ref, yb_ref, mix_ref, cv_s):
        i = pl.program_id(1)
        cv, cv1, cv2 = _conv_inputs(gc_ref, v_ref, gch_ref, vh_ref, cv_s, i, tm)
        w = cw_ref[...]
        conv = w[0:1] * cv2 + w[1:2] * cv1 + w[2:3] * cv
        ya = _dot((gb_ref[...].astype(F32) * conv).astype(BF16), wco_ref[...])
        s0 = ys_ref[...] + dsk_ref[...] * u_ref[...]
        s1, _ = _gelu(s0)
        z = _dot(s1.astype(BF16), wglu_ref[...])
        s2 = s1 * jax.nn.sigmoid(z)
        yb = _dot(s2.astype(BF16), wso_ref[...], NT)
        merged = sga_ref[...].astype(F32) * ya + sgb_ref[...].astype(F32) * yb
        mix = _dot(merged.astype(BF16), wout_ref[...])
        _, _, gt = _mod_rows(mod_ref, 1)
        xo_ref[...] = x_ref[...] + gt * mix
        ya_ref[...] = ya.astype(BF16)
        yb_ref[...] = yb.astype(BF16)
        mix_ref[...] = mix.astype(BF16)

    rd, rc, rw = _row_spec(tm, d, tps), _row_spec(tm, cw, tps), _row_spec(tm, sw, tps)
    hc = _halo_spec(tm, cw, tps)
    return pl.pallas_call(
        body, name="mixer_merge_forward", grid=(nb, tps),
        out_shape=[jax.ShapeDtypeStruct((t, d), F32)] + [jax.ShapeDtypeStruct((t, d), BF16)] * 3,
        in_specs=[rd, rc, rc, rc, hc, hc, rd, rd, rw, rw, _mod_spec(d), _const(conv_w.shape), _const((1, sw)),
                  _const(wco.shape), _const(wglu.shape), _const(wso_t.shape), _const(wout.shape)],
        out_specs=[rd, rd, rd, rd],
        scratch_shapes=[pltpu.VMEM((tm + HALO, cw), F32)],
        compiler_params=_params(("arbitrary", "arbitrary")),
    )(x, gb, gc, v, gc, v, sga, sgb, yssm, u, mod, conv_w, dskip, wco, wglu, wso_t, wout)


def mixer_merge_backward(dxo, mix, ya, yb, gb, gc, v, sga, sgb, yssm, u, mod, conv_w, dskip,
                         wco, wglu, wso_t, wout, nb):
    t, d = dxo.shape
    cw, sw = gb.shape[1], u.shape[1]
    s = t // nb
    tm = _tile(s, 256)
    tps = s // tm

    def body(dxo_ref, mix_ref, ya_ref, yb_ref, gb_ref, gc_ref, v_ref, gch_ref, vh_ref, sga_ref, sgb_ref, ys_ref,
             u_ref, mod_ref, cw_ref, dsk_ref, wco_ref, wglu_ref, wso_ref, wout_ref,
             dgla_ref, dglb_ref, dgb_ref, dconv_ref, ds0_ref, dgt_ref, ddsk_ref, dwout_ref, dwco_ref, dwso_ref,
             dwglu_ref, cv_s):
        bi, i = pl.program_id(0), pl.program_id(1)
        first = jnp.logical_and(bi == 0, i == 0)
        dxo = dxo_ref[...]
        _, _, gt = _mod_rows(mod_ref, 1)
        dmix = (gt * dxo).astype(BF16)
        _accumulate(dgt_ref, jnp.sum(dxo * mix_ref[...].astype(F32), axis=0, keepdims=True)[None], i == 0)
        ya, yb = ya_ref[...].astype(F32), yb_ref[...].astype(F32)
        sga, sgb = sga_ref[...].astype(F32), sgb_ref[...].astype(F32)
        merged = (sga * ya + sgb * yb).astype(BF16)
        _accumulate(dwout_ref, _dot(merged, dmix, TN), first)
        dmerged = _dot(dmix, wout_ref[...], NT)
        dgla_ref[...] = (dmerged * ya * sga * (1.0 - sga)).astype(BF16)
        dglb_ref[...] = (dmerged * yb * sgb * (1.0 - sgb)).astype(BF16)
        dya = (dmerged * sga).astype(BF16)
        dyb = (dmerged * sgb).astype(BF16)
        cv, cv1, cv2 = _conv_inputs(gc_ref, v_ref, gch_ref, vh_ref, cv_s, i, tm)
        w = cw_ref[...]
        conv = w[0:1] * cv2 + w[1:2] * cv1 + w[2:3] * cv
        gbv = gb_ref[...].astype(F32)
        _accumulate(dwco_ref, _dot((gbv * conv).astype(BF16), dya, TN), first)
        dya_in = _dot(dya, wco_ref[...], NT)
        dgb_ref[...] = (dya_in * conv).astype(BF16)
        dconv_ref[...] = dya_in * gbv
        uv = u_ref[...]
        s0 = ys_ref[...] + dsk_ref[...] * uv
        s1, th = _gelu(s0)
        s1b = s1.astype(BF16)
        sz = jax.nn.sigmoid(_dot(s1b, wglu_ref[...]))
        s2b = (s1 * sz).astype(BF16)
        _accumulate(dwso_ref, _dot(dyb, s2b, TN), first)
        ds2 = _dot(dyb, wso_ref[...])
        dz = (ds2 * s1 * sz * (1.0 - sz)).astype(BF16)
        _accumulate(dwglu_ref, _dot(s1b, dz, TN), first)
        ds1 = ds2 * sz + _dot(dz, wglu_ref[...], NT)
        ds0 = ds1 * _gelu_grad(s0, th)
        ds0_ref[...] = ds0
        _accumulate(ddsk_ref, jnp.sum(ds0 * uv, axis=0, keepdims=True), first)

    rd, rc, rw = _row_spec(tm, d, tps), _row_spec(tm, cw, tps), _row_spec(tm, sw, tps)
    hc = _halo_spec(tm, cw, tps)
    return pl.pallas_call(
        body, name="mixer_merge_backward", grid=(nb, tps),
        out_shape=[jax.ShapeDtypeStruct((t, d), BF16), jax.ShapeDtypeStruct((t, d), BF16),
                   jax.ShapeDtypeStruct((t, cw), BF16), jax.ShapeDtypeStruct((t, cw), F32),
                   jax.ShapeDtypeStruct((t, sw), F32), jax.ShapeDtypeStruct((nb, 1, d), F32),
                   jax.ShapeDtypeStruct((1, sw), F32), jax.ShapeDtypeStruct(wout.shape, F32),
                   jax.ShapeDtypeStruct(wco.shape, F32), jax.ShapeDtypeStruct(wso_t.shape, F32),
                   jax.ShapeDtypeStruct(wglu.shape, F32)],
        in_specs=[rd, rd, rd, rd, rc, rc, rc, hc, hc, rd, rd, rw, rw, _mod_spec(d), _const(conv_w.shape),
                  _const((1, sw)), _const(wco.shape), _const(wglu.shape), _const(wso_t.shape), _const(wout.shape)],
        out_specs=[rd, rd, rc, rc, rw, pl.BlockSpec((1, 1, d), lambda b, i: (b, 0, 0)), _full((1, sw)),
                   _full(wout.shape), _full(wco.shape), _full(wso_t.shape), _full(wglu.shape)],
        scratch_shapes=[pltpu.VMEM((tm + HALO, cw), F32)],
        compiler_params=_params(("arbitrary", "arbitrary")),
    )(dxo, mix, ya, yb, gb, gc, v, gc, v, sga, sgb, yssm, u, mod, conv_w, dskip, wco, wglu, wso_t, wout)


def conv_backward(dconv, gc, v, conv_w, nb):
    t, cw = dconv.shape
    s = t // nb
    tm = _tile(s, 512)
    tps = s // tm
    per = tm // ROWS

    def body(dc_ref, dcn_ref, gc_ref, v_ref, gch_ref, vh_ref, cw_ref, dgc_ref, dv_ref, dw_ref, cv_s, dc_s):
        bi, i = pl.program_id(0), pl.program_id(1)
        cv, cv1, cv2 = _conv_inputs(gc_ref, v_ref, gch_ref, vh_ref, cv_s, i, tm)
        dc = dc_ref[...]
        dc_s[0:tm, :] = dc
        dc_s[tm:, :] = jnp.where(i == tps - 1, 0.0, dcn_ref[...])
        w = cw_ref[...]
        dcv = w[2:3] * dc + w[1:2] * dc_s[1:1 + tm, :] + w[0:1] * dc_s[2:2 + tm, :]
        dgc_ref[...] = (dcv * v_ref[...].astype(F32)).astype(BF16)
        dv_ref[...] = (dcv * gc_ref[...].astype(F32)).astype(BF16)
        dw = jnp.concatenate([jnp.sum(dc * cv2, axis=0, keepdims=True), jnp.sum(dc * cv1, axis=0, keepdims=True),
                              jnp.sum(dc * cv, axis=0, keepdims=True)], axis=0)
        _accumulate(dw_ref, dw, jnp.logical_and(bi == 0, i == 0))

    rc = _row_spec(tm, cw, tps)
    nxt = pl.BlockSpec((ROWS, cw), lambda b, i: (jnp.minimum((b * tps + i + 1) * per, nb * tps * per - 1), 0))
    hc = _halo_spec(tm, cw, tps)
    return pl.pallas_call(
        body, name="conv_backward", grid=(nb, tps),
        out_shape=[jax.ShapeDtypeStruct((t, cw), BF16), jax.ShapeDtypeStruct((t, cw), BF16),
                   jax.ShapeDtypeStruct(conv_w.shape, F32)],
        in_specs=[rc, nxt, rc, rc, hc, hc, _full(conv_w.shape)],
        out_specs=[rc, rc, _full(conv_w.shape)],
        scratch_shapes=[pltpu.VMEM((tm + HALO, cw), F32), pltpu.VMEM((tm + ROWS, cw), F32)],
        compiler_params=_params(("arbitrary", "arbitrary")),
    )(dconv, dconv, gc, v, gc, v, conv_w)


def loss_forward_backward(x, target, g):
    t, d = x.shape
    tm = _tile(t, 512)

    def body(x_ref, t_ref, g_ref, l_ref, dx_ref, dg_ref):
        i = pl.program_id(0)
        xv = x_ref[...]
        gv = g_ref[...]
        r = lax.rsqrt(jnp.mean(xv * xv, axis=-1, keepdims=True) + EPS)
        n = xv * r
        err = n * gv - t_ref[...]
        l_ref[...] = jnp.full(l_ref.shape, 0.5 * jnp.sum(jnp.mean(err * err, axis=-1)), F32)
        dy = err * (1.0 / d)
        dn = dy * gv
        dx_ref[...] = r * (dn - n * jnp.mean(n * dn, axis=-1, keepdims=True))
        _accumulate(dg_ref, jnp.sum(dy * n, axis=0, keepdims=True), i == 0)

    row = pl.BlockSpec((tm, d), lambda i: (i, 0))
    return pl.pallas_call(
        body, name="loss_forward_backward", grid=(t // tm,),
        out_shape=[jax.ShapeDtypeStruct((t // tm, 1, 128), F32), jax.ShapeDtypeStruct((t, d), F32),
                   jax.ShapeDtypeStruct((1, d), F32)],
        in_specs=[row, row, _full((1, d))],
        out_specs=[pl.BlockSpec((1, 1, 128), lambda i: (i, 0, 0)), row, _full((1, d))],
        compiler_params=_params(("arbitrary",)),
    )(x, target, g)


def sum_slots(slots, name):
    _, r, c = slots.shape
    tr = _tile(r, 352) if r % 352 == 0 else _tile(r, 256)

    def body(s_ref, o_ref):
        acc = s_ref[0].astype(F32)
        for j in range(1, N_DEV):
            acc = acc + s_ref[j].astype(F32)
        o_ref[...] = acc

    return pl.pallas_call(
        body, name=name, grid=(r // tr,),
        out_shape=jax.ShapeDtypeStruct((r, c), F32),
        in_specs=[pl.BlockSpec((N_DEV, tr, c), lambda i: (0, i, 0))],
        out_specs=pl.BlockSpec((tr, c), lambda i: (i, 0)),
        compiler_params=_params(("arbitrary",)),
    )(slots)


def adamw_update(w, g, m, v, name):
    r, c = w.shape
    tr = _tile(r, 256) if r % 8 == 0 else r

    def body(w_ref, g_ref, m_ref, v_ref, d_ref, mo_ref, vo_ref):
        d_ref[...], mo_ref[...], vo_ref[...] = _adamw(w_ref[...], g_ref[...], m_ref[...], v_ref[...])

    spec = pl.BlockSpec((tr, c), lambda i: (i, 0))
    return pl.pallas_call(
        body, name=name, grid=(r // tr,),
        out_shape=[jax.ShapeDtypeStruct((r, c), F32)] * 3,
        in_specs=[spec] * 4, out_specs=[spec] * 3,
        compiler_params=_params(("arbitrary",)),
    )(w, g, m, v)


def adamw_update_small(ws, gs, ms, vs):
    n = len(ws)

    def body(*refs):
        w_r, g_r, m_r, v_r = refs[:n], refs[n:2 * n], refs[2 * n:3 * n], refs[3 * n:4 * n]
        d_r, mo_r, vo_r = refs[4 * n:5 * n], refs[5 * n:6 * n], refs[6 * n:7 * n]
        for k in range(n):
            d_r[k][...], mo_r[k][...], vo_r[k][...] = _adamw(w_r[k][...], g_r[k][...], m_r[k][...], v_r[k][...])

    shapes = [jax.ShapeDtypeStruct(w.shape, F32) for w in ws]
    out = pl.pallas_call(body, name="adamw_update_small", out_shape=shapes * 3,
                         compiler_params=_params())(*ws, *gs, *ms, *vs)
    return out[:n], out[n:2 * n], out[2 * n:]


def _slots(grad_t):
    return grad_t.reshape(N_DEV, grad_t.shape[0] // N_DEV, grad_t.shape[1])


def kernel(x, c, w_ada, b_ada, g_ffn1, w1_a, w3_a, w2_a, g_mix, w_in, conv_w, w_conv_out, a_re, a_im, b_re, b_im, c_re, c_im, log_dt, d_skip, w_glu, w_ssm_out, w_out, g_ffn2, w1_b, w3_b, w2_b, g_final, loss_target, m_w_ada, m_b_ada, m_g_ffn1, m_w1_a, m_w3_a, m_w2_a, m_g_mix, m_w_in, m_conv_w, m_w_conv_out, m_a_re, m_a_im, m_b_re, m_b_im, m_c_re, m_c_im, m_log_dt, m_d_skip, m_w_glu, m_w_ssm_out, m_w_out, m_g_ffn2, m_w1_b, m_w3_b, m_w2_b, m_g_final, v_w_ada, v_b_ada, v_g_ffn1, v_w1_a, v_w3_a, v_w2_a, v_g_mix, v_w_in, v_conv_w, v_w_conv_out, v_a_re, v_a_im, v_b_re, v_b_im, v_c_re, v_c_im, v_log_dt, v_d_skip, v_w_glu, v_w_ssm_out, v_w_out, v_g_ffn2, v_w1_b, v_w3_b, v_w2_b, v_g_final):
    nb, s, d = x.shape
    t = nb * s
    me = 4 * lax.axis_index("x") + 2 * lax.axis_index("y") + lax.axis_index("c")
    g_n, p_n, h_n = b_re.shape[1:]
    cw_n = w_conv_out.shape[1] * N_DEV
    sw_n = w_ssm_out.shape[1]
    glu_fold = d // w_glu.shape[2]

    weights = dict(w_ada=w_ada, b_ada=b_ada, g_ffn1=g_ffn1, w1_a=w1_a, w3_a=w3_a, w2_a=w2_a, g_mix=g_mix, w_in=w_in,
                   conv_w=conv_w, w_conv_out=w_conv_out, a_re=a_re, a_im=a_im, b_re=b_re, b_im=b_im, c_re=c_re,
                   c_im=c_im, log_dt=log_dt, d_skip=d_skip, w_glu=w_glu, w_ssm_out=w_ssm_out, w_out=w_out,
                   g_ffn2=g_ffn2, w1_b=w1_b, w3_b=w3_b, w2_b=w2_b, g_final=g_final)
    mom1 = dict(w_ada=m_w_ada, b_ada=m_b_ada, g_ffn1=m_g_ffn1, w1_a=m_w1_a, w3_a=m_w3_a, w2_a=m_w2_a, g_mix=m_g_mix,
                w_in=m_w_in, conv_w=m_conv_w, w_conv_out=m_w_conv_out, a_re=m_a_re, a_im=m_a_im, b_re=m_b_re,
                b_im=m_b_im, c_re=m_c_re, c_im=m_c_im, log_dt=m_log_dt, d_skip=m_d_skip, w_glu=m_w_glu,
                w_ssm_out=m_w_ssm_out, w_out=m_w_out, g_ffn2=m_g_ffn2, w1_b=m_w1_b, w3_b=m_w3_b, w2_b=m_w2_b,
                g_final=m_g_final)
    mom2 = dict(w_ada=v_w_ada, b_ada=v_b_ada, g_ffn1=v_g_ffn1, w1_a=v_w1_a, w3_a=v_w3_a, w2_a=v_w2_a, g_mix=v_g_mix,
                w_in=v_w_in, conv_w=v_conv_w, w_conv_out=v_w_conv_out, a_re=v_a_re, a_im=v_a_im, b_re=v_b_re,
                b_im=v_b_im, c_re=v_c_re, c_im=v_c_im, log_dt=v_log_dt, d_skip=v_d_skip, w_glu=v_w_glu,
                w_ssm_out=v_w_ssm_out, w_out=v_w_out, g_ffn2=v_g_ffn2, w1_b=v_w1_b, w3_b=v_w3_b, w2_b=v_w2_b,
                g_final=v_g_final)
    names = list(weights)
    transposed = ("w1_a", "w3_a", "w_in", "w_ssm_out", "w1_b", "w3_b")
    natural = ("w2_a", "w_conv_out", "w_glu", "w_out", "w2_b")
    big = ("w1_a", "w3_a", "w2_a", "w_in", "w_conv_out", "w_glu", "w_ssm_out", "w_out", "w1_b", "w3_b", "w2_b")

    pad_rows = lambda a: jnp.pad(a, ((0, -a.shape[0] % ROWS), (0, 0)))
    c_all, conv_all = all_gather_rows([pad_rows(c), pad_rows(conv_w[0])], "gather_cond")
    c_all = c_all.reshape(N_DEV, -1, d)[:, :nb].reshape(N_DEV * nb, d)
    conv_full = conv_all.reshape(N_DEV, ROWS, -1)[:, :CONV_K].transpose(1, 0, 2).reshape(CONV_K, cw_n)
    ada_cols = w_ada.shape[2]
    b_cols = lax.dynamic_slice(b_ada, (0, me * ada_cols), (1, ada_cols))
    mod_cols = ada_forward(c_all, w_ada[0], b_cols)
    (mod_all,) = all_gather_rows([mod_cols], "gather_mod")
    mod_mine = lax.dynamic_slice(mod_all.reshape(N_DEV, N_DEV * nb, ada_cols), (0, me * nb, 0), (N_DEV, nb, ada_cols))
    mod = mod_mine.transpose(1, 0, 2).reshape(nb, N_MOD, d)

    def shard_rows(name):
        w = weights[name][0]
        if name in transposed:
            w = w.T
        if name == "w_glu":
            w = w.reshape(w.shape[0] // glu_fold, d)
        return w.astype(BF16)

    gathered = dict(zip(big, all_gather_rows([shard_rows(n) for n in big], "gather_weights")))
    gathered["w_glu"] = gathered["w_glu"].reshape(sw_n, sw_n)
    gw = gathered

    disc_in = (a_re[0], a_im[0], b_re[0], b_im[0], log_dt[0])
    (abr, abi, bbr, bbi), disc_vjp = jax.vjp(ssm_discretise, *disc_in)
    bd, cd, ml, ap, mlb, apb = ssm_tables(abr, abi, bbr, bbi, c_re[0], c_im[0])

    x0 = x.reshape(t, d)
    x1, a1, b1, f1 = ffn_forward(x0, mod, g_ffn1, gw["w1_a"], gw["w3_a"], gw["w2_a"], 0, nb, "ffn_a_forward")
    gb, gc, vv, u, sga, sgb, h2 = mixer_proj_forward(x1, mod, g_mix, gw["w_in"], cw_n, sw_n, nb)
    yssm, st = ssm_forward(u, bd, cd, ml, ap, nb)
    x2, ya, yb, mix = mixer_merge_forward(x1, gb, gc, vv, sga, sgb, yssm, u, mod, conv_full, d_skip,
                                          gw["w_conv_out"], gw["w_glu"], gw["w_ssm_out"], gw["w_out"], nb)
    x3, a3, b3, f3 = ffn_forward(x2, mod, g_ffn2, gw["w1_b"], gw["w3_b"], gw["w2_b"], 2, nb, "ffn_b_forward")
    loss_parts, dx3, dg_final = loss_forward_backward(x3, loss_target.reshape(t, d), g_final.reshape(1, d))
    loss = lax.psum(jnp.sum(loss_parts[:, 0, 0]), ("x", "y", "c"))

    dx2, da3, db3, h3, df3, dmod3, dg_ffn2 = ffn_backward(dx3, x2, a3, b3, f3, mod, g_ffn2, gw["w1_b"], gw["w3_b"],
                                                          gw["w2_b"], 2, nb, "ffn_b_backward")
    part = {}
    part["w1_b"] = tn_matmul(da3, h3, "grad_w1_b")
    part["w3_b"] = tn_matmul(db3, h3, "grad_w3_b")
    part["w2_b"] = tn_matmul(a3, df3, "grad_w2_b", lhs2=b3)
    (dgla, dglb, dgb, dconv, ds0, dgt2, dd_skip, dw_out, dw_co, dw_so_t, dw_glu) = mixer_merge_backward(
        dx2, mix, ya, yb, gb, gc, vv, sga, sgb, yssm, u, mod, conv_full, d_skip,
        gw["w_conv_out"], gw["w_glu"], gw["w_ssm_out"], gw["w_out"], nb)
    du, dab, dbd, dcd = ssm_backward(u, ds0, st, bd, cd, ml, ap, mlb, apb, d_skip, nb)
    dgc, dvv, dconv_w = conv_backward(dconv, gc, vv, conv_full, nb)
    dx1, dmod2, dg_mix = mixer_proj_backward(dgb, dgc, dvv, du, dgla, dglb, dx2, x1, mod, g_mix, gw["w_in"], nb)
    part["w_in"] = jnp.concatenate([tn_matmul(p, h2, "grad_w_in_%d" % k)
                                    for k, p in enumerate((dgb, dgc, dvv, du, dgla, dglb))], axis=0)
    part["w_out"] = dw_out.astype(BF16)
    part["w_conv_out"] = dw_co.astype(BF16)
    part["w_ssm_out"] = dw_so_t.astype(BF16)
    part["w_glu"] = dw_glu.reshape(sw_n // glu_fold, d).astype(BF16)
    dx0, da1, db1, h1, df1, dmod1, dg_ffn1 = ffn_backward(dx1, x0, a1, b1, f1, mod, g_ffn1, gw["w1_a"], gw["w3_a"],
                                                          gw["w2_a"], 0, nb, "ffn_a_backward")
    part["w1_a"] = tn_matmul(da1, h1, "grad_w1_a")
    part["w3_a"] = tn_matmul(db1, h1, "grad_w3_a")
    part["w2_a"] = tn_matmul(a1, df1, "grad_w2_a", lhs2=b1)

    received = dict(zip(big, all_to_all_slots([_slots(part[n]) for n in big], "exchange_weight_grads")))

    dabr, dabi, dbbr, dbbi, dcr, dci = ssm_table_grads(dab, dbd, dcd, g_n, p_n, h_n)
    gmod = jnp.concatenate([dmod1, dmod2, dgt2, dmod3], axis=1)
    small = dict(gmod=gmod, g_ffn1=dg_ffn1, g_mix=dg_mix, g_ffn2=dg_ffn2, g_final=dg_final, d_skip=dd_skip,
                 abr=dabr, abi=dabi, bbr=dbbr, bbi=dbbi, c_re=dcr, c_im=dci, conv_w=dconv_w)
    flat = jnp.concatenate([a.reshape(-1) for a in small.values()])
    n_small = flat.shape[0]
    n_rows = -(-n_small // (128 * ROWS)) * ROWS
    flat = jnp.pad(flat, (0, n_rows * 128 - n_small)).reshape(n_rows, 128)
    (small_all,) = all_gather_rows([flat], "gather_small_grads")
    small_all = small_all.reshape(N_DEV, n_rows, 128)
    total = sum_slots(small_all, "sum_small_grads").reshape(-1)
    tot, off = {}, 0
    for key, like in small.items():
        n = math.prod(like.shape)
        tot[key], off = total[off:off + n].reshape(like.shape), off + n
    gmod_all = small_all.reshape(N_DEV, n_rows * 128)[:, :nb * N_MOD * d].reshape(N_DEV * nb, N_MOD * d)
    g_a_re, g_a_im, g_b_re, g_b_im, g_log_dt = disc_vjp((tot["abr"], tot["abi"], tot["bbr"], tot["bbi"]))

    grads = {}
    grads["b_ada"] = sum_rows(tot["gmod"].reshape(nb, N_MOD * d))
    grads["g_ffn1"], grads["g_mix"], grads["g_ffn2"] = tot["g_ffn1"], tot["g_mix"], tot["g_ffn2"]
    grads["g_final"] = tot["g_final"].reshape(d)
    grads["d_skip"] = tot["d_skip"]
    grads["a_re"], grads["a_im"], grads["log_dt"] = g_a_re[None], g_a_im[None], g_log_dt[None]
    grads["b_re"], grads["b_im"] = g_b_re[None], g_b_im[None]
    grads["c_re"], grads["c_im"] = tot["c_re"][None], tot["c_im"][None]
    grads["conv_w"] = lax.dynamic_slice(tot["conv_w"], (0, me * conv_w.shape[2]), (CONV_K, conv_w.shape[2]))[None]

    delta, new_m, new_v = {}, {}, {}
    for name in big:
        gsum = sum_slots(received[name], "sum_" + name)
        if name == "w_glu":
            gsum = gsum.reshape(w_glu.shape[1], w_glu.shape[2])
        if name in transposed:
            gsum = gsum.T
        grads[name] = gsum[None]
        dl, mm, vn = adamw_update(weights[name][0], gsum, mom1[name][0], mom2[name][0], "adamw_" + name)
        delta[name], new_m[name], new_v[name] = dl[None], mm[None], vn[None]

    gmod_cols = lax.dynamic_slice(gmod_all, (0, me * ada_cols), (N_DEV * nb, ada_cols))
    g_wada, d_wada, m_wada, v_wada = ada_backward_update(c_all, gmod_cols, w_ada[0], m_w_ada[0], v_w_ada[0])
    grads["w_ada"], delta["w_ada"], new_m["w_ada"], new_v["w_ada"] = g_wada[None], d_wada[None], m_wada[None], v_wada[None]

    small_names = [n for n in names if n not in big and n != "w_ada"]

    def as2d(a):
        return a.reshape(-1, a.shape[-1])

    sw_, sg_, sm_, sv_ = ([as2d(src[n]) for n in small_names] for src in (weights, grads, mom1, mom2))
    sd, smo, svo = adamw_update_small(sw_, sg_, sm_, sv_)
    for n, dl, mm, vn in zip(small_names, sd, smo, svo):
        shape = weights[n].shape
        grads[n] = grads[n].reshape(shape)
        delta[n], new_m[n], new_v[n] = dl.reshape(shape), mm.reshape(shape), vn.reshape(shape)

    grad_x = dx0.reshape(nb, s, d)
    return (loss, grad_x, *[grads[n] for n in names], *[delta[n] for n in names],
            *[new_m[n] for n in names], *[new_v[n] for n in names])


def sum_rows(a):
    r, c = a.shape

    def body(a_ref, o_ref):
        acc = a_ref[0:1, :]
        for j in range(1, r):
            acc = acc + a_ref[j:j + 1, :]
        o_ref[...] = acc

    return pl.pallas_call(body, name="sum_rows", out_shape=jax.ShapeDtypeStruct((1, c), F32),
                          compiler_params=_params())(a)
```

```python
import functools
import math

import jax
import jax.numpy as jnp
from jax import lax
from jax.experimental import pallas as pl
from jax.experimental.pallas import tpu as pltpu

F32 = jnp.float32
BF16 = jnp.bfloat16
N_DEV = 8
N_MOD = 9
EPS = 1e-6
CONV_K = 3
ADAM_LR = 0.001
ADAM_B1 = 0.9
ADAM_B2 = 0.999
ADAM_EPS = 1e-08
ADAM_WD = 0.01
ADAM_STEP = 10
GELU_C0 = math.sqrt(2.0 / math.pi)
GELU_C1 = 0.044715
V7X_VMEM_LIMIT = 56 * 1024 * 1024
MESH_ID = pl.DeviceIdType.MESH
NT = (((1,), (1,)), ((), ()))
TN = (((0,), (0,)), ((), ()))


def _dot(a, b, dims=None):
    if dims is None:
        return jnp.dot(a, b, preferred_element_type=F32)
    return lax.dot_general(a, b, dims, preferred_element_type=F32)


def _params(sem=None, vmem=V7X_VMEM_LIMIT):
    return pltpu.CompilerParams(dimension_semantics=sem, vmem_limit_bytes=vmem)


def _full(shape):
    return pl.BlockSpec(shape, lambda *_: (0,) * len(shape))


def _const(shape):
    return pl.BlockSpec(shape, lambda *_: (0,) * len(shape), pipeline_mode=pl.Buffered(1))


def _tile(n, want):
    t = min(n, want)
    while n % t:
        t //= 2
    return t


class GatherStage:
    def __init__(self, shards):
        n = len(shards)
        self.inputs = list(shards)
        self.out_shape = [jax.ShapeDtypeStruct((N_DEV * s.shape[0], s.shape[1]), s.dtype) for s in shards]
        self.scratch = [pltpu.SemaphoreType.DMA((7 * n,)), pltpu.SemaphoreType.DMA((7 * n,)),
                        pltpu.SemaphoreType.DMA((n,))]

    def _plan(self, ins, outs, sems):
        send_sems, recv_sems, local_sems = sems
        n = len(ins)
        x, y, c = lax.axis_index("x"), lax.axis_index("y"), lax.axis_index("c")
        me, sibling = (x, y, c), (x, y, 1 - c)
        chips = [(1 - x, y), (x, 1 - y), (1 - x, 1 - y)]

        def rows(k, px, py, pc):
            r = ins[k].shape[0]
            return outs[k].at[pl.ds((4 * px + 2 * py + pc) * r, r), :]

        def copy(k, j, block, to, src=None):
            return pltpu.make_async_remote_copy(
                src_ref=rows(k, *block) if src is None else src, dst_ref=rows(k, *block),
                send_sem=send_sems.at[7 * k + j], recv_sem=recv_sems.at[7 * k + j],
                device_id=to, device_id_type=MESH_ID)

        mine = [pltpu.make_async_copy(ins[k], rows(k, *me), local_sems.at[k]) for k in range(n)]
        first = []
        for k in range(n):
            first.append(copy(k, 0, me, sibling, src=ins[k]))
            first += [copy(k, 1 + j, me, (*chip, c), src=ins[k]) for j, chip in enumerate(chips)]
        return n, c, me, sibling, chips, copy, mine, first

    def start(self, ins, outs, sems):
        *_, mine, first = self._plan(ins, outs, sems)
        for cp in mine + first:
            cp.start()

    def finish(self, ins, outs, sems):
        n, c, me, sibling, chips, copy, mine, first = self._plan(ins, outs, sems)
        passed = []
        for j, chip in enumerate(chips):
            for k in range(n):
                copy(k, 1 + j, (*chip, c), me).wait_recv()
                cp = copy(k, 4 + j, (*chip, c), sibling)
                cp.start()
                passed.append(cp)
        for k in range(n):
            copy(k, 0, sibling, me).wait_recv()
            for j, chip in enumerate(chips):
                copy(k, 4 + j, (*chip, 1 - c), me).wait_recv()
        for cp in first + passed:
            cp.wait_send()
        for cp in mine:
            cp.wait()


class ExchangeStage:
    def __init__(self, bufs):
        n = len(bufs)
        self.inputs = list(bufs)
        self.out_shape = [jax.ShapeDtypeStruct(b.shape, b.dtype) for b in bufs]
        self.scratch = [pltpu.SemaphoreType.DMA((7 * n,)), pltpu.SemaphoreType.DMA((7 * n,)),
                        pltpu.SemaphoreType.DMA((n,))]

    def _plan(self, ins, outs, sems):
        send_sems, recv_sems, local_sems = sems
        n = len(ins)
        x, y, c = lax.axis_index("x"), lax.axis_index("y"), lax.axis_index("c")
        me = 4 * x + 2 * y + c
        mine = [pltpu.make_async_copy(ins[k].at[me], outs[k].at[me], local_sems.at[k]) for k in range(n)]
        copies = []
        for mask in range(1, N_DEV):
            px, py, pc = x ^ (mask >> 2), y ^ ((mask >> 1) & 1), c ^ (mask & 1)
            for k in range(n):
                copies.append(pltpu.make_async_remote_copy(
                    src_ref=ins[k].at[4 * px + 2 * py + pc], dst_ref=outs[k].at[me],
                    send_sem=send_sems.at[7 * k + mask - 1], recv_sem=recv_sems.at[7 * k + mask - 1],
                    device_id=(px, py, pc), device_id_type=MESH_ID))
        return mine, copies

    def start(self, ins, outs, sems):
        mine, copies = self._plan(ins, outs, sems)
        for cp in mine + copies:
            cp.start()

    def finish(self, ins, outs, sems):
        mine, copies = self._plan(ins, outs, sems)
        for cp in copies:
            cp.wait_recv()
        for cp in copies:
            cp.wait_send()
        for cp in mine:
            cp.wait()


ANY_SPEC = pl.BlockSpec(memory_space=pl.ANY)


def run_stage(stage, name):
    ci, co = len(stage.inputs), len(stage.out_shape)

    def body(*refs):
        ins, outs, sems = refs[:ci], refs[ci:ci + co], refs[ci + co:]
        stage.start(ins, outs, sems)
        stage.finish(ins, outs, sems)

    return pl.pallas_call(body, name=name, out_shape=stage.out_shape, in_specs=[ANY_SPEC] * ci,
                          out_specs=[ANY_SPEC] * co, scratch_shapes=stage.scratch)(*stage.inputs)


def _call(body, *, name, grid, in_specs, out_specs, out_shape, args, scratch_shapes=(), carried=None):
    sem = ("arbitrary",) * len(grid)
    if carried is None:
        return pl.pallas_call(body, name=name, grid=grid, in_specs=list(in_specs), out_specs=list(out_specs),
                              out_shape=list(out_shape), scratch_shapes=list(scratch_shapes),
                              compiler_params=_params(sem))(*args), None
    ni, no, ns = len(in_specs), len(out_shape), len(scratch_shapes)
    ci, co = len(carried.inputs), len(carried.out_shape)

    def wrapped(*refs):
        ins, refs = refs[:ni], refs[ni:]
        cins, refs = refs[:ci], refs[ci:]
        outs, refs = refs[:no], refs[no:]
        couts, refs = refs[:co], refs[co:]
        scr, csems = refs[:ns], refs[ns:]
        ids = [pl.program_id(k) for k in range(len(grid))]
        first = functools.reduce(jnp.logical_and, [i == 0 for i in ids])
        last = functools.reduce(jnp.logical_and, [i == g - 1 for i, g in zip(ids, grid)])

        @pl.when(first)
        def _():
            carried.start(cins, couts, csems)

        body(*ins, *outs, *scr)

        @pl.when(last)
        def _():
            carried.finish(cins, couts, csems)

    res = pl.pallas_call(
        wrapped, name=name, grid=grid, in_specs=list(in_specs) + [ANY_SPEC] * ci,
        out_specs=list(out_specs) + [ANY_SPEC] * co, out_shape=list(out_shape) + carried.out_shape,
        scratch_shapes=list(scratch_shapes) + carried.scratch, compiler_params=_params(sem),
    )(*args, *carried.inputs)
    return res[:no], res[no:]


def _norm_mod(x, g, shift, scale):
    r = lax.rsqrt(jnp.mean(x * x, axis=-1, keepdims=True) + EPS)
    n = x * r
    return (n * g) * (1.0 + scale) + shift, n, r


def _norm_mod_bwd(dh, n, r, g, scale):
    dsh = jnp.sum(dh, axis=0, keepdims=True)
    dsc = jnp.sum(dh * (n * g), axis=0, keepdims=True)
    dg = jnp.sum(dh * (1.0 + scale) * n, axis=0, keepdims=True)
    dn = dh * ((1.0 + scale) * g)
    dx = r * (dn - n * jnp.mean(n * dn, axis=-1, keepdims=True))
    return dx, dsh, dsc, dg


def _mod_rows(mod_ref, sub):
    m = mod_ref[0]
    return m[3 * sub:3 * sub + 1], m[3 * sub + 1:3 * sub + 2], m[3 * sub + 2:3 * sub + 3]


def _gelu(x):
    t = jnp.tanh(GELU_C0 * (x + GELU_C1 * x * x * x))
    return 0.5 * x * (1.0 + t), t


def _gelu_grad(x, t):
    return 0.5 * (1.0 + t) + 0.5 * x * (1.0 - t * t) * (GELU_C0 * (1.0 + 3.0 * GELU_C1 * x * x))


def _accumulate(ref, val, first):
    @pl.when(first)
    def _():
        ref[...] = val

    @pl.when(jnp.logical_not(first))
    def _():
        ref[...] += val


def ada_forward(c_all, w_ada, b_ada_cols):
    def body(c_ref, w_ref, b_ref, o_ref):
        c = c_ref[...]
        cond = (c * jax.nn.sigmoid(c)).astype(BF16)
        o_ref[...] = _dot(cond, w_ref[...].astype(BF16)) + b_ref[...]

    nb, d = c_all.shape
    cols = w_ada.shape[1]
    tn = _tile(cols, 384)
    return pl.pallas_call(
        body, name="ada_forward", grid=(cols // tn,),
        out_shape=jax.ShapeDtypeStruct((nb, cols), F32),
        in_specs=[_full((nb, d)), pl.BlockSpec((d, tn), lambda j: (0, j)), pl.BlockSpec((1, tn), lambda j: (0, j))],
        out_specs=pl.BlockSpec((nb, tn), lambda j: (0, j)),
        compiler_params=_params(("arbitrary",)),
    )(c_all, w_ada, b_ada_cols)


def _adamw(w, g, m, v):
    m = ADAM_B1 * m + (1.0 - ADAM_B1) * g
    v = ADAM_B2 * v + (1.0 - ADAM_B2) * (g * g)
    m_hat = m / (1.0 - ADAM_B1 ** ADAM_STEP)
    v_hat = v / (1.0 - ADAM_B2 ** ADAM_STEP)
    delta = -ADAM_LR * (m_hat / (jnp.sqrt(v_hat) + ADAM_EPS) + ADAM_WD * w)
    return delta, m, v


def ada_backward_update(c_all, gmod_cols, w, m, v):
    def body(c_ref, g_ref, w_ref, m_ref, v_ref, go_ref, d_ref, mo_ref, vo_ref):
        c = c_ref[...]
        cond = (c * jax.nn.sigmoid(c)).astype(BF16)
        g = _dot(cond, g_ref[...].astype(BF16), TN)
        go_ref[...] = g
        d_ref[...], mo_ref[...], vo_ref[...] = _adamw(w_ref[...], g, m_ref[...], v_ref[...])

    nb, d = c_all.shape
    cols = w.shape[1]
    tn = _tile(cols, 128)
    col = pl.BlockSpec((d, tn), lambda j: (0, j))
    return pl.pallas_call(
        body, name="ada_backward_update", grid=(cols // tn,),
        out_shape=[jax.ShapeDtypeStruct(w.shape, F32)] * 4,
        in_specs=[_full((nb, d)), pl.BlockSpec((nb, tn), lambda j: (0, j)), col, col, col],
        out_specs=[col] * 4,
        compiler_params=_params(("arbitrary",)),
    )(c_all, gmod_cols, w, m, v)


def _row_spec(tm, width, tiles_per_seq):
    return pl.BlockSpec((tm, width), lambda b, i: (b * tiles_per_seq + i, 0))


def _mod_spec(d):
    return pl.BlockSpec((1, N_MOD, d), lambda b, i: (b, 0, 0))


def _col_spec(rows, tm, tiles_per_seq):
    return pl.BlockSpec((rows, tm), lambda b, i: (0, b * tiles_per_seq + i))


def _ffn_chunk(f):
    return f // 2 if f % 256 == 0 and f > 1536 else f


def ffn_forward(x, mod, g, w1t, w3t, w2t, sub, nb, name, carried=None):
    t, d = x.shape
    f = w1t.shape[0]
    s = t // nb
    tm = _tile(s, 512)
    fc = _ffn_chunk(f)

    def body(x_ref, mod_ref, g_ref, w1_ref, w3_ref, w2_ref, xo_ref, a_ref, b_ref, f_ref):
        xv = x_ref[...]
        sh, sc, gt = _mod_rows(mod_ref, sub)
        h, _, _ = _norm_mod(xv, g_ref[...], sh, sc)
        hb = h.astype(BF16)
        acc_t = jnp.zeros((d, tm), F32)
        for k in range(f // fc):
            rows = slice(k * fc, (k + 1) * fc)
            a = _dot(w1_ref[rows, :], hb, NT)
            b = _dot(w3_ref[rows, :], hb, NT)
            a_ref[rows, :] = a.astype(BF16)
            b_ref[rows, :] = b.astype(BF16)
            sw = (a * jax.nn.sigmoid(a)) * b
            acc_t = acc_t + _dot(w2_ref[:, rows], sw.astype(BF16))
        acc = acc_t.T
        f_ref[...] = acc.astype(BF16)
        xo_ref[...] = xv + (0.5 * gt) * acc

    tps = s // tm
    rd, cf = _row_spec(tm, d, tps), _col_spec(f, tm, tps)
    return _call(
        body, name=name, grid=(nb, tps), carried=carried,
        out_shape=[jax.ShapeDtypeStruct((t, d), F32), jax.ShapeDtypeStruct((f, t), BF16),
                   jax.ShapeDtypeStruct((f, t), BF16), jax.ShapeDtypeStruct((t, d), BF16)],
        in_specs=[rd, _mod_spec(d), _const((1, d)), _const((f, d)), _const((f, d)), _const((d, f))],
        out_specs=[rd, cf, cf, rd],
        args=(x, mod, g, w1t, w3t, w2t))


def ffn_backward(dxo, x, a_t, b_t, fo, mod, g, w1, w3, w2, sub, nb, name, carried=None):
    t, d = x.shape
    f = w2.shape[0]
    s = t // nb
    tm = _tile(s, 256)
    fc = _ffn_chunk(f)

    def body(dxo_ref, x_ref, a_ref, b_ref, f_ref, mod_ref, g_ref, w1_ref, w3_ref, w2_ref,
             dx_ref, da_ref, db_ref, h_ref, df_ref, dmod_ref, dg_ref):
        bi, i = pl.program_id(0), pl.program_id(1)
        dxo = dxo_ref[...]
        sh, sc, gt = _mod_rows(mod_ref, sub)
        gv = g_ref[...]
        h, n, r = _norm_mod(x_ref[...], gv, sh, sc)
        h_ref[...] = h.astype(BF16)
        dfb = ((0.5 * gt) * dxo).astype(BF16)
        df_ref[...] = dfb
        dgt = 0.5 * jnp.sum(dxo * f_ref[...].astype(F32), axis=0, keepdims=True)
        dh_t = jnp.zeros((d, tm), F32)
        for k in range(f // fc):
            rows = slice(k * fc, (k + 1) * fc)
            ds = _dot(w2_ref[rows, :], dfb, NT)
            av = a_ref[rows, :].astype(F32)
            bv = b_ref[rows, :].astype(F32)
            sig = jax.nn.sigmoid(av)
            dab = (ds * bv * (sig * (1.0 + av * (1.0 - sig)))).astype(BF16)
            dbb = (ds * (av * sig)).astype(BF16)
            da_ref[rows, :] = dab
            db_ref[rows, :] = dbb
            dh_t = dh_t + _dot(w1_ref[:, rows], dab) + _dot(w3_ref[:, rows], dbb)
        dxn, dsh, dsc, dg = _norm_mod_bwd(dh_t.T, n, r, gv, sc)
        dx_ref[...] = dxo + dxn
        _accumulate(dmod_ref, jnp.concatenate([dsh, dsc, dgt], axis=0)[None], i == 0)
        _accumulate(dg_ref, dg, jnp.logical_and(bi == 0, i == 0))

    tps = s // tm
    rd, cf = _row_spec(tm, d, tps), _col_spec(f, tm, tps)
    return _call(
        body, name=name, grid=(nb, tps), carried=carried,
        out_shape=[jax.ShapeDtypeStruct((t, d), F32), jax.ShapeDtypeStruct((f, t), BF16),
                   jax.ShapeDtypeStruct((f, t), BF16), jax.ShapeDtypeStruct((t, d), BF16),
                   jax.ShapeDtypeStruct((t, d), BF16), jax.ShapeDtypeStruct((nb, 3, d), F32),
                   jax.ShapeDtypeStruct((1, d), F32)],
        in_specs=[rd, rd, cf, cf, rd, _mod_spec(d), _const((1, d)), _const((d, f)), _const((d, f)), _const((f, d))],
        out_specs=[rd, cf, cf, rd, rd, pl.BlockSpec((1, 3, d), lambda b, i: (b, 0, 0)), _full((1, d))],
        args=(dxo, x, a_t, b_t, fo, mod, g, w1, w3, w2))


def nn_matmul(lhs_t, rhs, name, lhs2_t=None, carried=None):
    m, t = lhs_t.shape
    n = rhs.shape[1]
    tk = _tile(t, 512)
    tmm = m if m <= 1536 else m // 2
    swiglu = lhs2_t is not None
    nk = t // tk

    def body(*refs):
        if swiglu:
            a_ref, a2_ref, b_ref, o_ref, acc_ref = refs
            av = a_ref[...].astype(F32)
            lv = ((av * jax.nn.sigmoid(av)) * a2_ref[...].astype(F32)).astype(BF16)
        else:
            a_ref, b_ref, o_ref, acc_ref = refs
            lv = a_ref[...]
        k = pl.program_id(1)
        _accumulate(acc_ref, _dot(lv, b_ref[...]), k == 0)

        @pl.when(k == nk - 1)
        def _():
            o_ref[...] = acc_ref[...].astype(BF16)

    lspec = pl.BlockSpec((tmm, tk), lambda j, k: (j, k))
    ins = [lhs_t, lhs2_t, rhs] if swiglu else [lhs_t, rhs]
    return _call(
        body, name=name, grid=(m // tmm, nk), carried=carried,
        out_shape=[jax.ShapeDtypeStruct((m, n), BF16)],
        in_specs=([lspec, lspec] if swiglu else [lspec]) + [pl.BlockSpec((tk, n), lambda j, k: (k, 0))],
        out_specs=[pl.BlockSpec((tmm, n), lambda j, k: (j, 0))],
        scratch_shapes=[pltpu.VMEM((tmm, n), F32)],
        args=ins)


def tn_matmul(lhs, rhs, name):
    t, m = lhs.shape
    n = rhs.shape[1]
    tk = _tile(t, 512)
    nk = t // tk

    def body(a_ref, b_ref, o_ref, acc_ref):
        k = pl.program_id(0)
        _accumulate(acc_ref, _dot(a_ref[...], b_ref[...], TN), k == 0)

        @pl.when(k == nk - 1)
        def _():
            o_ref[...] = acc_ref[...].astype(BF16)

    return pl.pallas_call(
        body, name=name, grid=(nk,),
        out_shape=jax.ShapeDtypeStruct((m, n), BF16),
        in_specs=[pl.BlockSpec((tk, m), lambda k: (k, 0)), pl.BlockSpec((tk, n), lambda k: (k, 0))],
        out_specs=pl.BlockSpec((m, n), lambda k: (0, 0)),
        scratch_shapes=[pltpu.VMEM((m, n), F32)],
        compiler_params=_params(("arbitrary",)),
    )(lhs, rhs)


def mixer_proj_forward(x, mod, g, w_in_t, cw, sw, nb, carried=None):
    t, d = x.shape
    s = t // nb
    tm = _tile(s, 512)
    pieces = [(0, cw, "bf16"), (cw, cw, "bf16"), (2 * cw, cw, "bf16"), (3 * cw, sw, "f32"),
              (3 * cw + sw, d, "sig"), (3 * cw + sw + d, d, "sig")]

    def body(x_ref, mod_ref, g_ref, w_ref, *outs):
        h_ref = outs[-1]
        sh, sc, _ = _mod_rows(mod_ref, 1)
        h, _, _ = _norm_mod(x_ref[...], g_ref[...], sh, sc)
        hb = h.astype(BF16)
        h_ref[...] = hb
        for (off, width, kind), o_ref in zip(pieces, outs[:-1]):
            ck = _tile(width, 512)
            for j in range(width // ck):
                p = _dot(hb, w_ref[off + j * ck:off + (j + 1) * ck, :], NT)
                if kind == "sig":
                    p = jax.nn.sigmoid(p)
                o_ref[:, j * ck:(j + 1) * ck] = p.astype(o_ref.dtype)

    tps = s // tm
    widths = [(cw, BF16), (cw, BF16), (cw, BF16), (sw, F32), (d, BF16), (d, BF16), (d, BF16)]
    return _call(
        body, name="mixer_proj_forward", grid=(nb, tps), carried=carried,
        out_shape=[jax.ShapeDtypeStruct((t, w), dt) for w, dt in widths],
        in_specs=[_row_spec(tm, d, tps), _mod_spec(d), _const((1, d)), _const(w_in_t.shape)],
        out_specs=[_row_spec(tm, w, tps) for w, _ in widths],
        args=(x, mod, g, w_in_t))


def mixer_proj_backward(dgb, dgc, dv, du, dgla, dglb, dxo, x, mod, g, w_in_t, nb, carried=None):
    t, d = x.shape
    s = t // nb
    tm = _tile(s, 512)
    parts = [dgb, dgc, dv, du, dgla, dglb]
    offs = [0]
    for p in parts:
        offs.append(offs[-1] + p.shape[1])

    def body(*refs):
        p_refs = refs[:6]
        dxo_ref, x_ref, mod_ref, g_ref, w_ref, dx_ref, dmod_ref, dg_ref = refs[6:]
        bi, i = pl.program_id(0), pl.program_id(1)
        dh = jnp.zeros((tm, d), F32)
        for p_ref, off in zip(p_refs, offs):
            width = p_ref.shape[1]
            ck = _tile(width, 512)
            for j in range(width // ck):
                dh = dh + _dot(p_ref[:, j * ck:(j + 1) * ck], w_ref[off + j * ck:off + (j + 1) * ck, :])
        sh, sc, _ = _mod_rows(mod_ref, 1)
        gv = g_ref[...]
        _, n, r = _norm_mod(x_ref[...], gv, sh, sc)
        dxn, dsh, dsc, dg = _norm_mod_bwd(dh, n, r, gv, sc)
        dx_ref[...] = dxo_ref[...] + dxn
        _accumulate(dmod_ref, jnp.concatenate([dsh, dsc], axis=0)[None], i == 0)
        _accumulate(dg_ref, dg, jnp.logical_and(bi == 0, i == 0))

    tps = s // tm
    rd = _row_spec(tm, d, tps)
    return _call(
        body, name="mixer_proj_backward", grid=(nb, tps), carried=carried,
        out_shape=[jax.ShapeDtypeStruct((t, d), F32), jax.ShapeDtypeStruct((nb, 2, d), F32),
                   jax.ShapeDtypeStruct((1, d), F32)],
        in_specs=[_row_spec(tm, p.shape[1], tps) for p in parts]
        + [rd, rd, _mod_spec(d), _const((1, d)), _const(w_in_t.shape)],
        out_specs=[rd, pl.BlockSpec((1, 2, d), lambda b, i: (b, 0, 0)), _full((1, d))],
        args=(*parts, dxo, x, mod, g, w_in_t))


GROUPS_PER_BLOCK = 8
ROWS = 8


def _scan_rows(xr, xi, masks, shifts):
    for (mr, mi), sft in zip(masks, shifts):
        sr, si = pltpu.roll(xr, sft, 0), pltpu.roll(xi, sft, 0)
        xr, xi = xr + mr * sr - mi * si, xi + mr * si + mi * sr
    return xr, xi


def _ssm_scan_forward(xr_s, xi_s, row0, n_groups, lanes, ml_ref, ap_ref, carry):
    masks = [(ml_ref[d, 0, :, lanes], ml_ref[d, 1, :, lanes]) for d in range(3)]
    apr, api = ap_ref[0, :, lanes], ap_ref[1, :, lanes]

    def group(k, c):
        cr, ci = c
        rows = pl.ds(pl.multiple_of(row0 + k * ROWS, ROWS), ROWS)
        xr, xi = _scan_rows(xr_s[rows, lanes], xi_s[rows, lanes], masks, (1, 2, 4))
        xr, xi = xr + apr * cr - api * ci, xi + apr * ci + api * cr
        xr_s[rows, lanes] = xr
        xi_s[rows, lanes] = xi
        return jnp.broadcast_to(xr[ROWS - 1:ROWS], xr.shape), jnp.broadcast_to(xi[ROWS - 1:ROWS], xi.shape)

    return lax.fori_loop(0, n_groups, group, carry)


def ssm_forward(u, bd, cd, ml, ap, nb):
    t, w = u.shape
    s = t // nb
    tc = _tile(s, 256)
    nq, ub, lq = bd.shape[1], bd.shape[2], bd.shape[3]
    nl = nq * lq
    nch = s // tc

    def body(u_ref, bd_ref, cd_ref, ml_ref, ap_ref, y_ref, st_ref, xr_s, xi_s, car_s):
        i = pl.program_id(1)

        @pl.when(i == 0)
        def _():
            car_s[...] = jnp.zeros_like(car_s)

        st_ref[0] = car_s[...]
        ub16 = u_ref[...].astype(BF16)
        for q in range(nq):
            lanes = slice(q * lq, (q + 1) * lq)
            uq = ub16[:, q * ub:(q + 1) * ub]
            xr_s[:, lanes] = _dot(uq, bd_ref[0, q])
            xi_s[:, lanes] = _dot(uq, bd_ref[1, q])
        for q in range(nq):
            lanes = slice(q * lq, (q + 1) * lq)
            cr, ci = _ssm_scan_forward(xr_s, xi_s, 0, tc // ROWS, lanes, ml_ref, ap_ref,
                                       (car_s[0, :, lanes], car_s[1, :, lanes]))
            car_s[0, :, lanes] = cr
            car_s[1, :, lanes] = ci
        for q in range(nq):
            lanes = slice(q * lq, (q + 1) * lq)
            y_ref[:, q * ub:(q + 1) * ub] = (_dot(xr_s[:, lanes].astype(BF16), cd_ref[0, q])
                                             + _dot(xi_s[:, lanes].astype(BF16), cd_ref[1, q]))

    return pl.pallas_call(
        body, name="ssm_forward", grid=(nb, nch),
        out_shape=[jax.ShapeDtypeStruct((t, w), F32), jax.ShapeDtypeStruct((nb * nch, 2, ROWS, nl), F32)],
        in_specs=[pl.BlockSpec((tc, w), lambda b, i: (b * nch + i, 0)), _const(bd.shape), _const(cd.shape),
                  _const(ml.shape), _const(ap.shape)],
        out_specs=[pl.BlockSpec((tc, w), lambda b, i: (b * nch + i, 0)),
                   pl.BlockSpec((1, 2, ROWS, nl), lambda b, i: (b * nch + i, 0, 0, 0))],
        scratch_shapes=[pltpu.VMEM((tc, nl), F32), pltpu.VMEM((tc, nl), F32), pltpu.VMEM((2, ROWS, nl), F32)],
        compiler_params=_params(("arbitrary", "arbitrary")),
    )(u, bd, cd, ml, ap)


def ssm_backward(u, dy, st, bd, cd, ml, ap, mlb, apb, dskip, nb, carried=None):
    t, w = u.shape
    s = t // nb
    tc = _tile(s, 256)
    nq, ub, lq = bd.shape[1], bd.shape[2], bd.shape[3]
    nl = nq * lq
    nch = s // tc
    ng = tc // ROWS

    def body(u_ref, dy_ref, st_ref, bd_ref, cd_ref, ml_ref, ap_ref, mlb_ref, apb_ref, dsk_ref,
             du_ref, dab_ref, dbd_ref, dcd_ref, xr_s, xi_s, lr_s, li_s, car_s):
        bi, i = pl.program_id(0), pl.program_id(1)
        first = jnp.logical_and(bi == 0, i == 0)

        @pl.when(i == 0)
        def _():
            car_s[...] = jnp.zeros_like(car_s)

        @pl.when(first)
        def _():
            dab_ref[...] = jnp.zeros_like(dab_ref)
            dbd_ref[...] = jnp.zeros_like(dbd_ref)
            dcd_ref[...] = jnp.zeros_like(dcd_ref)

        uv = u_ref[...]
        dyv = dy_ref[...]
        ub16 = uv.astype(BF16)
        dyb16 = dyv.astype(BF16)
        xr_s[0:ROWS, :] = st_ref[0, 0]
        xi_s[0:ROWS, :] = st_ref[0, 1]
        for q in range(nq):
            lanes = slice(q * lq, (q + 1) * lq)
            uq = ub16[:, q * ub:(q + 1) * ub]
            dq = dyb16[:, q * ub:(q + 1) * ub]
            xr_s[ROWS:, lanes] = _dot(uq, bd_ref[0, q])
            xi_s[ROWS:, lanes] = _dot(uq, bd_ref[1, q])
            lr_s[:, lanes] = _dot(dq, cd_ref[0, q], NT)
            li_s[:, lanes] = _dot(dq, cd_ref[1, q], NT)
        for q in range(nq):
            lanes = slice(q * lq, (q + 1) * lq)
            _ssm_scan_forward(xr_s, xi_s, ROWS, ng, lanes, ml_ref, ap_ref,
                              (st_ref[0, 0, :, lanes], st_ref[0, 1, :, lanes]))
            masks = [(mlb_ref[d, 0, :, lanes], mlb_ref[d, 1, :, lanes]) for d in range(3)]
            apr, api = apb_ref[0, :, lanes], apb_ref[1, :, lanes]
            row_is_0 = lax.broadcasted_iota(jnp.int32, (ROWS, lq), 0) == 0

            def group(kk, c):
                cr, ci, accr, acci = c
                k = ng - 1 - kk
                rows = pl.ds(pl.multiple_of(k * ROWS, ROWS), ROWS)
                lr, li = _scan_rows(lr_s[rows, lanes], li_s[rows, lanes], masks, (7, 6, 4))
                lr, li = lr + apr * cr - api * ci, li + apr * ci + api * cr
                lr_s[rows, lanes] = lr
                li_s[rows, lanes] = li
                prev = pl.ds(pl.multiple_of(k * ROWS, ROWS), ROWS)
                cur = pl.ds(pl.multiple_of(k * ROWS + ROWS, ROWS), ROWS)
                pr = jnp.broadcast_to(xr_s[prev, lanes][ROWS - 1:ROWS], (ROWS, lq))
                pi = jnp.broadcast_to(xi_s[prev, lanes][ROWS - 1:ROWS], (ROWS, lq))
                xpr = jnp.where(row_is_0, pr, pltpu.roll(xr_s[cur, lanes], 1, 0))
                xpi = jnp.where(row_is_0, pi, pltpu.roll(xi_s[cur, lanes], 1, 0))
                accr = accr + lr * xpr + li * xpi
                acci = acci + li * xpr - lr * xpi
                return (jnp.broadcast_to(lr[0:1], lr.shape), jnp.broadcast_to(li[0:1], li.shape), accr, acci)

            zero = jnp.zeros((ROWS, lq), F32)
            cr, ci, accr, acci = lax.fori_loop(0, ng, group, (car_s[0, :, lanes], car_s[1, :, lanes], zero, zero))
            car_s[0, :, lanes] = cr
            car_s[1, :, lanes] = ci
            dab_ref[0, :, lanes] += accr
            dab_ref[1, :, lanes] += acci
        dsk = dsk_ref[...]
        for q in range(nq):
            lanes = slice(q * lq, (q + 1) * lq)
            cols = slice(q * ub, (q + 1) * ub)
            lrb, lib = lr_s[:, lanes].astype(BF16), li_s[:, lanes].astype(BF16)
            uq, dq = ub16[:, cols], dyb16[:, cols]
            du = _dot(lrb, bd_ref[0, q], NT) + _dot(lib, bd_ref[1, q], NT) + dsk[:, cols] * dyv[:, cols]
            du_ref[:, cols] = du.astype(BF16)
            dbd_ref[0, q] += _dot(uq, lrb, TN)
            dbd_ref[1, q] += _dot(uq, lib, TN)
            dcd_ref[0, q] += _dot(xr_s[ROWS:, lanes].astype(BF16), dq, TN)
            dcd_ref[1, q] += _dot(xi_s[ROWS:, lanes].astype(BF16), dq, TN)

    rev = lambda b, i: (b * nch + nch - 1 - i, 0)
    return _call(
        body, name="ssm_backward", grid=(nb, nch), carried=carried,
        out_shape=[jax.ShapeDtypeStruct((t, w), BF16), jax.ShapeDtypeStruct((2, ROWS, nl), F32),
                   jax.ShapeDtypeStruct(bd.shape, F32), jax.ShapeDtypeStruct(cd.shape, F32)],
        in_specs=[pl.BlockSpec((tc, w), rev), pl.BlockSpec((tc, w), rev),
                  pl.BlockSpec((1, 2, ROWS, nl), lambda b, i: (b * nch + nch - 1 - i, 0, 0, 0)),
                  _const(bd.shape), _const(cd.shape), _const(ml.shape), _const(ap.shape), _const(mlb.shape),
                  _const(apb.shape), _const((1, w))],
        out_specs=[pl.BlockSpec((tc, w), rev), _full((2, ROWS, nl)), _full(bd.shape), _full(cd.shape)],
        scratch_shapes=[pltpu.VMEM((tc + ROWS, nl), F32), pltpu.VMEM((tc + ROWS, nl), F32),
                        pltpu.VMEM((tc, nl), F32), pltpu.VMEM((tc, nl), F32), pltpu.VMEM((2, ROWS, nl), F32)],
        args=(u, dy, st, bd, cd, ml, ap, mlb, apb, dskip))


def ssm_discretise(a_re, a_im, b_re, b_im, log_dt):
    dt = jnp.exp(log_dt)[:, None]
    er = jnp.exp(a_re * dt)
    abr, abi = er * jnp.cos(a_im * dt), er * jnp.sin(a_im * dt)
    den = a_re * a_re + a_im * a_im
    nr, ni = abr - 1.0, abi
    fr = ((nr * a_re + ni * a_im) / den)[..., None]
    fi = ((ni * a_re - nr * a_im) / den)[..., None]
    return abr, abi, fr * b_re - fi * b_im, fr * b_im + fi * b_re


def _complex_powers(abr, abi, n):
    pr, pi = [abr.reshape(-1)], [abi.reshape(-1)]
    for _ in range(n - 1):
        pr, pi = pr + [pr[-1] * pr[0] - pi[-1] * pi[0]], pi + [pr[-1] * pi[0] + pi[-1] * pr[0]]
    return jnp.stack(pr), jnp.stack(pi)


def ssm_tables(abr, abi, bbr, bbi, c_re, c_im):
    g, p, h = bbr.shape
    nq = g // GROUPS_PER_BLOCK
    pw_r, pw_i = _complex_powers(abr, abi, ROWS)
    row = jnp.arange(ROWS)[:, None]
    ml, mlb = [], []
    for d in (1, 2, 4):
        fr, fi = pw_r[d - 1][None], pw_i[d - 1][None]
        ml.append(jnp.stack([jnp.where(row >= d, fr, 0.0), jnp.where(row >= d, fi, 0.0)]))
        mlb.append(jnp.stack([jnp.where(row + d < ROWS, fr, 0.0), jnp.where(row + d < ROWS, -fi, 0.0)]))
    ap = jnp.stack([pw_r, pw_i])
    apb = jnp.stack([pw_r[::-1], -pw_i[::-1]])
    eye = jnp.eye(GROUPS_PER_BLOCK, dtype=F32)

    def block_diag_in(bb):
        bq = bb.reshape(nq, GROUPS_PER_BLOCK, p, h)
        return jnp.einsum("qaph,ab->qahbp", bq, eye).reshape(nq, GROUPS_PER_BLOCK * h, GROUPS_PER_BLOCK * p)

    def block_diag_out(cc):
        cq = cc.reshape(nq, GROUPS_PER_BLOCK, h, p)
        return jnp.einsum("qahp,ab->qapbh", cq, eye).reshape(nq, GROUPS_PER_BLOCK * p, GROUPS_PER_BLOCK * h)

    bd = jnp.stack([block_diag_in(bbr), block_diag_in(bbi)]).astype(BF16)
    cd = jnp.stack([block_diag_out(c_re), block_diag_out(-c_im)]).astype(BF16)
    return bd, cd, jnp.stack(ml), ap, jnp.stack(mlb), apb


def ssm_table_grads(dab, dbd, dcd, g, p, h):
    nq = g // GROUPS_PER_BLOCK
    dabr, dabi = dab[0].sum(0).reshape(g, p), dab[1].sum(0).reshape(g, p)
    b5 = dbd.reshape(2, nq, GROUPS_PER_BLOCK, h, GROUPS_PER_BLOCK, p)
    dbb = jnp.einsum("rqahap->rqaph", b5).reshape(2, g, p, h)
    c5 = dcd.reshape(2, nq, GROUPS_PER_BLOCK, p, GROUPS_PER_BLOCK, h)
    dcc = jnp.einsum("rqapah->rqahp", c5).reshape(2, g, h, p)
    return dabr, dabi, dbb[0], dbb[1], dcc[0], -dcc[1]


HALO = 16


def _conv_inputs(gc_ref, v_ref, gch_ref, vh_ref, cv_s, i, tm):
    cv = gc_ref[...].astype(F32) * v_ref[...].astype(F32)
    halo = gch_ref[...].astype(F32) * vh_ref[...].astype(F32)
    cv_s[0:HALO, :] = jnp.where(i == 0, 0.0, halo)
    cv_s[HALO:, :] = cv
    return cv, cv_s[HALO - 1:HALO - 1 + tm, :], cv_s[HALO - 2:HALO - 2 + tm, :]


def _halo_spec(tm, width, tiles_per_seq):
    per = tm // HALO
    return pl.BlockSpec((HALO, width), lambda b, i: (jnp.maximum((b * tiles_per_seq + i) * per - 1, 0), 0))


def mixer_merge_forward(x, gb, gc, v, sga, sgb, yssm, u, mod, conv_w, dskip, wco, wglu, wso_t, wout, nb):
    t, d = x.shape
    cw, sw = gb.shape[1], u.shape[1]
    s = t // nb
    tm = _tile(s, 256)
    tps = s // tm

    def body(x_ref, gb_ref, gc_ref, v_ref, gch_ref, vh_ref, sga_ref, sgb_ref, ys_ref, u_ref, mod_ref, cw_ref,
             dsk_ref, wco_ref, wglu_ref, wso_ref, wout_ref, xo_ref, ya_ref, yb_ref, mix_ref, cv_s):
        i = pl.program_id(1)
        cv, cv1, cv2 = _conv_inputs(gc_ref, v_ref, gch_ref, vh_ref, cv_s, i, tm)
        w = cw_ref[...]
        conv = w[0:1] * cv2 + w[1:2] * cv1 + w[2:3] * cv
        ya = _dot((gb_ref[...].astype(F32) * conv).astype(BF16), wco_ref[...])
        s0 = ys_ref[...] + dsk_ref[...] * u_ref[...]
        s1, _ = _gelu(s0)
        z = _dot(s1.astype(BF16), wglu_ref[...])
        s2 = s1 * jax.nn.sigmoid(z)
        yb = _dot(s2.astype(BF16), wso_ref[...], NT)
        merged = sga_ref[...].astype(F32) * ya + sgb_ref[...].astype(F32) * yb
        mix = _dot(merged.astype(BF16), wout_ref[...])
        _, _, gt = _mod_rows(mod_ref, 1)
        xo_ref[...] = x_ref[...] + gt * mix
        ya_ref[...] = ya.astype(BF16)
        yb_ref[...] = yb.astype(BF16)
        mix_ref[...] = mix.astype(BF16)

    rd, rc, rw = _row_spec(tm, d, tps), _row_spec(tm, cw, tps), _row_spec(tm, sw, tps)
    hc = _halo_spec(tm, cw, tps)
    return pl.pallas_call(
        body, name="mixer_merge_forward", grid=(nb, tps),
        out_shape=[jax.ShapeDtypeStruct((t, d), F32)] + [jax.ShapeDtypeStruct((t, d), BF16)] * 3,
        in_specs=[rd, rc, rc, rc, hc, hc, rd, rd, rw, rw, _mod_spec(d), _const(conv_w.shape), _const((1, sw)),
                  _const(wco.shape), _const(wglu.shape), _const(wso_t.shape), _const(wout.shape)],
        out_specs=[rd, rd, rd, rd],
        scratch_shapes=[pltpu.VMEM((tm + HALO, cw), F32)],
        compiler_params=_params(("arbitrary", "arbitrary")),
    )(x, gb, gc, v, gc, v, sga, sgb, yssm, u, mod, conv_w, dskip, wco, wglu, wso_t, wout)


def mixer_merge_backward(dxo, mix, ya, yb, gb, gc, v, sga, sgb, yssm, u, mod, conv_w, dskip,
                         wco, wglu, wso_t, wout, nb, carried=None):
    t, d = dxo.shape
    cw, sw = gb.shape[1], u.shape[1]
    s = t // nb
    tm = _tile(s, 256)
    tps = s // tm

    def body(dxo_ref, mix_ref, ya_ref, yb_ref, gb_ref, gc_ref, v_ref, gch_ref, vh_ref, sga_ref, sgb_ref, ys_ref,
             u_ref, mod_ref, cw_ref, dsk_ref, wco_ref, wglu_ref, wso_ref, wout_ref,
             dgla_ref, dglb_ref, dgb_ref, dconv_ref, ds0_ref, dgt_ref, ddsk_ref, dwout_ref, dwco_ref, dwso_ref,
             dwglu_ref, cv_s):
        bi, i = pl.program_id(0), pl.program_id(1)
        first = jnp.logical_and(bi == 0, i == 0)
        dxo = dxo_ref[...]
        _, _, gt = _mod_rows(mod_ref, 1)
        dmix = (gt * dxo).astype(BF16)
        _accumulate(dgt_ref, jnp.sum(dxo * mix_ref[...].astype(F32), axis=0, keepdims=True)[None], i == 0)
        ya, yb = ya_ref[...].astype(F32), yb_ref[...].astype(F32)
        sga, sgb = sga_ref[...].astype(F32), sgb_ref[...].astype(F32)
        merged = (sga * ya + sgb * yb).astype(BF16)
        _accumulate(dwout_ref, _dot(merged, dmix, TN), first)
        dmerged = _dot(dmix, wout_ref[...], NT)
        dgla_ref[...] = (dmerged * ya * sga * (1.0 - sga)).astype(BF16)
        dglb_ref[...] = (dmerged * yb * sgb * (1.0 - sgb)).astype(BF16)
        dya = (dmerged * sga).astype(BF16)
        dyb = (dmerged * sgb).astype(BF16)
        cv, cv1, cv2 = _conv_inputs(gc_ref, v_ref, gch_ref, vh_ref, cv_s, i, tm)
        w = cw_ref[...]
        conv = w[0:1] * cv2 + w[1:2] * cv1 + w[2:3] * cv
        gbv = gb_ref[...].astype(F32)
        _accumulate(dwco_ref, _dot((gbv * conv).astype(BF16), dya, TN), first)
        dya_in = _dot(dya, wco_ref[...], NT)
        dgb_ref[...] = (dya_in * conv).astype(BF16)
        dconv_ref[...] = dya_in * gbv
        uv = u_ref[...]
        s0 = ys_ref[...] + dsk_ref[...] * uv
        s1, th = _gelu(s0)
        s1b = s1.astype(BF16)
        sz = jax.nn.sigmoid(_dot(s1b, wglu_ref[...]))
        s2b = (s1 * sz).astype(BF16)
        _accumulate(dwso_ref, _dot(dyb, s2b, TN), first)
        ds2 = _dot(dyb, wso_ref[...])
        dz = (ds2 * s1 * sz * (1.0 - sz)).astype(BF16)
        _accumulate(dwglu_ref, _dot(s1b, dz, TN), first)
        ds1 = ds2 * sz + _dot(dz, wglu_ref[...], NT)
        ds0 = ds1 * _gelu_grad(s0, th)
        ds0_ref[...] = ds0
        _accumulate(ddsk_ref, jnp.sum(ds0 * uv, axis=0, keepdims=True), first)

    rd, rc, rw = _row_spec(tm, d, tps), _row_spec(tm, cw, tps), _row_spec(tm, sw, tps)
    hc = _halo_spec(tm, cw, tps)
    return _call(
        body, name="mixer_merge_backward", grid=(nb, tps), carried=carried,
        out_shape=[jax.ShapeDtypeStruct((t, d), BF16), jax.ShapeDtypeStruct((t, d), BF16),
                   jax.ShapeDtypeStruct((t, cw), BF16), jax.ShapeDtypeStruct((t, cw), F32),
                   jax.ShapeDtypeStruct((t, sw), F32), jax.ShapeDtypeStruct((nb, 1, d), F32),
                   jax.ShapeDtypeStruct((1, sw), F32), jax.ShapeDtypeStruct(wout.shape, F32),
                   jax.ShapeDtypeStruct(wco.shape, F32), jax.ShapeDtypeStruct(wso_t.shape, F32),
                   jax.ShapeDtypeStruct(wglu.shape, F32)],
        in_specs=[rd, rd, rd, rd, rc, rc, rc, hc, hc, rd, rd, rw, rw, _mod_spec(d), _const(conv_w.shape),
                  _const((1, sw)), _const(wco.shape), _const(wglu.shape), _const(wso_t.shape), _const(wout.shape)],
        out_specs=[rd, rd, rc, rc, rw, pl.BlockSpec((1, 1, d), lambda b, i: (b, 0, 0)), _full((1, sw)),
                   _full(wout.shape), _full(wco.shape), _full(wso_t.shape), _full(wglu.shape)],
        scratch_shapes=[pltpu.VMEM((tm + HALO, cw), F32)],
        args=(dxo, mix, ya, yb, gb, gc, v, gc, v, sga, sgb, yssm, u, mod, conv_w, dskip, wco, wglu, wso_t, wout))


def conv_backward(dconv, gc, v, conv_w, nb):
    t, cw = dconv.shape
    s = t // nb
    tm = _tile(s, 512)
    tps = s // tm
    per = tm // ROWS

    def body(dc_ref, dcn_ref, gc_ref, v_ref, gch_ref, vh_ref, cw_ref, dgc_ref, dv_ref, dw_ref, cv_s, dc_s):
        bi, i = pl.program_id(0), pl.program_id(1)
        cv, cv1, cv2 = _conv_inputs(gc_ref, v_ref, gch_ref, vh_ref, cv_s, i, tm)
        dc = dc_ref[...]
        dc_s[0:tm, :] = dc
        dc_s[tm:, :] = jnp.where(i == tps - 1, 0.0, dcn_ref[...])
        w = cw_ref[...]
        dcv = w[2:3] * dc + w[1:2] * dc_s[1:1 + tm, :] + w[0:1] * dc_s[2:2 + tm, :]
        dgc_ref[...] = (dcv * v_ref[...].astype(F32)).astype(BF16)
        dv_ref[...] = (dcv * gc_ref[...].astype(F32)).astype(BF16)
        dw = jnp.concatenate([jnp.sum(dc * cv2, axis=0, keepdims=True), jnp.sum(dc * cv1, axis=0, keepdims=True),
                              jnp.sum(dc * cv, axis=0, keepdims=True)], axis=0)
        _accumulate(dw_ref, dw, jnp.logical_and(bi == 0, i == 0))

    rc = _row_spec(tm, cw, tps)
    nxt = pl.BlockSpec((ROWS, cw), lambda b, i: (jnp.minimum((b * tps + i + 1) * per, nb * tps * per - 1), 0))
    hc = _halo_spec(tm, cw, tps)
    return pl.pallas_call(
        body, name="conv_backward", grid=(nb, tps),
        out_shape=[jax.ShapeDtypeStruct((t, cw), BF16), jax.ShapeDtypeStruct((t, cw), BF16),
                   jax.ShapeDtypeStruct(conv_w.shape, F32)],
        in_specs=[rc, nxt, rc, rc, hc, hc, _full(conv_w.shape)],
        out_specs=[rc, rc, _full(conv_w.shape)],
        scratch_shapes=[pltpu.VMEM((tm + HALO, cw), F32), pltpu.VMEM((tm + ROWS, cw), F32)],
        compiler_params=_params(("arbitrary", "arbitrary")),
    )(dconv, dconv, gc, v, gc, v, conv_w)


def loss_forward_backward(x, target, g):
    t, d = x.shape
    tm = _tile(t, 512)

    def body(x_ref, t_ref, g_ref, l_ref, dx_ref, dg_ref):
        i = pl.program_id(0)
        xv = x_ref[...]
        gv = g_ref[...]
        r = lax.rsqrt(jnp.mean(xv * xv, axis=-1, keepdims=True) + EPS)
        n = xv * r
        err = n * gv - t_ref[...]
        l_ref[...] = jnp.full(l_ref.shape, 0.5 * jnp.sum(jnp.mean(err * err, axis=-1)), F32)
        dy = err * (1.0 / d)
        dn = dy * gv
        dx_ref[...] = r * (dn - n * jnp.mean(n * dn, axis=-1, keepdims=True))
        _accumulate(dg_ref, jnp.sum(dy * n, axis=0, keepdims=True), i == 0)

    row = pl.BlockSpec((tm, d), lambda i: (i, 0))
    return pl.pallas_call(
        body, name="loss_forward_backward", grid=(t // tm,),
        out_shape=[jax.ShapeDtypeStruct((t // tm, 1, 128), F32), jax.ShapeDtypeStruct((t, d), F32),
                   jax.ShapeDtypeStruct((1, d), F32)],
        in_specs=[row, row, _full((1, d))],
        out_specs=[pl.BlockSpec((1, 1, 128), lambda i: (i, 0, 0)), row, _full((1, d))],
        compiler_params=_params(("arbitrary",)),
    )(x, target, g)


def sum_slots(slots, name):
    _, r, c = slots.shape
    tr = _tile(r, 352) if r % 352 == 0 else _tile(r, 256)

    def body(s_ref, o_ref):
        acc = s_ref[0].astype(F32)
        for j in range(1, N_DEV):
            acc = acc + s_ref[j].astype(F32)
        o_ref[...] = acc

    return pl.pallas_call(
        body, name=name, grid=(r // tr,),
        out_shape=jax.ShapeDtypeStruct((r, c), F32),
        in_specs=[pl.BlockSpec((N_DEV, tr, c), lambda i: (0, i, 0))],
        out_specs=pl.BlockSpec((tr, c), lambda i: (i, 0)),
        compiler_params=_params(("arbitrary",)),
    )(slots)


def adamw_update(w, g, m, v, name):
    r, c = w.shape
    tr = _tile(r, 256) if r % 8 == 0 else r

    def body(w_ref, g_ref, m_ref, v_ref, d_ref, mo_ref, vo_ref):
        d_ref[...], mo_ref[...], vo_ref[...] = _adamw(w_ref[...], g_ref[...], m_ref[...], v_ref[...])

    spec = pl.BlockSpec((tr, c), lambda i: (i, 0))
    return pl.pallas_call(
        body, name=name, grid=(r // tr,),
        out_shape=[jax.ShapeDtypeStruct((r, c), F32)] * 3,
        in_specs=[spec] * 4, out_specs=[spec] * 3,
        compiler_params=_params(("arbitrary",)),
    )(w, g, m, v)


def adamw_update_small(ws, gs, ms, vs):
    n = len(ws)

    def body(*refs):
        w_r, g_r, m_r, v_r = refs[:n], refs[n:2 * n], refs[2 * n:3 * n], refs[3 * n:4 * n]
        d_r, mo_r, vo_r = refs[4 * n:5 * n], refs[5 * n:6 * n], refs[6 * n:7 * n]
        for k in range(n):
            d_r[k][...], mo_r[k][...], vo_r[k][...] = _adamw(w_r[k][...], g_r[k][...], m_r[k][...], v_r[k][...])

    shapes = [jax.ShapeDtypeStruct(w.shape, F32) for w in ws]
    out = pl.pallas_call(body, name="adamw_update_small", out_shape=shapes * 3,
                         compiler_params=_params())(*ws, *gs, *ms, *vs)
    return out[:n], out[n:2 * n], out[2 * n:]


def _slots(grad_t):
    return grad_t.reshape(N_DEV, grad_t.shape[0] // N_DEV, grad_t.shape[1])


def kernel(x, c, w_ada, b_ada, g_ffn1, w1_a, w3_a, w2_a, g_mix, w_in, conv_w, w_conv_out, a_re, a_im, b_re, b_im, c_re, c_im, log_dt, d_skip, w_glu, w_ssm_out, w_out, g_ffn2, w1_b, w3_b, w2_b, g_final, loss_target, m_w_ada, m_b_ada, m_g_ffn1, m_w1_a, m_w3_a, m_w2_a, m_g_mix, m_w_in, m_conv_w, m_w_conv_out, m_a_re, m_a_im, m_b_re, m_b_im, m_c_re, m_c_im, m_log_dt, m_d_skip, m_w_glu, m_w_ssm_out, m_w_out, m_g_ffn2, m_w1_b, m_w3_b, m_w2_b, m_g_final, v_w_ada, v_b_ada, v_g_ffn1, v_w1_a, v_w3_a, v_w2_a, v_g_mix, v_w_in, v_conv_w, v_w_conv_out, v_a_re, v_a_im, v_b_re, v_b_im, v_c_re, v_c_im, v_log_dt, v_d_skip, v_w_glu, v_w_ssm_out, v_w_out, v_g_ffn2, v_w1_b, v_w3_b, v_w2_b, v_g_final):
    nb, s, d = x.shape
    t = nb * s
    me = 4 * lax.axis_index("x") + 2 * lax.axis_index("y") + lax.axis_index("c")
    g_n, p_n, h_n = b_re.shape[1:]
    cw_n = w_conv_out.shape[1] * N_DEV
    sw_n = w_ssm_out.shape[1]
    glu_fold = d // w_glu.shape[2]

    weights = dict(w_ada=w_ada, b_ada=b_ada, g_ffn1=g_ffn1, w1_a=w1_a, w3_a=w3_a, w2_a=w2_a, g_mix=g_mix, w_in=w_in,
                   conv_w=conv_w, w_conv_out=w_conv_out, a_re=a_re, a_im=a_im, b_re=b_re, b_im=b_im, c_re=c_re,
                   c_im=c_im, log_dt=log_dt, d_skip=d_skip, w_glu=w_glu, w_ssm_out=w_ssm_out, w_out=w_out,
                   g_ffn2=g_ffn2, w1_b=w1_b, w3_b=w3_b, w2_b=w2_b, g_final=g_final)
    mom1 = dict(w_ada=m_w_ada, b_ada=m_b_ada, g_ffn1=m_g_ffn1, w1_a=m_w1_a, w3_a=m_w3_a, w2_a=m_w2_a, g_mix=m_g_mix,
                w_in=m_w_in, conv_w=m_conv_w, w_conv_out=m_w_conv_out, a_re=m_a_re, a_im=m_a_im, b_re=m_b_re,
                b_im=m_b_im, c_re=m_c_re, c_im=m_c_im, log_dt=m_log_dt, d_skip=m_d_skip, w_glu=m_w_glu,
                w_ssm_out=m_w_ssm_out, w_out=m_w_out, g_ffn2=m_g_ffn2, w1_b=m_w1_b, w3_b=m_w3_b, w2_b=m_w2_b,
                g_final=m_g_final)
    mom2 = dict(w_ada=v_w_ada, b_ada=v_b_ada, g_ffn1=v_g_ffn1, w1_a=v_w1_a, w3_a=v_w3_a, w2_a=v_w2_a, g_mix=v_g_mix,
                w_in=v_w_in, conv_w=v_conv_w, w_conv_out=v_w_conv_out, a_re=v_a_re, a_im=v_a_im, b_re=v_b_re,
                b_im=v_b_im, c_re=v_c_re, c_im=v_c_im, log_dt=v_log_dt, d_skip=v_d_skip, w_glu=v_w_glu,
                w_ssm_out=v_w_ssm_out, w_out=v_w_out, g_ffn2=v_g_ffn2, w1_b=v_w1_b, w3_b=v_w3_b, w2_b=v_w2_b,
                g_final=v_g_final)
    names = list(weights)
    transposed = ("w1_a", "w3_a", "w_in", "w_ssm_out", "w1_b", "w3_b")
    groups = dict(ffn_a=("w1_a", "w3_a", "w2_a"), mixer=("w_in", "w_conv_out", "w_glu", "w_ssm_out", "w_out"),
                  ffn_b=("w1_b", "w3_b", "w2_b"))
    big = groups["ffn_a"] + groups["mixer"] + groups["ffn_b"]

    pad_rows = lambda a: jnp.pad(a, ((0, -a.shape[0] % ROWS), (0, 0)))
    c_all, conv_all = run_stage(GatherStage([pad_rows(c), pad_rows(conv_w[0])]), "gather_cond")
    c_all = c_all.reshape(N_DEV, -1, d)[:, :nb].reshape(N_DEV * nb, d)
    conv_full = conv_all.reshape(N_DEV, ROWS, -1)[:, :CONV_K].transpose(1, 0, 2).reshape(CONV_K, cw_n)
    ada_cols = w_ada.shape[2]
    b_cols = lax.dynamic_slice(b_ada, (0, me * ada_cols), (1, ada_cols))
    mod_cols = ada_forward(c_all, w_ada[0], b_cols)
    (mod_all,) = run_stage(GatherStage([mod_cols]), "gather_mod")
    mod_mine = lax.dynamic_slice(mod_all.reshape(N_DEV, N_DEV * nb, ada_cols), (0, me * nb, 0), (N_DEV, nb, ada_cols))
    mod = mod_mine.transpose(1, 0, 2).reshape(nb, N_MOD, d)

    def shard_rows(name):
        w = weights[name][0]
        if name in transposed:
            w = w.T
        if name == "w_glu":
            w = w.reshape(w.shape[0] // glu_fold, d)
        return w.astype(BF16)

    def gather_stage(group):
        return GatherStage([shard_rows(n) for n in groups[group]])

    gw = {}

    def keep_weights(group, outs):
        for n, w in zip(groups[group], outs):
            gw[n] = w.reshape(sw_n, sw_n) if n == "w_glu" else w

    disc_in = (a_re[0], a_im[0], b_re[0], b_im[0], log_dt[0])
    (abr, abi, bbr, bbi), disc_vjp = jax.vjp(ssm_discretise, *disc_in)
    bd, cd, ml, ap, mlb, apb = ssm_tables(abr, abi, bbr, bbi, c_re[0], c_im[0])

    x0 = x.reshape(t, d)
    keep_weights("ffn_a", run_stage(gather_stage("ffn_a"), "gather_ffn_a"))
    (x1, a1, b1, f1), got = ffn_forward(x0, mod, g_ffn1, gw["w1_a"], gw["w3_a"], gw["w2_a"].T, 0, nb,
                                        "ffn_a_forward", carried=gather_stage("mixer"))
    keep_weights("mixer", got)
    (gb, gc, vv, u, sga, sgb, h2), got = mixer_proj_forward(x1, mod, g_mix, gw["w_in"], cw_n, sw_n, nb,
                                                            carried=gather_stage("ffn_b"))
    keep_weights("ffn_b", got)
    yssm, st = ssm_forward(u, bd, cd, ml, ap, nb)
    x2, ya, yb, mix = mixer_merge_forward(x1, gb, gc, vv, sga, sgb, yssm, u, mod, conv_full, d_skip,
                                          gw["w_conv_out"], gw["w_glu"], gw["w_ssm_out"], gw["w_out"], nb)
    (x3, a3, b3, f3), _ = ffn_forward(x2, mod, g_ffn2, gw["w1_b"], gw["w3_b"], gw["w2_b"].T, 2, nb, "ffn_b_forward")
    loss_parts, dx3, dg_final = loss_forward_backward(x3, loss_target.reshape(t, d), g_final.reshape(1, d))
    loss = lax.psum(jnp.sum(loss_parts[:, 0, 0]), ("x", "y", "c"))

    part, received = {}, {}

    def exchange_stage(ns):
        return ExchangeStage([_slots(part[n]) for n in ns])

    (dx2, da3, db3, h3, df3, dmod3, dg_ffn2), _ = ffn_backward(
        dx3, x2, a3, b3, f3, mod, g_ffn2, gw["w1_b"].T, gw["w3_b"].T, gw["w2_b"], 2, nb, "ffn_b_backward")
    (part["w1_b"],), _ = nn_matmul(da3, h3, "grad_w1_b")
    (part["w3_b"],), _ = nn_matmul(db3, h3, "grad_w3_b")
    (part["w2_b"],), _ = nn_matmul(a3, df3, "grad_w2_b", lhs2_t=b3)
    (dgla, dglb, dgb, dconv, ds0, dgt2, dd_skip, dw_out, dw_co, dw_so_t, dw_glu), got = mixer_merge_backward(
        dx2, mix, ya, yb, gb, gc, vv, sga, sgb, yssm, u, mod, conv_full, d_skip,
        gw["w_conv_out"], gw["w_glu"], gw["w_ssm_out"], gw["w_out"], nb, carried=exchange_stage(groups["ffn_b"]))
    received.update(zip(groups["ffn_b"], got))
    part["w_out"] = dw_out.astype(BF16)
    part["w_conv_out"] = dw_co.astype(BF16)
    part["w_ssm_out"] = dw_so_t.astype(BF16)
    part["w_glu"] = dw_glu.reshape(sw_n // glu_fold, d).astype(BF16)
    (du, dab, dbd, dcd), got = ssm_backward(u, ds0, st, bd, cd, ml, ap, mlb, apb, d_skip, nb,
                                            carried=exchange_stage(groups["mixer"][1:]))
    received.update(zip(groups["mixer"][1:], got))
    dgc, dvv, dconv_w = conv_backward(dconv, gc, vv, conv_full, nb)
    part["w_in"] = jnp.concatenate([tn_matmul(p, h2, "grad_w_in_%d" % k)
                                    for k, p in enumerate((dgb, dgc, dvv, du, dgla, dglb))], axis=0)
    (dx1, dmod2, dg_mix), got = mixer_proj_backward(dgb, dgc, dvv, du, dgla, dglb, dx2, x1, mod, g_mix, gw["w_in"], nb,
                                                    carried=exchange_stage(("w_in",)))
    received["w_in"] = got[0]
    (dx0, da1, db1, h1, df1, dmod1, dg_ffn1), _ = ffn_backward(
        dx1, x0, a1, b1, f1, mod, g_ffn1, gw["w1_a"].T, gw["w3_a"].T, gw["w2_a"], 0, nb, "ffn_a_backward")
    (part["w1_a"],), _ = nn_matmul(da1, h1, "grad_w1_a")
    (part["w3_a"],), got = nn_matmul(db1, h1, "grad_w3_a", carried=exchange_stage(("w1_a",)))
    received["w1_a"] = got[0]
    (part["w2_a"],), got = nn_matmul(a1, df1, "grad_w2_a", lhs2_t=b1, carried=exchange_stage(("w3_a",)))
    received["w3_a"] = got[0]
    (received["w2_a"],) = run_stage(exchange_stage(("w2_a",)), "exchange_w2_a")

    dabr, dabi, dbbr, dbbi, dcr, dci = ssm_table_grads(dab, dbd, dcd, g_n, p_n, h_n)
    gmod = jnp.concatenate([dmod1, dmod2, dgt2, dmod3], axis=1)
    small = dict(gmod=gmod, g_ffn1=dg_ffn1, g_mix=dg_mix, g_ffn2=dg_ffn2, g_final=dg_final, d_skip=dd_skip,
                 abr=dabr, abi=dabi, bbr=dbbr, bbi=dbbi, c_re=dcr, c_im=dci, conv_w=dconv_w)
    flat = jnp.concatenate([a.reshape(-1) for a in small.values()])
    n_small = flat.shape[0]
    n_rows = -(-n_small // (128 * ROWS)) * ROWS
    flat = jnp.pad(flat, (0, n_rows * 128 - n_small)).reshape(n_rows, 128)
    (small_all,) = run_stage(GatherStage([flat]), "gather_small_grads")
    small_all = small_all.reshape(N_DEV, n_rows, 128)
    total = sum_slots(small_all, "sum_small_grads").reshape(-1)
    tot, off = {}, 0
    for key, like in small.items():
        n = math.prod(like.shape)
        tot[key], off = total[off:off + n].reshape(like.shape), off + n
    gmod_all = small_all.reshape(N_DEV, n_rows * 128)[:, :nb * N_MOD * d].reshape(N_DEV * nb, N_MOD * d)
    g_a_re, g_a_im, g_b_re, g_b_im, g_log_dt = disc_vjp((tot["abr"], tot["abi"], tot["bbr"], tot["bbi"]))

    grads = {}
    grads["b_ada"] = sum_rows(tot["gmod"].reshape(nb, N_MOD * d))
    grads["g_ffn1"], grads["g_mix"], grads["g_ffn2"] = tot["g_ffn1"], tot["g_mix"], tot["g_ffn2"]
    grads["g_final"] = tot["g_final"].reshape(d)
    grads["d_skip"] = tot["d_skip"]
    grads["a_re"], grads["a_im"], grads["log_dt"] = g_a_re[None], g_a_im[None], g_log_dt[None]
    grads["b_re"], grads["b_im"] = g_b_re[None], g_b_im[None]
    grads["c_re"], grads["c_im"] = tot["c_re"][None], tot["c_im"][None]
    grads["conv_w"] = lax.dynamic_slice(tot["conv_w"], (0, me * conv_w.shape[2]), (CONV_K, conv_w.shape[2]))[None]

    delta, new_m, new_v = {}, {}, {}
    for name in big:
        gsum = sum_slots(received[name], "sum_" + name)
        if name == "w_glu":
            gsum = gsum.reshape(w_glu.shape[1], w_glu.shape[2])
        if name in transposed:
            gsum = gsum.T
        grads[name] = gsum[None]
        dl, mm, vn = adamw_update(weights[name][0], gsum, mom1[name][0], mom2[name][0], "adamw_" + name)
        delta[name], new_m[name], new_v[name] = dl[None], mm[None], vn[None]

    gmod_cols = lax.dynamic_slice(gmod_all, (0, me * ada_cols), (N_DEV * nb, ada_cols))
    g_wada, d_wada, m_wada, v_wada = ada_backward_update(c_all, gmod_cols, w_ada[0], m_w_ada[0], v_w_ada[0])
    grads["w_ada"], delta["w_ada"], new_m["w_ada"], new_v["w_ada"] = g_wada[None], d_wada[None], m_wada[None], v_wada[None]

    small_names = [n for n in names if n not in big and n != "w_ada"]

    def as2d(a):
        return a.reshape(-1, a.shape[-1])

    sw_, sg_, sm_, sv_ = ([as2d(src[n]) for n in small_names] for src in (weights, grads, mom1, mom2))
    sd, smo, svo = adamw_update_small(sw_, sg_, sm_, sv_)
    for n, dl, mm, vn in zip(small_names, sd, smo, svo):
        shape = weights[n].shape
        grads[n] = grads[n].reshape(shape)
        delta[n], new_m[n], new_v[n] = dl.reshape(shape), mm.reshape(shape), vn.reshape(shape)

    grad_x = dx0.reshape(nb, s, d)
    return (loss, grad_x, *[grads[n] for n in names], *[delta[n] for n in names],
            *[new_m[n] for n in names], *[new_v[n] for n in names])


def sum_rows(a):
    r, c = a.shape

    def body(a_ref, o_ref):
        acc = a_ref[0:1, :]
        for j in range(1, r):
            acc = acc + a_ref[j:j + 1, :]
        o_ref[...] = acc

    return pl.pallas_call(body, name="sum_rows", out_shape=jax.ShapeDtypeStruct((1, c), F32),
                          compiler_params=_params())(a)
```

```python
import functools
import math

import jax
import jax.numpy as jnp
from jax import lax
from jax.experimental import pallas as pl
from jax.experimental.pallas import tpu as pltpu

F32 = jnp.float32
BF16 = jnp.bfloat16
N_DEV = 8
N_MOD = 9
EPS = 1e-6
CONV_K = 3
ADAM_LR = 0.001
ADAM_B1 = 0.9
ADAM_B2 = 0.999
ADAM_EPS = 1e-08
ADAM_WD = 0.01
ADAM_STEP = 10
GELU_C0 = math.sqrt(2.0 / math.pi)
GELU_C1 = 0.044715
V7X_VMEM_LIMIT = 56 * 1024 * 1024
MESH_ID = pl.DeviceIdType.MESH
NT = (((1,), (1,)), ((), ()))
TN = (((0,), (0,)), ((), ()))


def _dot(a, b, dims=None):
    if dims is None:
        return jnp.dot(a, b, preferred_element_type=F32)
    return lax.dot_general(a, b, dims, preferred_element_type=F32)


def _params(sem=None, vmem=V7X_VMEM_LIMIT):
    return pltpu.CompilerParams(dimension_semantics=sem, vmem_limit_bytes=vmem)


def _full(shape):
    return pl.BlockSpec(shape, lambda *_: (0,) * len(shape))


def _const(shape):
    return pl.BlockSpec(shape, lambda *_: (0,) * len(shape), pipeline_mode=pl.Buffered(1))


def _tile(n, want):
    t = min(n, want)
    while n % t:
        t //= 2
    return t


class GatherStage:
    def __init__(self, shards):
        n = len(shards)
        self.inputs = list(shards)
        self.out_shape = [jax.ShapeDtypeStruct((N_DEV * s.shape[0], s.shape[1]), s.dtype) for s in shards]
        self.scratch = [pltpu.SemaphoreType.DMA((7 * n,)), pltpu.SemaphoreType.DMA((7 * n,)),
                        pltpu.SemaphoreType.DMA((n,))]

    def _plan(self, ins, outs, sems):
        send_sems, recv_sems, local_sems = sems
        n = len(ins)
        x, y, c = lax.axis_index("x"), lax.axis_index("y"), lax.axis_index("c")
        me, sibling = (x, y, c), (x, y, 1 - c)
        chips = [(1 - x, y), (x, 1 - y), (1 - x, 1 - y)]

        def rows(k, px, py, pc):
            r = ins[k].shape[0]
            return outs[k].at[pl.ds((4 * px + 2 * py + pc) * r, r), :]

        def copy(k, j, block, to, src=None):
            return pltpu.make_async_remote_copy(
                src_ref=rows(k, *block) if src is None else src, dst_ref=rows(k, *block),
                send_sem=send_sems.at[7 * k + j], recv_sem=recv_sems.at[7 * k + j],
                device_id=to, device_id_type=MESH_ID)

        mine = [pltpu.make_async_copy(ins[k], rows(k, *me), local_sems.at[k]) for k in range(n)]
        first = []
        for k in range(n):
            first.append(copy(k, 0, me, sibling, src=ins[k]))
            first += [copy(k, 1 + j, me, (*chip, c), src=ins[k]) for j, chip in enumerate(chips)]
        return n, c, me, sibling, chips, copy, mine, first

    def start(self, ins, outs, sems):
        *_, mine, first = self._plan(ins, outs, sems)
        for cp in mine + first:
            cp.start()

    def finish(self, ins, outs, sems):
        n, c, me, sibling, chips, copy, mine, first = self._plan(ins, outs, sems)
        passed = []
        for j, chip in enumerate(chips):
            for k in range(n):
                copy(k, 1 + j, (*chip, c), me).wait_recv()
                cp = copy(k, 4 + j, (*chip, c), sibling)
                cp.start()
                passed.append(cp)
        for k in range(n):
            copy(k, 0, sibling, me).wait_recv()
            for j, chip in enumerate(chips):
                copy(k, 4 + j, (*chip, 1 - c), me).wait_recv()
        for cp in first + passed:
            cp.wait_send()
        for cp in mine:
            cp.wait()


class ExchangeStage:
    def __init__(self, bufs):
        n = len(bufs)
        self.inputs = list(bufs)
        self.out_shape = [jax.ShapeDtypeStruct(b.shape, b.dtype) for b in bufs]
        self.scratch = [pltpu.SemaphoreType.DMA((7 * n,)), pltpu.SemaphoreType.DMA((7 * n,)),
                        pltpu.SemaphoreType.DMA((n,))]

    def _plan(self, ins, outs, sems):
        send_sems, recv_sems, local_sems = sems
        n = len(ins)
        x, y, c = lax.axis_index("x"), lax.axis_index("y"), lax.axis_index("c")
        me = 4 * x + 2 * y + c
        mine = [pltpu.make_async_copy(ins[k].at[me], outs[k].at[me], local_sems.at[k]) for k in range(n)]
        copies = []
        for mask in range(1, N_DEV):
            px, py, pc = x ^ (mask >> 2), y ^ ((mask >> 1) & 1), c ^ (mask & 1)
            for k in range(n):
                copies.append(pltpu.make_async_remote_copy(
                    src_ref=ins[k].at[4 * px + 2 * py + pc], dst_ref=outs[k].at[me],
                    send_sem=send_sems.at[7 * k + mask - 1], recv_sem=recv_sems.at[7 * k + mask - 1],
                    device_id=(px, py, pc), device_id_type=MESH_ID))
        return mine, copies

    def start(self, ins, outs, sems):
        mine, copies = self._plan(ins, outs, sems)
        for cp in mine + copies:
            cp.start()

    def finish(self, ins, outs, sems):
        mine, copies = self._plan(ins, outs, sems)
        for cp in copies:
            cp.wait_recv()
        for cp in copies:
            cp.wait_send()
        for cp in mine:
            cp.wait()


ANY_SPEC = pl.BlockSpec(memory_space=pl.ANY)


def run_stage(stage, name):
    ci, co = len(stage.inputs), len(stage.out_shape)

    def body(*refs):
        ins, outs, sems = refs[:ci], refs[ci:ci + co], refs[ci + co:]
        stage.start(ins, outs, sems)
        stage.finish(ins, outs, sems)

    return pl.pallas_call(body, name=name, out_shape=stage.out_shape, in_specs=[ANY_SPEC] * ci,
                          out_specs=[ANY_SPEC] * co, scratch_shapes=stage.scratch)(*stage.inputs)


def _call(body, *, name, grid, in_specs, out_specs, out_shape, args, scratch_shapes=(), carried=None):
    sem = ("arbitrary",) * len(grid)
    if carried is None:
        return pl.pallas_call(body, name=name, grid=grid, in_specs=list(in_specs), out_specs=list(out_specs),
                              out_shape=list(out_shape), scratch_shapes=list(scratch_shapes),
                              compiler_params=_params(sem))(*args), None
    ni, no, ns = len(in_specs), len(out_shape), len(scratch_shapes)
    ci, co = len(carried.inputs), len(carried.out_shape)

    def wrapped(*refs):
        ins, refs = refs[:ni], refs[ni:]
        cins, refs = refs[:ci], refs[ci:]
        outs, refs = refs[:no], refs[no:]
        couts, refs = refs[:co], refs[co:]
        scr, csems = refs[:ns], refs[ns:]
        ids = [pl.program_id(k) for k in range(len(grid))]
        first = functools.reduce(jnp.logical_and, [i == 0 for i in ids])
        last = functools.reduce(jnp.logical_and, [i == g - 1 for i, g in zip(ids, grid)])

        @pl.when(first)
        def _():
            carried.start(cins, couts, csems)

        body(*ins, *outs, *scr)

        @pl.when(last)
        def _():
            carried.finish(cins, couts, csems)

    res = pl.pallas_call(
        wrapped, name=name, grid=grid, in_specs=list(in_specs) + [ANY_SPEC] * ci,
        out_specs=list(out_specs) + [ANY_SPEC] * co, out_shape=list(out_shape) + carried.out_shape,
        scratch_shapes=list(scratch_shapes) + carried.scratch, compiler_params=_params(sem),
    )(*args, *carried.inputs)
    return res[:no], res[no:]


def _norm_mod(x, g, shift, scale):
    r = lax.rsqrt(jnp.mean(x * x, axis=-1, keepdims=True) + EPS)
    n = x * r
    return (n * g) * (1.0 + scale) + shift, n, r


def _norm_mod_bwd(dh, n, r, g, scale):
    dsh = jnp.sum(dh, axis=0, keepdims=True)
    dsc = jnp.sum(dh * (n * g), axis=0, keepdims=True)
    dg = jnp.sum(dh * (1.0 + scale) * n, axis=0, keepdims=True)
    dn = dh * ((1.0 + scale) * g)
    dx = r * (dn - n * jnp.mean(n * dn, axis=-1, keepdims=True))
    return dx, dsh, dsc, dg


def _mod_rows(mod_ref, sub):
    m = mod_ref[0]
    return m[3 * sub:3 * sub + 1], m[3 * sub + 1:3 * sub + 2], m[3 * sub + 2:3 * sub + 3]


def _gelu(x):
    t = jnp.tanh(GELU_C0 * (x + GELU_C1 * x * x * x))
    return 0.5 * x * (1.0 + t), t


def _gelu_grad(x, t):
    return 0.5 * (1.0 + t) + 0.5 * x * (1.0 - t * t) * (GELU_C0 * (1.0 + 3.0 * GELU_C1 * x * x))


def _accumulate(ref, val, first):
    @pl.when(first)
    def _():
        ref[...] = val

    @pl.when(jnp.logical_not(first))
    def _():
        ref[...] += val


def ada_forward(c_all, w_ada, b_ada_cols):
    def body(c_ref, w_ref, b_ref, o_ref):
        c = c_ref[...]
        cond = (c * jax.nn.sigmoid(c)).astype(BF16)
        o_ref[...] = _dot(cond, w_ref[...].astype(BF16)) + b_ref[...]

    nb, d = c_all.shape
    cols = w_ada.shape[1]
    tn = _tile(cols, 384)
    return pl.pallas_call(
        body, name="ada_forward", grid=(cols // tn,),
        out_shape=jax.ShapeDtypeStruct((nb, cols), F32),
        in_specs=[_full((nb, d)), pl.BlockSpec((d, tn), lambda j: (0, j)), pl.BlockSpec((1, tn), lambda j: (0, j))],
        out_specs=pl.BlockSpec((nb, tn), lambda j: (0, j)),
        compiler_params=_params(("arbitrary",)),
    )(c_all, w_ada, b_ada_cols)


def _adamw(w, g, m, v):
    m = ADAM_B1 * m + (1.0 - ADAM_B1) * g
    v = ADAM_B2 * v + (1.0 - ADAM_B2) * (g * g)
    m_hat = m / (1.0 - ADAM_B1 ** ADAM_STEP)
    v_hat = v / (1.0 - ADAM_B2 ** ADAM_STEP)
    delta = -ADAM_LR * (m_hat / (jnp.sqrt(v_hat) + ADAM_EPS) + ADAM_WD * w)
    return delta, m, v


def ada_backward_update(c_all, gmod_cols, w, m, v):
    def body(c_ref, g_ref, w_ref, m_ref, v_ref, go_ref, d_ref, mo_ref, vo_ref):
        c = c_ref[...]
        cond = (c * jax.nn.sigmoid(c)).astype(BF16)
        g = _dot(cond, g_ref[...].astype(BF16), TN)
        go_ref[...] = g
        d_ref[...], mo_ref[...], vo_ref[...] = _adamw(w_ref[...], g, m_ref[...], v_ref[...])

    nb, d = c_all.shape
    cols = w.shape[1]
    tn = _tile(cols, 128)
    col = pl.BlockSpec((d, tn), lambda j: (0, j))
    return pl.pallas_call(
        body, name="ada_backward_update", grid=(cols // tn,),
        out_shape=[jax.ShapeDtypeStruct(w.shape, F32)] * 4,
        in_specs=[_full((nb, d)), pl.BlockSpec((nb, tn), lambda j: (0, j)), col, col, col],
        out_specs=[col] * 4,
        compiler_params=_params(("arbitrary",)),
    )(c_all, gmod_cols, w, m, v)


def _row_spec(tm, width, tiles_per_seq):
    return pl.BlockSpec((tm, width), lambda b, i: (b * tiles_per_seq + i, 0))


def _mod_spec(d):
    return pl.BlockSpec((1, N_MOD, d), lambda b, i: (b, 0, 0))


def _col_spec(rows, tm, tiles_per_seq):
    return pl.BlockSpec((rows, tm), lambda b, i: (0, b * tiles_per_seq + i))


def _ffn_chunk(f):
    return f // 2 if f % 256 == 0 and f > 1536 else f


def ffn_forward(x, mod, g, w1t, w3t, w2t, sub, nb, name, carried=None):
    t, d = x.shape
    f = w1t.shape[0]
    s = t // nb
    tm = _tile(s, 512)
    fc = _ffn_chunk(f)

    def body(x_ref, mod_ref, g_ref, w1_ref, w3_ref, w2_ref, xo_ref, a_ref, b_ref, f_ref):
        xv = x_ref[...]
        sh, sc, gt = _mod_rows(mod_ref, sub)
        h, _, _ = _norm_mod(xv, g_ref[...], sh, sc)
        hb = h.astype(BF16)
        acc_t = jnp.zeros((d, tm), F32)
        for k in range(f // fc):
            rows = slice(k * fc, (k + 1) * fc)
            a = _dot(w1_ref[rows, :], hb, NT)
            b = _dot(w3_ref[rows, :], hb, NT)
            a_ref[rows, :] = a.astype(BF16)
            b_ref[rows, :] = b.astype(BF16)
            sw = (a * jax.nn.sigmoid(a)) * b
            acc_t = acc_t + _dot(w2_ref[:, rows], sw.astype(BF16))
        acc = acc_t.T
        f_ref[...] = acc.astype(BF16)
        xo_ref[...] = xv + (0.5 * gt) * acc

    tps = s // tm
    rd, cf = _row_spec(tm, d, tps), _col_spec(f, tm, tps)
    return _call(
        body, name=name, grid=(nb, tps), carried=carried,
        out_shape=[jax.ShapeDtypeStruct((t, d), F32), jax.ShapeDtypeStruct((f, t), BF16),
                   jax.ShapeDtypeStruct((f, t), BF16), jax.ShapeDtypeStruct((t, d), BF16)],
        in_specs=[rd, _mod_spec(d), _const((1, d)), _const((f, d)), _const((f, d)), _const((d, f))],
        out_specs=[rd, cf, cf, rd],
        args=(x, mod, g, w1t, w3t, w2t))


def ffn_backward_hidden(dxo, a_t, b_t, fo, mod, w2, sub, nb, name, carried=None):
    t, d = dxo.shape
    f = w2.shape[0]
    s = t // nb
    tm = _tile(s, 512)
    fc = _tile(f, 704) if f % 704 == 0 else _tile(f, 512)

    def body(dxo_ref, a_ref, b_ref, f_ref, mod_ref, w2_ref, da_ref, db_ref, s_ref, df_ref, dgt_ref):
        i = pl.program_id(1)
        dxo = dxo_ref[...]
        _, _, gt = _mod_rows(mod_ref, sub)
        dfb = ((0.5 * gt) * dxo).astype(BF16)
        df_ref[...] = dfb
        dgt = 0.5 * jnp.sum(dxo * f_ref[...].astype(F32), axis=0, keepdims=True)
        _accumulate(dgt_ref, dgt[None], i == 0)
        for k in range(f // fc):
            rows = slice(k * fc, (k + 1) * fc)
            ds = _dot(w2_ref[rows, :], dfb, NT)
            av = a_ref[rows, :].astype(F32)
            bv = b_ref[rows, :].astype(F32)
            sig = jax.nn.sigmoid(av)
            sl = av * sig
            da_ref[rows, :] = (ds * bv * (sig * (1.0 + av * (1.0 - sig)))).astype(BF16)
            db_ref[rows, :] = (ds * sl).astype(BF16)
            s_ref[rows, :] = (sl * bv).astype(BF16)

    tps = s // tm
    rd, cf = _row_spec(tm, d, tps), _col_spec(f, tm, tps)
    return _call(
        body, name=name, grid=(nb, tps), carried=carried,
        out_shape=[jax.ShapeDtypeStruct((f, t), BF16)] * 3
        + [jax.ShapeDtypeStruct((t, d), BF16), jax.ShapeDtypeStruct((nb, 1, d), F32)],
        in_specs=[rd, cf, cf, rd, _mod_spec(d), _const((f, d))],
        out_specs=[cf, cf, cf, rd, pl.BlockSpec((1, 1, d), lambda b, i: (b, 0, 0))],
        args=(dxo, a_t, b_t, fo, mod, w2))


def ffn_backward_input(dxo, x, da_t, db_t, mod, g, w1, w3, sub, nb, name, carried=None):
    t, d = x.shape
    f = w1.shape[1]
    s = t // nb
    tm = _tile(s, 512)

    def body(dxo_ref, x_ref, da_ref, db_ref, mod_ref, g_ref, w1_ref, w3_ref, dx_ref, h_ref, dmod_ref, dg_ref):
        bi, i = pl.program_id(0), pl.program_id(1)
        sh, sc, _ = _mod_rows(mod_ref, sub)
        gv = g_ref[...]
        h, n, r = _norm_mod(x_ref[...], gv, sh, sc)
        h_ref[...] = h.astype(BF16)
        dh_t = _dot(w1_ref[...], da_ref[...]) + _dot(w3_ref[...], db_ref[...])
        dxn, dsh, dsc, dg = _norm_mod_bwd(dh_t.T, n, r, gv, sc)
        dx_ref[...] = dxo_ref[...] + dxn
        _accumulate(dmod_ref, jnp.concatenate([dsh, dsc], axis=0)[None], i == 0)
        _accumulate(dg_ref, dg, jnp.logical_and(bi == 0, i == 0))

    tps = s // tm
    rd, cf = _row_spec(tm, d, tps), _col_spec(f, tm, tps)
    return _call(
        body, name=name, grid=(nb, tps), carried=carried,
        out_shape=[jax.ShapeDtypeStruct((t, d), F32), jax.ShapeDtypeStruct((t, d), BF16),
                   jax.ShapeDtypeStruct((nb, 2, d), F32), jax.ShapeDtypeStruct((1, d), F32)],
        in_specs=[rd, rd, cf, cf, _mod_spec(d), _const((1, d)), _const((d, f)), _const((d, f))],
        out_specs=[rd, rd, pl.BlockSpec((1, 2, d), lambda b, i: (b, 0, 0)), _full((1, d))],
        args=(dxo, x, da_t, db_t, mod, g, w1, w3))


def nn_matmul(lhs_t, rhs, name, carried=None):
    m, t = lhs_t.shape
    n = rhs.shape[1]
    tk = _tile(t, 2048)
    tmm = m if m <= 1536 else m // 2
    nk = t // tk

    def body(a_ref, b_ref, o_ref, acc_ref):
        k = pl.program_id(1)
        _accumulate(acc_ref, _dot(a_ref[...], b_ref[...]), k == 0)

        @pl.when(k == nk - 1)
        def _():
            o_ref[...] = acc_ref[...].astype(BF16)

    return _call(
        body, name=name, grid=(m // tmm, nk), carried=carried,
        out_shape=[jax.ShapeDtypeStruct((m, n), BF16)],
        in_specs=[pl.BlockSpec((tmm, tk), lambda j, k: (j, k)), pl.BlockSpec((tk, n), lambda j, k: (k, 0))],
        out_specs=[pl.BlockSpec((tmm, n), lambda j, k: (j, 0))],
        scratch_shapes=[pltpu.VMEM((tmm, n), F32)],
        args=(lhs_t, rhs))


def tn_matmul(lhs, rhs, name):
    t, m = lhs.shape
    n = rhs.shape[1]
    tk = _tile(t, 2048)
    nk = t // tk

    def body(a_ref, b_ref, o_ref, acc_ref):
        k = pl.program_id(0)
        _accumulate(acc_ref, _dot(a_ref[...], b_ref[...], TN), k == 0)

        @pl.when(k == nk - 1)
        def _():
            o_ref[...] = acc_ref[...].astype(BF16)

    return pl.pallas_call(
        body, name=name, grid=(nk,),
        out_shape=jax.ShapeDtypeStruct((m, n), BF16),
        in_specs=[pl.BlockSpec((tk, m), lambda k: (k, 0)), pl.BlockSpec((tk, n), lambda k: (k, 0))],
        out_specs=pl.BlockSpec((m, n), lambda k: (0, 0)),
        scratch_shapes=[pltpu.VMEM((m, n), F32)],
        compiler_params=_params(("arbitrary",)),
    )(lhs, rhs)


def mixer_proj_forward(x, mod, g, w_in_t, cw, sw, nb, carried=None):
    t, d = x.shape
    s = t // nb
    tm = _tile(s, 512)
    pieces = [(0, cw, "bf16"), (cw, cw, "bf16"), (2 * cw, cw, "bf16"), (3 * cw, sw, "f32"),
              (3 * cw + sw, d, "sig"), (3 * cw + sw + d, d, "sig")]

    def body(x_ref, mod_ref, g_ref, w_ref, *outs):
        h_ref = outs[-1]
        sh, sc, _ = _mod_rows(mod_ref, 1)
        h, _, _ = _norm_mod(x_ref[...], g_ref[...], sh, sc)
        hb = h.astype(BF16)
        h_ref[...] = hb
        for (off, width, kind), o_ref in zip(pieces, outs[:-1]):
            ck = _tile(width, 512)
            for j in range(width // ck):
                p = _dot(hb, w_ref[off + j * ck:off + (j + 1) * ck, :], NT)
                if kind == "sig":
                    p = jax.nn.sigmoid(p)
                o_ref[:, j * ck:(j + 1) * ck] = p.astype(o_ref.dtype)

    tps = s // tm
    widths = [(cw, BF16), (cw, BF16), (cw, BF16), (sw, F32), (d, BF16), (d, BF16), (d, BF16)]
    return _call(
        body, name="mixer_proj_forward", grid=(nb, tps), carried=carried,
        out_shape=[jax.ShapeDtypeStruct((t, w), dt) for w, dt in widths],
        in_specs=[_row_spec(tm, d, tps), _mod_spec(d), _const((1, d)), _const(w_in_t.shape)],
        out_specs=[_row_spec(tm, w, tps) for w, _ in widths],
        args=(x, mod, g, w_in_t))


def mixer_proj_backward(dgb, dgc, dv, du, dgla, dglb, dxo, x, mod, g, w_in_t, nb, carried=None):
    t, d = x.shape
    s = t // nb
    tm = _tile(s, 512)
    parts = [dgb, dgc, dv, du, dgla, dglb]
    offs = [0]
    for p in parts:
        offs.append(offs[-1] + p.shape[1])

    def body(*refs):
        p_refs = refs[:6]
        dxo_ref, x_ref, mod_ref, g_ref, w_ref, dx_ref, dmod_ref, dg_ref = refs[6:]
        bi, i = pl.program_id(0), pl.program_id(1)
        dh = jnp.zeros((tm, d), F32)
        for p_ref, off in zip(p_refs, offs):
            width = p_ref.shape[1]
            ck = _tile(width, 512)
            for j in range(width // ck):
                dh = dh + _dot(p_ref[:, j * ck:(j + 1) * ck], w_ref[off + j * ck:off + (j + 1) * ck, :])
        sh, sc, _ = _mod_rows(mod_ref, 1)
        gv = g_ref[...]
        _, n, r = _norm_mod(x_ref[...], gv, sh, sc)
        dxn, dsh, dsc, dg = _norm_mod_bwd(dh, n, r, gv, sc)
        dx_ref[...] = dxo_ref[...] + dxn
        _accumulate(dmod_ref, jnp.concatenate([dsh, dsc], axis=0)[None], i == 0)
        _accumulate(dg_ref, dg, jnp.logical_and(bi == 0, i == 0))

    tps = s // tm
    rd = _row_spec(tm, d, tps)
    return _call(
        body, name="mixer_proj_backward", grid=(nb, tps), carried=carried,
        out_shape=[jax.ShapeDtypeStruct((t, d), F32), jax.ShapeDtypeStruct((nb, 2, d), F32),
                   jax.ShapeDtypeStruct((1, d), F32)],
        in_specs=[_row_spec(tm, p.shape[1], tps) for p in parts]
        + [rd, rd, _mod_spec(d), _const((1, d)), _const(w_in_t.shape)],
        out_specs=[rd, pl.BlockSpec((1, 2, d), lambda b, i: (b, 0, 0)), _full((1, d))],
        args=(*parts, dxo, x, mod, g, w_in_t))


GROUPS_PER_BLOCK = 8
ROWS = 8


def _scan_rows(xr, xi, masks, shifts):
    for (mr, mi), sft in zip(masks, shifts):
        sr, si = pltpu.roll(xr, sft, 0), pltpu.roll(xi, sft, 0)
        xr, xi = xr + mr * sr - mi * si, xi + mr * si + mi * sr
    return xr, xi


def _ssm_scan_forward(xr_s, xi_s, row0, n_groups, lanes, ml_ref, ap_ref, carry):
    masks = [(ml_ref[d, 0, :, lanes], ml_ref[d, 1, :, lanes]) for d in range(3)]
    apr, api = ap_ref[0, :, lanes], ap_ref[1, :, lanes]

    def group(k, c):
        cr, ci = c
        rows = pl.ds(pl.multiple_of(row0 + k * ROWS, ROWS), ROWS)
        xr, xi = _scan_rows(xr_s[rows, lanes], xi_s[rows, lanes], masks, (1, 2, 4))
        xr, xi = xr + apr * cr - api * ci, xi + apr * ci + api * cr
        xr_s[rows, lanes] = xr
        xi_s[rows, lanes] = xi
        return jnp.broadcast_to(xr[ROWS - 1:ROWS], xr.shape), jnp.broadcast_to(xi[ROWS - 1:ROWS], xi.shape)

    return lax.fori_loop(0, n_groups, group, carry)


def ssm_forward(u, bd, cd, ml, ap, nb):
    t, w = u.shape
    s = t // nb
    tc = _tile(s, 256)
    nq, ub, lq = bd.shape[1], bd.shape[2], bd.shape[3]
    nl = nq * lq
    nch = s // tc

    def body(u_ref, bd_ref, cd_ref, ml_ref, ap_ref, y_ref, st_ref, xr_s, xi_s, car_s):
        i = pl.program_id(1)

        @pl.when(i == 0)
        def _():
            car_s[...] = jnp.zeros_like(car_s)

        st_ref[0] = car_s[...]
        ub16 = u_ref[...].astype(BF16)
        for q in range(nq):
            lanes = slice(q * lq, (q + 1) * lq)
            uq = ub16[:, q * ub:(q + 1) * ub]
            xr_s[:, lanes] = _dot(uq, bd_ref[0, q])
            xi_s[:, lanes] = _dot(uq, bd_ref[1, q])
        for q in range(nq):
            lanes = slice(q * lq, (q + 1) * lq)
            cr, ci = _ssm_scan_forward(xr_s, xi_s, 0, tc // ROWS, lanes, ml_ref, ap_ref,
                                       (car_s[0, :, lanes], car_s[1, :, lanes]))
            car_s[0, :, lanes] = cr
            car_s[1, :, lanes] = ci
        for q in range(nq):
            lanes = slice(q * lq, (q + 1) * lq)
            y_ref[:, q * ub:(q + 1) * ub] = (_dot(xr_s[:, lanes].astype(BF16), cd_ref[0, q])
                                             + _dot(xi_s[:, lanes].astype(BF16), cd_ref[1, q]))

    return pl.pallas_call(
        body, name="ssm_forward", grid=(nb, nch),
        out_shape=[jax.ShapeDtypeStruct((t, w), F32), jax.ShapeDtypeStruct((nb * nch, 2, ROWS, nl), F32)],
        in_specs=[pl.BlockSpec((tc, w), lambda b, i: (b * nch + i, 0)), _const(bd.shape), _const(cd.shape),
                  _const(ml.shape), _const(ap.shape)],
        out_specs=[pl.BlockSpec((tc, w), lambda b, i: (b * nch + i, 0)),
                   pl.BlockSpec((1, 2, ROWS, nl), lambda b, i: (b * nch + i, 0, 0, 0))],
        scratch_shapes=[pltpu.VMEM((tc, nl), F32), pltpu.VMEM((tc, nl), F32), pltpu.VMEM((2, ROWS, nl), F32)],
        compiler_params=_params(("arbitrary", "arbitrary")),
    )(u, bd, cd, ml, ap)


def ssm_backward(u, dy, st, bd, cd, ml, ap, mlb, apb, dskip, nb, carried=None):
    t, w = u.shape
    s = t // nb
    tc = _tile(s, 256)
    nq, ub, lq = bd.shape[1], bd.shape[2], bd.shape[3]
    nl = nq * lq
    nch = s // tc
    ng = tc // ROWS

    def body(u_ref, dy_ref, st_ref, bd_ref, cd_ref, ml_ref, ap_ref, mlb_ref, apb_ref, dsk_ref,
             du_ref, dab_ref, dbd_ref, dcd_ref, xr_s, xi_s, lr_s, li_s, car_s):
        bi, i = pl.program_id(0), pl.program_id(1)
        first = jnp.logical_and(bi == 0, i == 0)

        @pl.when(i == 0)
        def _():
            car_s[...] = jnp.zeros_like(car_s)

        @pl.when(first)
        def _():
            dab_ref[...] = jnp.zeros_like(dab_ref)
            dbd_ref[...] = jnp.zeros_like(dbd_ref)
            dcd_ref[...] = jnp.zeros_like(dcd_ref)

        uv = u_ref[...]
        dyv = dy_ref[...]
        ub16 = uv.astype(BF16)
        dyb16 = dyv.astype(BF16)
        xr_s[0:ROWS, :] = st_ref[0, 0]
        xi_s[0:ROWS, :] = st_ref[0, 1]
        for q in range(nq):
            lanes = slice(q * lq, (q + 1) * lq)
            uq = ub16[:, q * ub:(q + 1) * ub]
            dq = dyb16[:, q * ub:(q + 1) * ub]
            xr_s[ROWS:, lanes] = _dot(uq, bd_ref[0, q])
            xi_s[ROWS:, lanes] = _dot(uq, bd_ref[1, q])
            lr_s[:, lanes] = _dot(dq, cd_ref[0, q], NT)
            li_s[:, lanes] = _dot(dq, cd_ref[1, q], NT)
        for q in range(nq):
            lanes = slice(q * lq, (q + 1) * lq)
            _ssm_scan_forward(xr_s, xi_s, ROWS, ng, lanes, ml_ref, ap_ref,
                              (st_ref[0, 0, :, lanes], st_ref[0, 1, :, lanes]))
            masks = [(mlb_ref[d, 0, :, lanes], mlb_ref[d, 1, :, lanes]) for d in range(3)]
            apr, api = apb_ref[0, :, lanes], apb_ref[1, :, lanes]
            row_is_0 = lax.broadcasted_iota(jnp.int32, (ROWS, lq), 0) == 0

            def group(kk, c):
                cr, ci, accr, acci = c
                k = ng - 1 - kk
                rows = pl.ds(pl.multiple_of(k * ROWS, ROWS), ROWS)
                lr, li = _scan_rows(lr_s[rows, lanes], li_s[rows, lanes], masks, (7, 6, 4))
                lr, li = lr + apr * cr - api * ci, li + apr * ci + api * cr
                lr_s[rows, lanes] = lr
                li_s[rows, lanes] = li
                prev = pl.ds(pl.multiple_of(k * ROWS, ROWS), ROWS)
                cur = pl.ds(pl.multiple_of(k * ROWS + ROWS, ROWS), ROWS)
                pr = jnp.broadcast_to(xr_s[prev, lanes][ROWS - 1:ROWS], (ROWS, lq))
                pi = jnp.broadcast_to(xi_s[prev, lanes][ROWS - 1:ROWS], (ROWS, lq))
                xpr = jnp.where(row_is_0, pr, pltpu.roll(xr_s[cur, lanes], 1, 0))
                xpi = jnp.where(row_is_0, pi, pltpu.roll(xi_s[cur, lanes], 1, 0))
                accr = accr + lr * xpr + li * xpi
                acci = acci + li * xpr - lr * xpi
                return (jnp.broadcast_to(lr[0:1], lr.shape), jnp.broadcast_to(li[0:1], li.shape), accr, acci)

            zero = jnp.zeros((ROWS, lq), F32)
            cr, ci, accr, acci = lax.fori_loop(0, ng, group, (car_s[0, :, lanes], car_s[1, :, lanes], zero, zero))
            car_s[0, :, lanes] = cr
            car_s[1, :, lanes] = ci
            dab_ref[0, :, lanes] += accr
            dab_ref[1, :, lanes] += acci
        dsk = dsk_ref[...]
        for q in range(nq):
            lanes = slice(q * lq, (q + 1) * lq)
            cols = slice(q * ub, (q + 1) * ub)
            lrb, lib = lr_s[:, lanes].astype(BF16), li_s[:, lanes].astype(BF16)
            uq, dq = ub16[:, cols], dyb16[:, cols]
            du = _dot(lrb, bd_ref[0, q], NT) + _dot(lib, bd_ref[1, q], NT) + dsk[:, cols] * dyv[:, cols]
            du_ref[:, cols] = du.astype(BF16)
            dbd_ref[0, q] += _dot(uq, lrb, TN)
            dbd_ref[1, q] += _dot(uq, lib, TN)
            dcd_ref[0, q] += _dot(xr_s[ROWS:, lanes].astype(BF16), dq, TN)
            dcd_ref[1, q] += _dot(xi_s[ROWS:, lanes].astype(BF16), dq, TN)

    rev = lambda b, i: (b * nch + nch - 1 - i, 0)
    return _call(
        body, name="ssm_backward", grid=(nb, nch), carried=carried,
        out_shape=[jax.ShapeDtypeStruct((t, w), BF16), jax.ShapeDtypeStruct((2, ROWS, nl), F32),
                   jax.ShapeDtypeStruct(bd.shape, F32), jax.ShapeDtypeStruct(cd.shape, F32)],
        in_specs=[pl.BlockSpec((tc, w), rev), pl.BlockSpec((tc, w), rev),
                  pl.BlockSpec((1, 2, ROWS, nl), lambda b, i: (b * nch + nch - 1 - i, 0, 0, 0)),
                  _const(bd.shape), _const(cd.shape), _const(ml.shape), _const(ap.shape), _const(mlb.shape),
                  _const(apb.shape), _const((1, w))],
        out_specs=[pl.BlockSpec((tc, w), rev), _full((2, ROWS, nl)), _full(bd.shape), _full(cd.shape)],
        scratch_shapes=[pltpu.VMEM((tc + ROWS, nl), F32), pltpu.VMEM((tc + ROWS, nl), F32),
                        pltpu.VMEM((tc, nl), F32), pltpu.VMEM((tc, nl), F32), pltpu.VMEM((2, ROWS, nl), F32)],
        args=(u, dy, st, bd, cd, ml, ap, mlb, apb, dskip))


def ssm_discretise(a_re, a_im, b_re, b_im, log_dt):
    dt = jnp.exp(log_dt)[:, None]
    er = jnp.exp(a_re * dt)
    abr, abi = er * jnp.cos(a_im * dt), er * jnp.sin(a_im * dt)
    den = a_re * a_re + a_im * a_im
    nr, ni = abr - 1.0, abi
    fr = ((nr * a_re + ni * a_im) / den)[..., None]
    fi = ((ni * a_re - nr * a_im) / den)[..., None]
    return abr, abi, fr * b_re - fi * b_im, fr * b_im + fi * b_re


def _complex_powers(abr, abi, n):
    pr, pi = [abr.reshape(-1)], [abi.reshape(-1)]
    for _ in range(n - 1):
        pr, pi = pr + [pr[-1] * pr[0] - pi[-1] * pi[0]], pi + [pr[-1] * pi[0] + pi[-1] * pr[0]]
    return jnp.stack(pr), jnp.stack(pi)


def ssm_tables(abr, abi, bbr, bbi, c_re, c_im):
    g, p, h = bbr.shape
    nq = g // GROUPS_PER_BLOCK
    pw_r, pw_i = _complex_powers(abr, abi, ROWS)
    row = jnp.arange(ROWS)[:, None]
    ml, mlb = [], []
    for d in (1, 2, 4):
        fr, fi = pw_r[d - 1][None], pw_i[d - 1][None]
        ml.append(jnp.stack([jnp.where(row >= d, fr, 0.0), jnp.where(row >= d, fi, 0.0)]))
        mlb.append(jnp.stack([jnp.where(row + d < ROWS, fr, 0.0), jnp.where(row + d < ROWS, -fi, 0.0)]))
    ap = jnp.stack([pw_r, pw_i])
    apb = jnp.stack([pw_r[::-1], -pw_i[::-1]])
    eye = jnp.eye(GROUPS_PER_BLOCK, dtype=F32)

    def block_diag_in(bb):
        bq = bb.reshape(nq, GROUPS_PER_BLOCK, p, h)
        return jnp.einsum("qaph,ab->qahbp", bq, eye).reshape(nq, GROUPS_PER_BLOCK * h, GROUPS_PER_BLOCK * p)

    def block_diag_out(cc):
        cq = cc.reshape(nq, GROUPS_PER_BLOCK, h, p)
        return jnp.einsum("qahp,ab->qapbh", cq, eye).reshape(nq, GROUPS_PER_BLOCK * p, GROUPS_PER_BLOCK * h)

    bd = jnp.stack([block_diag_in(bbr), block_diag_in(bbi)]).astype(BF16)
    cd = jnp.stack([block_diag_out(c_re), block_diag_out(-c_im)]).astype(BF16)
    return bd, cd, jnp.stack(ml), ap, jnp.stack(mlb), apb


def ssm_table_grads(dab, dbd, dcd, g, p, h):
    nq = g // GROUPS_PER_BLOCK
    dabr, dabi = dab[0].sum(0).reshape(g, p), dab[1].sum(0).reshape(g, p)
    b5 = dbd.reshape(2, nq, GROUPS_PER_BLOCK, h, GROUPS_PER_BLOCK, p)
    dbb = jnp.einsum("rqahap->rqaph", b5).reshape(2, g, p, h)
    c5 = dcd.reshape(2, nq, GROUPS_PER_BLOCK, p, GROUPS_PER_BLOCK, h)
    dcc = jnp.einsum("rqapah->rqahp", c5).reshape(2, g, h, p)
    return dabr, dabi, dbb[0], dbb[1], dcc[0], -dcc[1]


HALO = 16


def _conv_inputs(gc_ref, v_ref, gch_ref, vh_ref, cv_s, i, tm):
    cv = gc_ref[...].astype(F32) * v_ref[...].astype(F32)
    halo = gch_ref[...].astype(F32) * vh_ref[...].astype(F32)
    cv_s[0:HALO, :] = jnp.where(i == 0, 0.0, halo)
    cv_s[HALO:, :] = cv
    return cv, cv_s[HALO - 1:HALO - 1 + tm, :], cv_s[HALO - 2:HALO - 2 + tm, :]


def _halo_spec(tm, width, tiles_per_seq):
    per = tm // HALO
    return pl.BlockSpec((HALO, width), lambda b, i: (jnp.maximum((b * tiles_per_seq + i) * per - 1, 0), 0))


def mixer_merge_forward(x, gb, gc, v, sga, sgb, yssm, u, mod, conv_w, dskip, wco, wglu, wso_t, wout, nb):
    t, d = x.shape
    cw, sw = gb.shape[1], u.shape[1]
    s = t // nb
    tm = _tile(s, 256)
    tps = s // tm

    def body(x_ref, gb_ref, gc_ref, v_ref, gch_ref, vh_ref, sga_ref, sgb_ref, ys_ref, u_ref, mod_ref, cw_ref,
             dsk_ref, wco_ref, wglu_ref, wso_ref, wout_ref, xo_ref, ya_ref, yb_ref, mix_ref, cv_s):
        i = pl.program_id(1)
        cv, cv1, cv2 = _conv_inputs(gc_ref, v_ref, gch_ref, vh_ref, cv_s, i, tm)
        w = cw_ref[...]
        conv = w[0:1] * cv2 + w[1:2] * cv1 + w[2:3] * cv
        ya = _dot((gb_ref[...].astype(F32) * conv).astype(BF16), wco_ref[...])
        s0 = ys_ref[...] + dsk_ref[...] * u_ref[...]
        s1, _ = _gelu(s0)
        z = _dot(s1.astype(BF16), wglu_ref[...])
        s2 = s1 * jax.nn.sigmoid(z)
        yb = _dot(s2.astype(BF16), wso_ref[...], NT)
        merged = sga_ref[...].astype(F32) * ya + sgb_ref[...].astype(F32) * yb
        mix = _dot(merged.astype(BF16), wout_ref[...])
        _, _, gt = _mod_rows(mod_ref, 1)
        xo_ref[...] = x_ref[...] + gt * mix
        ya_ref[...] = ya.astype(BF16)
        yb_ref[...] = yb.astype(BF16)
        mix_ref[...] = mix.astype(BF16)

    rd, rc, rw = _row_spec(tm, d, tps), _row_spec(tm, cw, tps), _row_spec(tm, sw, tps)
    hc = _halo_spec(tm, cw, tps)
    return pl.pallas_call(
        body, name="mixer_merge_forward", grid=(nb, tps),
        out_shape=[jax.ShapeDtypeStruct((t, d), F32)] + [jax.ShapeDtypeStruct((t, d), BF16)] * 3,
        in_specs=[rd, rc, rc, rc, hc, hc, rd, rd, rw, rw, _mod_spec(d), _const(conv_w.shape), _const((1, sw)),
                  _const(wco.shape), _const(wglu.shape), _const(wso_t.shape), _const(wout.shape)],
        out_specs=[rd, rd, rd, rd],
        scratch_shapes=[pltpu.VMEM((tm + HALO, cw), F32)],
        compiler_params=_params(("arbitrary", "arbitrary")),
    )(x, gb, gc, v, gc, v, sga, sgb, yssm, u, mod, conv_w, dskip, wco, wglu, wso_t, wout)


def mixer_merge_backward(dxo, mix, ya, yb, gb, gc, v, sga, sgb, yssm, u, mod, conv_w, dskip,
                         wco, wglu, wso_t, wout, nb, carried=None):
    t, d = dxo.shape
    cw, sw = gb.shape[1], u.shape[1]
    s = t // nb
    tm = _tile(s, 256)
    tps = s // tm

    def body(dxo_ref, mix_ref, ya_ref, yb_ref, gb_ref, gc_ref, v_ref, gch_ref, vh_ref, sga_ref, sgb_ref, ys_ref,
             u_ref, mod_ref, cw_ref, dsk_ref, wco_ref, wglu_ref, wso_ref, wout_ref,
             dgla_ref, dglb_ref, dgb_ref, dconv_ref, ds0_ref, dgt_ref, ddsk_ref, dwout_ref, dwco_ref, dwso_ref,
             dwglu_ref, cv_s):
        bi, i = pl.program_id(0), pl.program_id(1)
        first = jnp.logical_and(bi == 0, i == 0)
        dxo = dxo_ref[...]
        _, _, gt = _mod_rows(mod_ref, 1)
        dmix = (gt * dxo).astype(BF16)
        _accumulate(dgt_ref, jnp.sum(dxo * mix_ref[...].astype(F32), axis=0, keepdims=True)[None], i == 0)
        ya, yb = ya_ref[...].astype(F32), yb_ref[...].astype(F32)
        sga, sgb = sga_ref[...].astype(F32), sgb_ref[...].astype(F32)
        merged = (sga * ya + sgb * yb).astype(BF16)
        _accumulate(dwout_ref, _dot(merged, dmix, TN), first)
        dmerged = _dot(dmix, wout_ref[...], NT)
        dgla_ref[...] = (dmerged * ya * sga * (1.0 - sga)).astype(BF16)
        dglb_ref[...] = (dmerged * yb * sgb * (1.0 - sgb)).astype(BF16)
        dya = (dmerged * sga).astype(BF16)
        dyb = (dmerged * sgb).astype(BF16)
        cv, cv1, cv2 = _conv_inputs(gc_ref, v_ref, gch_ref, vh_ref, cv_s, i, tm)
        w = cw_ref[...]
        conv = w[0:1] * cv2 + w[1:2] * cv1 + w[2:3] * cv
        gbv = gb_ref[...].astype(F32)
        _accumulate(dwco_ref, _dot((gbv * conv).astype(BF16), dya, TN), first)
        dya_in = _dot(dya, wco_ref[...], NT)
        dgb_ref[...] = (dya_in * conv).astype(BF16)
        dconv_ref[...] = dya_in * gbv
        uv = u_ref[...]
        s0 = ys_ref[...] + dsk_ref[...] * uv
        s1, th = _gelu(s0)
        s1b = s1.astype(BF16)
        sz = jax.nn.sigmoid(_dot(s1b, wglu_ref[...]))
        s2b = (s1 * sz).astype(BF16)
        _accumulate(dwso_ref, _dot(dyb, s2b, TN), first)
        ds2 = _dot(dyb, wso_ref[...])
        dz = (ds2 * s1 * sz * (1.0 - sz)).astype(BF16)
        _accumulate(dwglu_ref, _dot(s1b, dz, TN), first)
        ds1 = ds2 * sz + _dot(dz, wglu_ref[...], NT)
        ds0 = ds1 * _gelu_grad(s0, th)
        ds0_ref[...] = ds0
        _accumulate(ddsk_ref, jnp.sum(ds0 * uv, axis=0, keepdims=True), first)

    rd, rc, rw = _row_spec(tm, d, tps), _row_spec(tm, cw, tps), _row_spec(tm, sw, tps)
    hc = _halo_spec(tm, cw, tps)
    return _call(
        body, name="mixer_merge_backward", grid=(nb, tps), carried=carried,
        out_shape=[jax.ShapeDtypeStruct((t, d), BF16), jax.ShapeDtypeStruct((t, d), BF16),
                   jax.ShapeDtypeStruct((t, cw), BF16), jax.ShapeDtypeStruct((t, cw), F32),
                   jax.ShapeDtypeStruct((t, sw), F32), jax.ShapeDtypeStruct((nb, 1, d), F32),
                   jax.ShapeDtypeStruct((1, sw), F32), jax.ShapeDtypeStruct(wout.shape, F32),
                   jax.ShapeDtypeStruct(wco.shape, F32), jax.ShapeDtypeStruct(wso_t.shape, F32),
                   jax.ShapeDtypeStruct(wglu.shape, F32)],
        in_specs=[rd, rd, rd, rd, rc, rc, rc, hc, hc, rd, rd, rw, rw, _mod_spec(d), _const(conv_w.shape),
                  _const((1, sw)), _const(wco.shape), _const(wglu.shape), _const(wso_t.shape), _const(wout.shape)],
        out_specs=[rd, rd, rc, rc, rw, pl.BlockSpec((1, 1, d), lambda b, i: (b, 0, 0)), _full((1, sw)),
                   _full(wout.shape), _full(wco.shape), _full(wso_t.shape), _full(wglu.shape)],
        scratch_shapes=[pltpu.VMEM((tm + HALO, cw), F32)],
        args=(dxo, mix, ya, yb, gb, gc, v, gc, v, sga, sgb, yssm, u, mod, conv_w, dskip, wco, wglu, wso_t, wout))


def conv_backward(dconv, gc, v, conv_w, nb):
    t, cw = dconv.shape
    s = t // nb
    tm = _tile(s, 512)
    tps = s // tm
    per = tm // ROWS

    def body(dc_ref, dcn_ref, gc_ref, v_ref, gch_ref, vh_ref, cw_ref, dgc_ref, dv_ref, dw_ref, cv_s, dc_s):
        bi, i = pl.program_id(0), pl.program_id(1)
        cv, cv1, cv2 = _conv_inputs(gc_ref, v_ref, gch_ref, vh_ref, cv_s, i, tm)
        dc = dc_ref[...]
        dc_s[0:tm, :] = dc
        dc_s[tm:, :] = jnp.where(i == tps - 1, 0.0, dcn_ref[...])
        w = cw_ref[...]
        dcv = w[2:3] * dc + w[1:2] * dc_s[1:1 + tm, :] + w[0:1] * dc_s[2:2 + tm, :]
        dgc_ref[...] = (dcv * v_ref[...].astype(F32)).astype(BF16)
        dv_ref[...] = (dcv * gc_ref[...].astype(F32)).astype(BF16)
        dw = jnp.concatenate([jnp.sum(dc * cv2, axis=0, keepdims=True), jnp.sum(dc * cv1, axis=0, keepdims=True),
                              jnp.sum(dc * cv, axis=0, keepdims=True)], axis=0)
        _accumulate(dw_ref, dw, jnp.logical_and(bi == 0, i == 0))

    rc = _row_spec(tm, cw, tps)
    nxt = pl.BlockSpec((ROWS, cw), lambda b, i: (jnp.minimum((b * tps + i + 1) * per, nb * tps * per - 1), 0))
    hc = _halo_spec(tm, cw, tps)
    return pl.pallas_call(
        body, name="conv_backward", grid=(nb, tps),
        out_shape=[jax.ShapeDtypeStruct((t, cw), BF16), jax.ShapeDtypeStruct((t, cw), BF16),
                   jax.ShapeDtypeStruct(conv_w.shape, F32)],
        in_specs=[rc, nxt, rc, rc, hc, hc, _full(conv_w.shape)],
        out_specs=[rc, rc, _full(conv_w.shape)],
        scratch_shapes=[pltpu.VMEM((tm + HALO, cw), F32), pltpu.VMEM((tm + ROWS, cw), F32)],
        compiler_params=_params(("arbitrary", "arbitrary")),
    )(dconv, dconv, gc, v, gc, v, conv_w)


def loss_forward_backward(x, target, g):
    t, d = x.shape
    tm = _tile(t, 512)

    def body(x_ref, t_ref, g_ref, l_ref, dx_ref, dg_ref):
        i = pl.program_id(0)
        xv = x_ref[...]
        gv = g_ref[...]
        r = lax.rsqrt(jnp.mean(xv * xv, axis=-1, keepdims=True) + EPS)
        n = xv * r
        err = n * gv - t_ref[...]
        l_ref[...] = jnp.full(l_ref.shape, 0.5 * jnp.sum(jnp.mean(err * err, axis=-1)), F32)
        dy = err * (1.0 / d)
        dn = dy * gv
        dx_ref[...] = r * (dn - n * jnp.mean(n * dn, axis=-1, keepdims=True))
        _accumulate(dg_ref, jnp.sum(dy * n, axis=0, keepdims=True), i == 0)

    row = pl.BlockSpec((tm, d), lambda i: (i, 0))
    return pl.pallas_call(
        body, name="loss_forward_backward", grid=(t // tm,),
        out_shape=[jax.ShapeDtypeStruct((t // tm, 1, 128), F32), jax.ShapeDtypeStruct((t, d), F32),
                   jax.ShapeDtypeStruct((1, d), F32)],
        in_specs=[row, row, _full((1, d))],
        out_specs=[pl.BlockSpec((1, 1, 128), lambda i: (i, 0, 0)), row, _full((1, d))],
        compiler_params=_params(("arbitrary",)),
    )(x, target, g)


def sum_slots(slots, name):
    _, r, c = slots.shape
    tr = _tile(r, 352) if r % 352 == 0 else _tile(r, 256)

    def body(s_ref, o_ref):
        acc = s_ref[0].astype(F32)
        for j in range(1, N_DEV):
            acc = acc + s_ref[j].astype(F32)
        o_ref[...] = acc

    return pl.pallas_call(
        body, name=name, grid=(r // tr,),
        out_shape=jax.ShapeDtypeStruct((r, c), F32),
        in_specs=[pl.BlockSpec((N_DEV, tr, c), lambda i: (0, i, 0))],
        out_specs=pl.BlockSpec((tr, c), lambda i: (i, 0)),
        compiler_params=_params(("arbitrary",)),
    )(slots)


def adamw_update(w, g, m, v, name):
    r, c = w.shape
    tr = _tile(r, 256) if r % 8 == 0 else r

    def body(w_ref, g_ref, m_ref, v_ref, d_ref, mo_ref, vo_ref):
        d_ref[...], mo_ref[...], vo_ref[...] = _adamw(w_ref[...], g_ref[...], m_ref[...], v_ref[...])

    spec = pl.BlockSpec((tr, c), lambda i: (i, 0))
    return pl.pallas_call(
        body, name=name, grid=(r // tr,),
        out_shape=[jax.ShapeDtypeStruct((r, c), F32)] * 3,
        in_specs=[spec] * 4, out_specs=[spec] * 3,
        compiler_params=_params(("arbitrary",)),
    )(w, g, m, v)


def adamw_update_small(ws, gs, ms, vs):
    n = len(ws)

    def body(*refs):
        w_r, g_r, m_r, v_r = refs[:n], refs[n:2 * n], refs[2 * n:3 * n], refs[3 * n:4 * n]
        d_r, mo_r, vo_r = refs[4 * n:5 * n], refs[5 * n:6 * n], refs[6 * n:7 * n]
        for k in range(n):
            d_r[k][...], mo_r[k][...], vo_r[k][...] = _adamw(w_r[k][...], g_r[k][...], m_r[k][...], v_r[k][...])

    shapes = [jax.ShapeDtypeStruct(w.shape, F32) for w in ws]
    out = pl.pallas_call(body, name="adamw_update_small", out_shape=shapes * 3,
                         compiler_params=_params())(*ws, *gs, *ms, *vs)
    return out[:n], out[n:2 * n], out[2 * n:]


def _slots(grad_t):
    return grad_t.reshape(N_DEV, grad_t.shape[0] // N_DEV, grad_t.shape[1])


def kernel(x, c, w_ada, b_ada, g_ffn1, w1_a, w3_a, w2_a, g_mix, w_in, conv_w, w_conv_out, a_re, a_im, b_re, b_im, c_re, c_im, log_dt, d_skip, w_glu, w_ssm_out, w_out, g_ffn2, w1_b, w3_b, w2_b, g_final, loss_target, m_w_ada, m_b_ada, m_g_ffn1, m_w1_a, m_w3_a, m_w2_a, m_g_mix, m_w_in, m_conv_w, m_w_conv_out, m_a_re, m_a_im, m_b_re, m_b_im, m_c_re, m_c_im, m_log_dt, m_d_skip, m_w_glu, m_w_ssm_out, m_w_out, m_g_ffn2, m_w1_b, m_w3_b, m_w2_b, m_g_final, v_w_ada, v_b_ada, v_g_ffn1, v_w1_a, v_w3_a, v_w2_a, v_g_mix, v_w_in, v_conv_w, v_w_conv_out, v_a_re, v_a_im, v_b_re, v_b_im, v_c_re, v_c_im, v_log_dt, v_d_skip, v_w_glu, v_w_ssm_out, v_w_out, v_g_ffn2, v_w1_b, v_w3_b, v_w2_b, v_g_final):
    nb, s, d = x.shape
    t = nb * s
    me = 4 * lax.axis_index("x") + 2 * lax.axis_index("y") + lax.axis_index("c")
    g_n, p_n, h_n = b_re.shape[1:]
    cw_n = w_conv_out.shape[1] * N_DEV
    sw_n = w_ssm_out.shape[1]
    glu_fold = d // w_glu.shape[2]

    weights = dict(w_ada=w_ada, b_ada=b_ada, g_ffn1=g_ffn1, w1_a=w1_a, w3_a=w3_a, w2_a=w2_a, g_mix=g_mix, w_in=w_in,
                   conv_w=conv_w, w_conv_out=w_conv_out, a_re=a_re, a_im=a_im, b_re=b_re, b_im=b_im, c_re=c_re,
                   c_im=c_im, log_dt=log_dt, d_skip=d_skip, w_glu=w_glu, w_ssm_out=w_ssm_out, w_out=w_out,
                   g_ffn2=g_ffn2, w1_b=w1_b, w3_b=w3_b, w2_b=w2_b, g_final=g_final)
    mom1 = dict(w_ada=m_w_ada, b_ada=m_b_ada, g_ffn1=m_g_ffn1, w1_a=m_w1_a, w3_a=m_w3_a, w2_a=m_w2_a, g_mix=m_g_mix,
                w_in=m_w_in, conv_w=m_conv_w, w_conv_out=m_w_conv_out, a_re=m_a_re, a_im=m_a_im, b_re=m_b_re,
                b_im=m_b_im, c_re=m_c_re, c_im=m_c_im, log_dt=m_log_dt, d_skip=m_d_skip, w_glu=m_w_glu,
                w_ssm_out=m_w_ssm_out, w_out=m_w_out, g_ffn2=m_g_ffn2, w1_b=m_w1_b, w3_b=m_w3_b, w2_b=m_w2_b,
                g_final=m_g_final)
    mom2 = dict(w_ada=v_w_ada, b_ada=v_b_ada, g_ffn1=v_g_ffn1, w1_a=v_w1_a, w3_a=v_w3_a, w2_a=v_w2_a, g_mix=v_g_mix,
                w_in=v_w_in, conv_w=v_conv_w, w_conv_out=v_w_conv_out, a_re=v_a_re, a_im=v_a_im, b_re=v_b_re,
                b_im=v_b_im, c_re=v_c_re, c_im=v_c_im, log_dt=v_log_dt, d_skip=v_d_skip, w_glu=v_w_glu,
                w_ssm_out=v_w_ssm_out, w_out=v_w_out, g_ffn2=v_g_ffn2, w1_b=v_w1_b, w3_b=v_w3_b, w2_b=v_w2_b,
                g_final=v_g_final)
    names = list(weights)
    transposed = ("w1_a", "w3_a", "w_in", "w_ssm_out", "w1_b", "w3_b")
    groups = dict(ffn_a=("w1_a", "w3_a", "w2_a"), mixer=("w_in", "w_conv_out", "w_glu", "w_ssm_out", "w_out"),
                  ffn_b=("w1_b", "w3_b", "w2_b"))
    big = groups["ffn_a"] + groups["mixer"] + groups["ffn_b"]

    pad_rows = lambda a: jnp.pad(a, ((0, -a.shape[0] % ROWS), (0, 0)))
    c_all, conv_all = run_stage(GatherStage([pad_rows(c), pad_rows(conv_w[0])]), "gather_cond")
    c_all = c_all.reshape(N_DEV, -1, d)[:, :nb].reshape(N_DEV * nb, d)
    conv_full = conv_all.reshape(N_DEV, ROWS, -1)[:, :CONV_K].transpose(1, 0, 2).reshape(CONV_K, cw_n)
    ada_cols = w_ada.shape[2]
    b_cols = lax.dynamic_slice(b_ada, (0, me * ada_cols), (1, ada_cols))
    mod_cols = ada_forward(c_all, w_ada[0], b_cols)
    (mod_all,) = run_stage(GatherStage([mod_cols]), "gather_mod")
    mod_mine = lax.dynamic_slice(mod_all.reshape(N_DEV, N_DEV * nb, ada_cols), (0, me * nb, 0), (N_DEV, nb, ada_cols))
    mod = mod_mine.transpose(1, 0, 2).reshape(nb, N_MOD, d)

    def shard_rows(name):
        w = weights[name][0]
        if name in transposed:
            w = w.T
        if name == "w_glu":
            w = w.reshape(w.shape[0] // glu_fold, d)
        return w.astype(BF16)

    def gather_stage(group):
        return GatherStage([shard_rows(n) for n in groups[group]])

    gw = {}

    def keep_weights(group, outs):
        for n, w in zip(groups[group], outs):
            gw[n] = w.reshape(sw_n, sw_n) if n == "w_glu" else w

    disc_in = (a_re[0], a_im[0], b_re[0], b_im[0], log_dt[0])
    (abr, abi, bbr, bbi), disc_vjp = jax.vjp(ssm_discretise, *disc_in)
    bd, cd, ml, ap, mlb, apb = ssm_tables(abr, abi, bbr, bbi, c_re[0], c_im[0])

    x0 = x.reshape(t, d)
    keep_weights("ffn_a", run_stage(gather_stage("ffn_a"), "gather_ffn_a"))
    (x1, a1, b1, f1), got = ffn_forward(x0, mod, g_ffn1, gw["w1_a"], gw["w3_a"], gw["w2_a"].T, 0, nb,
                                        "ffn_a_forward", carried=gather_stage("mixer"))
    keep_weights("mixer", got)
    (gb, gc, vv, u, sga, sgb, h2), got = mixer_proj_forward(x1, mod, g_mix, gw["w_in"], cw_n, sw_n, nb,
                                                            carried=gather_stage("ffn_b"))
    keep_weights("ffn_b", got)
    yssm, st = ssm_forward(u, bd, cd, ml, ap, nb)
    x2, ya, yb, mix = mixer_merge_forward(x1, gb, gc, vv, sga, sgb, yssm, u, mod, conv_full, d_skip,
                                          gw["w_conv_out"], gw["w_glu"], gw["w_ssm_out"], gw["w_out"], nb)
    (x3, a3, b3, f3), _ = ffn_forward(x2, mod, g_ffn2, gw["w1_b"], gw["w3_b"], gw["w2_b"].T, 2, nb, "ffn_b_forward")
    loss_parts, dx3, dg_final = loss_forward_backward(x3, loss_target.reshape(t, d), g_final.reshape(1, d))
    loss = lax.psum(jnp.sum(loss_parts[:, 0, 0]), ("x", "y", "c"))

    part, received = {}, {}

    def exchange_stage(ns):
        return ExchangeStage([_slots(part[n]) for n in ns])

    (da3, db3, sw3, df3, dgt3), _ = ffn_backward_hidden(dx3, a3, b3, f3, mod, gw["w2_b"], 2, nb,
                                                        "ffn_b_backward_hidden")
    (dx2, h3, dmod3, dg_ffn2), _ = ffn_backward_input(dx3, x2, da3, db3, mod, g_ffn2, gw["w1_b"].T, gw["w3_b"].T,
                                                      2, nb, "ffn_b_backward_input")
    (part["w1_b"],), _ = nn_matmul(da3, h3, "grad_w1_b")
    (part["w3_b"],), _ = nn_matmul(db3, h3, "grad_w3_b")
    (part["w2_b"],), _ = nn_matmul(sw3, df3, "grad_w2_b")
    (dgla, dglb, dgb, dconv, ds0, dgt2, dd_skip, dw_out, dw_co, dw_so_t, dw_glu), got = mixer_merge_backward(
        dx2, mix, ya, yb, gb, gc, vv, sga, sgb, yssm, u, mod, conv_full, d_skip,
        gw["w_conv_out"], gw["w_glu"], gw["w_ssm_out"], gw["w_out"], nb, carried=exchange_stage(groups["ffn_b"]))
    received.update(zip(groups["ffn_b"], got))
    part["w_out"] = dw_out.astype(BF16)
    part["w_conv_out"] = dw_co.astype(BF16)
    part["w_ssm_out"] = dw_so_t.astype(BF16)
    part["w_glu"] = dw_glu.reshape(sw_n // glu_fold, d).astype(BF16)
    (du, dab, dbd, dcd), got = ssm_backward(u, ds0, st, bd, cd, ml, ap, mlb, apb, d_skip, nb,
                                            carried=exchange_stage(groups["mixer"][1:]))
    received.update(zip(groups["mixer"][1:], got))
    dgc, dvv, dconv_w = conv_backward(dconv, gc, vv, conv_full, nb)
    part["w_in"] = jnp.concatenate([tn_matmul(p, h2, "grad_w_in_%d" % k)
                                    for k, p in enumerate((dgb, dgc, dvv, du, dgla, dglb))], axis=0)
    (dx1, dmod2, dg_mix), got = mixer_proj_backward(dgb, dgc, dvv, du, dgla, dglb, dx2, x1, mod, g_mix, gw["w_in"], nb,
                                                    carried=exchange_stage(("w_in",)))
    received["w_in"] = got[0]
    (da1, db1, sw1, df1, dgt1), _ = ffn_backward_hidden(dx1, a1, b1, f1, mod, gw["w2_a"], 0, nb,
                                                        "ffn_a_backward_hidden")
    (part["w2_a"],), _ = nn_matmul(sw1, df1, "grad_w2_a")
    (dx0, h1, dmod1, dg_ffn1), _ = ffn_backward_input(dx1, x0, da1, db1, mod, g_ffn1, gw["w1_a"].T, gw["w3_a"].T,
                                                      0, nb, "ffn_a_backward_input")
    (part["w1_a"],), got = nn_matmul(da1, h1, "grad_w1_a", carried=exchange_stage(("w2_a",)))
    received["w2_a"] = got[0]
    (part["w3_a"],), got = nn_matmul(db1, h1, "grad_w3_a", carried=exchange_stage(("w1_a",)))
    received["w1_a"] = got[0]
    (received["w3_a"],) = run_stage(exchange_stage(("w3_a",)), "exchange_w3_a")

    dabr, dabi, dbbr, dbbi, dcr, dci = ssm_table_grads(dab, dbd, dcd, g_n, p_n, h_n)
    gmod = jnp.concatenate([dmod1, dgt1, dmod2, dgt2, dmod3, dgt3], axis=1)
    small = dict(gmod=gmod, g_ffn1=dg_ffn1, g_mix=dg_mix, g_ffn2=dg_ffn2, g_final=dg_final, d_skip=dd_skip,
                 abr=dabr, abi=dabi, bbr=dbbr, bbi=dbbi, c_re=dcr, c_im=dci, conv_w=dconv_w)
    flat = jnp.concatenate([a.reshape(-1) for a in small.values()])
    n_small = flat.shape[0]
    n_rows = -(-n_small // (128 * ROWS)) * ROWS
    flat = jnp.pad(flat, (0, n_rows * 128 - n_small)).reshape(n_rows, 128)
    (small_all,) = run_stage(GatherStage([flat]), "gather_small_grads")
    small_all = small_all.reshape(N_DEV, n_rows, 128)
    total = sum_slots(small_all, "sum_small_grads").reshape(-1)
    tot, off = {}, 0
    for key, like in small.items():
        n = math.prod(like.shape)
        tot[key], off = total[off:off + n].reshape(like.shape), off + n
    gmod_all = small_all.reshape(N_DEV, n_rows * 128)[:, :nb * N_MOD * d].reshape(N_DEV * nb, N_MOD * d)
    g_a_re, g_a_im, g_b_re, g_b_im, g_log_dt = disc_vjp((tot["abr"], tot["abi"], tot["bbr"], tot["bbi"]))

    grads = {}
    grads["b_ada"] = sum_rows(tot["gmod"].reshape(nb, N_MOD * d))
    grads["g_ffn1"], grads["g_mix"], grads["g_ffn2"] = tot["g_ffn1"], tot["g_mix"], tot["g_ffn2"]
    grads["g_final"] = tot["g_final"].reshape(d)
    grads["d_skip"] = tot["d_skip"]
    grads["a_re"], grads["a_im"], grads["log_dt"] = g_a_re[None], g_a_im[None], g_log_dt[None]
    grads["b_re"], grads["b_im"] = g_b_re[None], g_b_im[None]
    grads["c_re"], grads["c_im"] = tot["c_re"][None], tot["c_im"][None]
    grads["conv_w"] = lax.dynamic_slice(tot["conv_w"], (0, me * conv_w.shape[2]), (CONV_K, conv_w.shape[2]))[None]

    delta, new_m, new_v = {}, {}, {}
    for name in big:
        gsum = sum_slots(received[name], "sum_" + name)
        if name == "w_glu":
            gsum = gsum.reshape(w_glu.shape[1], w_glu.shape[2])
        if name in transposed:
            gsum = gsum.T
        grads[name] = gsum[None]
        dl, mm, vn = adamw_update(weights[name][0], gsum, mom1[name][0], mom2[name][0], "adamw_" + name)
        delta[name], new_m[name], new_v[name] = dl[None], mm[None], vn[None]

    gmod_cols = lax.dynamic_slice(gmod_all, (0, me * ada_cols), (N_DEV * nb, ada_cols))
    g_wada, d_wada, m_wada, v_wada = ada_backward_update(c_all, gmod_cols, w_ada[0], m_w_ada[0], v_w_ada[0])
    grads["w_ada"], delta["w_ada"], new_m["w_ada"], new_v["w_ada"] = g_wada[None], d_wada[None], m_wada[None], v_wada[None]

    small_names = [n for n in names if n not in big and n != "w_ada"]

    def as2d(a):
        return a.reshape(-1, a.shape[-1])

    sw_, sg_, sm_, sv_ = ([as2d(src[n]) for n in small_names] for src in (weights, grads, mom1, mom2))
    sd, smo, svo = adamw_update_small(sw_, sg_, sm_, sv_)
    for n, dl, mm, vn in zip(small_names, sd, smo, svo):
        shape = weights[n].shape
        grads[n] = grads[n].reshape(shape)
        delta[n], new_m[n], new_v[n] = dl.reshape(shape), mm.reshape(shape), vn.reshape(shape)

    grad_x = dx0.reshape(nb, s, d)
    return (loss, grad_x, *[grads[n] for n in names], *[delta[n] for n in names],
            *[new_m[n] for n in names], *[new_v[n] for n in names])


def sum_rows(a):
    r, c = a.shape

    def body(a_ref, o_ref):
        acc = a_ref[0:1, :]
        for j in range(1, r):
            acc = acc + a_ref[j:j + 1, :]
        o_ref[...] = acc

    return pl.pallas_call(body, name="sum_rows", out_shape=jax.ShapeDtypeStruct((1, c), F32),
                          compiler_params=_params())(a)
```

```python
import functools
import math

import jax
import jax.numpy as jnp
from jax import lax
from jax.experimental import pallas as pl
from jax.experimental.pallas import tpu as pltpu

F32 = jnp.float32
BF16 = jnp.bfloat16
N_DEV = 8
N_MOD = 9
EPS = 1e-6
CONV_K = 3
ADAM_LR = 0.001
ADAM_B1 = 0.9
ADAM_B2 = 0.999
ADAM_EPS = 1e-08
ADAM_WD = 0.01
ADAM_STEP = 10
GELU_C0 = math.sqrt(2.0 / math.pi)
GELU_C1 = 0.044715
V7X_VMEM_LIMIT = 56 * 1024 * 1024
MESH_ID = pl.DeviceIdType.MESH
NT = (((1,), (1,)), ((), ()))
TN = (((0,), (0,)), ((), ()))


def _dot(a, b, dims=None):
    if dims is None:
        return jnp.dot(a, b, preferred_element_type=F32)
    return lax.dot_general(a, b, dims, preferred_element_type=F32)


def _params(sem=None, vmem=V7X_VMEM_LIMIT):
    return pltpu.CompilerParams(dimension_semantics=sem, vmem_limit_bytes=vmem)


def _full(shape):
    return pl.BlockSpec(shape, lambda *_: (0,) * len(shape))


def _const(shape):
    return pl.BlockSpec(shape, lambda *_: (0,) * len(shape), pipeline_mode=pl.Buffered(1))


def _tile(n, want):
    t = min(n, want)
    while n % t:
        t //= 2
    return t


class GatherStage:
    def __init__(self, shards):
        n = len(shards)
        self.inputs = list(shards)
        self.out_shape = [jax.ShapeDtypeStruct((N_DEV * s.shape[0], s.shape[1]), s.dtype) for s in shards]
        self.scratch = [pltpu.SemaphoreType.DMA((7 * n,)), pltpu.SemaphoreType.DMA((7 * n,)),
                        pltpu.SemaphoreType.DMA((n,))]

    def _plan(self, ins, outs, sems):
        send_sems, recv_sems, local_sems = sems
        n = len(ins)
        x, y, c = lax.axis_index("x"), lax.axis_index("y"), lax.axis_index("c")
        me, sibling = (x, y, c), (x, y, 1 - c)
        chips = [(1 - x, y), (x, 1 - y), (1 - x, 1 - y)]

        def rows(k, px, py, pc):
            r = ins[k].shape[0]
            return outs[k].at[pl.ds((4 * px + 2 * py + pc) * r, r), :]

        def copy(k, j, block, to, src=None):
            return pltpu.make_async_remote_copy(
                src_ref=rows(k, *block) if src is None else src, dst_ref=rows(k, *block),
                send_sem=send_sems.at[7 * k + j], recv_sem=recv_sems.at[7 * k + j],
                device_id=to, device_id_type=MESH_ID)

        mine = [pltpu.make_async_copy(ins[k], rows(k, *me), local_sems.at[k]) for k in range(n)]
        first = []
        for k in range(n):
            first.append(copy(k, 0, me, sibling, src=ins[k]))
            first += [copy(k, 1 + j, me, (*chip, c), src=ins[k]) for j, chip in enumerate(chips)]
        return n, c, me, sibling, chips, copy, mine, first

    def start(self, ins, outs, sems):
        *_, mine, first = self._plan(ins, outs, sems)
        for cp in mine + first:
            cp.start()

    def finish(self, ins, outs, sems):
        n, c, me, sibling, chips, copy, mine, first = self._plan(ins, outs, sems)
        passed = []
        for j, chip in enumerate(chips):
            for k in range(n):
                copy(k, 1 + j, (*chip, c), me).wait_recv()
                cp = copy(k, 4 + j, (*chip, c), sibling)
                cp.start()
                passed.append(cp)
        for k in range(n):
            copy(k, 0, sibling, me).wait_recv()
            for j, chip in enumerate(chips):
                copy(k, 4 + j, (*chip, 1 - c), me).wait_recv()
        for cp in first + passed:
            cp.wait_send()
        for cp in mine:
            cp.wait()


class ExchangeStage:
    def __init__(self, bufs):
        n = len(bufs)
        self.inputs = list(bufs)
        self.out_shape = [jax.ShapeDtypeStruct(b.shape, b.dtype) for b in bufs]
        self.scratch = [pltpu.SemaphoreType.DMA((7 * n,)), pltpu.SemaphoreType.DMA((7 * n,)),
                        pltpu.SemaphoreType.DMA((n,))]

    def _plan(self, ins, outs, sems):
        send_sems, recv_sems, local_sems = sems
        n = len(ins)
        x, y, c = lax.axis_index("x"), lax.axis_index("y"), lax.axis_index("c")
        me = 4 * x + 2 * y + c
        mine = [pltpu.make_async_copy(ins[k].at[me], outs[k].at[me], local_sems.at[k]) for k in range(n)]
        copies = []
        for mask in range(1, N_DEV):
            px, py, pc = x ^ (mask >> 2), y ^ ((mask >> 1) & 1), c ^ (mask & 1)
            for k in range(n):
                copies.append(pltpu.make_async_remote_copy(
                    src_ref=ins[k].at[4 * px + 2 * py + pc], dst_ref=outs[k].at[me],
                    send_sem=send_sems.at[7 * k + mask - 1], recv_sem=recv_sems.at[7 * k + mask - 1],
                    device_id=(px, py, pc), device_id_type=MESH_ID))
        return mine, copies

    def start(self, ins, outs, sems):
        mine, copies = self._plan(ins, outs, sems)
        for cp in mine + copies:
            cp.start()

    def finish(self, ins, outs, sems):
        mine, copies = self._plan(ins, outs, sems)
        for cp in copies:
            cp.wait_recv()
        for cp in copies:
            cp.wait_send()
        for cp in mine:
            cp.wait()


ANY_SPEC = pl.BlockSpec(memory_space=pl.ANY)


def run_stage(stage, name):
    ci, co = len(stage.inputs), len(stage.out_shape)

    def body(*refs):
        ins, outs, sems = refs[:ci], refs[ci:ci + co], refs[ci + co:]
        stage.start(ins, outs, sems)
        stage.finish(ins, outs, sems)

    return pl.pallas_call(body, name=name, out_shape=stage.out_shape, in_specs=[ANY_SPEC] * ci,
                          out_specs=[ANY_SPEC] * co, scratch_shapes=stage.scratch)(*stage.inputs)


def _call(body, *, name, grid, in_specs, out_specs, out_shape, args, scratch_shapes=(), carried=None):
    sem = ("arbitrary",) * len(grid)
    if carried is None:
        return pl.pallas_call(body, name=name, grid=grid, in_specs=list(in_specs), out_specs=list(out_specs),
                              out_shape=list(out_shape), scratch_shapes=list(scratch_shapes),
                              compiler_params=_params(sem))(*args), None
    ni, no, ns = len(in_specs), len(out_shape), len(scratch_shapes)
    ci, co = len(carried.inputs), len(carried.out_shape)

    def wrapped(*refs):
        ins, refs = refs[:ni], refs[ni:]
        cins, refs = refs[:ci], refs[ci:]
        outs, refs = refs[:no], refs[no:]
        couts, refs = refs[:co], refs[co:]
        scr, csems = refs[:ns], refs[ns:]
        ids = [pl.program_id(k) for k in range(len(grid))]
        first = functools.reduce(jnp.logical_and, [i == 0 for i in ids])
        last = functools.reduce(jnp.logical_and, [i == g - 1 for i, g in zip(ids, grid)])

        @pl.when(first)
        def _():
            carried.start(cins, couts, csems)

        body(*ins, *outs, *scr)

        @pl.when(last)
        def _():
            carried.finish(cins, couts, csems)

    res = pl.pallas_call(
        wrapped, name=name, grid=grid, in_specs=list(in_specs) + [ANY_SPEC] * ci,
        out_specs=list(out_specs) + [ANY_SPEC] * co, out_shape=list(out_shape) + carried.out_shape,
        scratch_shapes=list(scratch_shapes) + carried.scratch, compiler_params=_params(sem),
    )(*args, *carried.inputs)
    return res[:no], res[no:]


def _norm_mod(x, g, shift, scale):
    r = lax.rsqrt(jnp.mean(x * x, axis=-1, keepdims=True) + EPS)
    n = x * r
    return (n * g) * (1.0 + scale) + shift, n, r


def _norm_mod_bwd(dh, n, r, g, scale):
    dsh = jnp.sum(dh, axis=0, keepdims=True)
    dsc = jnp.sum(dh * (n * g), axis=0, keepdims=True)
    dg = jnp.sum(dh * (1.0 + scale) * n, axis=0, keepdims=True)
    dn = dh * ((1.0 + scale) * g)
    dx = r * (dn - n * jnp.mean(n * dn, axis=-1, keepdims=True))
    return dx, dsh, dsc, dg


def _mod_rows(mod_ref, sub):
    m = mod_ref[0]
    return m[3 * sub:3 * sub + 1], m[3 * sub + 1:3 * sub + 2], m[3 * sub + 2:3 * sub + 3]


def _gelu(x):
    t = jnp.tanh(GELU_C0 * (x + GELU_C1 * x * x * x))
    return 0.5 * x * (1.0 + t), t


def _gelu_grad(x, t):
    return 0.5 * (1.0 + t) + 0.5 * x * (1.0 - t * t) * (GELU_C0 * (1.0 + 3.0 * GELU_C1 * x * x))


def _accumulate(ref, val, first):
    @pl.when(first)
    def _():
        ref[...] = val

    @pl.when(jnp.logical_not(first))
    def _():
        ref[...] += val


def ada_forward(c_all, w_ada, b_ada_cols):
    def body(c_ref, w_ref, b_ref, o_ref):
        c = c_ref[...]
        cond = (c * jax.nn.sigmoid(c)).astype(BF16)
        o_ref[...] = _dot(cond, w_ref[...].astype(BF16)) + b_ref[...]

    nb, d = c_all.shape
    cols = w_ada.shape[1]
    tn = _tile(cols, 384)
    return pl.pallas_call(
        body, name="ada_forward", grid=(cols // tn,),
        out_shape=jax.ShapeDtypeStruct((nb, cols), F32),
        in_specs=[_full((nb, d)), pl.BlockSpec((d, tn), lambda j: (0, j)), pl.BlockSpec((1, tn), lambda j: (0, j))],
        out_specs=pl.BlockSpec((nb, tn), lambda j: (0, j)),
        compiler_params=_params(("arbitrary",)),
    )(c_all, w_ada, b_ada_cols)


def _adamw(w, g, m, v):
    m = ADAM_B1 * m + (1.0 - ADAM_B1) * g
    v = ADAM_B2 * v + (1.0 - ADAM_B2) * (g * g)
    m_hat = m / (1.0 - ADAM_B1 ** ADAM_STEP)
    v_hat = v / (1.0 - ADAM_B2 ** ADAM_STEP)
    delta = -ADAM_LR * (m_hat / (jnp.sqrt(v_hat) + ADAM_EPS) + ADAM_WD * w)
    return delta, m, v


def ada_backward_update(c_all, gmod_cols, w, m, v):
    def body(c_ref, g_ref, w_ref, m_ref, v_ref, go_ref, d_ref, mo_ref, vo_ref):
        c = c_ref[...]
        cond = (c * jax.nn.sigmoid(c)).astype(BF16)
        g = _dot(cond, g_ref[...].astype(BF16), TN)
        go_ref[...] = g
        d_ref[...], mo_ref[...], vo_ref[...] = _adamw(w_ref[...], g, m_ref[...], v_ref[...])

    nb, d = c_all.shape
    cols = w.shape[1]
    tn = _tile(cols, 128)
    col = pl.BlockSpec((d, tn), lambda j: (0, j))
    return pl.pallas_call(
        body, name="ada_backward_update", grid=(cols // tn,),
        out_shape=[jax.ShapeDtypeStruct(w.shape, F32)] * 4,
        in_specs=[_full((nb, d)), pl.BlockSpec((nb, tn), lambda j: (0, j)), col, col, col],
        out_specs=[col] * 4,
        compiler_params=_params(("arbitrary",)),
    )(c_all, gmod_cols, w, m, v)


def _row_spec(tm, width, tiles_per_seq):
    return pl.BlockSpec((tm, width), lambda b, i: (b * tiles_per_seq + i, 0))


def _mod_spec(d):
    return pl.BlockSpec((1, N_MOD, d), lambda b, i: (b, 0, 0))


def _col_spec(rows, tm, tiles_per_seq):
    return pl.BlockSpec((rows, tm), lambda b, i: (0, b * tiles_per_seq + i))


def _ffn_chunk(f):
    return f // 2 if f % 256 == 0 and f > 1536 else f


def ffn_forward(x, mod, g, w1t, w3t, w2t, sub, nb, name, carried=None):
    t, d = x.shape
    f = w1t.shape[0]
    s = t // nb
    tm = _tile(s, 512)
    fc = _ffn_chunk(f)

    def body(x_ref, mod_ref, g_ref, w1_ref, w3_ref, w2_ref, xo_ref, a_ref, b_ref, f_ref):
        xv = x_ref[...]
        sh, sc, gt = _mod_rows(mod_ref, sub)
        h, _, _ = _norm_mod(xv, g_ref[...], sh, sc)
        hb = h.astype(BF16)
        acc_t = jnp.zeros((d, tm), F32)
        for k in range(f // fc):
            rows = slice(k * fc, (k + 1) * fc)
            a = _dot(w1_ref[rows, :], hb, NT)
            b = _dot(w3_ref[rows, :], hb, NT)
            a_ref[rows, :] = a.astype(BF16)
            b_ref[rows, :] = b.astype(BF16)
            sw = (a * jax.nn.sigmoid(a)) * b
            acc_t = acc_t + _dot(w2_ref[:, rows], sw.astype(BF16))
        acc = acc_t.T
        f_ref[...] = acc.astype(BF16)
        xo_ref[...] = xv + (0.5 * gt) * acc

    tps = s // tm
    rd, cf = _row_spec(tm, d, tps), _col_spec(f, tm, tps)
    return _call(
        body, name=name, grid=(nb, tps), carried=carried,
        out_shape=[jax.ShapeDtypeStruct((t, d), F32), jax.ShapeDtypeStruct((f, t), BF16),
                   jax.ShapeDtypeStruct((f, t), BF16), jax.ShapeDtypeStruct((t, d), BF16)],
        in_specs=[rd, _mod_spec(d), _const((1, d)), _const((f, d)), _const((f, d)), _const((d, f))],
        out_specs=[rd, cf, cf, rd],
        args=(x, mod, g, w1t, w3t, w2t))


def ffn_backward_hidden(dxo, a_t, b_t, fo, mod, w2, sub, nb, name, carried=None):
    t, d = dxo.shape
    f = w2.shape[0]
    s = t // nb
    tm = _tile(s, 512)
    fc = _tile(f, 704) if f % 704 == 0 else _tile(f, 512)

    def body(dxo_ref, a_ref, b_ref, f_ref, mod_ref, w2_ref, da_ref, db_ref, s_ref, df_ref, dgt_ref):
        i = pl.program_id(1)
        dxo = dxo_ref[...]
        _, _, gt = _mod_rows(mod_ref, sub)
        dfb = ((0.5 * gt) * dxo).astype(BF16)
        df_ref[...] = dfb
        dgt = 0.5 * jnp.sum(dxo * f_ref[...].astype(F32), axis=0, keepdims=True)
        _accumulate(dgt_ref, dgt[None], i == 0)
        for k in range(f // fc):
            rows = slice(k * fc, (k + 1) * fc)
            ds = _dot(w2_ref[rows, :], dfb, NT)
            av = a_ref[rows, :].astype(F32)
            bv = b_ref[rows, :].astype(F32)
            sig = jax.nn.sigmoid(av)
            sl = av * sig
            da_ref[rows, :] = (ds * bv * (sig * (1.0 + av * (1.0 - sig)))).astype(BF16)
            db_ref[rows, :] = (ds * sl).astype(BF16)
            s_ref[rows, :] = (sl * bv).astype(BF16)

    tps = s // tm
    rd, cf = _row_spec(tm, d, tps), _col_spec(f, tm, tps)
    return _call(
        body, name=name, grid=(nb, tps), carried=carried,
        out_shape=[jax.ShapeDtypeStruct((f, t), BF16)] * 3
        + [jax.ShapeDtypeStruct((t, d), BF16), jax.ShapeDtypeStruct((nb, 1, d), F32)],
        in_specs=[rd, cf, cf, rd, _mod_spec(d), _const((f, d))],
        out_specs=[cf, cf, cf, rd, pl.BlockSpec((1, 1, d), lambda b, i: (b, 0, 0))],
        args=(dxo, a_t, b_t, fo, mod, w2))


def ffn_backward_input(dxo, x, da_t, db_t, mod, g, w1, w3, sub, nb, name, carried=None):
    t, d = x.shape
    f = w1.shape[1]
    s = t // nb
    tm = _tile(s, 512)

    def body(dxo_ref, x_ref, da_ref, db_ref, mod_ref, g_ref, w1_ref, w3_ref, dx_ref, h_ref, dmod_ref, dg_ref):
        bi, i = pl.program_id(0), pl.program_id(1)
        sh, sc, _ = _mod_rows(mod_ref, sub)
        gv = g_ref[...]
        h, n, r = _norm_mod(x_ref[...], gv, sh, sc)
        h_ref[...] = h.astype(BF16)
        dh_t = _dot(w1_ref[...], da_ref[...]) + _dot(w3_ref[...], db_ref[...])
        dxn, dsh, dsc, dg = _norm_mod_bwd(dh_t.T, n, r, gv, sc)
        dx_ref[...] = dxo_ref[...] + dxn
        _accumulate(dmod_ref, jnp.concatenate([dsh, dsc], axis=0)[None], i == 0)
        _accumulate(dg_ref, dg, jnp.logical_and(bi == 0, i == 0))

    tps = s // tm
    rd, cf = _row_spec(tm, d, tps), _col_spec(f, tm, tps)
    return _call(
        body, name=name, grid=(nb, tps), carried=carried,
        out_shape=[jax.ShapeDtypeStruct((t, d), F32), jax.ShapeDtypeStruct((t, d), BF16),
                   jax.ShapeDtypeStruct((nb, 2, d), F32), jax.ShapeDtypeStruct((1, d), F32)],
        in_specs=[rd, rd, cf, cf, _mod_spec(d), _const((1, d)), _const((d, f)), _const((d, f))],
        out_specs=[rd, rd, pl.BlockSpec((1, 2, d), lambda b, i: (b, 0, 0)), _full((1, d))],
        args=(dxo, x, da_t, db_t, mod, g, w1, w3))


def nn_matmul(lhs_t, rhs, name, carried=None):
    m, t = lhs_t.shape
    n = rhs.shape[1]
    tk = _tile(t, 2048)
    tmm = m if m <= 1536 else m // 2
    nk = t // tk

    def body(a_ref, b_ref, o_ref, acc_ref):
        k = pl.program_id(1)
        _accumulate(acc_ref, _dot(a_ref[...], b_ref[...]), k == 0)

        @pl.when(k == nk - 1)
        def _():
            o_ref[...] = acc_ref[...].astype(BF16)

    return _call(
        body, name=name, grid=(m // tmm, nk), carried=carried,
        out_shape=[jax.ShapeDtypeStruct((m, n), BF16)],
        in_specs=[pl.BlockSpec((tmm, tk), lambda j, k: (j, k)), pl.BlockSpec((tk, n), lambda j, k: (k, 0))],
        out_specs=[pl.BlockSpec((tmm, n), lambda j, k: (j, 0))],
        scratch_shapes=[pltpu.VMEM((tmm, n), F32)],
        args=(lhs_t, rhs))


def tn_matmul(lhs, rhs, name):
    t, m = lhs.shape
    n = rhs.shape[1]
    tk = _tile(t, 2048)
    nk = t // tk

    def body(a_ref, b_ref, o_ref, acc_ref):
        k = pl.program_id(0)
        _accumulate(acc_ref, _dot(a_ref[...], b_ref[...], TN), k == 0)

        @pl.when(k == nk - 1)
        def _():
            o_ref[...] = acc_ref[...].astype(BF16)

    return pl.pallas_call(
        body, name=name, grid=(nk,),
        out_shape=jax.ShapeDtypeStruct((m, n), BF16),
        in_specs=[pl.BlockSpec((tk, m), lambda k: (k, 0)), pl.BlockSpec((tk, n), lambda k: (k, 0))],
        out_specs=pl.BlockSpec((m, n), lambda k: (0, 0)),
        scratch_shapes=[pltpu.VMEM((m, n), F32)],
        compiler_params=_params(("arbitrary",)),
    )(lhs, rhs)


def mixer_proj_forward(x, mod, g, w_in_t, cw, sw, nb, carried=None):
    t, d = x.shape
    s = t // nb
    tm = _tile(s, 512)
    pieces = [(0, cw, "bf16"), (cw, cw, "bf16"), (2 * cw, cw, "bf16"), (3 * cw, sw, "f32"),
              (3 * cw + sw, d, "sig"), (3 * cw + sw + d, d, "sig")]

    def body(x_ref, mod_ref, g_ref, w_ref, *outs):
        h_ref = outs[-1]
        sh, sc, _ = _mod_rows(mod_ref, 1)
        h, _, _ = _norm_mod(x_ref[...], g_ref[...], sh, sc)
        hb = h.astype(BF16)
        h_ref[...] = hb
        for (off, width, kind), o_ref in zip(pieces, outs[:-1]):
            ck = _tile(width, 512)
            for j in range(width // ck):
                p = _dot(hb, w_ref[off + j * ck:off + (j + 1) * ck, :], NT)
                if kind == "sig":
                    p = jax.nn.sigmoid(p)
                o_ref[:, j * ck:(j + 1) * ck] = p.astype(o_ref.dtype)

    tps = s // tm
    widths = [(cw, BF16), (cw, BF16), (cw, BF16), (sw, F32), (d, BF16), (d, BF16), (d, BF16)]
    return _call(
        body, name="mixer_proj_forward", grid=(nb, tps), carried=carried,
        out_shape=[jax.ShapeDtypeStruct((t, w), dt) for w, dt in widths],
        in_specs=[_row_spec(tm, d, tps), _mod_spec(d), _const((1, d)), _const(w_in_t.shape)],
        out_specs=[_row_spec(tm, w, tps) for w, _ in widths],
        args=(x, mod, g, w_in_t))


def mixer_proj_backward(dgb, dgc, dv, du, dgla, dglb, dxo, x, mod, g, w_in_t, nb, carried=None):
    t, d = x.shape
    s = t // nb
    tm = _tile(s, 512)
    parts = [dgb, dgc, dv, du, dgla, dglb]
    offs = [0]
    for p in parts:
        offs.append(offs[-1] + p.shape[1])

    def body(*refs):
        p_refs = refs[:6]
        dxo_ref, x_ref, mod_ref, g_ref, w_ref, dx_ref, dmod_ref, dg_ref = refs[6:]
        bi, i = pl.program_id(0), pl.program_id(1)
        dh = jnp.zeros((tm, d), F32)
        for p_ref, off in zip(p_refs, offs):
            width = p_ref.shape[1]
            ck = _tile(width, 512)
            for j in range(width // ck):
                dh = dh + _dot(p_ref[:, j * ck:(j + 1) * ck], w_ref[off + j * ck:off + (j + 1) * ck, :])
        sh, sc, _ = _mod_rows(mod_ref, 1)
        gv = g_ref[...]
        _, n, r = _norm_mod(x_ref[...], gv, sh, sc)
        dxn, dsh, dsc, dg = _norm_mod_bwd(dh, n, r, gv, sc)
        dx_ref[...] = dxo_ref[...] + dxn
        _accumulate(dmod_ref, jnp.concatenate([dsh, dsc], axis=0)[None], i == 0)
        _accumulate(dg_ref, dg, jnp.logical_and(bi == 0, i == 0))

    tps = s // tm
    rd = _row_spec(tm, d, tps)
    return _call(
        body, name="mixer_proj_backward", grid=(nb, tps), carried=carried,
        out_shape=[jax.ShapeDtypeStruct((t, d), F32), jax.ShapeDtypeStruct((nb, 2, d), F32),
                   jax.ShapeDtypeStruct((1, d), F32)],
        in_specs=[_row_spec(tm, p.shape[1], tps) for p in parts]
        + [rd, rd, _mod_spec(d), _const((1, d)), _const(w_in_t.shape)],
        out_specs=[rd, pl.BlockSpec((1, 2, d), lambda b, i: (b, 0, 0)), _full((1, d))],
        args=(*parts, dxo, x, mod, g, w_in_t))


GROUPS_PER_BLOCK = 8
ROWS = 8
SCAN_LANES = 512


def _scan_rows(xr, xi, masks, shifts):
    for (mr, mi), sft in zip(masks, shifts):
        sr, si = pltpu.roll(xr, sft, 0), pltpu.roll(xi, sft, 0)
        xr, xi = xr + mr * sr - mi * si, xi + mr * si + mi * sr
    return xr, xi


def _cmul_add(ar, ai, cr, ci, br, bi):
    return ar * cr - ai * ci + br, ar * ci + ai * cr + bi


def _segment_rows(perm_ref, x):
    return _dot(perm_ref[0], x).astype(BF16)


def _time_rows(perm_ref, x):
    hi = x.astype(BF16)
    lo = (x - hi.astype(F32)).astype(BF16)
    return _dot(perm_ref[1], hi) + _dot(perm_ref[1], lo)


def segment_permutation(tc):
    r = jnp.arange(tc)
    p = (r[:, None] % ROWS * (tc // ROWS) + r[:, None] // ROWS == r[None, :]).astype(BF16)
    return jnp.stack([p, p.T])


def _rows_at(k, offset=0):
    return pl.ds(pl.multiple_of(k * ROWS + offset, ROWS), ROWS)


def ssm_forward(u, bd, cd, a1, ml, nb):
    t, w = u.shape
    s = t // nb
    tc = _tile(s, 256)
    seg = tc // ROWS
    nq, ub, lq = bd.shape[1], bd.shape[2], bd.shape[3]
    nl = nq * lq
    nch = s // tc
    lw = min(nl, SCAN_LANES)

    def body(u_ref, perm_ref, bd_ref, cd_ref, a1_ref, ml_ref, y_ref, st_ref, xr_s, xi_s, car_s):
        i = pl.program_id(1)

        @pl.when(i == 0)
        def _():
            car_s[...] = jnp.zeros_like(car_s)

        ub16 = _segment_rows(perm_ref, u_ref[...].astype(BF16))
        for q in range(nq):
            lanes = slice(q * lq, (q + 1) * lq)
            uq = ub16[:, q * ub:(q + 1) * ub]
            xr_s[:, lanes] = _dot(uq, bd_ref[0, q])
            xi_s[:, lanes] = _dot(uq, bd_ref[1, q])
        row_is_0 = lax.broadcasted_iota(jnp.int32, (ROWS, lw), 0) == 0
        zero = jnp.zeros((ROWS, lw), F32)
        for j in range(nl // lw):
            lanes = slice(j * lw, (j + 1) * lw)
            ar, ai = a1_ref[0, :, lanes], a1_ref[1, :, lanes]

            def local(k, c):
                return _cmul_add(ar, ai, c[0], c[1], xr_s[_rows_at(k), lanes], xi_s[_rows_at(k), lanes])

            er, ei = lax.fori_loop(0, seg, local, (zero, zero))
            masks = [(ml_ref[d, 0, :, lanes], ml_ref[d, 1, :, lanes]) for d in range(3)]
            cr, ci = _scan_rows(jnp.where(row_is_0, car_s[0, :, lanes], pltpu.roll(er, 1, 0)),
                                jnp.where(row_is_0, car_s[1, :, lanes], pltpu.roll(ei, 1, 0)), masks, (1, 2, 4))
            st_ref[0, 0, :, lanes] = cr
            st_ref[0, 1, :, lanes] = ci

            def full(k, c):
                xr, xi = _cmul_add(ar, ai, c[0], c[1], xr_s[_rows_at(k), lanes], xi_s[_rows_at(k), lanes])
                xr_s[_rows_at(k), lanes] = xr
                xi_s[_rows_at(k), lanes] = xi
                return xr, xi

            fr, fi = lax.fori_loop(0, seg, full, (cr, ci))
            car_s[0, :, lanes] = jnp.broadcast_to(fr[ROWS - 1:ROWS], fr.shape)
            car_s[1, :, lanes] = jnp.broadcast_to(fi[ROWS - 1:ROWS], fi.shape)
        y = jnp.concatenate([_dot(xr_s[:, q * lq:(q + 1) * lq].astype(BF16), cd_ref[0, q])
                             + _dot(xi_s[:, q * lq:(q + 1) * lq].astype(BF16), cd_ref[1, q]) for q in range(nq)],
                            axis=1)
        y_ref[...] = _time_rows(perm_ref, y)

    perm = segment_permutation(tc)
    return pl.pallas_call(
        body, name="ssm_forward", grid=(nb, nch),
        out_shape=[jax.ShapeDtypeStruct((t, w), F32), jax.ShapeDtypeStruct((nb * nch, 2, ROWS, nl), F32)],
        in_specs=[pl.BlockSpec((tc, w), lambda b, i: (b * nch + i, 0)), _const(perm.shape), _const(bd.shape),
                  _const(cd.shape), _const(a1.shape), _const(ml.shape)],
        out_specs=[pl.BlockSpec((tc, w), lambda b, i: (b * nch + i, 0)),
                   pl.BlockSpec((1, 2, ROWS, nl), lambda b, i: (b * nch + i, 0, 0, 0))],
        scratch_shapes=[pltpu.VMEM((tc, nl), F32), pltpu.VMEM((tc, nl), F32), pltpu.VMEM((2, ROWS, nl), F32)],
        compiler_params=_params(("arbitrary", "arbitrary")),
    )(u, perm, bd, cd, a1, ml)


def ssm_backward(u, dy, st, bd, cd, a1, mlb, dskip, nb, carried=None):
    t, w = u.shape
    s = t // nb
    tc = _tile(s, 256)
    seg = tc // ROWS
    nq, ub, lq = bd.shape[1], bd.shape[2], bd.shape[3]
    nl = nq * lq
    nch = s // tc
    lw = min(nl, SCAN_LANES)

    def body(u_ref, dy_ref, st_ref, perm_ref, bd_ref, cd_ref, a1_ref, mlb_ref, dsk_ref,
             du_ref, dab_ref, dbd_ref, dcd_ref, xr_s, xi_s, lr_s, li_s, car_s):
        bi, i = pl.program_id(0), pl.program_id(1)
        first = jnp.logical_and(bi == 0, i == 0)

        @pl.when(i == 0)
        def _():
            car_s[...] = jnp.zeros_like(car_s)

        @pl.when(first)
        def _():
            dab_ref[...] = jnp.zeros_like(dab_ref)
            dbd_ref[...] = jnp.zeros_like(dbd_ref)
            dcd_ref[...] = jnp.zeros_like(dcd_ref)

        ub16 = _segment_rows(perm_ref, u_ref[...].astype(BF16))
        dyb16 = _segment_rows(perm_ref, dy_ref[...].astype(BF16))
        xr_s[0:ROWS, :] = st_ref[0, 0]
        xi_s[0:ROWS, :] = st_ref[0, 1]
        for q in range(nq):
            lanes = slice(q * lq, (q + 1) * lq)
            uq = ub16[:, q * ub:(q + 1) * ub]
            dq = dyb16[:, q * ub:(q + 1) * ub]
            xr_s[ROWS:, lanes] = _dot(uq, bd_ref[0, q])
            xi_s[ROWS:, lanes] = _dot(uq, bd_ref[1, q])
            lr_s[:, lanes] = _dot(dq, cd_ref[0, q], NT)
            li_s[:, lanes] = _dot(dq, cd_ref[1, q], NT)
        row_is_7 = lax.broadcasted_iota(jnp.int32, (ROWS, lw), 0) == ROWS - 1
        zero = jnp.zeros((ROWS, lw), F32)
        for j in range(nl // lw):
            lanes = slice(j * lw, (j + 1) * lw)
            ar, ai = a1_ref[0, :, lanes], a1_ref[1, :, lanes]
            nai = -ai

            def states(k, c):
                xr, xi = _cmul_add(ar, ai, c[0], c[1], xr_s[_rows_at(k, ROWS), lanes], xi_s[_rows_at(k, ROWS), lanes])
                xr_s[_rows_at(k, ROWS), lanes] = xr
                xi_s[_rows_at(k, ROWS), lanes] = xi
                return xr, xi

            lax.fori_loop(0, seg, states, (st_ref[0, 0, :, lanes], st_ref[0, 1, :, lanes]))

            def local(kk, c):
                k = seg - 1 - kk
                return _cmul_add(ar, nai, c[0], c[1], lr_s[_rows_at(k), lanes], li_s[_rows_at(k), lanes])

            er, ei = lax.fori_loop(0, seg, local, (zero, zero))
            masks = [(mlb_ref[d, 0, :, lanes], mlb_ref[d, 1, :, lanes]) for d in range(3)]
            cr, ci = _scan_rows(jnp.where(row_is_7, car_s[0, :, lanes], pltpu.roll(er, ROWS - 1, 0)),
                                jnp.where(row_is_7, car_s[1, :, lanes], pltpu.roll(ei, ROWS - 1, 0)), masks, (7, 6, 4))

            def full(kk, c):
                cr_, ci_, accr, acci = c
                k = seg - 1 - kk
                lr, li = _cmul_add(ar, nai, cr_, ci_, lr_s[_rows_at(k), lanes], li_s[_rows_at(k), lanes])
                lr_s[_rows_at(k), lanes] = lr
                li_s[_rows_at(k), lanes] = li
                xpr, xpi = xr_s[_rows_at(k), lanes], xi_s[_rows_at(k), lanes]
                return lr, li, accr + lr * xpr + li * xpi, acci + li * xpr - lr * xpi

            lr0, li0, accr, acci = lax.fori_loop(0, seg, full, (cr, ci, zero, zero))
            car_s[0, :, lanes] = jnp.broadcast_to(lr0[0:1], lr0.shape)
            car_s[1, :, lanes] = jnp.broadcast_to(li0[0:1], li0.shape)
            dab_ref[0, :, lanes] += accr
            dab_ref[1, :, lanes] += acci
        du_parts = []
        for q in range(nq):
            lanes = slice(q * lq, (q + 1) * lq)
            cols = slice(q * ub, (q + 1) * ub)
            lrb, lib = lr_s[:, lanes].astype(BF16), li_s[:, lanes].astype(BF16)
            uq, dq = ub16[:, cols], dyb16[:, cols]
            du_parts.append(_dot(lrb, bd_ref[0, q], NT) + _dot(lib, bd_ref[1, q], NT))
            dbd_ref[0, q] += _dot(uq, lrb, TN)
            dbd_ref[1, q] += _dot(uq, lib, TN)
            dcd_ref[0, q] += _dot(xr_s[ROWS:, lanes].astype(BF16), dq, TN)
            dcd_ref[1, q] += _dot(xi_s[ROWS:, lanes].astype(BF16), dq, TN)
        du = _time_rows(perm_ref, jnp.concatenate(du_parts, axis=1)) + dsk_ref[...] * dy_ref[...]
        du_ref[...] = du.astype(BF16)

    rev = lambda b, i: (b * nch + nch - 1 - i, 0)
    perm = segment_permutation(tc)
    return _call(
        body, name="ssm_backward", grid=(nb, nch), carried=carried,
        out_shape=[jax.ShapeDtypeStruct((t, w), BF16), jax.ShapeDtypeStruct((2, ROWS, nl), F32),
                   jax.ShapeDtypeStruct(bd.shape, F32), jax.ShapeDtypeStruct(cd.shape, F32)],
        in_specs=[pl.BlockSpec((tc, w), rev), pl.BlockSpec((tc, w), rev),
                  pl.BlockSpec((1, 2, ROWS, nl), lambda b, i: (b * nch + nch - 1 - i, 0, 0, 0)),
                  _const(perm.shape), _const(bd.shape), _const(cd.shape), _const(a1.shape), _const(mlb.shape),
                  _const((1, w))],
        out_specs=[pl.BlockSpec((tc, w), rev), _full((2, ROWS, nl)), _full(bd.shape), _full(cd.shape)],
        scratch_shapes=[pltpu.VMEM((tc + ROWS, nl), F32), pltpu.VMEM((tc + ROWS, nl), F32),
                        pltpu.VMEM((tc, nl), F32), pltpu.VMEM((tc, nl), F32), pltpu.VMEM((2, ROWS, nl), F32)],
        args=(u, dy, st, perm, bd, cd, a1, mlb, dskip))


def ssm_discretise(a_re, a_im, b_re, b_im, log_dt):
    dt = jnp.exp(log_dt)[:, None]
    er = jnp.exp(a_re * dt)
    abr, abi = er * jnp.cos(a_im * dt), er * jnp.sin(a_im * dt)
    den = a_re * a_re + a_im * a_im
    nr, ni = abr - 1.0, abi
    fr = ((nr * a_re + ni * a_im) / den)[..., None]
    fi = ((ni * a_re - nr * a_im) / den)[..., None]
    return abr, abi, fr * b_re - fi * b_im, fr * b_im + fi * b_re


def _complex_square(zr, zi):
    return zr * zr - zi * zi, 2.0 * zr * zi


def ssm_tables(abr, abi, bbr, bbi, c_re, c_im, seg):
    g, p, h = bbr.shape
    nq = g // GROUPS_PER_BLOCK
    zr, zi = abr.reshape(1, -1), abi.reshape(1, -1)
    a1 = jnp.stack([jnp.broadcast_to(zr, (ROWS, g * p)), jnp.broadcast_to(zi, (ROWS, g * p))])
    for _ in range(seg.bit_length() - 1):
        zr, zi = _complex_square(zr, zi)
    row = jnp.arange(ROWS)[:, None]
    ml, mlb = [], []
    for d in (1, 2, 4):
        ml.append(jnp.stack([jnp.where(row >= d, zr, 0.0), jnp.where(row >= d, zi, 0.0)]))
        mlb.append(jnp.stack([jnp.where(row + d < ROWS, zr, 0.0), jnp.where(row + d < ROWS, -zi, 0.0)]))
        zr, zi = _complex_square(zr, zi)
    eye = jnp.eye(GROUPS_PER_BLOCK, dtype=F32)

    def block_diag_in(bb):
        bq = bb.reshape(nq, GROUPS_PER_BLOCK, p, h)
        return jnp.einsum("qaph,ab->qahbp", bq, eye).reshape(nq, GROUPS_PER_BLOCK * h, GROUPS_PER_BLOCK * p)

    def block_diag_out(cc):
        cq = cc.reshape(nq, GROUPS_PER_BLOCK, h, p)
        return jnp.einsum("qahp,ab->qapbh", cq, eye).reshape(nq, GROUPS_PER_BLOCK * p, GROUPS_PER_BLOCK * h)

    bd = jnp.stack([block_diag_in(bbr), block_diag_in(bbi)]).astype(BF16)
    cd = jnp.stack([block_diag_out(c_re), block_diag_out(-c_im)]).astype(BF16)
    return bd, cd, a1, jnp.stack(ml), jnp.stack(mlb)


def ssm_table_grads(dab, dbd, dcd, g, p, h):
    nq = g // GROUPS_PER_BLOCK
    dabr, dabi = dab[0].sum(0).reshape(g, p), dab[1].sum(0).reshape(g, p)
    b5 = dbd.reshape(2, nq, GROUPS_PER_BLOCK, h, GROUPS_PER_BLOCK, p)
    dbb = jnp.einsum("rqahap->rqaph", b5).reshape(2, g, p, h)
    c5 = dcd.reshape(2, nq, GROUPS_PER_BLOCK, p, GROUPS_PER_BLOCK, h)
    dcc = jnp.einsum("rqapah->rqahp", c5).reshape(2, g, h, p)
    return dabr, dabi, dbb[0], dbb[1], dcc[0], -dcc[1]


HALO = 16


def _conv_inputs(gc_ref, v_ref, gch_ref, vh_ref, cv_s, i, tm):
    cv = gc_ref[...].astype(F32) * v_ref[...].astype(F32)
    halo = gch_ref[...].astype(F32) * vh_ref[...].astype(F32)
    cv_s[0:HALO, :] = jnp.where(i == 0, 0.0, halo)
    cv_s[HALO:, :] = cv
    return cv, cv_s[HALO - 1:HALO - 1 + tm, :], cv_s[HALO - 2:HALO - 2 + tm, :]


def _halo_spec(tm, width, tiles_per_seq):
    per = tm // HALO
    return pl.BlockSpec((HALO, width), lambda b, i: (jnp.maximum((b * tiles_per_seq + i) * per - 1, 0), 0))


def mixer_merge_forward(x, gb, gc, v, sga, sgb, yssm, u, mod, conv_w, dskip, wco, wglu, wso_t, wout, nb):
    t, d = x.shape
    cw, sw = gb.shape[1], u.shape[1]
    s = t // nb
    tm = _tile(s, 256)
    tps = s // tm

    def body(x_ref, gb_ref, gc_ref, v_ref, gch_ref, vh_ref, sga_ref, sgb_ref, ys_ref, u_ref, mod_ref, cw_ref,
             dsk_ref, wco_ref, wglu_ref, wso_ref, wout_ref, xo_ref, ya_ref, yb_ref, mix_ref, cv_s):
        i = pl.program_id(1)
        cv, cv1, cv2 = _conv_inputs(gc_ref, v_ref, gch_ref, vh_ref, cv_s, i, tm)
        w = cw_ref[...]
        conv = w[0:1] * cv2 + w[1:2] * cv1 + w[2:3] * cv
        ya = _dot((gb_ref[...].astype(F32) * conv).astype(BF16), wco_ref[...])
        s0 = ys_ref[...] + dsk_ref[...] * u_ref[...]
        s1, _ = _gelu(s0)
        z = _dot(s1.astype(BF16), wglu_ref[...])
        s2 = s1 * jax.nn.sigmoid(z)
        yb = _dot(s2.astype(BF16), wso_ref[...], NT)
        merged = sga_ref[...].astype(F32) * ya + sgb_ref[...].astype(F32) * yb
        mix = _dot(merged.astype(BF16), wout_ref[...])
        _, _, gt = _mod_rows(mod_ref, 1)
        xo_ref[...] = x_ref[...] + gt * mix
        ya_ref[...] = ya.astype(BF16)
        yb_ref[...] = yb.astype(BF16)
        mix_ref[...] = mix.astype(BF16)

    rd, rc, rw = _row_spec(tm, d, tps), _row_spec(tm, cw, tps), _row_spec(tm, sw, tps)
    hc = _halo_spec(tm, cw, tps)
    return pl.pallas_call(
        body, name="mixer_merge_forward", grid=(nb, tps),
        out_shape=[jax.ShapeDtypeStruct((t, d), F32)] + [jax.ShapeDtypeStruct((t, d), BF16)] * 3,
        in_specs=[rd, rc, rc, rc, hc, hc, rd, rd, rw, rw, _mod_spec(d), _const(conv_w.shape), _const((1, sw)),
                  _const(wco.shape), _const(wglu.shape), _const(wso_t.shape), _const(wout.shape)],
        out_specs=[rd, rd, rd, rd],
        scratch_shapes=[pltpu.VMEM((tm + HALO, cw), F32)],
        compiler_params=_params(("arbitrary", "arbitrary")),
    )(x, gb, gc, v, gc, v, sga, sgb, yssm, u, mod, conv_w, dskip, wco, wglu, wso_t, wout)


def mixer_merge_backward(dxo, mix, ya, yb, gb, gc, v, sga, sgb, yssm, u, mod, conv_w, dskip,
                         wco, wglu, wso_t, wout, nb, carried=None):
    t, d = dxo.shape
    cw, sw = gb.shape[1], u.shape[1]
    s = t // nb
    tm = _tile(s, 256)
    tps = s // tm

    def body(dxo_ref, mix_ref, ya_ref, yb_ref, gb_ref, gc_ref, v_ref, gch_ref, vh_ref, sga_ref, sgb_ref, ys_ref,
             u_ref, mod_ref, cw_ref, dsk_ref, wco_ref, wglu_ref, wso_ref, wout_ref,
             dgla_ref, dglb_ref, dgb_ref, dconv_ref, ds0_ref, dgt_ref, ddsk_ref, dwout_ref, dwco_ref, dwso_ref,
             dwglu_ref, cv_s):
        bi, i = pl.program_id(0), pl.program_id(1)
        first = jnp.logical_and(bi == 0, i == 0)
        dxo = dxo_ref[...]
        _, _, gt = _mod_rows(mod_ref, 1)
        dmix = (gt * dxo).astype(BF16)
        _accumulate(dgt_ref, jnp.sum(dxo * mix_ref[...].astype(F32), axis=0, keepdims=True)[None], i == 0)
        ya, yb = ya_ref[...].astype(F32), yb_ref[...].astype(F32)
        sga, sgb = sga_ref[...].astype(F32), sgb_ref[...].astype(F32)
        merged = (sga * ya + sgb * yb).astype(BF16)
        _accumulate(dwout_ref, _dot(merged, dmix, TN), first)
        dmerged = _dot(dmix, wout_ref[...], NT)
        dgla_ref[...] = (dmerged * ya * sga * (1.0 - sga)).astype(BF16)
        dglb_ref[...] = (dmerged * yb * sgb * (1.0 - sgb)).astype(BF16)
        dya = (dmerged * sga).astype(BF16)
        dyb = (dmerged * sgb).astype(BF16)
        cv, cv1, cv2 = _conv_inputs(gc_ref, v_ref, gch_ref, vh_ref, cv_s, i, tm)
        w = cw_ref[...]
        conv = w[0:1] * cv2 + w[1:2] * cv1 + w[2:3] * cv
        gbv = gb_ref[...].astype(F32)
        _accumulate(dwco_ref, _dot((gbv * conv).astype(BF16), dya, TN), first)
        dya_in = _dot(dya, wco_ref[...], NT)
        dgb_ref[...] = (dya_in * conv).astype(BF16)
        dconv_ref[...] = dya_in * gbv
        uv = u_ref[...]
        s0 = ys_ref[...] + dsk_ref[...] * uv
        s1, th = _gelu(s0)
        s1b = s1.astype(BF16)
        sz = jax.nn.sigmoid(_dot(s1b, wglu_ref[...]))
        s2b = (s1 * sz).astype(BF16)
        _accumulate(dwso_ref, _dot(dyb, s2b, TN), first)
        ds2 = _dot(dyb, wso_ref[...])
        dz = (ds2 * s1 * sz * (1.0 - sz)).astype(BF16)
        _accumulate(dwglu_ref, _dot(s1b, dz, TN), first)
        ds1 = ds2 * sz + _dot(dz, wglu_ref[...], NT)
        ds0 = ds1 * _gelu_grad(s0, th)
        ds0_ref[...] = ds0
        _accumulate(ddsk_ref, jnp.sum(ds0 * uv, axis=0, keepdims=True), first)

    rd, rc, rw = _row_spec(tm, d, tps), _row_spec(tm, cw, tps), _row_spec(tm, sw, tps)
    hc = _halo_spec(tm, cw, tps)
    return _call(
        body, name="mixer_merge_backward", grid=(nb, tps), carried=carried,
        out_shape=[jax.ShapeDtypeStruct((t, d), BF16), jax.ShapeDtypeStruct((t, d), BF16),
                   jax.ShapeDtypeStruct((t, cw), BF16), jax.ShapeDtypeStruct((t, cw), F32),
                   jax.ShapeDtypeStruct((t, sw), F32), jax.ShapeDtypeStruct((nb, 1, d), F32),
                   jax.ShapeDtypeStruct((1, sw), F32), jax.ShapeDtypeStruct(wout.shape, F32),
                   jax.ShapeDtypeStruct(wco.shape, F32), jax.ShapeDtypeStruct(wso_t.shape, F32),
                   jax.ShapeDtypeStruct(wglu.shape, F32)],
        in_specs=[rd, rd, rd, rd, rc, rc, rc, hc, hc, rd, rd, rw, rw, _mod_spec(d), _const(conv_w.shape),
                  _const((1, sw)), _const(wco.shape), _const(wglu.shape), _const(wso_t.shape), _const(wout.shape)],
        out_specs=[rd, rd, rc, rc, rw, pl.BlockSpec((1, 1, d), lambda b, i: (b, 0, 0)), _full((1, sw)),
                   _full(wout.shape), _full(wco.shape), _full(wso_t.shape), _full(wglu.shape)],
        scratch_shapes=[pltpu.VMEM((tm + HALO, cw), F32)],
        args=(dxo, mix, ya, yb, gb, gc, v, gc, v, sga, sgb, yssm, u, mod, conv_w, dskip, wco, wglu, wso_t, wout))


def conv_backward(dconv, gc, v, conv_w, nb):
    t, cw = dconv.shape
    s = t // nb
    tm = _tile(s, 512)
    tps = s // tm
    per = tm // ROWS

    def body(dc_ref, dcn_ref, gc_ref, v_ref, gch_ref, vh_ref, cw_ref, dgc_ref, dv_ref, dw_ref, cv_s, dc_s):
        bi, i = pl.program_id(0), pl.program_id(1)
        cv, cv1, cv2 = _conv_inputs(gc_ref, v_ref, gch_ref, vh_ref, cv_s, i, tm)
        dc = dc_ref[...]
        dc_s[0:tm, :] = dc
        dc_s[tm:, :] = jnp.where(i == tps - 1, 0.0, dcn_ref[...])
        w = cw_ref[...]
        dcv = w[2:3] * dc + w[1:2] * dc_s[1:1 + tm, :] + w[0:1] * dc_s[2:2 + tm, :]
        dgc_ref[...] = (dcv * v_ref[...].astype(F32)).astype(BF16)
        dv_ref[...] = (dcv * gc_ref[...].astype(F32)).astype(BF16)
        dw = jnp.concatenate([jnp.sum(dc * cv2, axis=0, keepdims=True), jnp.sum(dc * cv1, axis=0, keepdims=True),
                              jnp.sum(dc * cv, axis=0, keepdims=True)], axis=0)
        _accumulate(dw_ref, dw, jnp.logical_and(bi == 0, i == 0))

    rc = _row_spec(tm, cw, tps)
    nxt = pl.BlockSpec((ROWS, cw), lambda b, i: (jnp.minimum((b * tps + i + 1) * per, nb * tps * per - 1), 0))
    hc = _halo_spec(tm, cw, tps)
    return pl.pallas_call(
        body, name="conv_backward", grid=(nb, tps),
        out_shape=[jax.ShapeDtypeStruct((t, cw), BF16), jax.ShapeDtypeStruct((t, cw), BF16),
                   jax.ShapeDtypeStruct(conv_w.shape, F32)],
        in_specs=[rc, nxt, rc, rc, hc, hc, _full(conv_w.shape)],
        out_specs=[rc, rc, _full(conv_w.shape)],
        scratch_shapes=[pltpu.VMEM((tm + HALO, cw), F32), pltpu.VMEM((tm + ROWS, cw), F32)],
        compiler_params=_params(("arbitrary", "arbitrary")),
    )(dconv, dconv, gc, v, gc, v, conv_w)


def loss_forward_backward(x, target, g):
    t, d = x.shape
    tm = _tile(t, 512)

    def body(x_ref, t_ref, g_ref, l_ref, dx_ref, dg_ref):
        i = pl.program_id(0)
        xv = x_ref[...]
        gv = g_ref[...]
        r = lax.rsqrt(jnp.mean(xv * xv, axis=-1, keepdims=True) + EPS)
        n = xv * r
        err = n * gv - t_ref[...]
        l_ref[...] = jnp.full(l_ref.shape, 0.5 * jnp.sum(jnp.mean(err * err, axis=-1)), F32)
        dy = err * (1.0 / d)
        dn = dy * gv
        dx_ref[...] = r * (dn - n * jnp.mean(n * dn, axis=-1, keepdims=True))
        _accumulate(dg_ref, jnp.sum(dy * n, axis=0, keepdims=True), i == 0)

    row = pl.BlockSpec((tm, d), lambda i: (i, 0))
    return pl.pallas_call(
        body, name="loss_forward_backward", grid=(t // tm,),
        out_shape=[jax.ShapeDtypeStruct((t // tm, 1, 128), F32), jax.ShapeDtypeStruct((t, d), F32),
                   jax.ShapeDtypeStruct((1, d), F32)],
        in_specs=[row, row, _full((1, d))],
        out_specs=[pl.BlockSpec((1, 1, 128), lambda i: (i, 0, 0)), row, _full((1, d))],
        compiler_params=_params(("arbitrary",)),
    )(x, target, g)


def sum_slots(slots, name):
    _, r, c = slots.shape
    tr = _tile(r, 352) if r % 352 == 0 else _tile(r, 256)

    def body(s_ref, o_ref):
        acc = s_ref[0].astype(F32)
        for j in range(1, N_DEV):
            acc = acc + s_ref[j].astype(F32)
        o_ref[...] = acc

    return pl.pallas_call(
        body, name=name, grid=(r // tr,),
        out_shape=jax.ShapeDtypeStruct((r, c), F32),
        in_specs=[pl.BlockSpec((N_DEV, tr, c), lambda i: (0, i, 0))],
        out_specs=pl.BlockSpec((tr, c), lambda i: (i, 0)),
        compiler_params=_params(("arbitrary",)),
    )(slots)


def adamw_update(w, g, m, v, name):
    r, c = w.shape
    tr = _tile(r, 256) if r % 8 == 0 else r

    def body(w_ref, g_ref, m_ref, v_ref, d_ref, mo_ref, vo_ref):
        d_ref[...], mo_ref[...], vo_ref[...] = _adamw(w_ref[...], g_ref[...], m_ref[...], v_ref[...])

    spec = pl.BlockSpec((tr, c), lambda i: (i, 0))
    return pl.pallas_call(
        body, name=name, grid=(r // tr,),
        out_shape=[jax.ShapeDtypeStruct((r, c), F32)] * 3,
        in_specs=[spec] * 4, out_specs=[spec] * 3,
        compiler_params=_params(("arbitrary",)),
    )(w, g, m, v)


def adamw_update_small(ws, gs, ms, vs):
    n = len(ws)

    def body(*refs):
        w_r, g_r, m_r, v_r = refs[:n], refs[n:2 * n], refs[2 * n:3 * n], refs[3 * n:4 * n]
        d_r, mo_r, vo_r = refs[4 * n:5 * n], refs[5 * n:6 * n], refs[6 * n:7 * n]
        for k in range(n):
            d_r[k][...], mo_r[k][...], vo_r[k][...] = _adamw(w_r[k][...], g_r[k][...], m_r[k][...], v_r[k][...])

    shapes = [jax.ShapeDtypeStruct(w.shape, F32) for w in ws]
    out = pl.pallas_call(body, name="adamw_update_small", out_shape=shapes * 3,
                         compiler_params=_params())(*ws, *gs, *ms, *vs)
    return out[:n], out[n:2 * n], out[2 * n:]


def _slots(grad_t):
    return grad_t.reshape(N_DEV, grad_t.shape[0] // N_DEV, grad_t.shape[1])


def kernel(x, c, w_ada, b_ada, g_ffn1, w1_a, w3_a, w2_a, g_mix, w_in, conv_w, w_conv_out, a_re, a_im, b_re, b_im, c_re, c_im, log_dt, d_skip, w_glu, w_ssm_out, w_out, g_ffn2, w1_b, w3_b, w2_b, g_final, loss_target, m_w_ada, m_b_ada, m_g_ffn1, m_w1_a, m_w3_a, m_w2_a, m_g_mix, m_w_in, m_conv_w, m_w_conv_out, m_a_re, m_a_im, m_b_re, m_b_im, m_c_re, m_c_im, m_log_dt, m_d_skip, m_w_glu, m_w_ssm_out, m_w_out, m_g_ffn2, m_w1_b, m_w3_b, m_w2_b, m_g_final, v_w_ada, v_b_ada, v_g_ffn1, v_w1_a, v_w3_a, v_w2_a, v_g_mix, v_w_in, v_conv_w, v_w_conv_out, v_a_re, v_a_im, v_b_re, v_b_im, v_c_re, v_c_im, v_log_dt, v_d_skip, v_w_glu, v_w_ssm_out, v_w_out, v_g_ffn2, v_w1_b, v_w3_b, v_w2_b, v_g_final):
    nb, s, d = x.shape
    t = nb * s
    me = 4 * lax.axis_index("x") + 2 * lax.axis_index("y") + lax.axis_index("c")
    g_n, p_n, h_n = b_re.shape[1:]
    cw_n = w_conv_out.shape[1] * N_DEV
    sw_n = w_ssm_out.shape[1]
    glu_fold = d // w_glu.shape[2]

    weights = dict(w_ada=w_ada, b_ada=b_ada, g_ffn1=g_ffn1, w1_a=w1_a, w3_a=w3_a, w2_a=w2_a, g_mix=g_mix, w_in=w_in,
                   conv_w=conv_w, w_conv_out=w_conv_out, a_re=a_re, a_im=a_im, b_re=b_re, b_im=b_im, c_re=c_re,
                   c_im=c_im, log_dt=log_dt, d_skip=d_skip, w_glu=w_glu, w_ssm_out=w_ssm_out, w_out=w_out,
                   g_ffn2=g_ffn2, w1_b=w1_b, w3_b=w3_b, w2_b=w2_b, g_final=g_final)
    mom1 = dict(w_ada=m_w_ada, b_ada=m_b_ada, g_ffn1=m_g_ffn1, w1_a=m_w1_a, w3_a=m_w3_a, w2_a=m_w2_a, g_mix=m_g_mix,
                w_in=m_w_in, conv_w=m_conv_w, w_conv_out=m_w_conv_out, a_re=m_a_re, a_im=m_a_im, b_re=m_b_re,
                b_im=m_b_im, c_re=m_c_re, c_im=m_c_im, log_dt=m_log_dt, d_skip=m_d_skip, w_glu=m_w_glu,
                w_ssm_out=m_w_ssm_out, w_out=m_w_out, g_ffn2=m_g_ffn2, w1_b=m_w1_b, w3_b=m_w3_b, w2_b=m_w2_b,
                g_final=m_g_final)
    mom2 = dict(w_ada=v_w_ada, b_ada=v_b_ada, g_ffn1=v_g_ffn1, w1_a=v_w1_a, w3_a=v_w3_a, w2_a=v_w2_a, g_mix=v_g_mix,
                w_in=v_w_in, conv_w=v_conv_w, w_conv_out=v_w_conv_out, a_re=v_a_re, a_im=v_a_im, b_re=v_b_re,
                b_im=v_b_im, c_re=v_c_re, c_im=v_c_im, log_dt=v_log_dt, d_skip=v_d_skip, w_glu=v_w_glu,
                w_ssm_out=v_w_ssm_out, w_out=v_w_out, g_ffn2=v_g_ffn2, w1_b=v_w1_b, w3_b=v_w3_b, w2_b=v_w2_b,
                g_final=v_g_final)
    names = list(weights)
    transposed = ("w1_a", "w3_a", "w_in", "w_ssm_out", "w1_b", "w3_b")
    groups = dict(ffn_a=("w1_a", "w3_a", "w2_a"), mixer=("w_in", "w_conv_out", "w_glu", "w_ssm_out", "w_out"),
                  ffn_b=("w1_b", "w3_b", "w2_b"))
    big = groups["ffn_a"] + groups["mixer"] + groups["ffn_b"]

    pad_rows = lambda a: jnp.pad(a, ((0, -a.shape[0] % ROWS), (0, 0)))
    c_all, conv_all = run_stage(GatherStage([pad_rows(c), pad_rows(conv_w[0])]), "gather_cond")
    c_all = c_all.reshape(N_DEV, -1, d)[:, :nb].reshape(N_DEV * nb, d)
    conv_full = conv_all.reshape(N_DEV, ROWS, -1)[:, :CONV_K].transpose(1, 0, 2).reshape(CONV_K, cw_n)
    ada_cols = w_ada.shape[2]
    b_cols = lax.dynamic_slice(b_ada, (0, me * ada_cols), (1, ada_cols))
    mod_cols = ada_forward(c_all, w_ada[0], b_cols)
    (mod_all,) = run_stage(GatherStage([mod_cols]), "gather_mod")
    mod_mine = lax.dynamic_slice(mod_all.reshape(N_DEV, N_DEV * nb, ada_cols), (0, me * nb, 0), (N_DEV, nb, ada_cols))
    mod = mod_mine.transpose(1, 0, 2).reshape(nb, N_MOD, d)

    def shard_rows(name):
        w = weights[name][0]
        if name in transposed:
            w = w.T
        if name == "w_glu":
            w = w.reshape(w.shape[0] // glu_fold, d)
        return w.astype(BF16)

    def gather_stage(group):
        return GatherStage([shard_rows(n) for n in groups[group]])

    gw = {}

    def keep_weights(group, outs):
        for n, w in zip(groups[group], outs):
            gw[n] = w.reshape(sw_n, sw_n) if n == "w_glu" else w

    disc_in = (a_re[0], a_im[0], b_re[0], b_im[0], log_dt[0])
    (abr, abi, bbr, bbi), disc_vjp = jax.vjp(ssm_discretise, *disc_in)
    bd, cd, abar8, ml, mlb = ssm_tables(abr, abi, bbr, bbi, c_re[0], c_im[0], _tile(s, 256) // ROWS)

    x0 = x.reshape(t, d)
    keep_weights("ffn_a", run_stage(gather_stage("ffn_a"), "gather_ffn_a"))
    (x1, a1, b1, f1), got = ffn_forward(x0, mod, g_ffn1, gw["w1_a"], gw["w3_a"], gw["w2_a"].T, 0, nb,
                                        "ffn_a_forward", carried=gather_stage("mixer"))
    keep_weights("mixer", got)
    (gb, gc, vv, u, sga, sgb, h2), got = mixer_proj_forward(x1, mod, g_mix, gw["w_in"], cw_n, sw_n, nb,
                                                            carried=gather_stage("ffn_b"))
    keep_weights("ffn_b", got)
    yssm, st = ssm_forward(u, bd, cd, abar8, ml, nb)
    x2, ya, yb, mix = mixer_merge_forward(x1, gb, gc, vv, sga, sgb, yssm, u, mod, conv_full, d_skip,
                                          gw["w_conv_out"], gw["w_glu"], gw["w_ssm_out"], gw["w_out"], nb)
    (x3, a3, b3, f3), _ = ffn_forward(x2, mod, g_ffn2, gw["w1_b"], gw["w3_b"], gw["w2_b"].T, 2, nb, "ffn_b_forward")
    loss_parts, dx3, dg_final = loss_forward_backward(x3, loss_target.reshape(t, d), g_final.reshape(1, d))
    loss = lax.psum(jnp.sum(loss_parts[:, 0, 0]), ("x", "y", "c"))

    part, received = {}, {}

    def exchange_stage(ns):
        return ExchangeStage([_slots(part[n]) for n in ns])

    (da3, db3, sw3, df3, dgt3), _ = ffn_backward_hidden(dx3, a3, b3, f3, mod, gw["w2_b"], 2, nb,
                                                        "ffn_b_backward_hidden")
    (dx2, h3, dmod3, dg_ffn2), _ = ffn_backward_input(dx3, x2, da3, db3, mod, g_ffn2, gw["w1_b"].T, gw["w3_b"].T,
                                                      2, nb, "ffn_b_backward_input")
    (part["w1_b"],), _ = nn_matmul(da3, h3, "grad_w1_b")
    (part["w3_b"],), _ = nn_matmul(db3, h3, "grad_w3_b")
    (part["w2_b"],), _ = nn_matmul(sw3, df3, "grad_w2_b")
    (dgla, dglb, dgb, dconv, ds0, dgt2, dd_skip, dw_out, dw_co, dw_so_t, dw_glu), got = mixer_merge_backward(
        dx2, mix, ya, yb, gb, gc, vv, sga, sgb, yssm, u, mod, conv_full, d_skip,
        gw["w_conv_out"], gw["w_glu"], gw["w_ssm_out"], gw["w_out"], nb, carried=exchange_stage(groups["ffn_b"]))
    received.update(zip(groups["ffn_b"], got))
    part["w_out"] = dw_out.astype(BF16)
    part["w_conv_out"] = dw_co.astype(BF16)
    part["w_ssm_out"] = dw_so_t.astype(BF16)
    part["w_glu"] = dw_glu.reshape(sw_n // glu_fold, d).astype(BF16)
    (du, dab, dbd, dcd), got = ssm_backward(u, ds0, st, bd, cd, abar8, mlb, d_skip, nb,
                                            carried=exchange_stage(groups["mixer"][1:]))
    received.update(zip(groups["mixer"][1:], got))
    dgc, dvv, dconv_w = conv_backward(dconv, gc, vv, conv_full, nb)
    part["w_in"] = jnp.concatenate([tn_matmul(p, h2, "grad_w_in_%d" % k)
                                    for k, p in enumerate((dgb, dgc, dvv, du, dgla, dglb))], axis=0)
    (dx1, dmod2, dg_mix), got = mixer_proj_backward(dgb, dgc, dvv, du, dgla, dglb, dx2, x1, mod, g_mix, gw["w_in"], nb,
                                                    carried=exchange_stage(("w_in",)))
    received["w_in"] = got[0]
    (da1, db1, sw1, df1, dgt1), _ = ffn_backward_hidden(dx1, a1, b1, f1, mod, gw["w2_a"], 0, nb,
                                                        "ffn_a_backward_hidden")
    (part["w2_a"],), _ = nn_matmul(sw1, df1, "grad_w2_a")
    (dx0, h1, dmod1, dg_ffn1), _ = ffn_backward_input(dx1, x0, da1, db1, mod, g_ffn1, gw["w1_a"].T, gw["w3_a"].T,
                                                      0, nb, "ffn_a_backward_input")
    (part["w1_a"],), got = nn_matmul(da1, h1, "grad_w1_a", carried=exchange_stage(("w2_a",)))
    received["w2_a"] = got[0]
    (part["w3_a"],), got = nn_matmul(db1, h1, "grad_w3_a", carried=exchange_stage(("w1_a",)))
    received["w1_a"] = got[0]
    (received["w3_a"],) = run_stage(exchange_stage(("w3_a",)), "exchange_w3_a")

    dabr, dabi, dbbr, dbbi, dcr, dci = ssm_table_grads(dab, dbd, dcd, g_n, p_n, h_n)
    gmod = jnp.concatenate([dmod1, dgt1, dmod2, dgt2, dmod3, dgt3], axis=1)
    small = dict(gmod=gmod, g_ffn1=dg_ffn1, g_mix=dg_mix, g_ffn2=dg_ffn2, g_final=dg_final, d_skip=dd_skip,
                 abr=dabr, abi=dabi, bbr=dbbr, bbi=dbbi, c_re=dcr, c_im=dci, conv_w=dconv_w)
    flat = jnp.concatenate([a.reshape(-1) for a in small.values()])
    n_small = flat.shape[0]
    n_rows = -(-n_small // (128 * ROWS)) * ROWS
    flat = jnp.pad(flat, (0, n_rows * 128 - n_small)).reshape(n_rows, 128)
    (small_all,) = run_stage(GatherStage([flat]), "gather_small_grads")
    small_all = small_all.reshape(N_DEV, n_rows, 128)
    total = sum_slots(small_all, "sum_small_grads").reshape(-1)
    tot, off = {}, 0
    for key, like in small.items():
        n = math.prod(like.shape)
        tot[key], off = total[off:off + n].reshape(like.shape), off + n
    gmod_all = small_all.reshape(N_DEV, n_rows * 128)[:, :nb * N_MOD * d].reshape(N_DEV * nb, N_MOD * d)
    g_a_re, g_a_im, g_b_re, g_b_im, g_log_dt = disc_vjp((tot["abr"], tot["abi"], tot["bbr"], tot["bbi"]))

    grads = {}
    grads["b_ada"] = sum_rows(tot["gmod"].reshape(nb, N_MOD * d))
    grads["g_ffn1"], grads["g_mix"], grads["g_ffn2"] = tot["g_ffn1"], tot["g_mix"], tot["g_ffn2"]
    grads["g_final"] = tot["g_final"].reshape(d)
    grads["d_skip"] = tot["d_skip"]
    grads["a_re"], grads["a_im"], grads["log_dt"] = g_a_re[None], g_a_im[None], g_log_dt[None]
    grads["b_re"], grads["b_im"] = g_b_re[None], g_b_im[None]
    grads["c_re"], grads["c_im"] = tot["c_re"][None], tot["c_im"][None]
    grads["conv_w"] = lax.dynamic_slice(tot["conv_w"], (0, me * conv_w.shape[2]), (CONV_K, conv_w.shape[2]))[None]

    delta, new_m, new_v = {}, {}, {}
    for name in big:
        gsum = sum_slots(received[name], "sum_" + name)
        if name == "w_glu":
            gsum = gsum.reshape(w_glu.shape[1], w_glu.shape[2])
        if name in transposed:
            gsum = gsum.T
        grads[name] = gsum[None]
        dl, mm, vn = adamw_update(weights[name][0], gsum, mom1[name][0], mom2[name][0], "adamw_" + name)
        delta[name], new_m[name], new_v[name] = dl[None], mm[None], vn[None]

    gmod_cols = lax.dynamic_slice(gmod_all, (0, me * ada_cols), (N_DEV * nb, ada_cols))
    g_wada, d_wada, m_wada, v_wada = ada_backward_update(c_all, gmod_cols, w_ada[0], m_w_ada[0], v_w_ada[0])
    grads["w_ada"], delta["w_ada"], new_m["w_ada"], new_v["w_ada"] = g_wada[None], d_wada[None], m_wada[None], v_wada[None]

    small_names = [n for n in names if n not in big and n != "w_ada"]

    def as2d(a):
        return a.reshape(-1, a.shape[-1])

    sw_, sg_, sm_, sv_ = ([as2d(src[n]) for n in small_names] for src in (weights, grads, mom1, mom2))
    sd, smo, svo = adamw_update_small(sw_, sg_, sm_, sv_)
    for n, dl, mm, vn in zip(small_names, sd, smo, svo):
        shape = weights[n].shape
        grads[n] = grads[n].reshape(shape)
        delta[n], new_m[n], new_v[n] = dl.reshape(shape), mm.reshape(shape), vn.reshape(shape)

    grad_x = dx0.reshape(nb, s, d)
    return (loss, grad_x, *[grads[n] for n in names], *[delta[n] for n in names],
            *[new_m[n] for n in names], *[new_v[n] for n in names])


def sum_rows(a):
    r, c = a.shape

    def body(a_ref, o_ref):
        acc = a_ref[0:1, :]
        for j in range(1, r):
            acc = acc + a_ref[j:j + 1, :]
        o_ref[...] = acc

    return pl.pallas_call(body, name="sum_rows", out_shape=jax.ShapeDtypeStruct((1, c), F32),
                          compiler_params=_params())(a)
```

```python
import functools
import math

import jax
import jax.numpy as jnp
from jax import lax
from jax.experimental import pallas as pl
from jax.experimental.pallas import tpu as pltpu

F32 = jnp.float32
BF16 = jnp.bfloat16
N_DEV = 8
N_MOD = 9
EPS = 1e-6
CONV_K = 3
ADAM_LR = 0.001
ADAM_B1 = 0.9
ADAM_B2 = 0.999
ADAM_EPS = 1e-08
ADAM_WD = 0.01
ADAM_STEP = 10
GELU_C0 = math.sqrt(2.0 / math.pi)
GELU_C1 = 0.044715
V7X_VMEM_LIMIT = 56 * 1024 * 1024
MESH_ID = pl.DeviceIdType.MESH
NT = (((1,), (1,)), ((), ()))
TN = (((0,), (0,)), ((), ()))


def _dot(a, b, dims=None):
    if dims is None:
        return jnp.dot(a, b, preferred_element_type=F32)
    return lax.dot_general(a, b, dims, preferred_element_type=F32)


def _params(sem=None, vmem=V7X_VMEM_LIMIT):
    return pltpu.CompilerParams(dimension_semantics=sem, vmem_limit_bytes=vmem)


def _full(shape):
    return pl.BlockSpec(shape, lambda *_: (0,) * len(shape))


def _const(shape):
    return pl.BlockSpec(shape, lambda *_: (0,) * len(shape), pipeline_mode=pl.Buffered(1))


def _tile(n, want):
    t = min(n, want)
    while n % t:
        t //= 2
    return t


class GatherStage:
    def __init__(self, shards):
        n = len(shards)
        self.inputs = list(shards)
        self.out_shape = [jax.ShapeDtypeStruct((N_DEV * s.shape[0], s.shape[1]), s.dtype) for s in shards]
        self.scratch = [pltpu.SemaphoreType.DMA((7 * n,)), pltpu.SemaphoreType.DMA((7 * n,)),
                        pltpu.SemaphoreType.DMA((n,))]

    def _plan(self, ins, outs, sems):
        send_sems, recv_sems, local_sems = sems
        n = len(ins)
        x, y, c = lax.axis_index("x"), lax.axis_index("y"), lax.axis_index("c")
        me, sibling = (x, y, c), (x, y, 1 - c)
        chips = [(1 - x, y), (x, 1 - y), (1 - x, 1 - y)]

        def rows(k, px, py, pc):
            r = ins[k].shape[0]
            return outs[k].at[pl.ds((4 * px + 2 * py + pc) * r, r), :]

        def copy(k, j, block, to, src=None):
            return pltpu.make_async_remote_copy(
                src_ref=rows(k, *block) if src is None else src, dst_ref=rows(k, *block),
                send_sem=send_sems.at[7 * k + j], recv_sem=recv_sems.at[7 * k + j],
                device_id=to, device_id_type=MESH_ID)

        mine = [pltpu.make_async_copy(ins[k], rows(k, *me), local_sems.at[k]) for k in range(n)]
        first = []
        for k in range(n):
            first.append(copy(k, 0, me, sibling, src=ins[k]))
            first += [copy(k, 1 + j, me, (*chip, c), src=ins[k]) for j, chip in enumerate(chips)]
        return n, c, me, sibling, chips, copy, mine, first

    def start(self, ins, outs, sems):
        *_, mine, first = self._plan(ins, outs, sems)
        for cp in mine + first:
            cp.start()

    def finish(self, ins, outs, sems):
        n, c, me, sibling, chips, copy, mine, first = self._plan(ins, outs, sems)
        passed = []
        for j, chip in enumerate(chips):
            for k in range(n):
                copy(k, 1 + j, (*chip, c), me).wait_recv()
                cp = copy(k, 4 + j, (*chip, c), sibling)
                cp.start()
                passed.append(cp)
        for k in range(n):
            copy(k, 0, sibling, me).wait_recv()
            for j, chip in enumerate(chips):
                copy(k, 4 + j, (*chip, 1 - c), me).wait_recv()
        for cp in first + passed:
            cp.wait_send()
        for cp in mine:
            cp.wait()


class ExchangeStage:
    def __init__(self, bufs):
        n = len(bufs)
        self.inputs = list(bufs)
        self.out_shape = [jax.ShapeDtypeStruct(b.shape, b.dtype) for b in bufs]
        self.scratch = [pltpu.SemaphoreType.DMA((7 * n,)), pltpu.SemaphoreType.DMA((7 * n,)),
                        pltpu.SemaphoreType.DMA((n,))]

    def _plan(self, ins, outs, sems):
        send_sems, recv_sems, local_sems = sems
        n = len(ins)
        x, y, c = lax.axis_index("x"), lax.axis_index("y"), lax.axis_index("c")
        me = 4 * x + 2 * y + c
        mine = [pltpu.make_async_copy(ins[k].at[me], outs[k].at[me], local_sems.at[k]) for k in range(n)]
        copies = []
        for mask in range(1, N_DEV):
            px, py, pc = x ^ (mask >> 2), y ^ ((mask >> 1) & 1), c ^ (mask & 1)
            for k in range(n):
                copies.append(pltpu.make_async_remote_copy(
                    src_ref=ins[k].at[4 * px + 2 * py + pc], dst_ref=outs[k].at[me],
                    send_sem=send_sems.at[7 * k + mask - 1], recv_sem=recv_sems.at[7 * k + mask - 1],
                    device_id=(px, py, pc), device_id_type=MESH_ID))
        return mine, copies

    def start(self, ins, outs, sems):
        mine, copies = self._plan(ins, outs, sems)
        for cp in mine + copies:
            cp.start()

    def finish(self, ins, outs, sems):
        mine, copies = self._plan(ins, outs, sems)
        for cp in copies:
            cp.wait_recv()
        for cp in copies:
            cp.wait_send()
        for cp in mine:
            cp.wait()


ANY_SPEC = pl.BlockSpec(memory_space=pl.ANY)


def run_stage(stage, name):
    ci, co = len(stage.inputs), len(stage.out_shape)

    def body(*refs):
        ins, outs, sems = refs[:ci], refs[ci:ci + co], refs[ci + co:]
        stage.start(ins, outs, sems)
        stage.finish(ins, outs, sems)

    return pl.pallas_call(body, name=name, out_shape=stage.out_shape, in_specs=[ANY_SPEC] * ci,
                          out_specs=[ANY_SPEC] * co, scratch_shapes=stage.scratch)(*stage.inputs)


def _call(body, *, name, grid, in_specs, out_specs, out_shape, args, scratch_shapes=(), carried=None):
    sem = ("arbitrary",) * len(grid)
    if carried is None:
        return pl.pallas_call(body, name=name, grid=grid, in_specs=list(in_specs), out_specs=list(out_specs),
                              out_shape=list(out_shape), scratch_shapes=list(scratch_shapes),
                              compiler_params=_params(sem))(*args), None
    ni, no, ns = len(in_specs), len(out_shape), len(scratch_shapes)
    ci, co = len(carried.inputs), len(carried.out_shape)

    def wrapped(*refs):
        ins, refs = refs[:ni], refs[ni:]
        cins, refs = refs[:ci], refs[ci:]
        outs, refs = refs[:no], refs[no:]
        couts, refs = refs[:co], refs[co:]
        scr, csems = refs[:ns], refs[ns:]
        ids = [pl.program_id(k) for k in range(len(grid))]
        first = functools.reduce(jnp.logical_and, [i == 0 for i in ids])
        last = functools.reduce(jnp.logical_and, [i == g - 1 for i, g in zip(ids, grid)])

        @pl.when(first)
        def _():
            carried.start(cins, couts, csems)

        body(*ins, *outs, *scr)

        @pl.when(last)
        def _():
            carried.finish(cins, couts, csems)

    res = pl.pallas_call(
        wrapped, name=name, grid=grid, in_specs=list(in_specs) + [ANY_SPEC] * ci,
        out_specs=list(out_specs) + [ANY_SPEC] * co, out_shape=list(out_shape) + carried.out_shape,
        scratch_shapes=list(scratch_shapes) + carried.scratch, compiler_params=_params(sem),
    )(*args, *carried.inputs)
    return res[:no], res[no:]


def _norm_mod(x, g, shift, scale):
    r = lax.rsqrt(jnp.mean(x * x, axis=-1, keepdims=True) + EPS)
    n = x * r
    return (n * g) * (1.0 + scale) + shift, n, r


def _norm_mod_bwd(dh, n, r, g, scale):
    dsh = jnp.sum(dh, axis=0, keepdims=True)
    dsc = jnp.sum(dh * (n * g), axis=0, keepdims=True)
    dg = jnp.sum(dh * (1.0 + scale) * n, axis=0, keepdims=True)
    dn = dh * ((1.0 + scale) * g)
    dx = r * (dn - n * jnp.mean(n * dn, axis=-1, keepdims=True))
    return dx, dsh, dsc, dg


def _mod_rows(mod_ref, sub):
    m = mod_ref[0]
    return m[3 * sub:3 * sub + 1], m[3 * sub + 1:3 * sub + 2], m[3 * sub + 2:3 * sub + 3]


def _gelu(x):
    t = jnp.tanh(GELU_C0 * (x + GELU_C1 * x * x * x))
    return 0.5 * x * (1.0 + t), t


def _gelu_grad(x, t):
    return 0.5 * (1.0 + t) + 0.5 * x * (1.0 - t * t) * (GELU_C0 * (1.0 + 3.0 * GELU_C1 * x * x))


def _zero_when(cond, *refs):
    @pl.when(cond)
    def _():
        for r in refs:
            r[...] = jnp.zeros_like(r)


def ada_forward(c_all, w_ada, b_ada_cols):
    def body(c_ref, w_ref, b_ref, o_ref):
        c = c_ref[...]
        cond = (c * jax.nn.sigmoid(c)).astype(BF16)
        o_ref[...] = _dot(cond, w_ref[...].astype(BF16)) + b_ref[...]

    nb, d = c_all.shape
    cols = w_ada.shape[1]
    tn = _tile(cols, 384)
    return pl.pallas_call(
        body, name="ada_forward", grid=(cols // tn,),
        out_shape=jax.ShapeDtypeStruct((nb, cols), F32),
        in_specs=[_full((nb, d)), pl.BlockSpec((d, tn), lambda j: (0, j)), pl.BlockSpec((1, tn), lambda j: (0, j))],
        out_specs=pl.BlockSpec((nb, tn), lambda j: (0, j)),
        compiler_params=_params(("arbitrary",)),
    )(c_all, w_ada, b_ada_cols)


def _adamw(w, g, m, v):
    m = ADAM_B1 * m + (1.0 - ADAM_B1) * g
    v = ADAM_B2 * v + (1.0 - ADAM_B2) * (g * g)
    m_hat = m / (1.0 - ADAM_B1 ** ADAM_STEP)
    v_hat = v / (1.0 - ADAM_B2 ** ADAM_STEP)
    delta = -ADAM_LR * (m_hat / (jnp.sqrt(v_hat) + ADAM_EPS) + ADAM_WD * w)
    return delta, m, v


def ada_backward_update(c_all, gmod_cols, w, m, v):
    def body(c_ref, g_ref, w_ref, m_ref, v_ref, go_ref, d_ref, mo_ref, vo_ref):
        c = c_ref[...]
        cond = (c * jax.nn.sigmoid(c)).astype(BF16)
        g = _dot(cond, g_ref[...].astype(BF16), TN)
        go_ref[...] = g
        d_ref[...], mo_ref[...], vo_ref[...] = _adamw(w_ref[...], g, m_ref[...], v_ref[...])

    nb, d = c_all.shape
    cols = w.shape[1]
    tn = _tile(cols, 128)
    col = pl.BlockSpec((d, tn), lambda j: (0, j))
    return pl.pallas_call(
        body, name="ada_backward_update", grid=(cols // tn,),
        out_shape=[jax.ShapeDtypeStruct(w.shape, F32)] * 4,
        in_specs=[_full((nb, d)), pl.BlockSpec((nb, tn), lambda j: (0, j)), col, col, col],
        out_specs=[col] * 4,
        compiler_params=_params(("arbitrary",)),
    )(c_all, gmod_cols, w, m, v)


def _row_spec(tm, width, tiles_per_seq):
    return pl.BlockSpec((tm, width), lambda b, i: (b * tiles_per_seq + i, 0))


def _mod_spec(d):
    return pl.BlockSpec((1, N_MOD, d), lambda b, i: (b, 0, 0))


def _col_spec(rows, tm, tiles_per_seq):
    return pl.BlockSpec((rows, tm), lambda b, i: (0, b * tiles_per_seq + i))


def _ffn_chunk(f):
    return f // 2 if f % 256 == 0 and f > 1536 else f


def ffn_forward(x, mod, g, w1t, w3t, w2, sub, nb, name, carried=None):
    t, d = x.shape
    f = w1t.shape[0]
    s = t // nb
    tm = _tile(s, 512)
    fc = _ffn_chunk(f)

    def body(x_ref, mod_ref, g_ref, w1_ref, w3_ref, w2_ref, xo_ref, a_ref, b_ref, f_ref):
        xv = x_ref[...]
        sh, sc, gt = _mod_rows(mod_ref, sub)
        h, _, _ = _norm_mod(xv, g_ref[...], sh, sc)
        hb = h.astype(BF16)
        acc_t = jnp.zeros((d, tm), F32)
        for k in range(f // fc):
            rows = slice(k * fc, (k + 1) * fc)
            a = _dot(w1_ref[rows, :], hb, NT)
            b = _dot(w3_ref[rows, :], hb, NT)
            a_ref[rows, :] = a.astype(BF16)
            b_ref[rows, :] = b.astype(BF16)
            sw = (a * jax.nn.sigmoid(a)) * b
            acc_t = acc_t + _dot(w2_ref[rows, :], sw.astype(BF16), TN)
        acc = acc_t.T
        f_ref[...] = acc.astype(BF16)
        xo_ref[...] = xv + (0.5 * gt) * acc

    tps = s // tm
    rd, cf = _row_spec(tm, d, tps), _col_spec(f, tm, tps)
    return _call(
        body, name=name, grid=(nb, tps), carried=carried,
        out_shape=[jax.ShapeDtypeStruct((t, d), F32), jax.ShapeDtypeStruct((f, t), BF16),
                   jax.ShapeDtypeStruct((f, t), BF16), jax.ShapeDtypeStruct((t, d), BF16)],
        in_specs=[rd, _mod_spec(d), _const((1, d)), _const((f, d)), _const((f, d)), _const((f, d))],
        out_specs=[rd, cf, cf, rd],
        args=(x, mod, g, w1t, w3t, w2))


def ffn_backward_hidden(dxo, a_t, b_t, fo, mod, w2, sub, nb, name, carried=None):
    t, d = dxo.shape
    f = w2.shape[0]
    s = t // nb
    tm = _tile(s, 512)
    fc = _tile(f, 704) if f % 704 == 0 else _tile(f, 512)

    def body(dxo_ref, a_ref, b_ref, f_ref, mod_ref, w2_ref, da_ref, db_ref, s_ref, df_ref, dgt_ref):
        _zero_when(pl.program_id(1) == 0, dgt_ref)
        dxo = dxo_ref[...]
        _, _, gt = _mod_rows(mod_ref, sub)
        dfb = ((0.5 * gt) * dxo).astype(BF16)
        df_ref[...] = dfb
        dgt_ref[...] += 0.5 * jnp.sum(dxo * f_ref[...].astype(F32), axis=0, keepdims=True)[None]
        for k in range(f // fc):
            rows = slice(k * fc, (k + 1) * fc)
            ds = _dot(w2_ref[rows, :], dfb, NT)
            av = a_ref[rows, :].astype(F32)
            bv = b_ref[rows, :].astype(F32)
            sig = jax.nn.sigmoid(av)
            sl = av * sig
            da_ref[rows, :] = (ds * bv * (sig * (1.0 + av * (1.0 - sig)))).astype(BF16)
            db_ref[rows, :] = (ds * sl).astype(BF16)
            s_ref[rows, :] = (sl * bv).astype(BF16)

    tps = s // tm
    rd, cf = _row_spec(tm, d, tps), _col_spec(f, tm, tps)
    return _call(
        body, name=name, grid=(nb, tps), carried=carried,
        out_shape=[jax.ShapeDtypeStruct((f, t), BF16)] * 3
        + [jax.ShapeDtypeStruct((t, d), BF16), jax.ShapeDtypeStruct((nb, 1, d), F32)],
        in_specs=[rd, cf, cf, rd, _mod_spec(d), _const((f, d))],
        out_specs=[cf, cf, cf, rd, pl.BlockSpec((1, 1, d), lambda b, i: (b, 0, 0))],
        args=(dxo, a_t, b_t, fo, mod, w2))


def ffn_backward_input(dxo, x, da_t, db_t, mod, g, w1t, w3t, sub, nb, name, carried=None):
    t, d = x.shape
    f = w1t.shape[0]
    s = t // nb
    tm = _tile(s, 512)

    def body(dxo_ref, x_ref, da_ref, db_ref, mod_ref, g_ref, w1_ref, w3_ref, dx_ref, h_ref, dmod_ref, dg_ref):
        bi, i = pl.program_id(0), pl.program_id(1)
        _zero_when(i == 0, dmod_ref)
        _zero_when(jnp.logical_and(bi == 0, i == 0), dg_ref)
        sh, sc, _ = _mod_rows(mod_ref, sub)
        gv = g_ref[...]
        h, n, r = _norm_mod(x_ref[...], gv, sh, sc)
        h_ref[...] = h.astype(BF16)
        dh_t = _dot(w1_ref[...], da_ref[...], TN) + _dot(w3_ref[...], db_ref[...], TN)
        dxn, dsh, dsc, dg = _norm_mod_bwd(dh_t.T, n, r, gv, sc)
        dx_ref[...] = dxo_ref[...] + dxn
        dmod_ref[...] += jnp.concatenate([dsh, dsc], axis=0)[None]
        dg_ref[...] += dg

    tps = s // tm
    rd, cf = _row_spec(tm, d, tps), _col_spec(f, tm, tps)
    return _call(
        body, name=name, grid=(nb, tps), carried=carried,
        out_shape=[jax.ShapeDtypeStruct((t, d), F32), jax.ShapeDtypeStruct((t, d), BF16),
                   jax.ShapeDtypeStruct((nb, 2, d), F32), jax.ShapeDtypeStruct((1, d), F32)],
        in_specs=[rd, rd, cf, cf, _mod_spec(d), _const((1, d)), _const((f, d)), _const((f, d))],
        out_specs=[rd, rd, pl.BlockSpec((1, 2, d), lambda b, i: (b, 0, 0)), _full((1, d))],
        args=(dxo, x, da_t, db_t, mod, g, w1t, w3t))


def nn_matmul(lhs_t, rhs, name, carried=None):
    m, t = lhs_t.shape
    n = rhs.shape[1]
    tk = _tile(t, 2048)
    tmm = m if m <= 1536 else m // 2
    nk = t // tk

    def body(a_ref, b_ref, o_ref, acc_ref):
        k = pl.program_id(1)
        _zero_when(k == 0, acc_ref)
        acc_ref[...] += _dot(a_ref[...], b_ref[...])

        @pl.when(k == nk - 1)
        def _():
            o_ref[...] = acc_ref[...].astype(BF16)

    return _call(
        body, name=name, grid=(m // tmm, nk), carried=carried,
        out_shape=[jax.ShapeDtypeStruct((m, n), BF16)],
        in_specs=[pl.BlockSpec((tmm, tk), lambda j, k: (j, k)), pl.BlockSpec((tk, n), lambda j, k: (k, 0))],
        out_specs=[pl.BlockSpec((tmm, n), lambda j, k: (j, 0))],
        scratch_shapes=[pltpu.VMEM((tmm, n), F32)],
        args=(lhs_t, rhs))


def tn_matmul(lhs, rhs, name):
    t, m = lhs.shape
    n = rhs.shape[1]
    tk = _tile(t, 2048)
    nk = t // tk

    def body(a_ref, b_ref, o_ref, acc_ref):
        k = pl.program_id(0)
        _zero_when(k == 0, acc_ref)
        acc_ref[...] += _dot(a_ref[...], b_ref[...], TN)

        @pl.when(k == nk - 1)
        def _():
            o_ref[...] = acc_ref[...].astype(BF16)

    return pl.pallas_call(
        body, name=name, grid=(nk,),
        out_shape=jax.ShapeDtypeStruct((m, n), BF16),
        in_specs=[pl.BlockSpec((tk, m), lambda k: (k, 0)), pl.BlockSpec((tk, n), lambda k: (k, 0))],
        out_specs=pl.BlockSpec((m, n), lambda k: (0, 0)),
        scratch_shapes=[pltpu.VMEM((m, n), F32)],
        compiler_params=_params(("arbitrary",)),
    )(lhs, rhs)


def mixer_proj_forward(x, mod, g, w_in_t, cw, sw, nb, carried=None):
    t, d = x.shape
    s = t // nb
    tm = _tile(s, 512)
    pieces = [(0, cw, "bf16"), (cw, cw, "bf16"), (2 * cw, cw, "bf16"), (3 * cw, sw, "f32"),
              (3 * cw + sw, d, "sig"), (3 * cw + sw + d, d, "sig")]

    def body(x_ref, mod_ref, g_ref, w_ref, *outs):
        h_ref = outs[-1]
        sh, sc, _ = _mod_rows(mod_ref, 1)
        h, _, _ = _norm_mod(x_ref[...], g_ref[...], sh, sc)
        hb = h.astype(BF16)
        h_ref[...] = hb
        for (off, width, kind), o_ref in zip(pieces, outs[:-1]):
            ck = _tile(width, 512)
            for j in range(width // ck):
                p = _dot(hb, w_ref[off + j * ck:off + (j + 1) * ck, :], NT)
                if kind == "sig":
                    p = jax.nn.sigmoid(p)
                o_ref[:, j * ck:(j + 1) * ck] = p.astype(o_ref.dtype)

    tps = s // tm
    widths = [(cw, BF16), (cw, BF16), (cw, BF16), (sw, F32), (d, BF16), (d, BF16), (d, BF16)]
    return _call(
        body, name="mixer_proj_forward", grid=(nb, tps), carried=carried,
        out_shape=[jax.ShapeDtypeStruct((t, w), dt) for w, dt in widths],
        in_specs=[_row_spec(tm, d, tps), _mod_spec(d), _const((1, d)), _const(w_in_t.shape)],
        out_specs=[_row_spec(tm, w, tps) for w, _ in widths],
        args=(x, mod, g, w_in_t))


def mixer_proj_backward(dgb, dgc, dv, du, dgla, dglb, dxo, x, mod, g, w_in_t, nb, carried=None):
    t, d = x.shape
    s = t // nb
    tm = _tile(s, 512)
    parts = [dgb, dgc, dv, du, dgla, dglb]
    offs = [0]
    for p in parts:
        offs.append(offs[-1] + p.shape[1])

    def body(*refs):
        p_refs = refs[:6]
        dxo_ref, x_ref, mod_ref, g_ref, w_ref, dx_ref, dmod_ref, dg_ref = refs[6:]
        bi, i = pl.program_id(0), pl.program_id(1)
        _zero_when(i == 0, dmod_ref)
        _zero_when(jnp.logical_and(bi == 0, i == 0), dg_ref)
        dh = jnp.zeros((tm, d), F32)
        for p_ref, off in zip(p_refs, offs):
            width = p_ref.shape[1]
            ck = _tile(width, 512)
            for j in range(width // ck):
                dh = dh + _dot(p_ref[:, j * ck:(j + 1) * ck], w_ref[off + j * ck:off + (j + 1) * ck, :])
        sh, sc, _ = _mod_rows(mod_ref, 1)
        gv = g_ref[...]
        _, n, r = _norm_mod(x_ref[...], gv, sh, sc)
        dxn, dsh, dsc, dg = _norm_mod_bwd(dh, n, r, gv, sc)
        dx_ref[...] = dxo_ref[...] + dxn
        dmod_ref[...] += jnp.concatenate([dsh, dsc], axis=0)[None]
        dg_ref[...] += dg

    tps = s // tm
    rd = _row_spec(tm, d, tps)
    return _call(
        body, name="mixer_proj_backward", grid=(nb, tps), carried=carried,
        out_shape=[jax.ShapeDtypeStruct((t, d), F32), jax.ShapeDtypeStruct((nb, 2, d), F32),
                   jax.ShapeDtypeStruct((1, d), F32)],
        in_specs=[_row_spec(tm, p.shape[1], tps) for p in parts]
        + [rd, rd, _mod_spec(d), _const((1, d)), _const(w_in_t.shape)],
        out_specs=[rd, pl.BlockSpec((1, 2, d), lambda b, i: (b, 0, 0)), _full((1, d))],
        args=(*parts, dxo, x, mod, g, w_in_t))


GROUPS_PER_BLOCK = 8
ROWS = 8
SCAN_LANES = 512


def _scan_rows(xr, xi, masks, shifts):
    for (mr, mi), sft in zip(masks, shifts):
        sr, si = pltpu.roll(xr, sft, 0), pltpu.roll(xi, sft, 0)
        xr, xi = xr + mr * sr - mi * si, xi + mr * si + mi * sr
    return xr, xi


def _cmul_add(ar, ai, cr, ci, br, bi):
    return ar * cr - ai * ci + br, ar * ci + ai * cr + bi


def _segment_rows(perm_ref, x):
    return _dot(perm_ref[0], x).astype(BF16)


def _time_rows(perm_ref, x):
    hi = x.astype(BF16)
    lo = (x - hi.astype(F32)).astype(BF16)
    return _dot(perm_ref[1], hi) + _dot(perm_ref[1], lo)


def segment_permutation(tc):
    r = jnp.arange(tc)
    p = (r[:, None] % ROWS * (tc // ROWS) + r[:, None] // ROWS == r[None, :]).astype(BF16)
    return jnp.stack([p, p.T])


def _rows_at(k, offset=0):
    return pl.ds(pl.multiple_of(k * ROWS + offset, ROWS), ROWS)


def ssm_forward(u, bd, cd, a1, ml, nb):
    t, w = u.shape
    s = t // nb
    tc = _tile(s, 256)
    seg = tc // ROWS
    nq, ub, lq = bd.shape[1], bd.shape[2], bd.shape[3]
    nl = nq * lq
    nch = s // tc
    lw = min(nl, SCAN_LANES)

    def body(u_ref, perm_ref, bd_ref, cd_ref, a1_ref, ml_ref, y_ref, st_ref, xr_s, xi_s, car_s):
        i = pl.program_id(1)

        @pl.when(i == 0)
        def _():
            car_s[...] = jnp.zeros_like(car_s)

        ub16 = _segment_rows(perm_ref, u_ref[...].astype(BF16))
        for q in range(nq):
            lanes = slice(q * lq, (q + 1) * lq)
            uq = ub16[:, q * ub:(q + 1) * ub]
            xr_s[:, lanes] = _dot(uq, bd_ref[0, q])
            xi_s[:, lanes] = _dot(uq, bd_ref[1, q])
        row_is_0 = lax.broadcasted_iota(jnp.int32, (ROWS, lw), 0) == 0
        zero = jnp.zeros((ROWS, lw), F32)
        for j in range(nl // lw):
            lanes = slice(j * lw, (j + 1) * lw)
            ar, ai = a1_ref[0, :, lanes], a1_ref[1, :, lanes]

            def local(k, c):
                return _cmul_add(ar, ai, c[0], c[1], xr_s[_rows_at(k), lanes], xi_s[_rows_at(k), lanes])

            er, ei = lax.fori_loop(0, seg, local, (zero, zero))
            masks = [(ml_ref[d, 0, :, lanes], ml_ref[d, 1, :, lanes]) for d in range(3)]
            cr, ci = _scan_rows(jnp.where(row_is_0, car_s[0, :, lanes], pltpu.roll(er, 1, 0)),
                                jnp.where(row_is_0, car_s[1, :, lanes], pltpu.roll(ei, 1, 0)), masks, (1, 2, 4))
            st_ref[0, 0, :, lanes] = cr
            st_ref[0, 1, :, lanes] = ci

            def full(k, c):
                xr, xi = _cmul_add(ar, ai, c[0], c[1], xr_s[_rows_at(k), lanes], xi_s[_rows_at(k), lanes])
                xr_s[_rows_at(k), lanes] = xr
                xi_s[_rows_at(k), lanes] = xi
                return xr, xi

            fr, fi = lax.fori_loop(0, seg, full, (cr, ci))
            car_s[0, :, lanes] = jnp.broadcast_to(fr[ROWS - 1:ROWS], fr.shape)
            car_s[1, :, lanes] = jnp.broadcast_to(fi[ROWS - 1:ROWS], fi.shape)
        y = jnp.concatenate([_dot(xr_s[:, q * lq:(q + 1) * lq].astype(BF16), cd_ref[0, q])
                             + _dot(xi_s[:, q * lq:(q + 1) * lq].astype(BF16), cd_ref[1, q]) for q in range(nq)],
                            axis=1)
        y_ref[...] = _time_rows(perm_ref, y)

    perm = segment_permutation(tc)
    return pl.pallas_call(
        body, name="ssm_forward", grid=(nb, nch),
        out_shape=[jax.ShapeDtypeStruct((t, w), F32), jax.ShapeDtypeStruct((nb * nch, 2, ROWS, nl), F32)],
        in_specs=[pl.BlockSpec((tc, w), lambda b, i: (b * nch + i, 0)), _const(perm.shape), _const(bd.shape),
                  _const(cd.shape), _const(a1.shape), _const(ml.shape)],
        out_specs=[pl.BlockSpec((tc, w), lambda b, i: (b * nch + i, 0)),
                   pl.BlockSpec((1, 2, ROWS, nl), lambda b, i: (b * nch + i, 0, 0, 0))],
        scratch_shapes=[pltpu.VMEM((tc, nl), F32), pltpu.VMEM((tc, nl), F32), pltpu.VMEM((2, ROWS, nl), F32)],
        compiler_params=_params(("arbitrary", "arbitrary")),
    )(u, perm, bd, cd, a1, ml)


def ssm_backward(u, dy, st, bd, cd, a1, mlb, dskip, nb, carried=None):
    t, w = u.shape
    s = t // nb
    tc = _tile(s, 256)
    seg = tc // ROWS
    nq, ub, lq = bd.shape[1], bd.shape[2], bd.shape[3]
    nl = nq * lq
    nch = s // tc
    lw = min(nl, SCAN_LANES)

    def body(u_ref, dy_ref, st_ref, perm_ref, bd_ref, cd_ref, a1_ref, mlb_ref, dsk_ref,
             du_ref, dab_ref, dbd_ref, dcd_ref, xr_s, xi_s, lr_s, li_s, car_s):
        bi, i = pl.program_id(0), pl.program_id(1)
        first = jnp.logical_and(bi == 0, i == 0)

        @pl.when(i == 0)
        def _():
            car_s[...] = jnp.zeros_like(car_s)

        @pl.when(first)
        def _():
            dab_ref[...] = jnp.zeros_like(dab_ref)
            dbd_ref[...] = jnp.zeros_like(dbd_ref)
            dcd_ref[...] = jnp.zeros_like(dcd_ref)

        ub16 = _segment_rows(perm_ref, u_ref[...].astype(BF16))
        dyb16 = _segment_rows(perm_ref, dy_ref[...].astype(BF16))
        xr_s[0:ROWS, :] = st_ref[0, 0]
        xi_s[0:ROWS, :] = st_ref[0, 1]
        for q in range(nq):
            lanes = slice(q * lq, (q + 1) * lq)
            uq = ub16[:, q * ub:(q + 1) * ub]
            dq = dyb16[:, q * ub:(q + 1) * ub]
            xr_s[ROWS:, lanes] = _dot(uq, bd_ref[0, q])
            xi_s[ROWS:, lanes] = _dot(uq, bd_ref[1, q])
            lr_s[:, lanes] = _dot(dq, cd_ref[0, q], NT)
            li_s[:, lanes] = _dot(dq, cd_ref[1, q], NT)
        row_is_7 = lax.broadcasted_iota(jnp.int32, (ROWS, lw), 0) == ROWS - 1
        zero = jnp.zeros((ROWS, lw), F32)
        for j in range(nl // lw):
            lanes = slice(j * lw, (j + 1) * lw)
            ar, ai = a1_ref[0, :, lanes], a1_ref[1, :, lanes]
            nai = -ai

            def states(k, c):
                xr, xi = _cmul_add(ar, ai, c[0], c[1], xr_s[_rows_at(k, ROWS), lanes], xi_s[_rows_at(k, ROWS), lanes])
                xr_s[_rows_at(k, ROWS), lanes] = xr
                xi_s[_rows_at(k, ROWS), lanes] = xi
                return xr, xi

            lax.fori_loop(0, seg, states, (st_ref[0, 0, :, lanes], st_ref[0, 1, :, lanes]))

            def local(kk, c):
                k = seg - 1 - kk
                return _cmul_add(ar, nai, c[0], c[1], lr_s[_rows_at(k), lanes], li_s[_rows_at(k), lanes])

            er, ei = lax.fori_loop(0, seg, local, (zero, zero))
            masks = [(mlb_ref[d, 0, :, lanes], mlb_ref[d, 1, :, lanes]) for d in range(3)]
            cr, ci = _scan_rows(jnp.where(row_is_7, car_s[0, :, lanes], pltpu.roll(er, ROWS - 1, 0)),
                                jnp.where(row_is_7, car_s[1, :, lanes], pltpu.roll(ei, ROWS - 1, 0)), masks, (7, 6, 4))

            def full(kk, c):
                cr_, ci_, accr, acci = c
                k = seg - 1 - kk
                lr, li = _cmul_add(ar, nai, cr_, ci_, lr_s[_rows_at(k), lanes], li_s[_rows_at(k), lanes])
                lr_s[_rows_at(k), lanes] = lr
                li_s[_rows_at(k), lanes] = li
                xpr, xpi = xr_s[_rows_at(k), lanes], xi_s[_rows_at(k), lanes]
                return lr, li, accr + lr * xpr + li * xpi, acci + li * xpr - lr * xpi

            lr0, li0, accr, acci = lax.fori_loop(0, seg, full, (cr, ci, zero, zero))
            car_s[0, :, lanes] = jnp.broadcast_to(lr0[0:1], lr0.shape)
            car_s[1, :, lanes] = jnp.broadcast_to(li0[0:1], li0.shape)
            dab_ref[0, :, lanes] += accr
            dab_ref[1, :, lanes] += acci
        du_parts = []
        for q in range(nq):
            lanes = slice(q * lq, (q + 1) * lq)
            cols = slice(q * ub, (q + 1) * ub)
            lrb, lib = lr_s[:, lanes].astype(BF16), li_s[:, lanes].astype(BF16)
            uq, dq = ub16[:, cols], dyb16[:, cols]
            du_parts.append(_dot(lrb, bd_ref[0, q], NT) + _dot(lib, bd_ref[1, q], NT))
            dbd_ref[0, q] += _dot(uq, lrb, TN)
            dbd_ref[1, q] += _dot(uq, lib, TN)
            dcd_ref[0, q] += _dot(xr_s[ROWS:, lanes].astype(BF16), dq, TN)
            dcd_ref[1, q] += _dot(xi_s[ROWS:, lanes].astype(BF16), dq, TN)
        du = _time_rows(perm_ref, jnp.concatenate(du_parts, axis=1)) + dsk_ref[...] * dy_ref[...]
        du_ref[...] = du.astype(BF16)

    rev = lambda b, i: (b * nch + nch - 1 - i, 0)
    perm = segment_permutation(tc)
    return _call(
        body, name="ssm_backward", grid=(nb, nch), carried=carried,
        out_shape=[jax.ShapeDtypeStruct((t, w), BF16), jax.ShapeDtypeStruct((2, ROWS, nl), F32),
                   jax.ShapeDtypeStruct(bd.shape, F32), jax.ShapeDtypeStruct(cd.shape, F32)],
        in_specs=[pl.BlockSpec((tc, w), rev), pl.BlockSpec((tc, w), rev),
                  pl.BlockSpec((1, 2, ROWS, nl), lambda b, i: (b * nch + nch - 1 - i, 0, 0, 0)),
                  _const(perm.shape), _const(bd.shape), _const(cd.shape), _const(a1.shape), _const(mlb.shape),
                  _const((1, w))],
        out_specs=[pl.BlockSpec((tc, w), rev), _full((2, ROWS, nl)), _full(bd.shape), _full(cd.shape)],
        scratch_shapes=[pltpu.VMEM((tc + ROWS, nl), F32), pltpu.VMEM((tc + ROWS, nl), F32),
                        pltpu.VMEM((tc, nl), F32), pltpu.VMEM((tc, nl), F32), pltpu.VMEM((2, ROWS, nl), F32)],
        args=(u, dy, st, perm, bd, cd, a1, mlb, dskip))


def ssm_discretise(a_re, a_im, b_re, b_im, log_dt):
    dt = jnp.exp(log_dt)[:, None]
    er = jnp.exp(a_re * dt)
    abr, abi = er * jnp.cos(a_im * dt), er * jnp.sin(a_im * dt)
    den = a_re * a_re + a_im * a_im
    nr, ni = abr - 1.0, abi
    fr = ((nr * a_re + ni * a_im) / den)[..., None]
    fi = ((ni * a_re - nr * a_im) / den)[..., None]
    return abr, abi, fr * b_re - fi * b_im, fr * b_im + fi * b_re


def _complex_square(zr, zi):
    return zr * zr - zi * zi, 2.0 * zr * zi


def ssm_tables(abr, abi, bbr, bbi, c_re, c_im, seg):
    g, p, h = bbr.shape
    nq = g // GROUPS_PER_BLOCK
    zr, zi = abr.reshape(1, -1), abi.reshape(1, -1)
    a1 = jnp.stack([jnp.broadcast_to(zr, (ROWS, g * p)), jnp.broadcast_to(zi, (ROWS, g * p))])
    for _ in range(seg.bit_length() - 1):
        zr, zi = _complex_square(zr, zi)
    row = jnp.arange(ROWS)[:, None]
    ml, mlb = [], []
    for d in (1, 2, 4):
        ml.append(jnp.stack([jnp.where(row >= d, zr, 0.0), jnp.where(row >= d, zi, 0.0)]))
        mlb.append(jnp.stack([jnp.where(row + d < ROWS, zr, 0.0), jnp.where(row + d < ROWS, -zi, 0.0)]))
        zr, zi = _complex_square(zr, zi)
    eye = jnp.eye(GROUPS_PER_BLOCK, dtype=F32)

    def block_diag_in(bb):
        bq = bb.reshape(nq, GROUPS_PER_BLOCK, p, h)
        return jnp.einsum("qaph,ab->qahbp", bq, eye).reshape(nq, GROUPS_PER_BLOCK * h, GROUPS_PER_BLOCK * p)

    def block_diag_out(cc):
        cq = cc.reshape(nq, GROUPS_PER_BLOCK, h, p)
        return jnp.einsum("qahp,ab->qapbh", cq, eye).reshape(nq, GROUPS_PER_BLOCK * p, GROUPS_PER_BLOCK * h)

    bd = jnp.stack([block_diag_in(bbr), block_diag_in(bbi)]).astype(BF16)
    cd = jnp.stack([block_diag_out(c_re), block_diag_out(-c_im)]).astype(BF16)
    return bd, cd, a1, jnp.stack(ml), jnp.stack(mlb)


def ssm_table_grads(dab, dbd, dcd, g, p, h):
    nq = g // GROUPS_PER_BLOCK
    dabr, dabi = dab[0].sum(0).reshape(g, p), dab[1].sum(0).reshape(g, p)
    b5 = dbd.reshape(2, nq, GROUPS_PER_BLOCK, h, GROUPS_PER_BLOCK, p)
    dbb = jnp.einsum("rqahap->rqaph", b5).reshape(2, g, p, h)
    c5 = dcd.reshape(2, nq, GROUPS_PER_BLOCK, p, GROUPS_PER_BLOCK, h)
    dcc = jnp.einsum("rqapah->rqahp", c5).reshape(2, g, h, p)
    return dabr, dabi, dbb[0], dbb[1], dcc[0], -dcc[1]


HALO = 16


def _conv_inputs(gc_ref, v_ref, gch_ref, vh_ref, cv_s, i, tm):
    cv = gc_ref[...].astype(F32) * v_ref[...].astype(F32)
    halo = gch_ref[...].astype(F32) * vh_ref[...].astype(F32)
    cv_s[0:HALO, :] = jnp.where(i == 0, 0.0, halo)
    cv_s[HALO:, :] = cv
    return cv, cv_s[HALO - 1:HALO - 1 + tm, :], cv_s[HALO - 2:HALO - 2 + tm, :]


def _halo_spec(tm, width, tiles_per_seq):
    per = tm // HALO
    return pl.BlockSpec((HALO, width), lambda b, i: (jnp.maximum((b * tiles_per_seq + i) * per - 1, 0), 0))


def mixer_merge_forward(x, gb, gc, v, sga, sgb, yssm, u, mod, conv_w, dskip, wco, wglu, wso_t, wout, nb):
    t, d = x.shape
    cw, sw = gb.shape[1], u.shape[1]
    s = t // nb
    tm = _tile(s, 256)
    tps = s // tm

    def body(x_ref, gb_ref, gc_ref, v_ref, gch_ref, vh_ref, sga_ref, sgb_ref, ys_ref, u_ref, mod_ref, cw_ref,
             dsk_ref, wco_ref, wglu_ref, wso_ref, wout_ref, xo_ref, ya_ref, yb_ref, mix_ref, cv_s):
        i = pl.program_id(1)
        cv, cv1, cv2 = _conv_inputs(gc_ref, v_ref, gch_ref, vh_ref, cv_s, i, tm)
        w = cw_ref[...]
        conv = w[0:1] * cv2 + w[1:2] * cv1 + w[2:3] * cv
        ya = _dot((gb_ref[...].astype(F32) * conv).astype(BF16), wco_ref[...])
        s0 = ys_ref[...] + dsk_ref[...] * u_ref[...]
        s1, _ = _gelu(s0)
        z = _dot(s1.astype(BF16), wglu_ref[...])
        s2 = s1 * jax.nn.sigmoid(z)
        yb = _dot(s2.astype(BF16), wso_ref[...], NT)
        merged = sga_ref[...].astype(F32) * ya + sgb_ref[...].astype(F32) * yb
        mix = _dot(merged.astype(BF16), wout_ref[...])
        _, _, gt = _mod_rows(mod_ref, 1)
        xo_ref[...] = x_ref[...] + gt * mix
        ya_ref[...] = ya.astype(BF16)
        yb_ref[...] = yb.astype(BF16)
        mix_ref[...] = mix.astype(BF16)

    rd, rc, rw = _row_spec(tm, d, tps), _row_spec(tm, cw, tps), _row_spec(tm, sw, tps)
    hc = _halo_spec(tm, cw, tps)
    return pl.pallas_call(
        body, name="mixer_merge_forward", grid=(nb, tps),
        out_shape=[jax.ShapeDtypeStruct((t, d), F32)] + [jax.ShapeDtypeStruct((t, d), BF16)] * 3,
        in_specs=[rd, rc, rc, rc, hc, hc, rd, rd, rw, rw, _mod_spec(d), _const(conv_w.shape), _const((1, sw)),
                  _const(wco.shape), _const(wglu.shape), _const(wso_t.shape), _const(wout.shape)],
        out_specs=[rd, rd, rd, rd],
        scratch_shapes=[pltpu.VMEM((tm + HALO, cw), F32)],
        compiler_params=_params(("arbitrary", "arbitrary")),
    )(x, gb, gc, v, gc, v, sga, sgb, yssm, u, mod, conv_w, dskip, wco, wglu, wso_t, wout)


def mixer_merge_backward(dxo, mix, ya, yb, gb, gc, v, sga, sgb, yssm, u, mod, conv_w, dskip,
                         wco, wglu, wso_t, wout, nb, carried=None):
    t, d = dxo.shape
    cw, sw = gb.shape[1], u.shape[1]
    s = t // nb
    tm = _tile(s, 256)
    tps = s // tm

    def body(dxo_ref, mix_ref, ya_ref, yb_ref, gb_ref, gc_ref, v_ref, gch_ref, vh_ref, sga_ref, sgb_ref, ys_ref,
             u_ref, mod_ref, cw_ref, dsk_ref, wco_ref, wglu_ref, wso_ref, wout_ref,
             dgla_ref, dglb_ref, dgb_ref, dconv_ref, ds0_ref, dgt_ref, ddsk_ref, dwout_ref, dwco_ref, dwso_ref,
             dwglu_ref, cv_s):
        bi, i = pl.program_id(0), pl.program_id(1)
        _zero_when(i == 0, dgt_ref)
        _zero_when(jnp.logical_and(bi == 0, i == 0), ddsk_ref, dwout_ref, dwco_ref, dwso_ref, dwglu_ref)
        dxo = dxo_ref[...]
        _, _, gt = _mod_rows(mod_ref, 1)
        dmix = (gt * dxo).astype(BF16)
        dgt_ref[...] += jnp.sum(dxo * mix_ref[...].astype(F32), axis=0, keepdims=True)[None]
        ya, yb = ya_ref[...].astype(F32), yb_ref[...].astype(F32)
        sga, sgb = sga_ref[...].astype(F32), sgb_ref[...].astype(F32)
        merged = (sga * ya + sgb * yb).astype(BF16)
        dwout_ref[...] += _dot(merged, dmix, TN)
        dmerged = _dot(dmix, wout_ref[...], NT)
        dgla_ref[...] = (dmerged * ya * sga * (1.0 - sga)).astype(BF16)
        dglb_ref[...] = (dmerged * yb * sgb * (1.0 - sgb)).astype(BF16)
        dya = (dmerged * sga).astype(BF16)
        dyb = (dmerged * sgb).astype(BF16)
        cv, cv1, cv2 = _conv_inputs(gc_ref, v_ref, gch_ref, vh_ref, cv_s, i, tm)
        w = cw_ref[...]
        conv = w[0:1] * cv2 + w[1:2] * cv1 + w[2:3] * cv
        gbv = gb_ref[...].astype(F32)
        dwco_ref[...] += _dot((gbv * conv).astype(BF16), dya, TN)
        dya_in = _dot(dya, wco_ref[...], NT)
        dgb_ref[...] = (dya_in * conv).astype(BF16)
        dconv_ref[...] = dya_in * gbv
        uv = u_ref[...]
        s0 = ys_ref[...] + dsk_ref[...] * uv
        s1, th = _gelu(s0)
        s1b = s1.astype(BF16)
        sz = jax.nn.sigmoid(_dot(s1b, wglu_ref[...]))
        s2b = (s1 * sz).astype(BF16)
        dwso_ref[...] += _dot(dyb, s2b, TN)
        ds2 = _dot(dyb, wso_ref[...])
        dz = (ds2 * s1 * sz * (1.0 - sz)).astype(BF16)
        dwglu_ref[...] += _dot(s1b, dz, TN)
        ds1 = ds2 * sz + _dot(dz, wglu_ref[...], NT)
        ds0 = ds1 * _gelu_grad(s0, th)
        ds0_ref[...] = ds0
        ddsk_ref[...] += jnp.sum(ds0 * uv, axis=0, keepdims=True)

    rd, rc, rw = _row_spec(tm, d, tps), _row_spec(tm, cw, tps), _row_spec(tm, sw, tps)
    hc = _halo_spec(tm, cw, tps)
    return _call(
        body, name="mixer_merge_backward", grid=(nb, tps), carried=carried,
        out_shape=[jax.ShapeDtypeStruct((t, d), BF16), jax.ShapeDtypeStruct((t, d), BF16),
                   jax.ShapeDtypeStruct((t, cw), BF16), jax.ShapeDtypeStruct((t, cw), F32),
                   jax.ShapeDtypeStruct((t, sw), F32), jax.ShapeDtypeStruct((nb, 1, d), F32),
                   jax.ShapeDtypeStruct((1, sw), F32), jax.ShapeDtypeStruct(wout.shape, F32),
                   jax.ShapeDtypeStruct(wco.shape, F32), jax.ShapeDtypeStruct(wso_t.shape, F32),
                   jax.ShapeDtypeStruct(wglu.shape, F32)],
        in_specs=[rd, rd, rd, rd, rc, rc, rc, hc, hc, rd, rd, rw, rw, _mod_spec(d), _const(conv_w.shape),
                  _const((1, sw)), _const(wco.shape), _const(wglu.shape), _const(wso_t.shape), _const(wout.shape)],
        out_specs=[rd, rd, rc, rc, rw, pl.BlockSpec((1, 1, d), lambda b, i: (b, 0, 0)), _full((1, sw)),
                   _full(wout.shape), _full(wco.shape), _full(wso_t.shape), _full(wglu.shape)],
        scratch_shapes=[pltpu.VMEM((tm + HALO, cw), F32)],
        args=(dxo, mix, ya, yb, gb, gc, v, gc, v, sga, sgb, yssm, u, mod, conv_w, dskip, wco, wglu, wso_t, wout))


def conv_backward(dconv, gc, v, conv_w, nb):
    t, cw = dconv.shape
    s = t // nb
    tm = _tile(s, 512)
    tps = s // tm
    per = tm // ROWS

    def body(dc_ref, dcn_ref, gc_ref, v_ref, gch_ref, vh_ref, cw_ref, dgc_ref, dv_ref, dw_ref, cv_s, dc_s):
        bi, i = pl.program_id(0), pl.program_id(1)
        _zero_when(jnp.logical_and(bi == 0, i == 0), dw_ref)
        cv, cv1, cv2 = _conv_inputs(gc_ref, v_ref, gch_ref, vh_ref, cv_s, i, tm)
        dc = dc_ref[...]
        dc_s[0:tm, :] = dc
        dc_s[tm:, :] = jnp.where(i == tps - 1, 0.0, dcn_ref[...])
        w = cw_ref[...]
        dcv = w[2:3] * dc + w[1:2] * dc_s[1:1 + tm, :] + w[0:1] * dc_s[2:2 + tm, :]
        dgc_ref[...] = (dcv * v_ref[...].astype(F32)).astype(BF16)
        dv_ref[...] = (dcv * gc_ref[...].astype(F32)).astype(BF16)
        dw_ref[...] += jnp.concatenate([jnp.sum(dc * cv2, axis=0, keepdims=True),
                                        jnp.sum(dc * cv1, axis=0, keepdims=True),
                                        jnp.sum(dc * cv, axis=0, keepdims=True)], axis=0)

    rc = _row_spec(tm, cw, tps)
    nxt = pl.BlockSpec((ROWS, cw), lambda b, i: (jnp.minimum((b * tps + i + 1) * per, nb * tps * per - 1), 0))
    hc = _halo_spec(tm, cw, tps)
    return pl.pallas_call(
        body, name="conv_backward", grid=(nb, tps),
        out_shape=[jax.ShapeDtypeStruct((t, cw), BF16), jax.ShapeDtypeStruct((t, cw), BF16),
                   jax.ShapeDtypeStruct(conv_w.shape, F32)],
        in_specs=[rc, nxt, rc, rc, hc, hc, _full(conv_w.shape)],
        out_specs=[rc, rc, _full(conv_w.shape)],
        scratch_shapes=[pltpu.VMEM((tm + HALO, cw), F32), pltpu.VMEM((tm + ROWS, cw), F32)],
        compiler_params=_params(("arbitrary", "arbitrary")),
    )(dconv, dconv, gc, v, gc, v, conv_w)


def loss_forward_backward(x, target, g):
    t, d = x.shape
    tm = _tile(t, 512)

    def body(x_ref, t_ref, g_ref, l_ref, dx_ref, dg_ref):
        _zero_when(pl.program_id(0) == 0, dg_ref)
        xv = x_ref[...]
        gv = g_ref[...]
        r = lax.rsqrt(jnp.mean(xv * xv, axis=-1, keepdims=True) + EPS)
        n = xv * r
        err = n * gv - t_ref[...]
        l_ref[...] = jnp.full(l_ref.shape, 0.5 * jnp.sum(jnp.mean(err * err, axis=-1)), F32)
        dy = err * (1.0 / d)
        dn = dy * gv
        dx_ref[...] = r * (dn - n * jnp.mean(n * dn, axis=-1, keepdims=True))
        dg_ref[...] += jnp.sum(dy * n, axis=0, keepdims=True)

    row = pl.BlockSpec((tm, d), lambda i: (i, 0))
    return pl.pallas_call(
        body, name="loss_forward_backward", grid=(t // tm,),
        out_shape=[jax.ShapeDtypeStruct((t // tm, 1, 128), F32), jax.ShapeDtypeStruct((t, d), F32),
                   jax.ShapeDtypeStruct((1, d), F32)],
        in_specs=[row, row, _full((1, d))],
        out_specs=[pl.BlockSpec((1, 1, 128), lambda i: (i, 0, 0)), row, _full((1, d))],
        compiler_params=_params(("arbitrary",)),
    )(x, target, g)


def sum_slots(slots, name):
    _, r, c = slots.shape
    tr = _tile(r, 352) if r % 352 == 0 else _tile(r, 256)

    def body(s_ref, o_ref):
        acc = s_ref[0].astype(F32)
        for j in range(1, N_DEV):
            acc = acc + s_ref[j].astype(F32)
        o_ref[...] = acc

    return pl.pallas_call(
        body, name=name, grid=(r // tr,),
        out_shape=jax.ShapeDtypeStruct((r, c), F32),
        in_specs=[pl.BlockSpec((N_DEV, tr, c), lambda i: (0, i, 0))],
        out_specs=pl.BlockSpec((tr, c), lambda i: (i, 0)),
        compiler_params=_params(("arbitrary",)),
    )(slots)


def adamw_update(w, g, m, v, name):
    r, c = w.shape
    tr = _tile(r, 256) if r % 8 == 0 else r

    def body(w_ref, g_ref, m_ref, v_ref, d_ref, mo_ref, vo_ref):
        d_ref[...], mo_ref[...], vo_ref[...] = _adamw(w_ref[...], g_ref[...], m_ref[...], v_ref[...])

    spec = pl.BlockSpec((tr, c), lambda i: (i, 0))
    return pl.pallas_call(
        body, name=name, grid=(r // tr,),
        out_shape=[jax.ShapeDtypeStruct((r, c), F32)] * 3,
        in_specs=[spec] * 4, out_specs=[spec] * 3,
        compiler_params=_params(("arbitrary",)),
    )(w, g, m, v)


def adamw_update_small(ws, gs, ms, vs):
    n = len(ws)

    def body(*refs):
        w_r, g_r, m_r, v_r = refs[:n], refs[n:2 * n], refs[2 * n:3 * n], refs[3 * n:4 * n]
        d_r, mo_r, vo_r = refs[4 * n:5 * n], refs[5 * n:6 * n], refs[6 * n:7 * n]
        for k in range(n):
            d_r[k][...], mo_r[k][...], vo_r[k][...] = _adamw(w_r[k][...], g_r[k][...], m_r[k][...], v_r[k][...])

    shapes = [jax.ShapeDtypeStruct(w.shape, F32) for w in ws]
    out = pl.pallas_call(body, name="adamw_update_small", out_shape=shapes * 3,
                         compiler_params=_params())(*ws, *gs, *ms, *vs)
    return out[:n], out[n:2 * n], out[2 * n:]


def _slots(grad_t):
    return grad_t.reshape(N_DEV, grad_t.shape[0] // N_DEV, grad_t.shape[1])


def kernel(x, c, w_ada, b_ada, g_ffn1, w1_a, w3_a, w2_a, g_mix, w_in, conv_w, w_conv_out, a_re, a_im, b_re, b_im, c_re, c_im, log_dt, d_skip, w_glu, w_ssm_out, w_out, g_ffn2, w1_b, w3_b, w2_b, g_final, loss_target, m_w_ada, m_b_ada, m_g_ffn1, m_w1_a, m_w3_a, m_w2_a, m_g_mix, m_w_in, m_conv_w, m_w_conv_out, m_a_re, m_a_im, m_b_re, m_b_im, m_c_re, m_c_im, m_log_dt, m_d_skip, m_w_glu, m_w_ssm_out, m_w_out, m_g_ffn2, m_w1_b, m_w3_b, m_w2_b, m_g_final, v_w_ada, v_b_ada, v_g_ffn1, v_w1_a, v_w3_a, v_w2_a, v_g_mix, v_w_in, v_conv_w, v_w_conv_out, v_a_re, v_a_im, v_b_re, v_b_im, v_c_re, v_c_im, v_log_dt, v_d_skip, v_w_glu, v_w_ssm_out, v_w_out, v_g_ffn2, v_w1_b, v_w3_b, v_w2_b, v_g_final):
    nb, s, d = x.shape
    t = nb * s
    me = 4 * lax.axis_index("x") + 2 * lax.axis_index("y") + lax.axis_index("c")
    g_n, p_n, h_n = b_re.shape[1:]
    cw_n = w_conv_out.shape[1] * N_DEV
    sw_n = w_ssm_out.shape[1]
    glu_fold = d // w_glu.shape[2]

    weights = dict(w_ada=w_ada, b_ada=b_ada, g_ffn1=g_ffn1, w1_a=w1_a, w3_a=w3_a, w2_a=w2_a, g_mix=g_mix, w_in=w_in,
                   conv_w=conv_w, w_conv_out=w_conv_out, a_re=a_re, a_im=a_im, b_re=b_re, b_im=b_im, c_re=c_re,
                   c_im=c_im, log_dt=log_dt, d_skip=d_skip, w_glu=w_glu, w_ssm_out=w_ssm_out, w_out=w_out,
                   g_ffn2=g_ffn2, w1_b=w1_b, w3_b=w3_b, w2_b=w2_b, g_final=g_final)
    mom1 = dict(w_ada=m_w_ada, b_ada=m_b_ada, g_ffn1=m_g_ffn1, w1_a=m_w1_a, w3_a=m_w3_a, w2_a=m_w2_a, g_mix=m_g_mix,
                w_in=m_w_in, conv_w=m_conv_w, w_conv_out=m_w_conv_out, a_re=m_a_re, a_im=m_a_im, b_re=m_b_re,
                b_im=m_b_im, c_re=m_c_re, c_im=m_c_im, log_dt=m_log_dt, d_skip=m_d_skip, w_glu=m_w_glu,
                w_ssm_out=m_w_ssm_out, w_out=m_w_out, g_ffn2=m_g_ffn2, w1_b=m_w1_b, w3_b=m_w3_b, w2_b=m_w2_b,
                g_final=m_g_final)
    mom2 = dict(w_ada=v_w_ada, b_ada=v_b_ada, g_ffn1=v_g_ffn1, w1_a=v_w1_a, w3_a=v_w3_a, w2_a=v_w2_a, g_mix=v_g_mix,
                w_in=v_w_in, conv_w=v_conv_w, w_conv_out=v_w_conv_out, a_re=v_a_re, a_im=v_a_im, b_re=v_b_re,
                b_im=v_b_im, c_re=v_c_re, c_im=v_c_im, log_dt=v_log_dt, d_skip=v_d_skip, w_glu=v_w_glu,
                w_ssm_out=v_w_ssm_out, w_out=v_w_out, g_ffn2=v_g_ffn2, w1_b=v_w1_b, w3_b=v_w3_b, w2_b=v_w2_b,
                g_final=v_g_final)
    names = list(weights)
    transposed = ("w1_a", "w3_a", "w_in", "w_ssm_out", "w1_b", "w3_b")
    groups = dict(ffn_a=("w1_a", "w3_a", "w2_a"), mixer=("w_in", "w_conv_out", "w_glu", "w_ssm_out", "w_out"),
                  ffn_b=("w1_b", "w3_b", "w2_b"))
    big = groups["ffn_a"] + groups["mixer"] + groups["ffn_b"]

    pad_rows = lambda a: jnp.pad(a, ((0, -a.shape[0] % ROWS), (0, 0)))
    c_all, conv_all = run_stage(GatherStage([pad_rows(c), pad_rows(conv_w[0])]), "gather_cond")
    c_all = c_all.reshape(N_DEV, -1, d)[:, :nb].reshape(N_DEV * nb, d)
    conv_full = conv_all.reshape(N_DEV, ROWS, -1)[:, :CONV_K].transpose(1, 0, 2).reshape(CONV_K, cw_n)
    ada_cols = w_ada.shape[2]
    b_cols = lax.dynamic_slice(b_ada, (0, me * ada_cols), (1, ada_cols))
    mod_cols = ada_forward(c_all, w_ada[0], b_cols)
    (mod_all,) = run_stage(GatherStage([mod_cols]), "gather_mod")
    mod_mine = lax.dynamic_slice(mod_all.reshape(N_DEV, N_DEV * nb, ada_cols), (0, me * nb, 0), (N_DEV, nb, ada_cols))
    mod = mod_mine.transpose(1, 0, 2).reshape(nb, N_MOD, d)

    def shard_rows(name):
        w = weights[name][0]
        if name in transposed:
            w = w.T
        if name == "w_glu":
            w = w.reshape(w.shape[0] // glu_fold, d)
        return w.astype(BF16)

    def gather_stage(group):
        return GatherStage([shard_rows(n) for n in groups[group]])

    gw = {}

    def keep_weights(group, outs):
        for n, w in zip(groups[group], outs):
            gw[n] = w.reshape(sw_n, sw_n) if n == "w_glu" else w

    disc_in = (a_re[0], a_im[0], b_re[0], b_im[0], log_dt[0])
    (abr, abi, bbr, bbi), disc_vjp = jax.vjp(ssm_discretise, *disc_in)
    bd, cd, abar8, ml, mlb = ssm_tables(abr, abi, bbr, bbi, c_re[0], c_im[0], _tile(s, 256) // ROWS)

    x0 = x.reshape(t, d)
    keep_weights("ffn_a", run_stage(gather_stage("ffn_a"), "gather_ffn_a"))
    (x1, a1, b1, f1), got = ffn_forward(x0, mod, g_ffn1, gw["w1_a"], gw["w3_a"], gw["w2_a"], 0, nb,
                                        "ffn_a_forward", carried=gather_stage("mixer"))
    keep_weights("mixer", got)
    (gb, gc, vv, u, sga, sgb, h2), got = mixer_proj_forward(x1, mod, g_mix, gw["w_in"], cw_n, sw_n, nb,
                                                            carried=gather_stage("ffn_b"))
    keep_weights("ffn_b", got)
    yssm, st = ssm_forward(u, bd, cd, abar8, ml, nb)
    x2, ya, yb, mix = mixer_merge_forward(x1, gb, gc, vv, sga, sgb, yssm, u, mod, conv_full, d_skip,
                                          gw["w_conv_out"], gw["w_glu"], gw["w_ssm_out"], gw["w_out"], nb)
    (x3, a3, b3, f3), _ = ffn_forward(x2, mod, g_ffn2, gw["w1_b"], gw["w3_b"], gw["w2_b"], 2, nb, "ffn_b_forward")
    loss_parts, dx3, dg_final = loss_forward_backward(x3, loss_target.reshape(t, d), g_final.reshape(1, d))
    loss = lax.psum(jnp.sum(loss_parts[:, 0, 0]), ("x", "y", "c"))

    part, received = {}, {}

    def exchange_stage(ns):
        return ExchangeStage([_slots(part[n]) for n in ns])

    (da3, db3, sw3, df3, dgt3), _ = ffn_backward_hidden(dx3, a3, b3, f3, mod, gw["w2_b"], 2, nb,
                                                        "ffn_b_backward_hidden")
    (dx2, h3, dmod3, dg_ffn2), _ = ffn_backward_input(dx3, x2, da3, db3, mod, g_ffn2, gw["w1_b"], gw["w3_b"],
                                                      2, nb, "ffn_b_backward_input")
    (part["w1_b"],), _ = nn_matmul(da3, h3, "grad_w1_b")
    (part["w3_b"],), _ = nn_matmul(db3, h3, "grad_w3_b")
    (part["w2_b"],), _ = nn_matmul(sw3, df3, "grad_w2_b")
    (dgla, dglb, dgb, dconv, ds0, dgt2, dd_skip, dw_out, dw_co, dw_so_t, dw_glu), got = mixer_merge_backward(
        dx2, mix, ya, yb, gb, gc, vv, sga, sgb, yssm, u, mod, conv_full, d_skip,
        gw["w_conv_out"], gw["w_glu"], gw["w_ssm_out"], gw["w_out"], nb, carried=exchange_stage(groups["ffn_b"]))
    received.update(zip(groups["ffn_b"], got))
    part["w_out"] = dw_out.astype(BF16)
    part["w_conv_out"] = dw_co.astype(BF16)
    part["w_ssm_out"] = dw_so_t.astype(BF16)
    part["w_glu"] = dw_glu.reshape(sw_n // glu_fold, d).astype(BF16)
    (du, dab, dbd, dcd), got = ssm_backward(u, ds0, st, bd, cd, abar8, mlb, d_skip, nb,
                                            carried=exchange_stage(groups["mixer"][1:]))
    received.update(zip(groups["mixer"][1:], got))
    dgc, dvv, dconv_w = conv_backward(dconv, gc, vv, conv_full, nb)
    part["w_in"] = jnp.concatenate([tn_matmul(p, h2, "grad_w_in_%d" % k)
                                    for k, p in enumerate((dgb, dgc, dvv, du, dgla, dglb))], axis=0)
    (dx1, dmod2, dg_mix), got = mixer_proj_backward(dgb, dgc, dvv, du, dgla, dglb, dx2, x1, mod, g_mix, gw["w_in"], nb,
                                                    carried=exchange_stage(("w_in",)))
    received["w_in"] = got[0]
    (da1, db1, sw1, df1, dgt1), _ = ffn_backward_hidden(dx1, a1, b1, f1, mod, gw["w2_a"], 0, nb,
                                                        "ffn_a_backward_hidden")
    (part["w2_a"],), _ = nn_matmul(sw1, df1, "grad_w2_a")
    (dx0, h1, dmod1, dg_ffn1), _ = ffn_backward_input(dx1, x0, da1, db1, mod, g_ffn1, gw["w1_a"], gw["w3_a"],
                                                      0, nb, "ffn_a_backward_input")
    (part["w1_a"],), got = nn_matmul(da1, h1, "grad_w1_a", carried=exchange_stage(("w2_a",)))
    received["w2_a"] = got[0]
    (part["w3_a"],), got = nn_matmul(db1, h1, "grad_w3_a", carried=exchange_stage(("w1_a",)))
    received["w1_a"] = got[0]
    (received["w3_a"],) = run_stage(exchange_stage(("w3_a",)), "exchange_w3_a")

    dabr, dabi, dbbr, dbbi, dcr, dci = ssm_table_grads(dab, dbd, dcd, g_n, p_n, h_n)
    gmod = jnp.concatenate([dmod1, dgt1, dmod2, dgt2, dmod3, dgt3], axis=1)
    small = dict(gmod=gmod, g_ffn1=dg_ffn1, g_mix=dg_mix, g_ffn2=dg_ffn2, g_final=dg_final, d_skip=dd_skip,
                 abr=dabr, abi=dabi, bbr=dbbr, bbi=dbbi, c_re=dcr, c_im=dci, conv_w=dconv_w)
    flat = jnp.concatenate([a.reshape(-1) for a in small.values()])
    n_small = flat.shape[0]
    n_rows = -(-n_small // (128 * ROWS)) * ROWS
    flat = jnp.pad(flat, (0, n_rows * 128 - n_small)).reshape(n_rows, 128)
    (small_all,) = run_stage(GatherStage([flat]), "gather_small_grads")
    small_all = small_all.reshape(N_DEV, n_rows, 128)
    total = sum_slots(small_all, "sum_small_grads").reshape(-1)
    tot, off = {}, 0
    for key, like in small.items():
        n = math.prod(like.shape)
        tot[key], off = total[off:off + n].reshape(like.shape), off + n
    gmod_all = small_all.reshape(N_DEV, n_rows * 128)[:, :nb * N_MOD * d].reshape(N_DEV * nb, N_MOD * d)
    g_a_re, g_a_im, g_b_re, g_b_im, g_log_dt = disc_vjp((tot["abr"], tot["abi"], tot["bbr"], tot["bbi"]))

    grads = {}
    grads["b_ada"] = sum_rows(tot["gmod"].reshape(nb, N_MOD * d))
    grads["g_ffn1"], grads["g_mix"], grads["g_ffn2"] = tot["g_ffn1"], tot["g_mix"], tot["g_ffn2"]
    grads["g_final"] = tot["g_final"].reshape(d)
    grads["d_skip"] = tot["d_skip"]
    grads["a_re"], grads["a_im"], grads["log_dt"] = g_a_re[None], g_a_im[None], g_log_dt[None]
    grads["b_re"], grads["b_im"] = g_b_re[None], g_b_im[None]
    grads["c_re"], grads["c_im"] = tot["c_re"][None], tot["c_im"][None]
    grads["conv_w"] = lax.dynamic_slice(tot["conv_w"], (0, me * conv_w.shape[2]), (CONV_K, conv_w.shape[2]))[None]

    delta, new_m, new_v = {}, {}, {}
    for name in big:
        gsum = sum_slots(received[name], "sum_" + name)
        if name == "w_glu":
            gsum = gsum.reshape(w_glu.shape[1], w_glu.shape[2])
        if name in transposed:
            gsum = gsum.T
        grads[name] = gsum[None]
        dl, mm, vn = adamw_update(weights[name][0], gsum, mom1[name][0], mom2[name][0], "adamw_" + name)
        delta[name], new_m[name], new_v[name] = dl[None], mm[None], vn[None]

    gmod_cols = lax.dynamic_slice(gmod_all, (0, me * ada_cols), (N_DEV * nb, ada_cols))
    g_wada, d_wada, m_wada, v_wada = ada_backward_update(c_all, gmod_cols, w_ada[0], m_w_ada[0], v_w_ada[0])
    grads["w_ada"], delta["w_ada"], new_m["w_ada"], new_v["w_ada"] = g_wada[None], d_wada[None], m_wada[None], v_wada[None]

    small_names = [n for n in names if n not in big and n != "w_ada"]

    def as2d(a):
        return a.reshape(-1, a.shape[-1])

    sw_, sg_, sm_, sv_ = ([as2d(src[n]) for n in small_names] for src in (weights, grads, mom1, mom2))
    sd, smo, svo = adamw_update_small(sw_, sg_, sm_, sv_)
    for n, dl, mm, vn in zip(small_names, sd, smo, svo):
        shape = weights[n].shape
        grads[n] = grads[n].reshape(shape)
        delta[n], new_m[n], new_v[n] = dl.reshape(shape), mm.reshape(shape), vn.reshape(shape)

    grad_x = dx0.reshape(nb, s, d)
    return (loss, grad_x, *[grads[n] for n in names], *[delta[n] for n in names],
            *[new_m[n] for n in names], *[new_v[n] for n in names])


def sum_rows(a):
    r, c = a.shape

    def body(a_ref, o_ref):
        acc = a_ref[0:1, :]
        for j in range(1, r):
            acc = acc + a_ref[j:j + 1, :]
        o_ref[...] = acc

    return pl.pallas_call(body, name="sum_rows", out_shape=jax.ShapeDtypeStruct((1, c), F32),
                          compiler_params=_params())(a)
```

```python
import functools
import math

import jax
import jax.numpy as jnp
from jax import lax
from jax.experimental import pallas as pl
from jax.experimental.pallas import tpu as pltpu

F32 = jnp.float32
BF16 = jnp.bfloat16
N_DEV = 8
N_MOD = 9
EPS = 1e-6
CONV_K = 3
ADAM_LR = 0.001
ADAM_B1 = 0.9
ADAM_B2 = 0.999
ADAM_EPS = 1e-08
ADAM_WD = 0.01
ADAM_STEP = 10
GELU_C0 = math.sqrt(2.0 / math.pi)
GELU_C1 = 0.044715
V7X_VMEM_LIMIT = 56 * 1024 * 1024
MESH_ID = pl.DeviceIdType.MESH
NT = (((1,), (1,)), ((), ()))
TN = (((0,), (0,)), ((), ()))


def _dot(a, b, dims=None):
    if dims is None:
        return jnp.dot(a, b, preferred_element_type=F32)
    return lax.dot_general(a, b, dims, preferred_element_type=F32)


def _params(sem=None, vmem=V7X_VMEM_LIMIT):
    return pltpu.CompilerParams(dimension_semantics=sem, vmem_limit_bytes=vmem)


def _full(shape):
    return pl.BlockSpec(shape, lambda *_: (0,) * len(shape))


def _const(shape):
    return pl.BlockSpec(shape, lambda *_: (0,) * len(shape), pipeline_mode=pl.Buffered(1))


def _tile(n, want):
    t = min(n, want)
    while n % t:
        t //= 2
    return t


class GatherStage:
    COPIES = 9

    def __init__(self, shards):
        n = len(shards)
        self.inputs = list(shards)
        self.out_shape = [jax.ShapeDtypeStruct((N_DEV * s.shape[0], s.shape[1]), s.dtype) for s in shards]
        self.scratch = [pltpu.SemaphoreType.DMA((self.COPIES * n,)), pltpu.SemaphoreType.DMA((self.COPIES * n,)),
                        pltpu.SemaphoreType.DMA((n,))]

    def _plan(self, ins, outs, sems):
        send_sems, recv_sems, local_sems = sems
        n = len(ins)
        x, y, c = lax.axis_index("x"), lax.axis_index("y"), lax.axis_index("c")
        me, sibling, xn, yn, dg = (x, y, c), (x, y, 1 - c), (1 - x, y, c), (x, 1 - y, c), (1 - x, 1 - y, c)

        def rows(k, block, half=None):
            r = ins[k].shape[0]
            px, py, pc = block
            base = (4 * px + 2 * py + pc) * r
            if half is None:
                return outs[k].at[pl.ds(base, r), :]
            return outs[k].at[pl.ds(base + half * (r // 2), r // 2), :]

        def copy(k, j, block, to, half=None, src=None):
            return pltpu.make_async_remote_copy(
                src_ref=rows(k, block, half) if src is None else src, dst_ref=rows(k, block, half),
                send_sem=send_sems.at[self.COPIES * k + j], recv_sem=recv_sems.at[self.COPIES * k + j],
                device_id=to, device_id_type=MESH_ID)

        sib = lambda b: (b[0], b[1], 1 - b[2])
        mine = [pltpu.make_async_copy(ins[k], rows(k, me), local_sems.at[k]) for k in range(n)]
        first = [(0, me, sibling, None, sibling), (1, me, xn, None, xn), (2, me, yn, None, yn)]
        second = [(3, xn, yn, 0, dg), (4, yn, xn, 1, dg), (5, xn, sibling, None, sib(xn)), (6, yn, sibling, None, sib(yn))]
        third = [(7, dg, sibling, 0, sib(dg)), (8, dg, sibling, 1, sib(dg))]
        return n, me, copy, mine, first, second, third

    def start(self, ins, outs, sems):
        n, me, copy, mine, first, _, _ = self._plan(ins, outs, sems)
        for cp in mine:
            cp.start()
        for k in range(n):
            for j, block, to, half, _ in first:
                copy(k, j, block, to, half, src=ins[k]).start()

    def finish(self, ins, outs, sems):
        n, me, copy, mine, first, second, third = self._plan(ins, outs, sems)
        arrived = lambda k, j, block, half: copy(k, j, block, me, half).wait_recv()
        for k in range(n):
            arrived(k, 1, first[1][4], None)
            arrived(k, 2, first[2][4], None)
            for j, block, to, half, _ in second:
                copy(k, j, block, to, half).start()
        for k in range(n):
            arrived(k, 3, second[0][4], 0)
            arrived(k, 4, second[1][4], 1)
            for j, block, to, half, _ in third:
                copy(k, j, block, to, half).start()
        for k in range(n):
            arrived(k, 0, first[0][4], None)
            arrived(k, 5, second[2][4], None)
            arrived(k, 6, second[3][4], None)
            arrived(k, 7, third[0][4], 0)
            arrived(k, 8, third[1][4], 1)
        for k in range(n):
            for j, block, to, half, _ in first:
                copy(k, j, block, to, half, src=ins[k]).wait_send()
            for j, block, to, half, _ in second + third:
                copy(k, j, block, to, half).wait_send()
        for cp in mine:
            cp.wait()


class ExchangeStage:
    def __init__(self, bufs):
        n = len(bufs)
        self.inputs = list(bufs)
        self.out_shape = [jax.ShapeDtypeStruct(b.shape, b.dtype) for b in bufs]
        self.scratch = [pltpu.SemaphoreType.DMA((7 * n,)), pltpu.SemaphoreType.DMA((7 * n,)),
                        pltpu.SemaphoreType.DMA((n,))]

    def _plan(self, ins, outs, sems):
        send_sems, recv_sems, local_sems = sems
        n = len(ins)
        x, y, c = lax.axis_index("x"), lax.axis_index("y"), lax.axis_index("c")
        me = 4 * x + 2 * y + c
        mine = [pltpu.make_async_copy(ins[k].at[me], outs[k].at[me], local_sems.at[k]) for k in range(n)]
        copies = []
        for mask in range(1, N_DEV):
            px, py, pc = x ^ (mask >> 2), y ^ ((mask >> 1) & 1), c ^ (mask & 1)
            for k in range(n):
                copies.append(pltpu.make_async_remote_copy(
                    src_ref=ins[k].at[4 * px + 2 * py + pc], dst_ref=outs[k].at[me],
                    send_sem=send_sems.at[7 * k + mask - 1], recv_sem=recv_sems.at[7 * k + mask - 1],
                    device_id=(px, py, pc), device_id_type=MESH_ID))
        return mine, copies

    def start(self, ins, outs, sems):
        mine, copies = self._plan(ins, outs, sems)
        for cp in mine + copies:
            cp.start()

    def finish(self, ins, outs, sems):
        mine, copies = self._plan(ins, outs, sems)
        for cp in copies:
            cp.wait_recv()
        for cp in copies:
            cp.wait_send()
        for cp in mine:
            cp.wait()


ANY_SPEC = pl.BlockSpec(memory_space=pl.ANY)
GATHER_ROWS = 16


class StageGroup:
    def __init__(self, stages):
        self.stages = list(stages)
        self.inputs = [a for s in stages for a in s.inputs]
        self.out_shape = [o for s in stages for o in s.out_shape]
        self.scratch = [t for s in stages for t in s.scratch]

    def _parts(self, ins, outs, sems):
        i = o = t = 0
        for s in self.stages:
            ni, no, nt = len(s.inputs), len(s.out_shape), len(s.scratch)
            yield s, ins[i:i + ni], outs[o:o + no], sems[t:t + nt]
            i, o, t = i + ni, o + no, t + nt

    def start(self, ins, outs, sems):
        for s, i_, o_, t_ in self._parts(ins, outs, sems):
            s.start(i_, o_, t_)

    def finish(self, ins, outs, sems):
        for s, i_, o_, t_ in self._parts(ins, outs, sems):
            s.finish(i_, o_, t_)


def run_stage(stage, name):
    ci, co = len(stage.inputs), len(stage.out_shape)

    def body(*refs):
        ins, outs, sems = refs[:ci], refs[ci:ci + co], refs[ci + co:]
        stage.start(ins, outs, sems)
        stage.finish(ins, outs, sems)

    return pl.pallas_call(body, name=name, out_shape=stage.out_shape, in_specs=[ANY_SPEC] * ci,
                          out_specs=[ANY_SPEC] * co, scratch_shapes=stage.scratch)(*stage.inputs)


def _call(body, *, name, grid, in_specs, out_specs, out_shape, args, scratch_shapes=(), carried=None):
    sem = ("arbitrary",) * len(grid)
    if carried is None:
        return pl.pallas_call(body, name=name, grid=grid, in_specs=list(in_specs), out_specs=list(out_specs),
                              out_shape=list(out_shape), scratch_shapes=list(scratch_shapes),
                              compiler_params=_params(sem))(*args), None
    ni, no, ns = len(in_specs), len(out_shape), len(scratch_shapes)
    ci, co = len(carried.inputs), len(carried.out_shape)

    def wrapped(*refs):
        ins, refs = refs[:ni], refs[ni:]
        cins, refs = refs[:ci], refs[ci:]
        outs, refs = refs[:no], refs[no:]
        couts, refs = refs[:co], refs[co:]
        scr, csems = refs[:ns], refs[ns:]
        ids = [pl.program_id(k) for k in range(len(grid))]
        first = functools.reduce(jnp.logical_and, [i == 0 for i in ids])
        last = functools.reduce(jnp.logical_and, [i == g - 1 for i, g in zip(ids, grid)])

        @pl.when(first)
        def _():
            carried.start(cins, couts, csems)

        body(*ins, *outs, *scr)

        @pl.when(last)
        def _():
            carried.finish(cins, couts, csems)

    res = pl.pallas_call(
        wrapped, name=name, grid=grid, in_specs=list(in_specs) + [ANY_SPEC] * ci,
        out_specs=list(out_specs) + [ANY_SPEC] * co, out_shape=list(out_shape) + carried.out_shape,
        scratch_shapes=list(scratch_shapes) + carried.scratch, compiler_params=_params(sem),
    )(*args, *carried.inputs)
    return res[:no], res[no:]


def _norm_mod(x, g, shift, scale):
    r = lax.rsqrt(jnp.mean(x * x, axis=-1, keepdims=True) + EPS)
    n = x * r
    return (n * g) * (1.0 + scale) + shift, n, r


def _norm_mod_bwd(dh, n, r, g, scale):
    dsh = jnp.sum(dh, axis=0, keepdims=True)
    dsc = jnp.sum(dh * (n * g), axis=0, keepdims=True)
    dg = jnp.sum(dh * (1.0 + scale) * n, axis=0, keepdims=True)
    dn = dh * ((1.0 + scale) * g)
    dx = r * (dn - n * jnp.mean(n * dn, axis=-1, keepdims=True))
    return dx, dsh, dsc, dg


def _mod_rows(mod_ref, sub):
    m = mod_ref[0]
    return m[3 * sub:3 * sub + 1], m[3 * sub + 1:3 * sub + 2], m[3 * sub + 2:3 * sub + 3]


def _gelu(x):
    t = jnp.tanh(GELU_C0 * (x + GELU_C1 * x * x * x))
    return 0.5 * x * (1.0 + t), t


def _gelu_grad(x, t):
    return 0.5 * (1.0 + t) + 0.5 * x * (1.0 - t * t) * (GELU_C0 * (1.0 + 3.0 * GELU_C1 * x * x))


def _zero_when(cond, *refs):
    @pl.when(cond)
    def _():
        for r in refs:
            r[...] = jnp.zeros_like(r)


def ada_forward(c_all, w_ada, b_ada_cols):
    def body(c_ref, w_ref, b_ref, o_ref):
        c = c_ref[...]
        cond = (c * jax.nn.sigmoid(c)).astype(BF16)
        o_ref[...] = _dot(cond, w_ref[...].astype(BF16)) + b_ref[...]

    nb, d = c_all.shape
    cols = w_ada.shape[1]
    tn = _tile(cols, 384)
    return pl.pallas_call(
        body, name="ada_forward", grid=(cols // tn,),
        out_shape=jax.ShapeDtypeStruct((nb, cols), F32),
        in_specs=[_full((nb, d)), pl.BlockSpec((d, tn), lambda j: (0, j)), pl.BlockSpec((1, tn), lambda j: (0, j))],
        out_specs=pl.BlockSpec((nb, tn), lambda j: (0, j)),
        compiler_params=_params(("arbitrary",)),
    )(c_all, w_ada, b_ada_cols)


def _adamw(w, g, m, v):
    m = ADAM_B1 * m + (1.0 - ADAM_B1) * g
    v = ADAM_B2 * v + (1.0 - ADAM_B2) * (g * g)
    m_hat = m / (1.0 - ADAM_B1 ** ADAM_STEP)
    v_hat = v / (1.0 - ADAM_B2 ** ADAM_STEP)
    delta = -ADAM_LR * (m_hat / (jnp.sqrt(v_hat) + ADAM_EPS) + ADAM_WD * w)
    return delta, m, v


def ada_backward_update(c_all, gmod_cols, w, m, v):
    def body(c_ref, g_ref, w_ref, m_ref, v_ref, go_ref, d_ref, mo_ref, vo_ref):
        c = c_ref[...]
        cond = (c * jax.nn.sigmoid(c)).astype(BF16)
        g = _dot(cond, g_ref[...].astype(BF16), TN)
        go_ref[...] = g
        d_ref[...], mo_ref[...], vo_ref[...] = _adamw(w_ref[...], g, m_ref[...], v_ref[...])

    nb, d = c_all.shape
    cols = w.shape[1]
    tn = _tile(cols, 128)
    col = pl.BlockSpec((d, tn), lambda j: (0, j))
    return pl.pallas_call(
        body, name="ada_backward_update", grid=(cols // tn,),
        out_shape=[jax.ShapeDtypeStruct(w.shape, F32)] * 4,
        in_specs=[_full((nb, d)), pl.BlockSpec((nb, tn), lambda j: (0, j)), col, col, col],
        out_specs=[col] * 4,
        compiler_params=_params(("arbitrary",)),
    )(c_all, gmod_cols, w, m, v)


def _row_spec(tm, width, tiles_per_seq):
    return pl.BlockSpec((tm, width), lambda b, i: (b * tiles_per_seq + i, 0))


def _mod_spec(d):
    return pl.BlockSpec((1, N_MOD, d), lambda b, i: (b, 0, 0))


def _col_spec(rows, tm, tiles_per_seq):
    return pl.BlockSpec((rows, tm), lambda b, i: (0, b * tiles_per_seq + i))


def _ffn_chunk(f):
    return f // 2 if f % 256 == 0 and f > 1536 else f


def ffn_forward(x, mod, g, w1t, w3t, w2, sub, nb, name, carried=None):
    t, d = x.shape
    f = w1t.shape[0]
    s = t // nb
    tm = _tile(s, 512)
    fc = _ffn_chunk(f)

    def body(x_ref, mod_ref, g_ref, w1_ref, w3_ref, w2_ref, xo_ref, a_ref, b_ref, f_ref):
        xv = x_ref[...]
        sh, sc, gt = _mod_rows(mod_ref, sub)
        h, _, _ = _norm_mod(xv, g_ref[...], sh, sc)
        hb = h.astype(BF16)
        acc_t = jnp.zeros((d, tm), F32)
        for k in range(f // fc):
            rows = slice(k * fc, (k + 1) * fc)
            a = _dot(w1_ref[rows, :], hb, NT)
            b = _dot(w3_ref[rows, :], hb, NT)
            a_ref[rows, :] = a.astype(BF16)
            b_ref[rows, :] = b.astype(BF16)
            sw = (a * jax.nn.sigmoid(a)) * b
            acc_t = acc_t + _dot(w2_ref[rows, :], sw.astype(BF16), TN)
        acc = acc_t.T
        f_ref[...] = acc.astype(BF16)
        xo_ref[...] = xv + (0.5 * gt) * acc

    tps = s // tm
    rd, cf = _row_spec(tm, d, tps), _col_spec(f, tm, tps)
    return _call(
        body, name=name, grid=(nb, tps), carried=carried,
        out_shape=[jax.ShapeDtypeStruct((t, d), F32), jax.ShapeDtypeStruct((f, t), BF16),
                   jax.ShapeDtypeStruct((f, t), BF16), jax.ShapeDtypeStruct((t, d), BF16)],
        in_specs=[rd, _mod_spec(d), _const((1, d)), _const((f, d)), _const((f, d)), _const((f, d))],
        out_specs=[rd, cf, cf, rd],
        args=(x, mod, g, w1t, w3t, w2))


def ffn_backward_hidden(dxo, a_t, b_t, fo, mod, w2, sub, nb, name, carried=None):
    t, d = dxo.shape
    f = w2.shape[0]
    s = t // nb
    tm = _tile(s, 512)
    fc = _tile(f, 704) if f % 704 == 0 else _tile(f, 512)

    def body(dxo_ref, a_ref, b_ref, f_ref, mod_ref, w2_ref, da_ref, db_ref, s_ref, df_ref, dgt_ref):
        _zero_when(pl.program_id(1) == 0, dgt_ref)
        dxo = dxo_ref[...]
        _, _, gt = _mod_rows(mod_ref, sub)
        dfb = ((0.5 * gt) * dxo).astype(BF16)
        df_ref[...] = dfb
        dgt_ref[...] += 0.5 * jnp.sum(dxo * f_ref[...].astype(F32), axis=0, keepdims=True)[None]
        for k in range(f // fc):
            rows = slice(k * fc, (k + 1) * fc)
            ds = _dot(w2_ref[rows, :], dfb, NT).astype(BF16)
            av = a_ref[rows, :].astype(F32)
            bv = b_ref[rows, :]
            sig = jax.nn.sigmoid(av)
            sl = av * sig
            slb = sl.astype(BF16)
            da_ref[rows, :] = ds * bv * (sig + sl * (1.0 - sig)).astype(BF16)
            db_ref[rows, :] = ds * slb
            s_ref[rows, :] = slb * bv

    tps = s // tm
    rd, cf = _row_spec(tm, d, tps), _col_spec(f, tm, tps)
    return _call(
        body, name=name, grid=(nb, tps), carried=carried,
        out_shape=[jax.ShapeDtypeStruct((f, t), BF16)] * 3
        + [jax.ShapeDtypeStruct((t, d), BF16), jax.ShapeDtypeStruct((nb, 1, d), F32)],
        in_specs=[rd, cf, cf, rd, _mod_spec(d), _const((f, d))],
        out_specs=[cf, cf, cf, rd, pl.BlockSpec((1, 1, d), lambda b, i: (b, 0, 0))],
        args=(dxo, a_t, b_t, fo, mod, w2))


def ffn_backward_input(dxo, x, da_t, db_t, mod, g, w1t, w3t, sub, nb, name, carried=None):
    t, d = x.shape
    f = w1t.shape[0]
    s = t // nb
    tm = _tile(s, 512)

    def body(dxo_ref, x_ref, da_ref, db_ref, mod_ref, g_ref, w1_ref, w3_ref, dx_ref, h_ref, dmod_ref, dg_ref):
        bi, i = pl.program_id(0), pl.program_id(1)
        _zero_when(i == 0, dmod_ref)
        _zero_when(jnp.logical_and(bi == 0, i == 0), dg_ref)
        sh, sc, _ = _mod_rows(mod_ref, sub)
        gv = g_ref[...]
        h, n, r = _norm_mod(x_ref[...], gv, sh, sc)
        h_ref[...] = h.astype(BF16)
        dh_t = _dot(w1_ref[...], da_ref[...], TN) + _dot(w3_ref[...], db_ref[...], TN)
        dxn, dsh, dsc, dg = _norm_mod_bwd(dh_t.T, n, r, gv, sc)
        dx_ref[...] = dxo_ref[...] + dxn
        dmod_ref[...] += jnp.concatenate([dsh, dsc], axis=0)[None]
        dg_ref[...] += dg

    tps = s // tm
    rd, cf = _row_spec(tm, d, tps), _col_spec(f, tm, tps)
    return _call(
        body, name=name, grid=(nb, tps), carried=carried,
        out_shape=[jax.ShapeDtypeStruct((t, d), F32), jax.ShapeDtypeStruct((t, d), BF16),
                   jax.ShapeDtypeStruct((nb, 2, d), F32), jax.ShapeDtypeStruct((1, d), F32)],
        in_specs=[rd, rd, cf, cf, _mod_spec(d), _const((1, d)), _const((f, d)), _const((f, d))],
        out_specs=[rd, rd, pl.BlockSpec((1, 2, d), lambda b, i: (b, 0, 0)), _full((1, d))],
        args=(dxo, x, da_t, db_t, mod, g, w1t, w3t))


def nn_matmul(lhs_t, rhs, name, carried=None):
    m, t = lhs_t.shape
    n = rhs.shape[1]
    tk = _tile(t, 2048)
    tmm = m if m <= 1536 else m // 2
    nk = t // tk

    def body(a_ref, b_ref, o_ref, acc_ref):
        k = pl.program_id(1)
        _zero_when(k == 0, acc_ref)
        acc_ref[...] += _dot(a_ref[...], b_ref[...])

        @pl.when(k == nk - 1)
        def _():
            o_ref[...] = acc_ref[...].astype(BF16)

    return _call(
        body, name=name, grid=(m // tmm, nk), carried=carried,
        out_shape=[jax.ShapeDtypeStruct((m, n), BF16)],
        in_specs=[pl.BlockSpec((tmm, tk), lambda j, k: (j, k)), pl.BlockSpec((tk, n), lambda j, k: (k, 0))],
        out_specs=[pl.BlockSpec((tmm, n), lambda j, k: (j, 0))],
        scratch_shapes=[pltpu.VMEM((tmm, n), F32)],
        args=(lhs_t, rhs))


def tn_matmul(lhs, rhs, name):
    t, m = lhs.shape
    n = rhs.shape[1]
    tk = _tile(t, 2048)
    nk = t // tk

    def body(a_ref, b_ref, o_ref, acc_ref):
        k = pl.program_id(0)
        _zero_when(k == 0, acc_ref)
        acc_ref[...] += _dot(a_ref[...], b_ref[...], TN)

        @pl.when(k == nk - 1)
        def _():
            o_ref[...] = acc_ref[...].astype(BF16)

    return pl.pallas_call(
        body, name=name, grid=(nk,),
        out_shape=jax.ShapeDtypeStruct((m, n), BF16),
        in_specs=[pl.BlockSpec((tk, m), lambda k: (k, 0)), pl.BlockSpec((tk, n), lambda k: (k, 0))],
        out_specs=pl.BlockSpec((m, n), lambda k: (0, 0)),
        scratch_shapes=[pltpu.VMEM((m, n), F32)],
        compiler_params=_params(("arbitrary",)),
    )(lhs, rhs)


def mixer_proj_forward(x, mod, g, w_in_t, cw, sw, nb, carried=None):
    t, d = x.shape
    s = t // nb
    tm = _tile(s, 512)
    pieces = [(0, cw, "bf16"), (cw, cw, "bf16"), (2 * cw, cw, "bf16"), (3 * cw, sw, "f32"),
              (3 * cw + sw, d, "sig"), (3 * cw + sw + d, d, "sig")]

    def body(x_ref, mod_ref, g_ref, w_ref, *outs):
        h_ref = outs[-1]
        sh, sc, _ = _mod_rows(mod_ref, 1)
        h, _, _ = _norm_mod(x_ref[...], g_ref[...], sh, sc)
        hb = h.astype(BF16)
        h_ref[...] = hb
        for (off, width, kind), o_ref in zip(pieces, outs[:-1]):
            ck = _tile(width, 512)
            for j in range(width // ck):
                p = _dot(hb, w_ref[off + j * ck:off + (j + 1) * ck, :], NT)
                if kind == "sig":
                    p = jax.nn.sigmoid(p)
                o_ref[:, j * ck:(j + 1) * ck] = p.astype(o_ref.dtype)

    tps = s // tm
    widths = [(cw, BF16), (cw, BF16), (cw, BF16), (sw, F32), (d, BF16), (d, BF16), (d, BF16)]
    return _call(
        body, name="mixer_proj_forward", grid=(nb, tps), carried=carried,
        out_shape=[jax.ShapeDtypeStruct((t, w), dt) for w, dt in widths],
        in_specs=[_row_spec(tm, d, tps), _mod_spec(d), _const((1, d)), _const(w_in_t.shape)],
        out_specs=[_row_spec(tm, w, tps) for w, _ in widths],
        args=(x, mod, g, w_in_t))


def mixer_proj_backward(dgb, dgc, dv, du, dgla, dglb, dxo, x, mod, g, w_in_t, nb, carried=None):
    t, d = x.shape
    s = t // nb
    tm = _tile(s, 512)
    parts = [dgb, dgc, dv, du, dgla, dglb]
    offs = [0]
    for p in parts:
        offs.append(offs[-1] + p.shape[1])

    def body(*refs):
        p_refs = refs[:6]
        dxo_ref, x_ref, mod_ref, g_ref, w_ref, dx_ref, dmod_ref, dg_ref = refs[6:]
        bi, i = pl.program_id(0), pl.program_id(1)
        _zero_when(i == 0, dmod_ref)
        _zero_when(jnp.logical_and(bi == 0, i == 0), dg_ref)
        dh = jnp.zeros((tm, d), F32)
        for p_ref, off in zip(p_refs, offs):
            width = p_ref.shape[1]
            ck = _tile(width, 512)
            for j in range(width // ck):
                dh = dh + _dot(p_ref[:, j * ck:(j + 1) * ck], w_ref[off + j * ck:off + (j + 1) * ck, :])
        sh, sc, _ = _mod_rows(mod_ref, 1)
        gv = g_ref[...]
        _, n, r = _norm_mod(x_ref[...], gv, sh, sc)
        dxn, dsh, dsc, dg = _norm_mod_bwd(dh, n, r, gv, sc)
        dx_ref[...] = dxo_ref[...] + dxn
        dmod_ref[...] += jnp.concatenate([dsh, dsc], axis=0)[None]
        dg_ref[...] += dg

    tps = s // tm
    rd = _row_spec(tm, d, tps)
    return _call(
        body, name="mixer_proj_backward", grid=(nb, tps), carried=carried,
        out_shape=[jax.ShapeDtypeStruct((t, d), F32), jax.ShapeDtypeStruct((nb, 2, d), F32),
                   jax.ShapeDtypeStruct((1, d), F32)],
        in_specs=[_row_spec(tm, p.shape[1], tps) for p in parts]
        + [rd, rd, _mod_spec(d), _const((1, d)), _const(w_in_t.shape)],
        out_specs=[rd, pl.BlockSpec((1, 2, d), lambda b, i: (b, 0, 0)), _full((1, d))],
        args=(*parts, dxo, x, mod, g, w_in_t))


GROUPS_PER_BLOCK = 8
ROWS = 8
SCAN_LANES = 512


def _scan_rows(xr, xi, masks, shifts):
    for (mr, mi), sft in zip(masks, shifts):
        sr, si = pltpu.roll(xr, sft, 0), pltpu.roll(xi, sft, 0)
        xr, xi = xr + mr * sr - mi * si, xi + mr * si + mi * sr
    return xr, xi


def _cmul_add(ar, ai, cr, ci, br, bi):
    return ar * cr - ai * ci + br, ar * ci + ai * cr + bi


def _segment_rows(perm_ref, x):
    return _dot(perm_ref[0], x).astype(BF16)


def _time_rows(perm_ref, x):
    hi = x.astype(BF16)
    lo = (x - hi.astype(F32)).astype(BF16)
    return _dot(perm_ref[1], hi) + _dot(perm_ref[1], lo)


def segment_permutation(tc):
    r = jnp.arange(tc)
    p = (r[:, None] % ROWS * (tc // ROWS) + r[:, None] // ROWS == r[None, :]).astype(BF16)
    return jnp.stack([p, p.T])


def _rows_at(k, offset=0):
    return pl.ds(pl.multiple_of(k * ROWS + offset, ROWS), ROWS)


def ssm_forward(u, bd, cd, a1, ml, nb):
    t, w = u.shape
    s = t // nb
    tc = _tile(s, 256)
    seg = tc // ROWS
    nq, ub, lq = bd.shape[1], bd.shape[2], bd.shape[3]
    nl = nq * lq
    nch = s // tc
    lw = min(nl, SCAN_LANES)

    def body(u_ref, perm_ref, bd_ref, cd_ref, a1_ref, ml_ref, y_ref, st_ref, xr_s, xi_s, car_s):
        i = pl.program_id(1)

        @pl.when(i == 0)
        def _():
            car_s[...] = jnp.zeros_like(car_s)

        ub16 = _segment_rows(perm_ref, u_ref[...].astype(BF16))
        for q in range(nq):
            lanes = slice(q * lq, (q + 1) * lq)
            uq = ub16[:, q * ub:(q + 1) * ub]
            xr_s[:, lanes] = _dot(uq, bd_ref[0, q])
            xi_s[:, lanes] = _dot(uq, bd_ref[1, q])
        row_is_0 = lax.broadcasted_iota(jnp.int32, (ROWS, lw), 0) == 0
        zero = jnp.zeros((ROWS, lw), F32)
        for j in range(nl // lw):
            lanes = slice(j * lw, (j + 1) * lw)
            ar, ai = a1_ref[0, :, lanes], a1_ref[1, :, lanes]

            def local(k, c):
                return _cmul_add(ar, ai, c[0], c[1], xr_s[_rows_at(k), lanes], xi_s[_rows_at(k), lanes])

            er, ei = lax.fori_loop(0, seg, local, (zero, zero))
            masks = [(ml_ref[d, 0, :, lanes], ml_ref[d, 1, :, lanes]) for d in range(3)]
            cr, ci = _scan_rows(jnp.where(row_is_0, car_s[0, :, lanes], pltpu.roll(er, 1, 0)),
                                jnp.where(row_is_0, car_s[1, :, lanes], pltpu.roll(ei, 1, 0)), masks, (1, 2, 4))
            st_ref[0, 0, :, lanes] = cr
            st_ref[0, 1, :, lanes] = ci

            def full(k, c):
                xr, xi = _cmul_add(ar, ai, c[0], c[1], xr_s[_rows_at(k), lanes], xi_s[_rows_at(k), lanes])
                xr_s[_rows_at(k), lanes] = xr
                xi_s[_rows_at(k), lanes] = xi
                return xr, xi

            fr, fi = lax.fori_loop(0, seg, full, (cr, ci))
            car_s[0, :, lanes] = jnp.broadcast_to(fr[ROWS - 1:ROWS], fr.shape)
            car_s[1, :, lanes] = jnp.broadcast_to(fi[ROWS - 1:ROWS], fi.shape)
        y = jnp.concatenate([_dot(xr_s[:, q * lq:(q + 1) * lq].astype(BF16), cd_ref[0, q])
                             + _dot(xi_s[:, q * lq:(q + 1) * lq].astype(BF16), cd_ref[1, q]) for q in range(nq)],
                            axis=1)
        y_ref[...] = _time_rows(perm_ref, y)

    perm = segment_permutation(tc)
    return pl.pallas_call(
        body, name="ssm_forward", grid=(nb, nch),
        out_shape=[jax.ShapeDtypeStruct((t, w), F32), jax.ShapeDtypeStruct((nb * nch, 2, ROWS, nl), F32)],
        in_specs=[pl.BlockSpec((tc, w), lambda b, i: (b * nch + i, 0)), _const(perm.shape), _const(bd.shape),
                  _const(cd.shape), _const(a1.shape), _const(ml.shape)],
        out_specs=[pl.BlockSpec((tc, w), lambda b, i: (b * nch + i, 0)),
                   pl.BlockSpec((1, 2, ROWS, nl), lambda b, i: (b * nch + i, 0, 0, 0))],
        scratch_shapes=[pltpu.VMEM((tc, nl), F32), pltpu.VMEM((tc, nl), F32), pltpu.VMEM((2, ROWS, nl), F32)],
        compiler_params=_params(("arbitrary", "arbitrary")),
    )(u, perm, bd, cd, a1, ml)


def ssm_backward(u, dy, st, bd, cd, a1, mlb, dskip, nb, carried=None):
    t, w = u.shape
    s = t // nb
    tc = _tile(s, 256)
    seg = tc // ROWS
    nq, ub, lq = bd.shape[1], bd.shape[2], bd.shape[3]
    nl = nq * lq
    nch = s // tc
    lw = min(nl, SCAN_LANES)

    def body(u_ref, dy_ref, st_ref, perm_ref, bd_ref, cd_ref, a1_ref, mlb_ref, dsk_ref,
             du_ref, dab_ref, dbd_ref, dcd_ref, xr_s, xi_s, lr_s, li_s, car_s):
        bi, i = pl.program_id(0), pl.program_id(1)
        first = jnp.logical_and(bi == 0, i == 0)

        @pl.when(i == 0)
        def _():
            car_s[...] = jnp.zeros_like(car_s)

        @pl.when(first)
        def _():
            dab_ref[...] = jnp.zeros_like(dab_ref)
            dbd_ref[...] = jnp.zeros_like(dbd_ref)
            dcd_ref[...] = jnp.zeros_like(dcd_ref)

        ub16 = _segment_rows(perm_ref, u_ref[...].astype(BF16))
        dyb16 = _segment_rows(perm_ref, dy_ref[...].astype(BF16))
        xr_s[0:ROWS, :] = st_ref[0, 0]
        xi_s[0:ROWS, :] = st_ref[0, 1]
        for q in range(nq):
            lanes = slice(q * lq, (q + 1) * lq)
            uq = ub16[:, q * ub:(q + 1) * ub]
            dq = dyb16[:, q * ub:(q + 1) * ub]
            xr_s[ROWS:, lanes] = _dot(uq, bd_ref[0, q])
            xi_s[ROWS:, lanes] = _dot(uq, bd_ref[1, q])
            lr_s[:, lanes] = _dot(dq, cd_ref[0, q], NT)
            li_s[:, lanes] = _dot(dq, cd_ref[1, q], NT)
        row_is_7 = lax.broadcasted_iota(jnp.int32, (ROWS, lw), 0) == ROWS - 1
        zero = jnp.zeros((ROWS, lw), F32)
        for j in range(nl // lw):
            lanes = slice(j * lw, (j + 1) * lw)
            ar, ai = a1_ref[0, :, lanes], a1_ref[1, :, lanes]
            nai = -ai

            def states(k, c):
                xr, xi = _cmul_add(ar, ai, c[0], c[1], xr_s[_rows_at(k, ROWS), lanes], xi_s[_rows_at(k, ROWS), lanes])
                xr_s[_rows_at(k, ROWS), lanes] = xr
                xi_s[_rows_at(k, ROWS), lanes] = xi
                return xr, xi

            lax.fori_loop(0, seg, states, (st_ref[0, 0, :, lanes], st_ref[0, 1, :, lanes]))

            def local(kk, c):
                k = seg - 1 - kk
                return _cmul_add(ar, nai, c[0], c[1], lr_s[_rows_at(k), lanes], li_s[_rows_at(k), lanes])

            er, ei = lax.fori_loop(0, seg, local, (zero, zero))
            masks = [(mlb_ref[d, 0, :, lanes], mlb_ref[d, 1, :, lanes]) for d in range(3)]
            cr, ci = _scan_rows(jnp.where(row_is_7, car_s[0, :, lanes], pltpu.roll(er, ROWS - 1, 0)),
                                jnp.where(row_is_7, car_s[1, :, lanes], pltpu.roll(ei, ROWS - 1, 0)), masks, (7, 6, 4))

            def full(kk, c):
                cr_, ci_, accr, acci = c
                k = seg - 1 - kk
                lr, li = _cmul_add(ar, nai, cr_, ci_, lr_s[_rows_at(k), lanes], li_s[_rows_at(k), lanes])
                lr_s[_rows_at(k), lanes] = lr
                li_s[_rows_at(k), lanes] = li
                xpr, xpi = xr_s[_rows_at(k), lanes], xi_s[_rows_at(k), lanes]
                return lr, li, accr + lr * xpr + li * xpi, acci + li * xpr - lr * xpi

            lr0, li0, accr, acci = lax.fori_loop(0, seg, full, (cr, ci, zero, zero))
            car_s[0, :, lanes] = jnp.broadcast_to(lr0[0:1], lr0.shape)
            car_s[1, :, lanes] = jnp.broadcast_to(li0[0:1], li0.shape)
            dab_ref[0, :, lanes] += accr
            dab_ref[1, :, lanes] += acci
        du_parts = []
        for q in range(nq):
            lanes = slice(q * lq, (q + 1) * lq)
            cols = slice(q * ub, (q + 1) * ub)
            lrb, lib = lr_s[:, lanes].astype(BF16), li_s[:, lanes].astype(BF16)
            uq, dq = ub16[:, cols], dyb16[:, cols]
            du_parts.append(_dot(lrb, bd_ref[0, q], NT) + _dot(lib, bd_ref[1, q], NT))
            dbd_ref[0, q] += _dot(uq, lrb, TN)
            dbd_ref[1, q] += _dot(uq, lib, TN)
            dcd_ref[0, q] += _dot(xr_s[ROWS:, lanes].astype(BF16), dq, TN)
            dcd_ref[1, q] += _dot(xi_s[ROWS:, lanes].astype(BF16), dq, TN)
        du = _time_rows(perm_ref, jnp.concatenate(du_parts, axis=1)) + dsk_ref[...] * dy_ref[...]
        du_ref[...] = du.astype(BF16)

    rev = lambda b, i: (b * nch + nch - 1 - i, 0)
    perm = segment_permutation(tc)
    return _call(
        body, name="ssm_backward", grid=(nb, nch), carried=carried,
        out_shape=[jax.ShapeDtypeStruct((t, w), BF16), jax.ShapeDtypeStruct((2, ROWS, nl), F32),
                   jax.ShapeDtypeStruct(bd.shape, F32), jax.ShapeDtypeStruct(cd.shape, F32)],
        in_specs=[pl.BlockSpec((tc, w), rev), pl.BlockSpec((tc, w), rev),
                  pl.BlockSpec((1, 2, ROWS, nl), lambda b, i: (b * nch + nch - 1 - i, 0, 0, 0)),
                  _const(perm.shape), _const(bd.shape), _const(cd.shape), _const(a1.shape), _const(mlb.shape),
                  _const((1, w))],
        out_specs=[pl.BlockSpec((tc, w), rev), _full((2, ROWS, nl)), _full(bd.shape), _full(cd.shape)],
        scratch_shapes=[pltpu.VMEM((tc + ROWS, nl), F32), pltpu.VMEM((tc + ROWS, nl), F32),
                        pltpu.VMEM((tc, nl), F32), pltpu.VMEM((tc, nl), F32), pltpu.VMEM((2, ROWS, nl), F32)],
        args=(u, dy, st, perm, bd, cd, a1, mlb, dskip))


def ssm_discretise(a_re, a_im, b_re, b_im, log_dt):
    dt = jnp.exp(log_dt)[:, None]
    er = jnp.exp(a_re * dt)
    abr, abi = er * jnp.cos(a_im * dt), er * jnp.sin(a_im * dt)
    den = a_re * a_re + a_im * a_im
    nr, ni = abr - 1.0, abi
    fr = ((nr * a_re + ni * a_im) / den)[..., None]
    fi = ((ni * a_re - nr * a_im) / den)[..., None]
    return abr, abi, fr * b_re - fi * b_im, fr * b_im + fi * b_re


def _complex_square(zr, zi):
    return zr * zr - zi * zi, 2.0 * zr * zi


def ssm_tables(abr, abi, bbr, bbi, c_re, c_im, seg):
    g, p, h = bbr.shape
    nq = g // GROUPS_PER_BLOCK
    zr, zi = abr.reshape(1, -1), abi.reshape(1, -1)
    a1 = jnp.stack([jnp.broadcast_to(zr, (ROWS, g * p)), jnp.broadcast_to(zi, (ROWS, g * p))])
    for _ in range(seg.bit_length() - 1):
        zr, zi = _complex_square(zr, zi)
    row = jnp.arange(ROWS)[:, None]
    ml, mlb = [], []
    for d in (1, 2, 4):
        ml.append(jnp.stack([jnp.where(row >= d, zr, 0.0), jnp.where(row >= d, zi, 0.0)]))
        mlb.append(jnp.stack([jnp.where(row + d < ROWS, zr, 0.0), jnp.where(row + d < ROWS, -zi, 0.0)]))
        zr, zi = _complex_square(zr, zi)
    eye = jnp.eye(GROUPS_PER_BLOCK, dtype=F32)

    def block_diag_in(bb):
        bq = bb.reshape(nq, GROUPS_PER_BLOCK, p, h)
        return jnp.einsum("qaph,ab->qahbp", bq, eye).reshape(nq, GROUPS_PER_BLOCK * h, GROUPS_PER_BLOCK * p)

    def block_diag_out(cc):
        cq = cc.reshape(nq, GROUPS_PER_BLOCK, h, p)
        return jnp.einsum("qahp,ab->qapbh", cq, eye).reshape(nq, GROUPS_PER_BLOCK * p, GROUPS_PER_BLOCK * h)

    bd = jnp.stack([block_diag_in(bbr), block_diag_in(bbi)]).astype(BF16)
    cd = jnp.stack([block_diag_out(c_re), block_diag_out(-c_im)]).astype(BF16)
    return bd, cd, a1, jnp.stack(ml), jnp.stack(mlb)


def ssm_table_grads(dab, dbd, dcd, g, p, h):
    nq = g // GROUPS_PER_BLOCK
    dabr, dabi = dab[0].sum(0).reshape(g, p), dab[1].sum(0).reshape(g, p)
    b5 = dbd.reshape(2, nq, GROUPS_PER_BLOCK, h, GROUPS_PER_BLOCK, p)
    dbb = jnp.einsum("rqahap->rqaph", b5).reshape(2, g, p, h)
    c5 = dcd.reshape(2, nq, GROUPS_PER_BLOCK, p, GROUPS_PER_BLOCK, h)
    dcc = jnp.einsum("rqapah->rqahp", c5).reshape(2, g, h, p)
    return dabr, dabi, dbb[0], dbb[1], dcc[0], -dcc[1]


HALO = 16


def _conv_inputs(gc_ref, v_ref, gch_ref, vh_ref, cv_s, i, tm):
    cv = gc_ref[...].astype(F32) * v_ref[...].astype(F32)
    halo = gch_ref[...].astype(F32) * vh_ref[...].astype(F32)
    cv_s[0:HALO, :] = jnp.where(i == 0, 0.0, halo)
    cv_s[HALO:, :] = cv
    return cv, cv_s[HALO - 1:HALO - 1 + tm, :], cv_s[HALO - 2:HALO - 2 + tm, :]


def _halo_spec(tm, width, tiles_per_seq):
    per = tm // HALO
    return pl.BlockSpec((HALO, width), lambda b, i: (jnp.maximum((b * tiles_per_seq + i) * per - 1, 0), 0))


def mixer_merge_forward(x, gb, gc, v, sga, sgb, yssm, u, mod, conv_w, dskip, wco, wglu, wso_t, wout, nb):
    t, d = x.shape
    cw, sw = gb.shape[1], u.shape[1]
    s = t // nb
    tm = _tile(s, 256)
    tps = s // tm

    def body(x_ref, gb_ref, gc_ref, v_ref, gch_ref, vh_ref, sga_ref, sgb_ref, ys_ref, u_ref, mod_ref, cw_ref,
             dsk_ref, wco_ref, wglu_ref, wso_ref, wout_ref, xo_ref, ya_ref, yb_ref, mix_ref, cv_s):
        i = pl.program_id(1)
        cv, cv1, cv2 = _conv_inputs(gc_ref, v_ref, gch_ref, vh_ref, cv_s, i, tm)
        w = cw_ref[...]
        conv = w[0:1] * cv2 + w[1:2] * cv1 + w[2:3] * cv
        ya = _dot((gb_ref[...].astype(F32) * conv).astype(BF16), wco_ref[...])
        s0 = ys_ref[...] + dsk_ref[...] * u_ref[...]
        s1, _ = _gelu(s0)
        z = _dot(s1.astype(BF16), wglu_ref[...])
        s2 = s1 * jax.nn.sigmoid(z)
        yb = _dot(s2.astype(BF16), wso_ref[...], NT)
        merged = sga_ref[...].astype(F32) * ya + sgb_ref[...].astype(F32) * yb
        mix = _dot(merged.astype(BF16), wout_ref[...])
        _, _, gt = _mod_rows(mod_ref, 1)
        xo_ref[...] = x_ref[...] + gt * mix
        ya_ref[...] = ya.astype(BF16)
        yb_ref[...] = yb.astype(BF16)
        mix_ref[...] = mix.astype(BF16)

    rd, rc, rw = _row_spec(tm, d, tps), _row_spec(tm, cw, tps), _row_spec(tm, sw, tps)
    hc = _halo_spec(tm, cw, tps)
    return pl.pallas_call(
        body, name="mixer_merge_forward", grid=(nb, tps),
        out_shape=[jax.ShapeDtypeStruct((t, d), F32)] + [jax.ShapeDtypeStruct((t, d), BF16)] * 3,
        in_specs=[rd, rc, rc, rc, hc, hc, rd, rd, rw, rw, _mod_spec(d), _const(conv_w.shape), _const((1, sw)),
                  _const(wco.shape), _const(wglu.shape), _const(wso_t.shape), _const(wout.shape)],
        out_specs=[rd, rd, rd, rd],
        scratch_shapes=[pltpu.VMEM((tm + HALO, cw), F32)],
        compiler_params=_params(("arbitrary", "arbitrary")),
    )(x, gb, gc, v, gc, v, sga, sgb, yssm, u, mod, conv_w, dskip, wco, wglu, wso_t, wout)


def mixer_merge_backward(dxo, mix, ya, yb, gb, gc, v, sga, sgb, yssm, u, mod, conv_w, dskip,
                         wco, wglu, wso_t, wout, nb, carried=None):
    t, d = dxo.shape
    cw, sw = gb.shape[1], u.shape[1]
    s = t // nb
    tm = _tile(s, 256)
    tps = s // tm

    def body(dxo_ref, mix_ref, ya_ref, yb_ref, gb_ref, gc_ref, v_ref, gch_ref, vh_ref, sga_ref, sgb_ref, ys_ref,
             u_ref, mod_ref, cw_ref, dsk_ref, wco_ref, wglu_ref, wso_ref, wout_ref,
             dgla_ref, dglb_ref, dgb_ref, dconv_ref, ds0_ref, dgt_ref, ddsk_ref, dwout_ref, dwco_ref, dwso_ref,
             dwglu_ref, cv_s):
        bi, i = pl.program_id(0), pl.program_id(1)
        _zero_when(i == 0, dgt_ref)
        _zero_when(jnp.logical_and(bi == 0, i == 0), ddsk_ref, dwout_ref, dwco_ref, dwso_ref, dwglu_ref)
        dxo = dxo_ref[...]
        _, _, gt = _mod_rows(mod_ref, 1)
        dmix = (gt * dxo).astype(BF16)
        dgt_ref[...] += jnp.sum(dxo * mix_ref[...].astype(F32), axis=0, keepdims=True)[None]
        ya, yb = ya_ref[...].astype(F32), yb_ref[...].astype(F32)
        sga, sgb = sga_ref[...].astype(F32), sgb_ref[...].astype(F32)
        merged = (sga * ya + sgb * yb).astype(BF16)
        dwout_ref[...] += _dot(merged, dmix, TN)
        dmerged = _dot(dmix, wout_ref[...], NT)
        dgla_ref[...] = (dmerged * ya * sga * (1.0 - sga)).astype(BF16)
        dglb_ref[...] = (dmerged * yb * sgb * (1.0 - sgb)).astype(BF16)
        dya = (dmerged * sga).astype(BF16)
        dyb = (dmerged * sgb).astype(BF16)
        cv, cv1, cv2 = _conv_inputs(gc_ref, v_ref, gch_ref, vh_ref, cv_s, i, tm)
        w = cw_ref[...]
        conv = w[0:1] * cv2 + w[1:2] * cv1 + w[2:3] * cv
        gbv = gb_ref[...].astype(F32)
        dwco_ref[...] += _dot((gbv * conv).astype(BF16), dya, TN)
        dya_in = _dot(dya, wco_ref[...], NT)
        dgb_ref[...] = (dya_in * conv).astype(BF16)
        dconv_ref[...] = dya_in * gbv
        uv = u_ref[...]
        s0 = ys_ref[...] + dsk_ref[...] * uv
        s1, th = _gelu(s0)
        s1b = s1.astype(BF16)
        sz = jax.nn.sigmoid(_dot(s1b, wglu_ref[...]))
        s2b = (s1 * sz).astype(BF16)
        dwso_ref[...] += _dot(dyb, s2b, TN)
        ds2 = _dot(dyb, wso_ref[...])
        dz = (ds2 * s1 * sz * (1.0 - sz)).astype(BF16)
        dwglu_ref[...] += _dot(s1b, dz, TN)
        ds1 = ds2 * sz + _dot(dz, wglu_ref[...], NT)
        ds0 = ds1 * _gelu_grad(s0, th)
        ds0_ref[...] = ds0
        ddsk_ref[...] += jnp.sum(ds0 * uv, axis=0, keepdims=True)

    rd, rc, rw = _row_spec(tm, d, tps), _row_spec(tm, cw, tps), _row_spec(tm, sw, tps)
    hc = _halo_spec(tm, cw, tps)
    return _call(
        body, name="mixer_merge_backward", grid=(nb, tps), carried=carried,
        out_shape=[jax.ShapeDtypeStruct((t, d), BF16), jax.ShapeDtypeStruct((t, d), BF16),
                   jax.ShapeDtypeStruct((t, cw), BF16), jax.ShapeDtypeStruct((t, cw), F32),
                   jax.ShapeDtypeStruct((t, sw), F32), jax.ShapeDtypeStruct((nb, 1, d), F32),
                   jax.ShapeDtypeStruct((1, sw), F32), jax.ShapeDtypeStruct(wout.shape, F32),
                   jax.ShapeDtypeStruct(wco.shape, F32), jax.ShapeDtypeStruct(wso_t.shape, F32),
                   jax.ShapeDtypeStruct(wglu.shape, F32)],
        in_specs=[rd, rd, rd, rd, rc, rc, rc, hc, hc, rd, rd, rw, rw, _mod_spec(d), _const(conv_w.shape),
                  _const((1, sw)), _const(wco.shape), _const(wglu.shape), _const(wso_t.shape), _const(wout.shape)],
        out_specs=[rd, rd, rc, rc, rw, pl.BlockSpec((1, 1, d), lambda b, i: (b, 0, 0)), _full((1, sw)),
                   _full(wout.shape), _full(wco.shape), _full(wso_t.shape), _full(wglu.shape)],
        scratch_shapes=[pltpu.VMEM((tm + HALO, cw), F32)],
        args=(dxo, mix, ya, yb, gb, gc, v, gc, v, sga, sgb, yssm, u, mod, conv_w, dskip, wco, wglu, wso_t, wout))


def conv_backward(dconv, gc, v, conv_w, nb):
    t, cw = dconv.shape
    s = t // nb
    tm = _tile(s, 512)
    tps = s // tm
    per = tm // ROWS
    slab = _tile(tm, 32)

    def body(dc_ref, dcn_ref, gc_ref, v_ref, gch_ref, vh_ref, cw_ref, dgc_ref, dv_ref, dw_ref, cv_s, dc_s):
        bi, i = pl.program_id(0), pl.program_id(1)
        _zero_when(jnp.logical_and(bi == 0, i == 0), dw_ref)
        cv_s[0:HALO, :] = jnp.where(i == 0, 0.0, gch_ref[...].astype(F32) * vh_ref[...].astype(F32))
        dc_s[tm:, :] = jnp.where(i == tps - 1, 0.0, dcn_ref[...])
        for r in range(0, tm, slab):
            cv_s[HALO + r:HALO + r + slab, :] = (gc_ref[r:r + slab, :].astype(F32)
                                                 * v_ref[r:r + slab, :].astype(F32))
            dc_s[r:r + slab, :] = dc_ref[r:r + slab, :]
        w = cw_ref[...]
        sums = [jnp.zeros((ROWS, cw), F32)] * CONV_K
        for r in range(0, tm, slab):
            dc = dc_s[r:r + slab, :]
            dcv = w[2:3] * dc + w[1:2] * dc_s[r + 1:r + 1 + slab, :] + w[0:1] * dc_s[r + 2:r + 2 + slab, :]
            dgc_ref[r:r + slab, :] = (dcv * v_ref[r:r + slab, :].astype(F32)).astype(BF16)
            dv_ref[r:r + slab, :] = (dcv * gc_ref[r:r + slab, :].astype(F32)).astype(BF16)
            for k in range(CONV_K):
                lag = HALO + r - (CONV_K - 1 - k)
                prod = dc * cv_s[lag:lag + slab, :]
                sums[k] = sums[k] + jnp.sum(prod.reshape(slab // ROWS, ROWS, cw), axis=0)
        dw_ref[...] += jnp.concatenate([jnp.sum(a, axis=0, keepdims=True) for a in sums], axis=0)

    rc = _row_spec(tm, cw, tps)
    nxt = pl.BlockSpec((ROWS, cw), lambda b, i: (jnp.minimum((b * tps + i + 1) * per, nb * tps * per - 1), 0))
    hc = _halo_spec(tm, cw, tps)
    return pl.pallas_call(
        body, name="conv_backward", grid=(nb, tps),
        out_shape=[jax.ShapeDtypeStruct((t, cw), BF16), jax.ShapeDtypeStruct((t, cw), BF16),
                   jax.ShapeDtypeStruct(conv_w.shape, F32)],
        in_specs=[rc, nxt, rc, rc, hc, hc, _full(conv_w.shape)],
        out_specs=[rc, rc, _full(conv_w.shape)],
        scratch_shapes=[pltpu.VMEM((tm + HALO, cw), F32), pltpu.VMEM((tm + ROWS, cw), F32)],
        compiler_params=_params(("arbitrary", "arbitrary")),
    )(dconv, dconv, gc, v, gc, v, conv_w)


def loss_forward_backward(x, target, g):
    t, d = x.shape
    tm = _tile(t, 512)

    def body(x_ref, t_ref, g_ref, l_ref, dx_ref, dg_ref):
        _zero_when(pl.program_id(0) == 0, dg_ref)
        xv = x_ref[...]
        gv = g_ref[...]
        r = lax.rsqrt(jnp.mean(xv * xv, axis=-1, keepdims=True) + EPS)
        n = xv * r
        err = n * gv - t_ref[...]
        l_ref[...] = jnp.full(l_ref.shape, 0.5 * jnp.sum(jnp.mean(err * err, axis=-1)), F32)
        dy = err * (1.0 / d)
        dn = dy * gv
        dx_ref[...] = r * (dn - n * jnp.mean(n * dn, axis=-1, keepdims=True))
        dg_ref[...] += jnp.sum(dy * n, axis=0, keepdims=True)

    row = pl.BlockSpec((tm, d), lambda i: (i, 0))
    return pl.pallas_call(
        body, name="loss_forward_backward", grid=(t // tm,),
        out_shape=[jax.ShapeDtypeStruct((t // tm, 1, 128), F32), jax.ShapeDtypeStruct((t, d), F32),
                   jax.ShapeDtypeStruct((1, d), F32)],
        in_specs=[row, row, _full((1, d))],
        out_specs=[pl.BlockSpec((1, 1, 128), lambda i: (i, 0, 0)), row, _full((1, d))],
        compiler_params=_params(("arbitrary",)),
    )(x, target, g)


def sum_slots(slots, name):
    _, r, c = slots.shape
    tr = _tile(r, 352) if r % 352 == 0 else _tile(r, 256)

    def body(s_ref, o_ref):
        acc = s_ref[0].astype(F32)
        for j in range(1, N_DEV):
            acc = acc + s_ref[j].astype(F32)
        o_ref[...] = acc

    return pl.pallas_call(
        body, name=name, grid=(r // tr,),
        out_shape=jax.ShapeDtypeStruct((r, c), F32),
        in_specs=[pl.BlockSpec((N_DEV, tr, c), lambda i: (0, i, 0))],
        out_specs=pl.BlockSpec((tr, c), lambda i: (i, 0)),
        compiler_params=_params(("arbitrary",)),
    )(slots)


def adamw_update(w, g, m, v, name):
    r, c = w.shape
    tr = _tile(r, 256) if r % 8 == 0 else r

    def body(w_ref, g_ref, m_ref, v_ref, d_ref, mo_ref, vo_ref):
        d_ref[...], mo_ref[...], vo_ref[...] = _adamw(w_ref[...], g_ref[...], m_ref[...], v_ref[...])

    spec = pl.BlockSpec((tr, c), lambda i: (i, 0))
    return pl.pallas_call(
        body, name=name, grid=(r // tr,),
        out_shape=[jax.ShapeDtypeStruct((r, c), F32)] * 3,
        in_specs=[spec] * 4, out_specs=[spec] * 3,
        compiler_params=_params(("arbitrary",)),
    )(w, g, m, v)


def sum_adamw_update(slots, w, m, v, name):
    _, r, c = slots.shape
    tr = _tile(r, 352) if r % 352 == 0 else _tile(r, 256)

    def body(s_ref, w_ref, m_ref, v_ref, g_ref, d_ref, mo_ref, vo_ref):
        g = s_ref[0].astype(F32)
        for j in range(1, N_DEV):
            g = g + s_ref[j].astype(F32)
        g_ref[...] = g
        d_ref[...], mo_ref[...], vo_ref[...] = _adamw(w_ref[...], g, m_ref[...], v_ref[...])

    spec = pl.BlockSpec((tr, c), lambda i: (i, 0))
    return pl.pallas_call(
        body, name=name, grid=(r // tr,),
        out_shape=[jax.ShapeDtypeStruct((r, c), F32)] * 4,
        in_specs=[pl.BlockSpec((N_DEV, tr, c), lambda i: (0, i, 0))] + [spec] * 3, out_specs=[spec] * 4,
        compiler_params=_params(("arbitrary",)),
    )(slots, w, m, v)


def adamw_update_small(ws, gs, ms, vs):
    n = len(ws)

    def body(*refs):
        w_r, g_r, m_r, v_r = refs[:n], refs[n:2 * n], refs[2 * n:3 * n], refs[3 * n:4 * n]
        d_r, mo_r, vo_r = refs[4 * n:5 * n], refs[5 * n:6 * n], refs[6 * n:7 * n]
        for k in range(n):
            d_r[k][...], mo_r[k][...], vo_r[k][...] = _adamw(w_r[k][...], g_r[k][...], m_r[k][...], v_r[k][...])

    shapes = [jax.ShapeDtypeStruct(w.shape, F32) for w in ws]
    out = pl.pallas_call(body, name="adamw_update_small", out_shape=shapes * 3,
                         compiler_params=_params())(*ws, *gs, *ms, *vs)
    return out[:n], out[n:2 * n], out[2 * n:]


def _slots(grad_t):
    return grad_t.reshape(N_DEV, grad_t.shape[0] // N_DEV, grad_t.shape[1])


def kernel(x, c, w_ada, b_ada, g_ffn1, w1_a, w3_a, w2_a, g_mix, w_in, conv_w, w_conv_out, a_re, a_im, b_re, b_im, c_re, c_im, log_dt, d_skip, w_glu, w_ssm_out, w_out, g_ffn2, w1_b, w3_b, w2_b, g_final, loss_target, m_w_ada, m_b_ada, m_g_ffn1, m_w1_a, m_w3_a, m_w2_a, m_g_mix, m_w_in, m_conv_w, m_w_conv_out, m_a_re, m_a_im, m_b_re, m_b_im, m_c_re, m_c_im, m_log_dt, m_d_skip, m_w_glu, m_w_ssm_out, m_w_out, m_g_ffn2, m_w1_b, m_w3_b, m_w2_b, m_g_final, v_w_ada, v_b_ada, v_g_ffn1, v_w1_a, v_w3_a, v_w2_a, v_g_mix, v_w_in, v_conv_w, v_w_conv_out, v_a_re, v_a_im, v_b_re, v_b_im, v_c_re, v_c_im, v_log_dt, v_d_skip, v_w_glu, v_w_ssm_out, v_w_out, v_g_ffn2, v_w1_b, v_w3_b, v_w2_b, v_g_final):
    nb, s, d = x.shape
    t = nb * s
    me = 4 * lax.axis_index("x") + 2 * lax.axis_index("y") + lax.axis_index("c")
    g_n, p_n, h_n = b_re.shape[1:]
    cw_n = w_conv_out.shape[1] * N_DEV
    sw_n = w_ssm_out.shape[1]
    glu_fold = d // w_glu.shape[2]

    weights = dict(w_ada=w_ada, b_ada=b_ada, g_ffn1=g_ffn1, w1_a=w1_a, w3_a=w3_a, w2_a=w2_a, g_mix=g_mix, w_in=w_in,
                   conv_w=conv_w, w_conv_out=w_conv_out, a_re=a_re, a_im=a_im, b_re=b_re, b_im=b_im, c_re=c_re,
                   c_im=c_im, log_dt=log_dt, d_skip=d_skip, w_glu=w_glu, w_ssm_out=w_ssm_out, w_out=w_out,
                   g_ffn2=g_ffn2, w1_b=w1_b, w3_b=w3_b, w2_b=w2_b, g_final=g_final)
    mom1 = dict(w_ada=m_w_ada, b_ada=m_b_ada, g_ffn1=m_g_ffn1, w1_a=m_w1_a, w3_a=m_w3_a, w2_a=m_w2_a, g_mix=m_g_mix,
                w_in=m_w_in, conv_w=m_conv_w, w_conv_out=m_w_conv_out, a_re=m_a_re, a_im=m_a_im, b_re=m_b_re,
                b_im=m_b_im, c_re=m_c_re, c_im=m_c_im, log_dt=m_log_dt, d_skip=m_d_skip, w_glu=m_w_glu,
                w_ssm_out=m_w_ssm_out, w_out=m_w_out, g_ffn2=m_g_ffn2, w1_b=m_w1_b, w3_b=m_w3_b, w2_b=m_w2_b,
                g_final=m_g_final)
    mom2 = dict(w_ada=v_w_ada, b_ada=v_b_ada, g_ffn1=v_g_ffn1, w1_a=v_w1_a, w3_a=v_w3_a, w2_a=v_w2_a, g_mix=v_g_mix,
                w_in=v_w_in, conv_w=v_conv_w, w_conv_out=v_w_conv_out, a_re=v_a_re, a_im=v_a_im, b_re=v_b_re,
                b_im=v_b_im, c_re=v_c_re, c_im=v_c_im, log_dt=v_log_dt, d_skip=v_d_skip, w_glu=v_w_glu,
                w_ssm_out=v_w_ssm_out, w_out=v_w_out, g_ffn2=v_g_ffn2, w1_b=v_w1_b, w3_b=v_w3_b, w2_b=v_w2_b,
                g_final=v_g_final)
    names = list(weights)
    transposed = ("w1_a", "w3_a", "w_in", "w_ssm_out", "w1_b", "w3_b")
    groups = dict(ffn_a=("w1_a", "w3_a", "w2_a"), mixer=("w_in", "w_conv_out", "w_glu", "w_ssm_out", "w_out"),
                  ffn_b=("w1_b", "w3_b", "w2_b"))
    big = groups["ffn_a"] + groups["mixer"] + groups["ffn_b"]

    pad_rows = lambda a: jnp.pad(a, ((0, -a.shape[0] % GATHER_ROWS), (0, 0)))
    c_all, conv_all = run_stage(GatherStage([pad_rows(c), pad_rows(conv_w[0])]), "gather_cond")
    c_all = c_all.reshape(N_DEV, -1, d)[:, :nb].reshape(N_DEV * nb, d)
    conv_full = conv_all.reshape(N_DEV, GATHER_ROWS, -1)[:, :CONV_K].transpose(1, 0, 2).reshape(CONV_K, cw_n)
    ada_cols = w_ada.shape[2]
    b_cols = lax.dynamic_slice(b_ada, (0, me * ada_cols), (1, ada_cols))
    mod_cols = ada_forward(c_all, w_ada[0], b_cols)
    (mod_all,) = run_stage(GatherStage([mod_cols]), "gather_mod")
    mod_mine = lax.dynamic_slice(mod_all.reshape(N_DEV, N_DEV * nb, ada_cols), (0, me * nb, 0), (N_DEV, nb, ada_cols))
    mod = mod_mine.transpose(1, 0, 2).reshape(nb, N_MOD, d)

    def shard_rows(name):
        w = weights[name][0]
        if name in transposed:
            w = w.T
        if name == "w_glu":
            w = w.reshape(w.shape[0] // glu_fold, d)
        return w.astype(BF16)

    def gather_stage(group):
        return GatherStage([shard_rows(n) for n in groups[group]])

    gw = {}

    def keep_weights(group, outs):
        for n, w in zip(groups[group], outs):
            gw[n] = w.reshape(sw_n, sw_n) if n == "w_glu" else w

    disc_in = (a_re[0], a_im[0], b_re[0], b_im[0], log_dt[0])
    (abr, abi, bbr, bbi), disc_vjp = jax.vjp(ssm_discretise, *disc_in)
    bd, cd, abar8, ml, mlb = ssm_tables(abr, abi, bbr, bbi, c_re[0], c_im[0], _tile(s, 256) // ROWS)

    x0 = x.reshape(t, d)
    keep_weights("ffn_a", run_stage(gather_stage("ffn_a"), "gather_ffn_a"))
    (x1, a1, b1, f1), got = ffn_forward(x0, mod, g_ffn1, gw["w1_a"], gw["w3_a"], gw["w2_a"], 0, nb,
                                        "ffn_a_forward", carried=gather_stage("mixer"))
    keep_weights("mixer", got)
    (gb, gc, vv, u, sga, sgb, h2), got = mixer_proj_forward(x1, mod, g_mix, gw["w_in"], cw_n, sw_n, nb,
                                                            carried=gather_stage("ffn_b"))
    keep_weights("ffn_b", got)
    yssm, st = ssm_forward(u, bd, cd, abar8, ml, nb)
    x2, ya, yb, mix = mixer_merge_forward(x1, gb, gc, vv, sga, sgb, yssm, u, mod, conv_full, d_skip,
                                          gw["w_conv_out"], gw["w_glu"], gw["w_ssm_out"], gw["w_out"], nb)
    (x3, a3, b3, f3), _ = ffn_forward(x2, mod, g_ffn2, gw["w1_b"], gw["w3_b"], gw["w2_b"], 2, nb, "ffn_b_forward")
    loss_parts, dx3, dg_final = loss_forward_backward(x3, loss_target.reshape(t, d), g_final.reshape(1, d))
    loss = lax.psum(jnp.sum(loss_parts[:, 0, 0]), ("x", "y", "c"))

    part, received = {}, {}

    def exchange_stage(ns):
        return ExchangeStage([_slots(part[n]) for n in ns])

    (da3, db3, sw3, df3, dgt3), _ = ffn_backward_hidden(dx3, a3, b3, f3, mod, gw["w2_b"], 2, nb,
                                                        "ffn_b_backward_hidden")
    (dx2, h3, dmod3, dg_ffn2), _ = ffn_backward_input(dx3, x2, da3, db3, mod, g_ffn2, gw["w1_b"], gw["w3_b"],
                                                      2, nb, "ffn_b_backward_input")
    (part["w1_b"],), _ = nn_matmul(da3, h3, "grad_w1_b")
    (part["w3_b"],), _ = nn_matmul(db3, h3, "grad_w3_b")
    (part["w2_b"],), _ = nn_matmul(sw3, df3, "grad_w2_b")
    (dgla, dglb, dgb, dconv, ds0, dgt2, dd_skip, dw_out, dw_co, dw_so_t, dw_glu), got = mixer_merge_backward(
        dx2, mix, ya, yb, gb, gc, vv, sga, sgb, yssm, u, mod, conv_full, d_skip,
        gw["w_conv_out"], gw["w_glu"], gw["w_ssm_out"], gw["w_out"], nb, carried=exchange_stage(groups["ffn_b"]))
    received.update(zip(groups["ffn_b"], got))
    part["w_out"] = dw_out.astype(BF16)
    part["w_conv_out"] = dw_co.astype(BF16)
    part["w_ssm_out"] = dw_so_t.astype(BF16)
    part["w_glu"] = dw_glu.reshape(sw_n // glu_fold, d).astype(BF16)
    (du, dab, dbd, dcd), got = ssm_backward(u, ds0, st, bd, cd, abar8, mlb, d_skip, nb,
                                            carried=exchange_stage(groups["mixer"][1:]))
    received.update(zip(groups["mixer"][1:], got))
    dgc, dvv, dconv_w = conv_backward(dconv, gc, vv, conv_full, nb)
    part["w_in"] = jnp.concatenate([tn_matmul(p, h2, "grad_w_in_%d" % k)
                                    for k, p in enumerate((dgb, dgc, dvv, du, dgla, dglb))], axis=0)
    (dx1, dmod2, dg_mix), got = mixer_proj_backward(dgb, dgc, dvv, du, dgla, dglb, dx2, x1, mod, g_mix, gw["w_in"], nb,
                                                    carried=exchange_stage(("w_in",)))
    received["w_in"] = got[0]
    (da1, db1, sw1, df1, dgt1), _ = ffn_backward_hidden(dx1, a1, b1, f1, mod, gw["w2_a"], 0, nb,
                                                        "ffn_a_backward_hidden")
    (part["w2_a"],), _ = nn_matmul(sw1, df1, "grad_w2_a")
    (dx0, h1, dmod1, dg_ffn1), _ = ffn_backward_input(dx1, x0, da1, db1, mod, g_ffn1, gw["w1_a"], gw["w3_a"],
                                                      0, nb, "ffn_a_backward_input")
    dabr, dabi, dbbr, dbbi, dcr, dci = ssm_table_grads(dab, dbd, dcd, g_n, p_n, h_n)
    gmod = jnp.concatenate([dmod1, dgt1, dmod2, dgt2, dmod3, dgt3], axis=1)
    small = dict(gmod=gmod, g_ffn1=dg_ffn1, g_mix=dg_mix, g_ffn2=dg_ffn2, g_final=dg_final, d_skip=dd_skip,
                 abr=dabr, abi=dabi, bbr=dbbr, bbi=dbbi, c_re=dcr, c_im=dci, conv_w=dconv_w)
    flat = jnp.concatenate([a.reshape(-1) for a in small.values()])
    n_small = flat.shape[0]
    n_rows = -(-n_small // (128 * GATHER_ROWS)) * GATHER_ROWS
    flat = jnp.pad(flat, (0, n_rows * 128 - n_small)).reshape(n_rows, 128)
    (part["w1_a"],), got = nn_matmul(da1, h1, "grad_w1_a",
                                     carried=StageGroup([exchange_stage(("w2_a",)), GatherStage([flat])]))
    received["w2_a"], small_all = got
    (part["w3_a"],), got = nn_matmul(db1, h1, "grad_w3_a", carried=exchange_stage(("w1_a",)))
    received["w1_a"] = got[0]
    (received["w3_a"],) = run_stage(exchange_stage(("w3_a",)), "exchange_w3_a")
    small_all = small_all.reshape(N_DEV, n_rows, 128)
    total = sum_slots(small_all, "sum_small_grads").reshape(-1)
    tot, off = {}, 0
    for key, like in small.items():
        n = math.prod(like.shape)
        tot[key], off = total[off:off + n].reshape(like.shape), off + n
    gmod_all = small_all.reshape(N_DEV, n_rows * 128)[:, :nb * N_MOD * d].reshape(N_DEV * nb, N_MOD * d)
    g_a_re, g_a_im, g_b_re, g_b_im, g_log_dt = disc_vjp((tot["abr"], tot["abi"], tot["bbr"], tot["bbi"]))

    grads = {}
    grads["b_ada"] = sum_rows(tot["gmod"].reshape(nb, N_MOD * d))
    grads["g_ffn1"], grads["g_mix"], grads["g_ffn2"] = tot["g_ffn1"], tot["g_mix"], tot["g_ffn2"]
    grads["g_final"] = tot["g_final"].reshape(d)
    grads["d_skip"] = tot["d_skip"]
    grads["a_re"], grads["a_im"], grads["log_dt"] = g_a_re[None], g_a_im[None], g_log_dt[None]
    grads["b_re"], grads["b_im"] = g_b_re[None], g_b_im[None]
    grads["c_re"], grads["c_im"] = tot["c_re"][None], tot["c_im"][None]
    grads["conv_w"] = lax.dynamic_slice(tot["conv_w"], (0, me * conv_w.shape[2]), (CONV_K, conv_w.shape[2]))[None]

    delta, new_m, new_v = {}, {}, {}
    for name in big:
        if received[name].shape[1:] == weights[name].shape[1:]:
            gsum, dl, mm, vn = sum_adamw_update(received[name], weights[name][0], mom1[name][0], mom2[name][0],
                                                "adamw_" + name)
        else:
            gsum = sum_slots(received[name], "sum_" + name)
            gsum = gsum.T if name in transposed else gsum.reshape(weights[name].shape[1:])
            dl, mm, vn = adamw_update(weights[name][0], gsum, mom1[name][0], mom2[name][0], "adamw_" + name)
        grads[name] = gsum[None]
        delta[name], new_m[name], new_v[name] = dl[None], mm[None], vn[None]

    gmod_cols = lax.dynamic_slice(gmod_all, (0, me * ada_cols), (N_DEV * nb, ada_cols))
    g_wada, d_wada, m_wada, v_wada = ada_backward_update(c_all, gmod_cols, w_ada[0], m_w_ada[0], v_w_ada[0])
    grads["w_ada"], delta["w_ada"], new_m["w_ada"], new_v["w_ada"] = g_wada[None], d_wada[None], m_wada[None], v_wada[None]

    small_names = [n for n in names if n not in big and n != "w_ada"]

    def as2d(a):
        return a.reshape(-1, a.shape[-1])

    sw_, sg_, sm_, sv_ = ([as2d(src[n]) for n in small_names] for src in (weights, grads, mom1, mom2))
    sd, smo, svo = adamw_update_small(sw_, sg_, sm_, sv_)
    for n, dl, mm, vn in zip(small_names, sd, smo, svo):
        shape = weights[n].shape
        grads[n] = grads[n].reshape(shape)
        delta[n], new_m[n], new_v[n] = dl.reshape(shape), mm.reshape(shape), vn.reshape(shape)

    grad_x = dx0.reshape(nb, s, d)
    return (loss, grad_x, *[grads[n] for n in names], *[delta[n] for n in names],
            *[new_m[n] for n in names], *[new_v[n] for n in names])


def sum_rows(a):
    r, c = a.shape

    def body(a_ref, o_ref):
        acc = a_ref[0:1, :]
        for j in range(1, r):
            acc = acc + a_ref[j:j + 1, :]
        o_ref[...] = acc

    return pl.pallas_call(body, name="sum_rows", out_shape=jax.ShapeDtypeStruct((1, c), F32),
                          compiler_params=_params())(a)
```

```python
import functools
import math

import jax
import jax.numpy as jnp
from jax import lax
from jax.experimental import pallas as pl
from jax.experimental.pallas import tpu as pltpu

F32 = jnp.float32
BF16 = jnp.bfloat16
N_DEV = 8
N_MOD = 9
EPS = 1e-6
CONV_K = 3
ADAM_LR = 0.001
ADAM_B1 = 0.9
ADAM_B2 = 0.999
ADAM_EPS = 1e-08
ADAM_WD = 0.01
ADAM_STEP = 10
GELU_C0 = math.sqrt(2.0 / math.pi)
GELU_C1 = 0.044715
V7X_VMEM_LIMIT = 56 * 1024 * 1024
MESH_ID = pl.DeviceIdType.MESH
NT = (((1,), (1,)), ((), ()))
TN = (((0,), (0,)), ((), ()))


def _dot(a, b, dims=None):
    if dims is None:
        return jnp.dot(a, b, preferred_element_type=F32)
    return lax.dot_general(a, b, dims, preferred_element_type=F32)


def _params(sem=None, vmem=V7X_VMEM_LIMIT):
    return pltpu.CompilerParams(dimension_semantics=sem, vmem_limit_bytes=vmem)


def _full(shape):
    return pl.BlockSpec(shape, lambda *_: (0,) * len(shape))


def _const(shape):
    return pl.BlockSpec(shape, lambda *_: (0,) * len(shape), pipeline_mode=pl.Buffered(1))


def _tile(n, want):
    t = min(n, want)
    while n % t:
        t //= 2
    return t


class GatherStage:
    COPIES = 9

    def __init__(self, shards):
        n = len(shards)
        self.inputs = list(shards)
        self.out_shape = [jax.ShapeDtypeStruct((N_DEV * s.shape[0], s.shape[1]), s.dtype) for s in shards]
        self.scratch = [pltpu.SemaphoreType.DMA((self.COPIES * n,)), pltpu.SemaphoreType.DMA((self.COPIES * n,)),
                        pltpu.SemaphoreType.DMA((n,))]

    def _plan(self, ins, outs, sems):
        send_sems, recv_sems, local_sems = sems
        n = len(ins)
        x, y, c = lax.axis_index("x"), lax.axis_index("y"), lax.axis_index("c")
        me, sibling, xn, yn, dg = (x, y, c), (x, y, 1 - c), (1 - x, y, c), (x, 1 - y, c), (1 - x, 1 - y, c)

        def rows(k, block, half=None):
            r = ins[k].shape[0]
            px, py, pc = block
            base = (4 * px + 2 * py + pc) * r
            if half is None:
                return outs[k].at[pl.ds(base, r), :]
            return outs[k].at[pl.ds(base + half * (r // 2), r // 2), :]

        def copy(k, j, block, to, half=None, src=None):
            return pltpu.make_async_remote_copy(
                src_ref=rows(k, block, half) if src is None else src, dst_ref=rows(k, block, half),
                send_sem=send_sems.at[self.COPIES * k + j], recv_sem=recv_sems.at[self.COPIES * k + j],
                device_id=to, device_id_type=MESH_ID)

        sib = lambda b: (b[0], b[1], 1 - b[2])
        mine = [pltpu.make_async_copy(ins[k], rows(k, me), local_sems.at[k]) for k in range(n)]
        first = [(0, me, sibling, None, sibling), (1, me, xn, None, xn), (2, me, yn, None, yn)]
        second = [(3, xn, yn, 0, dg), (4, yn, xn, 1, dg), (5, xn, sibling, None, sib(xn)), (6, yn, sibling, None, sib(yn))]
        third = [(7, dg, sibling, 0, sib(dg)), (8, dg, sibling, 1, sib(dg))]
        return n, me, copy, mine, first, second, third

    def start(self, ins, outs, sems):
        n, me, copy, mine, first, _, _ = self._plan(ins, outs, sems)
        for cp in mine:
            cp.start()
        for k in range(n):
            for j, block, to, half, _ in first:
                copy(k, j, block, to, half, src=ins[k]).start()

    def advance(self, ins, outs, sems):
        n, me, copy, mine, first, second, third = self._plan(ins, outs, sems)
        for k in range(n):
            copy(k, 1, first[1][4], me).wait_recv()
            copy(k, 2, first[2][4], me).wait_recv()
            for j, block, to, half, _ in second:
                copy(k, j, block, to, half).start()

    def advance_again(self, ins, outs, sems):
        n, me, copy, mine, first, second, third = self._plan(ins, outs, sems)
        for k in range(n):
            copy(k, 3, second[0][4], me, 0).wait_recv()
            copy(k, 4, second[1][4], me, 1).wait_recv()
            for j, block, to, half, _ in third:
                copy(k, j, block, to, half).start()

    def finish(self, ins, outs, sems):
        n, me, copy, mine, first, second, third = self._plan(ins, outs, sems)
        arrived = lambda k, j, block, half: copy(k, j, block, me, half).wait_recv()
        for k in range(n):
            arrived(k, 0, first[0][4], None)
            arrived(k, 5, second[2][4], None)
            arrived(k, 6, second[3][4], None)
            arrived(k, 7, third[0][4], 0)
            arrived(k, 8, third[1][4], 1)
        for k in range(n):
            for j, block, to, half, _ in first:
                copy(k, j, block, to, half, src=ins[k]).wait_send()
            for j, block, to, half, _ in second + third:
                copy(k, j, block, to, half).wait_send()
        for cp in mine:
            cp.wait()


class ExchangeStage:
    def __init__(self, bufs):
        n = len(bufs)
        self.inputs = list(bufs)
        self.out_shape = [jax.ShapeDtypeStruct(b.shape, b.dtype) for b in bufs]
        self.scratch = [pltpu.SemaphoreType.DMA((7 * n,)), pltpu.SemaphoreType.DMA((7 * n,)),
                        pltpu.SemaphoreType.DMA((n,))]

    def _plan(self, ins, outs, sems):
        send_sems, recv_sems, local_sems = sems
        n = len(ins)
        x, y, c = lax.axis_index("x"), lax.axis_index("y"), lax.axis_index("c")
        me = 4 * x + 2 * y + c
        mine = [pltpu.make_async_copy(ins[k].at[me], outs[k].at[me], local_sems.at[k]) for k in range(n)]
        copies = []
        for mask in range(1, N_DEV):
            px, py, pc = x ^ (mask >> 2), y ^ ((mask >> 1) & 1), c ^ (mask & 1)
            for k in range(n):
                copies.append(pltpu.make_async_remote_copy(
                    src_ref=ins[k].at[4 * px + 2 * py + pc], dst_ref=outs[k].at[me],
                    send_sem=send_sems.at[7 * k + mask - 1], recv_sem=recv_sems.at[7 * k + mask - 1],
                    device_id=(px, py, pc), device_id_type=MESH_ID))
        return mine, copies

    def start(self, ins, outs, sems):
        mine, copies = self._plan(ins, outs, sems)
        for cp in mine + copies:
            cp.start()

    def advance(self, ins, outs, sems):
        pass

    advance_again = advance

    def finish(self, ins, outs, sems):
        mine, copies = self._plan(ins, outs, sems)
        for cp in copies:
            cp.wait_recv()
        for cp in copies:
            cp.wait_send()
        for cp in mine:
            cp.wait()


ANY_SPEC = pl.BlockSpec(memory_space=pl.ANY)
GATHER_ROWS = 16


class StageGroup:
    def __init__(self, stages):
        self.stages = list(stages)
        self.inputs = [a for s in stages for a in s.inputs]
        self.out_shape = [o for s in stages for o in s.out_shape]
        self.scratch = [t for s in stages for t in s.scratch]

    def _parts(self, ins, outs, sems):
        i = o = t = 0
        for s in self.stages:
            ni, no, nt = len(s.inputs), len(s.out_shape), len(s.scratch)
            yield s, ins[i:i + ni], outs[o:o + no], sems[t:t + nt]
            i, o, t = i + ni, o + no, t + nt

    def start(self, ins, outs, sems):
        for s, i_, o_, t_ in self._parts(ins, outs, sems):
            s.start(i_, o_, t_)

    def advance(self, ins, outs, sems):
        for s, i_, o_, t_ in self._parts(ins, outs, sems):
            s.advance(i_, o_, t_)

    def advance_again(self, ins, outs, sems):
        for s, i_, o_, t_ in self._parts(ins, outs, sems):
            s.advance_again(i_, o_, t_)

    def finish(self, ins, outs, sems):
        for s, i_, o_, t_ in self._parts(ins, outs, sems):
            s.finish(i_, o_, t_)


def run_stage(stage, name):
    ci, co = len(stage.inputs), len(stage.out_shape)

    def body(*refs):
        ins, outs, sems = refs[:ci], refs[ci:ci + co], refs[ci + co:]
        stage.start(ins, outs, sems)
        stage.advance(ins, outs, sems)
        stage.advance_again(ins, outs, sems)
        stage.finish(ins, outs, sems)

    return pl.pallas_call(body, name=name, out_shape=stage.out_shape, in_specs=[ANY_SPEC] * ci,
                          out_specs=[ANY_SPEC] * co, scratch_shapes=stage.scratch)(*stage.inputs)


def _call(body, *, name, grid, in_specs, out_specs, out_shape, args, scratch_shapes=(), carried=None):
    sem = ("arbitrary",) * len(grid)
    if carried is None:
        return pl.pallas_call(body, name=name, grid=grid, in_specs=list(in_specs), out_specs=list(out_specs),
                              out_shape=list(out_shape), scratch_shapes=list(scratch_shapes),
                              compiler_params=_params(sem))(*args), None
    ni, no, ns = len(in_specs), len(out_shape), len(scratch_shapes)
    ci, co = len(carried.inputs), len(carried.out_shape)
    n_steps = math.prod(grid)

    def wrapped(*refs):
        ins, refs = refs[:ni], refs[ni:]
        cins, refs = refs[:ci], refs[ci:]
        outs, refs = refs[:no], refs[no:]
        couts, refs = refs[:co], refs[co:]
        scr, csems = refs[:ns], refs[ns:]
        step = functools.reduce(lambda acc, ig: acc * ig[1] + ig[0],
                                zip([pl.program_id(k) for k in range(len(grid))], grid), 0)

        @pl.when(step == 0)
        def _():
            carried.start(cins, couts, csems)

        @pl.when(step == n_steps // 2)
        def _():
            carried.advance(cins, couts, csems)

        @pl.when(step == (3 * n_steps) // 4)
        def _():
            carried.advance_again(cins, couts, csems)

        body(*ins, *outs, *scr)

        @pl.when(step == n_steps - 1)
        def _():
            carried.finish(cins, couts, csems)

    res = pl.pallas_call(
        wrapped, name=name, grid=grid, in_specs=list(in_specs) + [ANY_SPEC] * ci,
        out_specs=list(out_specs) + [ANY_SPEC] * co, out_shape=list(out_shape) + carried.out_shape,
        scratch_shapes=list(scratch_shapes) + carried.scratch, compiler_params=_params(sem),
    )(*args, *carried.inputs)
    return res[:no], res[no:]


def _norm_mod(x, g, shift, scale):
    r = lax.rsqrt(jnp.mean(x * x, axis=-1, keepdims=True) + EPS)
    n = x * r
    return (n * g) * (1.0 + scale) + shift, n, r


def _norm_mod_bwd(dh, n, r, g, scale):
    dsh = jnp.sum(dh, axis=0, keepdims=True)
    dsc = jnp.sum(dh * (n * g), axis=0, keepdims=True)
    dg = jnp.sum(dh * (1.0 + scale) * n, axis=0, keepdims=True)
    dn = dh * ((1.0 + scale) * g)
    dx = r * (dn - n * jnp.mean(n * dn, axis=-1, keepdims=True))
    return dx, dsh, dsc, dg


def _mod_rows(mod_ref, sub):
    m = mod_ref[0]
    return m[3 * sub:3 * sub + 1], m[3 * sub + 1:3 * sub + 2], m[3 * sub + 2:3 * sub + 3]


def _gelu(x):
    t = jnp.tanh(GELU_C0 * (x + GELU_C1 * x * x * x))
    return 0.5 * x * (1.0 + t), t


def _gelu_grad(x, t):
    return 0.5 * (1.0 + t) + 0.5 * x * (1.0 - t * t) * (GELU_C0 * (1.0 + 3.0 * GELU_C1 * x * x))


def _zero_when(cond, *refs):
    @pl.when(cond)
    def _():
        for r in refs:
            r[...] = jnp.zeros_like(r)


def ada_forward(c_all, w_ada, b_ada_cols):
    def body(c_ref, w_ref, b_ref, o_ref):
        c = c_ref[...]
        cond = (c * jax.nn.sigmoid(c)).astype(BF16)
        o_ref[...] = _dot(cond, w_ref[...].astype(BF16)) + b_ref[...]

    nb, d = c_all.shape
    cols = w_ada.shape[1]
    tn = _tile(cols, 384)
    return pl.pallas_call(
        body, name="ada_forward", grid=(cols // tn,),
        out_shape=jax.ShapeDtypeStruct((nb, cols), F32),
        in_specs=[_full((nb, d)), pl.BlockSpec((d, tn), lambda j: (0, j)), pl.BlockSpec((1, tn), lambda j: (0, j))],
        out_specs=pl.BlockSpec((nb, tn), lambda j: (0, j)),
        compiler_params=_params(("arbitrary",)),
    )(c_all, w_ada, b_ada_cols)


def _adamw(w, g, m, v):
    m = ADAM_B1 * m + (1.0 - ADAM_B1) * g
    v = ADAM_B2 * v + (1.0 - ADAM_B2) * (g * g)
    m_hat = m / (1.0 - ADAM_B1 ** ADAM_STEP)
    v_hat = v / (1.0 - ADAM_B2 ** ADAM_STEP)
    delta = -ADAM_LR * (m_hat / (jnp.sqrt(v_hat) + ADAM_EPS) + ADAM_WD * w)
    return delta, m, v


def ada_backward_update(c_all, gmod_cols, w, m, v):
    def body(c_ref, g_ref, w_ref, m_ref, v_ref, go_ref, d_ref, mo_ref, vo_ref):
        c = c_ref[...]
        cond = (c * jax.nn.sigmoid(c)).astype(BF16)
        g = _dot(cond, g_ref[...].astype(BF16), TN)
        go_ref[...] = g
        d_ref[...], mo_ref[...], vo_ref[...] = _adamw(w_ref[...], g, m_ref[...], v_ref[...])

    nb, d = c_all.shape
    cols = w.shape[1]
    tn = _tile(cols, 128)
    col = pl.BlockSpec((d, tn), lambda j: (0, j))
    return pl.pallas_call(
        body, name="ada_backward_update", grid=(cols // tn,),
        out_shape=[jax.ShapeDtypeStruct(w.shape, F32)] * 4,
        in_specs=[_full((nb, d)), pl.BlockSpec((nb, tn), lambda j: (0, j)), col, col, col],
        out_specs=[col] * 4,
        compiler_params=_params(("arbitrary",)),
    )(c_all, gmod_cols, w, m, v)


def _row_spec(tm, width, tiles_per_seq):
    return pl.BlockSpec((tm, width), lambda b, i: (b * tiles_per_seq + i, 0))


def _mod_spec(d):
    return pl.BlockSpec((1, N_MOD, d), lambda b, i: (b, 0, 0))


def _col_spec(rows, tm, tiles_per_seq):
    return pl.BlockSpec((rows, tm), lambda b, i: (0, b * tiles_per_seq + i))


def _ffn_chunk(f):
    return f // 2 if f % 256 == 0 and f > 1536 else f


def ffn_forward(x, mod, g, w1t, w3t, w2, sub, nb, name, carried=None):
    t, d = x.shape
    f = w1t.shape[0]
    s = t // nb
    tm = _tile(s, 512)
    fc = _ffn_chunk(f)

    def body(x_ref, mod_ref, g_ref, w1_ref, w3_ref, w2_ref, xo_ref, a_ref, b_ref, f_ref):
        xv = x_ref[...]
        sh, sc, gt = _mod_rows(mod_ref, sub)
        h, _, _ = _norm_mod(xv, g_ref[...], sh, sc)
        hb = h.astype(BF16)
        acc_t = jnp.zeros((d, tm), F32)
        for k in range(f // fc):
            rows = slice(k * fc, (k + 1) * fc)
            a = _dot(w1_ref[rows, :], hb, NT)
            b = _dot(w3_ref[rows, :], hb, NT)
            a_ref[rows, :] = a.astype(BF16)
            b_ref[rows, :] = b.astype(BF16)
            sw = (a * jax.nn.sigmoid(a)) * b
            acc_t = acc_t + _dot(w2_ref[rows, :], sw.astype(BF16), TN)
        acc = acc_t.T
        f_ref[...] = acc.astype(BF16)
        xo_ref[...] = xv + (0.5 * gt) * acc

    tps = s // tm
    rd, cf = _row_spec(tm, d, tps), _col_spec(f, tm, tps)
    return _call(
        body, name=name, grid=(nb, tps), carried=carried,
        out_shape=[jax.ShapeDtypeStruct((t, d), F32), jax.ShapeDtypeStruct((f, t), BF16),
                   jax.ShapeDtypeStruct((f, t), BF16), jax.ShapeDtypeStruct((t, d), BF16)],
        in_specs=[rd, _mod_spec(d), _const((1, d)), _const((f, d)), _const((f, d)), _const((f, d))],
        out_specs=[rd, cf, cf, rd],
        args=(x, mod, g, w1t, w3t, w2))


def ffn_backward_hidden(dxo, a_t, b_t, fo, mod, w2, sub, nb, name, carried=None):
    t, d = dxo.shape
    f = w2.shape[0]
    s = t // nb
    tm = _tile(s, 512)
    fc = _tile(f, 704) if f % 704 == 0 else _tile(f, 512)

    def body(dxo_ref, a_ref, b_ref, f_ref, mod_ref, w2_ref, da_ref, db_ref, s_ref, df_ref, dgt_ref):
        _zero_when(pl.program_id(1) == 0, dgt_ref)
        dxo = dxo_ref[...]
        _, _, gt = _mod_rows(mod_ref, sub)
        dfb = ((0.5 * gt) * dxo).astype(BF16)
        df_ref[...] = dfb
        dgt_ref[...] += 0.5 * jnp.sum(dxo * f_ref[...].astype(F32), axis=0, keepdims=True)[None]
        for k in range(f // fc):
            rows = slice(k * fc, (k + 1) * fc)
            ds = _dot(w2_ref[rows, :], dfb, NT).astype(BF16)
            av = a_ref[rows, :].astype(F32)
            bv = b_ref[rows, :]
            sig = jax.nn.sigmoid(av)
            sl = av * sig
            slb = sl.astype(BF16)
            da_ref[rows, :] = ds * bv * (sig + sl * (1.0 - sig)).astype(BF16)
            db_ref[rows, :] = ds * slb
            s_ref[rows, :] = slb * bv

    tps = s // tm
    rd, cf = _row_spec(tm, d, tps), _col_spec(f, tm, tps)
    return _call(
        body, name=name, grid=(nb, tps), carried=carried,
        out_shape=[jax.ShapeDtypeStruct((f, t), BF16)] * 3
        + [jax.ShapeDtypeStruct((t, d), BF16), jax.ShapeDtypeStruct((nb, 1, d), F32)],
        in_specs=[rd, cf, cf, rd, _mod_spec(d), _const((f, d))],
        out_specs=[cf, cf, cf, rd, pl.BlockSpec((1, 1, d), lambda b, i: (b, 0, 0))],
        args=(dxo, a_t, b_t, fo, mod, w2))


def ffn_backward_input(dxo, x, da_t, db_t, mod, g, w1t, w3t, sub, nb, name, carried=None):
    t, d = x.shape
    f = w1t.shape[0]
    s = t // nb
    tm = _tile(s, 512)

    def body(dxo_ref, x_ref, da_ref, db_ref, mod_ref, g_ref, w1_ref, w3_ref, dx_ref, h_ref, dmod_ref, dg_ref):
        bi, i = pl.program_id(0), pl.program_id(1)
        _zero_when(i == 0, dmod_ref)
        _zero_when(jnp.logical_and(bi == 0, i == 0), dg_ref)
        sh, sc, _ = _mod_rows(mod_ref, sub)
        gv = g_ref[...]
        h, n, r = _norm_mod(x_ref[...], gv, sh, sc)
        h_ref[...] = h.astype(BF16)
        dh_t = _dot(w1_ref[...], da_ref[...], TN) + _dot(w3_ref[...], db_ref[...], TN)
        dxn, dsh, dsc, dg = _norm_mod_bwd(dh_t.T, n, r, gv, sc)
        dx_ref[...] = dxo_ref[...] + dxn
        dmod_ref[...] += jnp.concatenate([dsh, dsc], axis=0)[None]
        dg_ref[...] += dg

    tps = s // tm
    rd, cf = _row_spec(tm, d, tps), _col_spec(f, tm, tps)
    return _call(
        body, name=name, grid=(nb, tps), carried=carried,
        out_shape=[jax.ShapeDtypeStruct((t, d), F32), jax.ShapeDtypeStruct((t, d), BF16),
                   jax.ShapeDtypeStruct((nb, 2, d), F32), jax.ShapeDtypeStruct((1, d), F32)],
        in_specs=[rd, rd, cf, cf, _mod_spec(d), _const((1, d)), _const((f, d)), _const((f, d))],
        out_specs=[rd, rd, pl.BlockSpec((1, 2, d), lambda b, i: (b, 0, 0)), _full((1, d))],
        args=(dxo, x, da_t, db_t, mod, g, w1t, w3t))


def nn_matmul(lhs_t, rhs, name, carried=None):
    m, t = lhs_t.shape
    n = rhs.shape[1]
    tk = _tile(t, 2048)
    tmm = m if m <= 1536 else m // 2
    nk = t // tk

    def body(a_ref, b_ref, o_ref, acc_ref):
        k = pl.program_id(1)
        _zero_when(k == 0, acc_ref)
        acc_ref[...] += _dot(a_ref[...], b_ref[...])

        @pl.when(k == nk - 1)
        def _():
            o_ref[...] = acc_ref[...].astype(BF16)

    return _call(
        body, name=name, grid=(m // tmm, nk), carried=carried,
        out_shape=[jax.ShapeDtypeStruct((m, n), BF16)],
        in_specs=[pl.BlockSpec((tmm, tk), lambda j, k: (j, k)), pl.BlockSpec((tk, n), lambda j, k: (k, 0))],
        out_specs=[pl.BlockSpec((tmm, n), lambda j, k: (j, 0))],
        scratch_shapes=[pltpu.VMEM((tmm, n), F32)],
        args=(lhs_t, rhs))


def tn_matmul(lhs, rhs, name):
    t, m = lhs.shape
    n = rhs.shape[1]
    tk = _tile(t, 2048)
    nk = t // tk

    def body(a_ref, b_ref, o_ref, acc_ref):
        k = pl.program_id(0)
        _zero_when(k == 0, acc_ref)
        acc_ref[...] += _dot(a_ref[...], b_ref[...], TN)

        @pl.when(k == nk - 1)
        def _():
            o_ref[...] = acc_ref[...].astype(BF16)

    return pl.pallas_call(
        body, name=name, grid=(nk,),
        out_shape=jax.ShapeDtypeStruct((m, n), BF16),
        in_specs=[pl.BlockSpec((tk, m), lambda k: (k, 0)), pl.BlockSpec((tk, n), lambda k: (k, 0))],
        out_specs=pl.BlockSpec((m, n), lambda k: (0, 0)),
        scratch_shapes=[pltpu.VMEM((m, n), F32)],
        compiler_params=_params(("arbitrary",)),
    )(lhs, rhs)


def mixer_proj_forward(x, mod, g, w_in_t, cw, sw, nb, carried=None):
    t, d = x.shape
    s = t // nb
    tm = _tile(s, 512)
    pieces = [(0, cw, "bf16"), (cw, cw, "bf16"), (2 * cw, cw, "bf16"), (3 * cw, sw, "f32"),
              (3 * cw + sw, d, "sig"), (3 * cw + sw + d, d, "sig")]

    def body(x_ref, mod_ref, g_ref, w_ref, *outs):
        h_ref = outs[-1]
        sh, sc, _ = _mod_rows(mod_ref, 1)
        h, _, _ = _norm_mod(x_ref[...], g_ref[...], sh, sc)
        hb = h.astype(BF16)
        h_ref[...] = hb
        for (off, width, kind), o_ref in zip(pieces, outs[:-1]):
            ck = _tile(width, 512)
            for j in range(width // ck):
                p = _dot(hb, w_ref[off + j * ck:off + (j + 1) * ck, :], NT)
                if kind == "sig":
                    p = jax.nn.sigmoid(p)
                o_ref[:, j * ck:(j + 1) * ck] = p.astype(o_ref.dtype)

    tps = s // tm
    widths = [(cw, BF16), (cw, BF16), (cw, BF16), (sw, F32), (d, BF16), (d, BF16), (d, BF16)]
    return _call(
        body, name="mixer_proj_forward", grid=(nb, tps), carried=carried,
        out_shape=[jax.ShapeDtypeStruct((t, w), dt) for w, dt in widths],
        in_specs=[_row_spec(tm, d, tps), _mod_spec(d), _const((1, d)), _const(w_in_t.shape)],
        out_specs=[_row_spec(tm, w, tps) for w, _ in widths],
        args=(x, mod, g, w_in_t))


def mixer_proj_backward(dgb, dgc, dv, du, dgla, dglb, dxo, x, mod, g, w_in_t, nb, carried=None):
    t, d = x.shape
    s = t // nb
    tm = _tile(s, 512)
    parts = [dgb, dgc, dv, du, dgla, dglb]
    offs = [0]
    for p in parts:
        offs.append(offs[-1] + p.shape[1])

    def body(*refs):
        p_refs = refs[:6]
        dxo_ref, x_ref, mod_ref, g_ref, w_ref, dx_ref, dmod_ref, dg_ref = refs[6:]
        bi, i = pl.program_id(0), pl.program_id(1)
        _zero_when(i == 0, dmod_ref)
        _zero_when(jnp.logical_and(bi == 0, i == 0), dg_ref)
        dh = jnp.zeros((tm, d), F32)
        for p_ref, off in zip(p_refs, offs):
            width = p_ref.shape[1]
            ck = _tile(width, 512)
            for j in range(width // ck):
                dh = dh + _dot(p_ref[:, j * ck:(j + 1) * ck], w_ref[off + j * ck:off + (j + 1) * ck, :])
        sh, sc, _ = _mod_rows(mod_ref, 1)
        gv = g_ref[...]
        _, n, r = _norm_mod(x_ref[...], gv, sh, sc)
        dxn, dsh, dsc, dg = _norm_mod_bwd(dh, n, r, gv, sc)
        dx_ref[...] = dxo_ref[...] + dxn
        dmod_ref[...] += jnp.concatenate([dsh, dsc], axis=0)[None]
        dg_ref[...] += dg

    tps = s // tm
    rd = _row_spec(tm, d, tps)
    return _call(
        body, name="mixer_proj_backward", grid=(nb, tps), carried=carried,
        out_shape=[jax.ShapeDtypeStruct((t, d), F32), jax.ShapeDtypeStruct((nb, 2, d), F32),
                   jax.ShapeDtypeStruct((1, d), F32)],
        in_specs=[_row_spec(tm, p.shape[1], tps) for p in parts]
        + [rd, rd, _mod_spec(d), _const((1, d)), _const(w_in_t.shape)],
        out_specs=[rd, pl.BlockSpec((1, 2, d), lambda b, i: (b, 0, 0)), _full((1, d))],
        args=(*parts, dxo, x, mod, g, w_in_t))


GROUPS_PER_BLOCK = 8
ROWS = 8
SCAN_LANES = 512


def _scan_rows(xr, xi, masks, shifts):
    for (mr, mi), sft in zip(masks, shifts):
        sr, si = pltpu.roll(xr, sft, 0), pltpu.roll(xi, sft, 0)
        xr, xi = xr + mr * sr - mi * si, xi + mr * si + mi * sr
    return xr, xi


def _cmul_add(ar, ai, cr, ci, br, bi):
    return ar * cr - ai * ci + br, ar * ci + ai * cr + bi


def _segment_rows(perm_ref, x):
    return _dot(perm_ref[0], x).astype(BF16)


def _time_rows(perm_ref, x):
    hi = x.astype(BF16)
    lo = (x - hi.astype(F32)).astype(BF16)
    return _dot(perm_ref[1], hi) + _dot(perm_ref[1], lo)


def segment_permutation(tc):
    r = jnp.arange(tc)
    p = (r[:, None] % ROWS * (tc // ROWS) + r[:, None] // ROWS == r[None, :]).astype(BF16)
    return jnp.stack([p, p.T])


def _rows_at(k, offset=0):
    return pl.ds(pl.multiple_of(k * ROWS + offset, ROWS), ROWS)


def ssm_forward(u, bd, cd, a1, ml, nb):
    t, w = u.shape
    s = t // nb
    tc = _tile(s, 256)
    seg = tc // ROWS
    nq, ub, lq = bd.shape[1], bd.shape[2], bd.shape[3]
    nl = nq * lq
    nch = s // tc
    lw = min(nl, SCAN_LANES)

    def body(u_ref, perm_ref, bd_ref, cd_ref, a1_ref, ml_ref, y_ref, st_ref, xr_s, xi_s, car_s):
        i = pl.program_id(1)

        @pl.when(i == 0)
        def _():
            car_s[...] = jnp.zeros_like(car_s)

        ub16 = _segment_rows(perm_ref, u_ref[...].astype(BF16))
        for q in range(nq):
            lanes = slice(q * lq, (q + 1) * lq)
            uq = ub16[:, q * ub:(q + 1) * ub]
            xr_s[:, lanes] = _dot(uq, bd_ref[0, q])
            xi_s[:, lanes] = _dot(uq, bd_ref[1, q])
        row_is_0 = lax.broadcasted_iota(jnp.int32, (ROWS, lw), 0) == 0
        zero = jnp.zeros((ROWS, lw), F32)
        for j in range(nl // lw):
            lanes = slice(j * lw, (j + 1) * lw)
            ar, ai = a1_ref[0, :, lanes], a1_ref[1, :, lanes]

            def local(k, c):
                return _cmul_add(ar, ai, c[0], c[1], xr_s[_rows_at(k), lanes], xi_s[_rows_at(k), lanes])

            er, ei = lax.fori_loop(0, seg, local, (zero, zero))
            masks = [(ml_ref[d, 0, :, lanes], ml_ref[d, 1, :, lanes]) for d in range(3)]
            cr, ci = _scan_rows(jnp.where(row_is_0, car_s[0, :, lanes], pltpu.roll(er, 1, 0)),
                                jnp.where(row_is_0, car_s[1, :, lanes], pltpu.roll(ei, 1, 0)), masks, (1, 2, 4))
            st_ref[0, 0, :, lanes] = cr
            st_ref[0, 1, :, lanes] = ci

            def full(k, c):
                xr, xi = _cmul_add(ar, ai, c[0], c[1], xr_s[_rows_at(k), lanes], xi_s[_rows_at(k), lanes])
                xr_s[_rows_at(k), lanes] = xr
                xi_s[_rows_at(k), lanes] = xi
                return xr, xi

            fr, fi = lax.fori_loop(0, seg, full, (cr, ci))
            car_s[0, :, lanes] = jnp.broadcast_to(fr[ROWS - 1:ROWS], fr.shape)
            car_s[1, :, lanes] = jnp.broadcast_to(fi[ROWS - 1:ROWS], fi.shape)
        y = jnp.concatenate([_dot(xr_s[:, q * lq:(q + 1) * lq].astype(BF16), cd_ref[0, q])
                             + _dot(xi_s[:, q * lq:(q + 1) * lq].astype(BF16), cd_ref[1, q]) for q in range(nq)],
                            axis=1)
        y_ref[...] = _time_rows(perm_ref, y)

    perm = segment_permutation(tc)
    return pl.pallas_call(
        body, name="ssm_forward", grid=(nb, nch),
        out_shape=[jax.ShapeDtypeStruct((t, w), F32), jax.ShapeDtypeStruct((nb * nch, 2, ROWS, nl), F32)],
        in_specs=[pl.BlockSpec((tc, w), lambda b, i: (b * nch + i, 0)), _const(perm.shape), _const(bd.shape),
                  _const(cd.shape), _const(a1.shape), _const(ml.shape)],
        out_specs=[pl.BlockSpec((tc, w), lambda b, i: (b * nch + i, 0)),
                   pl.BlockSpec((1, 2, ROWS, nl), lambda b, i: (b * nch + i, 0, 0, 0))],
        scratch_shapes=[pltpu.VMEM((tc, nl), F32), pltpu.VMEM((tc, nl), F32), pltpu.VMEM((2, ROWS, nl), F32)],
        compiler_params=_params(("arbitrary", "arbitrary")),
    )(u, perm, bd, cd, a1, ml)


def ssm_backward(u, dy, st, bd, cd, a1, mlb, dskip, nb, carried=None):
    t, w = u.shape
    s = t // nb
    tc = _tile(s, 256)
    seg = tc // ROWS
    nq, ub, lq = bd.shape[1], bd.shape[2], bd.shape[3]
    nl = nq * lq
    nch = s // tc
    lw = min(nl, SCAN_LANES)

    def body(u_ref, dy_ref, st_ref, perm_ref, bd_ref, cd_ref, a1_ref, mlb_ref, dsk_ref,
             du_ref, dab_ref, dbd_ref, dcd_ref, xr_s, xi_s, lr_s, li_s, car_s):
        bi, i = pl.program_id(0), pl.program_id(1)
        first = jnp.logical_and(bi == 0, i == 0)

        @pl.when(i == 0)
        def _():
            car_s[...] = jnp.zeros_like(car_s)

        @pl.when(first)
        def _():
            dab_ref[...] = jnp.zeros_like(dab_ref)
            dbd_ref[...] = jnp.zeros_like(dbd_ref)
            dcd_ref[...] = jnp.zeros_like(dcd_ref)

        ub16 = _segment_rows(perm_ref, u_ref[...].astype(BF16))
        dyb16 = _segment_rows(perm_ref, dy_ref[...].astype(BF16))
        xr_s[0:ROWS, :] = st_ref[0, 0]
        xi_s[0:ROWS, :] = st_ref[0, 1]
        for q in range(nq):
            lanes = slice(q * lq, (q + 1) * lq)
            uq = ub16[:, q * ub:(q + 1) * ub]
            dq = dyb16[:, q * ub:(q + 1) * ub]
            xr_s[ROWS:, lanes] = _dot(uq, bd_ref[0, q])
            xi_s[ROWS:, lanes] = _dot(uq, bd_ref[1, q])
            lr_s[:, lanes] = _dot(dq, cd_ref[0, q], NT)
            li_s[:, lanes] = _dot(dq, cd_ref[1, q], NT)
        row_is_7 = lax.broadcasted_iota(jnp.int32, (ROWS, lw), 0) == ROWS - 1
        zero = jnp.zeros((ROWS, lw), F32)
        for j in range(nl // lw):
            lanes = slice(j * lw, (j + 1) * lw)
            ar, ai = a1_ref[0, :, lanes], a1_ref[1, :, lanes]
            nai = -ai

            def states(k, c):
                xr, xi = _cmul_add(ar, ai, c[0], c[1], xr_s[_rows_at(k, ROWS), lanes], xi_s[_rows_at(k, ROWS), lanes])
                xr_s[_rows_at(k, ROWS), lanes] = xr
                xi_s[_rows_at(k, ROWS), lanes] = xi
                return xr, xi

            lax.fori_loop(0, seg, states, (st_ref[0, 0, :, lanes], st_ref[0, 1, :, lanes]))

            def local(kk, c):
                k = seg - 1 - kk
                return _cmul_add(ar, nai, c[0], c[1], lr_s[_rows_at(k), lanes], li_s[_rows_at(k), lanes])

            er, ei = lax.fori_loop(0, seg, local, (zero, zero))
            masks = [(mlb_ref[d, 0, :, lanes], mlb_ref[d, 1, :, lanes]) for d in range(3)]
            cr, ci = _scan_rows(jnp.where(row_is_7, car_s[0, :, lanes], pltpu.roll(er, ROWS - 1, 0)),
                                jnp.where(row_is_7, car_s[1, :, lanes], pltpu.roll(ei, ROWS - 1, 0)), masks, (7, 6, 4))

            def full(kk, c):
                cr_, ci_, accr, acci = c
                k = seg - 1 - kk
                lr, li = _cmul_add(ar, nai, cr_, ci_, lr_s[_rows_at(k), lanes], li_s[_rows_at(k), lanes])
                lr_s[_rows_at(k), lanes] = lr
                li_s[_rows_at(k), lanes] = li
                xpr, xpi = xr_s[_rows_at(k), lanes], xi_s[_rows_at(k), lanes]
                return lr, li, accr + lr * xpr + li * xpi, acci + li * xpr - lr * xpi

            lr0, li0, accr, acci = lax.fori_loop(0, seg, full, (cr, ci, zero, zero))
            car_s[0, :, lanes] = jnp.broadcast_to(lr0[0:1], lr0.shape)
            car_s[1, :, lanes] = jnp.broadcast_to(li0[0:1], li0.shape)
            dab_ref[0, :, lanes] += accr
            dab_ref[1, :, lanes] += acci
        du_parts = []
        for q in range(nq):
            lanes = slice(q * lq, (q + 1) * lq)
            cols = slice(q * ub, (q + 1) * ub)
            lrb, lib = lr_s[:, lanes].astype(BF16), li_s[:, lanes].astype(BF16)
            uq, dq = ub16[:, cols], dyb16[:, cols]
            du_parts.append(_dot(lrb, bd_ref[0, q], NT) + _dot(lib, bd_ref[1, q], NT))
            dbd_ref[0, q] += _dot(uq, lrb, TN)
            dbd_ref[1, q] += _dot(uq, lib, TN)
            dcd_ref[0, q] += _dot(xr_s[ROWS:, lanes].astype(BF16), dq, TN)
            dcd_ref[1, q] += _dot(xi_s[ROWS:, lanes].astype(BF16), dq, TN)
        du = _time_rows(perm_ref, jnp.concatenate(du_parts, axis=1)) + dsk_ref[...] * dy_ref[...]
        du_ref[...] = du.astype(BF16)

    rev = lambda b, i: (b * nch + nch - 1 - i, 0)
    perm = segment_permutation(tc)
    return _call(
        body, name="ssm_backward", grid=(nb, nch), carried=carried,
        out_shape=[jax.ShapeDtypeStruct((t, w), BF16), jax.ShapeDtypeStruct((2, ROWS, nl), F32),
                   jax.ShapeDtypeStruct(bd.shape, F32), jax.ShapeDtypeStruct(cd.shape, F32)],
        in_specs=[pl.BlockSpec((tc, w), rev), pl.BlockSpec((tc, w), rev),
                  pl.BlockSpec((1, 2, ROWS, nl), lambda b, i: (b * nch + nch - 1 - i, 0, 0, 0)),
                  _const(perm.shape), _const(bd.shape), _const(cd.shape), _const(a1.shape), _const(mlb.shape),
                  _const((1, w))],
        out_specs=[pl.BlockSpec((tc, w), rev), _full((2, ROWS, nl)), _full(bd.shape), _full(cd.shape)],
        scratch_shapes=[pltpu.VMEM((tc + ROWS, nl), F32), pltpu.VMEM((tc + ROWS, nl), F32),
                        pltpu.VMEM((tc, nl), F32), pltpu.VMEM((tc, nl), F32), pltpu.VMEM((2, ROWS, nl), F32)],
        args=(u, dy, st, perm, bd, cd, a1, mlb, dskip))


def ssm_discretise(a_re, a_im, b_re, b_im, log_dt):
    dt = jnp.exp(log_dt)[:, None]
    er = jnp.exp(a_re * dt)
    abr, abi = er * jnp.cos(a_im * dt), er * jnp.sin(a_im * dt)
    den = a_re * a_re + a_im * a_im
    nr, ni = abr - 1.0, abi
    fr = ((nr * a_re + ni * a_im) / den)[..., None]
    fi = ((ni * a_re - nr * a_im) / den)[..., None]
    return abr, abi, fr * b_re - fi * b_im, fr * b_im + fi * b_re


def _complex_square(zr, zi):
    return zr * zr - zi * zi, 2.0 * zr * zi


def ssm_tables(abr, abi, bbr, bbi, c_re, c_im, seg):
    g, p, h = bbr.shape
    nq = g // GROUPS_PER_BLOCK
    zr, zi = abr.reshape(1, -1), abi.reshape(1, -1)
    a1 = jnp.stack([jnp.broadcast_to(zr, (ROWS, g * p)), jnp.broadcast_to(zi, (ROWS, g * p))])
    for _ in range(seg.bit_length() - 1):
        zr, zi = _complex_square(zr, zi)
    row = jnp.arange(ROWS)[:, None]
    ml, mlb = [], []
    for d in (1, 2, 4):
        ml.append(jnp.stack([jnp.where(row >= d, zr, 0.0), jnp.where(row >= d, zi, 0.0)]))
        mlb.append(jnp.stack([jnp.where(row + d < ROWS, zr, 0.0), jnp.where(row + d < ROWS, -zi, 0.0)]))
        zr, zi = _complex_square(zr, zi)
    eye = jnp.eye(GROUPS_PER_BLOCK, dtype=F32)

    def block_diag_in(bb):
        bq = bb.reshape(nq, GROUPS_PER_BLOCK, p, h)
        return jnp.einsum("qaph,ab->qahbp", bq, eye).reshape(nq, GROUPS_PER_BLOCK * h, GROUPS_PER_BLOCK * p)

    def block_diag_out(cc):
        cq = cc.reshape(nq, GROUPS_PER_BLOCK, h, p)
        return jnp.einsum("qahp,ab->qapbh", cq, eye).reshape(nq, GROUPS_PER_BLOCK * p, GROUPS_PER_BLOCK * h)

    bd = jnp.stack([block_diag_in(bbr), block_diag_in(bbi)]).astype(BF16)
    cd = jnp.stack([block_diag_out(c_re), block_diag_out(-c_im)]).astype(BF16)
    return bd, cd, a1, jnp.stack(ml), jnp.stack(mlb)


def ssm_table_grads(dab, dbd, dcd, g, p, h):
    nq = g // GROUPS_PER_BLOCK
    dabr, dabi = dab[0].sum(0).reshape(g, p), dab[1].sum(0).reshape(g, p)
    b5 = dbd.reshape(2, nq, GROUPS_PER_BLOCK, h, GROUPS_PER_BLOCK, p)
    dbb = jnp.einsum("rqahap->rqaph", b5).reshape(2, g, p, h)
    c5 = dcd.reshape(2, nq, GROUPS_PER_BLOCK, p, GROUPS_PER_BLOCK, h)
    dcc = jnp.einsum("rqapah->rqahp", c5).reshape(2, g, h, p)
    return dabr, dabi, dbb[0], dbb[1], dcc[0], -dcc[1]


HALO = 16


def _conv_inputs(gc_ref, v_ref, gch_ref, vh_ref, cv_s, i, tm):
    cv = gc_ref[...].astype(F32) * v_ref[...].astype(F32)
    halo = gch_ref[...].astype(F32) * vh_ref[...].astype(F32)
    cv_s[0:HALO, :] = jnp.where(i == 0, 0.0, halo)
    cv_s[HALO:, :] = cv
    return cv, cv_s[HALO - 1:HALO - 1 + tm, :], cv_s[HALO - 2:HALO - 2 + tm, :]


def _halo_spec(tm, width, tiles_per_seq):
    per = tm // HALO
    return pl.BlockSpec((HALO, width), lambda b, i: (jnp.maximum((b * tiles_per_seq + i) * per - 1, 0), 0))


def mixer_merge_forward(x, gb, gc, v, sga, sgb, yssm, u, mod, conv_w, dskip, wco, wglu, wso_t, wout, nb):
    t, d = x.shape
    cw, sw = gb.shape[1], u.shape[1]
    s = t // nb
    tm = _tile(s, 256)
    tps = s // tm

    def body(x_ref, gb_ref, gc_ref, v_ref, gch_ref, vh_ref, sga_ref, sgb_ref, ys_ref, u_ref, mod_ref, cw_ref,
             dsk_ref, wco_ref, wglu_ref, wso_ref, wout_ref, xo_ref, ya_ref, yb_ref, mix_ref, cv_s):
        i = pl.program_id(1)
        cv, cv1, cv2 = _conv_inputs(gc_ref, v_ref, gch_ref, vh_ref, cv_s, i, tm)
        w = cw_ref[...]
        conv = w[0:1] * cv2 + w[1:2] * cv1 + w[2:3] * cv
        ya = _dot((gb_ref[...].astype(F32) * conv).astype(BF16), wco_ref[...])
        s0 = ys_ref[...] + dsk_ref[...] * u_ref[...]
        s1, _ = _gelu(s0)
        z = _dot(s1.astype(BF16), wglu_ref[...])
        s2 = s1 * jax.nn.sigmoid(z)
        yb = _dot(s2.astype(BF16), wso_ref[...], NT)
        merged = sga_ref[...].astype(F32) * ya + sgb_ref[...].astype(F32) * yb
        mix = _dot(merged.astype(BF16), wout_ref[...])
        _, _, gt = _mod_rows(mod_ref, 1)
        xo_ref[...] = x_ref[...] + gt * mix
        ya_ref[...] = ya.astype(BF16)
        yb_ref[...] = yb.astype(BF16)
        mix_ref[...] = mix.astype(BF16)

    rd, rc, rw = _row_spec(tm, d, tps), _row_spec(tm, cw, tps), _row_spec(tm, sw, tps)
    hc = _halo_spec(tm, cw, tps)
    return pl.pallas_call(
        body, name="mixer_merge_forward", grid=(nb, tps),
        out_shape=[jax.ShapeDtypeStruct((t, d), F32)] + [jax.ShapeDtypeStruct((t, d), BF16)] * 3,
        in_specs=[rd, rc, rc, rc, hc, hc, rd, rd, rw, rw, _mod_spec(d), _const(conv_w.shape), _const((1, sw)),
                  _const(wco.shape), _const(wglu.shape), _const(wso_t.shape), _const(wout.shape)],
        out_specs=[rd, rd, rd, rd],
        scratch_shapes=[pltpu.VMEM((tm + HALO, cw), F32)],
        compiler_params=_params(("arbitrary", "arbitrary")),
    )(x, gb, gc, v, gc, v, sga, sgb, yssm, u, mod, conv_w, dskip, wco, wglu, wso_t, wout)


def mixer_merge_backward(dxo, mix, ya, yb, gb, gc, v, sga, sgb, yssm, u, mod, conv_w, dskip,
                         wco, wglu, wso_t, wout, nb, carried=None):
    t, d = dxo.shape
    cw, sw = gb.shape[1], u.shape[1]
    s = t // nb
    tm = _tile(s, 256)
    tps = s // tm

    def body(dxo_ref, mix_ref, ya_ref, yb_ref, gb_ref, gc_ref, v_ref, gch_ref, vh_ref, sga_ref, sgb_ref, ys_ref,
             u_ref, mod_ref, cw_ref, dsk_ref, wco_ref, wglu_ref, wso_ref, wout_ref,
             dgla_ref, dglb_ref, dgb_ref, dconv_ref, ds0_ref, dgt_ref, ddsk_ref, dwout_ref, dwco_ref, dwso_ref,
             dwglu_ref, cv_s):
        bi, i = pl.program_id(0), pl.program_id(1)
        _zero_when(i == 0, dgt_ref)
        _zero_when(jnp.logical_and(bi == 0, i == 0), ddsk_ref, dwout_ref, dwco_ref, dwso_ref, dwglu_ref)
        dxo = dxo_ref[...]
        _, _, gt = _mod_rows(mod_ref, 1)
        dmix = (gt * dxo).astype(BF16)
        dgt_ref[...] += jnp.sum(dxo * mix_ref[...].astype(F32), axis=0, keepdims=True)[None]
        ya, yb = ya_ref[...].astype(F32), yb_ref[...].astype(F32)
        sga, sgb = sga_ref[...].astype(F32), sgb_ref[...].astype(F32)
        merged = (sga * ya + sgb * yb).astype(BF16)
        dwout_ref[...] += _dot(merged, dmix, TN)
        dmerged = _dot(dmix, wout_ref[...], NT)
        dgla_ref[...] = (dmerged * ya * sga * (1.0 - sga)).astype(BF16)
        dglb_ref[...] = (dmerged * yb * sgb * (1.0 - sgb)).astype(BF16)
        dya = (dmerged * sga).astype(BF16)
        dyb = (dmerged * sgb).astype(BF16)
        cv, cv1, cv2 = _conv_inputs(gc_ref, v_ref, gch_ref, vh_ref, cv_s, i, tm)
        w = cw_ref[...]
        conv = w[0:1] * cv2 + w[1:2] * cv1 + w[2:3] * cv
        gbv = gb_ref[...].astype(F32)
        dwco_ref[...] += _dot((gbv * conv).astype(BF16), dya, TN)
        dya_in = _dot(dya, wco_ref[...], NT)
        dgb_ref[...] = (dya_in * conv).astype(BF16)
        dconv_ref[...] = dya_in * gbv
        uv = u_ref[...]
        s0 = ys_ref[...] + dsk_ref[...] * uv
        s1, th = _gelu(s0)
        s1b = s1.astype(BF16)
        sz = jax.nn.sigmoid(_dot(s1b, wglu_ref[...]))
        s2b = (s1 * sz).astype(BF16)
        dwso_ref[...] += _dot(dyb, s2b, TN)
        ds2 = _dot(dyb, wso_ref[...])
        dz = (ds2 * s1 * sz * (1.0 - sz)).astype(BF16)
        dwglu_ref[...] += _dot(s1b, dz, TN)
        ds1 = ds2 * sz + _dot(dz, wglu_ref[...], NT)
        ds0 = ds1 * _gelu_grad(s0, th)
        ds0_ref[...] = ds0
        ddsk_ref[...] += jnp.sum(ds0 * uv, axis=0, keepdims=True)

    rd, rc, rw = _row_spec(tm, d, tps), _row_spec(tm, cw, tps), _row_spec(tm, sw, tps)
    hc = _halo_spec(tm, cw, tps)
    return _call(
        body, name="mixer_merge_backward", grid=(nb, tps), carried=carried,
        out_shape=[jax.ShapeDtypeStruct((t, d), BF16), jax.ShapeDtypeStruct((t, d), BF16),
                   jax.ShapeDtypeStruct((t, cw), BF16), jax.ShapeDtypeStruct((t, cw), F32),
                   jax.ShapeDtypeStruct((t, sw), F32), jax.ShapeDtypeStruct((nb, 1, d), F32),
                   jax.ShapeDtypeStruct((1, sw), F32), jax.ShapeDtypeStruct(wout.shape, F32),
                   jax.ShapeDtypeStruct(wco.shape, F32), jax.ShapeDtypeStruct(wso_t.shape, F32),
                   jax.ShapeDtypeStruct(wglu.shape, F32)],
        in_specs=[rd, rd, rd, rd, rc, rc, rc, hc, hc, rd, rd, rw, rw, _mod_spec(d), _const(conv_w.shape),
                  _const((1, sw)), _const(wco.shape), _const(wglu.shape), _const(wso_t.shape), _const(wout.shape)],
        out_specs=[rd, rd, rc, rc, rw, pl.BlockSpec((1, 1, d), lambda b, i: (b, 0, 0)), _full((1, sw)),
                   _full(wout.shape), _full(wco.shape), _full(wso_t.shape), _full(wglu.shape)],
        scratch_shapes=[pltpu.VMEM((tm + HALO, cw), F32)],
        args=(dxo, mix, ya, yb, gb, gc, v, gc, v, sga, sgb, yssm, u, mod, conv_w, dskip, wco, wglu, wso_t, wout))


def conv_backward(dconv, gc, v, conv_w, nb):
    t, cw = dconv.shape
    s = t // nb
    tm = _tile(s, 512)
    tps = s // tm
    per = tm // ROWS
    slab = _tile(tm, 32)

    def body(dc_ref, dcn_ref, gc_ref, v_ref, gch_ref, vh_ref, cw_ref, dgc_ref, dv_ref, dw_ref, cv_s, dc_s):
        bi, i = pl.program_id(0), pl.program_id(1)
        _zero_when(jnp.logical_and(bi == 0, i == 0), dw_ref)
        cv_s[0:HALO, :] = jnp.where(i == 0, 0.0, gch_ref[...].astype(F32) * vh_ref[...].astype(F32))
        dc_s[tm:, :] = jnp.where(i == tps - 1, 0.0, dcn_ref[...])
        for r in range(0, tm, slab):
            cv_s[HALO + r:HALO + r + slab, :] = (gc_ref[r:r + slab, :].astype(F32)
                                                 * v_ref[r:r + slab, :].astype(F32))
            dc_s[r:r + slab, :] = dc_ref[r:r + slab, :]
        w = cw_ref[...]
        sums = [jnp.zeros((ROWS, cw), F32)] * CONV_K
        for r in range(0, tm, slab):
            dc = dc_s[r:r + slab, :]
            dcv = w[2:3] * dc + w[1:2] * dc_s[r + 1:r + 1 + slab, :] + w[0:1] * dc_s[r + 2:r + 2 + slab, :]
            dgc_ref[r:r + slab, :] = (dcv * v_ref[r:r + slab, :].astype(F32)).astype(BF16)
            dv_ref[r:r + slab, :] = (dcv * gc_ref[r:r + slab, :].astype(F32)).astype(BF16)
            for k in range(CONV_K):
                lag = HALO + r - (CONV_K - 1 - k)
                prod = dc * cv_s[lag:lag + slab, :]
                sums[k] = sums[k] + jnp.sum(prod.reshape(slab // ROWS, ROWS, cw), axis=0)
        dw_ref[...] += jnp.concatenate([jnp.sum(a, axis=0, keepdims=True) for a in sums], axis=0)

    rc = _row_spec(tm, cw, tps)
    nxt = pl.BlockSpec((ROWS, cw), lambda b, i: (jnp.minimum((b * tps + i + 1) * per, nb * tps * per - 1), 0))
    hc = _halo_spec(tm, cw, tps)
    return pl.pallas_call(
        body, name="conv_backward", grid=(nb, tps),
        out_shape=[jax.ShapeDtypeStruct((t, cw), BF16), jax.ShapeDtypeStruct((t, cw), BF16),
                   jax.ShapeDtypeStruct(conv_w.shape, F32)],
        in_specs=[rc, nxt, rc, rc, hc, hc, _full(conv_w.shape)],
        out_specs=[rc, rc, _full(conv_w.shape)],
        scratch_shapes=[pltpu.VMEM((tm + HALO, cw), F32), pltpu.VMEM((tm + ROWS, cw), F32)],
        compiler_params=_params(("arbitrary", "arbitrary")),
    )(dconv, dconv, gc, v, gc, v, conv_w)


def loss_forward_backward(x, target, g):
    t, d = x.shape
    tm = _tile(t, 512)

    def body(x_ref, t_ref, g_ref, l_ref, dx_ref, dg_ref):
        _zero_when(pl.program_id(0) == 0, dg_ref)
        xv = x_ref[...]
        gv = g_ref[...]
        r = lax.rsqrt(jnp.mean(xv * xv, axis=-1, keepdims=True) + EPS)
        n = xv * r
        err = n * gv - t_ref[...]
        l_ref[...] = jnp.full(l_ref.shape, 0.5 * jnp.sum(jnp.mean(err * err, axis=-1)), F32)
        dy = err * (1.0 / d)
        dn = dy * gv
        dx_ref[...] = r * (dn - n * jnp.mean(n * dn, axis=-1, keepdims=True))
        dg_ref[...] += jnp.sum(dy * n, axis=0, keepdims=True)

    row = pl.BlockSpec((tm, d), lambda i: (i, 0))
    return pl.pallas_call(
        body, name="loss_forward_backward", grid=(t // tm,),
        out_shape=[jax.ShapeDtypeStruct((t // tm, 1, 128), F32), jax.ShapeDtypeStruct((t, d), F32),
                   jax.ShapeDtypeStruct((1, d), F32)],
        in_specs=[row, row, _full((1, d))],
        out_specs=[pl.BlockSpec((1, 1, 128), lambda i: (i, 0, 0)), row, _full((1, d))],
        compiler_params=_params(("arbitrary",)),
    )(x, target, g)


def sum_slots(slots, name):
    _, r, c = slots.shape
    tr = _tile(r, 352) if r % 352 == 0 else _tile(r, 256)

    def body(s_ref, o_ref):
        acc = s_ref[0].astype(F32)
        for j in range(1, N_DEV):
            acc = acc + s_ref[j].astype(F32)
        o_ref[...] = acc

    return pl.pallas_call(
        body, name=name, grid=(r // tr,),
        out_shape=jax.ShapeDtypeStruct((r, c), F32),
        in_specs=[pl.BlockSpec((N_DEV, tr, c), lambda i: (0, i, 0))],
        out_specs=pl.BlockSpec((tr, c), lambda i: (i, 0)),
        compiler_params=_params(("arbitrary",)),
    )(slots)


def adamw_update(w, g, m, v, name):
    r, c = w.shape
    tr = _tile(r, 256) if r % 8 == 0 else r

    def body(w_ref, g_ref, m_ref, v_ref, d_ref, mo_ref, vo_ref):
        d_ref[...], mo_ref[...], vo_ref[...] = _adamw(w_ref[...], g_ref[...], m_ref[...], v_ref[...])

    spec = pl.BlockSpec((tr, c), lambda i: (i, 0))
    return pl.pallas_call(
        body, name=name, grid=(r // tr,),
        out_shape=[jax.ShapeDtypeStruct((r, c), F32)] * 3,
        in_specs=[spec] * 4, out_specs=[spec] * 3,
        compiler_params=_params(("arbitrary",)),
    )(w, g, m, v)


def sum_adamw_update(slots, w, m, v, name):
    _, r, c = slots.shape
    tr = _tile(r, 352) if r % 352 == 0 else _tile(r, 256)

    def body(s_ref, w_ref, m_ref, v_ref, g_ref, d_ref, mo_ref, vo_ref):
        g = s_ref[0].astype(F32)
        for j in range(1, N_DEV):
            g = g + s_ref[j].astype(F32)
        g_ref[...] = g
        d_ref[...], mo_ref[...], vo_ref[...] = _adamw(w_ref[...], g, m_ref[...], v_ref[...])

    spec = pl.BlockSpec((tr, c), lambda i: (i, 0))
    return pl.pallas_call(
        body, name=name, grid=(r // tr,),
        out_shape=[jax.ShapeDtypeStruct((r, c), F32)] * 4,
        in_specs=[pl.BlockSpec((N_DEV, tr, c), lambda i: (0, i, 0))] + [spec] * 3, out_specs=[spec] * 4,
        compiler_params=_params(("arbitrary",)),
    )(slots, w, m, v)


def adamw_update_small(ws, gs, ms, vs):
    n = len(ws)

    def body(*refs):
        w_r, g_r, m_r, v_r = refs[:n], refs[n:2 * n], refs[2 * n:3 * n], refs[3 * n:4 * n]
        d_r, mo_r, vo_r = refs[4 * n:5 * n], refs[5 * n:6 * n], refs[6 * n:7 * n]
        for k in range(n):
            d_r[k][...], mo_r[k][...], vo_r[k][...] = _adamw(w_r[k][...], g_r[k][...], m_r[k][...], v_r[k][...])

    shapes = [jax.ShapeDtypeStruct(w.shape, F32) for w in ws]
    out = pl.pallas_call(body, name="adamw_update_small", out_shape=shapes * 3,
                         compiler_params=_params())(*ws, *gs, *ms, *vs)
    return out[:n], out[n:2 * n], out[2 * n:]


def _slots(grad_t):
    return grad_t.reshape(N_DEV, grad_t.shape[0] // N_DEV, grad_t.shape[1])


def kernel(x, c, w_ada, b_ada, g_ffn1, w1_a, w3_a, w2_a, g_mix, w_in, conv_w, w_conv_out, a_re, a_im, b_re, b_im, c_re, c_im, log_dt, d_skip, w_glu, w_ssm_out, w_out, g_ffn2, w1_b, w3_b, w2_b, g_final, loss_target, m_w_ada, m_b_ada, m_g_ffn1, m_w1_a, m_w3_a, m_w2_a, m_g_mix, m_w_in, m_conv_w, m_w_conv_out, m_a_re, m_a_im, m_b_re, m_b_im, m_c_re, m_c_im, m_log_dt, m_d_skip, m_w_glu, m_w_ssm_out, m_w_out, m_g_ffn2, m_w1_b, m_w3_b, m_w2_b, m_g_final, v_w_ada, v_b_ada, v_g_ffn1, v_w1_a, v_w3_a, v_w2_a, v_g_mix, v_w_in, v_conv_w, v_w_conv_out, v_a_re, v_a_im, v_b_re, v_b_im, v_c_re, v_c_im, v_log_dt, v_d_skip, v_w_glu, v_w_ssm_out, v_w_out, v_g_ffn2, v_w1_b, v_w3_b, v_w2_b, v_g_final):
    nb, s, d = x.shape
    t = nb * s
    me = 4 * lax.axis_index("x") + 2 * lax.axis_index("y") + lax.axis_index("c")
    g_n, p_n, h_n = b_re.shape[1:]
    cw_n = w_conv_out.shape[1] * N_DEV
    sw_n = w_ssm_out.shape[1]
    glu_fold = d // w_glu.shape[2]

    weights = dict(w_ada=w_ada, b_ada=b_ada, g_ffn1=g_ffn1, w1_a=w1_a, w3_a=w3_a, w2_a=w2_a, g_mix=g_mix, w_in=w_in,
                   conv_w=conv_w, w_conv_out=w_conv_out, a_re=a_re, a_im=a_im, b_re=b_re, b_im=b_im, c_re=c_re,
                   c_im=c_im, log_dt=log_dt, d_skip=d_skip, w_glu=w_glu, w_ssm_out=w_ssm_out, w_out=w_out,
                   g_ffn2=g_ffn2, w1_b=w1_b, w3_b=w3_b, w2_b=w2_b, g_final=g_final)
    mom1 = dict(w_ada=m_w_ada, b_ada=m_b_ada, g_ffn1=m_g_ffn1, w1_a=m_w1_a, w3_a=m_w3_a, w2_a=m_w2_a, g_mix=m_g_mix,
                w_in=m_w_in, conv_w=m_conv_w, w_conv_out=m_w_conv_out, a_re=m_a_re, a_im=m_a_im, b_re=m_b_re,
                b_im=m_b_im, c_re=m_c_re, c_im=m_c_im, log_dt=m_log_dt, d_skip=m_d_skip, w_glu=m_w_glu,
                w_ssm_out=m_w_ssm_out, w_out=m_w_out, g_ffn2=m_g_ffn2, w1_b=m_w1_b, w3_b=m_w3_b, w2_b=m_w2_b,
                g_final=m_g_final)
    mom2 = dict(w_ada=v_w_ada, b_ada=v_b_ada, g_ffn1=v_g_ffn1, w1_a=v_w1_a, w3_a=v_w3_a, w2_a=v_w2_a, g_mix=v_g_mix,
                w_in=v_w_in, conv_w=v_conv_w, w_conv_out=v_w_conv_out, a_re=v_a_re, a_im=v_a_im, b_re=v_b_re,
                b_im=v_b_im, c_re=v_c_re, c_im=v_c_im, log_dt=v_log_dt, d_skip=v_d_skip, w_glu=v_w_glu,
                w_ssm_out=v_w_ssm_out, w_out=v_w_out, g_ffn2=v_g_ffn2, w1_b=v_w1_b, w3_b=v_w3_b, w2_b=v_w2_b,
                g_final=v_g_final)
    names = list(weights)
    transposed = ("w1_a", "w3_a", "w_in", "w_ssm_out", "w1_b", "w3_b")
    groups = dict(ffn_a=("w1_a", "w3_a", "w2_a"), mixer=("w_in", "w_conv_out", "w_glu", "w_ssm_out", "w_out"),
                  ffn_b=("w1_b", "w3_b", "w2_b"))
    big = groups["ffn_a"] + groups["mixer"] + groups["ffn_b"]

    pad_rows = lambda a: jnp.pad(a, ((0, -a.shape[0] % GATHER_ROWS), (0, 0)))
    c_all, conv_all = run_stage(GatherStage([pad_rows(c), pad_rows(conv_w[0])]), "gather_cond")
    c_all = c_all.reshape(N_DEV, -1, d)[:, :nb].reshape(N_DEV * nb, d)
    conv_full = conv_all.reshape(N_DEV, GATHER_ROWS, -1)[:, :CONV_K].transpose(1, 0, 2).reshape(CONV_K, cw_n)
    ada_cols = w_ada.shape[2]
    b_cols = lax.dynamic_slice(b_ada, (0, me * ada_cols), (1, ada_cols))
    mod_cols = ada_forward(c_all, w_ada[0], b_cols)
    (mod_all,) = run_stage(GatherStage([mod_cols]), "gather_mod")
    mod_mine = lax.dynamic_slice(mod_all.reshape(N_DEV, N_DEV * nb, ada_cols), (0, me * nb, 0), (N_DEV, nb, ada_cols))
    mod = mod_mine.transpose(1, 0, 2).reshape(nb, N_MOD, d)

    def shard_rows(name):
        w = weights[name][0]
        if name in transposed:
            w = w.T
        if name == "w_glu":
            w = w.reshape(w.shape[0] // glu_fold, d)
        return w.astype(BF16)

    def gather_stage(group):
        return GatherStage([shard_rows(n) for n in groups[group]])

    gw = {}

    def keep_weights(group, outs):
        for n, w in zip(groups[group], outs):
            gw[n] = w.reshape(sw_n, sw_n) if n == "w_glu" else w

    disc_in = (a_re[0], a_im[0], b_re[0], b_im[0], log_dt[0])
    (abr, abi, bbr, bbi), disc_vjp = jax.vjp(ssm_discretise, *disc_in)
    bd, cd, abar8, ml, mlb = ssm_tables(abr, abi, bbr, bbi, c_re[0], c_im[0], _tile(s, 256) // ROWS)

    x0 = x.reshape(t, d)
    keep_weights("ffn_a", run_stage(gather_stage("ffn_a"), "gather_ffn_a"))
    (x1, a1, b1, f1), got = ffn_forward(x0, mod, g_ffn1, gw["w1_a"], gw["w3_a"], gw["w2_a"], 0, nb,
                                        "ffn_a_forward", carried=gather_stage("mixer"))
    keep_weights("mixer", got)
    (gb, gc, vv, u, sga, sgb, h2), got = mixer_proj_forward(x1, mod, g_mix, gw["w_in"], cw_n, sw_n, nb,
                                                            carried=gather_stage("ffn_b"))
    keep_weights("ffn_b", got)
    yssm, st = ssm_forward(u, bd, cd, abar8, ml, nb)
    x2, ya, yb, mix = mixer_merge_forward(x1, gb, gc, vv, sga, sgb, yssm, u, mod, conv_full, d_skip,
                                          gw["w_conv_out"], gw["w_glu"], gw["w_ssm_out"], gw["w_out"], nb)
    (x3, a3, b3, f3), _ = ffn_forward(x2, mod, g_ffn2, gw["w1_b"], gw["w3_b"], gw["w2_b"], 2, nb, "ffn_b_forward")
    loss_parts, dx3, dg_final = loss_forward_backward(x3, loss_target.reshape(t, d), g_final.reshape(1, d))
    loss = lax.psum(jnp.sum(loss_parts[:, 0, 0]), ("x", "y", "c"))

    part, received = {}, {}

    def exchange_stage(ns):
        return ExchangeStage([_slots(part[n]) for n in ns])

    (da3, db3, sw3, df3, dgt3), _ = ffn_backward_hidden(dx3, a3, b3, f3, mod, gw["w2_b"], 2, nb,
                                                        "ffn_b_backward_hidden")
    (dx2, h3, dmod3, dg_ffn2), _ = ffn_backward_input(dx3, x2, da3, db3, mod, g_ffn2, gw["w1_b"], gw["w3_b"],
                                                      2, nb, "ffn_b_backward_input")
    (part["w1_b"],), _ = nn_matmul(da3, h3, "grad_w1_b")
    (part["w3_b"],), _ = nn_matmul(db3, h3, "grad_w3_b")
    (part["w2_b"],), _ = nn_matmul(sw3, df3, "grad_w2_b")
    (dgla, dglb, dgb, dconv, ds0, dgt2, dd_skip, dw_out, dw_co, dw_so_t, dw_glu), got = mixer_merge_backward(
        dx2, mix, ya, yb, gb, gc, vv, sga, sgb, yssm, u, mod, conv_full, d_skip,
        gw["w_conv_out"], gw["w_glu"], gw["w_ssm_out"], gw["w_out"], nb, carried=exchange_stage(groups["ffn_b"]))
    received.update(zip(groups["ffn_b"], got))
    part["w_out"] = dw_out.astype(BF16)
    part["w_conv_out"] = dw_co.astype(BF16)
    part["w_ssm_out"] = dw_so_t.astype(BF16)
    part["w_glu"] = dw_glu.reshape(sw_n // glu_fold, d).astype(BF16)
    (du, dab, dbd, dcd), got = ssm_backward(u, ds0, st, bd, cd, abar8, mlb, d_skip, nb,
                                            carried=exchange_stage(groups["mixer"][1:]))
    received.update(zip(groups["mixer"][1:], got))
    dgc, dvv, dconv_w = conv_backward(dconv, gc, vv, conv_full, nb)
    part["w_in"] = jnp.concatenate([tn_matmul(p, h2, "grad_w_in_%d" % k)
                                    for k, p in enumerate((dgb, dgc, dvv, du, dgla, dglb))], axis=0)
    (dx1, dmod2, dg_mix), got = mixer_proj_backward(dgb, dgc, dvv, du, dgla, dglb, dx2, x1, mod, g_mix, gw["w_in"], nb,
                                                    carried=exchange_stage(("w_in",)))
    received["w_in"] = got[0]
    (da1, db1, sw1, df1, dgt1), _ = ffn_backward_hidden(dx1, a1, b1, f1, mod, gw["w2_a"], 0, nb,
                                                        "ffn_a_backward_hidden")
    (part["w2_a"],), _ = nn_matmul(sw1, df1, "grad_w2_a")
    (dx0, h1, dmod1, dg_ffn1), _ = ffn_backward_input(dx1, x0, da1, db1, mod, g_ffn1, gw["w1_a"], gw["w3_a"],
                                                      0, nb, "ffn_a_backward_input")
    dabr, dabi, dbbr, dbbi, dcr, dci = ssm_table_grads(dab, dbd, dcd, g_n, p_n, h_n)
    gmod = jnp.concatenate([dmod1, dgt1, dmod2, dgt2, dmod3, dgt3], axis=1)
    small = dict(gmod=gmod, g_ffn1=dg_ffn1, g_mix=dg_mix, g_ffn2=dg_ffn2, g_final=dg_final, d_skip=dd_skip,
                 abr=dabr, abi=dabi, bbr=dbbr, bbi=dbbi, c_re=dcr, c_im=dci, conv_w=dconv_w)
    flat = jnp.concatenate([a.reshape(-1) for a in small.values()])
    n_small = flat.shape[0]
    n_rows = -(-n_small // (128 * GATHER_ROWS)) * GATHER_ROWS
    flat = jnp.pad(flat, (0, n_rows * 128 - n_small)).reshape(n_rows, 128)
    (part["w1_a"],), got = nn_matmul(da1, h1, "grad_w1_a",
                                     carried=StageGroup([exchange_stage(("w2_a",)), GatherStage([flat])]))
    received["w2_a"], small_all = got
    (part["w3_a"],), got = nn_matmul(db1, h1, "grad_w3_a", carried=exchange_stage(("w1_a",)))
    received["w1_a"] = got[0]
    (received["w3_a"],) = run_stage(exchange_stage(("w3_a",)), "exchange_w3_a")
    small_all = small_all.reshape(N_DEV, n_rows, 128)
    total = sum_slots(small_all, "sum_small_grads").reshape(-1)
    tot, off = {}, 0
    for key, like in small.items():
        n = math.prod(like.shape)
        tot[key], off = total[off:off + n].reshape(like.shape), off + n
    gmod_all = small_all.reshape(N_DEV, n_rows * 128)[:, :nb * N_MOD * d].reshape(N_DEV * nb, N_MOD * d)
    g_a_re, g_a_im, g_b_re, g_b_im, g_log_dt = disc_vjp((tot["abr"], tot["abi"], tot["bbr"], tot["bbi"]))

    grads = {}
    grads["b_ada"] = sum_rows(tot["gmod"].reshape(nb, N_MOD * d))
    grads["g_ffn1"], grads["g_mix"], grads["g_ffn2"] = tot["g_ffn1"], tot["g_mix"], tot["g_ffn2"]
    grads["g_final"] = tot["g_final"].reshape(d)
    grads["d_skip"] = tot["d_skip"]
    grads["a_re"], grads["a_im"], grads["log_dt"] = g_a_re[None], g_a_im[None], g_log_dt[None]
    grads["b_re"], grads["b_im"] = g_b_re[None], g_b_im[None]
    grads["c_re"], grads["c_im"] = tot["c_re"][None], tot["c_im"][None]
    grads["conv_w"] = lax.dynamic_slice(tot["conv_w"], (0, me * conv_w.shape[2]), (CONV_K, conv_w.shape[2]))[None]

    delta, new_m, new_v = {}, {}, {}
    for name in big:
        wmv = (weights[name][0], mom1[name][0], mom2[name][0])
        if name in transposed:
            outs = sum_adamw_update(received[name], *[a.T for a in wmv], "adamw_" + name)
            gsum, dl, mm, vn = [a.T for a in outs]
        elif received[name].shape[1:] == weights[name].shape[1:]:
            gsum, dl, mm, vn = sum_adamw_update(received[name], *wmv, "adamw_" + name)
        else:
            gsum = sum_slots(received[name], "sum_" + name).reshape(weights[name].shape[1:])
            dl, mm, vn = adamw_update(*wmv[:1], gsum, *wmv[1:], "adamw_" + name)
        grads[name] = gsum[None]
        delta[name], new_m[name], new_v[name] = dl[None], mm[None], vn[None]

    gmod_cols = lax.dynamic_slice(gmod_all, (0, me * ada_cols), (N_DEV * nb, ada_cols))
    g_wada, d_wada, m_wada, v_wada = ada_backward_update(c_all, gmod_cols, w_ada[0], m_w_ada[0], v_w_ada[0])
    grads["w_ada"], delta["w_ada"], new_m["w_ada"], new_v["w_ada"] = g_wada[None], d_wada[None], m_wada[None], v_wada[None]

    small_names = [n for n in names if n not in big and n != "w_ada"]

    narrow = ("b_re", "b_im")

    def as2d(n, a):
        a = jnp.swapaxes(a.reshape(weights[n].shape), -1, -2) if n in narrow else a
        return a.reshape(-1, a.shape[-1])

    def from2d(n, a):
        shape = weights[n].shape
        return jnp.swapaxes(a.reshape(shape[:-2] + (shape[-1], shape[-2])), -1, -2) if n in narrow else a.reshape(shape)

    sw_, sg_, sm_, sv_ = ([as2d(n, src[n]) for n in small_names] for src in (weights, grads, mom1, mom2))
    sd, smo, svo = adamw_update_small(sw_, sg_, sm_, sv_)
    for n, dl, mm, vn in zip(small_names, sd, smo, svo):
        grads[n] = grads[n].reshape(weights[n].shape)
        delta[n], new_m[n], new_v[n] = from2d(n, dl), from2d(n, mm), from2d(n, vn)

    grad_x = dx0.reshape(nb, s, d)
    return (loss, grad_x, *[grads[n] for n in names], *[delta[n] for n in names],
            *[new_m[n] for n in names], *[new_v[n] for n in names])


def sum_rows(a):
    r, c = a.shape

    def body(a_ref, o_ref):
        acc = a_ref[0:1, :]
        for j in range(1, r):
            acc = acc + a_ref[j:j + 1, :]
        o_ref[...] = acc

    return pl.pallas_call(body, name="sum_rows", out_shape=jax.ShapeDtypeStruct((1, c), F32),
                          compiler_params=_params())(a)
```

```python
import functools
import math

import jax
import jax.numpy as jnp
from jax import lax
from jax.experimental import pallas as pl
from jax.experimental.pallas import tpu as pltpu

F32 = jnp.float32
BF16 = jnp.bfloat16
N_DEV = 8
N_MOD = 9
EPS = 1e-6
CONV_K = 3
ADAM_LR = 0.001
ADAM_B1 = 0.9
ADAM_B2 = 0.999
ADAM_EPS = 1e-08
ADAM_WD = 0.01
ADAM_STEP = 10
GELU_C0 = math.sqrt(2.0 / math.pi)
GELU_C1 = 0.044715
V7X_VMEM_LIMIT = 56 * 1024 * 1024
MESH_ID = pl.DeviceIdType.MESH
NT = (((1,), (1,)), ((), ()))
TN = (((0,), (0,)), ((), ()))


def _dot(a, b, dims=None):
    if dims is None:
        return jnp.dot(a, b, preferred_element_type=F32)
    return lax.dot_general(a, b, dims, preferred_element_type=F32)


def _params(sem=None, vmem=V7X_VMEM_LIMIT):
    return pltpu.CompilerParams(dimension_semantics=sem, vmem_limit_bytes=vmem)


def _full(shape):
    return pl.BlockSpec(shape, lambda *_: (0,) * len(shape))


def _const(shape):
    return pl.BlockSpec(shape, lambda *_: (0,) * len(shape), pipeline_mode=pl.Buffered(1))


def _tile(n, want):
    t = min(n, want)
    while n % t:
        t //= 2
    return t


class GatherStage:
    COPIES = 9

    def __init__(self, shards):
        n = len(shards)
        self.inputs = list(shards)
        self.out_shape = [jax.ShapeDtypeStruct((N_DEV * s.shape[0], s.shape[1]), s.dtype) for s in shards]
        self.scratch = [pltpu.SemaphoreType.DMA((self.COPIES * n,)), pltpu.SemaphoreType.DMA((self.COPIES * n,)),
                        pltpu.SemaphoreType.DMA((n,))]

    def _plan(self, ins, outs, sems):
        send_sems, recv_sems, local_sems = sems
        n = len(ins)
        x, y, c = lax.axis_index("x"), lax.axis_index("y"), lax.axis_index("c")
        me, sibling, xn, yn, dg = (x, y, c), (x, y, 1 - c), (1 - x, y, c), (x, 1 - y, c), (1 - x, 1 - y, c)

        def rows(k, block, half=None):
            r = ins[k].shape[0]
            px, py, pc = block
            base = (4 * px + 2 * py + pc) * r
            if half is None:
                return outs[k].at[pl.ds(base, r), :]
            return outs[k].at[pl.ds(base + half * (r // 2), r // 2), :]

        def copy(k, j, block, to, half=None, src=None):
            return pltpu.make_async_remote_copy(
                src_ref=rows(k, block, half) if src is None else src, dst_ref=rows(k, block, half),
                send_sem=send_sems.at[self.COPIES * k + j], recv_sem=recv_sems.at[self.COPIES * k + j],
                device_id=to, device_id_type=MESH_ID)

        sib = lambda b: (b[0], b[1], 1 - b[2])
        mine = [pltpu.make_async_copy(ins[k], rows(k, me), local_sems.at[k]) for k in range(n)]
        first = [(0, me, sibling, None, sibling), (1, me, xn, None, xn), (2, me, yn, None, yn)]
        second = [(3, xn, yn, 0, dg), (4, yn, xn, 1, dg), (5, xn, sibling, None, sib(xn)), (6, yn, sibling, None, sib(yn))]
        third = [(7, dg, sibling, 0, sib(dg)), (8, dg, sibling, 1, sib(dg))]
        return n, me, copy, mine, first, second, third

    def start(self, ins, outs, sems):
        n, me, copy, mine, first, _, _ = self._plan(ins, outs, sems)
        for cp in mine:
            cp.start()
        for k in range(n):
            for j, block, to, half, _ in first:
                copy(k, j, block, to, half, src=ins[k]).start()

    def advance(self, ins, outs, sems):
        n, me, copy, mine, first, second, third = self._plan(ins, outs, sems)
        for k in range(n):
            copy(k, 1, first[1][4], me).wait_recv()
            copy(k, 2, first[2][4], me).wait_recv()
            for j, block, to, half, _ in second:
                copy(k, j, block, to, half).start()

    def advance_again(self, ins, outs, sems):
        n, me, copy, mine, first, second, third = self._plan(ins, outs, sems)
        for k in range(n):
            copy(k, 3, second[0][4], me, 0).wait_recv()
            copy(k, 4, second[1][4], me, 1).wait_recv()
            for j, block, to, half, _ in third:
                copy(k, j, block, to, half).start()

    def finish(self, ins, outs, sems):
        n, me, copy, mine, first, second, third = self._plan(ins, outs, sems)
        arrived = lambda k, j, block, half: copy(k, j, block, me, half).wait_recv()
        for k in range(n):
            arrived(k, 0, first[0][4], None)
            arrived(k, 5, second[2][4], None)
            arrived(k, 6, second[3][4], None)
            arrived(k, 7, third[0][4], 0)
            arrived(k, 8, third[1][4], 1)
        for k in range(n):
            for j, block, to, half, _ in first:
                copy(k, j, block, to, half, src=ins[k]).wait_send()
            for j, block, to, half, _ in second + third:
                copy(k, j, block, to, half).wait_send()
        for cp in mine:
            cp.wait()


class ExchangeStage:
    def __init__(self, bufs):
        n = len(bufs)
        self.inputs = list(bufs)
        self.out_shape = [jax.ShapeDtypeStruct(b.shape, b.dtype) for b in bufs]
        self.scratch = [pltpu.SemaphoreType.DMA((7 * n,)), pltpu.SemaphoreType.DMA((7 * n,)),
                        pltpu.SemaphoreType.DMA((n,))]

    def _plan(self, ins, outs, sems):
        send_sems, recv_sems, local_sems = sems
        n = len(ins)
        x, y, c = lax.axis_index("x"), lax.axis_index("y"), lax.axis_index("c")
        me = 4 * x + 2 * y + c
        mine = [pltpu.make_async_copy(ins[k].at[me], outs[k].at[me], local_sems.at[k]) for k in range(n)]
        copies = []
        for mask in range(1, N_DEV):
            px, py, pc = x ^ (mask >> 2), y ^ ((mask >> 1) & 1), c ^ (mask & 1)
            for k in range(n):
                copies.append(pltpu.make_async_remote_copy(
                    src_ref=ins[k].at[4 * px + 2 * py + pc], dst_ref=outs[k].at[me],
                    send_sem=send_sems.at[7 * k + mask - 1], recv_sem=recv_sems.at[7 * k + mask - 1],
                    device_id=(px, py, pc), device_id_type=MESH_ID))
        return mine, copies

    def start(self, ins, outs, sems):
        mine, copies = self._plan(ins, outs, sems)
        for cp in mine + copies:
            cp.start()

    def advance(self, ins, outs, sems):
        pass

    advance_again = advance

    def finish(self, ins, outs, sems):
        mine, copies = self._plan(ins, outs, sems)
        for cp in copies:
            cp.wait_recv()
        for cp in copies:
            cp.wait_send()
        for cp in mine:
            cp.wait()


ANY_SPEC = pl.BlockSpec(memory_space=pl.ANY)
GATHER_ROWS = 16


class StageGroup:
    def __init__(self, stages):
        self.stages = list(stages)
        self.inputs = [a for s in stages for a in s.inputs]
        self.out_shape = [o for s in stages for o in s.out_shape]
        self.scratch = [t for s in stages for t in s.scratch]

    def _parts(self, ins, outs, sems):
        i = o = t = 0
        for s in self.stages:
            ni, no, nt = len(s.inputs), len(s.out_shape), len(s.scratch)
            yield s, ins[i:i + ni], outs[o:o + no], sems[t:t + nt]
            i, o, t = i + ni, o + no, t + nt

    def start(self, ins, outs, sems):
        for s, i_, o_, t_ in self._parts(ins, outs, sems):
            s.start(i_, o_, t_)

    def advance(self, ins, outs, sems):
        for s, i_, o_, t_ in self._parts(ins, outs, sems):
            s.advance(i_, o_, t_)

    def advance_again(self, ins, outs, sems):
        for s, i_, o_, t_ in self._parts(ins, outs, sems):
            s.advance_again(i_, o_, t_)

    def finish(self, ins, outs, sems):
        for s, i_, o_, t_ in self._parts(ins, outs, sems):
            s.finish(i_, o_, t_)


def run_stage(stage, name):
    ci, co = len(stage.inputs), len(stage.out_shape)

    def body(*refs):
        ins, outs, sems = refs[:ci], refs[ci:ci + co], refs[ci + co:]
        stage.start(ins, outs, sems)
        stage.advance(ins, outs, sems)
        stage.advance_again(ins, outs, sems)
        stage.finish(ins, outs, sems)

    return pl.pallas_call(body, name=name, out_shape=stage.out_shape, in_specs=[ANY_SPEC] * ci,
                          out_specs=[ANY_SPEC] * co, scratch_shapes=stage.scratch)(*stage.inputs)


def _call(body, *, name, grid, in_specs, out_specs, out_shape, args, scratch_shapes=(), carried=None):
    sem = ("arbitrary",) * len(grid)
    if carried is None:
        return pl.pallas_call(body, name=name, grid=grid, in_specs=list(in_specs), out_specs=list(out_specs),
                              out_shape=list(out_shape), scratch_shapes=list(scratch_shapes),
                              compiler_params=_params(sem))(*args), None
    ni, no, ns = len(in_specs), len(out_shape), len(scratch_shapes)
    ci, co = len(carried.inputs), len(carried.out_shape)
    n_steps = math.prod(grid)

    def wrapped(*refs):
        ins, refs = refs[:ni], refs[ni:]
        cins, refs = refs[:ci], refs[ci:]
        outs, refs = refs[:no], refs[no:]
        couts, refs = refs[:co], refs[co:]
        scr, csems = refs[:ns], refs[ns:]
        step = functools.reduce(lambda acc, ig: acc * ig[1] + ig[0],
                                zip([pl.program_id(k) for k in range(len(grid))], grid), 0)

        @pl.when(step == 0)
        def _():
            carried.start(cins, couts, csems)

        @pl.when(step == n_steps // 2)
        def _():
            carried.advance(cins, couts, csems)

        @pl.when(step == (3 * n_steps) // 4)
        def _():
            carried.advance_again(cins, couts, csems)

        body(*ins, *outs, *scr)

        @pl.when(step == n_steps - 1)
        def _():
            carried.finish(cins, couts, csems)

    res = pl.pallas_call(
        wrapped, name=name, grid=grid, in_specs=list(in_specs) + [ANY_SPEC] * ci,
        out_specs=list(out_specs) + [ANY_SPEC] * co, out_shape=list(out_shape) + carried.out_shape,
        scratch_shapes=list(scratch_shapes) + carried.scratch, compiler_params=_params(sem),
    )(*args, *carried.inputs)
    return res[:no], res[no:]


def _norm_mod(x, g, shift, scale):
    r = lax.rsqrt(jnp.mean(x * x, axis=-1, keepdims=True) + EPS)
    n = x * r
    return (n * g) * (1.0 + scale) + shift, n, r


def _norm_mod_bwd(dh, n, r, g, scale):
    dsh = jnp.sum(dh, axis=0, keepdims=True)
    dsc = jnp.sum(dh * (n * g), axis=0, keepdims=True)
    dg = jnp.sum(dh * (1.0 + scale) * n, axis=0, keepdims=True)
    dn = dh * ((1.0 + scale) * g)
    dx = r * (dn - n * jnp.mean(n * dn, axis=-1, keepdims=True))
    return dx, dsh, dsc, dg


def _mod_rows(mod_ref, sub):
    m = mod_ref[0]
    return m[3 * sub:3 * sub + 1], m[3 * sub + 1:3 * sub + 2], m[3 * sub + 2:3 * sub + 3]


def _gelu(x):
    t = jnp.tanh(GELU_C0 * (x + GELU_C1 * x * x * x))
    return 0.5 * x * (1.0 + t), t


def _gelu_grad(x, t):
    return 0.5 * (1.0 + t) + 0.5 * x * (1.0 - t * t) * (GELU_C0 * (1.0 + 3.0 * GELU_C1 * x * x))


def _zero_when(cond, *refs):
    @pl.when(cond)
    def _():
        for r in refs:
            r[...] = jnp.zeros_like(r)


def ada_forward(c_all, w_ada, b_ada_cols):
    def body(c_ref, w_ref, b_ref, o_ref):
        c = c_ref[...]
        cond = (c * jax.nn.sigmoid(c)).astype(BF16)
        o_ref[...] = _dot(cond, w_ref[...].astype(BF16)) + b_ref[...]

    nb, d = c_all.shape
    cols = w_ada.shape[1]
    tn = _tile(cols, 384)
    return pl.pallas_call(
        body, name="ada_forward", grid=(cols // tn,),
        out_shape=jax.ShapeDtypeStruct((nb, cols), F32),
        in_specs=[_full((nb, d)), pl.BlockSpec((d, tn), lambda j: (0, j)), pl.BlockSpec((1, tn), lambda j: (0, j))],
        out_specs=pl.BlockSpec((nb, tn), lambda j: (0, j)),
        compiler_params=_params(("arbitrary",)),
    )(c_all, w_ada, b_ada_cols)


def _adamw(w, g, m, v):
    m = ADAM_B1 * m + (1.0 - ADAM_B1) * g
    v = ADAM_B2 * v + (1.0 - ADAM_B2) * (g * g)
    m_hat = m / (1.0 - ADAM_B1 ** ADAM_STEP)
    v_hat = v / (1.0 - ADAM_B2 ** ADAM_STEP)
    delta = -ADAM_LR * (m_hat / (jnp.sqrt(v_hat) + ADAM_EPS) + ADAM_WD * w)
    return delta, m, v


def ada_backward_update(c_all, gmod_cols, w, m, v):
    def body(c_ref, g_ref, w_ref, m_ref, v_ref, go_ref, d_ref, mo_ref, vo_ref):
        c = c_ref[...]
        cond = (c * jax.nn.sigmoid(c)).astype(BF16)
        g = _dot(cond, g_ref[...].astype(BF16), TN)
        go_ref[...] = g
        d_ref[...], mo_ref[...], vo_ref[...] = _adamw(w_ref[...], g, m_ref[...], v_ref[...])

    nb, d = c_all.shape
    cols = w.shape[1]
    tn = _tile(cols, 128)
    col = pl.BlockSpec((d, tn), lambda j: (0, j))
    return pl.pallas_call(
        body, name="ada_backward_update", grid=(cols // tn,),
        out_shape=[jax.ShapeDtypeStruct(w.shape, F32)] * 4,
        in_specs=[_full((nb, d)), pl.BlockSpec((nb, tn), lambda j: (0, j)), col, col, col],
        out_specs=[col] * 4,
        compiler_params=_params(("arbitrary",)),
    )(c_all, gmod_cols, w, m, v)


def _row_spec(tm, width, tiles_per_seq):
    return pl.BlockSpec((tm, width), lambda b, i: (b * tiles_per_seq + i, 0))


def _mod_spec(d):
    return pl.BlockSpec((1, N_MOD, d), lambda b, i: (b, 0, 0))


def _col_spec(rows, tm, tiles_per_seq):
    return pl.BlockSpec((rows, tm), lambda b, i: (0, b * tiles_per_seq + i))


def _ffn_chunk(f):
    return f // 2 if f % 256 == 0 and f > 1536 else f


def ffn_forward(x, mod, g, w1t, w3t, w2, sub, nb, name, carried=None):
    t, d = x.shape
    f = w1t.shape[0]
    s = t // nb
    tm = _tile(s, 512)
    fc = _ffn_chunk(f)

    def body(x_ref, mod_ref, g_ref, w1_ref, w3_ref, w2_ref, xo_ref, a_ref, b_ref, f_ref, h_ref):
        xv = x_ref[...]
        sh, sc, gt = _mod_rows(mod_ref, sub)
        h, _, _ = _norm_mod(xv, g_ref[...], sh, sc)
        hb = h.astype(BF16)
        h_ref[...] = hb
        acc_t = jnp.zeros((d, tm), F32)
        for k in range(f // fc):
            rows = slice(k * fc, (k + 1) * fc)
            a = _dot(w1_ref[rows, :], hb, NT)
            b = _dot(w3_ref[rows, :], hb, NT)
            a_ref[rows, :] = a.astype(BF16)
            b_ref[rows, :] = b.astype(BF16)
            sw = (a * jax.nn.sigmoid(a)) * b
            acc_t = acc_t + _dot(w2_ref[rows, :], sw.astype(BF16), TN)
        acc = acc_t.T
        f_ref[...] = acc.astype(BF16)
        xo_ref[...] = xv + (0.5 * gt) * acc

    tps = s // tm
    rd, cf = _row_spec(tm, d, tps), _col_spec(f, tm, tps)
    return _call(
        body, name=name, grid=(nb, tps), carried=carried,
        out_shape=[jax.ShapeDtypeStruct((t, d), F32), jax.ShapeDtypeStruct((f, t), BF16),
                   jax.ShapeDtypeStruct((f, t), BF16), jax.ShapeDtypeStruct((t, d), BF16),
                   jax.ShapeDtypeStruct((t, d), BF16)],
        in_specs=[rd, _mod_spec(d), _const((1, d)), _const((f, d)), _const((f, d)), _const((f, d))],
        out_specs=[rd, cf, cf, rd, rd],
        args=(x, mod, g, w1t, w3t, w2))


def ffn_backward_hidden(dxo, a_t, b_t, fo, mod, w2, sub, nb, name, carried=None):
    t, d = dxo.shape
    f = w2.shape[0]
    s = t // nb
    tm = _tile(s, 512)
    fc = _tile(f, 704) if f % 704 == 0 else _tile(f, 512)

    def body(dxo_ref, a_ref, b_ref, f_ref, mod_ref, w2_ref, da_ref, db_ref, s_ref, df_ref, dgt_ref):
        _zero_when(pl.program_id(1) == 0, dgt_ref)
        dxo = dxo_ref[...]
        _, _, gt = _mod_rows(mod_ref, sub)
        dfb = ((0.5 * gt) * dxo).astype(BF16)
        df_ref[...] = dfb
        dgt_ref[...] += 0.5 * jnp.sum(dxo * f_ref[...].astype(F32), axis=0, keepdims=True)[None]
        for k in range(f // fc):
            rows = slice(k * fc, (k + 1) * fc)
            ds = _dot(w2_ref[rows, :], dfb, NT).astype(BF16)
            av = a_ref[rows, :].astype(F32)
            bv = b_ref[rows, :]
            sig = jax.nn.sigmoid(av)
            sl = av * sig
            slb = sl.astype(BF16)
            da_ref[rows, :] = ds * bv * (sig + sl * (1.0 - sig)).astype(BF16)
            db_ref[rows, :] = ds * slb
            s_ref[rows, :] = slb * bv

    tps = s // tm
    rd, cf = _row_spec(tm, d, tps), _col_spec(f, tm, tps)
    return _call(
        body, name=name, grid=(nb, tps), carried=carried,
        out_shape=[jax.ShapeDtypeStruct((f, t), BF16)] * 3
        + [jax.ShapeDtypeStruct((t, d), BF16), jax.ShapeDtypeStruct((nb, 1, d), F32)],
        in_specs=[rd, cf, cf, rd, _mod_spec(d), _const((f, d))],
        out_specs=[cf, cf, cf, rd, pl.BlockSpec((1, 1, d), lambda b, i: (b, 0, 0))],
        args=(dxo, a_t, b_t, fo, mod, w2))


def ffn_backward_input(dxo, x, da_t, db_t, mod, g, w1t, w3t, sub, nb, name, carried=None):
    t, d = x.shape
    f = w1t.shape[0]
    s = t // nb
    tm = _tile(s, 512)

    def body(dxo_ref, x_ref, da_ref, db_ref, mod_ref, g_ref, w1_ref, w3_ref, dx_ref, dmod_ref, dg_ref):
        bi, i = pl.program_id(0), pl.program_id(1)
        _zero_when(i == 0, dmod_ref)
        _zero_when(jnp.logical_and(bi == 0, i == 0), dg_ref)
        sh, sc, _ = _mod_rows(mod_ref, sub)
        gv = g_ref[...]
        _, n, r = _norm_mod(x_ref[...], gv, sh, sc)
        dh_t = _dot(w1_ref[...], da_ref[...], TN) + _dot(w3_ref[...], db_ref[...], TN)
        dxn, dsh, dsc, dg = _norm_mod_bwd(dh_t.T, n, r, gv, sc)
        dx_ref[...] = dxo_ref[...] + dxn
        dmod_ref[...] += jnp.concatenate([dsh, dsc], axis=0)[None]
        dg_ref[...] += dg

    tps = s // tm
    rd, cf = _row_spec(tm, d, tps), _col_spec(f, tm, tps)
    return _call(
        body, name=name, grid=(nb, tps), carried=carried,
        out_shape=[jax.ShapeDtypeStruct((t, d), F32), jax.ShapeDtypeStruct((nb, 2, d), F32),
                   jax.ShapeDtypeStruct((1, d), F32)],
        in_specs=[rd, rd, cf, cf, _mod_spec(d), _const((1, d)), _const((f, d)), _const((f, d))],
        out_specs=[rd, pl.BlockSpec((1, 2, d), lambda b, i: (b, 0, 0)), _full((1, d))],
        args=(dxo, x, da_t, db_t, mod, g, w1t, w3t))


def nn_matmul(lhs_t, rhs, name, carried=None):
    m, t = lhs_t.shape
    n = rhs.shape[1]
    tk = _tile(t, 2048)
    tmm = m if m <= 1536 else m // 2
    nk = t // tk

    def body(a_ref, b_ref, o_ref, acc_ref):
        k = pl.program_id(1)
        _zero_when(k == 0, acc_ref)
        acc_ref[...] += _dot(a_ref[...], b_ref[...])

        @pl.when(k == nk - 1)
        def _():
            o_ref[...] = acc_ref[...].astype(BF16)

    return _call(
        body, name=name, grid=(m // tmm, nk), carried=carried,
        out_shape=[jax.ShapeDtypeStruct((m, n), BF16)],
        in_specs=[pl.BlockSpec((tmm, tk), lambda j, k: (j, k)), pl.BlockSpec((tk, n), lambda j, k: (k, 0))],
        out_specs=[pl.BlockSpec((tmm, n), lambda j, k: (j, 0))],
        scratch_shapes=[pltpu.VMEM((tmm, n), F32)],
        args=(lhs_t, rhs))


def tn_matmul(lhs, rhs, name):
    t, m = lhs.shape
    n = rhs.shape[1]
    tk = _tile(t, 2048)
    nk = t // tk

    def body(a_ref, b_ref, o_ref, acc_ref):
        k = pl.program_id(0)
        _zero_when(k == 0, acc_ref)
        acc_ref[...] += _dot(a_ref[...], b_ref[...], TN)

        @pl.when(k == nk - 1)
        def _():
            o_ref[...] = acc_ref[...].astype(BF16)

    return pl.pallas_call(
        body, name=name, grid=(nk,),
        out_shape=jax.ShapeDtypeStruct((m, n), BF16),
        in_specs=[pl.BlockSpec((tk, m), lambda k: (k, 0)), pl.BlockSpec((tk, n), lambda k: (k, 0))],
        out_specs=pl.BlockSpec((m, n), lambda k: (0, 0)),
        scratch_shapes=[pltpu.VMEM((m, n), F32)],
        compiler_params=_params(("arbitrary",)),
    )(lhs, rhs)


def mixer_proj_forward(x, mod, g, w_in_t, cw, sw, nb, carried=None):
    t, d = x.shape
    s = t // nb
    tm = _tile(s, 512)
    pieces = [(0, cw, "bf16"), (cw, cw, "bf16"), (2 * cw, cw, "bf16"), (3 * cw, sw, "f32"),
              (3 * cw + sw, d, "sig"), (3 * cw + sw + d, d, "sig")]

    def body(x_ref, mod_ref, g_ref, w_ref, *outs):
        h_ref = outs[-1]
        sh, sc, _ = _mod_rows(mod_ref, 1)
        h, _, _ = _norm_mod(x_ref[...], g_ref[...], sh, sc)
        hb = h.astype(BF16)
        h_ref[...] = hb
        for (off, width, kind), o_ref in zip(pieces, outs[:-1]):
            ck = _tile(width, 512)
            for j in range(width // ck):
                p = _dot(hb, w_ref[off + j * ck:off + (j + 1) * ck, :], NT)
                if kind == "sig":
                    p = jax.nn.sigmoid(p)
                o_ref[:, j * ck:(j + 1) * ck] = p.astype(o_ref.dtype)

    tps = s // tm
    widths = [(cw, BF16), (cw, BF16), (cw, BF16), (sw, F32), (d, BF16), (d, BF16), (d, BF16)]
    return _call(
        body, name="mixer_proj_forward", grid=(nb, tps), carried=carried,
        out_shape=[jax.ShapeDtypeStruct((t, w), dt) for w, dt in widths],
        in_specs=[_row_spec(tm, d, tps), _mod_spec(d), _const((1, d)), _const(w_in_t.shape)],
        out_specs=[_row_spec(tm, w, tps) for w, _ in widths],
        args=(x, mod, g, w_in_t))


def mixer_proj_backward(dgb, dgc, dv, du, dgla, dglb, dxo, x, mod, g, w_in_t, nb, carried=None):
    t, d = x.shape
    s = t // nb
    tm = _tile(s, 512)
    parts = [dgb, dgc, dv, du, dgla, dglb]
    offs = [0]
    for p in parts:
        offs.append(offs[-1] + p.shape[1])

    def body(*refs):
        p_refs = refs[:6]
        dxo_ref, x_ref, mod_ref, g_ref, w_ref, dx_ref, dmod_ref, dg_ref = refs[6:]
        bi, i = pl.program_id(0), pl.program_id(1)
        _zero_when(i == 0, dmod_ref)
        _zero_when(jnp.logical_and(bi == 0, i == 0), dg_ref)
        dh = jnp.zeros((tm, d), F32)
        for p_ref, off in zip(p_refs, offs):
            width = p_ref.shape[1]
            ck = _tile(width, 512)
            for j in range(width // ck):
                dh = dh + _dot(p_ref[:, j * ck:(j + 1) * ck], w_ref[off + j * ck:off + (j + 1) * ck, :])
        sh, sc, _ = _mod_rows(mod_ref, 1)
        gv = g_ref[...]
        _, n, r = _norm_mod(x_ref[...], gv, sh, sc)
        dxn, dsh, dsc, dg = _norm_mod_bwd(dh, n, r, gv, sc)
        dx_ref[...] = dxo_ref[...] + dxn
        dmod_ref[...] += jnp.concatenate([dsh, dsc], axis=0)[None]
        dg_ref[...] += dg

    tps = s // tm
    rd = _row_spec(tm, d, tps)
    return _call(
        body, name="mixer_proj_backward", grid=(nb, tps), carried=carried,
        out_shape=[jax.ShapeDtypeStruct((t, d), F32), jax.ShapeDtypeStruct((nb, 2, d), F32),
                   jax.ShapeDtypeStruct((1, d), F32)],
        in_specs=[_row_spec(tm, p.shape[1], tps) for p in parts]
        + [rd, rd, _mod_spec(d), _const((1, d)), _const(w_in_t.shape)],
        out_specs=[rd, pl.BlockSpec((1, 2, d), lambda b, i: (b, 0, 0)), _full((1, d))],
        args=(*parts, dxo, x, mod, g, w_in_t))


GROUPS_PER_BLOCK = 8
ROWS = 8
SCAN_LANES = 512


def _scan_rows(xr, xi, masks, shifts):
    for (mr, mi), sft in zip(masks, shifts):
        sr, si = pltpu.roll(xr, sft, 0), pltpu.roll(xi, sft, 0)
        xr, xi = xr + mr * sr - mi * si, xi + mr * si + mi * sr
    return xr, xi


def _cmul_add(ar, ai, cr, ci, br, bi):
    return ar * cr - ai * ci + br, ar * ci + ai * cr + bi


def _segment_rows(perm_ref, x):
    return _dot(perm_ref[0], x).astype(BF16)


def _time_rows(perm_ref, x):
    hi = x.astype(BF16)
    lo = (x - hi.astype(F32)).astype(BF16)
    return _dot(perm_ref[1], hi) + _dot(perm_ref[1], lo)


def segment_permutation(tc):
    r = jnp.arange(tc)
    p = (r[:, None] % ROWS * (tc // ROWS) + r[:, None] // ROWS == r[None, :]).astype(BF16)
    return jnp.stack([p, p.T])


def _rows_at(k, offset=0):
    return pl.ds(pl.multiple_of(k * ROWS + offset, ROWS), ROWS)


def ssm_forward(u, bd, cd, a1, ml, nb):
    t, w = u.shape
    s = t // nb
    tc = _tile(s, 256)
    seg = tc // ROWS
    nq, ub, lq = bd.shape[1], bd.shape[2], bd.shape[3]
    nl = nq * lq
    nch = s // tc
    lw = min(nl, SCAN_LANES)

    def body(u_ref, perm_ref, bd_ref, cd_ref, a1_ref, ml_ref, y_ref, st_ref, xr_s, xi_s, car_s):
        i = pl.program_id(1)

        @pl.when(i == 0)
        def _():
            car_s[...] = jnp.zeros_like(car_s)

        ub16 = _segment_rows(perm_ref, u_ref[...].astype(BF16))
        for q in range(nq):
            lanes = slice(q * lq, (q + 1) * lq)
            uq = ub16[:, q * ub:(q + 1) * ub]
            xr_s[:, lanes] = _dot(uq, bd_ref[0, q])
            xi_s[:, lanes] = _dot(uq, bd_ref[1, q])
        row_is_0 = lax.broadcasted_iota(jnp.int32, (ROWS, lw), 0) == 0
        zero = jnp.zeros((ROWS, lw), F32)
        for j in range(nl // lw):
            lanes = slice(j * lw, (j + 1) * lw)
            ar, ai = a1_ref[0, :, lanes], a1_ref[1, :, lanes]

            def local(k, c):
                return _cmul_add(ar, ai, c[0], c[1], xr_s[_rows_at(k), lanes], xi_s[_rows_at(k), lanes])

            er, ei = lax.fori_loop(0, seg, local, (zero, zero))
            masks = [(ml_ref[d, 0, :, lanes], ml_ref[d, 1, :, lanes]) for d in range(3)]
            cr, ci = _scan_rows(jnp.where(row_is_0, car_s[0, :, lanes], pltpu.roll(er, 1, 0)),
                                jnp.where(row_is_0, car_s[1, :, lanes], pltpu.roll(ei, 1, 0)), masks, (1, 2, 4))
            st_ref[0, 0, :, lanes] = cr
            st_ref[0, 1, :, lanes] = ci

            def full(k, c):
                xr, xi = _cmul_add(ar, ai, c[0], c[1], xr_s[_rows_at(k), lanes], xi_s[_rows_at(k), lanes])
                xr_s[_rows_at(k), lanes] = xr
                xi_s[_rows_at(k), lanes] = xi
                return xr, xi

            fr, fi = lax.fori_loop(0, seg, full, (cr, ci))
            car_s[0, :, lanes] = jnp.broadcast_to(fr[ROWS - 1:ROWS], fr.shape)
            car_s[1, :, lanes] = jnp.broadcast_to(fi[ROWS - 1:ROWS], fi.shape)
        y = jnp.concatenate([_dot(xr_s[:, q * lq:(q + 1) * lq].astype(BF16), cd_ref[0, q])
                             + _dot(xi_s[:, q * lq:(q + 1) * lq].astype(BF16), cd_ref[1, q]) for q in range(nq)],
                            axis=1)
        y_ref[...] = _time_rows(perm_ref, y)

    perm = segment_permutation(tc)
    return pl.pallas_call(
        body, name="ssm_forward", grid=(nb, nch),
        out_shape=[jax.ShapeDtypeStruct((t, w), F32), jax.ShapeDtypeStruct((nb * nch, 2, ROWS, nl), F32)],
        in_specs=[pl.BlockSpec((tc, w), lambda b, i: (b * nch + i, 0)), _const(perm.shape), _const(bd.shape),
                  _const(cd.shape), _const(a1.shape), _const(ml.shape)],
        out_specs=[pl.BlockSpec((tc, w), lambda b, i: (b * nch + i, 0)),
                   pl.BlockSpec((1, 2, ROWS, nl), lambda b, i: (b * nch + i, 0, 0, 0))],
        scratch_shapes=[pltpu.VMEM((tc, nl), F32), pltpu.VMEM((tc, nl), F32), pltpu.VMEM((2, ROWS, nl), F32)],
        compiler_params=_params(("arbitrary", "arbitrary")),
    )(u, perm, bd, cd, a1, ml)


def ssm_backward(u, dy, st, bd, cd, a1, mlb, dskip, nb, carried=None):
    t, w = u.shape
    s = t // nb
    tc = _tile(s, 256)
    seg = tc // ROWS
    nq, ub, lq = bd.shape[1], bd.shape[2], bd.shape[3]
    nl = nq * lq
    nch = s // tc
    lw = min(nl, SCAN_LANES)

    def body(u_ref, dy_ref, st_ref, perm_ref, bd_ref, cd_ref, a1_ref, mlb_ref, dsk_ref,
             du_ref, dab_ref, dbd_ref, dcd_ref, xr_s, xi_s, lr_s, li_s, car_s):
        bi, i = pl.program_id(0), pl.program_id(1)
        first = jnp.logical_and(bi == 0, i == 0)

        @pl.when(i == 0)
        def _():
            car_s[...] = jnp.zeros_like(car_s)

        @pl.when(first)
        def _():
            dab_ref[...] = jnp.zeros_like(dab_ref)
            dbd_ref[...] = jnp.zeros_like(dbd_ref)
            dcd_ref[...] = jnp.zeros_like(dcd_ref)

        ub16 = _segment_rows(perm_ref, u_ref[...].astype(BF16))
        dyb16 = _segment_rows(perm_ref, dy_ref[...].astype(BF16))
        xr_s[0:ROWS, :] = st_ref[0, 0]
        xi_s[0:ROWS, :] = st_ref[0, 1]
        for q in range(nq):
            lanes = slice(q * lq, (q + 1) * lq)
            uq = ub16[:, q * ub:(q + 1) * ub]
            dq = dyb16[:, q * ub:(q + 1) * ub]
            xr_s[ROWS:, lanes] = _dot(uq, bd_ref[0, q])
            xi_s[ROWS:, lanes] = _dot(uq, bd_ref[1, q])
            lr_s[:, lanes] = _dot(dq, cd_ref[0, q], NT)
            li_s[:, lanes] = _dot(dq, cd_ref[1, q], NT)
        row_is_7 = lax.broadcasted_iota(jnp.int32, (ROWS, lw), 0) == ROWS - 1
        zero = jnp.zeros((ROWS, lw), F32)
        for j in range(nl // lw):
            lanes = slice(j * lw, (j + 1) * lw)
            ar, ai = a1_ref[0, :, lanes], a1_ref[1, :, lanes]
            nai = -ai

            def states(k, c):
                xr, xi = _cmul_add(ar, ai, c[0], c[1], xr_s[_rows_at(k, ROWS), lanes], xi_s[_rows_at(k, ROWS), lanes])
                xr_s[_rows_at(k, ROWS), lanes] = xr
                xi_s[_rows_at(k, ROWS), lanes] = xi
                return xr, xi

            lax.fori_loop(0, seg, states, (st_ref[0, 0, :, lanes], st_ref[0, 1, :, lanes]))

            def local(kk, c):
                k = seg - 1 - kk
                return _cmul_add(ar, nai, c[0], c[1], lr_s[_rows_at(k), lanes], li_s[_rows_at(k), lanes])

            er, ei = lax.fori_loop(0, seg, local, (zero, zero))
            masks = [(mlb_ref[d, 0, :, lanes], mlb_ref[d, 1, :, lanes]) for d in range(3)]
            cr, ci = _scan_rows(jnp.where(row_is_7, car_s[0, :, lanes], pltpu.roll(er, ROWS - 1, 0)),
                                jnp.where(row_is_7, car_s[1, :, lanes], pltpu.roll(ei, ROWS - 1, 0)), masks, (7, 6, 4))

            def full(kk, c):
                cr_, ci_, accr, acci = c
                k = seg - 1 - kk
                lr, li = _cmul_add(ar, nai, cr_, ci_, lr_s[_rows_at(k), lanes], li_s[_rows_at(k), lanes])
                lr_s[_rows_at(k), lanes] = lr
                li_s[_rows_at(k), lanes] = li
                xpr, xpi = xr_s[_rows_at(k), lanes], xi_s[_rows_at(k), lanes]
                return lr, li, accr + lr * xpr + li * xpi, acci + li * xpr - lr * xpi

            lr0, li0, accr, acci = lax.fori_loop(0, seg, full, (cr, ci, zero, zero))
            car_s[0, :, lanes] = jnp.broadcast_to(lr0[0:1], lr0.shape)
            car_s[1, :, lanes] = jnp.broadcast_to(li0[0:1], li0.shape)
            dab_ref[0, :, lanes] += accr
            dab_ref[1, :, lanes] += acci
        du_parts = []
        for q in range(nq):
            lanes = slice(q * lq, (q + 1) * lq)
            cols = slice(q * ub, (q + 1) * ub)
            lrb, lib = lr_s[:, lanes].astype(BF16), li_s[:, lanes].astype(BF16)
            uq, dq = ub16[:, cols], dyb16[:, cols]
            du_parts.append(_dot(lrb, bd_ref[0, q], NT) + _dot(lib, bd_ref[1, q], NT))
            dbd_ref[0, q] += _dot(uq, lrb, TN)
            dbd_ref[1, q] += _dot(uq, lib, TN)
            dcd_ref[0, q] += _dot(xr_s[ROWS:, lanes].astype(BF16), dq, TN)
            dcd_ref[1, q] += _dot(xi_s[ROWS:, lanes].astype(BF16), dq, TN)
        du = _time_rows(perm_ref, jnp.concatenate(du_parts, axis=1)) + dsk_ref[...] * dy_ref[...]
        du_ref[...] = du.astype(BF16)

    rev = lambda b, i: (b * nch + nch - 1 - i, 0)
    perm = segment_permutation(tc)
    return _call(
        body, name="ssm_backward", grid=(nb, nch), carried=carried,
        out_shape=[jax.ShapeDtypeStruct((t, w), BF16), jax.ShapeDtypeStruct((2, ROWS, nl), F32),
                   jax.ShapeDtypeStruct(bd.shape, F32), jax.ShapeDtypeStruct(cd.shape, F32)],
        in_specs=[pl.BlockSpec((tc, w), rev), pl.BlockSpec((tc, w), rev),
                  pl.BlockSpec((1, 2, ROWS, nl), lambda b, i: (b * nch + nch - 1 - i, 0, 0, 0)),
                  _const(perm.shape), _const(bd.shape), _const(cd.shape), _const(a1.shape), _const(mlb.shape),
                  _const((1, w))],
        out_specs=[pl.BlockSpec((tc, w), rev), _full((2, ROWS, nl)), _full(bd.shape), _full(cd.shape)],
        scratch_shapes=[pltpu.VMEM((tc + ROWS, nl), F32), pltpu.VMEM((tc + ROWS, nl), F32),
                        pltpu.VMEM((tc, nl), F32), pltpu.VMEM((tc, nl), F32), pltpu.VMEM((2, ROWS, nl), F32)],
        args=(u, dy, st, perm, bd, cd, a1, mlb, dskip))


def ssm_discretise(a_re, a_im, b_re, b_im, log_dt):
    dt = jnp.exp(log_dt)[:, None]
    er = jnp.exp(a_re * dt)
    abr, abi = er * jnp.cos(a_im * dt), er * jnp.sin(a_im * dt)
    den = a_re * a_re + a_im * a_im
    nr, ni = abr - 1.0, abi
    fr = ((nr * a_re + ni * a_im) / den)[..., None]
    fi = ((ni * a_re - nr * a_im) / den)[..., None]
    return abr, abi, fr * b_re - fi * b_im, fr * b_im + fi * b_re


def _complex_square(zr, zi):
    return zr * zr - zi * zi, 2.0 * zr * zi


def ssm_tables(abr, abi, bbr, bbi, c_re, c_im, seg):
    g, p, h = bbr.shape
    nq = g // GROUPS_PER_BLOCK
    zr, zi = abr.reshape(1, -1), abi.reshape(1, -1)
    a1 = jnp.stack([jnp.broadcast_to(zr, (ROWS, g * p)), jnp.broadcast_to(zi, (ROWS, g * p))])
    for _ in range(seg.bit_length() - 1):
        zr, zi = _complex_square(zr, zi)
    row = jnp.arange(ROWS)[:, None]
    ml, mlb = [], []
    for d in (1, 2, 4):
        ml.append(jnp.stack([jnp.where(row >= d, zr, 0.0), jnp.where(row >= d, zi, 0.0)]))
        mlb.append(jnp.stack([jnp.where(row + d < ROWS, zr, 0.0), jnp.where(row + d < ROWS, -zi, 0.0)]))
        zr, zi = _complex_square(zr, zi)
    eye = jnp.eye(GROUPS_PER_BLOCK, dtype=F32)

    def block_diag_in(bb):
        bq = bb.reshape(nq, GROUPS_PER_BLOCK, p, h)
        return jnp.einsum("qaph,ab->qahbp", bq, eye).reshape(nq, GROUPS_PER_BLOCK * h, GROUPS_PER_BLOCK * p)

    def block_diag_out(cc):
        cq = cc.reshape(nq, GROUPS_PER_BLOCK, h, p)
        return jnp.einsum("qahp,ab->qapbh", cq, eye).reshape(nq, GROUPS_PER_BLOCK * p, GROUPS_PER_BLOCK * h)

    bd = jnp.stack([block_diag_in(bbr), block_diag_in(bbi)]).astype(BF16)
    cd = jnp.stack([block_diag_out(c_re), block_diag_out(-c_im)]).astype(BF16)
    return bd, cd, a1, jnp.stack(ml), jnp.stack(mlb)


def ssm_table_grads(dab, dbd, dcd, g, p, h):
    nq = g // GROUPS_PER_BLOCK
    dabr, dabi = dab[0].sum(0).reshape(g, p), dab[1].sum(0).reshape(g, p)
    b5 = dbd.reshape(2, nq, GROUPS_PER_BLOCK, h, GROUPS_PER_BLOCK, p)
    dbb = jnp.einsum("rqahap->rqaph", b5).reshape(2, g, p, h)
    c5 = dcd.reshape(2, nq, GROUPS_PER_BLOCK, p, GROUPS_PER_BLOCK, h)
    dcc = jnp.einsum("rqapah->rqahp", c5).reshape(2, g, h, p)
    return dabr, dabi, dbb[0], dbb[1], dcc[0], -dcc[1]


HALO = 16


def _conv_inputs(gc_ref, v_ref, gch_ref, vh_ref, cv_s, i, tm):
    cv = gc_ref[...].astype(F32) * v_ref[...].astype(F32)
    halo = gch_ref[...].astype(F32) * vh_ref[...].astype(F32)
    cv_s[0:HALO, :] = jnp.where(i == 0, 0.0, halo)
    cv_s[HALO:, :] = cv
    return cv, cv_s[HALO - 1:HALO - 1 + tm, :], cv_s[HALO - 2:HALO - 2 + tm, :]


def _halo_spec(tm, width, tiles_per_seq):
    per = tm // HALO
    return pl.BlockSpec((HALO, width), lambda b, i: (jnp.maximum((b * tiles_per_seq + i) * per - 1, 0), 0))


def mixer_merge_forward(x, gb, gc, v, sga, sgb, yssm, u, mod, conv_w, dskip, wco, wglu, wso_t, wout, nb):
    t, d = x.shape
    cw, sw = gb.shape[1], u.shape[1]
    s = t // nb
    tm = _tile(s, 256)
    tps = s // tm

    def body(x_ref, gb_ref, gc_ref, v_ref, gch_ref, vh_ref, sga_ref, sgb_ref, ys_ref, u_ref, mod_ref, cw_ref,
             dsk_ref, wco_ref, wglu_ref, wso_ref, wout_ref, xo_ref, ya_ref, yb_ref, mix_ref, cv_s):
        i = pl.program_id(1)
        cv, cv1, cv2 = _conv_inputs(gc_ref, v_ref, gch_ref, vh_ref, cv_s, i, tm)
        w = cw_ref[...]
        conv = w[0:1] * cv2 + w[1:2] * cv1 + w[2:3] * cv
        ya = _dot((gb_ref[...].astype(F32) * conv).astype(BF16), wco_ref[...])
        s0 = ys_ref[...] + dsk_ref[...] * u_ref[...]
        s1, _ = _gelu(s0)
        z = _dot(s1.astype(BF16), wglu_ref[...])
        s2 = s1 * jax.nn.sigmoid(z)
        yb = _dot(s2.astype(BF16), wso_ref[...], NT)
        merged = sga_ref[...].astype(F32) * ya + sgb_ref[...].astype(F32) * yb
        mix = _dot(merged.astype(BF16), wout_ref[...])
        _, _, gt = _mod_rows(mod_ref, 1)
        xo_ref[...] = x_ref[...] + gt * mix
        ya_ref[...] = ya.astype(BF16)
        yb_ref[...] = yb.astype(BF16)
        mix_ref[...] = mix.astype(BF16)

    rd, rc, rw = _row_spec(tm, d, tps), _row_spec(tm, cw, tps), _row_spec(tm, sw, tps)
    hc = _halo_spec(tm, cw, tps)
    return pl.pallas_call(
        body, name="mixer_merge_forward", grid=(nb, tps),
        out_shape=[jax.ShapeDtypeStruct((t, d), F32)] + [jax.ShapeDtypeStruct((t, d), BF16)] * 3,
        in_specs=[rd, rc, rc, rc, hc, hc, rd, rd, rw, rw, _mod_spec(d), _const(conv_w.shape), _const((1, sw)),
                  _const(wco.shape), _const(wglu.shape), _const(wso_t.shape), _const(wout.shape)],
        out_specs=[rd, rd, rd, rd],
        scratch_shapes=[pltpu.VMEM((tm + HALO, cw), F32)],
        compiler_params=_params(("arbitrary", "arbitrary")),
    )(x, gb, gc, v, gc, v, sga, sgb, yssm, u, mod, conv_w, dskip, wco, wglu, wso_t, wout)


def mixer_merge_backward(dxo, mix, ya, yb, gb, gc, v, sga, sgb, yssm, u, mod, conv_w, dskip,
                         wco, wglu, wso_t, wout, nb, carried=None):
    t, d = dxo.shape
    cw, sw = gb.shape[1], u.shape[1]
    s = t // nb
    tm = _tile(s, 256)
    tps = s // tm

    def body(dxo_ref, mix_ref, ya_ref, yb_ref, gb_ref, gc_ref, v_ref, gch_ref, vh_ref, sga_ref, sgb_ref, ys_ref,
             u_ref, mod_ref, cw_ref, dsk_ref, wco_ref, wglu_ref, wso_ref, wout_ref,
             dgla_ref, dglb_ref, dgb_ref, dconv_ref, ds0_ref, dgt_ref, ddsk_ref, dwout_ref, dwco_ref, dwso_ref,
             dwglu_ref, cv_s):
        bi, i = pl.program_id(0), pl.program_id(1)
        _zero_when(i == 0, dgt_ref)
        _zero_when(jnp.logical_and(bi == 0, i == 0), ddsk_ref, dwout_ref, dwco_ref, dwso_ref, dwglu_ref)
        dxo = dxo_ref[...]
        _, _, gt = _mod_rows(mod_ref, 1)
        dmix = (gt * dxo).astype(BF16)
        dgt_ref[...] += jnp.sum(dxo * mix_ref[...].astype(F32), axis=0, keepdims=True)[None]
        ya, yb = ya_ref[...].astype(F32), yb_ref[...].astype(F32)
        sga, sgb = sga_ref[...].astype(F32), sgb_ref[...].astype(F32)
        merged = (sga * ya + sgb * yb).astype(BF16)
        dwout_ref[...] += _dot(merged, dmix, TN)
        dmerged = _dot(dmix, wout_ref[...], NT)
        dgla_ref[...] = (dmerged * ya * sga * (1.0 - sga)).astype(BF16)
        dglb_ref[...] = (dmerged * yb * sgb * (1.0 - sgb)).astype(BF16)
        dya = (dmerged * sga).astype(BF16)
        dyb = (dmerged * sgb).astype(BF16)
        cv, cv1, cv2 = _conv_inputs(gc_ref, v_ref, gch_ref, vh_ref, cv_s, i, tm)
        w = cw_ref[...]
        conv = w[0:1] * cv2 + w[1:2] * cv1 + w[2:3] * cv
        gbv = gb_ref[...].astype(F32)
        dwco_ref[...] += _dot((gbv * conv).astype(BF16), dya, TN)
        dya_in = _dot(dya, wco_ref[...], NT)
        dgb_ref[...] = (dya_in * conv).astype(BF16)
        dconv_ref[...] = dya_in * gbv
        uv = u_ref[...]
        s0 = ys_ref[...] + dsk_ref[...] * uv
        s1, th = _gelu(s0)
        s1b = s1.astype(BF16)
        sz = jax.nn.sigmoid(_dot(s1b, wglu_ref[...]))
        s2b = (s1 * sz).astype(BF16)
        dwso_ref[...] += _dot(dyb, s2b, TN)
        ds2 = _dot(dyb, wso_ref[...])
        dz = (ds2 * s1 * sz * (1.0 - sz)).astype(BF16)
        dwglu_ref[...] += _dot(s1b, dz, TN)
        ds1 = ds2 * sz + _dot(dz, wglu_ref[...], NT)
        ds0 = ds1 * _gelu_grad(s0, th)
        ds0_ref[...] = ds0
        ddsk_ref[...] += jnp.sum(ds0 * uv, axis=0, keepdims=True)

    rd, rc, rw = _row_spec(tm, d, tps), _row_spec(tm, cw, tps), _row_spec(tm, sw, tps)
    hc = _halo_spec(tm, cw, tps)
    return _call(
        body, name="mixer_merge_backward", grid=(nb, tps), carried=carried,
        out_shape=[jax.ShapeDtypeStruct((t, d), BF16), jax.ShapeDtypeStruct((t, d), BF16),
                   jax.ShapeDtypeStruct((t, cw), BF16), jax.ShapeDtypeStruct((t, cw), F32),
                   jax.ShapeDtypeStruct((t, sw), F32), jax.ShapeDtypeStruct((nb, 1, d), F32),
                   jax.ShapeDtypeStruct((1, sw), F32), jax.ShapeDtypeStruct(wout.shape, F32),
                   jax.ShapeDtypeStruct(wco.shape, F32), jax.ShapeDtypeStruct(wso_t.shape, F32),
                   jax.ShapeDtypeStruct(wglu.shape, F32)],
        in_specs=[rd, rd, rd, rd, rc, rc, rc, hc, hc, rd, rd, rw, rw, _mod_spec(d), _const(conv_w.shape),
                  _const((1, sw)), _const(wco.shape), _const(wglu.shape), _const(wso_t.shape), _const(wout.shape)],
        out_specs=[rd, rd, rc, rc, rw, pl.BlockSpec((1, 1, d), lambda b, i: (b, 0, 0)), _full((1, sw)),
                   _full(wout.shape), _full(wco.shape), _full(wso_t.shape), _full(wglu.shape)],
        scratch_shapes=[pltpu.VMEM((tm + HALO, cw), F32)],
        args=(dxo, mix, ya, yb, gb, gc, v, gc, v, sga, sgb, yssm, u, mod, conv_w, dskip, wco, wglu, wso_t, wout))


def conv_backward(dconv, gc, v, conv_w, nb):
    t, cw = dconv.shape
    s = t // nb
    tm = _tile(s, 512)
    tps = s // tm
    per = tm // ROWS
    slab = _tile(tm, 32)

    def body(dc_ref, dcn_ref, gc_ref, v_ref, gch_ref, vh_ref, cw_ref, dgc_ref, dv_ref, dw_ref, cv_s, dc_s):
        bi, i = pl.program_id(0), pl.program_id(1)
        _zero_when(jnp.logical_and(bi == 0, i == 0), dw_ref)
        cv_s[0:HALO, :] = jnp.where(i == 0, 0.0, gch_ref[...].astype(F32) * vh_ref[...].astype(F32))
        dc_s[tm:, :] = jnp.where(i == tps - 1, 0.0, dcn_ref[...])
        for r in range(0, tm, slab):
            cv_s[HALO + r:HALO + r + slab, :] = (gc_ref[r:r + slab, :].astype(F32)
                                                 * v_ref[r:r + slab, :].astype(F32))
            dc_s[r:r + slab, :] = dc_ref[r:r + slab, :]
        w = cw_ref[...]
        sums = [jnp.zeros((ROWS, cw), F32)] * CONV_K
        for r in range(0, tm, slab):
            dc = dc_s[r:r + slab, :]
            dcv = w[2:3] * dc + w[1:2] * dc_s[r + 1:r + 1 + slab, :] + w[0:1] * dc_s[r + 2:r + 2 + slab, :]
            dgc_ref[r:r + slab, :] = (dcv * v_ref[r:r + slab, :].astype(F32)).astype(BF16)
            dv_ref[r:r + slab, :] = (dcv * gc_ref[r:r + slab, :].astype(F32)).astype(BF16)
            for k in range(CONV_K):
                lag = HALO + r - (CONV_K - 1 - k)
                prod = dc * cv_s[lag:lag + slab, :]
                sums[k] = sums[k] + jnp.sum(prod.reshape(slab // ROWS, ROWS, cw), axis=0)
        dw_ref[...] += jnp.concatenate([jnp.sum(a, axis=0, keepdims=True) for a in sums], axis=0)

    rc = _row_spec(tm, cw, tps)
    nxt = pl.BlockSpec((ROWS, cw), lambda b, i: (jnp.minimum((b * tps + i + 1) * per, nb * tps * per - 1), 0))
    hc = _halo_spec(tm, cw, tps)
    return pl.pallas_call(
        body, name="conv_backward", grid=(nb, tps),
        out_shape=[jax.ShapeDtypeStruct((t, cw), BF16), jax.ShapeDtypeStruct((t, cw), BF16),
                   jax.ShapeDtypeStruct(conv_w.shape, F32)],
        in_specs=[rc, nxt, rc, rc, hc, hc, _full(conv_w.shape)],
        out_specs=[rc, rc, _full(conv_w.shape)],
        scratch_shapes=[pltpu.VMEM((tm + HALO, cw), F32), pltpu.VMEM((tm + ROWS, cw), F32)],
        compiler_params=_params(("arbitrary", "arbitrary")),
    )(dconv, dconv, gc, v, gc, v, conv_w)


def loss_forward_backward(x, target, g):
    t, d = x.shape
    tm = _tile(t, 512)

    def body(x_ref, t_ref, g_ref, l_ref, dx_ref, dg_ref):
        _zero_when(pl.program_id(0) == 0, dg_ref)
        xv = x_ref[...]
        gv = g_ref[...]
        r = lax.rsqrt(jnp.mean(xv * xv, axis=-1, keepdims=True) + EPS)
        n = xv * r
        err = n * gv - t_ref[...]
        l_ref[...] = jnp.full(l_ref.shape, 0.5 * jnp.sum(jnp.mean(err * err, axis=-1)), F32)
        dy = err * (1.0 / d)
        dn = dy * gv
        dx_ref[...] = r * (dn - n * jnp.mean(n * dn, axis=-1, keepdims=True))
        dg_ref[...] += jnp.sum(dy * n, axis=0, keepdims=True)

    row = pl.BlockSpec((tm, d), lambda i: (i, 0))
    return pl.pallas_call(
        body, name="loss_forward_backward", grid=(t // tm,),
        out_shape=[jax.ShapeDtypeStruct((t // tm, 1, 128), F32), jax.ShapeDtypeStruct((t, d), F32),
                   jax.ShapeDtypeStruct((1, d), F32)],
        in_specs=[row, row, _full((1, d))],
        out_specs=[pl.BlockSpec((1, 1, 128), lambda i: (i, 0, 0)), row, _full((1, d))],
        compiler_params=_params(("arbitrary",)),
    )(x, target, g)


def sum_slots(slots, name):
    _, r, c = slots.shape
    tr = _tile(r, 352) if r % 352 == 0 else _tile(r, 256)

    def body(s_ref, o_ref):
        acc = s_ref[0].astype(F32)
        for j in range(1, N_DEV):
            acc = acc + s_ref[j].astype(F32)
        o_ref[...] = acc

    return pl.pallas_call(
        body, name=name, grid=(r // tr,),
        out_shape=jax.ShapeDtypeStruct((r, c), F32),
        in_specs=[pl.BlockSpec((N_DEV, tr, c), lambda i: (0, i, 0))],
        out_specs=pl.BlockSpec((tr, c), lambda i: (i, 0)),
        compiler_params=_params(("arbitrary",)),
    )(slots)


def adamw_update(w, g, m, v, name):
    r, c = w.shape
    tr = _tile(r, 256) if r % 8 == 0 else r

    def body(w_ref, g_ref, m_ref, v_ref, d_ref, mo_ref, vo_ref):
        d_ref[...], mo_ref[...], vo_ref[...] = _adamw(w_ref[...], g_ref[...], m_ref[...], v_ref[...])

    spec = pl.BlockSpec((tr, c), lambda i: (i, 0))
    return pl.pallas_call(
        body, name=name, grid=(r // tr,),
        out_shape=[jax.ShapeDtypeStruct((r, c), F32)] * 3,
        in_specs=[spec] * 4, out_specs=[spec] * 3,
        compiler_params=_params(("arbitrary",)),
    )(w, g, m, v)


def sum_adamw_update(slots, w, m, v, name):
    _, r, c = slots.shape
    tr = _tile(r, 352) if r % 352 == 0 else _tile(r, 256)

    def body(s_ref, w_ref, m_ref, v_ref, g_ref, d_ref, mo_ref, vo_ref):
        g = s_ref[0].astype(F32)
        for j in range(1, N_DEV):
            g = g + s_ref[j].astype(F32)
        g_ref[...] = g
        d_ref[...], mo_ref[...], vo_ref[...] = _adamw(w_ref[...], g, m_ref[...], v_ref[...])

    spec = pl.BlockSpec((tr, c), lambda i: (i, 0))
    return pl.pallas_call(
        body, name=name, grid=(r // tr,),
        out_shape=[jax.ShapeDtypeStruct((r, c), F32)] * 4,
        in_specs=[pl.BlockSpec((N_DEV, tr, c), lambda i: (0, i, 0))] + [spec] * 3, out_specs=[spec] * 4,
        compiler_params=_params(("arbitrary",)),
    )(slots, w, m, v)


def adamw_update_small(ws, gs, ms, vs):
    n = len(ws)

    def body(*refs):
        w_r, g_r, m_r, v_r = refs[:n], refs[n:2 * n], refs[2 * n:3 * n], refs[3 * n:4 * n]
        d_r, mo_r, vo_r = refs[4 * n:5 * n], refs[5 * n:6 * n], refs[6 * n:7 * n]
        for k in range(n):
            d_r[k][...], mo_r[k][...], vo_r[k][...] = _adamw(w_r[k][...], g_r[k][...], m_r[k][...], v_r[k][...])

    shapes = [jax.ShapeDtypeStruct(w.shape, F32) for w in ws]
    out = pl.pallas_call(body, name="adamw_update_small", out_shape=shapes * 3,
                         compiler_params=_params())(*ws, *gs, *ms, *vs)
    return out[:n], out[n:2 * n], out[2 * n:]


def _slots(grad_t):
    return grad_t.reshape(N_DEV, grad_t.shape[0] // N_DEV, grad_t.shape[1])


def kernel(x, c, w_ada, b_ada, g_ffn1, w1_a, w3_a, w2_a, g_mix, w_in, conv_w, w_conv_out, a_re, a_im, b_re, b_im, c_re, c_im, log_dt, d_skip, w_glu, w_ssm_out, w_out, g_ffn2, w1_b, w3_b, w2_b, g_final, loss_target, m_w_ada, m_b_ada, m_g_ffn1, m_w1_a, m_w3_a, m_w2_a, m_g_mix, m_w_in, m_conv_w, m_w_conv_out, m_a_re, m_a_im, m_b_re, m_b_im, m_c_re, m_c_im, m_log_dt, m_d_skip, m_w_glu, m_w_ssm_out, m_w_out, m_g_ffn2, m_w1_b, m_w3_b, m_w2_b, m_g_final, v_w_ada, v_b_ada, v_g_ffn1, v_w1_a, v_w3_a, v_w2_a, v_g_mix, v_w_in, v_conv_w, v_w_conv_out, v_a_re, v_a_im, v_b_re, v_b_im, v_c_re, v_c_im, v_log_dt, v_d_skip, v_w_glu, v_w_ssm_out, v_w_out, v_g_ffn2, v_w1_b, v_w3_b, v_w2_b, v_g_final):
    nb, s, d = x.shape
    t = nb * s
    me = 4 * lax.axis_index("x") + 2 * lax.axis_index("y") + lax.axis_index("c")
    g_n, p_n, h_n = b_re.shape[1:]
    cw_n = w_conv_out.shape[1] * N_DEV
    sw_n = w_ssm_out.shape[1]
    glu_fold = d // w_glu.shape[2]

    weights = dict(w_ada=w_ada, b_ada=b_ada, g_ffn1=g_ffn1, w1_a=w1_a, w3_a=w3_a, w2_a=w2_a, g_mix=g_mix, w_in=w_in,
                   conv_w=conv_w, w_conv_out=w_conv_out, a_re=a_re, a_im=a_im, b_re=b_re, b_im=b_im, c_re=c_re,
                   c_im=c_im, log_dt=log_dt, d_skip=d_skip, w_glu=w_glu, w_ssm_out=w_ssm_out, w_out=w_out,
                   g_ffn2=g_ffn2, w1_b=w1_b, w3_b=w3_b, w2_b=w2_b, g_final=g_final)
    mom1 = dict(w_ada=m_w_ada, b_ada=m_b_ada, g_ffn1=m_g_ffn1, w1_a=m_w1_a, w3_a=m_w3_a, w2_a=m_w2_a, g_mix=m_g_mix,
                w_in=m_w_in, conv_w=m_conv_w, w_conv_out=m_w_conv_out, a_re=m_a_re, a_im=m_a_im, b_re=m_b_re,
                b_im=m_b_im, c_re=m_c_re, c_im=m_c_im, log_dt=m_log_dt, d_skip=m_d_skip, w_glu=m_w_glu,
                w_ssm_out=m_w_ssm_out, w_out=m_w_out, g_ffn2=m_g_ffn2, w1_b=m_w1_b, w3_b=m_w3_b, w2_b=m_w2_b,
                g_final=m_g_final)
    mom2 = dict(w_ada=v_w_ada, b_ada=v_b_ada, g_ffn1=v_g_ffn1, w1_a=v_w1_a, w3_a=v_w3_a, w2_a=v_w2_a, g_mix=v_g_mix,
                w_in=v_w_in, conv_w=v_conv_w, w_conv_out=v_w_conv_out, a_re=v_a_re, a_im=v_a_im, b_re=v_b_re,
                b_im=v_b_im, c_re=v_c_re, c_im=v_c_im, log_dt=v_log_dt, d_skip=v_d_skip, w_glu=v_w_glu,
                w_ssm_out=v_w_ssm_out, w_out=v_w_out, g_ffn2=v_g_ffn2, w1_b=v_w1_b, w3_b=v_w3_b, w2_b=v_w2_b,
                g_final=v_g_final)
    names = list(weights)
    transposed = ("w1_a", "w3_a", "w_in", "w_ssm_out", "w1_b", "w3_b")
    groups = dict(ffn_a=("w1_a", "w3_a", "w2_a"), mixer=("w_in", "w_conv_out", "w_glu", "w_ssm_out", "w_out"),
                  ffn_b=("w1_b", "w3_b", "w2_b"))
    big = groups["ffn_a"] + groups["mixer"] + groups["ffn_b"]

    def shard_rows(name):
        w = weights[name][0]
        if name in transposed:
            w = w.T
        if name == "w_glu":
            w = w.reshape(w.shape[0] // glu_fold, d)
        return w.astype(BF16)

    def gather_stage(group):
        return GatherStage([shard_rows(n) for n in groups[group]])

    gw = {}

    def keep_weights(group, outs):
        for n, w in zip(groups[group], outs):
            gw[n] = w.reshape(sw_n, sw_n) if n == "w_glu" else w

    pad_rows = lambda a: jnp.pad(a, ((0, -a.shape[0] % GATHER_ROWS), (0, 0)))
    c_all, conv_all, *ffn_a_weights = run_stage(
        StageGroup([GatherStage([pad_rows(c), pad_rows(conv_w[0])]), gather_stage("ffn_a")]), "gather_cond_ffn_a")
    keep_weights("ffn_a", ffn_a_weights)
    c_all = c_all.reshape(N_DEV, -1, d)[:, :nb].reshape(N_DEV * nb, d)
    conv_full = conv_all.reshape(N_DEV, GATHER_ROWS, -1)[:, :CONV_K].transpose(1, 0, 2).reshape(CONV_K, cw_n)
    ada_cols = w_ada.shape[2]
    b_cols = lax.dynamic_slice(b_ada, (0, me * ada_cols), (1, ada_cols))
    mod_cols = ada_forward(c_all, w_ada[0], b_cols)
    (mod_all,) = run_stage(GatherStage([mod_cols]), "gather_mod")
    mod_mine = lax.dynamic_slice(mod_all.reshape(N_DEV, N_DEV * nb, ada_cols), (0, me * nb, 0), (N_DEV, nb, ada_cols))
    mod = mod_mine.transpose(1, 0, 2).reshape(nb, N_MOD, d)

    disc_in = (a_re[0], a_im[0], b_re[0], b_im[0], log_dt[0])
    (abr, abi, bbr, bbi), disc_vjp = jax.vjp(ssm_discretise, *disc_in)
    bd, cd, abar8, ml, mlb = ssm_tables(abr, abi, bbr, bbi, c_re[0], c_im[0], _tile(s, 256) // ROWS)

    x0 = x.reshape(t, d)
    (x1, a1, b1, f1, h1), got = ffn_forward(x0, mod, g_ffn1, gw["w1_a"], gw["w3_a"], gw["w2_a"], 0, nb,
                                            "ffn_a_forward", carried=gather_stage("mixer"))
    keep_weights("mixer", got)
    (gb, gc, vv, u, sga, sgb, h2), got = mixer_proj_forward(x1, mod, g_mix, gw["w_in"], cw_n, sw_n, nb,
                                                            carried=gather_stage("ffn_b"))
    keep_weights("ffn_b", got)
    yssm, st = ssm_forward(u, bd, cd, abar8, ml, nb)
    x2, ya, yb, mix = mixer_merge_forward(x1, gb, gc, vv, sga, sgb, yssm, u, mod, conv_full, d_skip,
                                          gw["w_conv_out"], gw["w_glu"], gw["w_ssm_out"], gw["w_out"], nb)
    (x3, a3, b3, f3, h3), _ = ffn_forward(x2, mod, g_ffn2, gw["w1_b"], gw["w3_b"], gw["w2_b"], 2, nb,
                                          "ffn_b_forward")
    loss_parts, dx3, dg_final = loss_forward_backward(x3, loss_target.reshape(t, d), g_final.reshape(1, d))
    loss = lax.psum(jnp.sum(loss_parts[:, 0, 0]), ("x", "y", "c"))

    part, received = {}, {}

    def exchange_stage(ns):
        return ExchangeStage([_slots(part[n]) for n in ns])

    (da3, db3, sw3, df3, dgt3), _ = ffn_backward_hidden(dx3, a3, b3, f3, mod, gw["w2_b"], 2, nb,
                                                        "ffn_b_backward_hidden")
    (dx2, dmod3, dg_ffn2), _ = ffn_backward_input(dx3, x2, da3, db3, mod, g_ffn2, gw["w1_b"], gw["w3_b"],
                                                  2, nb, "ffn_b_backward_input")
    (part["w1_b"],), _ = nn_matmul(da3, h3, "grad_w1_b")
    (part["w3_b"],), _ = nn_matmul(db3, h3, "grad_w3_b")
    (part["w2_b"],), _ = nn_matmul(sw3, df3, "grad_w2_b")
    (dgla, dglb, dgb, dconv, ds0, dgt2, dd_skip, dw_out, dw_co, dw_so_t, dw_glu), got = mixer_merge_backward(
        dx2, mix, ya, yb, gb, gc, vv, sga, sgb, yssm, u, mod, conv_full, d_skip,
        gw["w_conv_out"], gw["w_glu"], gw["w_ssm_out"], gw["w_out"], nb, carried=exchange_stage(groups["ffn_b"]))
    received.update(zip(groups["ffn_b"], got))
    part["w_out"] = dw_out.astype(BF16)
    part["w_conv_out"] = dw_co.astype(BF16)
    part["w_ssm_out"] = dw_so_t.astype(BF16)
    part["w_glu"] = dw_glu.reshape(sw_n // glu_fold, d).astype(BF16)
    (du, dab, dbd, dcd), got = ssm_backward(u, ds0, st, bd, cd, abar8, mlb, d_skip, nb,
                                            carried=exchange_stage(groups["mixer"][1:]))
    received.update(zip(groups["mixer"][1:], got))
    dgc, dvv, dconv_w = conv_backward(dconv, gc, vv, conv_full, nb)
    part["w_in"] = jnp.concatenate([tn_matmul(p, h2, "grad_w_in_%d" % k)
                                    for k, p in enumerate((dgb, dgc, dvv, du, dgla, dglb))], axis=0)
    (dx1, dmod2, dg_mix), got = mixer_proj_backward(dgb, dgc, dvv, du, dgla, dglb, dx2, x1, mod, g_mix, gw["w_in"], nb,
                                                    carried=exchange_stage(("w_in",)))
    received["w_in"] = got[0]
    def pack(parts):
        flat = jnp.concatenate([a.reshape(-1) for a in parts.values()])
        rows = -(-flat.shape[0] // (128 * GATHER_ROWS)) * GATHER_ROWS
        return jnp.pad(flat, (0, rows * 128 - flat.shape[0])).reshape(rows, 128)

    def unpack(flat, parts):
        out, off = {}, 0
        for key, like in parts.items():
            n = math.prod(like.shape)
            out[key], off = flat[..., off:off + n].reshape(flat.shape[:-1] + like.shape), off + n
        return out

    dabr, dabi, dbbr, dbbi, dcr, dci = ssm_table_grads(dab, dbd, dcd, g_n, p_n, h_n)
    early = dict(gmod=jnp.concatenate([dmod2, dgt2, dmod3, dgt3], axis=1), g_mix=dg_mix, g_ffn2=dg_ffn2,
                 g_final=dg_final, d_skip=dd_skip, abr=dabr, abi=dabi, bbr=dbbr, bbi=dbbi, c_re=dcr, c_im=dci,
                 conv_w=dconv_w)
    (da1, db1, sw1, df1, dgt1), _ = ffn_backward_hidden(dx1, a1, b1, f1, mod, gw["w2_a"], 0, nb,
                                                        "ffn_a_backward_hidden")
    (part["w2_a"],), (early_all,) = nn_matmul(sw1, df1, "grad_w2_a", carried=GatherStage([pack(early)]))
    (part["w1_a"],), got = nn_matmul(da1, h1, "grad_w1_a", carried=exchange_stage(("w2_a",)))
    received["w2_a"] = got[0]
    (part["w3_a"],), got = nn_matmul(db1, h1, "grad_w3_a", carried=exchange_stage(("w1_a",)))
    received["w1_a"] = got[0]
    (dx0, dmod1, dg_ffn1), got = ffn_backward_input(dx1, x0, da1, db1, mod, g_ffn1, gw["w1_a"], gw["w3_a"],
                                                    0, nb, "ffn_a_backward_input", carried=exchange_stage(("w3_a",)))
    received["w3_a"] = got[0]
    late = dict(gmod=jnp.concatenate([dmod1, dgt1], axis=1), g_ffn1=dg_ffn1)
    (late_all,) = run_stage(GatherStage([pack(late)]), "gather_ffn_a_small_grads")

    tot, per_dev = {}, {}
    for parts, gathered, name in ((early, early_all, "early"), (late, late_all, "late")):
        slots = gathered.reshape(N_DEV, -1, 128)
        total = sum_slots(slots, "sum_small_grads_" + name).reshape(-1)
        for key, val in unpack(total, parts).items():
            tot[name + key if key == "gmod" else key] = val
        per_dev[name] = unpack(slots.reshape(N_DEV, -1), parts)["gmod"]
    gmod_all = jnp.concatenate([per_dev["late"], per_dev["early"]], axis=2).reshape(N_DEV * nb, N_MOD * d)
    gmod_tot = jnp.concatenate([tot["lategmod"], tot["earlygmod"]], axis=1)
    g_a_re, g_a_im, g_b_re, g_b_im, g_log_dt = disc_vjp((tot["abr"], tot["abi"], tot["bbr"], tot["bbi"]))

    grads = {}
    grads["b_ada"] = sum_rows(gmod_tot.reshape(nb, N_MOD * d))
    grads["g_ffn1"], grads["g_mix"], grads["g_ffn2"] = tot["g_ffn1"], tot["g_mix"], tot["g_ffn2"]
    grads["g_final"] = tot["g_final"].reshape(d)
    grads["d_skip"] = tot["d_skip"]
    grads["a_re"], grads["a_im"], grads["log_dt"] = g_a_re[None], g_a_im[None], g_log_dt[None]
    grads["b_re"], grads["b_im"] = g_b_re[None], g_b_im[None]
    grads["c_re"], grads["c_im"] = tot["c_re"][None], tot["c_im"][None]
    grads["conv_w"] = lax.dynamic_slice(tot["conv_w"], (0, me * conv_w.shape[2]), (CONV_K, conv_w.shape[2]))[None]

    delta, new_m, new_v = {}, {}, {}
    for name in big:
        wmv = (weights[name][0], mom1[name][0], mom2[name][0])
        if name in transposed:
            outs = sum_adamw_update(received[name], *[a.T for a in wmv], "adamw_" + name)
            gsum, dl, mm, vn = [a.T for a in outs]
        elif received[name].shape[1:] == weights[name].shape[1:]:
            gsum, dl, mm, vn = sum_adamw_update(received[name], *wmv, "adamw_" + name)
        else:
            gsum = sum_slots(received[name], "sum_" + name).reshape(weights[name].shape[1:])
            dl, mm, vn = adamw_update(*wmv[:1], gsum, *wmv[1:], "adamw_" + name)
        grads[name] = gsum[None]
        delta[name], new_m[name], new_v[name] = dl[None], mm[None], vn[None]

    gmod_cols = lax.dynamic_slice(gmod_all, (0, me * ada_cols), (N_DEV * nb, ada_cols))
    g_wada, d_wada, m_wada, v_wada = ada_backward_update(c_all, gmod_cols, w_ada[0], m_w_ada[0], v_w_ada[0])
    grads["w_ada"], delta["w_ada"], new_m["w_ada"], new_v["w_ada"] = g_wada[None], d_wada[None], m_wada[None], v_wada[None]

    small_names = [n for n in names if n not in big and n != "w_ada"]

    narrow = ("b_re", "b_im")

    def as2d(n, a):
        a = jnp.swapaxes(a.reshape(weights[n].shape), -1, -2) if n in narrow else a
        return a.reshape(-1, a.shape[-1])

    def from2d(n, a):
        shape = weights[n].shape
        return jnp.swapaxes(a.reshape(shape[:-2] + (shape[-1], shape[-2])), -1, -2) if n in narrow else a.reshape(shape)

    sw_, sg_, sm_, sv_ = ([as2d(n, src[n]) for n in small_names] for src in (weights, grads, mom1, mom2))
    sd, smo, svo = adamw_update_small(sw_, sg_, sm_, sv_)
    for n, dl, mm, vn in zip(small_names, sd, smo, svo):
        grads[n] = grads[n].reshape(weights[n].shape)
        delta[n], new_m[n], new_v[n] = from2d(n, dl), from2d(n, mm), from2d(n, vn)

    grad_x = dx0.reshape(nb, s, d)
    return (loss, grad_x, *[grads[n] for n in names], *[delta[n] for n in names],
            *[new_m[n] for n in names], *[new_v[n] for n in names])


def sum_rows(a):
    r, c = a.shape

    def body(a_ref, o_ref):
        acc = a_ref[0:1, :]
        for j in range(1, r):
            acc = acc + a_ref[j:j + 1, :]
        o_ref[...] = acc

    return pl.pallas_call(body, name="sum_rows", out_shape=jax.ShapeDtypeStruct((1, c), F32),
                          compiler_params=_params())(a)
```

```python
import functools
import math

import jax
import jax.numpy as jnp
from jax import lax
from jax.experimental import pallas as pl
from jax.experimental.pallas import tpu as pltpu

F32 = jnp.float32
BF16 = jnp.bfloat16
N_DEV = 8
N_MOD = 9
EPS = 1e-6
CONV_K = 3
ADAM_LR = 0.001
ADAM_B1 = 0.9
ADAM_B2 = 0.999
ADAM_EPS = 1e-08
ADAM_WD = 0.01
ADAM_STEP = 10
GELU_C0 = math.sqrt(2.0 / math.pi)
GELU_C1 = 0.044715
V7X_VMEM_LIMIT = 56 * 1024 * 1024
MESH_ID = pl.DeviceIdType.MESH
NT = (((1,), (1,)), ((), ()))
TN = (((0,), (0,)), ((), ()))


def _dot(a, b, dims=None):
    if dims is None:
        return jnp.dot(a, b, preferred_element_type=F32)
    return lax.dot_general(a, b, dims, preferred_element_type=F32)


def _params(sem=None, vmem=V7X_VMEM_LIMIT):
    return pltpu.CompilerParams(dimension_semantics=sem, vmem_limit_bytes=vmem)


def _full(shape):
    return pl.BlockSpec(shape, lambda *_: (0,) * len(shape))


def _const(shape):
    return pl.BlockSpec(shape, lambda *_: (0,) * len(shape), pipeline_mode=pl.Buffered(1))


def _tile(n, want):
    t = min(n, want)
    while n % t:
        t //= 2
    return t


class GatherStage:
    COPIES = 9

    def __init__(self, shards):
        n = len(shards)
        self.inputs = list(shards)
        self.out_shape = [jax.ShapeDtypeStruct((N_DEV * s.shape[0], s.shape[1]), s.dtype) for s in shards]
        self.scratch = [pltpu.SemaphoreType.DMA((self.COPIES * n,)), pltpu.SemaphoreType.DMA((self.COPIES * n,)),
                        pltpu.SemaphoreType.DMA((n,))]

    def _plan(self, ins, outs, sems):
        send_sems, recv_sems, local_sems = sems
        n = len(ins)
        x, y, c = lax.axis_index("x"), lax.axis_index("y"), lax.axis_index("c")
        me, sibling, xn, yn, dg = (x, y, c), (x, y, 1 - c), (1 - x, y, c), (x, 1 - y, c), (1 - x, 1 - y, c)

        def rows(k, block, half=None):
            r = ins[k].shape[0]
            px, py, pc = block
            base = (4 * px + 2 * py + pc) * r
            if half is None:
                return outs[k].at[pl.ds(base, r), :]
            return outs[k].at[pl.ds(base + half * (r // 2), r // 2), :]

        def copy(k, j, block, to, half=None, src=None):
            return pltpu.make_async_remote_copy(
                src_ref=rows(k, block, half) if src is None else src, dst_ref=rows(k, block, half),
                send_sem=send_sems.at[self.COPIES * k + j], recv_sem=recv_sems.at[self.COPIES * k + j],
                device_id=to, device_id_type=MESH_ID)

        sib = lambda b: (b[0], b[1], 1 - b[2])
        mine = [pltpu.make_async_copy(ins[k], rows(k, me), local_sems.at[k]) for k in range(n)]
        first = [(0, me, sibling, None, sibling), (1, me, xn, None, xn), (2, me, yn, None, yn)]
        second = [(3, xn, yn, 0, dg), (4, yn, xn, 1, dg), (5, xn, sibling, None, sib(xn)), (6, yn, sibling, None, sib(yn))]
        third = [(7, dg, sibling, 0, sib(dg)), (8, dg, sibling, 1, sib(dg))]
        return n, me, copy, mine, first, second, third

    def start(self, ins, outs, sems):
        n, me, copy, mine, first, _, _ = self._plan(ins, outs, sems)
        for cp in mine:
            cp.start()
        for k in range(n):
            for j, block, to, half, _ in first:
                copy(k, j, block, to, half, src=ins[k]).start()

    def advance(self, ins, outs, sems):
        n, me, copy, mine, first, second, third = self._plan(ins, outs, sems)
        for k in range(n):
            copy(k, 1, first[1][4], me).wait_recv()
            copy(k, 2, first[2][4], me).wait_recv()
            for j, block, to, half, _ in second:
                copy(k, j, block, to, half).start()

    def advance_again(self, ins, outs, sems):
        n, me, copy, mine, first, second, third = self._plan(ins, outs, sems)
        for k in range(n):
            copy(k, 3, second[0][4], me, 0).wait_recv()
            copy(k, 4, second[1][4], me, 1).wait_recv()
            for j, block, to, half, _ in third:
                copy(k, j, block, to, half).start()

    def finish(self, ins, outs, sems):
        n, me, copy, mine, first, second, third = self._plan(ins, outs, sems)
        arrived = lambda k, j, block, half: copy(k, j, block, me, half).wait_recv()
        for k in range(n):
            arrived(k, 0, first[0][4], None)
            arrived(k, 5, second[2][4], None)
            arrived(k, 6, second[3][4], None)
            arrived(k, 7, third[0][4], 0)
            arrived(k, 8, third[1][4], 1)
        for k in range(n):
            for j, block, to, half, _ in first:
                copy(k, j, block, to, half, src=ins[k]).wait_send()
            for j, block, to, half, _ in second + third:
                copy(k, j, block, to, half).wait_send()
        for cp in mine:
            cp.wait()


class ExchangeStage:
    def __init__(self, bufs):
        n = len(bufs)
        self.inputs = list(bufs)
        self.out_shape = [jax.ShapeDtypeStruct(b.shape, b.dtype) for b in bufs]
        self.scratch = [pltpu.SemaphoreType.DMA((7 * n,)), pltpu.SemaphoreType.DMA((7 * n,)),
                        pltpu.SemaphoreType.DMA((n,))]

    def _plan(self, ins, outs, sems):
        send_sems, recv_sems, local_sems = sems
        n = len(ins)
        x, y, c = lax.axis_index("x"), lax.axis_index("y"), lax.axis_index("c")
        me = 4 * x + 2 * y + c
        mine = [pltpu.make_async_copy(ins[k].at[me], outs[k].at[me], local_sems.at[k]) for k in range(n)]
        copies = []
        for mask in range(1, N_DEV):
            px, py, pc = x ^ (mask >> 2), y ^ ((mask >> 1) & 1), c ^ (mask & 1)
            for k in range(n):
                copies.append(pltpu.make_async_remote_copy(
                    src_ref=ins[k].at[4 * px + 2 * py + pc], dst_ref=outs[k].at[me],
                    send_sem=send_sems.at[7 * k + mask - 1], recv_sem=recv_sems.at[7 * k + mask - 1],
                    device_id=(px, py, pc), device_id_type=MESH_ID))
        return mine, copies

    def start(self, ins, outs, sems):
        mine, copies = self._plan(ins, outs, sems)
        for cp in mine + copies:
            cp.start()

    def advance(self, ins, outs, sems):
        pass

    advance_again = advance

    def finish(self, ins, outs, sems):
        mine, copies = self._plan(ins, outs, sems)
        for cp in copies:
            cp.wait_recv()
        for cp in copies:
            cp.wait_send()
        for cp in mine:
            cp.wait()


ANY_SPEC = pl.BlockSpec(memory_space=pl.ANY)
GATHER_ROWS = 16


class StageGroup:
    def __init__(self, stages):
        self.stages = list(stages)
        self.inputs = [a for s in stages for a in s.inputs]
        self.out_shape = [o for s in stages for o in s.out_shape]
        self.scratch = [t for s in stages for t in s.scratch]

    def _parts(self, ins, outs, sems):
        i = o = t = 0
        for s in self.stages:
            ni, no, nt = len(s.inputs), len(s.out_shape), len(s.scratch)
            yield s, ins[i:i + ni], outs[o:o + no], sems[t:t + nt]
            i, o, t = i + ni, o + no, t + nt

    def start(self, ins, outs, sems):
        for s, i_, o_, t_ in self._parts(ins, outs, sems):
            s.start(i_, o_, t_)

    def advance(self, ins, outs, sems):
        for s, i_, o_, t_ in self._parts(ins, outs, sems):
            s.advance(i_, o_, t_)

    def advance_again(self, ins, outs, sems):
        for s, i_, o_, t_ in self._parts(ins, outs, sems):
            s.advance_again(i_, o_, t_)

    def finish(self, ins, outs, sems):
        for s, i_, o_, t_ in self._parts(ins, outs, sems):
            s.finish(i_, o_, t_)


def run_stage(stage, name):
    ci, co = len(stage.inputs), len(stage.out_shape)

    def body(*refs):
        ins, outs, sems = refs[:ci], refs[ci:ci + co], refs[ci + co:]
        stage.start(ins, outs, sems)
        stage.advance(ins, outs, sems)
        stage.advance_again(ins, outs, sems)
        stage.finish(ins, outs, sems)

    return pl.pallas_call(body, name=name, out_shape=stage.out_shape, in_specs=[ANY_SPEC] * ci,
                          out_specs=[ANY_SPEC] * co, scratch_shapes=stage.scratch)(*stage.inputs)


def _call(body, *, name, grid, in_specs, out_specs, out_shape, args, scratch_shapes=(), carried=None):
    sem = ("arbitrary",) * len(grid)
    if carried is None:
        return pl.pallas_call(body, name=name, grid=grid, in_specs=list(in_specs), out_specs=list(out_specs),
                              out_shape=list(out_shape), scratch_shapes=list(scratch_shapes),
                              compiler_params=_params(sem))(*args), None
    ni, no, ns = len(in_specs), len(out_shape), len(scratch_shapes)
    ci, co = len(carried.inputs), len(carried.out_shape)
    n_steps = math.prod(grid)

    def wrapped(*refs):
        ins, refs = refs[:ni], refs[ni:]
        cins, refs = refs[:ci], refs[ci:]
        outs, refs = refs[:no], refs[no:]
        couts, refs = refs[:co], refs[co:]
        scr, csems = refs[:ns], refs[ns:]
        step = functools.reduce(lambda acc, ig: acc * ig[1] + ig[0],
                                zip([pl.program_id(k) for k in range(len(grid))], grid), 0)

        @pl.when(step == 0)
        def _():
            carried.start(cins, couts, csems)

        @pl.when(step == n_steps // 2)
        def _():
            carried.advance(cins, couts, csems)

        @pl.when(step == (3 * n_steps) // 4)
        def _():
            carried.advance_again(cins, couts, csems)

        body(*ins, *outs, *scr)

        @pl.when(step == n_steps - 1)
        def _():
            carried.finish(cins, couts, csems)

    res = pl.pallas_call(
        wrapped, name=name, grid=grid, in_specs=list(in_specs) + [ANY_SPEC] * ci,
        out_specs=list(out_specs) + [ANY_SPEC] * co, out_shape=list(out_shape) + carried.out_shape,
        scratch_shapes=list(scratch_shapes) + carried.scratch, compiler_params=_params(sem),
    )(*args, *carried.inputs)
    return res[:no], res[no:]


def _norm_mod(x, g, shift, scale):
    r = lax.rsqrt(jnp.mean(x * x, axis=-1, keepdims=True) + EPS)
    n = x * r
    return (n * g) * (1.0 + scale) + shift, n, r


def _norm_mod_bwd(dh, n, r, g, scale):
    dsh = jnp.sum(dh, axis=0, keepdims=True)
    dsc = jnp.sum(dh * (n * g), axis=0, keepdims=True)
    dg = jnp.sum(dh * (1.0 + scale) * n, axis=0, keepdims=True)
    dn = dh * ((1.0 + scale) * g)
    dx = r * (dn - n * jnp.mean(n * dn, axis=-1, keepdims=True))
    return dx, dsh, dsc, dg


def _mod_rows(mod_ref, sub):
    m = mod_ref[0]
    return m[3 * sub:3 * sub + 1], m[3 * sub + 1:3 * sub + 2], m[3 * sub + 2:3 * sub + 3]


def _gelu(x):
    t = jnp.tanh(GELU_C0 * (x + GELU_C1 * x * x * x))
    return 0.5 * x * (1.0 + t), t


def _gelu_grad(x, t):
    return 0.5 * (1.0 + t) + 0.5 * x * (1.0 - t * t) * (GELU_C0 * (1.0 + 3.0 * GELU_C1 * x * x))


def _zero_when(cond, *refs):
    @pl.when(cond)
    def _():
        for r in refs:
            r[...] = jnp.zeros_like(r)


def ada_forward(c_all, w_ada, b_ada_cols):
    def body(c_ref, w_ref, b_ref, o_ref):
        c = c_ref[...]
        cond = (c * jax.nn.sigmoid(c)).astype(BF16)
        o_ref[...] = _dot(cond, w_ref[...].astype(BF16)) + b_ref[...]

    nb, d = c_all.shape
    cols = w_ada.shape[1]
    tn = _tile(cols, 384)
    return pl.pallas_call(
        body, name="ada_forward", grid=(cols // tn,),
        out_shape=jax.ShapeDtypeStruct((nb, cols), F32),
        in_specs=[_full((nb, d)), pl.BlockSpec((d, tn), lambda j: (0, j)), pl.BlockSpec((1, tn), lambda j: (0, j))],
        out_specs=pl.BlockSpec((nb, tn), lambda j: (0, j)),
        compiler_params=_params(("arbitrary",)),
    )(c_all, w_ada, b_ada_cols)


def _adamw(w, g, m, v):
    m = ADAM_B1 * m + (1.0 - ADAM_B1) * g
    v = ADAM_B2 * v + (1.0 - ADAM_B2) * (g * g)
    m_hat = m / (1.0 - ADAM_B1 ** ADAM_STEP)
    v_hat = v / (1.0 - ADAM_B2 ** ADAM_STEP)
    delta = -ADAM_LR * (m_hat / (jnp.sqrt(v_hat) + ADAM_EPS) + ADAM_WD * w)
    return delta, m, v


def ada_backward_update(c_all, gmod_cols, w, m, v):
    def body(c_ref, g_ref, w_ref, m_ref, v_ref, go_ref, d_ref, mo_ref, vo_ref):
        c = c_ref[...]
        cond = (c * jax.nn.sigmoid(c)).astype(BF16)
        g = _dot(cond, g_ref[...].astype(BF16), TN)
        go_ref[...] = g
        d_ref[...], mo_ref[...], vo_ref[...] = _adamw(w_ref[...], g, m_ref[...], v_ref[...])

    nb, d = c_all.shape
    cols = w.shape[1]
    tn = _tile(cols, 128)
    col = pl.BlockSpec((d, tn), lambda j: (0, j))
    return pl.pallas_call(
        body, name="ada_backward_update", grid=(cols // tn,),
        out_shape=[jax.ShapeDtypeStruct(w.shape, F32)] * 4,
        in_specs=[_full((nb, d)), pl.BlockSpec((nb, tn), lambda j: (0, j)), col, col, col],
        out_specs=[col] * 4,
        compiler_params=_params(("arbitrary",)),
    )(c_all, gmod_cols, w, m, v)


def _row_spec(tm, width, tiles_per_seq):
    return pl.BlockSpec((tm, width), lambda b, i: (b * tiles_per_seq + i, 0))


def _mod_spec(d):
    return pl.BlockSpec((1, N_MOD, d), lambda b, i: (b, 0, 0))


def _col_spec(rows, tm, tiles_per_seq):
    return pl.BlockSpec((rows, tm), lambda b, i: (0, b * tiles_per_seq + i))


def _ffn_chunk(f):
    return f // 2 if f % 256 == 0 and f > 1536 else f


def ffn_forward(x, mod, g, w1t, w3t, w2, sub, nb, name, carried=None):
    t, d = x.shape
    f = w1t.shape[0]
    s = t // nb
    tm = _tile(s, 512)
    fc = _ffn_chunk(f)

    def body(x_ref, mod_ref, g_ref, w1_ref, w3_ref, w2_ref, xo_ref, sl_ref, dsl_ref, b_ref, f_ref, h_ref):
        xv = x_ref[...]
        sh, sc, gt = _mod_rows(mod_ref, sub)
        h, _, _ = _norm_mod(xv, g_ref[...], sh, sc)
        hb = h.astype(BF16)
        h_ref[...] = hb
        acc_t = jnp.zeros((d, tm), F32)
        for k in range(f // fc):
            rows = slice(k * fc, (k + 1) * fc)
            a = _dot(w1_ref[rows, :], hb, NT)
            b = _dot(w3_ref[rows, :], hb, NT)
            sig = jax.nn.sigmoid(a)
            sl = a * sig
            sl_ref[rows, :] = sl.astype(BF16)
            dsl_ref[rows, :] = (sig + sl * (1.0 - sig)).astype(BF16)
            b_ref[rows, :] = b.astype(BF16)
            acc_t = acc_t + _dot(w2_ref[rows, :], (sl * b).astype(BF16), TN)
        acc = acc_t.T
        f_ref[...] = acc.astype(BF16)
        xo_ref[...] = xv + (0.5 * gt) * acc

    tps = s // tm
    rd, cf = _row_spec(tm, d, tps), _col_spec(f, tm, tps)
    return _call(
        body, name=name, grid=(nb, tps), carried=carried,
        out_shape=[jax.ShapeDtypeStruct((t, d), F32)] + [jax.ShapeDtypeStruct((f, t), BF16)] * 3
        + [jax.ShapeDtypeStruct((t, d), BF16)] * 2,
        in_specs=[rd, _mod_spec(d), _const((1, d)), _const((f, d)), _const((f, d)), _const((f, d))],
        out_specs=[rd, cf, cf, cf, rd, rd],
        args=(x, mod, g, w1t, w3t, w2))


def ffn_backward_hidden(dxo, sl_t, dsl_t, b_t, fo, mod, w2, sub, nb, name, carried=None):
    t, d = dxo.shape
    f = w2.shape[0]
    s = t // nb
    tm = _tile(s, 512)
    fc = _tile(f, 704) if f % 704 == 0 else _tile(f, 512)

    def body(dxo_ref, sl_ref, dsl_ref, b_ref, f_ref, mod_ref, w2_ref, da_ref, db_ref, df_ref, dgt_ref):
        _zero_when(pl.program_id(1) == 0, dgt_ref)
        dxo = dxo_ref[...]
        _, _, gt = _mod_rows(mod_ref, sub)
        dfb = ((0.5 * gt) * dxo).astype(BF16)
        df_ref[...] = dfb
        dgt_ref[...] += 0.5 * jnp.sum(dxo * f_ref[...].astype(F32), axis=0, keepdims=True)[None]
        for k in range(f // fc):
            rows = slice(k * fc, (k + 1) * fc)
            ds = _dot(w2_ref[rows, :], dfb, NT).astype(BF16)
            da_ref[rows, :] = ds * b_ref[rows, :] * dsl_ref[rows, :]
            db_ref[rows, :] = ds * sl_ref[rows, :]

    tps = s // tm
    rd, cf = _row_spec(tm, d, tps), _col_spec(f, tm, tps)
    return _call(
        body, name=name, grid=(nb, tps), carried=carried,
        out_shape=[jax.ShapeDtypeStruct((f, t), BF16)] * 2
        + [jax.ShapeDtypeStruct((t, d), BF16), jax.ShapeDtypeStruct((nb, 1, d), F32)],
        in_specs=[rd, cf, cf, cf, rd, _mod_spec(d), _const((f, d))],
        out_specs=[cf, cf, rd, pl.BlockSpec((1, 1, d), lambda b, i: (b, 0, 0))],
        args=(dxo, sl_t, dsl_t, b_t, fo, mod, w2))


def ffn_backward_input(dxo, x, da_t, db_t, mod, g, w1t, w3t, sub, nb, name, carried=None):
    t, d = x.shape
    f = w1t.shape[0]
    s = t // nb
    tm = _tile(s, 512)

    def body(dxo_ref, x_ref, da_ref, db_ref, mod_ref, g_ref, w1_ref, w3_ref, dx_ref, dmod_ref, dg_ref):
        bi, i = pl.program_id(0), pl.program_id(1)
        _zero_when(i == 0, dmod_ref)
        _zero_when(jnp.logical_and(bi == 0, i == 0), dg_ref)
        sh, sc, _ = _mod_rows(mod_ref, sub)
        gv = g_ref[...]
        _, n, r = _norm_mod(x_ref[...], gv, sh, sc)
        dh_t = _dot(w1_ref[...], da_ref[...], TN) + _dot(w3_ref[...], db_ref[...], TN)
        dxn, dsh, dsc, dg = _norm_mod_bwd(dh_t.T, n, r, gv, sc)
        dx_ref[...] = dxo_ref[...] + dxn
        dmod_ref[...] += jnp.concatenate([dsh, dsc], axis=0)[None]
        dg_ref[...] += dg

    tps = s // tm
    rd, cf = _row_spec(tm, d, tps), _col_spec(f, tm, tps)
    return _call(
        body, name=name, grid=(nb, tps), carried=carried,
        out_shape=[jax.ShapeDtypeStruct((t, d), F32), jax.ShapeDtypeStruct((nb, 2, d), F32),
                   jax.ShapeDtypeStruct((1, d), F32)],
        in_specs=[rd, rd, cf, cf, _mod_spec(d), _const((1, d)), _const((f, d)), _const((f, d))],
        out_specs=[rd, pl.BlockSpec((1, 2, d), lambda b, i: (b, 0, 0)), _full((1, d))],
        args=(dxo, x, da_t, db_t, mod, g, w1t, w3t))


def nn_matmul(lhs_t, rhs, name, carried=None, lhs2_t=None):
    m, t = lhs_t.shape
    n = rhs.shape[1]
    tk = _tile(t, 2048)
    tmm = m if m <= 1536 else m // 2
    nk = t // tk
    n_lhs = 1 if lhs2_t is None else 2

    def body(*refs):
        b_ref, o_ref, acc_ref = refs[n_lhs:]
        lhs = refs[0][...] if n_lhs == 1 else refs[0][...] * refs[1][...]
        k = pl.program_id(1)
        _zero_when(k == 0, acc_ref)
        acc_ref[...] += _dot(lhs, b_ref[...])

        @pl.when(k == nk - 1)
        def _():
            o_ref[...] = acc_ref[...].astype(BF16)

    lspec = pl.BlockSpec((tmm, tk), lambda j, k: (j, k))
    return _call(
        body, name=name, grid=(m // tmm, nk), carried=carried,
        out_shape=[jax.ShapeDtypeStruct((m, n), BF16)],
        in_specs=[lspec] * n_lhs + [pl.BlockSpec((tk, n), lambda j, k: (k, 0))],
        out_specs=[pl.BlockSpec((tmm, n), lambda j, k: (j, 0))],
        scratch_shapes=[pltpu.VMEM((tmm, n), F32)],
        args=(lhs_t, rhs) if lhs2_t is None else (lhs_t, lhs2_t, rhs))


def tn_matmul(lhs, rhs, name):
    t, m = lhs.shape
    n = rhs.shape[1]
    tk = _tile(t, 2048)
    nk = t // tk

    def body(a_ref, b_ref, o_ref, acc_ref):
        k = pl.program_id(0)
        _zero_when(k == 0, acc_ref)
        acc_ref[...] += _dot(a_ref[...], b_ref[...], TN)

        @pl.when(k == nk - 1)
        def _():
            o_ref[...] = acc_ref[...].astype(BF16)

    return pl.pallas_call(
        body, name=name, grid=(nk,),
        out_shape=jax.ShapeDtypeStruct((m, n), BF16),
        in_specs=[pl.BlockSpec((tk, m), lambda k: (k, 0)), pl.BlockSpec((tk, n), lambda k: (k, 0))],
        out_specs=pl.BlockSpec((m, n), lambda k: (0, 0)),
        scratch_shapes=[pltpu.VMEM((m, n), F32)],
        compiler_params=_params(("arbitrary",)),
    )(lhs, rhs)


def mixer_proj_forward(x, mod, g, w_in_t, cw, sw, nb, carried=None):
    t, d = x.shape
    s = t // nb
    tm = _tile(s, 512)
    pieces = [(0, cw, "bf16"), (cw, cw, "bf16"), (2 * cw, cw, "bf16"), (3 * cw, sw, "f32"),
              (3 * cw + sw, d, "sig"), (3 * cw + sw + d, d, "sig")]

    def body(x_ref, mod_ref, g_ref, w_ref, *outs):
        h_ref = outs[-1]
        sh, sc, _ = _mod_rows(mod_ref, 1)
        h, _, _ = _norm_mod(x_ref[...], g_ref[...], sh, sc)
        hb = h.astype(BF16)
        h_ref[...] = hb
        for (off, width, kind), o_ref in zip(pieces, outs[:-1]):
            ck = _tile(width, 512)
            for j in range(width // ck):
                p = _dot(hb, w_ref[off + j * ck:off + (j + 1) * ck, :], NT)
                if kind == "sig":
                    p = jax.nn.sigmoid(p)
                o_ref[:, j * ck:(j + 1) * ck] = p.astype(o_ref.dtype)

    tps = s // tm
    widths = [(cw, BF16), (cw, BF16), (cw, BF16), (sw, F32), (d, BF16), (d, BF16), (d, BF16)]
    return _call(
        body, name="mixer_proj_forward", grid=(nb, tps), carried=carried,
        out_shape=[jax.ShapeDtypeStruct((t, w), dt) for w, dt in widths],
        in_specs=[_row_spec(tm, d, tps), _mod_spec(d), _const((1, d)), _const(w_in_t.shape)],
        out_specs=[_row_spec(tm, w, tps) for w, _ in widths],
        args=(x, mod, g, w_in_t))


def mixer_proj_backward(dgb, dgc, dv, du, dgla, dglb, dxo, x, mod, g, w_in_t, nb, carried=None):
    t, d = x.shape
    s = t // nb
    tm = _tile(s, 512)
    parts = [dgb, dgc, dv, du, dgla, dglb]
    offs = [0]
    for p in parts:
        offs.append(offs[-1] + p.shape[1])

    def body(*refs):
        p_refs = refs[:6]
        dxo_ref, x_ref, mod_ref, g_ref, w_ref, dx_ref, dmod_ref, dg_ref = refs[6:]
        bi, i = pl.program_id(0), pl.program_id(1)
        _zero_when(i == 0, dmod_ref)
        _zero_when(jnp.logical_and(bi == 0, i == 0), dg_ref)
        dh = jnp.zeros((tm, d), F32)
        for p_ref, off in zip(p_refs, offs):
            width = p_ref.shape[1]
            ck = _tile(width, 512)
            for j in range(width // ck):
                dh = dh + _dot(p_ref[:, j * ck:(j + 1) * ck], w_ref[off + j * ck:off + (j + 1) * ck, :])
        sh, sc, _ = _mod_rows(mod_ref, 1)
        gv = g_ref[...]
        _, n, r = _norm_mod(x_ref[...], gv, sh, sc)
        dxn, dsh, dsc, dg = _norm_mod_bwd(dh, n, r, gv, sc)
        dx_ref[...] = dxo_ref[...] + dxn
        dmod_ref[...] += jnp.concatenate([dsh, dsc], axis=0)[None]
        dg_ref[...] += dg

    tps = s // tm
    rd = _row_spec(tm, d, tps)
    return _call(
        body, name="mixer_proj_backward", grid=(nb, tps), carried=carried,
        out_shape=[jax.ShapeDtypeStruct((t, d), F32), jax.ShapeDtypeStruct((nb, 2, d), F32),
                   jax.ShapeDtypeStruct((1, d), F32)],
        in_specs=[_row_spec(tm, p.shape[1], tps) for p in parts]
        + [rd, rd, _mod_spec(d), _const((1, d)), _const(w_in_t.shape)],
        out_specs=[rd, pl.BlockSpec((1, 2, d), lambda b, i: (b, 0, 0)), _full((1, d))],
        args=(*parts, dxo, x, mod, g, w_in_t))


GROUPS_PER_BLOCK = 8
ROWS = 8
SCAN_LANES = 512


def _scan_rows(xr, xi, masks, shifts):
    for (mr, mi), sft in zip(masks, shifts):
        sr, si = pltpu.roll(xr, sft, 0), pltpu.roll(xi, sft, 0)
        xr, xi = xr + mr * sr - mi * si, xi + mr * si + mi * sr
    return xr, xi


def _cmul_add(ar, ai, cr, ci, br, bi):
    return ar * cr - ai * ci + br, ar * ci + ai * cr + bi


def _segment_rows(perm_ref, x):
    return _dot(perm_ref[0], x).astype(BF16)


def _time_rows(perm_ref, x):
    hi = x.astype(BF16)
    lo = (x - hi.astype(F32)).astype(BF16)
    return _dot(perm_ref[1], hi) + _dot(perm_ref[1], lo)


def segment_permutation(tc):
    r = jnp.arange(tc)
    p = (r[:, None] % ROWS * (tc // ROWS) + r[:, None] // ROWS == r[None, :]).astype(BF16)
    return jnp.stack([p, p.T])


def _rows_at(k, offset=0):
    return pl.ds(pl.multiple_of(k * ROWS + offset, ROWS), ROWS)


def ssm_forward(u, bd, cd, a1, ml, nb):
    t, w = u.shape
    s = t // nb
    tc = _tile(s, 256)
    seg = tc // ROWS
    nq, ub, lq = bd.shape[1], bd.shape[2], bd.shape[3]
    nl = nq * lq
    nch = s // tc
    lw = min(nl, SCAN_LANES)

    def body(u_ref, perm_ref, bd_ref, cd_ref, a1_ref, ml_ref, y_ref, st_ref, xr_s, xi_s, car_s):
        i = pl.program_id(1)

        @pl.when(i == 0)
        def _():
            car_s[...] = jnp.zeros_like(car_s)

        ub16 = _segment_rows(perm_ref, u_ref[...].astype(BF16))
        for q in range(nq):
            lanes = slice(q * lq, (q + 1) * lq)
            uq = ub16[:, q * ub:(q + 1) * ub]
            xr_s[:, lanes] = _dot(uq, bd_ref[0, q])
            xi_s[:, lanes] = _dot(uq, bd_ref[1, q])
        row_is_0 = lax.broadcasted_iota(jnp.int32, (ROWS, lw), 0) == 0
        zero = jnp.zeros((ROWS, lw), F32)
        for j in range(nl // lw):
            lanes = slice(j * lw, (j + 1) * lw)
            ar, ai = a1_ref[0, :, lanes], a1_ref[1, :, lanes]

            def local(k, c):
                return _cmul_add(ar, ai, c[0], c[1], xr_s[_rows_at(k), lanes], xi_s[_rows_at(k), lanes])

            er, ei = lax.fori_loop(0, seg, local, (zero, zero))
            masks = [(ml_ref[d, 0, :, lanes], ml_ref[d, 1, :, lanes]) for d in range(3)]
            cr, ci = _scan_rows(jnp.where(row_is_0, car_s[0, :, lanes], pltpu.roll(er, 1, 0)),
                                jnp.where(row_is_0, car_s[1, :, lanes], pltpu.roll(ei, 1, 0)), masks, (1, 2, 4))
            st_ref[0, 0, :, lanes] = cr
            st_ref[0, 1, :, lanes] = ci

            def full(k, c):
                xr, xi = _cmul_add(ar, ai, c[0], c[1], xr_s[_rows_at(k), lanes], xi_s[_rows_at(k), lanes])
                xr_s[_rows_at(k), lanes] = xr
                xi_s[_rows_at(k), lanes] = xi
                return xr, xi

            fr, fi = lax.fori_loop(0, seg, full, (cr, ci))
            car_s[0, :, lanes] = jnp.broadcast_to(fr[ROWS - 1:ROWS], fr.shape)
            car_s[1, :, lanes] = jnp.broadcast_to(fi[ROWS - 1:ROWS], fi.shape)
        y = jnp.concatenate([_dot(xr_s[:, q * lq:(q + 1) * lq].astype(BF16), cd_ref[0, q])
                             + _dot(xi_s[:, q * lq:(q + 1) * lq].astype(BF16), cd_ref[1, q]) for q in range(nq)],
                            axis=1)
        y_ref[...] = _time_rows(perm_ref, y)

    perm = segment_permutation(tc)
    return pl.pallas_call(
        body, name="ssm_forward", grid=(nb, nch),
        out_shape=[jax.ShapeDtypeStruct((t, w), F32), jax.ShapeDtypeStruct((nb * nch, 2, ROWS, nl), F32)],
        in_specs=[pl.BlockSpec((tc, w), lambda b, i: (b * nch + i, 0)), _const(perm.shape), _const(bd.shape),
                  _const(cd.shape), _const(a1.shape), _const(ml.shape)],
        out_specs=[pl.BlockSpec((tc, w), lambda b, i: (b * nch + i, 0)),
                   pl.BlockSpec((1, 2, ROWS, nl), lambda b, i: (b * nch + i, 0, 0, 0))],
        scratch_shapes=[pltpu.VMEM((tc, nl), F32), pltpu.VMEM((tc, nl), F32), pltpu.VMEM((2, ROWS, nl), F32)],
        compiler_params=_params(("arbitrary", "arbitrary")),
    )(u, perm, bd, cd, a1, ml)


def ssm_backward(u, dy, st, bd, cd, a1, mlb, dskip, nb, carried=None):
    t, w = u.shape
    s = t // nb
    tc = _tile(s, 256)
    seg = tc // ROWS
    nq, ub, lq = bd.shape[1], bd.shape[2], bd.shape[3]
    nl = nq * lq
    nch = s // tc
    lw = min(nl, SCAN_LANES)

    def body(u_ref, dy_ref, st_ref, perm_ref, bd_ref, cd_ref, a1_ref, mlb_ref, dsk_ref,
             du_ref, dab_ref, dbd_ref, dcd_ref, xr_s, xi_s, lr_s, li_s, car_s):
        bi, i = pl.program_id(0), pl.program_id(1)
        first = jnp.logical_and(bi == 0, i == 0)

        @pl.when(i == 0)
        def _():
            car_s[...] = jnp.zeros_like(car_s)

        @pl.when(first)
        def _():
            dab_ref[...] = jnp.zeros_like(dab_ref)
            dbd_ref[...] = jnp.zeros_like(dbd_ref)
            dcd_ref[...] = jnp.zeros_like(dcd_ref)

        ub16 = _segment_rows(perm_ref, u_ref[...].astype(BF16))
        dyb16 = _segment_rows(perm_ref, dy_ref[...].astype(BF16))
        xr_s[0:ROWS, :] = st_ref[0, 0]
        xi_s[0:ROWS, :] = st_ref[0, 1]
        for q in range(nq):
            lanes = slice(q * lq, (q + 1) * lq)
            uq = ub16[:, q * ub:(q + 1) * ub]
            dq = dyb16[:, q * ub:(q + 1) * ub]
            xr_s[ROWS:, lanes] = _dot(uq, bd_ref[0, q])
            xi_s[ROWS:, lanes] = _dot(uq, bd_ref[1, q])
            lr_s[:, lanes] = _dot(dq, cd_ref[0, q], NT)
            li_s[:, lanes] = _dot(dq, cd_ref[1, q], NT)
        row_is_7 = lax.broadcasted_iota(jnp.int32, (ROWS, lw), 0) == ROWS - 1
        zero = jnp.zeros((ROWS, lw), F32)
        for j in range(nl // lw):
            lanes = slice(j * lw, (j + 1) * lw)
            ar, ai = a1_ref[0, :, lanes], a1_ref[1, :, lanes]
            nai = -ai

            def states(k, c):
                xr, xi = _cmul_add(ar, ai, c[0], c[1], xr_s[_rows_at(k, ROWS), lanes], xi_s[_rows_at(k, ROWS), lanes])
                xr_s[_rows_at(k, ROWS), lanes] = xr
                xi_s[_rows_at(k, ROWS), lanes] = xi
                return xr, xi

            lax.fori_loop(0, seg, states, (st_ref[0, 0, :, lanes], st_ref[0, 1, :, lanes]))

            def local(kk, c):
                k = seg - 1 - kk
                return _cmul_add(ar, nai, c[0], c[1], lr_s[_rows_at(k), lanes], li_s[_rows_at(k), lanes])

            er, ei = lax.fori_loop(0, seg, local, (zero, zero))
            masks = [(mlb_ref[d, 0, :, lanes], mlb_ref[d, 1, :, lanes]) for d in range(3)]
            cr, ci = _scan_rows(jnp.where(row_is_7, car_s[0, :, lanes], pltpu.roll(er, ROWS - 1, 0)),
                                jnp.where(row_is_7, car_s[1, :, lanes], pltpu.roll(ei, ROWS - 1, 0)), masks, (7, 6, 4))

            def full(kk, c):
                cr_, ci_, accr, acci = c
                k = seg - 1 - kk
                lr, li = _cmul_add(ar, nai, cr_, ci_, lr_s[_rows_at(k), lanes], li_s[_rows_at(k), lanes])
                lr_s[_rows_at(k), lanes] = lr
                li_s[_rows_at(k), lanes] = li
                xpr, xpi = xr_s[_rows_at(k), lanes], xi_s[_rows_at(k), lanes]
                return lr, li, accr + lr * xpr + li * xpi, acci + li * xpr - lr * xpi

            lr0, li0, accr, acci = lax.fori_loop(0, seg, full, (cr, ci, zero, zero))
            car_s[0, :, lanes] = jnp.broadcast_to(lr0[0:1], lr0.shape)
            car_s[1, :, lanes] = jnp.broadcast_to(li0[0:1], li0.shape)
            dab_ref[0, :, lanes] += accr
            dab_ref[1, :, lanes] += acci
        du_parts = []
        for q in range(nq):
            lanes = slice(q * lq, (q + 1) * lq)
            cols = slice(q * ub, (q + 1) * ub)
            lrb, lib = lr_s[:, lanes].astype(BF16), li_s[:, lanes].astype(BF16)
            uq, dq = ub16[:, cols], dyb16[:, cols]
            du_parts.append(_dot(lrb, bd_ref[0, q], NT) + _dot(lib, bd_ref[1, q], NT))
            dbd_ref[0, q] += _dot(uq, lrb, TN)
            dbd_ref[1, q] += _dot(uq, lib, TN)
            dcd_ref[0, q] += _dot(xr_s[ROWS:, lanes].astype(BF16), dq, TN)
            dcd_ref[1, q] += _dot(xi_s[ROWS:, lanes].astype(BF16), dq, TN)
        du = _time_rows(perm_ref, jnp.concatenate(du_parts, axis=1)) + dsk_ref[...] * dy_ref[...]
        du_ref[...] = du.astype(BF16)

    rev = lambda b, i: (b * nch + nch - 1 - i, 0)
    perm = segment_permutation(tc)
    return _call(
        body, name="ssm_backward", grid=(nb, nch), carried=carried,
        out_shape=[jax.ShapeDtypeStruct((t, w), BF16), jax.ShapeDtypeStruct((2, ROWS, nl), F32),
                   jax.ShapeDtypeStruct(bd.shape, F32), jax.ShapeDtypeStruct(cd.shape, F32)],
        in_specs=[pl.BlockSpec((tc, w), rev), pl.BlockSpec((tc, w), rev),
                  pl.BlockSpec((1, 2, ROWS, nl), lambda b, i: (b * nch + nch - 1 - i, 0, 0, 0)),
                  _const(perm.shape), _const(bd.shape), _const(cd.shape), _const(a1.shape), _const(mlb.shape),
                  _const((1, w))],
        out_specs=[pl.BlockSpec((tc, w), rev), _full((2, ROWS, nl)), _full(bd.shape), _full(cd.shape)],
        scratch_shapes=[pltpu.VMEM((tc + ROWS, nl), F32), pltpu.VMEM((tc + ROWS, nl), F32),
                        pltpu.VMEM((tc, nl), F32), pltpu.VMEM((tc, nl), F32), pltpu.VMEM((2, ROWS, nl), F32)],
        args=(u, dy, st, perm, bd, cd, a1, mlb, dskip))


def ssm_discretise(a_re, a_im, b_re, b_im, log_dt):
    dt = jnp.exp(log_dt)[:, None]
    er = jnp.exp(a_re * dt)
    abr, abi = er * jnp.cos(a_im * dt), er * jnp.sin(a_im * dt)
    den = a_re * a_re + a_im * a_im
    nr, ni = abr - 1.0, abi
    fr = ((nr * a_re + ni * a_im) / den)[..., None]
    fi = ((ni * a_re - nr * a_im) / den)[..., None]
    return abr, abi, fr * b_re - fi * b_im, fr * b_im + fi * b_re


def _complex_square(zr, zi):
    return zr * zr - zi * zi, 2.0 * zr * zi


def ssm_tables(abr, abi, bbr, bbi, c_re, c_im, seg):
    g, p, h = bbr.shape
    nq = g // GROUPS_PER_BLOCK
    zr, zi = abr.reshape(1, -1), abi.reshape(1, -1)
    a1 = jnp.stack([jnp.broadcast_to(zr, (ROWS, g * p)), jnp.broadcast_to(zi, (ROWS, g * p))])
    for _ in range(seg.bit_length() - 1):
        zr, zi = _complex_square(zr, zi)
    row = jnp.arange(ROWS)[:, None]
    ml, mlb = [], []
    for d in (1, 2, 4):
        ml.append(jnp.stack([jnp.where(row >= d, zr, 0.0), jnp.where(row >= d, zi, 0.0)]))
        mlb.append(jnp.stack([jnp.where(row + d < ROWS, zr, 0.0), jnp.where(row + d < ROWS, -zi, 0.0)]))
        zr, zi = _complex_square(zr, zi)
    eye = jnp.eye(GROUPS_PER_BLOCK, dtype=F32)

    def block_diag_in(bb):
        bq = bb.reshape(nq, GROUPS_PER_BLOCK, p, h)
        return jnp.einsum("qaph,ab->qahbp", bq, eye).reshape(nq, GROUPS_PER_BLOCK * h, GROUPS_PER_BLOCK * p)

    def block_diag_out(cc):
        cq = cc.reshape(nq, GROUPS_PER_BLOCK, h, p)
        return jnp.einsum("qahp,ab->qapbh", cq, eye).reshape(nq, GROUPS_PER_BLOCK * p, GROUPS_PER_BLOCK * h)

    bd = jnp.stack([block_diag_in(bbr), block_diag_in(bbi)]).astype(BF16)
    cd = jnp.stack([block_diag_out(c_re), block_diag_out(-c_im)]).astype(BF16)
    return bd, cd, a1, jnp.stack(ml), jnp.stack(mlb)


def ssm_table_grads(dab, dbd, dcd, g, p, h):
    nq = g // GROUPS_PER_BLOCK
    dabr, dabi = dab[0].sum(0).reshape(g, p), dab[1].sum(0).reshape(g, p)
    b5 = dbd.reshape(2, nq, GROUPS_PER_BLOCK, h, GROUPS_PER_BLOCK, p)
    dbb = jnp.einsum("rqahap->rqaph", b5).reshape(2, g, p, h)
    c5 = dcd.reshape(2, nq, GROUPS_PER_BLOCK, p, GROUPS_PER_BLOCK, h)
    dcc = jnp.einsum("rqapah->rqahp", c5).reshape(2, g, h, p)
    return dabr, dabi, dbb[0], dbb[1], dcc[0], -dcc[1]


HALO = 16


def _conv_inputs(gc_ref, v_ref, gch_ref, vh_ref, cv_s, i, tm):
    cv = gc_ref[...].astype(F32) * v_ref[...].astype(F32)
    halo = gch_ref[...].astype(F32) * vh_ref[...].astype(F32)
    cv_s[0:HALO, :] = jnp.where(i == 0, 0.0, halo)
    cv_s[HALO:, :] = cv
    return cv, cv_s[HALO - 1:HALO - 1 + tm, :], cv_s[HALO - 2:HALO - 2 + tm, :]


def _halo_spec(tm, width, tiles_per_seq):
    per = tm // HALO
    return pl.BlockSpec((HALO, width), lambda b, i: (jnp.maximum((b * tiles_per_seq + i) * per - 1, 0), 0))


def mixer_merge_forward(x, gb, gc, v, sga, sgb, yssm, u, mod, conv_w, dskip, wco, wglu, wso_t, wout, nb):
    t, d = x.shape
    cw, sw = gb.shape[1], u.shape[1]
    s = t // nb
    tm = _tile(s, 256)
    tps = s // tm

    def body(x_ref, gb_ref, gc_ref, v_ref, gch_ref, vh_ref, sga_ref, sgb_ref, ys_ref, u_ref, mod_ref, cw_ref,
             dsk_ref, wco_ref, wglu_ref, wso_ref, wout_ref, xo_ref, ya_ref, yb_ref, mix_ref, cv_s):
        i = pl.program_id(1)
        cv, cv1, cv2 = _conv_inputs(gc_ref, v_ref, gch_ref, vh_ref, cv_s, i, tm)
        w = cw_ref[...]
        conv = w[0:1] * cv2 + w[1:2] * cv1 + w[2:3] * cv
        ya = _dot((gb_ref[...].astype(F32) * conv).astype(BF16), wco_ref[...])
        s0 = ys_ref[...] + dsk_ref[...] * u_ref[...]
        s1, _ = _gelu(s0)
        z = _dot(s1.astype(BF16), wglu_ref[...])
        s2 = s1 * jax.nn.sigmoid(z)
        yb = _dot(s2.astype(BF16), wso_ref[...], NT)
        merged = sga_ref[...].astype(F32) * ya + sgb_ref[...].astype(F32) * yb
        mix = _dot(merged.astype(BF16), wout_ref[...])
        _, _, gt = _mod_rows(mod_ref, 1)
        xo_ref[...] = x_ref[...] + gt * mix
        ya_ref[...] = ya.astype(BF16)
        yb_ref[...] = yb.astype(BF16)
        mix_ref[...] = mix.astype(BF16)

    rd, rc, rw = _row_spec(tm, d, tps), _row_spec(tm, cw, tps), _row_spec(tm, sw, tps)
    hc = _halo_spec(tm, cw, tps)
    return pl.pallas_call(
        body, name="mixer_merge_forward", grid=(nb, tps),
        out_shape=[jax.ShapeDtypeStruct((t, d), F32)] + [jax.ShapeDtypeStruct((t, d), BF16)] * 3,
        in_specs=[rd, rc, rc, rc, hc, hc, rd, rd, rw, rw, _mod_spec(d), _const(conv_w.shape), _const((1, sw)),
                  _const(wco.shape), _const(wglu.shape), _const(wso_t.shape), _const(wout.shape)],
        out_specs=[rd, rd, rd, rd],
        scratch_shapes=[pltpu.VMEM((tm + HALO, cw), F32)],
        compiler_params=_params(("arbitrary", "arbitrary")),
    )(x, gb, gc, v, gc, v, sga, sgb, yssm, u, mod, conv_w, dskip, wco, wglu, wso_t, wout)


def mixer_merge_backward(dxo, mix, ya, yb, gb, gc, v, sga, sgb, yssm, u, mod, conv_w, dskip,
                         wco, wglu, wso_t, wout, nb, carried=None):
    t, d = dxo.shape
    cw, sw = gb.shape[1], u.shape[1]
    s = t // nb
    tm = _tile(s, 256)
    tps = s // tm

    def body(dxo_ref, mix_ref, ya_ref, yb_ref, gb_ref, gc_ref, v_ref, gch_ref, vh_ref, sga_ref, sgb_ref, ys_ref,
             u_ref, mod_ref, cw_ref, dsk_ref, wco_ref, wglu_ref, wso_ref, wout_ref,
             dgla_ref, dglb_ref, dgb_ref, dconv_ref, ds0_ref, dgt_ref, ddsk_ref, dwout_ref, dwco_ref, dwso_ref,
             dwglu_ref, cv_s):
        bi, i = pl.program_id(0), pl.program_id(1)
        _zero_when(i == 0, dgt_ref)
        _zero_when(jnp.logical_and(bi == 0, i == 0), ddsk_ref, dwout_ref, dwco_ref, dwso_ref, dwglu_ref)
        dxo = dxo_ref[...]
        _, _, gt = _mod_rows(mod_ref, 1)
        dmix = (gt * dxo).astype(BF16)
        dgt_ref[...] += jnp.sum(dxo * mix_ref[...].astype(F32), axis=0, keepdims=True)[None]
        ya, yb = ya_ref[...].astype(F32), yb_ref[...].astype(F32)
        sga, sgb = sga_ref[...].astype(F32), sgb_ref[...].astype(F32)
        merged = (sga * ya + sgb * yb).astype(BF16)
        dwout_ref[...] += _dot(merged, dmix, TN)
        dmerged = _dot(dmix, wout_ref[...], NT)
        dgla_ref[...] = (dmerged * ya * sga * (1.0 - sga)).astype(BF16)
        dglb_ref[...] = (dmerged * yb * sgb * (1.0 - sgb)).astype(BF16)
        dya = (dmerged * sga).astype(BF16)
        dyb = (dmerged * sgb).astype(BF16)
        cv, cv1, cv2 = _conv_inputs(gc_ref, v_ref, gch_ref, vh_ref, cv_s, i, tm)
        w = cw_ref[...]
        conv = w[0:1] * cv2 + w[1:2] * cv1 + w[2:3] * cv
        gbv = gb_ref[...].astype(F32)
        dwco_ref[...] += _dot((gbv * conv).astype(BF16), dya, TN)
        dya_in = _dot(dya, wco_ref[...], NT)
        dgb_ref[...] = (dya_in * conv).astype(BF16)
        dconv_ref[...] = dya_in * gbv
        uv = u_ref[...]
        s0 = ys_ref[...] + dsk_ref[...] * uv
        s1, th = _gelu(s0)
        s1b = s1.astype(BF16)
        sz = jax.nn.sigmoid(_dot(s1b, wglu_ref[...]))
        s2b = (s1 * sz).astype(BF16)
        dwso_ref[...] += _dot(dyb, s2b, TN)
        ds2 = _dot(dyb, wso_ref[...])
        dz = (ds2 * s1 * sz * (1.0 - sz)).astype(BF16)
        dwglu_ref[...] += _dot(s1b, dz, TN)
        ds1 = ds2 * sz + _dot(dz, wglu_ref[...], NT)
        ds0 = ds1 * _gelu_grad(s0, th)
        ds0_ref[...] = ds0
        ddsk_ref[...] += jnp.sum(ds0 * uv, axis=0, keepdims=True)

    rd, rc, rw = _row_spec(tm, d, tps), _row_spec(tm, cw, tps), _row_spec(tm, sw, tps)
    hc = _halo_spec(tm, cw, tps)
    return _call(
        body, name="mixer_merge_backward", grid=(nb, tps), carried=carried,
        out_shape=[jax.ShapeDtypeStruct((t, d), BF16), jax.ShapeDtypeStruct((t, d), BF16),
                   jax.ShapeDtypeStruct((t, cw), BF16), jax.ShapeDtypeStruct((t, cw), F32),
                   jax.ShapeDtypeStruct((t, sw), F32), jax.ShapeDtypeStruct((nb, 1, d), F32),
                   jax.ShapeDtypeStruct((1, sw), F32), jax.ShapeDtypeStruct(wout.shape, F32),
                   jax.ShapeDtypeStruct(wco.shape, F32), jax.ShapeDtypeStruct(wso_t.shape, F32),
                   jax.ShapeDtypeStruct(wglu.shape, F32)],
        in_specs=[rd, rd, rd, rd, rc, rc, rc, hc, hc, rd, rd, rw, rw, _mod_spec(d), _const(conv_w.shape),
                  _const((1, sw)), _const(wco.shape), _const(wglu.shape), _const(wso_t.shape), _const(wout.shape)],
        out_specs=[rd, rd, rc, rc, rw, pl.BlockSpec((1, 1, d), lambda b, i: (b, 0, 0)), _full((1, sw)),
                   _full(wout.shape), _full(wco.shape), _full(wso_t.shape), _full(wglu.shape)],
        scratch_shapes=[pltpu.VMEM((tm + HALO, cw), F32)],
        args=(dxo, mix, ya, yb, gb, gc, v, gc, v, sga, sgb, yssm, u, mod, conv_w, dskip, wco, wglu, wso_t, wout))


def conv_backward(dconv, gc, v, conv_w, nb):
    t, cw = dconv.shape
    s = t // nb
    tm = _tile(s, 512)
    tps = s // tm
    per = tm // ROWS
    slab = _tile(tm, 32)

    def body(dc_ref, dcn_ref, gc_ref, v_ref, gch_ref, vh_ref, cw_ref, dgc_ref, dv_ref, dw_ref, cv_s, dc_s):
        bi, i = pl.program_id(0), pl.program_id(1)
        _zero_when(jnp.logical_and(bi == 0, i == 0), dw_ref)
        cv_s[0:HALO, :] = jnp.where(i == 0, 0.0, gch_ref[...].astype(F32) * vh_ref[...].astype(F32))
        dc_s[tm:, :] = jnp.where(i == tps - 1, 0.0, dcn_ref[...])
        for r in range(0, tm, slab):
            cv_s[HALO + r:HALO + r + slab, :] = (gc_ref[r:r + slab, :].astype(F32)
                                                 * v_ref[r:r + slab, :].astype(F32))
            dc_s[r:r + slab, :] = dc_ref[r:r + slab, :]
        w = cw_ref[...]
        sums = [jnp.zeros((ROWS, cw), F32)] * CONV_K
        for r in range(0, tm, slab):
            dc = dc_s[r:r + slab, :]
            dcv = w[2:3] * dc + w[1:2] * dc_s[r + 1:r + 1 + slab, :] + w[0:1] * dc_s[r + 2:r + 2 + slab, :]
            dgc_ref[r:r + slab, :] = (dcv * v_ref[r:r + slab, :].astype(F32)).astype(BF16)
            dv_ref[r:r + slab, :] = (dcv * gc_ref[r:r + slab, :].astype(F32)).astype(BF16)
            for k in range(CONV_K):
                lag = HALO + r - (CONV_K - 1 - k)
                prod = dc * cv_s[lag:lag + slab, :]
                sums[k] = sums[k] + jnp.sum(prod.reshape(slab // ROWS, ROWS, cw), axis=0)
        dw_ref[...] += jnp.concatenate([jnp.sum(a, axis=0, keepdims=True) for a in sums], axis=0)

    rc = _row_spec(tm, cw, tps)
    nxt = pl.BlockSpec((ROWS, cw), lambda b, i: (jnp.minimum((b * tps + i + 1) * per, nb * tps * per - 1), 0))
    hc = _halo_spec(tm, cw, tps)
    return pl.pallas_call(
        body, name="conv_backward", grid=(nb, tps),
        out_shape=[jax.ShapeDtypeStruct((t, cw), BF16), jax.ShapeDtypeStruct((t, cw), BF16),
                   jax.ShapeDtypeStruct(conv_w.shape, F32)],
        in_specs=[rc, nxt, rc, rc, hc, hc, _full(conv_w.shape)],
        out_specs=[rc, rc, _full(conv_w.shape)],
        scratch_shapes=[pltpu.VMEM((tm + HALO, cw), F32), pltpu.VMEM((tm + ROWS, cw), F32)],
        compiler_params=_params(("arbitrary", "arbitrary")),
    )(dconv, dconv, gc, v, gc, v, conv_w)


def loss_forward_backward(x, target, g):
    t, d = x.shape
    tm = _tile(t, 512)

    def body(x_ref, t_ref, g_ref, l_ref, dx_ref, dg_ref):
        _zero_when(pl.program_id(0) == 0, dg_ref)
        xv = x_ref[...]
        gv = g_ref[...]
        r = lax.rsqrt(jnp.mean(xv * xv, axis=-1, keepdims=True) + EPS)
        n = xv * r
        err = n * gv - t_ref[...]
        l_ref[...] = jnp.full(l_ref.shape, 0.5 * jnp.sum(jnp.mean(err * err, axis=-1)), F32)
        dy = err * (1.0 / d)
        dn = dy * gv
        dx_ref[...] = r * (dn - n * jnp.mean(n * dn, axis=-1, keepdims=True))
        dg_ref[...] += jnp.sum(dy * n, axis=0, keepdims=True)

    row = pl.BlockSpec((tm, d), lambda i: (i, 0))
    return pl.pallas_call(
        body, name="loss_forward_backward", grid=(t // tm,),
        out_shape=[jax.ShapeDtypeStruct((t // tm, 1, 128), F32), jax.ShapeDtypeStruct((t, d), F32),
                   jax.ShapeDtypeStruct((1, d), F32)],
        in_specs=[row, row, _full((1, d))],
        out_specs=[pl.BlockSpec((1, 1, 128), lambda i: (i, 0, 0)), row, _full((1, d))],
        compiler_params=_params(("arbitrary",)),
    )(x, target, g)


def sum_slots(slots, name):
    _, r, c = slots.shape
    tr = _tile(r, 352) if r % 352 == 0 else _tile(r, 256)
    if tr < 128:
        tr = r

    def body(s_ref, o_ref):
        acc = s_ref[0].astype(F32)
        for j in range(1, N_DEV):
            acc = acc + s_ref[j].astype(F32)
        o_ref[...] = acc

    return pl.pallas_call(
        body, name=name, grid=(r // tr,),
        out_shape=jax.ShapeDtypeStruct((r, c), F32),
        in_specs=[pl.BlockSpec((N_DEV, tr, c), lambda i: (0, i, 0))],
        out_specs=pl.BlockSpec((tr, c), lambda i: (i, 0)),
        compiler_params=_params(("arbitrary",)),
    )(slots)


def adamw_update(w, g, m, v, name):
    r, c = w.shape
    tr = _tile(r, 256) if r % 8 == 0 else r

    def body(w_ref, g_ref, m_ref, v_ref, d_ref, mo_ref, vo_ref):
        d_ref[...], mo_ref[...], vo_ref[...] = _adamw(w_ref[...], g_ref[...], m_ref[...], v_ref[...])

    spec = pl.BlockSpec((tr, c), lambda i: (i, 0))
    return pl.pallas_call(
        body, name=name, grid=(r // tr,),
        out_shape=[jax.ShapeDtypeStruct((r, c), F32)] * 3,
        in_specs=[spec] * 4, out_specs=[spec] * 3,
        compiler_params=_params(("arbitrary",)),
    )(w, g, m, v)


def sum_adamw_update(slots, w, m, v, name):
    _, r, c = slots.shape
    tr = _tile(r, 352) if r % 352 == 0 else _tile(r, 256)

    def body(s_ref, w_ref, m_ref, v_ref, g_ref, d_ref, mo_ref, vo_ref):
        g = s_ref[0].astype(F32)
        for j in range(1, N_DEV):
            g = g + s_ref[j].astype(F32)
        g_ref[...] = g
        d_ref[...], mo_ref[...], vo_ref[...] = _adamw(w_ref[...], g, m_ref[...], v_ref[...])

    spec = pl.BlockSpec((tr, c), lambda i: (i, 0))
    return pl.pallas_call(
        body, name=name, grid=(r // tr,),
        out_shape=[jax.ShapeDtypeStruct((r, c), F32)] * 4,
        in_specs=[pl.BlockSpec((N_DEV, tr, c), lambda i: (0, i, 0))] + [spec] * 3, out_specs=[spec] * 4,
        compiler_params=_params(("arbitrary",)),
    )(slots, w, m, v)


def adamw_update_small(ws, gs, ms, vs):
    n = len(ws)

    def body(*refs):
        w_r, g_r, m_r, v_r = refs[:n], refs[n:2 * n], refs[2 * n:3 * n], refs[3 * n:4 * n]
        d_r, mo_r, vo_r = refs[4 * n:5 * n], refs[5 * n:6 * n], refs[6 * n:7 * n]
        for k in range(n):
            d_r[k][...], mo_r[k][...], vo_r[k][...] = _adamw(w_r[k][...], g_r[k][...], m_r[k][...], v_r[k][...])

    shapes = [jax.ShapeDtypeStruct(w.shape, F32) for w in ws]
    out = pl.pallas_call(body, name="adamw_update_small", out_shape=shapes * 3,
                         compiler_params=_params())(*ws, *gs, *ms, *vs)
    return out[:n], out[n:2 * n], out[2 * n:]


def _slots(grad_t):
    return grad_t.reshape(N_DEV, grad_t.shape[0] // N_DEV, grad_t.shape[1])


def kernel(x, c, w_ada, b_ada, g_ffn1, w1_a, w3_a, w2_a, g_mix, w_in, conv_w, w_conv_out, a_re, a_im, b_re, b_im, c_re, c_im, log_dt, d_skip, w_glu, w_ssm_out, w_out, g_ffn2, w1_b, w3_b, w2_b, g_final, loss_target, m_w_ada, m_b_ada, m_g_ffn1, m_w1_a, m_w3_a, m_w2_a, m_g_mix, m_w_in, m_conv_w, m_w_conv_out, m_a_re, m_a_im, m_b_re, m_b_im, m_c_re, m_c_im, m_log_dt, m_d_skip, m_w_glu, m_w_ssm_out, m_w_out, m_g_ffn2, m_w1_b, m_w3_b, m_w2_b, m_g_final, v_w_ada, v_b_ada, v_g_ffn1, v_w1_a, v_w3_a, v_w2_a, v_g_mix, v_w_in, v_conv_w, v_w_conv_out, v_a_re, v_a_im, v_b_re, v_b_im, v_c_re, v_c_im, v_log_dt, v_d_skip, v_w_glu, v_w_ssm_out, v_w_out, v_g_ffn2, v_w1_b, v_w3_b, v_w2_b, v_g_final):
    nb, s, d = x.shape
    t = nb * s
    me = 4 * lax.axis_index("x") + 2 * lax.axis_index("y") + lax.axis_index("c")
    g_n, p_n, h_n = b_re.shape[1:]
    cw_n = w_conv_out.shape[1] * N_DEV
    sw_n = w_ssm_out.shape[1]
    glu_fold = d // w_glu.shape[2]

    weights = dict(w_ada=w_ada, b_ada=b_ada, g_ffn1=g_ffn1, w1_a=w1_a, w3_a=w3_a, w2_a=w2_a, g_mix=g_mix, w_in=w_in,
                   conv_w=conv_w, w_conv_out=w_conv_out, a_re=a_re, a_im=a_im, b_re=b_re, b_im=b_im, c_re=c_re,
                   c_im=c_im, log_dt=log_dt, d_skip=d_skip, w_glu=w_glu, w_ssm_out=w_ssm_out, w_out=w_out,
                   g_ffn2=g_ffn2, w1_b=w1_b, w3_b=w3_b, w2_b=w2_b, g_final=g_final)
    mom1 = dict(w_ada=m_w_ada, b_ada=m_b_ada, g_ffn1=m_g_ffn1, w1_a=m_w1_a, w3_a=m_w3_a, w2_a=m_w2_a, g_mix=m_g_mix,
                w_in=m_w_in, conv_w=m_conv_w, w_conv_out=m_w_conv_out, a_re=m_a_re, a_im=m_a_im, b_re=m_b_re,
                b_im=m_b_im, c_re=m_c_re, c_im=m_c_im, log_dt=m_log_dt, d_skip=m_d_skip, w_glu=m_w_glu,
                w_ssm_out=m_w_ssm_out, w_out=m_w_out, g_ffn2=m_g_ffn2, w1_b=m_w1_b, w3_b=m_w3_b, w2_b=m_w2_b,
                g_final=m_g_final)
    mom2 = dict(w_ada=v_w_ada, b_ada=v_b_ada, g_ffn1=v_g_ffn1, w1_a=v_w1_a, w3_a=v_w3_a, w2_a=v_w2_a, g_mix=v_g_mix,
                w_in=v_w_in, conv_w=v_conv_w, w_conv_out=v_w_conv_out, a_re=v_a_re, a_im=v_a_im, b_re=v_b_re,
                b_im=v_b_im, c_re=v_c_re, c_im=v_c_im, log_dt=v_log_dt, d_skip=v_d_skip, w_glu=v_w_glu,
                w_ssm_out=v_w_ssm_out, w_out=v_w_out, g_ffn2=v_g_ffn2, w1_b=v_w1_b, w3_b=v_w3_b, w2_b=v_w2_b,
                g_final=v_g_final)
    names = list(weights)
    transposed = ("w1_a", "w3_a", "w_in", "w_ssm_out", "w1_b", "w3_b")
    groups = dict(ffn_a=("w1_a", "w3_a", "w2_a"), mixer=("w_in", "w_conv_out", "w_glu", "w_ssm_out", "w_out"),
                  ffn_b=("w1_b", "w3_b", "w2_b"))
    big = groups["ffn_a"] + groups["mixer"] + groups["ffn_b"]

    def shard_rows(name):
        w = weights[name][0]
        if name in transposed:
            w = w.T
        if name == "w_glu":
            w = w.reshape(w.shape[0] // glu_fold, d)
        return w.astype(BF16)

    def gather_stage(group):
        return GatherStage([shard_rows(n) for n in groups[group]])

    gw = {}

    def keep_weights(group, outs):
        for n, w in zip(groups[group], outs):
            gw[n] = w.reshape(sw_n, sw_n) if n == "w_glu" else w

    pad_rows = lambda a: jnp.pad(a, ((0, -a.shape[0] % GATHER_ROWS), (0, 0)))
    c_all, conv_all, *ffn_a_weights = run_stage(
        StageGroup([GatherStage([pad_rows(c), pad_rows(conv_w[0])]), gather_stage("ffn_a")]), "gather_cond_ffn_a")
    keep_weights("ffn_a", ffn_a_weights)
    c_all = c_all.reshape(N_DEV, -1, d)[:, :nb].reshape(N_DEV * nb, d)
    conv_full = conv_all.reshape(N_DEV, GATHER_ROWS, -1)[:, :CONV_K].transpose(1, 0, 2).reshape(CONV_K, cw_n)
    ada_cols = w_ada.shape[2]
    b_cols = lax.dynamic_slice(b_ada, (0, me * ada_cols), (1, ada_cols))
    mod_cols = ada_forward(c_all, w_ada[0], b_cols)
    (mod_all,) = run_stage(GatherStage([mod_cols]), "gather_mod")
    mod_mine = lax.dynamic_slice(mod_all.reshape(N_DEV, N_DEV * nb, ada_cols), (0, me * nb, 0), (N_DEV, nb, ada_cols))
    mod = mod_mine.transpose(1, 0, 2).reshape(nb, N_MOD, d)

    disc_in = (a_re[0], a_im[0], b_re[0], b_im[0], log_dt[0])
    (abr, abi, bbr, bbi), disc_vjp = jax.vjp(ssm_discretise, *disc_in)
    bd, cd, abar8, ml, mlb = ssm_tables(abr, abi, bbr, bbi, c_re[0], c_im[0], _tile(s, 256) // ROWS)

    x0 = x.reshape(t, d)
    (x1, sl1, dsl1, b1, f1, h1), got = ffn_forward(x0, mod, g_ffn1, gw["w1_a"], gw["w3_a"], gw["w2_a"], 0, nb,
                                                   "ffn_a_forward", carried=gather_stage("mixer"))
    keep_weights("mixer", got)
    (gb, gc, vv, u, sga, sgb, h2), got = mixer_proj_forward(x1, mod, g_mix, gw["w_in"], cw_n, sw_n, nb,
                                                            carried=gather_stage("ffn_b"))
    keep_weights("ffn_b", got)
    yssm, st = ssm_forward(u, bd, cd, abar8, ml, nb)
    x2, ya, yb, mix = mixer_merge_forward(x1, gb, gc, vv, sga, sgb, yssm, u, mod, conv_full, d_skip,
                                          gw["w_conv_out"], gw["w_glu"], gw["w_ssm_out"], gw["w_out"], nb)
    (x3, sl3, dsl3, b3, f3, h3), _ = ffn_forward(x2, mod, g_ffn2, gw["w1_b"], gw["w3_b"], gw["w2_b"], 2, nb,
                                                 "ffn_b_forward")
    loss_parts, dx3, dg_final = loss_forward_backward(x3, loss_target.reshape(t, d), g_final.reshape(1, d))
    loss = lax.psum(jnp.sum(loss_parts[:, 0, 0]), ("x", "y", "c"))

    part, received = {}, {}

    def exchange_stage(ns):
        return ExchangeStage([_slots(part[n]) for n in ns])

    (da3, db3, df3, dgt3), _ = ffn_backward_hidden(dx3, sl3, dsl3, b3, f3, mod, gw["w2_b"], 2, nb,
                                                   "ffn_b_backward_hidden")
    (dx2, dmod3, dg_ffn2), _ = ffn_backward_input(dx3, x2, da3, db3, mod, g_ffn2, gw["w1_b"], gw["w3_b"],
                                                  2, nb, "ffn_b_backward_input")
    (part["w1_b"],), _ = nn_matmul(da3, h3, "grad_w1_b")
    (part["w3_b"],), _ = nn_matmul(db3, h3, "grad_w3_b")
    (part["w2_b"],), _ = nn_matmul(sl3, df3, "grad_w2_b", lhs2_t=b3)
    (dgla, dglb, dgb, dconv, ds0, dgt2, dd_skip, dw_out, dw_co, dw_so_t, dw_glu), got = mixer_merge_backward(
        dx2, mix, ya, yb, gb, gc, vv, sga, sgb, yssm, u, mod, conv_full, d_skip,
        gw["w_conv_out"], gw["w_glu"], gw["w_ssm_out"], gw["w_out"], nb, carried=exchange_stage(groups["ffn_b"]))
    received.update(zip(groups["ffn_b"], got))
    part["w_out"] = dw_out.astype(BF16)
    part["w_conv_out"] = dw_co.astype(BF16)
    part["w_ssm_out"] = dw_so_t.astype(BF16)
    part["w_glu"] = dw_glu.reshape(sw_n // glu_fold, d).astype(BF16)
    (du, dab, dbd, dcd), got = ssm_backward(u, ds0, st, bd, cd, abar8, mlb, d_skip, nb,
                                            carried=exchange_stage(groups["mixer"][1:]))
    received.update(zip(groups["mixer"][1:], got))
    dgc, dvv, dconv_w = conv_backward(dconv, gc, vv, conv_full, nb)
    part["w_in"] = jnp.concatenate([tn_matmul(p, h2, "grad_w_in_%d" % k)
                                    for k, p in enumerate((dgb, dgc, dvv, du, dgla, dglb))], axis=0)
    (dx1, dmod2, dg_mix), got = mixer_proj_backward(dgb, dgc, dvv, du, dgla, dglb, dx2, x1, mod, g_mix, gw["w_in"], nb,
                                                    carried=exchange_stage(("w_in",)))
    received["w_in"] = got[0]
    def pack(parts):
        flat = jnp.concatenate([a.reshape(-1) for a in parts.values()])
        rows = -(-flat.shape[0] // (128 * GATHER_ROWS)) * GATHER_ROWS
        return jnp.pad(flat, (0, rows * 128 - flat.shape[0])).reshape(rows, 128)

    def unpack(flat, parts):
        out, off = {}, 0
        for key, like in parts.items():
            n = math.prod(like.shape)
            out[key], off = flat[..., off:off + n].reshape(flat.shape[:-1] + like.shape), off + n
        return out

    dabr, dabi, dbbr, dbbi, dcr, dci = ssm_table_grads(dab, dbd, dcd, g_n, p_n, h_n)
    early = dict(gmod=jnp.concatenate([dmod2, dgt2, dmod3, dgt3], axis=1), g_mix=dg_mix, g_ffn2=dg_ffn2,
                 g_final=dg_final, d_skip=dd_skip, abr=dabr, abi=dabi, bbr=dbbr, bbi=dbbi, c_re=dcr, c_im=dci,
                 conv_w=dconv_w)
    (da1, db1, df1, dgt1), _ = ffn_backward_hidden(dx1, sl1, dsl1, b1, f1, mod, gw["w2_a"], 0, nb,
                                                   "ffn_a_backward_hidden")
    (part["w2_a"],), (early_all,) = nn_matmul(sl1, df1, "grad_w2_a", carried=GatherStage([pack(early)]), lhs2_t=b1)
    (part["w1_a"],), got = nn_matmul(da1, h1, "grad_w1_a", carried=exchange_stage(("w2_a",)))
    received["w2_a"] = got[0]
    (part["w3_a"],), got = nn_matmul(db1, h1, "grad_w3_a", carried=exchange_stage(("w1_a",)))
    received["w1_a"] = got[0]
    (dx0, dmod1, dg_ffn1), got = ffn_backward_input(dx1, x0, da1, db1, mod, g_ffn1, gw["w1_a"], gw["w3_a"],
                                                    0, nb, "ffn_a_backward_input", carried=exchange_stage(("w3_a",)))
    received["w3_a"] = got[0]
    late = dict(gmod=jnp.concatenate([dmod1, dgt1], axis=1), g_ffn1=dg_ffn1)
    (late_all,) = run_stage(GatherStage([pack(late)]), "gather_ffn_a_small_grads")

    tot, per_dev = {}, {}
    for parts, gathered, name in ((early, early_all, "early"), (late, late_all, "late")):
        slots = gathered.reshape(N_DEV, -1, 128)
        total = sum_slots(slots, "sum_small_grads_" + name).reshape(-1)
        for key, val in unpack(total, parts).items():
            tot[name + key if key == "gmod" else key] = val
        per_dev[name] = unpack(slots.reshape(N_DEV, -1), parts)["gmod"]
    gmod_all = jnp.concatenate([per_dev["late"], per_dev["early"]], axis=2).reshape(N_DEV * nb, N_MOD * d)
    gmod_tot = jnp.concatenate([tot["lategmod"], tot["earlygmod"]], axis=1)
    g_a_re, g_a_im, g_b_re, g_b_im, g_log_dt = disc_vjp((tot["abr"], tot["abi"], tot["bbr"], tot["bbi"]))

    grads = {}
    grads["b_ada"] = sum_rows(gmod_tot.reshape(nb, N_MOD * d))
    grads["g_ffn1"], grads["g_mix"], grads["g_ffn2"] = tot["g_ffn1"], tot["g_mix"], tot["g_ffn2"]
    grads["g_final"] = tot["g_final"].reshape(d)
    grads["d_skip"] = tot["d_skip"]
    grads["a_re"], grads["a_im"], grads["log_dt"] = g_a_re[None], g_a_im[None], g_log_dt[None]
    grads["b_re"], grads["b_im"] = g_b_re[None], g_b_im[None]
    grads["c_re"], grads["c_im"] = tot["c_re"][None], tot["c_im"][None]
    grads["conv_w"] = lax.dynamic_slice(tot["conv_w"], (0, me * conv_w.shape[2]), (CONV_K, conv_w.shape[2]))[None]

    delta, new_m, new_v = {}, {}, {}
    for name in big:
        wmv = (weights[name][0], mom1[name][0], mom2[name][0])
        if name in transposed:
            outs = sum_adamw_update(received[name], *[a.T for a in wmv], "adamw_" + name)
            gsum, dl, mm, vn = [a.T for a in outs]
        elif received[name].shape[1:] == weights[name].shape[1:]:
            gsum, dl, mm, vn = sum_adamw_update(received[name], *wmv, "adamw_" + name)
        else:
            gsum = sum_slots(received[name], "sum_" + name).reshape(weights[name].shape[1:])
            dl, mm, vn = adamw_update(*wmv[:1], gsum, *wmv[1:], "adamw_" + name)
        grads[name] = gsum[None]
        delta[name], new_m[name], new_v[name] = dl[None], mm[None], vn[None]

    gmod_cols = lax.dynamic_slice(gmod_all, (0, me * ada_cols), (N_DEV * nb, ada_cols))
    g_wada, d_wada, m_wada, v_wada = ada_backward_update(c_all, gmod_cols, w_ada[0], m_w_ada[0], v_w_ada[0])
    grads["w_ada"], delta["w_ada"], new_m["w_ada"], new_v["w_ada"] = g_wada[None], d_wada[None], m_wada[None], v_wada[None]

    small_names = [n for n in names if n not in big and n != "w_ada"]

    narrow = ("b_re", "b_im")

    def as2d(n, a):
        a = jnp.swapaxes(a.reshape(weights[n].shape), -1, -2) if n in narrow else a
        return a.reshape(-1, a.shape[-1])

    def from2d(n, a):
        shape = weights[n].shape
        return jnp.swapaxes(a.reshape(shape[:-2] + (shape[-1], shape[-2])), -1, -2) if n in narrow else a.reshape(shape)

    sw_, sg_, sm_, sv_ = ([as2d(n, src[n]) for n in small_names] for src in (weights, grads, mom1, mom2))
    sd, smo, svo = adamw_update_small(sw_, sg_, sm_, sv_)
    for n, dl, mm, vn in zip(small_names, sd, smo, svo):
        grads[n] = grads[n].reshape(weights[n].shape)
        delta[n], new_m[n], new_v[n] = from2d(n, dl), from2d(n, mm), from2d(n, vn)

    grad_x = dx0.reshape(nb, s, d)
    return (loss, grad_x, *[grads[n] for n in names], *[delta[n] for n in names],
            *[new_m[n] for n in names], *[new_v[n] for n in names])


def sum_rows(a):
    r, c = a.shape

    def body(a_ref, o_ref):
        acc = a_ref[0:1, :]
        for j in range(1, r):
            acc = acc + a_ref[j:j + 1, :]
        o_ref[...] = acc

    return pl.pallas_call(body, name="sum_rows", out_shape=jax.ShapeDtypeStruct((1, c), F32),
                          compiler_params=_params())(a)
```

```python
import functools
import math

import jax
import jax.numpy as jnp
from jax import lax
from jax.experimental import pallas as pl
from jax.experimental.pallas import tpu as pltpu

F32 = jnp.float32
BF16 = jnp.bfloat16
N_DEV = 8
N_MOD = 9
EPS = 1e-6
CONV_K = 3
ADAM_LR = 0.001
ADAM_B1 = 0.9
ADAM_B2 = 0.999
ADAM_EPS = 1e-08
ADAM_WD = 0.01
ADAM_STEP = 10
GELU_C0 = math.sqrt(2.0 / math.pi)
GELU_C1 = 0.044715
V7X_VMEM_LIMIT = 56 * 1024 * 1024
MESH_ID = pl.DeviceIdType.MESH
NT = (((1,), (1,)), ((), ()))
TN = (((0,), (0,)), ((), ()))


def _dot(a, b, dims=None):
    if dims is None:
        return jnp.dot(a, b, preferred_element_type=F32)
    return lax.dot_general(a, b, dims, preferred_element_type=F32)


def _params(sem=None, vmem=V7X_VMEM_LIMIT):
    return pltpu.CompilerParams(dimension_semantics=sem, vmem_limit_bytes=vmem)


def _full(shape):
    return pl.BlockSpec(shape, lambda *_: (0,) * len(shape))


def _const(shape):
    return pl.BlockSpec(shape, lambda *_: (0,) * len(shape), pipeline_mode=pl.Buffered(1))


def _tile(n, want):
    t = min(n, want)
    while n % t:
        t //= 2
    return t


class GatherStage:
    COPIES = 9

    def __init__(self, shards):
        n = len(shards)
        self.inputs = list(shards)
        self.out_shape = [jax.ShapeDtypeStruct((N_DEV * s.shape[0], s.shape[1]), s.dtype) for s in shards]
        self.scratch = [pltpu.SemaphoreType.DMA((self.COPIES * n,)), pltpu.SemaphoreType.DMA((self.COPIES * n,)),
                        pltpu.SemaphoreType.DMA((n,))]

    def _plan(self, ins, outs, sems):
        send_sems, recv_sems, local_sems = sems
        n = len(ins)
        x, y, c = lax.axis_index("x"), lax.axis_index("y"), lax.axis_index("c")
        me, sibling, xn, yn, dg = (x, y, c), (x, y, 1 - c), (1 - x, y, c), (x, 1 - y, c), (1 - x, 1 - y, c)

        def rows(k, block, half=None):
            r = ins[k].shape[0]
            px, py, pc = block
            base = (4 * px + 2 * py + pc) * r
            if half is None:
                return outs[k].at[pl.ds(base, r), :]
            return outs[k].at[pl.ds(base + half * (r // 2), r // 2), :]

        def copy(k, j, block, to, half=None, src=None):
            return pltpu.make_async_remote_copy(
                src_ref=rows(k, block, half) if src is None else src, dst_ref=rows(k, block, half),
                send_sem=send_sems.at[self.COPIES * k + j], recv_sem=recv_sems.at[self.COPIES * k + j],
                device_id=to, device_id_type=MESH_ID)

        sib = lambda b: (b[0], b[1], 1 - b[2])
        mine = [pltpu.make_async_copy(ins[k], rows(k, me), local_sems.at[k]) for k in range(n)]
        first = [(0, me, sibling, None, sibling), (1, me, xn, None, xn), (2, me, yn, None, yn)]
        second = [(3, xn, yn, 0, dg), (4, yn, xn, 1, dg), (5, xn, sibling, None, sib(xn)), (6, yn, sibling, None, sib(yn))]
        third = [(7, dg, sibling, 0, sib(dg)), (8, dg, sibling, 1, sib(dg))]
        return n, me, copy, mine, first, second, third

    def start(self, ins, outs, sems):
        n, me, copy, mine, first, _, _ = self._plan(ins, outs, sems)
        for cp in mine:
            cp.start()
        for k in range(n):
            for j, block, to, half, _ in first:
                copy(k, j, block, to, half, src=ins[k]).start()

    def advance(self, ins, outs, sems):
        n, me, copy, mine, first, second, third = self._plan(ins, outs, sems)
        for k in range(n):
            copy(k, 1, first[1][4], me).wait_recv()
            copy(k, 2, first[2][4], me).wait_recv()
            for j, block, to, half, _ in second:
                copy(k, j, block, to, half).start()

    def advance_again(self, ins, outs, sems):
        n, me, copy, mine, first, second, third = self._plan(ins, outs, sems)
        for k in range(n):
            copy(k, 3, second[0][4], me, 0).wait_recv()
            copy(k, 4, second[1][4], me, 1).wait_recv()
            for j, block, to, half, _ in third:
                copy(k, j, block, to, half).start()

    def finish(self, ins, outs, sems):
        n, me, copy, mine, first, second, third = self._plan(ins, outs, sems)
        arrived = lambda k, j, block, half: copy(k, j, block, me, half).wait_recv()
        for k in range(n):
            arrived(k, 0, first[0][4], None)
            arrived(k, 5, second[2][4], None)
            arrived(k, 6, second[3][4], None)
            arrived(k, 7, third[0][4], 0)
            arrived(k, 8, third[1][4], 1)
        for k in range(n):
            for j, block, to, half, _ in first:
                copy(k, j, block, to, half, src=ins[k]).wait_send()
            for j, block, to, half, _ in second + third:
                copy(k, j, block, to, half).wait_send()
        for cp in mine:
            cp.wait()


class ExchangeStage:
    def __init__(self, bufs):
        n = len(bufs)
        self.inputs = list(bufs)
        self.out_shape = [jax.ShapeDtypeStruct(b.shape, b.dtype) for b in bufs]
        self.scratch = [pltpu.SemaphoreType.DMA((7 * n,)), pltpu.SemaphoreType.DMA((7 * n,)),
                        pltpu.SemaphoreType.DMA((n,))]

    def _plan(self, ins, outs, sems):
        send_sems, recv_sems, local_sems = sems
        n = len(ins)
        x, y, c = lax.axis_index("x"), lax.axis_index("y"), lax.axis_index("c")
        me = 4 * x + 2 * y + c
        mine = [pltpu.make_async_copy(ins[k].at[me], outs[k].at[me], local_sems.at[k]) for k in range(n)]
        copies = []
        for mask in range(1, N_DEV):
            px, py, pc = x ^ (mask >> 2), y ^ ((mask >> 1) & 1), c ^ (mask & 1)
            for k in range(n):
                copies.append(pltpu.make_async_remote_copy(
                    src_ref=ins[k].at[4 * px + 2 * py + pc], dst_ref=outs[k].at[me],
                    send_sem=send_sems.at[7 * k + mask - 1], recv_sem=recv_sems.at[7 * k + mask - 1],
                    device_id=(px, py, pc), device_id_type=MESH_ID))
        return mine, copies

    def start(self, ins, outs, sems):
        mine, copies = self._plan(ins, outs, sems)
        for cp in mine + copies:
            cp.start()

    def advance(self, ins, outs, sems):
        pass

    advance_again = advance

    def finish(self, ins, outs, sems):
        mine, copies = self._plan(ins, outs, sems)
        for cp in copies:
            cp.wait_recv()
        for cp in copies:
            cp.wait_send()
        for cp in mine:
            cp.wait()


ANY_SPEC = pl.BlockSpec(memory_space=pl.ANY)
GATHER_ROWS = 16


class StageGroup:
    def __init__(self, stages):
        self.stages = list(stages)
        self.inputs = [a for s in stages for a in s.inputs]
        self.out_shape = [o for s in stages for o in s.out_shape]
        self.scratch = [t for s in stages for t in s.scratch]

    def _parts(self, ins, outs, sems):
        i = o = t = 0
        for s in self.stages:
            ni, no, nt = len(s.inputs), len(s.out_shape), len(s.scratch)
            yield s, ins[i:i + ni], outs[o:o + no], sems[t:t + nt]
            i, o, t = i + ni, o + no, t + nt

    def start(self, ins, outs, sems):
        for s, i_, o_, t_ in self._parts(ins, outs, sems):
            s.start(i_, o_, t_)

    def advance(self, ins, outs, sems):
        for s, i_, o_, t_ in self._parts(ins, outs, sems):
            s.advance(i_, o_, t_)

    def advance_again(self, ins, outs, sems):
        for s, i_, o_, t_ in self._parts(ins, outs, sems):
            s.advance_again(i_, o_, t_)

    def finish(self, ins, outs, sems):
        for s, i_, o_, t_ in self._parts(ins, outs, sems):
            s.finish(i_, o_, t_)


def run_stage(stage, name):
    ci, co = len(stage.inputs), len(stage.out_shape)

    def body(*refs):
        ins, outs, sems = refs[:ci], refs[ci:ci + co], refs[ci + co:]
        stage.start(ins, outs, sems)
        stage.advance(ins, outs, sems)
        stage.advance_again(ins, outs, sems)
        stage.finish(ins, outs, sems)

    return pl.pallas_call(body, name=name, out_shape=stage.out_shape, in_specs=[ANY_SPEC] * ci,
                          out_specs=[ANY_SPEC] * co, scratch_shapes=stage.scratch)(*stage.inputs)


def _call(body, *, name, grid, in_specs, out_specs, out_shape, args, scratch_shapes=(), carried=None):
    sem = ("arbitrary",) * len(grid)
    if carried is None:
        return pl.pallas_call(body, name=name, grid=grid, in_specs=list(in_specs), out_specs=list(out_specs),
                              out_shape=list(out_shape), scratch_shapes=list(scratch_shapes),
                              compiler_params=_params(sem))(*args), None
    ni, no, ns = len(in_specs), len(out_shape), len(scratch_shapes)
    ci, co = len(carried.inputs), len(carried.out_shape)
    n_steps = math.prod(grid)

    def wrapped(*refs):
        ins, refs = refs[:ni], refs[ni:]
        cins, refs = refs[:ci], refs[ci:]
        outs, refs = refs[:no], refs[no:]
        couts, refs = refs[:co], refs[co:]
        scr, csems = refs[:ns], refs[ns:]
        step = functools.reduce(lambda acc, ig: acc * ig[1] + ig[0],
                                zip([pl.program_id(k) for k in range(len(grid))], grid), 0)

        @pl.when(step == 0)
        def _():
            carried.start(cins, couts, csems)

        @pl.when(step == n_steps // 2)
        def _():
            carried.advance(cins, couts, csems)

        @pl.when(step == (3 * n_steps) // 4)
        def _():
            carried.advance_again(cins, couts, csems)

        body(*ins, *outs, *scr)

        @pl.when(step == n_steps - 1)
        def _():
            carried.finish(cins, couts, csems)

    res = pl.pallas_call(
        wrapped, name=name, grid=grid, in_specs=list(in_specs) + [ANY_SPEC] * ci,
        out_specs=list(out_specs) + [ANY_SPEC] * co, out_shape=list(out_shape) + carried.out_shape,
        scratch_shapes=list(scratch_shapes) + carried.scratch, compiler_params=_params(sem),
    )(*args, *carried.inputs)
    return res[:no], res[no:]


def _norm_mod(x, g, shift, scale):
    r = lax.rsqrt(jnp.mean(x * x, axis=-1, keepdims=True) + EPS)
    n = x * r
    return (n * g) * (1.0 + scale) + shift, n, r


def _norm_mod_bwd(dh, n, r, g, scale):
    dsh = jnp.sum(dh, axis=0, keepdims=True)
    dsc = jnp.sum(dh * (n * g), axis=0, keepdims=True)
    dg = jnp.sum(dh * (1.0 + scale) * n, axis=0, keepdims=True)
    dn = dh * ((1.0 + scale) * g)
    dx = r * (dn - n * jnp.mean(n * dn, axis=-1, keepdims=True))
    return dx, dsh, dsc, dg


def _mod_rows(mod_ref, sub):
    m = mod_ref[0]
    return m[3 * sub:3 * sub + 1], m[3 * sub + 1:3 * sub + 2], m[3 * sub + 2:3 * sub + 3]


def _gelu(x):
    t = jnp.tanh(GELU_C0 * (x + GELU_C1 * x * x * x))
    return 0.5 * x * (1.0 + t), t


def _gelu_grad(x, t):
    return 0.5 * (1.0 + t) + 0.5 * x * (1.0 - t * t) * (GELU_C0 * (1.0 + 3.0 * GELU_C1 * x * x))


def _zero_when(cond, *refs):
    @pl.when(cond)
    def _():
        for r in refs:
            r[...] = jnp.zeros_like(r)


def ada_forward(c_all, w_ada, b_ada_cols):
    def body(c_ref, w_ref, b_ref, o_ref):
        c = c_ref[...]
        cond = (c * jax.nn.sigmoid(c)).astype(BF16)
        o_ref[...] = _dot(cond, w_ref[...].astype(BF16)) + b_ref[...]

    nb, d = c_all.shape
    cols = w_ada.shape[1]
    tn = _tile(cols, 384)
    return pl.pallas_call(
        body, name="ada_forward", grid=(cols // tn,),
        out_shape=jax.ShapeDtypeStruct((nb, cols), F32),
        in_specs=[_full((nb, d)), pl.BlockSpec((d, tn), lambda j: (0, j)), pl.BlockSpec((1, tn), lambda j: (0, j))],
        out_specs=pl.BlockSpec((nb, tn), lambda j: (0, j)),
        compiler_params=_params(("arbitrary",)),
    )(c_all, w_ada, b_ada_cols)


def _adamw(w, g, m, v):
    m = ADAM_B1 * m + (1.0 - ADAM_B1) * g
    v = ADAM_B2 * v + (1.0 - ADAM_B2) * (g * g)
    m_hat = m / (1.0 - ADAM_B1 ** ADAM_STEP)
    v_hat = v / (1.0 - ADAM_B2 ** ADAM_STEP)
    delta = -ADAM_LR * (m_hat / (jnp.sqrt(v_hat) + ADAM_EPS) + ADAM_WD * w)
    return delta, m, v


def ada_backward_update(c_all, gmod_cols, w, m, v):
    def body(c_ref, g_ref, w_ref, m_ref, v_ref, go_ref, d_ref, mo_ref, vo_ref):
        c = c_ref[...]
        cond = (c * jax.nn.sigmoid(c)).astype(BF16)
        g = _dot(cond, g_ref[...].astype(BF16), TN)
        go_ref[...] = g
        d_ref[...], mo_ref[...], vo_ref[...] = _adamw(w_ref[...], g, m_ref[...], v_ref[...])

    nb, d = c_all.shape
    cols = w.shape[1]
    tn = _tile(cols, 128)
    col = pl.BlockSpec((d, tn), lambda j: (0, j))
    return pl.pallas_call(
        body, name="ada_backward_update", grid=(cols // tn,),
        out_shape=[jax.ShapeDtypeStruct(w.shape, F32)] * 4,
        in_specs=[_full((nb, d)), pl.BlockSpec((nb, tn), lambda j: (0, j)), col, col, col],
        out_specs=[col] * 4,
        compiler_params=_params(("arbitrary",)),
    )(c_all, gmod_cols, w, m, v)


def _row_spec(tm, width, tiles_per_seq):
    return pl.BlockSpec((tm, width), lambda b, i: (b * tiles_per_seq + i, 0))


def _mod_spec(d):
    return pl.BlockSpec((1, N_MOD, d), lambda b, i: (b, 0, 0))


def _col_spec(rows, tm, tiles_per_seq):
    return pl.BlockSpec((rows, tm), lambda b, i: (0, b * tiles_per_seq + i))


def _ffn_chunk(f):
    return f // 4 if f % 512 == 0 and f > 1536 else f


def ffn_forward(x, mod, g, w1t, w3t, w2, sub, nb, name, carried=None):
    t, d = x.shape
    f = w1t.shape[0]
    s = t // nb
    tm = _tile(s, 512)
    fc = _ffn_chunk(f)

    def body(x_ref, mod_ref, g_ref, w1_ref, w3_ref, w2_ref, xo_ref, a_ref, b_ref, f_ref, h_ref):
        xv = x_ref[...]
        sh, sc, gt = _mod_rows(mod_ref, sub)
        h, _, _ = _norm_mod(xv, g_ref[...], sh, sc)
        hb = h.astype(BF16)
        h_ref[...] = hb
        acc_t = jnp.zeros((d, tm), F32)
        for k in range(f // fc):
            rows = slice(k * fc, (k + 1) * fc)
            a = _dot(w1_ref[rows, :], hb, NT)
            b = _dot(w3_ref[rows, :], hb, NT)
            a_ref[rows, :] = a.astype(BF16)
            b_ref[rows, :] = b.astype(BF16)
            sw = (a * jax.nn.sigmoid(a)) * b
            acc_t = acc_t + _dot(w2_ref[rows, :], sw.astype(BF16), TN)
        acc = acc_t.T
        f_ref[...] = acc.astype(BF16)
        xo_ref[...] = xv + (0.5 * gt) * acc

    tps = s // tm
    rd, cf = _row_spec(tm, d, tps), _col_spec(f, tm, tps)
    return _call(
        body, name=name, grid=(nb, tps), carried=carried,
        out_shape=[jax.ShapeDtypeStruct((t, d), F32)] + [jax.ShapeDtypeStruct((f, t), BF16)] * 2
        + [jax.ShapeDtypeStruct((t, d), BF16)] * 2,
        in_specs=[rd, _mod_spec(d), _const((1, d)), _const((f, d)), _const((f, d)), _const((f, d))],
        out_specs=[rd, cf, cf, rd, rd],
        args=(x, mod, g, w1t, w3t, w2))


def ffn_backward_hidden(dxo, a_t, b_t, fo, mod, w2, sub, nb, name, carried=None):
    t, d = dxo.shape
    f = w2.shape[0]
    s = t // nb
    tm = _tile(s, 512)
    fc = _tile(f, 704) if f % 704 == 0 else _tile(f, 512)

    def body(dxo_ref, a_ref, b_ref, f_ref, mod_ref, w2_ref, da_ref, db_ref, s_ref, df_ref, dgt_ref):
        _zero_when(pl.program_id(1) == 0, dgt_ref)
        dxo = dxo_ref[...]
        _, _, gt = _mod_rows(mod_ref, sub)
        dfb = ((0.5 * gt) * dxo).astype(BF16)
        df_ref[...] = dfb
        dgt_ref[...] += 0.5 * jnp.sum(dxo * f_ref[...].astype(F32), axis=0, keepdims=True)[None]
        for k in range(f // fc):
            rows = slice(k * fc, (k + 1) * fc)
            ds = _dot(w2_ref[rows, :], dfb, NT).astype(BF16)
            av = a_ref[rows, :].astype(F32)
            bv = b_ref[rows, :]
            sig = jax.nn.sigmoid(av)
            sl = av * sig
            slb = sl.astype(BF16)
            da_ref[rows, :] = ds * bv * (sig + sl * (1.0 - sig)).astype(BF16)
            db_ref[rows, :] = ds * slb
            s_ref[rows, :] = slb * bv

    tps = s // tm
    rd, cf = _row_spec(tm, d, tps), _col_spec(f, tm, tps)
    return _call(
        body, name=name, grid=(nb, tps), carried=carried,
        out_shape=[jax.ShapeDtypeStruct((f, t), BF16)] * 3
        + [jax.ShapeDtypeStruct((t, d), BF16), jax.ShapeDtypeStruct((nb, 1, d), F32)],
        in_specs=[rd, cf, cf, rd, _mod_spec(d), _const((f, d))],
        out_specs=[cf, cf, cf, rd, pl.BlockSpec((1, 1, d), lambda b, i: (b, 0, 0))],
        args=(dxo, a_t, b_t, fo, mod, w2))


def ffn_backward_input(dxo, x, da_t, db_t, mod, g, w1t, w3t, sub, nb, name, carried=None):
    t, d = x.shape
    f = w1t.shape[0]
    s = t // nb
    tm = _tile(s, 512)

    def body(dxo_ref, x_ref, da_ref, db_ref, mod_ref, g_ref, w1_ref, w3_ref, dx_ref, dmod_ref, dg_ref):
        bi, i = pl.program_id(0), pl.program_id(1)
        _zero_when(i == 0, dmod_ref)
        _zero_when(jnp.logical_and(bi == 0, i == 0), dg_ref)
        sh, sc, _ = _mod_rows(mod_ref, sub)
        gv = g_ref[...]
        _, n, r = _norm_mod(x_ref[...], gv, sh, sc)
        dh_t = _dot(w1_ref[...], da_ref[...], TN) + _dot(w3_ref[...], db_ref[...], TN)
        dxn, dsh, dsc, dg = _norm_mod_bwd(dh_t.T, n, r, gv, sc)
        dx_ref[...] = dxo_ref[...] + dxn
        dmod_ref[...] += jnp.concatenate([dsh, dsc], axis=0)[None]
        dg_ref[...] += dg

    tps = s // tm
    rd, cf = _row_spec(tm, d, tps), _col_spec(f, tm, tps)
    return _call(
        body, name=name, grid=(nb, tps), carried=carried,
        out_shape=[jax.ShapeDtypeStruct((t, d), F32), jax.ShapeDtypeStruct((nb, 2, d), F32),
                   jax.ShapeDtypeStruct((1, d), F32)],
        in_specs=[rd, rd, cf, cf, _mod_spec(d), _const((1, d)), _const((f, d)), _const((f, d))],
        out_specs=[rd, pl.BlockSpec((1, 2, d), lambda b, i: (b, 0, 0)), _full((1, d))],
        args=(dxo, x, da_t, db_t, mod, g, w1t, w3t))


def nn_matmul(lhs_t, rhs, name, carried=None):
    m, t = lhs_t.shape
    n = rhs.shape[1]
    tk = _tile(t, 2048)
    tmm = m if m <= 1536 else m // 2
    nk = t // tk

    def body(a_ref, b_ref, o_ref, acc_ref):
        k = pl.program_id(1)
        _zero_when(k == 0, acc_ref)
        acc_ref[...] += _dot(a_ref[...], b_ref[...])

        @pl.when(k == nk - 1)
        def _():
            o_ref[...] = acc_ref[...].astype(BF16)

    return _call(
        body, name=name, grid=(m // tmm, nk), carried=carried,
        out_shape=[jax.ShapeDtypeStruct((m, n), BF16)],
        in_specs=[pl.BlockSpec((tmm, tk), lambda j, k: (j, k)), pl.BlockSpec((tk, n), lambda j, k: (k, 0))],
        out_specs=[pl.BlockSpec((tmm, n), lambda j, k: (j, 0))],
        scratch_shapes=[pltpu.VMEM((tmm, n), F32)],
        args=(lhs_t, rhs))


def tn_matmul(lhs, rhs, name):
    t, m = lhs.shape
    n = rhs.shape[1]
    tk = _tile(t, 2048)
    nk = t // tk

    def body(a_ref, b_ref, o_ref, acc_ref):
        k = pl.program_id(0)
        _zero_when(k == 0, acc_ref)
        acc_ref[...] += _dot(a_ref[...], b_ref[...], TN)

        @pl.when(k == nk - 1)
        def _():
            o_ref[...] = acc_ref[...].astype(BF16)

    return pl.pallas_call(
        body, name=name, grid=(nk,),
        out_shape=jax.ShapeDtypeStruct((m, n), BF16),
        in_specs=[pl.BlockSpec((tk, m), lambda k: (k, 0)), pl.BlockSpec((tk, n), lambda k: (k, 0))],
        out_specs=pl.BlockSpec((m, n), lambda k: (0, 0)),
        scratch_shapes=[pltpu.VMEM((m, n), F32)],
        compiler_params=_params(("arbitrary",)),
    )(lhs, rhs)


def mixer_proj_forward(x, mod, g, w_in_t, cw, sw, nb, carried=None):
    t, d = x.shape
    s = t // nb
    tm = _tile(s, 512)
    pieces = [(0, cw, "bf16"), (cw, cw, "bf16"), (2 * cw, cw, "bf16"), (3 * cw, sw, "f32"),
              (3 * cw + sw, d, "sig"), (3 * cw + sw + d, d, "sig")]

    def body(x_ref, mod_ref, g_ref, w_ref, *outs):
        h_ref = outs[-1]
        sh, sc, _ = _mod_rows(mod_ref, 1)
        h, _, _ = _norm_mod(x_ref[...], g_ref[...], sh, sc)
        hb = h.astype(BF16)
        h_ref[...] = hb
        for (off, width, kind), o_ref in zip(pieces, outs[:-1]):
            ck = _tile(width, 512)
            for j in range(width // ck):
                p = _dot(hb, w_ref[off + j * ck:off + (j + 1) * ck, :], NT)
                if kind == "sig":
                    p = jax.nn.sigmoid(p)
                o_ref[:, j * ck:(j + 1) * ck] = p.astype(o_ref.dtype)

    tps = s // tm
    widths = [(cw, BF16), (cw, BF16), (cw, BF16), (sw, F32), (d, BF16), (d, BF16), (d, BF16)]
    return _call(
        body, name="mixer_proj_forward", grid=(nb, tps), carried=carried,
        out_shape=[jax.ShapeDtypeStruct((t, w), dt) for w, dt in widths],
        in_specs=[_row_spec(tm, d, tps), _mod_spec(d), _const((1, d)), _const(w_in_t.shape)],
        out_specs=[_row_spec(tm, w, tps) for w, _ in widths],
        args=(x, mod, g, w_in_t))


def mixer_proj_backward(dgb, dgc, dv, du, dgla, dglb, dxo, x, mod, g, w_in_t, nb, carried=None):
    t, d = x.shape
    s = t // nb
    tm = _tile(s, 512)
    parts = [dgb, dgc, dv, du, dgla, dglb]
    offs = [0]
    for p in parts:
        offs.append(offs[-1] + p.shape[1])

    def body(*refs):
        p_refs = refs[:6]
        dxo_ref, x_ref, mod_ref, g_ref, w_ref, dx_ref, dmod_ref, dg_ref = refs[6:]
        bi, i = pl.program_id(0), pl.program_id(1)
        _zero_when(i == 0, dmod_ref)
        _zero_when(jnp.logical_and(bi == 0, i == 0), dg_ref)
        dh = jnp.zeros((tm, d), F32)
        for p_ref, off in zip(p_refs, offs):
            width = p_ref.shape[1]
            ck = _tile(width, 512)
            for j in range(width // ck):
                dh = dh + _dot(p_ref[:, j * ck:(j + 1) * ck], w_ref[off + j * ck:off + (j + 1) * ck, :])
        sh, sc, _ = _mod_rows(mod_ref, 1)
        gv = g_ref[...]
        _, n, r = _norm_mod(x_ref[...], gv, sh, sc)
        dxn, dsh, dsc, dg = _norm_mod_bwd(dh, n, r, gv, sc)
        dx_ref[...] = dxo_ref[...] + dxn
        dmod_ref[...] += jnp.concatenate([dsh, dsc], axis=0)[None]
        dg_ref[...] += dg

    tps = s // tm
    rd = _row_spec(tm, d, tps)
    return _call(
        body, name="mixer_proj_backward", grid=(nb, tps), carried=carried,
        out_shape=[jax.ShapeDtypeStruct((t, d), F32), jax.ShapeDtypeStruct((nb, 2, d), F32),
                   jax.ShapeDtypeStruct((1, d), F32)],
        in_specs=[_row_spec(tm, p.shape[1], tps) for p in parts]
        + [rd, rd, _mod_spec(d), _const((1, d)), _const(w_in_t.shape)],
        out_specs=[rd, pl.BlockSpec((1, 2, d), lambda b, i: (b, 0, 0)), _full((1, d))],
        args=(*parts, dxo, x, mod, g, w_in_t))


GROUPS_PER_BLOCK = 8
ROWS = 8
SCAN_LANES = 512


def _scan_rows(xr, xi, masks, shifts):
    for (mr, mi), sft in zip(masks, shifts):
        sr, si = pltpu.roll(xr, sft, 0), pltpu.roll(xi, sft, 0)
        xr, xi = xr + mr * sr - mi * si, xi + mr * si + mi * sr
    return xr, xi


def _cmul_add(ar, ai, cr, ci, br, bi):
    return ar * cr - ai * ci + br, ar * ci + ai * cr + bi


def _segment_rows(perm_ref, x):
    return _dot(perm_ref[0], x).astype(BF16)


def _time_rows(perm_ref, x):
    hi = x.astype(BF16)
    lo = (x - hi.astype(F32)).astype(BF16)
    return _dot(perm_ref[1], hi) + _dot(perm_ref[1], lo)


def segment_permutation(tc):
    r = jnp.arange(tc)
    p = (r[:, None] % ROWS * (tc // ROWS) + r[:, None] // ROWS == r[None, :]).astype(BF16)
    return jnp.stack([p, p.T])


def _rows_at(k, offset=0):
    return pl.ds(pl.multiple_of(k * ROWS + offset, ROWS), ROWS)


def ssm_forward(u, bd, cd, a1, ml, nb):
    t, w = u.shape
    s = t // nb
    tc = _tile(s, 256)
    seg = tc // ROWS
    nq, ub, lq = bd.shape[1], bd.shape[2], bd.shape[3]
    nl = nq * lq
    nch = s // tc
    lw = min(nl, SCAN_LANES)

    def body(u_ref, perm_ref, bd_ref, cd_ref, a1_ref, ml_ref, y_ref, st_ref, xr_s, xi_s, car_s):
        i = pl.program_id(1)

        @pl.when(i == 0)
        def _():
            car_s[...] = jnp.zeros_like(car_s)

        ub16 = _segment_rows(perm_ref, u_ref[...].astype(BF16))
        for q in range(nq):
            lanes = slice(q * lq, (q + 1) * lq)
            uq = ub16[:, q * ub:(q + 1) * ub]
            xr_s[:, lanes] = _dot(uq, bd_ref[0, q])
            xi_s[:, lanes] = _dot(uq, bd_ref[1, q])
        row_is_0 = lax.broadcasted_iota(jnp.int32, (ROWS, lw), 0) == 0
        zero = jnp.zeros((ROWS, lw), F32)
        for j in range(nl // lw):
            lanes = slice(j * lw, (j + 1) * lw)
            ar, ai = a1_ref[0, :, lanes], a1_ref[1, :, lanes]

            def local(k, c):
                return _cmul_add(ar, ai, c[0], c[1], xr_s[_rows_at(k), lanes], xi_s[_rows_at(k), lanes])

            er, ei = lax.fori_loop(0, seg, local, (zero, zero))
            masks = [(ml_ref[d, 0, :, lanes], ml_ref[d, 1, :, lanes]) for d in range(3)]
            cr, ci = _scan_rows(jnp.where(row_is_0, car_s[0, :, lanes], pltpu.roll(er, 1, 0)),
                                jnp.where(row_is_0, car_s[1, :, lanes], pltpu.roll(ei, 1, 0)), masks, (1, 2, 4))
            st_ref[0, 0, :, lanes] = cr
            st_ref[0, 1, :, lanes] = ci

            def full(k, c):
                xr, xi = _cmul_add(ar, ai, c[0], c[1], xr_s[_rows_at(k), lanes], xi_s[_rows_at(k), lanes])
                xr_s[_rows_at(k), lanes] = xr
                xi_s[_rows_at(k), lanes] = xi
                return xr, xi

            fr, fi = lax.fori_loop(0, seg, full, (cr, ci))
            car_s[0, :, lanes] = jnp.broadcast_to(fr[ROWS - 1:ROWS], fr.shape)
            car_s[1, :, lanes] = jnp.broadcast_to(fi[ROWS - 1:ROWS], fi.shape)
        y = jnp.concatenate([_dot(xr_s[:, q * lq:(q + 1) * lq].astype(BF16), cd_ref[0, q])
                             + _dot(xi_s[:, q * lq:(q + 1) * lq].astype(BF16), cd_ref[1, q]) for q in range(nq)],
                            axis=1)
        y_ref[...] = _time_rows(perm_ref, y)

    perm = segment_permutation(tc)
    return pl.pallas_call(
        body, name="ssm_forward", grid=(nb, nch),
        out_shape=[jax.ShapeDtypeStruct((t, w), F32), jax.ShapeDtypeStruct((nb * nch, 2, ROWS, nl), F32)],
        in_specs=[pl.BlockSpec((tc, w), lambda b, i: (b * nch + i, 0)), _const(perm.shape), _const(bd.shape),
                  _const(cd.shape), _const(a1.shape), _const(ml.shape)],
        out_specs=[pl.BlockSpec((tc, w), lambda b, i: (b * nch + i, 0)),
                   pl.BlockSpec((1, 2, ROWS, nl), lambda b, i: (b * nch + i, 0, 0, 0))],
        scratch_shapes=[pltpu.VMEM((tc, nl), F32), pltpu.VMEM((tc, nl), F32), pltpu.VMEM((2, ROWS, nl), F32)],
        compiler_params=_params(("arbitrary", "arbitrary")),
    )(u, perm, bd, cd, a1, ml)


def ssm_backward(u, dy, st, bd, cd, a1, mlb, dskip, nb, carried=None):
    t, w = u.shape
    s = t // nb
    tc = _tile(s, 256)
    seg = tc // ROWS
    nq, ub, lq = bd.shape[1], bd.shape[2], bd.shape[3]
    nl = nq * lq
    nch = s // tc
    lw = min(nl, SCAN_LANES)

    def body(u_ref, dy_ref, st_ref, perm_ref, bd_ref, cd_ref, a1_ref, mlb_ref, dsk_ref,
             du_ref, dab_ref, dbd_ref, dcd_ref, xr_s, xi_s, lr_s, li_s, car_s):
        bi, i = pl.program_id(0), pl.program_id(1)
        first = jnp.logical_and(bi == 0, i == 0)

        @pl.when(i == 0)
        def _():
            car_s[...] = jnp.zeros_like(car_s)

        @pl.when(first)
        def _():
            dab_ref[...] = jnp.zeros_like(dab_ref)
            dbd_ref[...] = jnp.zeros_like(dbd_ref)
            dcd_ref[...] = jnp.zeros_like(dcd_ref)

        ub16 = _segment_rows(perm_ref, u_ref[...].astype(BF16))
        dyb16 = _segment_rows(perm_ref, dy_ref[...].astype(BF16))
        xr_s[0:ROWS, :] = st_ref[0, 0]
        xi_s[0:ROWS, :] = st_ref[0, 1]
        for q in range(nq):
            lanes = slice(q * lq, (q + 1) * lq)
            uq = ub16[:, q * ub:(q + 1) * ub]
            dq = dyb16[:, q * ub:(q + 1) * ub]
            xr_s[ROWS:, lanes] = _dot(uq, bd_ref[0, q])
            xi_s[ROWS:, lanes] = _dot(uq, bd_ref[1, q])
            lr_s[:, lanes] = _dot(dq, cd_ref[0, q], NT)
            li_s[:, lanes] = _dot(dq, cd_ref[1, q], NT)
        row_is_7 = lax.broadcasted_iota(jnp.int32, (ROWS, lw), 0) == ROWS - 1
        zero = jnp.zeros((ROWS, lw), F32)
        for j in range(nl // lw):
            lanes = slice(j * lw, (j + 1) * lw)
            ar, ai = a1_ref[0, :, lanes], a1_ref[1, :, lanes]
            nai = -ai

            def states(k, c):
                xr, xi = _cmul_add(ar, ai, c[0], c[1], xr_s[_rows_at(k, ROWS), lanes], xi_s[_rows_at(k, ROWS), lanes])
                xr_s[_rows_at(k, ROWS), lanes] = xr
                xi_s[_rows_at(k, ROWS), lanes] = xi
                return xr, xi

            lax.fori_loop(0, seg, states, (st_ref[0, 0, :, lanes], st_ref[0, 1, :, lanes]))

            def local(kk, c):
                k = seg - 1 - kk
                return _cmul_add(ar, nai, c[0], c[1], lr_s[_rows_at(k), lanes], li_s[_rows_at(k), lanes])

            er, ei = lax.fori_loop(0, seg, local, (zero, zero))
            masks = [(mlb_ref[d, 0, :, lanes], mlb_ref[d, 1, :, lanes]) for d in range(3)]
            cr, ci = _scan_rows(jnp.where(row_is_7, car_s[0, :, lanes], pltpu.roll(er, ROWS - 1, 0)),
                                jnp.where(row_is_7, car_s[1, :, lanes], pltpu.roll(ei, ROWS - 1, 0)), masks, (7, 6, 4))

            def full(kk, c):
                cr_, ci_, accr, acci = c
                k = seg - 1 - kk
                lr, li = _cmul_add(ar, nai, cr_, ci_, lr_s[_rows_at(k), lanes], li_s[_rows_at(k), lanes])
                lr_s[_rows_at(k), lanes] = lr
                li_s[_rows_at(k), lanes] = li
                xpr, xpi = xr_s[_rows_at(k), lanes], xi_s[_rows_at(k), lanes]
                return lr, li, accr + lr * xpr + li * xpi, acci + li * xpr - lr * xpi

            lr0, li0, accr, acci = lax.fori_loop(0, seg, full, (cr, ci, zero, zero))
            car_s[0, :, lanes] = jnp.broadcast_to(lr0[0:1], lr0.shape)
            car_s[1, :, lanes] = jnp.broadcast_to(li0[0:1], li0.shape)
            dab_ref[0, :, lanes] += accr
            dab_ref[1, :, lanes] += acci
        du_parts = []
        for q in range(nq):
            lanes = slice(q * lq, (q + 1) * lq)
            cols = slice(q * ub, (q + 1) * ub)
            lrb, lib = lr_s[:, lanes].astype(BF16), li_s[:, lanes].astype(BF16)
            uq, dq = ub16[:, cols], dyb16[:, cols]
            du_parts.append(_dot(lrb, bd_ref[0, q], NT) + _dot(lib, bd_ref[1, q], NT))
            dbd_ref[0, q] += _dot(uq, lrb, TN)
            dbd_ref[1, q] += _dot(uq, lib, TN)
            dcd_ref[0, q] += _dot(xr_s[ROWS:, lanes].astype(BF16), dq, TN)
            dcd_ref[1, q] += _dot(xi_s[ROWS:, lanes].astype(BF16), dq, TN)
        du = _time_rows(perm_ref, jnp.concatenate(du_parts, axis=1)) + dsk_ref[...] * dy_ref[...]
        du_ref[...] = du.astype(BF16)

    rev = lambda b, i: (b * nch + nch - 1 - i, 0)
    perm = segment_permutation(tc)
    return _call(
        body, name="ssm_backward", grid=(nb, nch), carried=carried,
        out_shape=[jax.ShapeDtypeStruct((t, w), BF16), jax.ShapeDtypeStruct((2, ROWS, nl), F32),
                   jax.ShapeDtypeStruct(bd.shape, F32), jax.ShapeDtypeStruct(cd.shape, F32)],
        in_specs=[pl.BlockSpec((tc, w), rev), pl.BlockSpec((tc, w), rev),
                  pl.BlockSpec((1, 2, ROWS, nl), lambda b, i: (b * nch + nch - 1 - i, 0, 0, 0)),
                  _const(perm.shape), _const(bd.shape), _const(cd.shape), _const(a1.shape), _const(mlb.shape),
                  _const((1, w))],
        out_specs=[pl.BlockSpec((tc, w), rev), _full((2, ROWS, nl)), _full(bd.shape), _full(cd.shape)],
        scratch_shapes=[pltpu.VMEM((tc + ROWS, nl), F32), pltpu.VMEM((tc + ROWS, nl), F32),
                        pltpu.VMEM((tc, nl), F32), pltpu.VMEM((tc, nl), F32), pltpu.VMEM((2, ROWS, nl), F32)],
        args=(u, dy, st, perm, bd, cd, a1, mlb, dskip))


def ssm_discretise(a_re, a_im, b_re, b_im, log_dt):
    dt = jnp.exp(log_dt)[:, None]
    er = jnp.exp(a_re * dt)
    abr, abi = er * jnp.cos(a_im * dt), er * jnp.sin(a_im * dt)
    den = a_re * a_re + a_im * a_im
    nr, ni = abr - 1.0, abi
    fr = ((nr * a_re + ni * a_im) / den)[..., None]
    fi = ((ni * a_re - nr * a_im) / den)[..., None]
    return abr, abi, fr * b_re - fi * b_im, fr * b_im + fi * b_re


def _complex_square(zr, zi):
    return zr * zr - zi * zi, 2.0 * zr * zi


def ssm_tables(abr, abi, bbr, bbi, c_re, c_im, seg):
    g, p, h = bbr.shape
    nq = g // GROUPS_PER_BLOCK
    zr, zi = abr.reshape(1, -1), abi.reshape(1, -1)
    a1 = jnp.stack([jnp.broadcast_to(zr, (ROWS, g * p)), jnp.broadcast_to(zi, (ROWS, g * p))])
    for _ in range(seg.bit_length() - 1):
        zr, zi = _complex_square(zr, zi)
    row = jnp.arange(ROWS)[:, None]
    ml, mlb = [], []
    for d in (1, 2, 4):
        ml.append(jnp.stack([jnp.where(row >= d, zr, 0.0), jnp.where(row >= d, zi, 0.0)]))
        mlb.append(jnp.stack([jnp.where(row + d < ROWS, zr, 0.0), jnp.where(row + d < ROWS, -zi, 0.0)]))
        zr, zi = _complex_square(zr, zi)
    eye = jnp.eye(GROUPS_PER_BLOCK, dtype=F32)

    def block_diag_in(bb):
        bq = bb.reshape(nq, GROUPS_PER_BLOCK, p, h)
        return jnp.einsum("qaph,ab->qahbp", bq, eye).reshape(nq, GROUPS_PER_BLOCK * h, GROUPS_PER_BLOCK * p)

    def block_diag_out(cc):
        cq = cc.reshape(nq, GROUPS_PER_BLOCK, h, p)
        return jnp.einsum("qahp,ab->qapbh", cq, eye).reshape(nq, GROUPS_PER_BLOCK * p, GROUPS_PER_BLOCK * h)

    bd = jnp.stack([block_diag_in(bbr), block_diag_in(bbi)]).astype(BF16)
    cd = jnp.stack([block_diag_out(c_re), block_diag_out(-c_im)]).astype(BF16)
    return bd, cd, a1, jnp.stack(ml), jnp.stack(mlb)


def ssm_table_grads(dab, dbd, dcd, g, p, h):
    nq = g // GROUPS_PER_BLOCK
    dabr, dabi = dab[0].sum(0).reshape(g, p), dab[1].sum(0).reshape(g, p)
    b5 = dbd.reshape(2, nq, GROUPS_PER_BLOCK, h, GROUPS_PER_BLOCK, p)
    dbb = jnp.einsum("rqahap->rqaph", b5).reshape(2, g, p, h)
    c5 = dcd.reshape(2, nq, GROUPS_PER_BLOCK, p, GROUPS_PER_BLOCK, h)
    dcc = jnp.einsum("rqapah->rqahp", c5).reshape(2, g, h, p)
    return dabr, dabi, dbb[0], dbb[1], dcc[0], -dcc[1]


HALO = 16


def _conv_inputs(gc_ref, v_ref, gch_ref, vh_ref, cv_s, i, tm):
    cv = gc_ref[...].astype(F32) * v_ref[...].astype(F32)
    halo = gch_ref[...].astype(F32) * vh_ref[...].astype(F32)
    cv_s[0:HALO, :] = jnp.where(i == 0, 0.0, halo)
    cv_s[HALO:, :] = cv
    return cv, cv_s[HALO - 1:HALO - 1 + tm, :], cv_s[HALO - 2:HALO - 2 + tm, :]


def _halo_spec(tm, width, tiles_per_seq):
    per = tm // HALO
    return pl.BlockSpec((HALO, width), lambda b, i: (jnp.maximum((b * tiles_per_seq + i) * per - 1, 0), 0))


def mixer_merge_forward(x, gb, gc, v, sga, sgb, yssm, u, mod, conv_w, dskip, wco, wglu, wso_t, wout, nb):
    t, d = x.shape
    cw, sw = gb.shape[1], u.shape[1]
    s = t // nb
    tm = _tile(s, 512)
    tps = s // tm

    def body(x_ref, gb_ref, gc_ref, v_ref, gch_ref, vh_ref, sga_ref, sgb_ref, ys_ref, u_ref, mod_ref, cw_ref,
             dsk_ref, wco_ref, wglu_ref, wso_ref, wout_ref, xo_ref, ya_ref, yb_ref, mix_ref, cv_s):
        i = pl.program_id(1)
        cv, cv1, cv2 = _conv_inputs(gc_ref, v_ref, gch_ref, vh_ref, cv_s, i, tm)
        w = cw_ref[...]
        conv = w[0:1] * cv2 + w[1:2] * cv1 + w[2:3] * cv
        ya = _dot((gb_ref[...].astype(F32) * conv).astype(BF16), wco_ref[...])
        s0 = ys_ref[...] + dsk_ref[...] * u_ref[...]
        s1, _ = _gelu(s0)
        z = _dot(s1.astype(BF16), wglu_ref[...])
        s2 = s1 * jax.nn.sigmoid(z)
        yb = _dot(s2.astype(BF16), wso_ref[...], NT)
        merged = sga_ref[...].astype(F32) * ya + sgb_ref[...].astype(F32) * yb
        mix = _dot(merged.astype(BF16), wout_ref[...])
        _, _, gt = _mod_rows(mod_ref, 1)
        xo_ref[...] = x_ref[...] + gt * mix
        ya_ref[...] = ya.astype(BF16)
        yb_ref[...] = yb.astype(BF16)
        mix_ref[...] = mix.astype(BF16)

    rd, rc, rw = _row_spec(tm, d, tps), _row_spec(tm, cw, tps), _row_spec(tm, sw, tps)
    hc = _halo_spec(tm, cw, tps)
    return pl.pallas_call(
        body, name="mixer_merge_forward", grid=(nb, tps),
        out_shape=[jax.ShapeDtypeStruct((t, d), F32)] + [jax.ShapeDtypeStruct((t, d), BF16)] * 3,
        in_specs=[rd, rc, rc, rc, hc, hc, rd, rd, rw, rw, _mod_spec(d), _const(conv_w.shape), _const((1, sw)),
                  _const(wco.shape), _const(wglu.shape), _const(wso_t.shape), _const(wout.shape)],
        out_specs=[rd, rd, rd, rd],
        scratch_shapes=[pltpu.VMEM((tm + HALO, cw), F32)],
        compiler_params=_params(("arbitrary", "arbitrary")),
    )(x, gb, gc, v, gc, v, sga, sgb, yssm, u, mod, conv_w, dskip, wco, wglu, wso_t, wout)


def mixer_merge_backward(dxo, mix, ya, yb, gb, gc, v, sga, sgb, yssm, u, mod, conv_w, dskip,
                         wco, wglu, wso_t, wout, nb, carried=None):
    t, d = dxo.shape
    cw, sw = gb.shape[1], u.shape[1]
    s = t // nb
    tm = _tile(s, 256)
    tps = s // tm

    def body(dxo_ref, mix_ref, ya_ref, yb_ref, gb_ref, gc_ref, v_ref, gch_ref, vh_ref, sga_ref, sgb_ref, ys_ref,
             u_ref, mod_ref, cw_ref, dsk_ref, wco_ref, wglu_ref, wso_ref, wout_ref,
             dgla_ref, dglb_ref, dgb_ref, dconv_ref, ds0_ref, dgt_ref, ddsk_ref, dwout_ref, dwco_ref, dwso_ref,
             dwglu_ref, cv_s):
        bi, i = pl.program_id(0), pl.program_id(1)
        _zero_when(i == 0, dgt_ref)
        _zero_when(jnp.logical_and(bi == 0, i == 0), ddsk_ref, dwout_ref, dwco_ref, dwso_ref, dwglu_ref)
        dxo = dxo_ref[...]
        _, _, gt = _mod_rows(mod_ref, 1)
        dmix = (gt * dxo).astype(BF16)
        dgt_ref[...] += jnp.sum(dxo * mix_ref[...].astype(F32), axis=0, keepdims=True)[None]
        ya, yb = ya_ref[...].astype(F32), yb_ref[...].astype(F32)
        sga, sgb = sga_ref[...].astype(F32), sgb_ref[...].astype(F32)
        merged = (sga * ya + sgb * yb).astype(BF16)
        dwout_ref[...] += _dot(merged, dmix, TN)
        dmerged = _dot(dmix, wout_ref[...], NT)
        dgla_ref[...] = (dmerged * ya * sga * (1.0 - sga)).astype(BF16)
        dglb_ref[...] = (dmerged * yb * sgb * (1.0 - sgb)).astype(BF16)
        dya = (dmerged * sga).astype(BF16)
        dyb = (dmerged * sgb).astype(BF16)
        cv, cv1, cv2 = _conv_inputs(gc_ref, v_ref, gch_ref, vh_ref, cv_s, i, tm)
        w = cw_ref[...]
        conv = w[0:1] * cv2 + w[1:2] * cv1 + w[2:3] * cv
        gbv = gb_ref[...].astype(F32)
        dwco_ref[...] += _dot((gbv * conv).astype(BF16), dya, TN)
        dya_in = _dot(dya, wco_ref[...], NT)
        dgb_ref[...] = (dya_in * conv).astype(BF16)
        dconv_ref[...] = dya_in * gbv
        uv = u_ref[...]
        s0 = ys_ref[...] + dsk_ref[...] * uv
        s1, th = _gelu(s0)
        s1b = s1.astype(BF16)
        sz = jax.nn.sigmoid(_dot(s1b, wglu_ref[...]))
        s2b = (s1 * sz).astype(BF16)
        dwso_ref[...] += _dot(dyb, s2b, TN)
        ds2 = _dot(dyb, wso_ref[...])
        dz = (ds2 * s1 * sz * (1.0 - sz)).astype(BF16)
        dwglu_ref[...] += _dot(s1b, dz, TN)
        ds1 = ds2 * sz + _dot(dz, wglu_ref[...], NT)
        ds0 = ds1 * _gelu_grad(s0, th)
        ds0_ref[...] = ds0
        ddsk_ref[...] += jnp.sum(ds0 * uv, axis=0, keepdims=True)

    rd, rc, rw = _row_spec(tm, d, tps), _row_spec(tm, cw, tps), _row_spec(tm, sw, tps)
    hc = _halo_spec(tm, cw, tps)
    return _call(
        body, name="mixer_merge_backward", grid=(nb, tps), carried=carried,
        out_shape=[jax.ShapeDtypeStruct((t, d), BF16), jax.ShapeDtypeStruct((t, d), BF16),
                   jax.ShapeDtypeStruct((t, cw), BF16), jax.ShapeDtypeStruct((t, cw), F32),
                   jax.ShapeDtypeStruct((t, sw), F32), jax.ShapeDtypeStruct((nb, 1, d), F32),
                   jax.ShapeDtypeStruct((1, sw), F32), jax.ShapeDtypeStruct(wout.shape, F32),
                   jax.ShapeDtypeStruct(wco.shape, F32), jax.ShapeDtypeStruct(wso_t.shape, F32),
                   jax.ShapeDtypeStruct(wglu.shape, F32)],
        in_specs=[rd, rd, rd, rd, rc, rc, rc, hc, hc, rd, rd, rw, rw, _mod_spec(d), _const(conv_w.shape),
                  _const((1, sw)), _const(wco.shape), _const(wglu.shape), _const(wso_t.shape), _const(wout.shape)],
        out_specs=[rd, rd, rc, rc, rw, pl.BlockSpec((1, 1, d), lambda b, i: (b, 0, 0)), _full((1, sw)),
                   _full(wout.shape), _full(wco.shape), _full(wso_t.shape), _full(wglu.shape)],
        scratch_shapes=[pltpu.VMEM((tm + HALO, cw), F32)],
        args=(dxo, mix, ya, yb, gb, gc, v, gc, v, sga, sgb, yssm, u, mod, conv_w, dskip, wco, wglu, wso_t, wout))


def conv_backward(dconv, gc, v, conv_w, nb):
    t, cw = dconv.shape
    s = t // nb
    tm = _tile(s, 512)
    tps = s // tm
    per = tm // ROWS
    slab = _tile(tm, 32)

    def body(dc_ref, dcn_ref, gc_ref, v_ref, gch_ref, vh_ref, cw_ref, dgc_ref, dv_ref, dw_ref, cv_s, dc_s):
        bi, i = pl.program_id(0), pl.program_id(1)
        _zero_when(jnp.logical_and(bi == 0, i == 0), dw_ref)
        cv_s[0:HALO, :] = jnp.where(i == 0, 0.0, gch_ref[...].astype(F32) * vh_ref[...].astype(F32))
        dc_s[tm:, :] = jnp.where(i == tps - 1, 0.0, dcn_ref[...])
        for r in range(0, tm, slab):
            cv_s[HALO + r:HALO + r + slab, :] = (gc_ref[r:r + slab, :].astype(F32)
                                                 * v_ref[r:r + slab, :].astype(F32))
            dc_s[r:r + slab, :] = dc_ref[r:r + slab, :]
        w = cw_ref[...]
        sums = [jnp.zeros((ROWS, cw), F32)] * CONV_K
        for r in range(0, tm, slab):
            dc = dc_s[r:r + slab, :]
            dcv = w[2:3] * dc + w[1:2] * dc_s[r + 1:r + 1 + slab, :] + w[0:1] * dc_s[r + 2:r + 2 + slab, :]
            dgc_ref[r:r + slab, :] = (dcv * v_ref[r:r + slab, :].astype(F32)).astype(BF16)
            dv_ref[r:r + slab, :] = (dcv * gc_ref[r:r + slab, :].astype(F32)).astype(BF16)
            for k in range(CONV_K):
                lag = HALO + r - (CONV_K - 1 - k)
                prod = dc * cv_s[lag:lag + slab, :]
                sums[k] = sums[k] + jnp.sum(prod.reshape(slab // ROWS, ROWS, cw), axis=0)
        dw_ref[...] += jnp.concatenate([jnp.sum(a, axis=0, keepdims=True) for a in sums], axis=0)

    rc = _row_spec(tm, cw, tps)
    nxt = pl.BlockSpec((ROWS, cw), lambda b, i: (jnp.minimum((b * tps + i + 1) * per, nb * tps * per - 1), 0))
    hc = _halo_spec(tm, cw, tps)
    return pl.pallas_call(
        body, name="conv_backward", grid=(nb, tps),
        out_shape=[jax.ShapeDtypeStruct((t, cw), BF16), jax.ShapeDtypeStruct((t, cw), BF16),
                   jax.ShapeDtypeStruct(conv_w.shape, F32)],
        in_specs=[rc, nxt, rc, rc, hc, hc, _full(conv_w.shape)],
        out_specs=[rc, rc, _full(conv_w.shape)],
        scratch_shapes=[pltpu.VMEM((tm + HALO, cw), F32), pltpu.VMEM((tm + ROWS, cw), F32)],
        compiler_params=_params(("arbitrary", "arbitrary")),
    )(dconv, dconv, gc, v, gc, v, conv_w)


def loss_forward_backward(x, target, g):
    t, d = x.shape
    tm = _tile(t, 512)

    def body(x_ref, t_ref, g_ref, l_ref, dx_ref, dg_ref):
        _zero_when(pl.program_id(0) == 0, dg_ref)
        xv = x_ref[...]
        gv = g_ref[...]
        r = lax.rsqrt(jnp.mean(xv * xv, axis=-1, keepdims=True) + EPS)
        n = xv * r
        err = n * gv - t_ref[...]
        l_ref[...] = jnp.full(l_ref.shape, 0.5 * jnp.sum(jnp.mean(err * err, axis=-1)), F32)
        dy = err * (1.0 / d)
        dn = dy * gv
        dx_ref[...] = r * (dn - n * jnp.mean(n * dn, axis=-1, keepdims=True))
        dg_ref[...] += jnp.sum(dy * n, axis=0, keepdims=True)

    row = pl.BlockSpec((tm, d), lambda i: (i, 0))
    return pl.pallas_call(
        body, name="loss_forward_backward", grid=(t // tm,),
        out_shape=[jax.ShapeDtypeStruct((t // tm, 1, 128), F32), jax.ShapeDtypeStruct((t, d), F32),
                   jax.ShapeDtypeStruct((1, d), F32)],
        in_specs=[row, row, _full((1, d))],
        out_specs=[pl.BlockSpec((1, 1, 128), lambda i: (i, 0, 0)), row, _full((1, d))],
        compiler_params=_params(("arbitrary",)),
    )(x, target, g)


def sum_slots(slots, name):
    _, r, c = slots.shape
    tr = _tile(r, 352) if r % 352 == 0 else _tile(r, 256)
    if tr < 128:
        tr = r

    def body(s_ref, o_ref):
        acc = s_ref[0].astype(F32)
        for j in range(1, N_DEV):
            acc = acc + s_ref[j].astype(F32)
        o_ref[...] = acc

    return pl.pallas_call(
        body, name=name, grid=(r // tr,),
        out_shape=jax.ShapeDtypeStruct((r, c), F32),
        in_specs=[pl.BlockSpec((N_DEV, tr, c), lambda i: (0, i, 0))],
        out_specs=pl.BlockSpec((tr, c), lambda i: (i, 0)),
        compiler_params=_params(("arbitrary",)),
    )(slots)


def adamw_update(w, g, m, v, name):
    r, c = w.shape
    tr = _tile(r, 256) if r % 8 == 0 else r

    def body(w_ref, g_ref, m_ref, v_ref, d_ref, mo_ref, vo_ref):
        d_ref[...], mo_ref[...], vo_ref[...] = _adamw(w_ref[...], g_ref[...], m_ref[...], v_ref[...])

    spec = pl.BlockSpec((tr, c), lambda i: (i, 0))
    return pl.pallas_call(
        body, name=name, grid=(r // tr,),
        out_shape=[jax.ShapeDtypeStruct((r, c), F32)] * 3,
        in_specs=[spec] * 4, out_specs=[spec] * 3,
        compiler_params=_params(("arbitrary",)),
    )(w, g, m, v)


def sum_adamw_update(slots, w, m, v, name):
    _, r, c = slots.shape
    tr = _tile(r, 352) if r % 352 == 0 else _tile(r, 256)

    def body(s_ref, w_ref, m_ref, v_ref, g_ref, d_ref, mo_ref, vo_ref):
        g = s_ref[0].astype(F32)
        for j in range(1, N_DEV):
            g = g + s_ref[j].astype(F32)
        g_ref[...] = g
        d_ref[...], mo_ref[...], vo_ref[...] = _adamw(w_ref[...], g, m_ref[...], v_ref[...])

    spec = pl.BlockSpec((tr, c), lambda i: (i, 0))
    return pl.pallas_call(
        body, name=name, grid=(r // tr,),
        out_shape=[jax.ShapeDtypeStruct((r, c), F32)] * 4,
        in_specs=[pl.BlockSpec((N_DEV, tr, c), lambda i: (0, i, 0))] + [spec] * 3, out_specs=[spec] * 4,
        compiler_params=_params(("arbitrary",)),
    )(slots, w, m, v)


def adamw_update_small(ws, gs, ms, vs):
    n = len(ws)

    def body(*refs):
        w_r, g_r, m_r, v_r = refs[:n], refs[n:2 * n], refs[2 * n:3 * n], refs[3 * n:4 * n]
        d_r, mo_r, vo_r = refs[4 * n:5 * n], refs[5 * n:6 * n], refs[6 * n:7 * n]
        for k in range(n):
            d_r[k][...], mo_r[k][...], vo_r[k][...] = _adamw(w_r[k][...], g_r[k][...], m_r[k][...], v_r[k][...])

    shapes = [jax.ShapeDtypeStruct(w.shape, F32) for w in ws]
    out = pl.pallas_call(body, name="adamw_update_small", out_shape=shapes * 3,
                         compiler_params=_params())(*ws, *gs, *ms, *vs)
    return out[:n], out[n:2 * n], out[2 * n:]


def _slots(grad_t):
    return grad_t.reshape(N_DEV, grad_t.shape[0] // N_DEV, grad_t.shape[1])


def kernel(x, c, w_ada, b_ada, g_ffn1, w1_a, w3_a, w2_a, g_mix, w_in, conv_w, w_conv_out, a_re, a_im, b_re, b_im, c_re, c_im, log_dt, d_skip, w_glu, w_ssm_out, w_out, g_ffn2, w1_b, w3_b, w2_b, g_final, loss_target, m_w_ada, m_b_ada, m_g_ffn1, m_w1_a, m_w3_a, m_w2_a, m_g_mix, m_w_in, m_conv_w, m_w_conv_out, m_a_re, m_a_im, m_b_re, m_b_im, m_c_re, m_c_im, m_log_dt, m_d_skip, m_w_glu, m_w_ssm_out, m_w_out, m_g_ffn2, m_w1_b, m_w3_b, m_w2_b, m_g_final, v_w_ada, v_b_ada, v_g_ffn1, v_w1_a, v_w3_a, v_w2_a, v_g_mix, v_w_in, v_conv_w, v_w_conv_out, v_a_re, v_a_im, v_b_re, v_b_im, v_c_re, v_c_im, v_log_dt, v_d_skip, v_w_glu, v_w_ssm_out, v_w_out, v_g_ffn2, v_w1_b, v_w3_b, v_w2_b, v_g_final):
    nb, s, d = x.shape
    t = nb * s
    me = 4 * lax.axis_index("x") + 2 * lax.axis_index("y") + lax.axis_index("c")
    g_n, p_n, h_n = b_re.shape[1:]
    cw_n = w_conv_out.shape[1] * N_DEV
    sw_n = w_ssm_out.shape[1]
    glu_fold = d // w_glu.shape[2]

    weights = dict(w_ada=w_ada, b_ada=b_ada, g_ffn1=g_ffn1, w1_a=w1_a, w3_a=w3_a, w2_a=w2_a, g_mix=g_mix, w_in=w_in,
                   conv_w=conv_w, w_conv_out=w_conv_out, a_re=a_re, a_im=a_im, b_re=b_re, b_im=b_im, c_re=c_re,
                   c_im=c_im, log_dt=log_dt, d_skip=d_skip, w_glu=w_glu, w_ssm_out=w_ssm_out, w_out=w_out,
                   g_ffn2=g_ffn2, w1_b=w1_b, w3_b=w3_b, w2_b=w2_b, g_final=g_final)
    mom1 = dict(w_ada=m_w_ada, b_ada=m_b_ada, g_ffn1=m_g_ffn1, w1_a=m_w1_a, w3_a=m_w3_a, w2_a=m_w2_a, g_mix=m_g_mix,
                w_in=m_w_in, conv_w=m_conv_w, w_conv_out=m_w_conv_out, a_re=m_a_re, a_im=m_a_im, b_re=m_b_re,
                b_im=m_b_im, c_re=m_c_re, c_im=m_c_im, log_dt=m_log_dt, d_skip=m_d_skip, w_glu=m_w_glu,
                w_ssm_out=m_w_ssm_out, w_out=m_w_out, g_ffn2=m_g_ffn2, w1_b=m_w1_b, w3_b=m_w3_b, w2_b=m_w2_b,
                g_final=m_g_final)
    mom2 = dict(w_ada=v_w_ada, b_ada=v_b_ada, g_ffn1=v_g_ffn1, w1_a=v_w1_a, w3_a=v_w3_a, w2_a=v_w2_a, g_mix=v_g_mix,
                w_in=v_w_in, conv_w=v_conv_w, w_conv_out=v_w_conv_out, a_re=v_a_re, a_im=v_a_im, b_re=v_b_re,
                b_im=v_b_im, c_re=v_c_re, c_im=v_c_im, log_dt=v_log_dt, d_skip=v_d_skip, w_glu=v_w_glu,
                w_ssm_out=v_w_ssm_out, w_out=v_w_out, g_ffn2=v_g_ffn2, w1_b=v_w1_b, w3_b=v_w3_b, w2_b=v_w2_b,
                g_final=v_g_final)
    names = list(weights)
    transposed = ("w1_a", "w3_a", "w_in", "w_ssm_out", "w1_b", "w3_b")
    groups = dict(ffn_a=("w1_a", "w3_a", "w2_a"), mixer=("w_in", "w_conv_out", "w_glu", "w_ssm_out", "w_out"),
                  ffn_b=("w1_b", "w3_b", "w2_b"))
    big = groups["ffn_a"] + groups["mixer"] + groups["ffn_b"]

    def shard_rows(name):
        w = weights[name][0]
        if name in transposed:
            w = w.T
        if name == "w_glu":
            w = w.reshape(w.shape[0] // glu_fold, d)
        return w.astype(BF16)

    def gather_stage(group):
        return GatherStage([shard_rows(n) for n in groups[group]])

    gw = {}

    def keep_weights(group, outs):
        for n, w in zip(groups[group], outs):
            gw[n] = w.reshape(sw_n, sw_n) if n == "w_glu" else w

    pad_rows = lambda a: jnp.pad(a, ((0, -a.shape[0] % GATHER_ROWS), (0, 0)))
    c_all, conv_all, *ffn_a_weights = run_stage(
        StageGroup([GatherStage([pad_rows(c), pad_rows(conv_w[0])]), gather_stage("ffn_a")]), "gather_cond_ffn_a")
    keep_weights("ffn_a", ffn_a_weights)
    c_all = c_all.reshape(N_DEV, -1, d)[:, :nb].reshape(N_DEV * nb, d)
    conv_full = conv_all.reshape(N_DEV, GATHER_ROWS, -1)[:, :CONV_K].transpose(1, 0, 2).reshape(CONV_K, cw_n)
    ada_cols = w_ada.shape[2]
    b_cols = lax.dynamic_slice(b_ada, (0, me * ada_cols), (1, ada_cols))
    mod_cols = ada_forward(c_all, w_ada[0], b_cols)
    (mod_all,) = run_stage(GatherStage([mod_cols]), "gather_mod")
    mod_mine = lax.dynamic_slice(mod_all.reshape(N_DEV, N_DEV * nb, ada_cols), (0, me * nb, 0), (N_DEV, nb, ada_cols))
    mod = mod_mine.transpose(1, 0, 2).reshape(nb, N_MOD, d)

    disc_in = (a_re[0], a_im[0], b_re[0], b_im[0], log_dt[0])
    (abr, abi, bbr, bbi), disc_vjp = jax.vjp(ssm_discretise, *disc_in)
    bd, cd, abar8, ml, mlb = ssm_tables(abr, abi, bbr, bbi, c_re[0], c_im[0], _tile(s, 256) // ROWS)

    x0 = x.reshape(t, d)
    (x1, a1, b1, f1, h1), got = ffn_forward(x0, mod, g_ffn1, gw["w1_a"], gw["w3_a"], gw["w2_a"], 0, nb,
                                            "ffn_a_forward", carried=gather_stage("mixer"))
    keep_weights("mixer", got)
    (gb, gc, vv, u, sga, sgb, h2), got = mixer_proj_forward(x1, mod, g_mix, gw["w_in"], cw_n, sw_n, nb,
                                                            carried=gather_stage("ffn_b"))
    keep_weights("ffn_b", got)
    yssm, st = ssm_forward(u, bd, cd, abar8, ml, nb)
    x2, ya, yb, mix = mixer_merge_forward(x1, gb, gc, vv, sga, sgb, yssm, u, mod, conv_full, d_skip,
                                          gw["w_conv_out"], gw["w_glu"], gw["w_ssm_out"], gw["w_out"], nb)
    (x3, a3, b3, f3, h3), _ = ffn_forward(x2, mod, g_ffn2, gw["w1_b"], gw["w3_b"], gw["w2_b"], 2, nb,
                                          "ffn_b_forward")
    loss_parts, dx3, dg_final = loss_forward_backward(x3, loss_target.reshape(t, d), g_final.reshape(1, d))
    loss = lax.psum(jnp.sum(loss_parts[:, 0, 0]), ("x", "y", "c"))

    part, received = {}, {}

    def exchange_stage(ns):
        return ExchangeStage([_slots(part[n]) for n in ns])

    (da3, db3, sw3, df3, dgt3), _ = ffn_backward_hidden(dx3, a3, b3, f3, mod, gw["w2_b"], 2, nb,
                                                        "ffn_b_backward_hidden")
    (dx2, dmod3, dg_ffn2), _ = ffn_backward_input(dx3, x2, da3, db3, mod, g_ffn2, gw["w1_b"], gw["w3_b"],
                                                  2, nb, "ffn_b_backward_input")
    (part["w1_b"],), _ = nn_matmul(da3, h3, "grad_w1_b")
    (part["w3_b"],), _ = nn_matmul(db3, h3, "grad_w3_b")
    (part["w2_b"],), _ = nn_matmul(sw3, df3, "grad_w2_b")
    (dgla, dglb, dgb, dconv, ds0, dgt2, dd_skip, dw_out, dw_co, dw_so_t, dw_glu), got = mixer_merge_backward(
        dx2, mix, ya, yb, gb, gc, vv, sga, sgb, yssm, u, mod, conv_full, d_skip,
        gw["w_conv_out"], gw["w_glu"], gw["w_ssm_out"], gw["w_out"], nb, carried=exchange_stage(groups["ffn_b"]))
    received.update(zip(groups["ffn_b"], got))
    part["w_out"] = dw_out.astype(BF16)
    part["w_conv_out"] = dw_co.astype(BF16)
    part["w_ssm_out"] = dw_so_t.astype(BF16)
    part["w_glu"] = dw_glu.reshape(sw_n // glu_fold, d).astype(BF16)
    (du, dab, dbd, dcd), got = ssm_backward(u, ds0, st, bd, cd, abar8, mlb, d_skip, nb,
                                            carried=exchange_stage(groups["mixer"][1:]))
    received.update(zip(groups["mixer"][1:], got))
    dgc, dvv, dconv_w = conv_backward(dconv, gc, vv, conv_full, nb)
    part["w_in"] = jnp.concatenate([tn_matmul(p, h2, "grad_w_in_%d" % k)
                                    for k, p in enumerate((dgb, dgc, dvv, du, dgla, dglb))], axis=0)
    (dx1, dmod2, dg_mix), got = mixer_proj_backward(dgb, dgc, dvv, du, dgla, dglb, dx2, x1, mod, g_mix, gw["w_in"], nb,
                                                    carried=exchange_stage(("w_in",)))
    received["w_in"] = got[0]
    def pack(parts):
        flat = jnp.concatenate([a.reshape(-1) for a in parts.values()])
        rows = -(-flat.shape[0] // (128 * GATHER_ROWS)) * GATHER_ROWS
        return jnp.pad(flat, (0, rows * 128 - flat.shape[0])).reshape(rows, 128)

    def unpack(flat, parts):
        out, off = {}, 0
        for key, like in parts.items():
            n = math.prod(like.shape)
            out[key], off = flat[..., off:off + n].reshape(flat.shape[:-1] + like.shape), off + n
        return out

    dabr, dabi, dbbr, dbbi, dcr, dci = ssm_table_grads(dab, dbd, dcd, g_n, p_n, h_n)
    early = dict(gmod=jnp.concatenate([dmod2, dgt2, dmod3, dgt3], axis=1), g_mix=dg_mix, g_ffn2=dg_ffn2,
                 g_final=dg_final, d_skip=dd_skip, abr=dabr, abi=dabi, bbr=dbbr, bbi=dbbi, c_re=dcr, c_im=dci,
                 conv_w=dconv_w)
    (da1, db1, sw1, df1, dgt1), _ = ffn_backward_hidden(dx1, a1, b1, f1, mod, gw["w2_a"], 0, nb,
                                                        "ffn_a_backward_hidden")
    (part["w2_a"],), (early_all,) = nn_matmul(sw1, df1, "grad_w2_a", carried=GatherStage([pack(early)]))
    (part["w1_a"],), got = nn_matmul(da1, h1, "grad_w1_a", carried=exchange_stage(("w2_a",)))
    received["w2_a"] = got[0]
    (part["w3_a"],), got = nn_matmul(db1, h1, "grad_w3_a", carried=exchange_stage(("w1_a",)))
    received["w1_a"] = got[0]
    (dx0, dmod1, dg_ffn1), got = ffn_backward_input(dx1, x0, da1, db1, mod, g_ffn1, gw["w1_a"], gw["w3_a"],
                                                    0, nb, "ffn_a_backward_input", carried=exchange_stage(("w3_a",)))
    received["w3_a"] = got[0]
    late = dict(gmod=jnp.concatenate([dmod1, dgt1], axis=1), g_ffn1=dg_ffn1)
    (late_all,) = run_stage(GatherStage([pack(late)]), "gather_ffn_a_small_grads")

    tot, per_dev = {}, {}
    for parts, gathered, name in ((early, early_all, "early"), (late, late_all, "late")):
        slots = gathered.reshape(N_DEV, -1, 128)
        total = sum_slots(slots, "sum_small_grads_" + name).reshape(-1)
        for key, val in unpack(total, parts).items():
            tot[name + key if key == "gmod" else key] = val
        per_dev[name] = unpack(slots.reshape(N_DEV, -1), parts)["gmod"]
    gmod_all = jnp.concatenate([per_dev["late"], per_dev["early"]], axis=2).reshape(N_DEV * nb, N_MOD * d)
    gmod_tot = jnp.concatenate([tot["lategmod"], tot["earlygmod"]], axis=1)
    g_a_re, g_a_im, g_b_re, g_b_im, g_log_dt = disc_vjp((tot["abr"], tot["abi"], tot["bbr"], tot["bbi"]))

    grads = {}
    grads["b_ada"] = sum_rows(gmod_tot.reshape(nb, N_MOD * d))
    grads["g_ffn1"], grads["g_mix"], grads["g_ffn2"] = tot["g_ffn1"], tot["g_mix"], tot["g_ffn2"]
    grads["g_final"] = tot["g_final"].reshape(d)
    grads["d_skip"] = tot["d_skip"]
    grads["a_re"], grads["a_im"], grads["log_dt"] = g_a_re[None], g_a_im[None], g_log_dt[None]
    grads["b_re"], grads["b_im"] = g_b_re[None], g_b_im[None]
    grads["c_re"], grads["c_im"] = tot["c_re"][None], tot["c_im"][None]
    grads["conv_w"] = lax.dynamic_slice(tot["conv_w"], (0, me * conv_w.shape[2]), (CONV_K, conv_w.shape[2]))[None]

    delta, new_m, new_v = {}, {}, {}
    for name in big:
        wmv = (weights[name][0], mom1[name][0], mom2[name][0])
        if name in transposed:
            outs = sum_adamw_update(received[name], *[a.T for a in wmv], "adamw_" + name)
            gsum, dl, mm, vn = [a.T for a in outs]
        elif received[name].shape[1:] == weights[name].shape[1:]:
            gsum, dl, mm, vn = sum_adamw_update(received[name], *wmv, "adamw_" + name)
        else:
            gsum = sum_slots(received[name], "sum_" + name).reshape(weights[name].shape[1:])
            dl, mm, vn = adamw_update(*wmv[:1], gsum, *wmv[1:], "adamw_" + name)
        grads[name] = gsum[None]
        delta[name], new_m[name], new_v[name] = dl[None], mm[None], vn[None]

    gmod_cols = lax.dynamic_slice(gmod_all, (0, me * ada_cols), (N_DEV * nb, ada_cols))
    g_wada, d_wada, m_wada, v_wada = ada_backward_update(c_all, gmod_cols, w_ada[0], m_w_ada[0], v_w_ada[0])
    grads["w_ada"], delta["w_ada"], new_m["w_ada"], new_v["w_ada"] = g_wada[None], d_wada[None], m_wada[None], v_wada[None]

    small_names = [n for n in names if n not in big and n != "w_ada"]

    narrow = ("b_re", "b_im")

    def as2d(n, a):
        a = jnp.swapaxes(a.reshape(weights[n].shape), -1, -2) if n in narrow else a
        return a.reshape(-1, a.shape[-1])

    def from2d(n, a):
        shape = weights[n].shape
        return jnp.swapaxes(a.reshape(shape[:-2] + (shape[-1], shape[-2])), -1, -2) if n in narrow else a.reshape(shape)

    sw_, sg_, sm_, sv_ = ([as2d(n, src[n]) for n in small_names] for src in (weights, grads, mom1, mom2))
    sd, smo, svo = adamw_update_small(sw_, sg_, sm_, sv_)
    for n, dl, mm, vn in zip(small_names, sd, smo, svo):
        grads[n] = grads[n].reshape(weights[n].shape)
        delta[n], new_m[n], new_v[n] = from2d(n, dl), from2d(n, mm), from2d(n, vn)

    grad_x = dx0.reshape(nb, s, d)
    return (loss, grad_x, *[grads[n] for n in names], *[delta[n] for n in names],
            *[new_m[n] for n in names], *[new_v[n] for n in names])


def sum_rows(a):
    r, c = a.shape

    def body(a_ref, o_ref):
        acc = a_ref[0:1, :]
        for j in range(1, r):
            acc = acc + a_ref[j:j + 1, :]
        o_ref[...] = acc

    return pl.pallas_call(body, name="sum_rows", out_shape=jax.ShapeDtypeStruct((1, c), F32),
                          compiler_params=_params())(a)
```

```python
import functools
import math

import jax
import jax.numpy as jnp
from jax import lax
from jax.experimental import pallas as pl
from jax.experimental.pallas import tpu as pltpu

F32 = jnp.float32
BF16 = jnp.bfloat16
N_DEV = 8
N_MOD = 9
EPS = 1e-6
CONV_K = 3
ADAM_LR = 0.001
ADAM_B1 = 0.9
ADAM_B2 = 0.999
ADAM_EPS = 1e-08
ADAM_WD = 0.01
ADAM_STEP = 10
GELU_C0 = math.sqrt(2.0 / math.pi)
GELU_C1 = 0.044715
V7X_VMEM_LIMIT = 56 * 1024 * 1024
MESH_ID = pl.DeviceIdType.MESH
NT = (((1,), (1,)), ((), ()))
TN = (((0,), (0,)), ((), ()))


def _dot(a, b, dims=None):
    if dims is None:
        return jnp.dot(a, b, preferred_element_type=F32)
    return lax.dot_general(a, b, dims, preferred_element_type=F32)


def _params(sem=None, vmem=V7X_VMEM_LIMIT):
    return pltpu.CompilerParams(dimension_semantics=sem, vmem_limit_bytes=vmem)


def _full(shape):
    return pl.BlockSpec(shape, lambda *_: (0,) * len(shape))


def _const(shape):
    return pl.BlockSpec(shape, lambda *_: (0,) * len(shape), pipeline_mode=pl.Buffered(1))


def _tile(n, want):
    t = min(n, want)
    while n % t:
        t //= 2
    return t


class GatherStage:
    COPIES = 9

    def __init__(self, shards):
        n = len(shards)
        self.inputs = list(shards)
        self.out_shape = [jax.ShapeDtypeStruct((N_DEV * s.shape[0], s.shape[1]), s.dtype) for s in shards]
        self.scratch = [pltpu.SemaphoreType.DMA((self.COPIES * n,)), pltpu.SemaphoreType.DMA((self.COPIES * n,)),
                        pltpu.SemaphoreType.DMA((n,))]

    def _plan(self, ins, outs, sems):
        send_sems, recv_sems, local_sems = sems
        n = len(ins)
        x, y, c = lax.axis_index("x"), lax.axis_index("y"), lax.axis_index("c")
        me, sibling, xn, yn, dg = (x, y, c), (x, y, 1 - c), (1 - x, y, c), (x, 1 - y, c), (1 - x, 1 - y, c)

        def rows(k, block, half=None):
            r = ins[k].shape[0]
            px, py, pc = block
            base = (4 * px + 2 * py + pc) * r
            if half is None:
                return outs[k].at[pl.ds(base, r), :]
            return outs[k].at[pl.ds(base + half * (r // 2), r // 2), :]

        def copy(k, j, block, to, half=None, src=None):
            return pltpu.make_async_remote_copy(
                src_ref=rows(k, block, half) if src is None else src, dst_ref=rows(k, block, half),
                send_sem=send_sems.at[self.COPIES * k + j], recv_sem=recv_sems.at[self.COPIES * k + j],
                device_id=to, device_id_type=MESH_ID)

        sib = lambda b: (b[0], b[1], 1 - b[2])
        mine = [pltpu.make_async_copy(ins[k], rows(k, me), local_sems.at[k]) for k in range(n)]
        first = [(0, me, sibling, None, sibling), (1, me, xn, None, xn), (2, me, yn, None, yn)]
        second = [(3, xn, yn, 0, dg), (4, yn, xn, 1, dg), (5, xn, sibling, None, sib(xn)), (6, yn, sibling, None, sib(yn))]
        third = [(7, dg, sibling, 0, sib(dg)), (8, dg, sibling, 1, sib(dg))]
        return n, me, copy, mine, first, second, third

    def start(self, ins, outs, sems):
        n, me, copy, mine, first, _, _ = self._plan(ins, outs, sems)
        for cp in mine:
            cp.start()
        for k in range(n):
            for j, block, to, half, _ in first:
                copy(k, j, block, to, half, src=ins[k]).start()

    def advance(self, ins, outs, sems):
        n, me, copy, mine, first, second, third = self._plan(ins, outs, sems)
        for k in range(n):
            copy(k, 1, first[1][4], me).wait_recv()
            copy(k, 2, first[2][4], me).wait_recv()
            for j, block, to, half, _ in second:
                copy(k, j, block, to, half).start()

    def advance_again(self, ins, outs, sems):
        n, me, copy, mine, first, second, third = self._plan(ins, outs, sems)
        for k in range(n):
            copy(k, 3, second[0][4], me, 0).wait_recv()
            copy(k, 4, second[1][4], me, 1).wait_recv()
            for j, block, to, half, _ in third:
                copy(k, j, block, to, half).start()

    def finish(self, ins, outs, sems):
        n, me, copy, mine, first, second, third = self._plan(ins, outs, sems)
        arrived = lambda k, j, block, half: copy(k, j, block, me, half).wait_recv()
        for k in range(n):
            arrived(k, 0, first[0][4], None)
            arrived(k, 5, second[2][4], None)
            arrived(k, 6, second[3][4], None)
            arrived(k, 7, third[0][4], 0)
            arrived(k, 8, third[1][4], 1)
        for k in range(n):
            for j, block, to, half, _ in first:
                copy(k, j, block, to, half, src=ins[k]).wait_send()
            for j, block, to, half, _ in second + third:
                copy(k, j, block, to, half).wait_send()
        for cp in mine:
            cp.wait()


class ExchangeStage:
    def __init__(self, bufs):
        n = len(bufs)
        self.inputs = list(bufs)
        self.out_shape = [jax.ShapeDtypeStruct(b.shape, b.dtype) for b in bufs]
        self.scratch = [pltpu.SemaphoreType.DMA((7 * n,)), pltpu.SemaphoreType.DMA((7 * n,)),
                        pltpu.SemaphoreType.DMA((n,))]

    def _plan(self, ins, outs, sems):
        send_sems, recv_sems, local_sems = sems
        n = len(ins)
        x, y, c = lax.axis_index("x"), lax.axis_index("y"), lax.axis_index("c")
        me = 4 * x + 2 * y + c
        mine = [pltpu.make_async_copy(ins[k].at[me], outs[k].at[me], local_sems.at[k]) for k in range(n)]
        copies = []
        for mask in range(1, N_DEV):
            px, py, pc = x ^ (mask >> 2), y ^ ((mask >> 1) & 1), c ^ (mask & 1)
            for k in range(n):
                copies.append(pltpu.make_async_remote_copy(
                    src_ref=ins[k].at[4 * px + 2 * py + pc], dst_ref=outs[k].at[me],
                    send_sem=send_sems.at[7 * k + mask - 1], recv_sem=recv_sems.at[7 * k + mask - 1],
                    device_id=(px, py, pc), device_id_type=MESH_ID))
        return mine, copies

    def start(self, ins, outs, sems):
        mine, copies = self._plan(ins, outs, sems)
        for cp in mine + copies:
            cp.start()

    def advance(self, ins, outs, sems):
        pass

    advance_again = advance

    def finish(self, ins, outs, sems):
        mine, copies = self._plan(ins, outs, sems)
        for cp in copies:
            cp.wait_recv()
        for cp in copies:
            cp.wait_send()
        for cp in mine:
            cp.wait()


ANY_SPEC = pl.BlockSpec(memory_space=pl.ANY)
GATHER_ROWS = 16


class StageGroup:
    def __init__(self, stages):
        self.stages = list(stages)
        self.inputs = [a for s in stages for a in s.inputs]
        self.out_shape = [o for s in stages for o in s.out_shape]
        self.scratch = [t for s in stages for t in s.scratch]

    def _parts(self, ins, outs, sems):
        i = o = t = 0
        for s in self.stages:
            ni, no, nt = len(s.inputs), len(s.out_shape), len(s.scratch)
            yield s, ins[i:i + ni], outs[o:o + no], sems[t:t + nt]
            i, o, t = i + ni, o + no, t + nt

    def start(self, ins, outs, sems):
        for s, i_, o_, t_ in self._parts(ins, outs, sems):
            s.start(i_, o_, t_)

    def advance(self, ins, outs, sems):
        for s, i_, o_, t_ in self._parts(ins, outs, sems):
            s.advance(i_, o_, t_)

    def advance_again(self, ins, outs, sems):
        for s, i_, o_, t_ in self._parts(ins, outs, sems):
            s.advance_again(i_, o_, t_)

    def finish(self, ins, outs, sems):
        for s, i_, o_, t_ in self._parts(ins, outs, sems):
            s.finish(i_, o_, t_)


def run_stage(stage, name):
    ci, co = len(stage.inputs), len(stage.out_shape)

    def body(*refs):
        ins, outs, sems = refs[:ci], refs[ci:ci + co], refs[ci + co:]
        stage.start(ins, outs, sems)
        stage.advance(ins, outs, sems)
        stage.advance_again(ins, outs, sems)
        stage.finish(ins, outs, sems)

    return pl.pallas_call(body, name=name, out_shape=stage.out_shape, in_specs=[ANY_SPEC] * ci,
                          out_specs=[ANY_SPEC] * co, scratch_shapes=stage.scratch)(*stage.inputs)


def _call(body, *, name, grid, in_specs, out_specs, out_shape, args, scratch_shapes=(), carried=None):
    sem = ("arbitrary",) * len(grid)
    if carried is None:
        return pl.pallas_call(body, name=name, grid=grid, in_specs=list(in_specs), out_specs=list(out_specs),
                              out_shape=list(out_shape), scratch_shapes=list(scratch_shapes),
                              compiler_params=_params(sem))(*args), None
    ni, no, ns = len(in_specs), len(out_shape), len(scratch_shapes)
    ci, co = len(carried.inputs), len(carried.out_shape)
    n_steps = math.prod(grid)

    def wrapped(*refs):
        ins, refs = refs[:ni], refs[ni:]
        cins, refs = refs[:ci], refs[ci:]
        outs, refs = refs[:no], refs[no:]
        couts, refs = refs[:co], refs[co:]
        scr, csems = refs[:ns], refs[ns:]
        step = functools.reduce(lambda acc, ig: acc * ig[1] + ig[0],
                                zip([pl.program_id(k) for k in range(len(grid))], grid), 0)

        @pl.when(step == 0)
        def _():
            carried.start(cins, couts, csems)

        @pl.when(step == n_steps // 2)
        def _():
            carried.advance(cins, couts, csems)

        @pl.when(step == (3 * n_steps) // 4)
        def _():
            carried.advance_again(cins, couts, csems)

        body(*ins, *outs, *scr)

        @pl.when(step == n_steps - 1)
        def _():
            carried.finish(cins, couts, csems)

    res = pl.pallas_call(
        wrapped, name=name, grid=grid, in_specs=list(in_specs) + [ANY_SPEC] * ci,
        out_specs=list(out_specs) + [ANY_SPEC] * co, out_shape=list(out_shape) + carried.out_shape,
        scratch_shapes=list(scratch_shapes) + carried.scratch, compiler_params=_params(sem),
    )(*args, *carried.inputs)
    return res[:no], res[no:]


def _norm_mod(x, g, shift, scale):
    r = lax.rsqrt(jnp.mean(x * x, axis=-1, keepdims=True) + EPS)
    n = x * r
    return (n * g) * (1.0 + scale) + shift, n, r


def _norm_mod_bwd(dh, n, r, g, scale):
    dsh = jnp.sum(dh, axis=0, keepdims=True)
    dsc = jnp.sum(dh * (n * g), axis=0, keepdims=True)
    dg = jnp.sum(dh * (1.0 + scale) * n, axis=0, keepdims=True)
    dn = dh * ((1.0 + scale) * g)
    dx = r * (dn - n * jnp.mean(n * dn, axis=-1, keepdims=True))
    return dx, dsh, dsc, dg


def _mod_rows(mod_ref, sub):
    m = mod_ref[0]
    return m[3 * sub:3 * sub + 1], m[3 * sub + 1:3 * sub + 2], m[3 * sub + 2:3 * sub + 3]


def _gelu(x):
    t = jnp.tanh(GELU_C0 * (x + GELU_C1 * x * x * x))
    return 0.5 * x * (1.0 + t), t


def _gelu_grad(x, t):
    return 0.5 * (1.0 + t) + 0.5 * x * (1.0 - t * t) * (GELU_C0 * (1.0 + 3.0 * GELU_C1 * x * x))


def _zero_when(cond, *refs):
    @pl.when(cond)
    def _():
        for r in refs:
            r[...] = jnp.zeros_like(r)


def ada_forward(c_all, w_ada, b_ada_cols):
    def body(c_ref, w_ref, b_ref, o_ref):
        c = c_ref[...]
        cond = (c * jax.nn.sigmoid(c)).astype(BF16)
        o_ref[...] = _dot(cond, w_ref[...].astype(BF16)) + b_ref[...]

    nb, d = c_all.shape
    cols = w_ada.shape[1]
    tn = _tile(cols, 384)
    return pl.pallas_call(
        body, name="ada_forward", grid=(cols // tn,),
        out_shape=jax.ShapeDtypeStruct((nb, cols), F32),
        in_specs=[_full((nb, d)), pl.BlockSpec((d, tn), lambda j: (0, j)), pl.BlockSpec((1, tn), lambda j: (0, j))],
        out_specs=pl.BlockSpec((nb, tn), lambda j: (0, j)),
        compiler_params=_params(("arbitrary",)),
    )(c_all, w_ada, b_ada_cols)


def _adamw(w, g, m, v):
    m = ADAM_B1 * m + (1.0 - ADAM_B1) * g
    v = ADAM_B2 * v + (1.0 - ADAM_B2) * (g * g)
    m_hat = m / (1.0 - ADAM_B1 ** ADAM_STEP)
    v_hat = v / (1.0 - ADAM_B2 ** ADAM_STEP)
    delta = -ADAM_LR * (m_hat / (jnp.sqrt(v_hat) + ADAM_EPS) + ADAM_WD * w)
    return delta, m, v


def ada_backward_update(c_all, gmod_cols, w, m, v):
    def body(c_ref, g_ref, w_ref, m_ref, v_ref, go_ref, d_ref, mo_ref, vo_ref):
        c = c_ref[...]
        cond = (c * jax.nn.sigmoid(c)).astype(BF16)
        g = _dot(cond, g_ref[...].astype(BF16), TN)
        go_ref[...] = g
        d_ref[...], mo_ref[...], vo_ref[...] = _adamw(w_ref[...], g, m_ref[...], v_ref[...])

    nb, d = c_all.shape
    cols = w.shape[1]
    tn = _tile(cols, 128)
    col = pl.BlockSpec((d, tn), lambda j: (0, j))
    return pl.pallas_call(
        body, name="ada_backward_update", grid=(cols // tn,),
        out_shape=[jax.ShapeDtypeStruct(w.shape, F32)] * 4,
        in_specs=[_full((nb, d)), pl.BlockSpec((nb, tn), lambda j: (0, j)), col, col, col],
        out_specs=[col] * 4,
        compiler_params=_params(("arbitrary",)),
    )(c_all, gmod_cols, w, m, v)


def _row_spec(tm, width, tiles_per_seq):
    return pl.BlockSpec((tm, width), lambda b, i: (b * tiles_per_seq + i, 0))


def _mod_spec(d):
    return pl.BlockSpec((1, N_MOD, d), lambda b, i: (b, 0, 0))


def _col_spec(rows, tm, tiles_per_seq):
    return pl.BlockSpec((rows, tm), lambda b, i: (0, b * tiles_per_seq + i))


def _ffn_chunk(f):
    return f // 4 if f % 512 == 0 and f > 1536 else f


def ffn_forward(x, mod, g, w1t, w3t, w2, sub, nb, name, carried=None):
    t, d = x.shape
    f = w1t.shape[0]
    s = t // nb
    tm = _tile(s, 512)
    fc = _ffn_chunk(f)

    def body(x_ref, mod_ref, g_ref, w1_ref, w3_ref, w2_ref, xo_ref, a_ref, b_ref, f_ref, h_ref):
        xv = x_ref[...]
        sh, sc, gt = _mod_rows(mod_ref, sub)
        h, _, _ = _norm_mod(xv, g_ref[...], sh, sc)
        hb = h.astype(BF16)
        h_ref[...] = hb
        acc_t = jnp.zeros((d, tm), F32)
        for k in range(f // fc):
            rows = slice(k * fc, (k + 1) * fc)
            a = _dot(w1_ref[rows, :], hb, NT)
            b = _dot(w3_ref[rows, :], hb, NT)
            a_ref[rows, :] = a.astype(BF16)
            b_ref[rows, :] = b.astype(BF16)
            sw = (a * jax.nn.sigmoid(a)) * b
            acc_t = acc_t + _dot(w2_ref[rows, :], sw.astype(BF16), TN)
        acc = acc_t.T
        f_ref[...] = acc.astype(BF16)
        xo_ref[...] = xv + (0.5 * gt) * acc

    tps = s // tm
    rd, cf = _row_spec(tm, d, tps), _col_spec(f, tm, tps)
    return _call(
        body, name=name, grid=(nb, tps), carried=carried,
        out_shape=[jax.ShapeDtypeStruct((t, d), F32)] + [jax.ShapeDtypeStruct((f, t), BF16)] * 2
        + [jax.ShapeDtypeStruct((t, d), BF16)] * 2,
        in_specs=[rd, _mod_spec(d), _const((1, d)), _const((f, d)), _const((f, d)), _const((f, d))],
        out_specs=[rd, cf, cf, rd, rd],
        args=(x, mod, g, w1t, w3t, w2))


def ffn_backward_hidden(dxo, a_t, b_t, fo, mod, w2, sub, nb, name, carried=None):
    t, d = dxo.shape
    f = w2.shape[0]
    s = t // nb
    tm = _tile(s, 512)
    fc = _tile(f, 704) if f % 704 == 0 else _tile(f, 512)

    def body(dxo_ref, a_ref, b_ref, f_ref, mod_ref, w2_ref, da_ref, db_ref, s_ref, df_ref, dgt_ref):
        _zero_when(pl.program_id(1) == 0, dgt_ref)
        dxo = dxo_ref[...]
        _, _, gt = _mod_rows(mod_ref, sub)
        dfb = ((0.5 * gt) * dxo).astype(BF16)
        df_ref[...] = dfb
        dgt_ref[...] += 0.5 * jnp.sum(dxo * f_ref[...].astype(F32), axis=0, keepdims=True)[None]
        for k in range(f // fc):
            rows = slice(k * fc, (k + 1) * fc)
            ds = _dot(w2_ref[rows, :], dfb, NT).astype(BF16)
            av = a_ref[rows, :].astype(F32)
            bv = b_ref[rows, :]
            sig = jax.nn.sigmoid(av)
            sl = av * sig
            slb = sl.astype(BF16)
            da_ref[rows, :] = ds * bv * (sig + sl * (1.0 - sig)).astype(BF16)
            db_ref[rows, :] = ds * slb
            s_ref[rows, :] = slb * bv

    tps = s // tm
    rd, cf = _row_spec(tm, d, tps), _col_spec(f, tm, tps)
    return _call(
        body, name=name, grid=(nb, tps), carried=carried,
        out_shape=[jax.ShapeDtypeStruct((f, t), BF16)] * 3
        + [jax.ShapeDtypeStruct((t, d), BF16), jax.ShapeDtypeStruct((nb, 1, d), F32)],
        in_specs=[rd, cf, cf, rd, _mod_spec(d), _const((f, d))],
        out_specs=[cf, cf, cf, rd, pl.BlockSpec((1, 1, d), lambda b, i: (b, 0, 0))],
        args=(dxo, a_t, b_t, fo, mod, w2))


def ffn_backward_input(dxo, x, da_t, db_t, mod, g, w1t, w3t, sub, nb, name, carried=None):
    t, d = x.shape
    f = w1t.shape[0]
    s = t // nb
    tm = _tile(s, 512)

    def body(dxo_ref, x_ref, da_ref, db_ref, mod_ref, g_ref, w1_ref, w3_ref, dx_ref, dmod_ref, dg_ref):
        bi, i = pl.program_id(0), pl.program_id(1)
        _zero_when(i == 0, dmod_ref)
        _zero_when(jnp.logical_and(bi == 0, i == 0), dg_ref)
        sh, sc, _ = _mod_rows(mod_ref, sub)
        gv = g_ref[...]
        _, n, r = _norm_mod(x_ref[...], gv, sh, sc)
        dh_t = _dot(w1_ref[...], da_ref[...], TN) + _dot(w3_ref[...], db_ref[...], TN)
        dxn, dsh, dsc, dg = _norm_mod_bwd(dh_t.T, n, r, gv, sc)
        dx_ref[...] = dxo_ref[...] + dxn
        dmod_ref[...] += jnp.concatenate([dsh, dsc], axis=0)[None]
        dg_ref[...] += dg

    tps = s // tm
    rd, cf = _row_spec(tm, d, tps), _col_spec(f, tm, tps)
    return _call(
        body, name=name, grid=(nb, tps), carried=carried,
        out_shape=[jax.ShapeDtypeStruct((t, d), F32), jax.ShapeDtypeStruct((nb, 2, d), F32),
                   jax.ShapeDtypeStruct((1, d), F32)],
        in_specs=[rd, rd, cf, cf, _mod_spec(d), _const((1, d)), _const((f, d)), _const((f, d))],
        out_specs=[rd, pl.BlockSpec((1, 2, d), lambda b, i: (b, 0, 0)), _full((1, d))],
        args=(dxo, x, da_t, db_t, mod, g, w1t, w3t))


def nn_matmul(lhs_t, rhs, name, carried=None):
    m, t = lhs_t.shape
    n = rhs.shape[1]
    tk = _tile(t, 2048)
    tmm = m if m <= 1536 else m // 2
    nk = t // tk

    def body(a_ref, b_ref, o_ref, acc_ref):
        k = pl.program_id(1)
        _zero_when(k == 0, acc_ref)
        acc_ref[...] += _dot(a_ref[...], b_ref[...])

        @pl.when(k == nk - 1)
        def _():
            o_ref[...] = acc_ref[...].astype(BF16)

    return _call(
        body, name=name, grid=(m // tmm, nk), carried=carried,
        out_shape=[jax.ShapeDtypeStruct((m, n), BF16)],
        in_specs=[pl.BlockSpec((tmm, tk), lambda j, k: (j, k)), pl.BlockSpec((tk, n), lambda j, k: (k, 0))],
        out_specs=[pl.BlockSpec((tmm, n), lambda j, k: (j, 0))],
        scratch_shapes=[pltpu.VMEM((tmm, n), F32)],
        args=(lhs_t, rhs))


def tn_matmul(lhs, rhs, name):
    t, m = lhs.shape
    n = rhs.shape[1]
    tk = _tile(t, 2048)
    nk = t // tk

    def body(a_ref, b_ref, o_ref, acc_ref):
        k = pl.program_id(0)
        _zero_when(k == 0, acc_ref)
        acc_ref[...] += _dot(a_ref[...], b_ref[...], TN)

        @pl.when(k == nk - 1)
        def _():
            o_ref[...] = acc_ref[...].astype(BF16)

    return pl.pallas_call(
        body, name=name, grid=(nk,),
        out_shape=jax.ShapeDtypeStruct((m, n), BF16),
        in_specs=[pl.BlockSpec((tk, m), lambda k: (k, 0)), pl.BlockSpec((tk, n), lambda k: (k, 0))],
        out_specs=pl.BlockSpec((m, n), lambda k: (0, 0)),
        scratch_shapes=[pltpu.VMEM((m, n), F32)],
        compiler_params=_params(("arbitrary",)),
    )(lhs, rhs)


def mixer_proj_forward(x, mod, g, w_in_t, cw, sw, nb, carried=None):
    t, d = x.shape
    s = t // nb
    tm = _tile(s, 512)
    pieces = [(0, cw, "bf16"), (cw, cw, "bf16"), (2 * cw, cw, "bf16"), (3 * cw, sw, "f32"),
              (3 * cw + sw, d, "sig"), (3 * cw + sw + d, d, "sig")]

    def body(x_ref, mod_ref, g_ref, w_ref, *outs):
        h_ref = outs[-1]
        sh, sc, _ = _mod_rows(mod_ref, 1)
        h, _, _ = _norm_mod(x_ref[...], g_ref[...], sh, sc)
        hb = h.astype(BF16)
        h_ref[...] = hb
        for (off, width, kind), o_ref in zip(pieces, outs[:-1]):
            ck = _tile(width, 512)
            for j in range(width // ck):
                p = _dot(hb, w_ref[off + j * ck:off + (j + 1) * ck, :], NT)
                if kind == "sig":
                    p = jax.nn.sigmoid(p)
                o_ref[:, j * ck:(j + 1) * ck] = p.astype(o_ref.dtype)

    tps = s // tm
    widths = [(cw, BF16), (cw, BF16), (cw, BF16), (sw, F32), (d, BF16), (d, BF16), (d, BF16)]
    return _call(
        body, name="mixer_proj_forward", grid=(nb, tps), carried=carried,
        out_shape=[jax.ShapeDtypeStruct((t, w), dt) for w, dt in widths],
        in_specs=[_row_spec(tm, d, tps), _mod_spec(d), _const((1, d)), _const(w_in_t.shape)],
        out_specs=[_row_spec(tm, w, tps) for w, _ in widths],
        args=(x, mod, g, w_in_t))


def mixer_proj_backward(dgb, dgc, dv, du, dgla, dglb, dxo, x, mod, g, w_in_t, nb, carried=None):
    t, d = x.shape
    s = t // nb
    tm = _tile(s, 512)
    parts = [dgb, dgc, dv, du, dgla, dglb]
    offs = [0]
    for p in parts:
        offs.append(offs[-1] + p.shape[1])

    def body(*refs):
        p_refs = refs[:6]
        dxo_ref, x_ref, mod_ref, g_ref, w_ref, dx_ref, dmod_ref, dg_ref = refs[6:]
        bi, i = pl.program_id(0), pl.program_id(1)
        _zero_when(i == 0, dmod_ref)
        _zero_when(jnp.logical_and(bi == 0, i == 0), dg_ref)
        dh = jnp.zeros((tm, d), F32)
        for p_ref, off in zip(p_refs, offs):
            width = p_ref.shape[1]
            ck = _tile(width, 512)
            for j in range(width // ck):
                dh = dh + _dot(p_ref[:, j * ck:(j + 1) * ck], w_ref[off + j * ck:off + (j + 1) * ck, :])
        sh, sc, _ = _mod_rows(mod_ref, 1)
        gv = g_ref[...]
        _, n, r = _norm_mod(x_ref[...], gv, sh, sc)
        dxn, dsh, dsc, dg = _norm_mod_bwd(dh, n, r, gv, sc)
        dx_ref[...] = dxo_ref[...] + dxn
        dmod_ref[...] += jnp.concatenate([dsh, dsc], axis=0)[None]
        dg_ref[...] += dg

    tps = s // tm
    rd = _row_spec(tm, d, tps)
    return _call(
        body, name="mixer_proj_backward", grid=(nb, tps), carried=carried,
        out_shape=[jax.ShapeDtypeStruct((t, d), F32), jax.ShapeDtypeStruct((nb, 2, d), F32),
                   jax.ShapeDtypeStruct((1, d), F32)],
        in_specs=[_row_spec(tm, p.shape[1], tps) for p in parts]
        + [rd, rd, _mod_spec(d), _const((1, d)), _const(w_in_t.shape)],
        out_specs=[rd, pl.BlockSpec((1, 2, d), lambda b, i: (b, 0, 0)), _full((1, d))],
        args=(*parts, dxo, x, mod, g, w_in_t))


GROUPS_PER_BLOCK = 8
ROWS = 8
SCAN_LANES = 512
SCAN_UNROLL = 8


def _scan_rows(xr, xi, masks, shifts):
    for (mr, mi), sft in zip(masks, shifts):
        sr, si = pltpu.roll(xr, sft, 0), pltpu.roll(xi, sft, 0)
        xr, xi = xr + mr * sr - mi * si, xi + mr * si + mi * sr
    return xr, xi


def _cmul_add(ar, ai, cr, ci, br, bi):
    return ar * cr - ai * ci + br, ar * ci + ai * cr + bi


def _segment_rows(perm_ref, x):
    return _dot(perm_ref[0], x).astype(BF16)


def _time_rows(perm_ref, x):
    hi = x.astype(BF16)
    lo = (x - hi.astype(F32)).astype(BF16)
    return _dot(perm_ref[1], hi) + _dot(perm_ref[1], lo)


def segment_permutation(tc):
    r = jnp.arange(tc)
    p = (r[:, None] % ROWS * (tc // ROWS) + r[:, None] // ROWS == r[None, :]).astype(BF16)
    return jnp.stack([p, p.T])


def _scan_loop(n, step, init):
    def trip(j, carry):
        for r in range(SCAN_UNROLL):
            carry = step(j * SCAN_UNROLL + r, carry)
        return carry

    return lax.fori_loop(0, n // SCAN_UNROLL, trip, init)


def _rows_at(k, offset=0):
    return pl.ds(pl.multiple_of(k * ROWS + offset, ROWS), ROWS)


def ssm_forward(u, bd, cd, a1, ml, nb):
    t, w = u.shape
    s = t // nb
    tc = _tile(s, 256)
    seg = tc // ROWS
    nq, ub, lq = bd.shape[1], bd.shape[2], bd.shape[3]
    nl = nq * lq
    nch = s // tc
    lw = min(nl, SCAN_LANES)

    def body(u_ref, perm_ref, bd_ref, cd_ref, a1_ref, ml_ref, y_ref, st_ref, xr_s, xi_s, car_s):
        i = pl.program_id(1)

        @pl.when(i == 0)
        def _():
            car_s[...] = jnp.zeros_like(car_s)

        ub16 = _segment_rows(perm_ref, u_ref[...].astype(BF16))
        for q in range(nq):
            lanes = slice(q * lq, (q + 1) * lq)
            uq = ub16[:, q * ub:(q + 1) * ub]
            xr_s[:, lanes] = _dot(uq, bd_ref[0, q])
            xi_s[:, lanes] = _dot(uq, bd_ref[1, q])
        row_is_0 = lax.broadcasted_iota(jnp.int32, (ROWS, lw), 0) == 0
        zero = jnp.zeros((ROWS, lw), F32)
        for j in range(nl // lw):
            lanes = slice(j * lw, (j + 1) * lw)
            ar, ai = a1_ref[0, :, lanes], a1_ref[1, :, lanes]

            def local(k, c):
                return _cmul_add(ar, ai, c[0], c[1], xr_s[_rows_at(k), lanes], xi_s[_rows_at(k), lanes])

            er, ei = _scan_loop(seg, local, (zero, zero))
            masks = [(ml_ref[d, 0, :, lanes], ml_ref[d, 1, :, lanes]) for d in range(3)]
            cr, ci = _scan_rows(jnp.where(row_is_0, car_s[0, :, lanes], pltpu.roll(er, 1, 0)),
                                jnp.where(row_is_0, car_s[1, :, lanes], pltpu.roll(ei, 1, 0)), masks, (1, 2, 4))
            st_ref[0, 0, :, lanes] = cr
            st_ref[0, 1, :, lanes] = ci

            def full(k, c):
                xr, xi = _cmul_add(ar, ai, c[0], c[1], xr_s[_rows_at(k), lanes], xi_s[_rows_at(k), lanes])
                xr_s[_rows_at(k), lanes] = xr
                xi_s[_rows_at(k), lanes] = xi
                return xr, xi

            fr, fi = _scan_loop(seg, full, (cr, ci))
            car_s[0, :, lanes] = jnp.broadcast_to(fr[ROWS - 1:ROWS], fr.shape)
            car_s[1, :, lanes] = jnp.broadcast_to(fi[ROWS - 1:ROWS], fi.shape)
        y = jnp.concatenate([_dot(xr_s[:, q * lq:(q + 1) * lq].astype(BF16), cd_ref[0, q])
                             + _dot(xi_s[:, q * lq:(q + 1) * lq].astype(BF16), cd_ref[1, q]) for q in range(nq)],
                            axis=1)
        y_ref[...] = _time_rows(perm_ref, y)

    perm = segment_permutation(tc)
    return pl.pallas_call(
        body, name="ssm_forward", grid=(nb, nch),
        out_shape=[jax.ShapeDtypeStruct((t, w), F32), jax.ShapeDtypeStruct((nb * nch, 2, ROWS, nl), F32)],
        in_specs=[pl.BlockSpec((tc, w), lambda b, i: (b * nch + i, 0)), _const(perm.shape), _const(bd.shape),
                  _const(cd.shape), _const(a1.shape), _const(ml.shape)],
        out_specs=[pl.BlockSpec((tc, w), lambda b, i: (b * nch + i, 0)),
                   pl.BlockSpec((1, 2, ROWS, nl), lambda b, i: (b * nch + i, 0, 0, 0))],
        scratch_shapes=[pltpu.VMEM((tc, nl), F32), pltpu.VMEM((tc, nl), F32), pltpu.VMEM((2, ROWS, nl), F32)],
        compiler_params=_params(("arbitrary", "arbitrary")),
    )(u, perm, bd, cd, a1, ml)


def ssm_backward(u, dy, st, bd, cd, a1, mlb, dskip, nb, carried=None):
    t, w = u.shape
    s = t // nb
    tc = _tile(s, 256)
    seg = tc // ROWS
    nq, ub, lq = bd.shape[1], bd.shape[2], bd.shape[3]
    nl = nq * lq
    nch = s // tc
    lw = min(nl, SCAN_LANES)

    def body(u_ref, dy_ref, st_ref, perm_ref, bd_ref, cd_ref, a1_ref, mlb_ref, dsk_ref,
             du_ref, dab_ref, dbd_ref, dcd_ref, xr_s, xi_s, lr_s, li_s, car_s):
        bi, i = pl.program_id(0), pl.program_id(1)
        first = jnp.logical_and(bi == 0, i == 0)

        @pl.when(i == 0)
        def _():
            car_s[...] = jnp.zeros_like(car_s)

        @pl.when(first)
        def _():
            dab_ref[...] = jnp.zeros_like(dab_ref)
            dbd_ref[...] = jnp.zeros_like(dbd_ref)
            dcd_ref[...] = jnp.zeros_like(dcd_ref)

        ub16 = _segment_rows(perm_ref, u_ref[...].astype(BF16))
        dyb16 = _segment_rows(perm_ref, dy_ref[...].astype(BF16))
        xr_s[0:ROWS, :] = st_ref[0, 0]
        xi_s[0:ROWS, :] = st_ref[0, 1]
        for q in range(nq):
            lanes = slice(q * lq, (q + 1) * lq)
            uq = ub16[:, q * ub:(q + 1) * ub]
            dq = dyb16[:, q * ub:(q + 1) * ub]
            xr_s[ROWS:, lanes] = _dot(uq, bd_ref[0, q])
            xi_s[ROWS:, lanes] = _dot(uq, bd_ref[1, q])
            lr_s[:, lanes] = _dot(dq, cd_ref[0, q], NT)
            li_s[:, lanes] = _dot(dq, cd_ref[1, q], NT)
        row_is_7 = lax.broadcasted_iota(jnp.int32, (ROWS, lw), 0) == ROWS - 1
        zero = jnp.zeros((ROWS, lw), F32)
        for j in range(nl // lw):
            lanes = slice(j * lw, (j + 1) * lw)
            ar, ai = a1_ref[0, :, lanes], a1_ref[1, :, lanes]
            nai = -ai

            def states(k, c):
                xr, xi = _cmul_add(ar, ai, c[0], c[1], xr_s[_rows_at(k, ROWS), lanes], xi_s[_rows_at(k, ROWS), lanes])
                xr_s[_rows_at(k, ROWS), lanes] = xr
                xi_s[_rows_at(k, ROWS), lanes] = xi
                return xr, xi

            _scan_loop(seg, states, (st_ref[0, 0, :, lanes], st_ref[0, 1, :, lanes]))

            def local(kk, c):
                k = seg - 1 - kk
                return _cmul_add(ar, nai, c[0], c[1], lr_s[_rows_at(k), lanes], li_s[_rows_at(k), lanes])

            er, ei = _scan_loop(seg, local, (zero, zero))
            masks = [(mlb_ref[d, 0, :, lanes], mlb_ref[d, 1, :, lanes]) for d in range(3)]
            cr, ci = _scan_rows(jnp.where(row_is_7, car_s[0, :, lanes], pltpu.roll(er, ROWS - 1, 0)),
                                jnp.where(row_is_7, car_s[1, :, lanes], pltpu.roll(ei, ROWS - 1, 0)), masks, (7, 6, 4))

            def full(kk, c):
                cr_, ci_, accr, acci = c
                k = seg - 1 - kk
                lr, li = _cmul_add(ar, nai, cr_, ci_, lr_s[_rows_at(k), lanes], li_s[_rows_at(k), lanes])
                lr_s[_rows_at(k), lanes] = lr
                li_s[_rows_at(k), lanes] = li
                xpr, xpi = xr_s[_rows_at(k), lanes], xi_s[_rows_at(k), lanes]
                return lr, li, accr + lr * xpr + li * xpi, acci + li * xpr - lr * xpi

            lr0, li0, accr, acci = _scan_loop(seg, full, (cr, ci, zero, zero))
            car_s[0, :, lanes] = jnp.broadcast_to(lr0[0:1], lr0.shape)
            car_s[1, :, lanes] = jnp.broadcast_to(li0[0:1], li0.shape)
            dab_ref[0, :, lanes] += accr
            dab_ref[1, :, lanes] += acci
        du_parts = []
        for q in range(nq):
            lanes = slice(q * lq, (q + 1) * lq)
            cols = slice(q * ub, (q + 1) * ub)
            lrb, lib = lr_s[:, lanes].astype(BF16), li_s[:, lanes].astype(BF16)
            uq, dq = ub16[:, cols], dyb16[:, cols]
            du_parts.append(_dot(lrb, bd_ref[0, q], NT) + _dot(lib, bd_ref[1, q], NT))
            dbd_ref[0, q] += _dot(uq, lrb, TN)
            dbd_ref[1, q] += _dot(uq, lib, TN)
            dcd_ref[0, q] += _dot(xr_s[ROWS:, lanes].astype(BF16), dq, TN)
            dcd_ref[1, q] += _dot(xi_s[ROWS:, lanes].astype(BF16), dq, TN)
        du = _time_rows(perm_ref, jnp.concatenate(du_parts, axis=1)) + dsk_ref[...] * dy_ref[...]
        du_ref[...] = du.astype(BF16)

    rev = lambda b, i: (b * nch + nch - 1 - i, 0)
    perm = segment_permutation(tc)
    return _call(
        body, name="ssm_backward", grid=(nb, nch), carried=carried,
        out_shape=[jax.ShapeDtypeStruct((t, w), BF16), jax.ShapeDtypeStruct((2, ROWS, nl), F32),
                   jax.ShapeDtypeStruct(bd.shape, F32), jax.ShapeDtypeStruct(cd.shape, F32)],
        in_specs=[pl.BlockSpec((tc, w), rev), pl.BlockSpec((tc, w), rev),
                  pl.BlockSpec((1, 2, ROWS, nl), lambda b, i: (b * nch + nch - 1 - i, 0, 0, 0)),
                  _const(perm.shape), _const(bd.shape), _const(cd.shape), _const(a1.shape), _const(mlb.shape),
                  _const((1, w))],
        out_specs=[pl.BlockSpec((tc, w), rev), _full((2, ROWS, nl)), _full(bd.shape), _full(cd.shape)],
        scratch_shapes=[pltpu.VMEM((tc + ROWS, nl), F32), pltpu.VMEM((tc + ROWS, nl), F32),
                        pltpu.VMEM((tc, nl), F32), pltpu.VMEM((tc, nl), F32), pltpu.VMEM((2, ROWS, nl), F32)],
        args=(u, dy, st, perm, bd, cd, a1, mlb, dskip))


def ssm_discretise(a_re, a_im, b_re, b_im, log_dt):
    dt = jnp.exp(log_dt)[:, None]
    er = jnp.exp(a_re * dt)
    abr, abi = er * jnp.cos(a_im * dt), er * jnp.sin(a_im * dt)
    den = a_re * a_re + a_im * a_im
    nr, ni = abr - 1.0, abi
    fr = ((nr * a_re + ni * a_im) / den)[..., None]
    fi = ((ni * a_re - nr * a_im) / den)[..., None]
    return abr, abi, fr * b_re - fi * b_im, fr * b_im + fi * b_re


def _complex_square(zr, zi):
    return zr * zr - zi * zi, 2.0 * zr * zi


def ssm_tables(abr, abi, bbr, bbi, c_re, c_im, seg):
    g, p, h = bbr.shape
    nq = g // GROUPS_PER_BLOCK
    zr, zi = abr.reshape(1, -1), abi.reshape(1, -1)
    a1 = jnp.stack([jnp.broadcast_to(zr, (ROWS, g * p)), jnp.broadcast_to(zi, (ROWS, g * p))])
    for _ in range(seg.bit_length() - 1):
        zr, zi = _complex_square(zr, zi)
    row = jnp.arange(ROWS)[:, None]
    ml, mlb = [], []
    for d in (1, 2, 4):
        ml.append(jnp.stack([jnp.where(row >= d, zr, 0.0), jnp.where(row >= d, zi, 0.0)]))
        mlb.append(jnp.stack([jnp.where(row + d < ROWS, zr, 0.0), jnp.where(row + d < ROWS, -zi, 0.0)]))
        zr, zi = _complex_square(zr, zi)
    eye = jnp.eye(GROUPS_PER_BLOCK, dtype=F32)

    def block_diag_in(bb):
        bq = bb.reshape(nq, GROUPS_PER_BLOCK, p, h)
        return jnp.einsum("qaph,ab->qahbp", bq, eye).reshape(nq, GROUPS_PER_BLOCK * h, GROUPS_PER_BLOCK * p)

    def block_diag_out(cc):
        cq = cc.reshape(nq, GROUPS_PER_BLOCK, h, p)
        return jnp.einsum("qahp,ab->qapbh", cq, eye).reshape(nq, GROUPS_PER_BLOCK * p, GROUPS_PER_BLOCK * h)

    bd = jnp.stack([block_diag_in(bbr), block_diag_in(bbi)]).astype(BF16)
    cd = jnp.stack([block_diag_out(c_re), block_diag_out(-c_im)]).astype(BF16)
    return bd, cd, a1, jnp.stack(ml), jnp.stack(mlb)


def ssm_table_grads(dab, dbd, dcd, g, p, h):
    nq = g // GROUPS_PER_BLOCK
    dabr, dabi = dab[0].sum(0).reshape(g, p), dab[1].sum(0).reshape(g, p)
    b5 = dbd.reshape(2, nq, GROUPS_PER_BLOCK, h, GROUPS_PER_BLOCK, p)
    dbb = jnp.einsum("rqahap->rqaph", b5).reshape(2, g, p, h)
    c5 = dcd.reshape(2, nq, GROUPS_PER_BLOCK, p, GROUPS_PER_BLOCK, h)
    dcc = jnp.einsum("rqapah->rqahp", c5).reshape(2, g, h, p)
    return dabr, dabi, dbb[0], dbb[1], dcc[0], -dcc[1]


HALO = 16


def _conv_inputs(gc_ref, v_ref, gch_ref, vh_ref, cv_s, i, tm):
    cv = gc_ref[...].astype(F32) * v_ref[...].astype(F32)
    halo = gch_ref[...].astype(F32) * vh_ref[...].astype(F32)
    cv_s[0:HALO, :] = jnp.where(i == 0, 0.0, halo)
    cv_s[HALO:, :] = cv
    return cv, cv_s[HALO - 1:HALO - 1 + tm, :], cv_s[HALO - 2:HALO - 2 + tm, :]


def _halo_spec(tm, width, tiles_per_seq):
    per = tm // HALO
    return pl.BlockSpec((HALO, width), lambda b, i: (jnp.maximum((b * tiles_per_seq + i) * per - 1, 0), 0))


def mixer_merge_forward(x, gb, gc, v, sga, sgb, yssm, u, mod, conv_w, dskip, wco, wglu, wso_t, wout, nb):
    t, d = x.shape
    cw, sw = gb.shape[1], u.shape[1]
    s = t // nb
    tm = _tile(s, 512)
    tps = s // tm

    def body(x_ref, gb_ref, gc_ref, v_ref, gch_ref, vh_ref, sga_ref, sgb_ref, ys_ref, u_ref, mod_ref, cw_ref,
             dsk_ref, wco_ref, wglu_ref, wso_ref, wout_ref, xo_ref, ya_ref, yb_ref, mix_ref, cv_s):
        i = pl.program_id(1)
        cv, cv1, cv2 = _conv_inputs(gc_ref, v_ref, gch_ref, vh_ref, cv_s, i, tm)
        w = cw_ref[...]
        conv = w[0:1] * cv2 + w[1:2] * cv1 + w[2:3] * cv
        ya = _dot((gb_ref[...].astype(F32) * conv).astype(BF16), wco_ref[...])
        s0 = ys_ref[...] + dsk_ref[...] * u_ref[...]
        s1, _ = _gelu(s0)
        z = _dot(s1.astype(BF16), wglu_ref[...])
        s2 = s1 * jax.nn.sigmoid(z)
        yb = _dot(s2.astype(BF16), wso_ref[...], NT)
        merged = sga_ref[...].astype(F32) * ya + sgb_ref[...].astype(F32) * yb
        mix = _dot(merged.astype(BF16), wout_ref[...])
        _, _, gt = _mod_rows(mod_ref, 1)
        xo_ref[...] = x_ref[...] + gt * mix
        ya_ref[...] = ya.astype(BF16)
        yb_ref[...] = yb.astype(BF16)
        mix_ref[...] = mix.astype(BF16)

    rd, rc, rw = _row_spec(tm, d, tps), _row_spec(tm, cw, tps), _row_spec(tm, sw, tps)
    hc = _halo_spec(tm, cw, tps)
    return pl.pallas_call(
        body, name="mixer_merge_forward", grid=(nb, tps),
        out_shape=[jax.ShapeDtypeStruct((t, d), F32)] + [jax.ShapeDtypeStruct((t, d), BF16)] * 3,
        in_specs=[rd, rc, rc, rc, hc, hc, rd, rd, rw, rw, _mod_spec(d), _const(conv_w.shape), _const((1, sw)),
                  _const(wco.shape), _const(wglu.shape), _const(wso_t.shape), _const(wout.shape)],
        out_specs=[rd, rd, rd, rd],
        scratch_shapes=[pltpu.VMEM((tm + HALO, cw), F32)],
        compiler_params=_params(("arbitrary", "arbitrary")),
    )(x, gb, gc, v, gc, v, sga, sgb, yssm, u, mod, conv_w, dskip, wco, wglu, wso_t, wout)


def mixer_merge_backward(dxo, mix, ya, yb, gb, gc, v, sga, sgb, yssm, u, mod, conv_w, dskip,
                         wco, wglu, wso_t, wout, nb, carried=None):
    t, d = dxo.shape
    cw, sw = gb.shape[1], u.shape[1]
    s = t // nb
    tm = _tile(s, 256)
    tps = s // tm

    def body(dxo_ref, mix_ref, ya_ref, yb_ref, gb_ref, gc_ref, v_ref, gch_ref, vh_ref, sga_ref, sgb_ref, ys_ref,
             u_ref, mod_ref, cw_ref, dsk_ref, wco_ref, wglu_ref, wso_ref, wout_ref,
             dgla_ref, dglb_ref, dgb_ref, dconv_ref, ds0_ref, dgt_ref, ddsk_ref, dwout_ref, dwco_ref, dwso_ref,
             dwglu_ref, cv_s):
        bi, i = pl.program_id(0), pl.program_id(1)
        _zero_when(i == 0, dgt_ref)
        _zero_when(jnp.logical_and(bi == 0, i == 0), ddsk_ref, dwout_ref, dwco_ref, dwso_ref, dwglu_ref)
        dxo = dxo_ref[...]
        _, _, gt = _mod_rows(mod_ref, 1)
        dmix = (gt * dxo).astype(BF16)
        dgt_ref[...] += jnp.sum(dxo * mix_ref[...].astype(F32), axis=0, keepdims=True)[None]
        ya, yb = ya_ref[...].astype(F32), yb_ref[...].astype(F32)
        sga, sgb = sga_ref[...].astype(F32), sgb_ref[...].astype(F32)
        merged = (sga * ya + sgb * yb).astype(BF16)
        dwout_ref[...] += _dot(merged, dmix, TN)
        dmerged = _dot(dmix, wout_ref[...], NT)
        dgla_ref[...] = (dmerged * ya * sga * (1.0 - sga)).astype(BF16)
        dglb_ref[...] = (dmerged * yb * sgb * (1.0 - sgb)).astype(BF16)
        dya = (dmerged * sga).astype(BF16)
        dyb = (dmerged * sgb).astype(BF16)
        cv, cv1, cv2 = _conv_inputs(gc_ref, v_ref, gch_ref, vh_ref, cv_s, i, tm)
        w = cw_ref[...]
        conv = w[0:1] * cv2 + w[1:2] * cv1 + w[2:3] * cv
        gbv = gb_ref[...].astype(F32)
        dwco_ref[...] += _dot((gbv * conv).astype(BF16), dya, TN)
        dya_in = _dot(dya, wco_ref[...], NT)
        dgb_ref[...] = (dya_in * conv).astype(BF16)
        dconv_ref[...] = dya_in * gbv
        uv = u_ref[...]
        s0 = ys_ref[...] + dsk_ref[...] * uv
        s1, th = _gelu(s0)
        s1b = s1.astype(BF16)
        sz = jax.nn.sigmoid(_dot(s1b, wglu_ref[...]))
        s2b = (s1 * sz).astype(BF16)
        dwso_ref[...] += _dot(dyb, s2b, TN)
        ds2 = _dot(dyb, wso_ref[...])
        dz = (ds2 * s1 * sz * (1.0 - sz)).astype(BF16)
        dwglu_ref[...] += _dot(s1b, dz, TN)
        ds1 = ds2 * sz + _dot(dz, wglu_ref[...], NT)
        ds0 = ds1 * _gelu_grad(s0, th)
        ds0_ref[...] = ds0
        ddsk_ref[...] += jnp.sum(ds0 * uv, axis=0, keepdims=True)

    rd, rc, rw = _row_spec(tm, d, tps), _row_spec(tm, cw, tps), _row_spec(tm, sw, tps)
    hc = _halo_spec(tm, cw, tps)
    return _call(
        body, name="mixer_merge_backward", grid=(nb, tps), carried=carried,
        out_shape=[jax.ShapeDtypeStruct((t, d), BF16), jax.ShapeDtypeStruct((t, d), BF16),
                   jax.ShapeDtypeStruct((t, cw), BF16), jax.ShapeDtypeStruct((t, cw), F32),
                   jax.ShapeDtypeStruct((t, sw), F32), jax.ShapeDtypeStruct((nb, 1, d), F32),
                   jax.ShapeDtypeStruct((1, sw), F32), jax.ShapeDtypeStruct(wout.shape, F32),
                   jax.ShapeDtypeStruct(wco.shape, F32), jax.ShapeDtypeStruct(wso_t.shape, F32),
                   jax.ShapeDtypeStruct(wglu.shape, F32)],
        in_specs=[rd, rd, rd, rd, rc, rc, rc, hc, hc, rd, rd, rw, rw, _mod_spec(d), _const(conv_w.shape),
                  _const((1, sw)), _const(wco.shape), _const(wglu.shape), _const(wso_t.shape), _const(wout.shape)],
        out_specs=[rd, rd, rc, rc, rw, pl.BlockSpec((1, 1, d), lambda b, i: (b, 0, 0)), _full((1, sw)),
                   _full(wout.shape), _full(wco.shape), _full(wso_t.shape), _full(wglu.shape)],
        scratch_shapes=[pltpu.VMEM((tm + HALO, cw), F32)],
        args=(dxo, mix, ya, yb, gb, gc, v, gc, v, sga, sgb, yssm, u, mod, conv_w, dskip, wco, wglu, wso_t, wout))


def conv_backward(dconv, gc, v, conv_w, nb):
    t, cw = dconv.shape
    s = t // nb
    tm = _tile(s, 512)
    tps = s // tm
    per = tm // ROWS
    slab = _tile(tm, 32)

    def body(dc_ref, dcn_ref, gc_ref, v_ref, gch_ref, vh_ref, cw_ref, dgc_ref, dv_ref, dw_ref, cv_s, dc_s):
        bi, i = pl.program_id(0), pl.program_id(1)
        _zero_when(jnp.logical_and(bi == 0, i == 0), dw_ref)
        cv_s[0:HALO, :] = jnp.where(i == 0, 0.0, gch_ref[...].astype(F32) * vh_ref[...].astype(F32))
        dc_s[tm:, :] = jnp.where(i == tps - 1, 0.0, dcn_ref[...])
        for r in range(0, tm, slab):
            cv_s[HALO + r:HALO + r + slab, :] = (gc_ref[r:r + slab, :].astype(F32)
                                                 * v_ref[r:r + slab, :].astype(F32))
            dc_s[r:r + slab, :] = dc_ref[r:r + slab, :]
        w = cw_ref[...]
        sums = [jnp.zeros((ROWS, cw), F32)] * CONV_K
        for r in range(0, tm, slab):
            dc = dc_s[r:r + slab, :]
            dcv = w[2:3] * dc + w[1:2] * dc_s[r + 1:r + 1 + slab, :] + w[0:1] * dc_s[r + 2:r + 2 + slab, :]
            dgc_ref[r:r + slab, :] = (dcv * v_ref[r:r + slab, :].astype(F32)).astype(BF16)
            dv_ref[r:r + slab, :] = (dcv * gc_ref[r:r + slab, :].astype(F32)).astype(BF16)
            for k in range(CONV_K):
                lag = HALO + r - (CONV_K - 1 - k)
                prod = dc * cv_s[lag:lag + slab, :]
                sums[k] = sums[k] + jnp.sum(prod.reshape(slab // ROWS, ROWS, cw), axis=0)
        dw_ref[...] += jnp.concatenate([jnp.sum(a, axis=0, keepdims=True) for a in sums], axis=0)

    rc = _row_spec(tm, cw, tps)
    nxt = pl.BlockSpec((ROWS, cw), lambda b, i: (jnp.minimum((b * tps + i + 1) * per, nb * tps * per - 1), 0))
    hc = _halo_spec(tm, cw, tps)
    return pl.pallas_call(
        body, name="conv_backward", grid=(nb, tps),
        out_shape=[jax.ShapeDtypeStruct((t, cw), BF16), jax.ShapeDtypeStruct((t, cw), BF16),
                   jax.ShapeDtypeStruct(conv_w.shape, F32)],
        in_specs=[rc, nxt, rc, rc, hc, hc, _full(conv_w.shape)],
        out_specs=[rc, rc, _full(conv_w.shape)],
        scratch_shapes=[pltpu.VMEM((tm + HALO, cw), F32), pltpu.VMEM((tm + ROWS, cw), F32)],
        compiler_params=_params(("arbitrary", "arbitrary")),
    )(dconv, dconv, gc, v, gc, v, conv_w)


def loss_forward_backward(x, target, g):
    t, d = x.shape
    tm = _tile(t, 512)

    def body(x_ref, t_ref, g_ref, l_ref, dx_ref, dg_ref):
        _zero_when(pl.program_id(0) == 0, dg_ref)
        xv = x_ref[...]
        gv = g_ref[...]
        r = lax.rsqrt(jnp.mean(xv * xv, axis=-1, keepdims=True) + EPS)
        n = xv * r
        err = n * gv - t_ref[...]
        l_ref[...] = jnp.full(l_ref.shape, 0.5 * jnp.sum(jnp.mean(err * err, axis=-1)), F32)
        dy = err * (1.0 / d)
        dn = dy * gv
        dx_ref[...] = r * (dn - n * jnp.mean(n * dn, axis=-1, keepdims=True))
        dg_ref[...] += jnp.sum(dy * n, axis=0, keepdims=True)

    row = pl.BlockSpec((tm, d), lambda i: (i, 0))
    return pl.pallas_call(
        body, name="loss_forward_backward", grid=(t // tm,),
        out_shape=[jax.ShapeDtypeStruct((t // tm, 1, 128), F32), jax.ShapeDtypeStruct((t, d), F32),
                   jax.ShapeDtypeStruct((1, d), F32)],
        in_specs=[row, row, _full((1, d))],
        out_specs=[pl.BlockSpec((1, 1, 128), lambda i: (i, 0, 0)), row, _full((1, d))],
        compiler_params=_params(("arbitrary",)),
    )(x, target, g)


def sum_slots(slots, name):
    _, r, c = slots.shape
    tr = _tile(r, 352) if r % 352 == 0 else _tile(r, 256)
    if tr < 128:
        tr = r

    def body(s_ref, o_ref):
        acc = s_ref[0].astype(F32)
        for j in range(1, N_DEV):
            acc = acc + s_ref[j].astype(F32)
        o_ref[...] = acc

    return pl.pallas_call(
        body, name=name, grid=(r // tr,),
        out_shape=jax.ShapeDtypeStruct((r, c), F32),
        in_specs=[pl.BlockSpec((N_DEV, tr, c), lambda i: (0, i, 0))],
        out_specs=pl.BlockSpec((tr, c), lambda i: (i, 0)),
        compiler_params=_params(("arbitrary",)),
    )(slots)


def adamw_update(w, g, m, v, name):
    r, c = w.shape
    tr = _tile(r, 256) if r % 8 == 0 else r

    def body(w_ref, g_ref, m_ref, v_ref, d_ref, mo_ref, vo_ref):
        d_ref[...], mo_ref[...], vo_ref[...] = _adamw(w_ref[...], g_ref[...], m_ref[...], v_ref[...])

    spec = pl.BlockSpec((tr, c), lambda i: (i, 0))
    return pl.pallas_call(
        body, name=name, grid=(r // tr,),
        out_shape=[jax.ShapeDtypeStruct((r, c), F32)] * 3,
        in_specs=[spec] * 4, out_specs=[spec] * 3,
        compiler_params=_params(("arbitrary",)),
    )(w, g, m, v)


def sum_adamw_update(slots, w, m, v, name):
    _, r, c = slots.shape
    tr = _tile(r, 352) if r % 352 == 0 else _tile(r, 256)

    def body(s_ref, w_ref, m_ref, v_ref, g_ref, d_ref, mo_ref, vo_ref):
        g = s_ref[0].astype(F32)
        for j in range(1, N_DEV):
            g = g + s_ref[j].astype(F32)
        g_ref[...] = g
        d_ref[...], mo_ref[...], vo_ref[...] = _adamw(w_ref[...], g, m_ref[...], v_ref[...])

    spec = pl.BlockSpec((tr, c), lambda i: (i, 0))
    return pl.pallas_call(
        body, name=name, grid=(r // tr,),
        out_shape=[jax.ShapeDtypeStruct((r, c), F32)] * 4,
        in_specs=[pl.BlockSpec((N_DEV, tr, c), lambda i: (0, i, 0))] + [spec] * 3, out_specs=[spec] * 4,
        compiler_params=_params(("arbitrary",)),
    )(slots, w, m, v)


def adamw_update_small(ws, gs, ms, vs):
    n = len(ws)

    def body(*refs):
        w_r, g_r, m_r, v_r = refs[:n], refs[n:2 * n], refs[2 * n:3 * n], refs[3 * n:4 * n]
        d_r, mo_r, vo_r = refs[4 * n:5 * n], refs[5 * n:6 * n], refs[6 * n:7 * n]
        for k in range(n):
            d_r[k][...], mo_r[k][...], vo_r[k][...] = _adamw(w_r[k][...], g_r[k][...], m_r[k][...], v_r[k][...])

    shapes = [jax.ShapeDtypeStruct(w.shape, F32) for w in ws]
    out = pl.pallas_call(body, name="adamw_update_small", out_shape=shapes * 3,
                         compiler_params=_params())(*ws, *gs, *ms, *vs)
    return out[:n], out[n:2 * n], out[2 * n:]


def _slots(grad_t):
    return grad_t.reshape(N_DEV, grad_t.shape[0] // N_DEV, grad_t.shape[1])


def kernel(x, c, w_ada, b_ada, g_ffn1, w1_a, w3_a, w2_a, g_mix, w_in, conv_w, w_conv_out, a_re, a_im, b_re, b_im, c_re, c_im, log_dt, d_skip, w_glu, w_ssm_out, w_out, g_ffn2, w1_b, w3_b, w2_b, g_final, loss_target, m_w_ada, m_b_ada, m_g_ffn1, m_w1_a, m_w3_a, m_w2_a, m_g_mix, m_w_in, m_conv_w, m_w_conv_out, m_a_re, m_a_im, m_b_re, m_b_im, m_c_re, m_c_im, m_log_dt, m_d_skip, m_w_glu, m_w_ssm_out, m_w_out, m_g_ffn2, m_w1_b, m_w3_b, m_w2_b, m_g_final, v_w_ada, v_b_ada, v_g_ffn1, v_w1_a, v_w3_a, v_w2_a, v_g_mix, v_w_in, v_conv_w, v_w_conv_out, v_a_re, v_a_im, v_b_re, v_b_im, v_c_re, v_c_im, v_log_dt, v_d_skip, v_w_glu, v_w_ssm_out, v_w_out, v_g_ffn2, v_w1_b, v_w3_b, v_w2_b, v_g_final):
    nb, s, d = x.shape
    t = nb * s
    me = 4 * lax.axis_index("x") + 2 * lax.axis_index("y") + lax.axis_index("c")
    g_n, p_n, h_n = b_re.shape[1:]
    cw_n = w_conv_out.shape[1] * N_DEV
    sw_n = w_ssm_out.shape[1]
    glu_fold = d // w_glu.shape[2]

    weights = dict(w_ada=w_ada, b_ada=b_ada, g_ffn1=g_ffn1, w1_a=w1_a, w3_a=w3_a, w2_a=w2_a, g_mix=g_mix, w_in=w_in,
                   conv_w=conv_w, w_conv_out=w_conv_out, a_re=a_re, a_im=a_im, b_re=b_re, b_im=b_im, c_re=c_re,
                   c_im=c_im, log_dt=log_dt, d_skip=d_skip, w_glu=w_glu, w_ssm_out=w_ssm_out, w_out=w_out,
                   g_ffn2=g_ffn2, w1_b=w1_b, w3_b=w3_b, w2_b=w2_b, g_final=g_final)
    mom1 = dict(w_ada=m_w_ada, b_ada=m_b_ada, g_ffn1=m_g_ffn1, w1_a=m_w1_a, w3_a=m_w3_a, w2_a=m_w2_a, g_mix=m_g_mix,
                w_in=m_w_in, conv_w=m_conv_w, w_conv_out=m_w_conv_out, a_re=m_a_re, a_im=m_a_im, b_re=m_b_re,
                b_im=m_b_im, c_re=m_c_re, c_im=m_c_im, log_dt=m_log_dt, d_skip=m_d_skip, w_glu=m_w_glu,
                w_ssm_out=m_w_ssm_out, w_out=m_w_out, g_ffn2=m_g_ffn2, w1_b=m_w1_b, w3_b=m_w3_b, w2_b=m_w2_b,
                g_final=m_g_final)
    mom2 = dict(w_ada=v_w_ada, b_ada=v_b_ada, g_ffn1=v_g_ffn1, w1_a=v_w1_a, w3_a=v_w3_a, w2_a=v_w2_a, g_mix=v_g_mix,
                w_in=v_w_in, conv_w=v_conv_w, w_conv_out=v_w_conv_out, a_re=v_a_re, a_im=v_a_im, b_re=v_b_re,
                b_im=v_b_im, c_re=v_c_re, c_im=v_c_im, log_dt=v_log_dt, d_skip=v_d_skip, w_glu=v_w_glu,
                w_ssm_out=v_w_ssm_out, w_out=v_w_out, g_ffn2=v_g_ffn2, w1_b=v_w1_b, w3_b=v_w3_b, w2_b=v_w2_b,
                g_final=v_g_final)
    names = list(weights)
    transposed = ("w1_a", "w3_a", "w_in", "w_ssm_out", "w1_b", "w3_b")
    groups = dict(ffn_a=("w1_a", "w3_a", "w2_a"), mixer=("w_in", "w_conv_out", "w_glu", "w_ssm_out", "w_out"),
                  ffn_b=("w1_b", "w3_b", "w2_b"))
    big = groups["ffn_a"] + groups["mixer"] + groups["ffn_b"]

    def shard_rows(name):
        w = weights[name][0]
        if name in transposed:
            w = w.T
        if name == "w_glu":
            w = w.reshape(w.shape[0] // glu_fold, d)
        return w.astype(BF16)

    def gather_stage(group):
        return GatherStage([shard_rows(n) for n in groups[group]])

    gw = {}

    def keep_weights(group, outs):
        for n, w in zip(groups[group], outs):
            gw[n] = w.reshape(sw_n, sw_n) if n == "w_glu" else w

    pad_rows = lambda a: jnp.pad(a, ((0, -a.shape[0] % GATHER_ROWS), (0, 0)))
    c_all, conv_all, *ffn_a_weights = run_stage(
        StageGroup([GatherStage([pad_rows(c), pad_rows(conv_w[0])]), gather_stage("ffn_a")]), "gather_cond_ffn_a")
    keep_weights("ffn_a", ffn_a_weights)
    c_all = c_all.reshape(N_DEV, -1, d)[:, :nb].reshape(N_DEV * nb, d)
    conv_full = conv_all.reshape(N_DEV, GATHER_ROWS, -1)[:, :CONV_K].transpose(1, 0, 2).reshape(CONV_K, cw_n)
    ada_cols = w_ada.shape[2]
    b_cols = lax.dynamic_slice(b_ada, (0, me * ada_cols), (1, ada_cols))
    mod_cols = ada_forward(c_all, w_ada[0], b_cols)
    (mod_all,) = run_stage(GatherStage([mod_cols]), "gather_mod")
    mod_mine = lax.dynamic_slice(mod_all.reshape(N_DEV, N_DEV * nb, ada_cols), (0, me * nb, 0), (N_DEV, nb, ada_cols))
    mod = mod_mine.transpose(1, 0, 2).reshape(nb, N_MOD, d)

    disc_in = (a_re[0], a_im[0], b_re[0], b_im[0], log_dt[0])
    (abr, abi, bbr, bbi), disc_vjp = jax.vjp(ssm_discretise, *disc_in)
    bd, cd, abar8, ml, mlb = ssm_tables(abr, abi, bbr, bbi, c_re[0], c_im[0], _tile(s, 256) // ROWS)

    x0 = x.reshape(t, d)
    (x1, a1, b1, f1, h1), got = ffn_forward(x0, mod, g_ffn1, gw["w1_a"], gw["w3_a"], gw["w2_a"], 0, nb,
                                            "ffn_a_forward", carried=gather_stage("mixer"))
    keep_weights("mixer", got)
    (gb, gc, vv, u, sga, sgb, h2), got = mixer_proj_forward(x1, mod, g_mix, gw["w_in"], cw_n, sw_n, nb,
                                                            carried=gather_stage("ffn_b"))
    keep_weights("ffn_b", got)
    yssm, st = ssm_forward(u, bd, cd, abar8, ml, nb)
    x2, ya, yb, mix = mixer_merge_forward(x1, gb, gc, vv, sga, sgb, yssm, u, mod, conv_full, d_skip,
                                          gw["w_conv_out"], gw["w_glu"], gw["w_ssm_out"], gw["w_out"], nb)
    (x3, a3, b3, f3, h3), _ = ffn_forward(x2, mod, g_ffn2, gw["w1_b"], gw["w3_b"], gw["w2_b"], 2, nb,
                                          "ffn_b_forward")
    loss_parts, dx3, dg_final = loss_forward_backward(x3, loss_target.reshape(t, d), g_final.reshape(1, d))
    loss_here = jnp.sum(loss_parts[:, 0, 0]).reshape(1)

    part, received = {}, {}

    def exchange_stage(ns):
        return ExchangeStage([_slots(part[n]) for n in ns])

    (da3, db3, sw3, df3, dgt3), _ = ffn_backward_hidden(dx3, a3, b3, f3, mod, gw["w2_b"], 2, nb,
                                                        "ffn_b_backward_hidden")
    (dx2, dmod3, dg_ffn2), _ = ffn_backward_input(dx3, x2, da3, db3, mod, g_ffn2, gw["w1_b"], gw["w3_b"],
                                                  2, nb, "ffn_b_backward_input")
    (part["w1_b"],), _ = nn_matmul(da3, h3, "grad_w1_b")
    (part["w3_b"],), _ = nn_matmul(db3, h3, "grad_w3_b")
    (part["w2_b"],), _ = nn_matmul(sw3, df3, "grad_w2_b")
    (dgla, dglb, dgb, dconv, ds0, dgt2, dd_skip, dw_out, dw_co, dw_so_t, dw_glu), got = mixer_merge_backward(
        dx2, mix, ya, yb, gb, gc, vv, sga, sgb, yssm, u, mod, conv_full, d_skip,
        gw["w_conv_out"], gw["w_glu"], gw["w_ssm_out"], gw["w_out"], nb, carried=exchange_stage(groups["ffn_b"]))
    received.update(zip(groups["ffn_b"], got))
    part["w_out"] = dw_out.astype(BF16)
    part["w_conv_out"] = dw_co.astype(BF16)
    part["w_ssm_out"] = dw_so_t.astype(BF16)
    part["w_glu"] = dw_glu.reshape(sw_n // glu_fold, d).astype(BF16)
    (du, dab, dbd, dcd), got = ssm_backward(u, ds0, st, bd, cd, abar8, mlb, d_skip, nb,
                                            carried=exchange_stage(groups["mixer"][1:]))
    received.update(zip(groups["mixer"][1:], got))
    dgc, dvv, dconv_w = conv_backward(dconv, gc, vv, conv_full, nb)
    part["w_in"] = jnp.concatenate([tn_matmul(p, h2, "grad_w_in_%d" % k)
                                    for k, p in enumerate((dgb, dgc, dvv, du, dgla, dglb))], axis=0)
    (dx1, dmod2, dg_mix), got = mixer_proj_backward(dgb, dgc, dvv, du, dgla, dglb, dx2, x1, mod, g_mix, gw["w_in"], nb,
                                                    carried=exchange_stage(("w_in",)))
    received["w_in"] = got[0]
    def pack(parts):
        flat = jnp.concatenate([a.reshape(-1) for a in parts.values()])
        rows = -(-flat.shape[0] // (128 * GATHER_ROWS)) * GATHER_ROWS
        return jnp.pad(flat, (0, rows * 128 - flat.shape[0])).reshape(rows, 128)

    def unpack(flat, parts):
        out, off = {}, 0
        for key, like in parts.items():
            n = math.prod(like.shape)
            out[key], off = flat[..., off:off + n].reshape(flat.shape[:-1] + like.shape), off + n
        return out

    dabr, dabi, dbbr, dbbi, dcr, dci = ssm_table_grads(dab, dbd, dcd, g_n, p_n, h_n)
    early = dict(gmod=jnp.concatenate([dmod2, dgt2, dmod3, dgt3], axis=1), g_mix=dg_mix, g_ffn2=dg_ffn2,
                 g_final=dg_final, d_skip=dd_skip, abr=dabr, abi=dabi, bbr=dbbr, bbi=dbbi, c_re=dcr, c_im=dci,
                 conv_w=dconv_w, loss=loss_here)
    (da1, db1, sw1, df1, dgt1), _ = ffn_backward_hidden(dx1, a1, b1, f1, mod, gw["w2_a"], 0, nb,
                                                        "ffn_a_backward_hidden")
    (part["w2_a"],), (early_all,) = nn_matmul(sw1, df1, "grad_w2_a", carried=GatherStage([pack(early)]))
    (part["w1_a"],), got = nn_matmul(da1, h1, "grad_w1_a", carried=exchange_stage(("w2_a",)))
    received["w2_a"] = got[0]
    (part["w3_a"],), got = nn_matmul(db1, h1, "grad_w3_a", carried=exchange_stage(("w1_a",)))
    received["w1_a"] = got[0]
    (dx0, dmod1, dg_ffn1), got = ffn_backward_input(dx1, x0, da1, db1, mod, g_ffn1, gw["w1_a"], gw["w3_a"],
                                                    0, nb, "ffn_a_backward_input", carried=exchange_stage(("w3_a",)))
    received["w3_a"] = got[0]
    late = dict(gmod=jnp.concatenate([dmod1, dgt1], axis=1), g_ffn1=dg_ffn1)
    (late_all,) = run_stage(GatherStage([pack(late)]), "gather_ffn_a_small_grads")

    tot, per_dev = {}, {}
    for parts, gathered, name in ((early, early_all, "early"), (late, late_all, "late")):
        slots = gathered.reshape(N_DEV, -1, 128)
        total = sum_slots(slots, "sum_small_grads_" + name).reshape(-1)
        for key, val in unpack(total, parts).items():
            tot[name + key if key == "gmod" else key] = val
        per_dev[name] = unpack(slots.reshape(N_DEV, -1), parts)["gmod"]
    gmod_all = jnp.concatenate([per_dev["late"], per_dev["early"]], axis=2).reshape(N_DEV * nb, N_MOD * d)
    gmod_tot = jnp.concatenate([tot["lategmod"], tot["earlygmod"]], axis=1)
    g_a_re, g_a_im, g_b_re, g_b_im, g_log_dt = disc_vjp((tot["abr"], tot["abi"], tot["bbr"], tot["bbi"]))

    loss = tot["loss"].reshape(())
    grads = {}
    grads["b_ada"] = sum_rows(gmod_tot.reshape(nb, N_MOD * d))
    grads["g_ffn1"], grads["g_mix"], grads["g_ffn2"] = tot["g_ffn1"], tot["g_mix"], tot["g_ffn2"]
    grads["g_final"] = tot["g_final"].reshape(d)
    grads["d_skip"] = tot["d_skip"]
    grads["a_re"], grads["a_im"], grads["log_dt"] = g_a_re[None], g_a_im[None], g_log_dt[None]
    grads["b_re"], grads["b_im"] = g_b_re[None], g_b_im[None]
    grads["c_re"], grads["c_im"] = tot["c_re"][None], tot["c_im"][None]
    grads["conv_w"] = lax.dynamic_slice(tot["conv_w"], (0, me * conv_w.shape[2]), (CONV_K, conv_w.shape[2]))[None]

    delta, new_m, new_v = {}, {}, {}
    for name in big:
        wmv = (weights[name][0], mom1[name][0], mom2[name][0])
        if name in transposed:
            outs = sum_adamw_update(received[name], *[a.T for a in wmv], "adamw_" + name)
            gsum, dl, mm, vn = [a.T for a in outs]
        elif received[name].shape[1:] == weights[name].shape[1:]:
            gsum, dl, mm, vn = sum_adamw_update(received[name], *wmv, "adamw_" + name)
        else:
            gsum = sum_slots(received[name], "sum_" + name).reshape(weights[name].shape[1:])
            dl, mm, vn = adamw_update(*wmv[:1], gsum, *wmv[1:], "adamw_" + name)
        grads[name] = gsum[None]
        delta[name], new_m[name], new_v[name] = dl[None], mm[None], vn[None]

    gmod_cols = lax.dynamic_slice(gmod_all, (0, me * ada_cols), (N_DEV * nb, ada_cols))
    g_wada, d_wada, m_wada, v_wada = ada_backward_update(c_all, gmod_cols, w_ada[0], m_w_ada[0], v_w_ada[0])
    grads["w_ada"], delta["w_ada"], new_m["w_ada"], new_v["w_ada"] = g_wada[None], d_wada[None], m_wada[None], v_wada[None]

    small_names = [n for n in names if n not in big and n != "w_ada"]

    narrow = ("b_re", "b_im")

    def as2d(n, a):
        a = jnp.swapaxes(a.reshape(weights[n].shape), -1, -2) if n in narrow else a
        return a.reshape(-1, a.shape[-1])

    def from2d(n, a):
        shape = weights[n].shape
        return jnp.swapaxes(a.reshape(shape[:-2] + (shape[-1], shape[-2])), -1, -2) if n in narrow else a.reshape(shape)

    sw_, sg_, sm_, sv_ = ([as2d(n, src[n]) for n in small_names] for src in (weights, grads, mom1, mom2))
    sd, smo, svo = adamw_update_small(sw_, sg_, sm_, sv_)
    for n, dl, mm, vn in zip(small_names, sd, smo, svo):
        grads[n] = grads[n].reshape(weights[n].shape)
        delta[n], new_m[n], new_v[n] = from2d(n, dl), from2d(n, mm), from2d(n, vn)

    grad_x = dx0.reshape(nb, s, d)
    return (loss, grad_x, *[grads[n] for n in names], *[delta[n] for n in names],
            *[new_m[n] for n in names], *[new_v[n] for n in names])


def sum_rows(a):
    r, c = a.shape

    def body(a_ref, o_ref):
        acc = a_ref[0:1, :]
        for j in range(1, r):
            acc = acc + a_ref[j:j + 1, :]
        o_ref[...] = acc

    return pl.pallas_call(body, name="sum_rows", out_shape=jax.ShapeDtypeStruct((1, c), F32),
                          compiler_params=_params())(a)
```

```python
import functools
import math

import jax
import jax.numpy as jnp
from jax import lax
from jax.experimental import pallas as pl
from jax.experimental.pallas import tpu as pltpu

F32 = jnp.float32
BF16 = jnp.bfloat16
N_DEV = 8
N_MOD = 9
EPS = 1e-6
CONV_K = 3
ADAM_LR = 0.001
ADAM_B1 = 0.9
ADAM_B2 = 0.999
ADAM_EPS = 1e-08
ADAM_WD = 0.01
ADAM_STEP = 10
GELU_C0 = math.sqrt(2.0 / math.pi)
GELU_C1 = 0.044715
V7X_VMEM_LIMIT = 56 * 1024 * 1024
MESH_ID = pl.DeviceIdType.MESH
NT = (((1,), (1,)), ((), ()))
TN = (((0,), (0,)), ((), ()))


def _dot(a, b, dims=None):
    if dims is None:
        return jnp.dot(a, b, preferred_element_type=F32)
    return lax.dot_general(a, b, dims, preferred_element_type=F32)


def _params(sem=None, vmem=V7X_VMEM_LIMIT):
    return pltpu.CompilerParams(dimension_semantics=sem, vmem_limit_bytes=vmem)


def _full(shape):
    return pl.BlockSpec(shape, lambda *_: (0,) * len(shape))


def _const(shape):
    return pl.BlockSpec(shape, lambda *_: (0,) * len(shape), pipeline_mode=pl.Buffered(1))


def _tile(n, want):
    t = min(n, want)
    while n % t:
        t //= 2
    return t


class GatherStage:
    COPIES = 9

    def __init__(self, shards):
        n = len(shards)
        self.inputs = list(shards)
        self.out_shape = [jax.ShapeDtypeStruct((N_DEV * s.shape[0], s.shape[1]), s.dtype) for s in shards]
        self.scratch = [pltpu.SemaphoreType.DMA((self.COPIES * n,)), pltpu.SemaphoreType.DMA((self.COPIES * n,)),
                        pltpu.SemaphoreType.DMA((n,))]

    def _plan(self, ins, outs, sems):
        send_sems, recv_sems, local_sems = sems
        n = len(ins)
        x, y, c = lax.axis_index("x"), lax.axis_index("y"), lax.axis_index("c")
        me, sibling, xn, yn, dg = (x, y, c), (x, y, 1 - c), (1 - x, y, c), (x, 1 - y, c), (1 - x, 1 - y, c)

        def rows(k, block, half=None):
            r = ins[k].shape[0]
            px, py, pc = block
            base = (4 * px + 2 * py + pc) * r
            if half is None:
                return outs[k].at[pl.ds(base, r), :]
            return outs[k].at[pl.ds(base + half * (r // 2), r // 2), :]

        def copy(k, j, block, to, half=None, src=None):
            return pltpu.make_async_remote_copy(
                src_ref=rows(k, block, half) if src is None else src, dst_ref=rows(k, block, half),
                send_sem=send_sems.at[self.COPIES * k + j], recv_sem=recv_sems.at[self.COPIES * k + j],
                device_id=to, device_id_type=MESH_ID)

        sib = lambda b: (b[0], b[1], 1 - b[2])
        mine = [pltpu.make_async_copy(ins[k], rows(k, me), local_sems.at[k]) for k in range(n)]
        first = [(0, me, sibling, None, sibling), (1, me, xn, None, xn), (2, me, yn, None, yn)]
        second = [(3, xn, yn, 0, dg), (4, yn, xn, 1, dg), (5, xn, sibling, None, sib(xn)), (6, yn, sibling, None, sib(yn))]
        third = [(7, dg, sibling, 0, sib(dg)), (8, dg, sibling, 1, sib(dg))]
        return n, me, copy, mine, first, second, third

    def start(self, ins, outs, sems):
        n, me, copy, mine, first, _, _ = self._plan(ins, outs, sems)
        for cp in mine:
            cp.start()
        for k in range(n):
            for j, block, to, half, _ in first:
                copy(k, j, block, to, half, src=ins[k]).start()

    def advance(self, ins, outs, sems):
        n, me, copy, mine, first, second, third = self._plan(ins, outs, sems)
        for k in range(n):
            copy(k, 1, first[1][4], me).wait_recv()
            copy(k, 2, first[2][4], me).wait_recv()
            for j, block, to, half, _ in second:
                copy(k, j, block, to, half).start()

    def advance_again(self, ins, outs, sems):
        n, me, copy, mine, first, second, third = self._plan(ins, outs, sems)
        for k in range(n):
            copy(k, 3, second[0][4], me, 0).wait_recv()
            copy(k, 4, second[1][4], me, 1).wait_recv()
            for j, block, to, half, _ in third:
                copy(k, j, block, to, half).start()

    def finish(self, ins, outs, sems):
        n, me, copy, mine, first, second, third = self._plan(ins, outs, sems)
        arrived = lambda k, j, block, half: copy(k, j, block, me, half).wait_recv()
        for k in range(n):
            arrived(k, 0, first[0][4], None)
            arrived(k, 5, second[2][4], None)
            arrived(k, 6, second[3][4], None)
            arrived(k, 7, third[0][4], 0)
            arrived(k, 8, third[1][4], 1)
        for k in range(n):
            for j, block, to, half, _ in first:
                copy(k, j, block, to, half, src=ins[k]).wait_send()
            for j, block, to, half, _ in second + third:
                copy(k, j, block, to, half).wait_send()
        for cp in mine:
            cp.wait()


class ExchangeStage:
    def __init__(self, bufs):
        n = len(bufs)
        self.inputs = list(bufs)
        self.out_shape = [jax.ShapeDtypeStruct(b.shape, b.dtype) for b in bufs]
        self.scratch = [pltpu.SemaphoreType.DMA((7 * n,)), pltpu.SemaphoreType.DMA((7 * n,)),
                        pltpu.SemaphoreType.DMA((n,))]

    def _plan(self, ins, outs, sems):
        send_sems, recv_sems, local_sems = sems
        n = len(ins)
        x, y, c = lax.axis_index("x"), lax.axis_index("y"), lax.axis_index("c")
        me = 4 * x + 2 * y + c
        mine = [pltpu.make_async_copy(ins[k].at[me], outs[k].at[me], local_sems.at[k]) for k in range(n)]
        copies = []
        for mask in range(1, N_DEV):
            px, py, pc = x ^ (mask >> 2), y ^ ((mask >> 1) & 1), c ^ (mask & 1)
            for k in range(n):
                copies.append(pltpu.make_async_remote_copy(
                    src_ref=ins[k].at[4 * px + 2 * py + pc], dst_ref=outs[k].at[me],
                    send_sem=send_sems.at[7 * k + mask - 1], recv_sem=recv_sems.at[7 * k + mask - 1],
                    device_id=(px, py, pc), device_id_type=MESH_ID))
        return mine, copies

    def start(self, ins, outs, sems):
        mine, copies = self._plan(ins, outs, sems)
        for cp in mine + copies:
            cp.start()

    def advance(self, ins, outs, sems):
        pass

    advance_again = advance

    def finish(self, ins, outs, sems):
        mine, copies = self._plan(ins, outs, sems)
        for cp in copies:
            cp.wait_recv()
        for cp in copies:
            cp.wait_send()
        for cp in mine:
            cp.wait()


ANY_SPEC = pl.BlockSpec(memory_space=pl.ANY)
GATHER_ROWS = 16


class StageGroup:
    def __init__(self, stages):
        self.stages = list(stages)
        self.inputs = [a for s in stages for a in s.inputs]
        self.out_shape = [o for s in stages for o in s.out_shape]
        self.scratch = [t for s in stages for t in s.scratch]

    def _parts(self, ins, outs, sems):
        i = o = t = 0
        for s in self.stages:
            ni, no, nt = len(s.inputs), len(s.out_shape), len(s.scratch)
            yield s, ins[i:i + ni], outs[o:o + no], sems[t:t + nt]
            i, o, t = i + ni, o + no, t + nt

    def start(self, ins, outs, sems):
        for s, i_, o_, t_ in self._parts(ins, outs, sems):
            s.start(i_, o_, t_)

    def advance(self, ins, outs, sems):
        for s, i_, o_, t_ in self._parts(ins, outs, sems):
            s.advance(i_, o_, t_)

    def advance_again(self, ins, outs, sems):
        for s, i_, o_, t_ in self._parts(ins, outs, sems):
            s.advance_again(i_, o_, t_)

    def finish(self, ins, outs, sems):
        for s, i_, o_, t_ in self._parts(ins, outs, sems):
            s.finish(i_, o_, t_)


def run_stage(stage, name):
    ci, co = len(stage.inputs), len(stage.out_shape)

    def body(*refs):
        ins, outs, sems = refs[:ci], refs[ci:ci + co], refs[ci + co:]
        stage.start(ins, outs, sems)
        stage.advance(ins, outs, sems)
        stage.advance_again(ins, outs, sems)
        stage.finish(ins, outs, sems)

    return pl.pallas_call(body, name=name, out_shape=stage.out_shape, in_specs=[ANY_SPEC] * ci,
                          out_specs=[ANY_SPEC] * co, scratch_shapes=stage.scratch)(*stage.inputs)


def _call(body, *, name, grid, in_specs, out_specs, out_shape, args, scratch_shapes=(), carried=None):
    sem = ("arbitrary",) * len(grid)
    if carried is None:
        return pl.pallas_call(body, name=name, grid=grid, in_specs=list(in_specs), out_specs=list(out_specs),
                              out_shape=list(out_shape), scratch_shapes=list(scratch_shapes),
                              compiler_params=_params(sem))(*args), None
    ni, no, ns = len(in_specs), len(out_shape), len(scratch_shapes)
    ci, co = len(carried.inputs), len(carried.out_shape)
    n_steps = math.prod(grid)

    def wrapped(*refs):
        ins, refs = refs[:ni], refs[ni:]
        cins, refs = refs[:ci], refs[ci:]
        outs, refs = refs[:no], refs[no:]
        couts, refs = refs[:co], refs[co:]
        scr, csems = refs[:ns], refs[ns:]
        step = functools.reduce(lambda acc, ig: acc * ig[1] + ig[0],
                                zip([pl.program_id(k) for k in range(len(grid))], grid), 0)

        @pl.when(step == 0)
        def _():
            carried.start(cins, couts, csems)

        @pl.when(step == n_steps // 2)
        def _():
            carried.advance(cins, couts, csems)

        @pl.when(step == (3 * n_steps) // 4)
        def _():
            carried.advance_again(cins, couts, csems)

        body(*ins, *outs, *scr)

        @pl.when(step == n_steps - 1)
        def _():
            carried.finish(cins, couts, csems)

    res = pl.pallas_call(
        wrapped, name=name, grid=grid, in_specs=list(in_specs) + [ANY_SPEC] * ci,
        out_specs=list(out_specs) + [ANY_SPEC] * co, out_shape=list(out_shape) + carried.out_shape,
        scratch_shapes=list(scratch_shapes) + carried.scratch, compiler_params=_params(sem),
    )(*args, *carried.inputs)
    return res[:no], res[no:]


def _norm_mod(x, g, shift, scale):
    r = lax.rsqrt(jnp.mean(x * x, axis=-1, keepdims=True) + EPS)
    n = x * r
    return (n * g) * (1.0 + scale) + shift, n, r


def _norm_mod_bwd(dh, n, r, g, scale):
    dsh = jnp.sum(dh, axis=0, keepdims=True)
    dsc = jnp.sum(dh * (n * g), axis=0, keepdims=True)
    dg = jnp.sum(dh * (1.0 + scale) * n, axis=0, keepdims=True)
    dn = dh * ((1.0 + scale) * g)
    dx = r * (dn - n * jnp.mean(n * dn, axis=-1, keepdims=True))
    return dx, dsh, dsc, dg


def _mod_rows(mod_ref, sub):
    m = mod_ref[0]
    return m[3 * sub:3 * sub + 1], m[3 * sub + 1:3 * sub + 2], m[3 * sub + 2:3 * sub + 3]


def _gelu(x):
    t = jnp.tanh(GELU_C0 * (x + GELU_C1 * x * x * x))
    return 0.5 * x * (1.0 + t), t


def _gelu_grad(x, t):
    return 0.5 * (1.0 + t) + 0.5 * x * (1.0 - t * t) * (GELU_C0 * (1.0 + 3.0 * GELU_C1 * x * x))


def _zero_when(cond, *refs):
    @pl.when(cond)
    def _():
        for r in refs:
            r[...] = jnp.zeros_like(r)


def ada_forward(c_all, w_ada, b_ada_cols):
    def body(c_ref, w_ref, b_ref, o_ref):
        c = c_ref[...]
        cond = (c * jax.nn.sigmoid(c)).astype(BF16)
        o_ref[...] = _dot(cond, w_ref[...].astype(BF16)) + b_ref[...]

    nb, d = c_all.shape
    cols = w_ada.shape[1]
    tn = _tile(cols, 384)
    return pl.pallas_call(
        body, name="ada_forward", grid=(cols // tn,),
        out_shape=jax.ShapeDtypeStruct((nb, cols), F32),
        in_specs=[_full((nb, d)), pl.BlockSpec((d, tn), lambda j: (0, j)), pl.BlockSpec((1, tn), lambda j: (0, j))],
        out_specs=pl.BlockSpec((nb, tn), lambda j: (0, j)),
        compiler_params=_params(("arbitrary",)),
    )(c_all, w_ada, b_ada_cols)


def _adamw(w, g, m, v):
    m = ADAM_B1 * m + (1.0 - ADAM_B1) * g
    v = ADAM_B2 * v + (1.0 - ADAM_B2) * (g * g)
    m_hat = m / (1.0 - ADAM_B1 ** ADAM_STEP)
    v_hat = v / (1.0 - ADAM_B2 ** ADAM_STEP)
    delta = -ADAM_LR * (m_hat / (jnp.sqrt(v_hat) + ADAM_EPS) + ADAM_WD * w)
    return delta, m, v


def ada_backward_update(c_all, gmod_cols, w, m, v):
    def body(c_ref, g_ref, w_ref, m_ref, v_ref, go_ref, d_ref, mo_ref, vo_ref):
        c = c_ref[...]
        cond = (c * jax.nn.sigmoid(c)).astype(BF16)
        g = _dot(cond, g_ref[...].astype(BF16), TN)
        go_ref[...] = g
        d_ref[...], mo_ref[...], vo_ref[...] = _adamw(w_ref[...], g, m_ref[...], v_ref[...])

    nb, d = c_all.shape
    cols = w.shape[1]
    tn = _tile(cols, 128)
    col = pl.BlockSpec((d, tn), lambda j: (0, j))
    return pl.pallas_call(
        body, name="ada_backward_update", grid=(cols // tn,),
        out_shape=[jax.ShapeDtypeStruct(w.shape, F32)] * 4,
        in_specs=[_full((nb, d)), pl.BlockSpec((nb, tn), lambda j: (0, j)), col, col, col],
        out_specs=[col] * 4,
        compiler_params=_params(("arbitrary",)),
    )(c_all, gmod_cols, w, m, v)


def _row_spec(tm, width, tiles_per_seq):
    return pl.BlockSpec((tm, width), lambda b, i: (b * tiles_per_seq + i, 0))


def _mod_spec(d):
    return pl.BlockSpec((1, N_MOD, d), lambda b, i: (b, 0, 0))


def _col_spec(rows, tm, tiles_per_seq):
    return pl.BlockSpec((rows, tm), lambda b, i: (0, b * tiles_per_seq + i))


def _ffn_chunk(f):
    return f // 4 if f % 512 == 0 and f > 1536 else f


def ffn_forward(x, mod, g, w1t, w3t, w2, sub, nb, name, carried=None):
    t, d = x.shape
    f = w1t.shape[0]
    s = t // nb
    tm = _tile(s, 512)
    fc = _ffn_chunk(f)

    def body(x_ref, mod_ref, g_ref, w1_ref, w3_ref, w2_ref, xo_ref, a_ref, b_ref, f_ref, h_ref):
        xv = x_ref[...]
        sh, sc, gt = _mod_rows(mod_ref, sub)
        h, _, _ = _norm_mod(xv, g_ref[...], sh, sc)
        hb = h.astype(BF16)
        h_ref[...] = hb
        acc_t = jnp.zeros((d, tm), F32)
        for k in range(f // fc):
            rows = slice(k * fc, (k + 1) * fc)
            a = _dot(w1_ref[rows, :], hb, NT)
            b = _dot(w3_ref[rows, :], hb, NT)
            a_ref[rows, :] = a.astype(BF16)
            b_ref[rows, :] = b.astype(BF16)
            sw = (a * jax.nn.sigmoid(a)) * b
            acc_t = acc_t + _dot(w2_ref[rows, :], sw.astype(BF16), TN)
        acc = acc_t.T
        f_ref[...] = acc.astype(BF16)
        xo_ref[...] = xv + (0.5 * gt) * acc

    tps = s // tm
    rd, cf = _row_spec(tm, d, tps), _col_spec(f, tm, tps)
    return _call(
        body, name=name, grid=(nb, tps), carried=carried,
        out_shape=[jax.ShapeDtypeStruct((t, d), F32)] + [jax.ShapeDtypeStruct((f, t), BF16)] * 2
        + [jax.ShapeDtypeStruct((t, d), BF16)] * 2,
        in_specs=[rd, _mod_spec(d), _const((1, d)), _const((f, d)), _const((f, d)), _const((f, d))],
        out_specs=[rd, cf, cf, rd, rd],
        args=(x, mod, g, w1t, w3t, w2))


def ffn_backward_hidden(dxo, a_t, b_t, fo, mod, w2, sub, nb, name, carried=None):
    t, d = dxo.shape
    f = w2.shape[0]
    s = t // nb
    tm = _tile(s, 512)
    fc = _tile(f, 704) if f % 704 == 0 else _tile(f, 512)

    def body(dxo_ref, a_ref, b_ref, f_ref, mod_ref, w2_ref, da_ref, db_ref, s_ref, df_ref, dgt_ref):
        _zero_when(pl.program_id(1) == 0, dgt_ref)
        dxo = dxo_ref[...]
        _, _, gt = _mod_rows(mod_ref, sub)
        dfb = ((0.5 * gt) * dxo).astype(BF16)
        df_ref[...] = dfb
        dgt_ref[...] += 0.5 * jnp.sum(dxo * f_ref[...].astype(F32), axis=0, keepdims=True)[None]
        for k in range(f // fc):
            rows = slice(k * fc, (k + 1) * fc)
            ds = _dot(w2_ref[rows, :], dfb, NT).astype(BF16)
            av = a_ref[rows, :].astype(F32)
            bv = b_ref[rows, :]
            sig = jax.nn.sigmoid(av)
            sl = av * sig
            slb = sl.astype(BF16)
            da_ref[rows, :] = ds * bv * (sig + sl * (1.0 - sig)).astype(BF16)
            db_ref[rows, :] = ds * slb
            s_ref[rows, :] = slb * bv

    tps = s // tm
    rd, cf = _row_spec(tm, d, tps), _col_spec(f, tm, tps)
    return _call(
        body, name=name, grid=(nb, tps), carried=carried,
        out_shape=[jax.ShapeDtypeStruct((f, t), BF16)] * 3
        + [jax.ShapeDtypeStruct((t, d), BF16), jax.ShapeDtypeStruct((nb, 1, d), F32)],
        in_specs=[rd, cf, cf, rd, _mod_spec(d), _const((f, d))],
        out_specs=[cf, cf, cf, rd, pl.BlockSpec((1, 1, d), lambda b, i: (b, 0, 0))],
        args=(dxo, a_t, b_t, fo, mod, w2))


def ffn_backward_input(dxo, x, da_t, db_t, mod, g, w1t, w3t, sub, nb, name, carried=None):
    t, d = x.shape
    f = w1t.shape[0]
    s = t // nb
    tm = _tile(s, 512)

    def body(dxo_ref, x_ref, da_ref, db_ref, mod_ref, g_ref, w1_ref, w3_ref, dx_ref, dmod_ref, dg_ref):
        bi, i = pl.program_id(0), pl.program_id(1)
        _zero_when(i == 0, dmod_ref)
        _zero_when(jnp.logical_and(bi == 0, i == 0), dg_ref)
        sh, sc, _ = _mod_rows(mod_ref, sub)
        gv = g_ref[...]
        _, n, r = _norm_mod(x_ref[...], gv, sh, sc)
        dh_t = _dot(w1_ref[...], da_ref[...], TN) + _dot(w3_ref[...], db_ref[...], TN)
        dxn, dsh, dsc, dg = _norm_mod_bwd(dh_t.T, n, r, gv, sc)
        dx_ref[...] = dxo_ref[...] + dxn
        dmod_ref[...] += jnp.concatenate([dsh, dsc], axis=0)[None]
        dg_ref[...] += dg

    tps = s // tm
    rd, cf = _row_spec(tm, d, tps), _col_spec(f, tm, tps)
    return _call(
        body, name=name, grid=(nb, tps), carried=carried,
        out_shape=[jax.ShapeDtypeStruct((t, d), F32), jax.ShapeDtypeStruct((nb, 2, d), F32),
                   jax.ShapeDtypeStruct((1, d), F32)],
        in_specs=[rd, rd, cf, cf, _mod_spec(d), _const((1, d)), _const((f, d)), _const((f, d))],
        out_specs=[rd, pl.BlockSpec((1, 2, d), lambda b, i: (b, 0, 0)), _full((1, d))],
        args=(dxo, x, da_t, db_t, mod, g, w1t, w3t))


def nn_matmul(lhs_t, rhs, name, carried=None):
    m, t = lhs_t.shape
    n = rhs.shape[1]
    tk = _tile(t, 2048)
    tmm = m if m <= 1536 else m // 2
    nk = t // tk

    def body(a_ref, b_ref, o_ref, acc_ref):
        k = pl.program_id(1)
        _zero_when(k == 0, acc_ref)
        acc_ref[...] += _dot(a_ref[...], b_ref[...])

        @pl.when(k == nk - 1)
        def _():
            o_ref[...] = acc_ref[...].astype(BF16)

    return _call(
        body, name=name, grid=(m // tmm, nk), carried=carried,
        out_shape=[jax.ShapeDtypeStruct((m, n), BF16)],
        in_specs=[pl.BlockSpec((tmm, tk), lambda j, k: (j, k)), pl.BlockSpec((tk, n), lambda j, k: (k, 0))],
        out_specs=[pl.BlockSpec((tmm, n), lambda j, k: (j, 0))],
        scratch_shapes=[pltpu.VMEM((tmm, n), F32)],
        args=(lhs_t, rhs))


def tn_matmul(lhs, rhs, name):
    t, m = lhs.shape
    n = rhs.shape[1]
    tk = _tile(t, 2048)
    nk = t // tk

    def body(a_ref, b_ref, o_ref, acc_ref):
        k = pl.program_id(0)
        _zero_when(k == 0, acc_ref)
        acc_ref[...] += _dot(a_ref[...], b_ref[...], TN)

        @pl.when(k == nk - 1)
        def _():
            o_ref[...] = acc_ref[...].astype(BF16)

    return pl.pallas_call(
        body, name=name, grid=(nk,),
        out_shape=jax.ShapeDtypeStruct((m, n), BF16),
        in_specs=[pl.BlockSpec((tk, m), lambda k: (k, 0)), pl.BlockSpec((tk, n), lambda k: (k, 0))],
        out_specs=pl.BlockSpec((m, n), lambda k: (0, 0)),
        scratch_shapes=[pltpu.VMEM((m, n), F32)],
        compiler_params=_params(("arbitrary",)),
    )(lhs, rhs)


def mixer_proj_forward(x, mod, g, w_in_t, cw, sw, nb, carried=None):
    t, d = x.shape
    s = t // nb
    tm = _tile(s, 512)
    pieces = [(0, cw, "bf16"), (cw, cw, "bf16"), (2 * cw, cw, "bf16"), (3 * cw, sw, "f32"),
              (3 * cw + sw, d, "sig"), (3 * cw + sw + d, d, "sig")]

    def body(x_ref, mod_ref, g_ref, w_ref, *outs):
        h_ref = outs[-1]
        sh, sc, _ = _mod_rows(mod_ref, 1)
        h, _, _ = _norm_mod(x_ref[...], g_ref[...], sh, sc)
        hb = h.astype(BF16)
        h_ref[...] = hb
        for (off, width, kind), o_ref in zip(pieces, outs[:-1]):
            ck = _tile(width, 512)
            for j in range(width // ck):
                p = _dot(hb, w_ref[off + j * ck:off + (j + 1) * ck, :], NT)
                if kind == "sig":
                    p = jax.nn.sigmoid(p)
                o_ref[:, j * ck:(j + 1) * ck] = p.astype(o_ref.dtype)

    tps = s // tm
    widths = [(cw, BF16), (cw, BF16), (cw, BF16), (sw, F32), (d, BF16), (d, BF16), (d, BF16)]
    return _call(
        body, name="mixer_proj_forward", grid=(nb, tps), carried=carried,
        out_shape=[jax.ShapeDtypeStruct((t, w), dt) for w, dt in widths],
        in_specs=[_row_spec(tm, d, tps), _mod_spec(d), _const((1, d)), _const(w_in_t.shape)],
        out_specs=[_row_spec(tm, w, tps) for w, _ in widths],
        args=(x, mod, g, w_in_t))


def mixer_proj_backward(dgb, dgc, dv, du, dgla, dglb, dxo, x, mod, g, w_in_t, nb, carried=None):
    t, d = x.shape
    s = t // nb
    tm = _tile(s, 512)
    parts = [dgb, dgc, dv, du, dgla, dglb]
    offs = [0]
    for p in parts:
        offs.append(offs[-1] + p.shape[1])

    def body(*refs):
        p_refs = refs[:6]
        dxo_ref, x_ref, mod_ref, g_ref, w_ref, dx_ref, dmod_ref, dg_ref = refs[6:]
        bi, i = pl.program_id(0), pl.program_id(1)
        _zero_when(i == 0, dmod_ref)
        _zero_when(jnp.logical_and(bi == 0, i == 0), dg_ref)
        dh = jnp.zeros((tm, d), F32)
        for p_ref, off in zip(p_refs, offs):
            dh = dh + _dot(p_ref[...], w_ref[off:off + p_ref.shape[1], :])
        sh, sc, _ = _mod_rows(mod_ref, 1)
        gv = g_ref[...]
        _, n, r = _norm_mod(x_ref[...], gv, sh, sc)
        dxn, dsh, dsc, dg = _norm_mod_bwd(dh, n, r, gv, sc)
        dx_ref[...] = dxo_ref[...] + dxn
        dmod_ref[...] += jnp.concatenate([dsh, dsc], axis=0)[None]
        dg_ref[...] += dg

    tps = s // tm
    rd = _row_spec(tm, d, tps)
    return _call(
        body, name="mixer_proj_backward", grid=(nb, tps), carried=carried,
        out_shape=[jax.ShapeDtypeStruct((t, d), F32), jax.ShapeDtypeStruct((nb, 2, d), F32),
                   jax.ShapeDtypeStruct((1, d), F32)],
        in_specs=[_row_spec(tm, p.shape[1], tps) for p in parts]
        + [rd, rd, _mod_spec(d), _const((1, d)), _const(w_in_t.shape)],
        out_specs=[rd, pl.BlockSpec((1, 2, d), lambda b, i: (b, 0, 0)), _full((1, d))],
        args=(*parts, dxo, x, mod, g, w_in_t))


GROUPS_PER_BLOCK = 8
ROWS = 8
SCAN_LANES = 512
SCAN_UNROLL = 8


def _scan_rows(xr, xi, masks, shifts):
    for (mr, mi), sft in zip(masks, shifts):
        sr, si = pltpu.roll(xr, sft, 0), pltpu.roll(xi, sft, 0)
        xr, xi = xr + mr * sr - mi * si, xi + mr * si + mi * sr
    return xr, xi


def _cmul_add(ar, ai, cr, ci, br, bi):
    return ar * cr - ai * ci + br, ar * ci + ai * cr + bi


def _segment_rows(perm_ref, x):
    return _dot(perm_ref[0], x).astype(BF16)


def _time_rows(perm_ref, x):
    hi = x.astype(BF16)
    lo = (x - hi.astype(F32)).astype(BF16)
    return _dot(perm_ref[1], hi) + _dot(perm_ref[1], lo)


def segment_permutation(tc):
    r = jnp.arange(tc)
    p = (r[:, None] % ROWS * (tc // ROWS) + r[:, None] // ROWS == r[None, :]).astype(BF16)
    return jnp.stack([p, p.T])


def _scan_loop(n, step, init):
    def trip(j, carry):
        for r in range(SCAN_UNROLL):
            carry = step(j * SCAN_UNROLL + r, carry)
        return carry

    return lax.fori_loop(0, n // SCAN_UNROLL, trip, init)


def _rows_at(k, offset=0):
    return pl.ds(pl.multiple_of(k * ROWS + offset, ROWS), ROWS)


def ssm_forward(u, bd, cd, a1, ml, nb):
    t, w = u.shape
    s = t // nb
    tc = _tile(s, 256)
    seg = tc // ROWS
    nq, ub, lq = bd.shape[1], bd.shape[2], bd.shape[3]
    nl = nq * lq
    nch = s // tc
    lw = min(nl, SCAN_LANES)

    def body(u_ref, perm_ref, bd_ref, cd_ref, a1_ref, ml_ref, y_ref, st_ref, xr_s, xi_s, car_s):
        i = pl.program_id(1)

        @pl.when(i == 0)
        def _():
            car_s[...] = jnp.zeros_like(car_s)

        ub16 = _segment_rows(perm_ref, u_ref[...].astype(BF16))
        for q in range(nq):
            lanes = slice(q * lq, (q + 1) * lq)
            uq = ub16[:, q * ub:(q + 1) * ub]
            xr_s[:, lanes] = _dot(uq, bd_ref[0, q])
            xi_s[:, lanes] = _dot(uq, bd_ref[1, q])
        row_is_0 = lax.broadcasted_iota(jnp.int32, (ROWS, lw), 0) == 0
        zero = jnp.zeros((ROWS, lw), F32)
        for j in range(nl // lw):
            lanes = slice(j * lw, (j + 1) * lw)
            ar, ai = a1_ref[0, :, lanes], a1_ref[1, :, lanes]

            def local(k, c):
                return _cmul_add(ar, ai, c[0], c[1], xr_s[_rows_at(k), lanes], xi_s[_rows_at(k), lanes])

            er, ei = _scan_loop(seg, local, (zero, zero))
            masks = [(ml_ref[d, 0, :, lanes], ml_ref[d, 1, :, lanes]) for d in range(3)]
            cr, ci = _scan_rows(jnp.where(row_is_0, car_s[0, :, lanes], pltpu.roll(er, 1, 0)),
                                jnp.where(row_is_0, car_s[1, :, lanes], pltpu.roll(ei, 1, 0)), masks, (1, 2, 4))
            st_ref[0, 0, :, lanes] = cr
            st_ref[0, 1, :, lanes] = ci

            def full(k, c):
                xr, xi = _cmul_add(ar, ai, c[0], c[1], xr_s[_rows_at(k), lanes], xi_s[_rows_at(k), lanes])
                xr_s[_rows_at(k), lanes] = xr
                xi_s[_rows_at(k), lanes] = xi
                return xr, xi

            fr, fi = _scan_loop(seg, full, (cr, ci))
            car_s[0, :, lanes] = jnp.broadcast_to(fr[ROWS - 1:ROWS], fr.shape)
            car_s[1, :, lanes] = jnp.broadcast_to(fi[ROWS - 1:ROWS], fi.shape)
        y = jnp.concatenate([_dot(xr_s[:, q * lq:(q + 1) * lq].astype(BF16), cd_ref[0, q])
                             + _dot(xi_s[:, q * lq:(q + 1) * lq].astype(BF16), cd_ref[1, q]) for q in range(nq)],
                            axis=1)
        y_ref[...] = _time_rows(perm_ref, y)

    perm = segment_permutation(tc)
    return pl.pallas_call(
        body, name="ssm_forward", grid=(nb, nch),
        out_shape=[jax.ShapeDtypeStruct((t, w), F32), jax.ShapeDtypeStruct((nb * nch, 2, ROWS, nl), F32)],
        in_specs=[pl.BlockSpec((tc, w), lambda b, i: (b * nch + i, 0)), _const(perm.shape), _const(bd.shape),
                  _const(cd.shape), _const(a1.shape), _const(ml.shape)],
        out_specs=[pl.BlockSpec((tc, w), lambda b, i: (b * nch + i, 0)),
                   pl.BlockSpec((1, 2, ROWS, nl), lambda b, i: (b * nch + i, 0, 0, 0))],
        scratch_shapes=[pltpu.VMEM((tc, nl), F32), pltpu.VMEM((tc, nl), F32), pltpu.VMEM((2, ROWS, nl), F32)],
        compiler_params=_params(("arbitrary", "arbitrary")),
    )(u, perm, bd, cd, a1, ml)


def ssm_backward(u, dy, st, bd, cd, a1, mlb, dskip, nb, carried=None):
    t, w = u.shape
    s = t // nb
    tc = _tile(s, 256)
    seg = tc // ROWS
    nq, ub, lq = bd.shape[1], bd.shape[2], bd.shape[3]
    nl = nq * lq
    nch = s // tc
    lw = min(nl, SCAN_LANES)

    def body(u_ref, dy_ref, st_ref, perm_ref, bd_ref, cd_ref, a1_ref, mlb_ref, dsk_ref,
             du_ref, dab_ref, dbd_ref, dcd_ref, xr_s, xi_s, lr_s, li_s, car_s):
        bi, i = pl.program_id(0), pl.program_id(1)
        first = jnp.logical_and(bi == 0, i == 0)

        @pl.when(i == 0)
        def _():
            car_s[...] = jnp.zeros_like(car_s)

        @pl.when(first)
        def _():
            dab_ref[...] = jnp.zeros_like(dab_ref)
            dbd_ref[...] = jnp.zeros_like(dbd_ref)
            dcd_ref[...] = jnp.zeros_like(dcd_ref)

        ub16 = _segment_rows(perm_ref, u_ref[...].astype(BF16))
        dyb16 = _segment_rows(perm_ref, dy_ref[...].astype(BF16))
        xr_s[0:ROWS, :] = st_ref[0, 0]
        xi_s[0:ROWS, :] = st_ref[0, 1]
        for q in range(nq):
            lanes = slice(q * lq, (q + 1) * lq)
            uq = ub16[:, q * ub:(q + 1) * ub]
            dq = dyb16[:, q * ub:(q + 1) * ub]
            xr_s[ROWS:, lanes] = _dot(uq, bd_ref[0, q])
            xi_s[ROWS:, lanes] = _dot(uq, bd_ref[1, q])
            lr_s[:, lanes] = _dot(dq, cd_ref[0, q], NT)
            li_s[:, lanes] = _dot(dq, cd_ref[1, q], NT)
        row_is_7 = lax.broadcasted_iota(jnp.int32, (ROWS, lw), 0) == ROWS - 1
        zero = jnp.zeros((ROWS, lw), F32)
        for j in range(nl // lw):
            lanes = slice(j * lw, (j + 1) * lw)
            ar, ai = a1_ref[0, :, lanes], a1_ref[1, :, lanes]
            nai = -ai

            def states(k, c):
                xr, xi = _cmul_add(ar, ai, c[0], c[1], xr_s[_rows_at(k, ROWS), lanes], xi_s[_rows_at(k, ROWS), lanes])
                xr_s[_rows_at(k, ROWS), lanes] = xr
                xi_s[_rows_at(k, ROWS), lanes] = xi
                return xr, xi

            _scan_loop(seg, states, (st_ref[0, 0, :, lanes], st_ref[0, 1, :, lanes]))

            def local(kk, c):
                k = seg - 1 - kk
                return _cmul_add(ar, nai, c[0], c[1], lr_s[_rows_at(k), lanes], li_s[_rows_at(k), lanes])

            er, ei = _scan_loop(seg, local, (zero, zero))
            masks = [(mlb_ref[d, 0, :, lanes], mlb_ref[d, 1, :, lanes]) for d in range(3)]
            cr, ci = _scan_rows(jnp.where(row_is_7, car_s[0, :, lanes], pltpu.roll(er, ROWS - 1, 0)),
                                jnp.where(row_is_7, car_s[1, :, lanes], pltpu.roll(ei, ROWS - 1, 0)), masks, (7, 6, 4))

            def full(kk, c):
                cr_, ci_, accr, acci = c
                k = seg - 1 - kk
                lr, li = _cmul_add(ar, nai, cr_, ci_, lr_s[_rows_at(k), lanes], li_s[_rows_at(k), lanes])
                lr_s[_rows_at(k), lanes] = lr
                li_s[_rows_at(k), lanes] = li
                xpr, xpi = xr_s[_rows_at(k), lanes], xi_s[_rows_at(k), lanes]
                return lr, li, accr + lr * xpr + li * xpi, acci + li * xpr - lr * xpi

            lr0, li0, accr, acci = _scan_loop(seg, full, (cr, ci, zero, zero))
            car_s[0, :, lanes] = jnp.broadcast_to(lr0[0:1], lr0.shape)
            car_s[1, :, lanes] = jnp.broadcast_to(li0[0:1], li0.shape)
            dab_ref[0, :, lanes] += accr
            dab_ref[1, :, lanes] += acci
        du_parts = []
        for q in range(nq):
            lanes = slice(q * lq, (q + 1) * lq)
            cols = slice(q * ub, (q + 1) * ub)
            lrb, lib = lr_s[:, lanes].astype(BF16), li_s[:, lanes].astype(BF16)
            uq, dq = ub16[:, cols], dyb16[:, cols]
            du_parts.append(_dot(lrb, bd_ref[0, q], NT) + _dot(lib, bd_ref[1, q], NT))
            dbd_ref[0, q] += _dot(uq, lrb, TN)
            dbd_ref[1, q] += _dot(uq, lib, TN)
            dcd_ref[0, q] += _dot(xr_s[ROWS:, lanes].astype(BF16), dq, TN)
            dcd_ref[1, q] += _dot(xi_s[ROWS:, lanes].astype(BF16), dq, TN)
        du = _time_rows(perm_ref, jnp.concatenate(du_parts, axis=1)) + dsk_ref[...] * dy_ref[...]
        du_ref[...] = du.astype(BF16)

    rev = lambda b, i: (b * nch + nch - 1 - i, 0)
    perm = segment_permutation(tc)
    return _call(
        body, name="ssm_backward", grid=(nb, nch), carried=carried,
        out_shape=[jax.ShapeDtypeStruct((t, w), BF16), jax.ShapeDtypeStruct((2, ROWS, nl), F32),
                   jax.ShapeDtypeStruct(bd.shape, F32), jax.ShapeDtypeStruct(cd.shape, F32)],
        in_specs=[pl.BlockSpec((tc, w), rev), pl.BlockSpec((tc, w), rev),
                  pl.BlockSpec((1, 2, ROWS, nl), lambda b, i: (b * nch + nch - 1 - i, 0, 0, 0)),
                  _const(perm.shape), _const(bd.shape), _const(cd.shape), _const(a1.shape), _const(mlb.shape),
                  _const((1, w))],
        out_specs=[pl.BlockSpec((tc, w), rev), _full((2, ROWS, nl)), _full(bd.shape), _full(cd.shape)],
        scratch_shapes=[pltpu.VMEM((tc + ROWS, nl), F32), pltpu.VMEM((tc + ROWS, nl), F32),
                        pltpu.VMEM((tc, nl), F32), pltpu.VMEM((tc, nl), F32), pltpu.VMEM((2, ROWS, nl), F32)],
        args=(u, dy, st, perm, bd, cd, a1, mlb, dskip))


def ssm_discretise(a_re, a_im, b_re, b_im, log_dt):
    dt = jnp.exp(log_dt)[:, None]
    er = jnp.exp(a_re * dt)
    abr, abi = er * jnp.cos(a_im * dt), er * jnp.sin(a_im * dt)
    den = a_re * a_re + a_im * a_im
    nr, ni = abr - 1.0, abi
    fr = ((nr * a_re + ni * a_im) / den)[..., None]
    fi = ((ni * a_re - nr * a_im) / den)[..., None]
    return abr, abi, fr * b_re - fi * b_im, fr * b_im + fi * b_re


def _complex_square(zr, zi):
    return zr * zr - zi * zi, 2.0 * zr * zi


def ssm_tables(abr, abi, bbr, bbi, c_re, c_im, seg):
    g, p, h = bbr.shape
    nq = g // GROUPS_PER_BLOCK
    zr, zi = abr.reshape(1, -1), abi.reshape(1, -1)
    a1 = jnp.stack([jnp.broadcast_to(zr, (ROWS, g * p)), jnp.broadcast_to(zi, (ROWS, g * p))])
    for _ in range(seg.bit_length() - 1):
        zr, zi = _complex_square(zr, zi)
    row = jnp.arange(ROWS)[:, None]
    ml, mlb = [], []
    for d in (1, 2, 4):
        ml.append(jnp.stack([jnp.where(row >= d, zr, 0.0), jnp.where(row >= d, zi, 0.0)]))
        mlb.append(jnp.stack([jnp.where(row + d < ROWS, zr, 0.0), jnp.where(row + d < ROWS, -zi, 0.0)]))
        zr, zi = _complex_square(zr, zi)
    eye = jnp.eye(GROUPS_PER_BLOCK, dtype=F32)

    def block_diag_in(bb):
        bq = bb.reshape(nq, GROUPS_PER_BLOCK, p, h)
        return jnp.einsum("qaph,ab->qahbp", bq, eye).reshape(nq, GROUPS_PER_BLOCK * h, GROUPS_PER_BLOCK * p)

    def block_diag_out(cc):
        cq = cc.reshape(nq, GROUPS_PER_BLOCK, h, p)
        return jnp.einsum("qahp,ab->qapbh", cq, eye).reshape(nq, GROUPS_PER_BLOCK * p, GROUPS_PER_BLOCK * h)

    bd = jnp.stack([block_diag_in(bbr), block_diag_in(bbi)]).astype(BF16)
    cd = jnp.stack([block_diag_out(c_re), block_diag_out(-c_im)]).astype(BF16)
    return bd, cd, a1, jnp.stack(ml), jnp.stack(mlb)


def ssm_table_grads(dab, dbd, dcd, g, p, h):
    nq = g // GROUPS_PER_BLOCK
    dabr, dabi = dab[0].sum(0).reshape(g, p), dab[1].sum(0).reshape(g, p)
    b5 = dbd.reshape(2, nq, GROUPS_PER_BLOCK, h, GROUPS_PER_BLOCK, p)
    dbb = jnp.einsum("rqahap->rqaph", b5).reshape(2, g, p, h)
    c5 = dcd.reshape(2, nq, GROUPS_PER_BLOCK, p, GROUPS_PER_BLOCK, h)
    dcc = jnp.einsum("rqapah->rqahp", c5).reshape(2, g, h, p)
    return dabr, dabi, dbb[0], dbb[1], dcc[0], -dcc[1]


HALO = 16


def _conv_inputs(gc_ref, v_ref, gch_ref, vh_ref, cv_s, i, tm):
    cv = gc_ref[...].astype(F32) * v_ref[...].astype(F32)
    halo = gch_ref[...].astype(F32) * vh_ref[...].astype(F32)
    cv_s[0:HALO, :] = jnp.where(i == 0, 0.0, halo)
    cv_s[HALO:, :] = cv
    return cv, cv_s[HALO - 1:HALO - 1 + tm, :], cv_s[HALO - 2:HALO - 2 + tm, :]


def _halo_spec(tm, width, tiles_per_seq):
    per = tm // HALO
    return pl.BlockSpec((HALO, width), lambda b, i: (jnp.maximum((b * tiles_per_seq + i) * per - 1, 0), 0))


def mixer_merge_forward(x, gb, gc, v, sga, sgb, yssm, u, mod, conv_w, dskip, wco, wglu, wso_t, wout, nb):
    t, d = x.shape
    cw, sw = gb.shape[1], u.shape[1]
    s = t // nb
    tm = _tile(s, 512)
    tps = s // tm

    def body(x_ref, gb_ref, gc_ref, v_ref, gch_ref, vh_ref, sga_ref, sgb_ref, ys_ref, u_ref, mod_ref, cw_ref,
             dsk_ref, wco_ref, wglu_ref, wso_ref, wout_ref, xo_ref, ya_ref, yb_ref, mix_ref, cv_s):
        i = pl.program_id(1)
        cv, cv1, cv2 = _conv_inputs(gc_ref, v_ref, gch_ref, vh_ref, cv_s, i, tm)
        w = cw_ref[...]
        conv = w[0:1] * cv2 + w[1:2] * cv1 + w[2:3] * cv
        ya = _dot((gb_ref[...].astype(F32) * conv).astype(BF16), wco_ref[...])
        s0 = ys_ref[...] + dsk_ref[...] * u_ref[...]
        s1, _ = _gelu(s0)
        z = _dot(s1.astype(BF16), wglu_ref[...])
        s2 = s1 * jax.nn.sigmoid(z)
        yb = _dot(s2.astype(BF16), wso_ref[...], NT)
        merged = sga_ref[...].astype(F32) * ya + sgb_ref[...].astype(F32) * yb
        mix = _dot(merged.astype(BF16), wout_ref[...])
        _, _, gt = _mod_rows(mod_ref, 1)
        xo_ref[...] = x_ref[...] + gt * mix
        ya_ref[...] = ya.astype(BF16)
        yb_ref[...] = yb.astype(BF16)
        mix_ref[...] = mix.astype(BF16)

    rd, rc, rw = _row_spec(tm, d, tps), _row_spec(tm, cw, tps), _row_spec(tm, sw, tps)
    hc = _halo_spec(tm, cw, tps)
    return pl.pallas_call(
        body, name="mixer_merge_forward", grid=(nb, tps),
        out_shape=[jax.ShapeDtypeStruct((t, d), F32)] + [jax.ShapeDtypeStruct((t, d), BF16)] * 3,
        in_specs=[rd, rc, rc, rc, hc, hc, rd, rd, rw, rw, _mod_spec(d), _const(conv_w.shape), _const((1, sw)),
                  _const(wco.shape), _const(wglu.shape), _const(wso_t.shape), _const(wout.shape)],
        out_specs=[rd, rd, rd, rd],
        scratch_shapes=[pltpu.VMEM((tm + HALO, cw), F32)],
        compiler_params=_params(("arbitrary", "arbitrary")),
    )(x, gb, gc, v, gc, v, sga, sgb, yssm, u, mod, conv_w, dskip, wco, wglu, wso_t, wout)


def mixer_merge_backward(dxo, mix, ya, yb, gb, gc, v, sga, sgb, yssm, u, mod, conv_w, dskip,
                         wco, wglu, wso_t, wout, nb, carried=None):
    t, d = dxo.shape
    cw, sw = gb.shape[1], u.shape[1]
    s = t // nb
    tm = _tile(s, 256)
    tps = s // tm

    def body(dxo_ref, mix_ref, ya_ref, yb_ref, gb_ref, gc_ref, v_ref, gch_ref, vh_ref, sga_ref, sgb_ref, ys_ref,
             u_ref, mod_ref, cw_ref, dsk_ref, wco_ref, wglu_ref, wso_ref, wout_ref,
             dgla_ref, dglb_ref, dgb_ref, dconv_ref, ds0_ref, dgt_ref, ddsk_ref, dwout_ref, dwco_ref, dwso_ref,
             dwglu_ref, cv_s):
        bi, i = pl.program_id(0), pl.program_id(1)
        _zero_when(i == 0, dgt_ref)
        _zero_when(jnp.logical_and(bi == 0, i == 0), ddsk_ref, dwout_ref, dwco_ref, dwso_ref, dwglu_ref)
        dxo = dxo_ref[...]
        _, _, gt = _mod_rows(mod_ref, 1)
        dmix = (gt * dxo).astype(BF16)
        dmerged = _dot(dmix, wout_ref[...], NT)
        uv = u_ref[...]
        s0 = ys_ref[...] + dsk_ref[...] * uv
        s1, th = _gelu(s0)
        s1b = s1.astype(BF16)
        sz = jax.nn.sigmoid(_dot(s1b, wglu_ref[...]))
        s2b = (s1 * sz).astype(BF16)
        cv, cv1, cv2 = _conv_inputs(gc_ref, v_ref, gch_ref, vh_ref, cv_s, i, tm)
        w = cw_ref[...]
        conv = w[0:1] * cv2 + w[1:2] * cv1 + w[2:3] * cv
        gbv = gb_ref[...].astype(F32)
        ya, yb = ya_ref[...].astype(F32), yb_ref[...].astype(F32)
        sga, sgb = sga_ref[...].astype(F32), sgb_ref[...].astype(F32)
        merged = (sga * ya + sgb * yb).astype(BF16)
        dwout_ref[...] += _dot(merged, dmix, TN)
        dgt_ref[...] += jnp.sum(dxo * mix_ref[...].astype(F32), axis=0, keepdims=True)[None]
        dgla_ref[...] = (dmerged * ya * sga * (1.0 - sga)).astype(BF16)
        dglb_ref[...] = (dmerged * yb * sgb * (1.0 - sgb)).astype(BF16)
        dya = (dmerged * sga).astype(BF16)
        dyb = (dmerged * sgb).astype(BF16)
        dwco_ref[...] += _dot((gbv * conv).astype(BF16), dya, TN)
        dya_in = _dot(dya, wco_ref[...], NT)
        dgb_ref[...] = (dya_in * conv).astype(BF16)
        dconv_ref[...] = dya_in * gbv
        dwso_ref[...] += _dot(dyb, s2b, TN)
        ds2 = _dot(dyb, wso_ref[...])
        dz = (ds2 * s1 * sz * (1.0 - sz)).astype(BF16)
        dwglu_ref[...] += _dot(s1b, dz, TN)
        ds1 = ds2 * sz + _dot(dz, wglu_ref[...], NT)
        ds0 = ds1 * _gelu_grad(s0, th)
        ds0_ref[...] = ds0
        ddsk_ref[...] += jnp.sum(ds0 * uv, axis=0, keepdims=True)

    rd, rc, rw = _row_spec(tm, d, tps), _row_spec(tm, cw, tps), _row_spec(tm, sw, tps)
    hc = _halo_spec(tm, cw, tps)
    return _call(
        body, name="mixer_merge_backward", grid=(nb, tps), carried=carried,
        out_shape=[jax.ShapeDtypeStruct((t, d), BF16), jax.ShapeDtypeStruct((t, d), BF16),
                   jax.ShapeDtypeStruct((t, cw), BF16), jax.ShapeDtypeStruct((t, cw), F32),
                   jax.ShapeDtypeStruct((t, sw), F32), jax.ShapeDtypeStruct((nb, 1, d), F32),
                   jax.ShapeDtypeStruct((1, sw), F32), jax.ShapeDtypeStruct(wout.shape, F32),
                   jax.ShapeDtypeStruct(wco.shape, F32), jax.ShapeDtypeStruct(wso_t.shape, F32),
                   jax.ShapeDtypeStruct(wglu.shape, F32)],
        in_specs=[rd, rd, rd, rd, rc, rc, rc, hc, hc, rd, rd, rw, rw, _mod_spec(d), _const(conv_w.shape),
                  _const((1, sw)), _const(wco.shape), _const(wglu.shape), _const(wso_t.shape), _const(wout.shape)],
        out_specs=[rd, rd, rc, rc, rw, pl.BlockSpec((1, 1, d), lambda b, i: (b, 0, 0)), _full((1, sw)),
                   _full(wout.shape), _full(wco.shape), _full(wso_t.shape), _full(wglu.shape)],
        scratch_shapes=[pltpu.VMEM((tm + HALO, cw), F32)],
        args=(dxo, mix, ya, yb, gb, gc, v, gc, v, sga, sgb, yssm, u, mod, conv_w, dskip, wco, wglu, wso_t, wout))


def conv_backward(dconv, gc, v, conv_w, nb):
    t, cw = dconv.shape
    s = t // nb
    tm = _tile(s, 512)
    tps = s // tm
    per = tm // ROWS
    slab = _tile(tm, 32)

    def body(dc_ref, dcn_ref, gc_ref, v_ref, gch_ref, vh_ref, cw_ref, dgc_ref, dv_ref, dw_ref, cv_s, dc_s):
        bi, i = pl.program_id(0), pl.program_id(1)
        _zero_when(jnp.logical_and(bi == 0, i == 0), dw_ref)
        cv_s[0:HALO, :] = jnp.where(i == 0, 0.0, gch_ref[...].astype(F32) * vh_ref[...].astype(F32))
        dc_s[tm:, :] = jnp.where(i == tps - 1, 0.0, dcn_ref[...])
        for r in range(0, tm, slab):
            cv_s[HALO + r:HALO + r + slab, :] = (gc_ref[r:r + slab, :].astype(F32)
                                                 * v_ref[r:r + slab, :].astype(F32))
            dc_s[r:r + slab, :] = dc_ref[r:r + slab, :]
        w = cw_ref[...]
        sums = [jnp.zeros((ROWS, cw), F32)] * CONV_K
        for r in range(0, tm, slab):
            dc = dc_s[r:r + slab, :]
            dcv = w[2:3] * dc + w[1:2] * dc_s[r + 1:r + 1 + slab, :] + w[0:1] * dc_s[r + 2:r + 2 + slab, :]
            dgc_ref[r:r + slab, :] = (dcv * v_ref[r:r + slab, :].astype(F32)).astype(BF16)
            dv_ref[r:r + slab, :] = (dcv * gc_ref[r:r + slab, :].astype(F32)).astype(BF16)
            for k in range(CONV_K):
                lag = HALO + r - (CONV_K - 1 - k)
                prod = dc * cv_s[lag:lag + slab, :]
                sums[k] = sums[k] + jnp.sum(prod.reshape(slab // ROWS, ROWS, cw), axis=0)
        dw_ref[...] += jnp.concatenate([jnp.sum(a, axis=0, keepdims=True) for a in sums], axis=0)

    rc = _row_spec(tm, cw, tps)
    nxt = pl.BlockSpec((ROWS, cw), lambda b, i: (jnp.minimum((b * tps + i + 1) * per, nb * tps * per - 1), 0))
    hc = _halo_spec(tm, cw, tps)
    return pl.pallas_call(
        body, name="conv_backward", grid=(nb, tps),
        out_shape=[jax.ShapeDtypeStruct((t, cw), BF16), jax.ShapeDtypeStruct((t, cw), BF16),
                   jax.ShapeDtypeStruct(conv_w.shape, F32)],
        in_specs=[rc, nxt, rc, rc, hc, hc, _full(conv_w.shape)],
        out_specs=[rc, rc, _full(conv_w.shape)],
        scratch_shapes=[pltpu.VMEM((tm + HALO, cw), F32), pltpu.VMEM((tm + ROWS, cw), F32)],
        compiler_params=_params(("arbitrary", "arbitrary")),
    )(dconv, dconv, gc, v, gc, v, conv_w)


def loss_forward_backward(x, target, g):
    t, d = x.shape
    tm = _tile(t, 512)

    def body(x_ref, t_ref, g_ref, l_ref, dx_ref, dg_ref):
        _zero_when(pl.program_id(0) == 0, dg_ref)
        xv = x_ref[...]
        gv = g_ref[...]
        r = lax.rsqrt(jnp.mean(xv * xv, axis=-1, keepdims=True) + EPS)
        n = xv * r
        err = n * gv - t_ref[...]
        l_ref[...] = jnp.full(l_ref.shape, 0.5 * jnp.sum(jnp.mean(err * err, axis=-1)), F32)
        dy = err * (1.0 / d)
        dn = dy * gv
        dx_ref[...] = r * (dn - n * jnp.mean(n * dn, axis=-1, keepdims=True))
        dg_ref[...] += jnp.sum(dy * n, axis=0, keepdims=True)

    row = pl.BlockSpec((tm, d), lambda i: (i, 0))
    return pl.pallas_call(
        body, name="loss_forward_backward", grid=(t // tm,),
        out_shape=[jax.ShapeDtypeStruct((t // tm, 1, 128), F32), jax.ShapeDtypeStruct((t, d), F32),
                   jax.ShapeDtypeStruct((1, d), F32)],
        in_specs=[row, row, _full((1, d))],
        out_specs=[pl.BlockSpec((1, 1, 128), lambda i: (i, 0, 0)), row, _full((1, d))],
        compiler_params=_params(("arbitrary",)),
    )(x, target, g)


def sum_slots(slots, name):
    _, r, c = slots.shape
    tr = _tile(r, 352) if r % 352 == 0 else _tile(r, 256)
    if tr < 128:
        tr = r

    def body(s_ref, o_ref):
        acc = s_ref[0].astype(F32)
        for j in range(1, N_DEV):
            acc = acc + s_ref[j].astype(F32)
        o_ref[...] = acc

    return pl.pallas_call(
        body, name=name, grid=(r // tr,),
        out_shape=jax.ShapeDtypeStruct((r, c), F32),
        in_specs=[pl.BlockSpec((N_DEV, tr, c), lambda i: (0, i, 0))],
        out_specs=pl.BlockSpec((tr, c), lambda i: (i, 0)),
        compiler_params=_params(("arbitrary",)),
    )(slots)


def adamw_update(w, g, m, v, name):
    r, c = w.shape
    tr = _tile(r, 256) if r % 8 == 0 else r

    def body(w_ref, g_ref, m_ref, v_ref, d_ref, mo_ref, vo_ref):
        d_ref[...], mo_ref[...], vo_ref[...] = _adamw(w_ref[...], g_ref[...], m_ref[...], v_ref[...])

    spec = pl.BlockSpec((tr, c), lambda i: (i, 0))
    return pl.pallas_call(
        body, name=name, grid=(r // tr,),
        out_shape=[jax.ShapeDtypeStruct((r, c), F32)] * 3,
        in_specs=[spec] * 4, out_specs=[spec] * 3,
        compiler_params=_params(("arbitrary",)),
    )(w, g, m, v)


def sum_adamw_update(slots, w, m, v, name):
    _, r, c = slots.shape
    tr = _tile(r, 352) if r % 352 == 0 else _tile(r, 256)

    def body(s_ref, w_ref, m_ref, v_ref, g_ref, d_ref, mo_ref, vo_ref):
        g = s_ref[0].astype(F32)
        for j in range(1, N_DEV):
            g = g + s_ref[j].astype(F32)
        g_ref[...] = g
        d_ref[...], mo_ref[...], vo_ref[...] = _adamw(w_ref[...], g, m_ref[...], v_ref[...])

    spec = pl.BlockSpec((tr, c), lambda i: (i, 0))
    return pl.pallas_call(
        body, name=name, grid=(r // tr,),
        out_shape=[jax.ShapeDtypeStruct((r, c), F32)] * 4,
        in_specs=[pl.BlockSpec((N_DEV, tr, c), lambda i: (0, i, 0))] + [spec] * 3, out_specs=[spec] * 4,
        compiler_params=_params(("arbitrary",)),
    )(slots, w, m, v)


def adamw_update_small(ws, gs, ms, vs):
    n = len(ws)

    def body(*refs):
        w_r, g_r, m_r, v_r = refs[:n], refs[n:2 * n], refs[2 * n:3 * n], refs[3 * n:4 * n]
        d_r, mo_r, vo_r = refs[4 * n:5 * n], refs[5 * n:6 * n], refs[6 * n:7 * n]
        for k in range(n):
            d_r[k][...], mo_r[k][...], vo_r[k][...] = _adamw(w_r[k][...], g_r[k][...], m_r[k][...], v_r[k][...])

    shapes = [jax.ShapeDtypeStruct(w.shape, F32) for w in ws]
    out = pl.pallas_call(body, name="adamw_update_small", out_shape=shapes * 3,
                         compiler_params=_params())(*ws, *gs, *ms, *vs)
    return out[:n], out[n:2 * n], out[2 * n:]


def _slots(grad_t):
    return grad_t.reshape(N_DEV, grad_t.shape[0] // N_DEV, grad_t.shape[1])


def kernel(x, c, w_ada, b_ada, g_ffn1, w1_a, w3_a, w2_a, g_mix, w_in, conv_w, w_conv_out, a_re, a_im, b_re, b_im, c_re, c_im, log_dt, d_skip, w_glu, w_ssm_out, w_out, g_ffn2, w1_b, w3_b, w2_b, g_final, loss_target, m_w_ada, m_b_ada, m_g_ffn1, m_w1_a, m_w3_a, m_w2_a, m_g_mix, m_w_in, m_conv_w, m_w_conv_out, m_a_re, m_a_im, m_b_re, m_b_im, m_c_re, m_c_im, m_log_dt, m_d_skip, m_w_glu, m_w_ssm_out, m_w_out, m_g_ffn2, m_w1_b, m_w3_b, m_w2_b, m_g_final, v_w_ada, v_b_ada, v_g_ffn1, v_w1_a, v_w3_a, v_w2_a, v_g_mix, v_w_in, v_conv_w, v_w_conv_out, v_a_re, v_a_im, v_b_re, v_b_im, v_c_re, v_c_im, v_log_dt, v_d_skip, v_w_glu, v_w_ssm_out, v_w_out, v_g_ffn2, v_w1_b, v_w3_b, v_w2_b, v_g_final):
    nb, s, d = x.shape
    t = nb * s
    me = 4 * lax.axis_index("x") + 2 * lax.axis_index("y") + lax.axis_index("c")
    g_n, p_n, h_n = b_re.shape[1:]
    cw_n = w_conv_out.shape[1] * N_DEV
    sw_n = w_ssm_out.shape[1]
    glu_fold = d // w_glu.shape[2]

    weights = dict(w_ada=w_ada, b_ada=b_ada, g_ffn1=g_ffn1, w1_a=w1_a, w3_a=w3_a, w2_a=w2_a, g_mix=g_mix, w_in=w_in,
                   conv_w=conv_w, w_conv_out=w_conv_out, a_re=a_re, a_im=a_im, b_re=b_re, b_im=b_im, c_re=c_re,
                   c_im=c_im, log_dt=log_dt, d_skip=d_skip, w_glu=w_glu, w_ssm_out=w_ssm_out, w_out=w_out,
                   g_ffn2=g_ffn2, w1_b=w1_b, w3_b=w3_b, w2_b=w2_b, g_final=g_final)
    mom1 = dict(w_ada=m_w_ada, b_ada=m_b_ada, g_ffn1=m_g_ffn1, w1_a=m_w1_a, w3_a=m_w3_a, w2_a=m_w2_a, g_mix=m_g_mix,
                w_in=m_w_in, conv_w=m_conv_w, w_conv_out=m_w_conv_out, a_re=m_a_re, a_im=m_a_im, b_re=m_b_re,
                b_im=m_b_im, c_re=m_c_re, c_im=m_c_im, log_dt=m_log_dt, d_skip=m_d_skip, w_glu=m_w_glu,
                w_ssm_out=m_w_ssm_out, w_out=m_w_out, g_ffn2=m_g_ffn2, w1_b=m_w1_b, w3_b=m_w3_b, w2_b=m_w2_b,
                g_final=m_g_final)
    mom2 = dict(w_ada=v_w_ada, b_ada=v_b_ada, g_ffn1=v_g_ffn1, w1_a=v_w1_a, w3_a=v_w3_a, w2_a=v_w2_a, g_mix=v_g_mix,
                w_in=v_w_in, conv_w=v_conv_w, w_conv_out=v_w_conv_out, a_re=v_a_re, a_im=v_a_im, b_re=v_b_re,
                b_im=v_b_im, c_re=v_c_re, c_im=v_c_im, log_dt=v_log_dt, d_skip=v_d_skip, w_glu=v_w_glu,
                w_ssm_out=v_w_ssm_out, w_out=v_w_out, g_ffn2=v_g_ffn2, w1_b=v_w1_b, w3_b=v_w3_b, w2_b=v_w2_b,
                g_final=v_g_final)
    names = list(weights)
    transposed = ("w1_a", "w3_a", "w_in", "w_ssm_out", "w1_b", "w3_b")
    groups = dict(ffn_a=("w1_a", "w3_a", "w2_a"), mixer=("w_in", "w_conv_out", "w_glu", "w_ssm_out", "w_out"),
                  ffn_b=("w1_b", "w3_b", "w2_b"))
    big = groups["ffn_a"] + groups["mixer"] + groups["ffn_b"]

    def shard_rows(name):
        w = weights[name][0]
        if name in transposed:
            w = w.T
        if name == "w_glu":
            w = w.reshape(w.shape[0] // glu_fold, d)
        return w.astype(BF16)

    def gather_stage(group):
        return GatherStage([shard_rows(n) for n in groups[group]])

    gw = {}

    def keep_weights(group, outs):
        for n, w in zip(groups[group], outs):
            gw[n] = w.reshape(sw_n, sw_n) if n == "w_glu" else w

    pad_rows = lambda a: jnp.pad(a, ((0, -a.shape[0] % GATHER_ROWS), (0, 0)))
    c_all, conv_all, *ffn_a_weights = run_stage(
        StageGroup([GatherStage([pad_rows(c), pad_rows(conv_w[0])]), gather_stage("ffn_a")]), "gather_cond_ffn_a")
    keep_weights("ffn_a", ffn_a_weights)
    c_all = c_all.reshape(N_DEV, -1, d)[:, :nb].reshape(N_DEV * nb, d)
    conv_full = conv_all.reshape(N_DEV, GATHER_ROWS, -1)[:, :CONV_K].transpose(1, 0, 2).reshape(CONV_K, cw_n)
    ada_cols = w_ada.shape[2]
    b_cols = lax.dynamic_slice(b_ada, (0, me * ada_cols), (1, ada_cols))
    mod_cols = ada_forward(c_all, w_ada[0], b_cols)
    (mod_all,) = run_stage(GatherStage([mod_cols]), "gather_mod")
    mod_mine = lax.dynamic_slice(mod_all.reshape(N_DEV, N_DEV * nb, ada_cols), (0, me * nb, 0), (N_DEV, nb, ada_cols))
    mod = mod_mine.transpose(1, 0, 2).reshape(nb, N_MOD, d)

    disc_in = (a_re[0], a_im[0], b_re[0], b_im[0], log_dt[0])
    (abr, abi, bbr, bbi), disc_vjp = jax.vjp(ssm_discretise, *disc_in)
    bd, cd, abar8, ml, mlb = ssm_tables(abr, abi, bbr, bbi, c_re[0], c_im[0], _tile(s, 256) // ROWS)

    x0 = x.reshape(t, d)
    (x1, a1, b1, f1, h1), got = ffn_forward(x0, mod, g_ffn1, gw["w1_a"], gw["w3_a"], gw["w2_a"], 0, nb,
                                            "ffn_a_forward", carried=gather_stage("mixer"))
    keep_weights("mixer", got)
    (gb, gc, vv, u, sga, sgb, h2), got = mixer_proj_forward(x1, mod, g_mix, gw["w_in"], cw_n, sw_n, nb,
                                                            carried=gather_stage("ffn_b"))
    keep_weights("ffn_b", got)
    yssm, st = ssm_forward(u, bd, cd, abar8, ml, nb)
    x2, ya, yb, mix = mixer_merge_forward(x1, gb, gc, vv, sga, sgb, yssm, u, mod, conv_full, d_skip,
                                          gw["w_conv_out"], gw["w_glu"], gw["w_ssm_out"], gw["w_out"], nb)
    (x3, a3, b3, f3, h3), _ = ffn_forward(x2, mod, g_ffn2, gw["w1_b"], gw["w3_b"], gw["w2_b"], 2, nb,
                                          "ffn_b_forward")
    loss_parts, dx3, dg_final = loss_forward_backward(x3, loss_target.reshape(t, d), g_final.reshape(1, d))
    loss_here = jnp.sum(loss_parts[:, 0, 0]).reshape(1)

    part, received = {}, {}

    def exchange_stage(ns):
        return ExchangeStage([_slots(part[n]) for n in ns])

    (da3, db3, sw3, df3, dgt3), _ = ffn_backward_hidden(dx3, a3, b3, f3, mod, gw["w2_b"], 2, nb,
                                                        "ffn_b_backward_hidden")
    (dx2, dmod3, dg_ffn2), _ = ffn_backward_input(dx3, x2, da3, db3, mod, g_ffn2, gw["w1_b"], gw["w3_b"],
                                                  2, nb, "ffn_b_backward_input")
    (part["w1_b"],), _ = nn_matmul(da3, h3, "grad_w1_b")
    (part["w3_b"],), _ = nn_matmul(db3, h3, "grad_w3_b")
    (part["w2_b"],), _ = nn_matmul(sw3, df3, "grad_w2_b")
    (dgla, dglb, dgb, dconv, ds0, dgt2, dd_skip, dw_out, dw_co, dw_so_t, dw_glu), got = mixer_merge_backward(
        dx2, mix, ya, yb, gb, gc, vv, sga, sgb, yssm, u, mod, conv_full, d_skip,
        gw["w_conv_out"], gw["w_glu"], gw["w_ssm_out"], gw["w_out"], nb, carried=exchange_stage(("w1_b", "w3_b")))
    received.update(zip(("w1_b", "w3_b"), got))
    part["w_out"] = dw_out.astype(BF16)
    part["w_conv_out"] = dw_co.astype(BF16)
    part["w_ssm_out"] = dw_so_t.astype(BF16)
    part["w_glu"] = dw_glu.reshape(sw_n // glu_fold, d).astype(BF16)
    (du, dab, dbd, dcd), got = ssm_backward(u, ds0, st, bd, cd, abar8, mlb, d_skip, nb,
                                            carried=exchange_stage(("w2_b",) + groups["mixer"][1:]))
    received.update(zip(("w2_b",) + groups["mixer"][1:], got))
    dgc, dvv, dconv_w = conv_backward(dconv, gc, vv, conv_full, nb)
    part["w_in"] = jnp.concatenate([tn_matmul(p, h2, "grad_w_in_%d" % k)
                                    for k, p in enumerate((dgb, dgc, dvv, du, dgla, dglb))], axis=0)
    (dx1, dmod2, dg_mix), got = mixer_proj_backward(dgb, dgc, dvv, du, dgla, dglb, dx2, x1, mod, g_mix, gw["w_in"], nb,
                                                    carried=exchange_stage(("w_in",)))
    received["w_in"] = got[0]
    def pack(parts):
        flat = jnp.concatenate([a.reshape(-1) for a in parts.values()])
        rows = -(-flat.shape[0] // (128 * GATHER_ROWS)) * GATHER_ROWS
        return jnp.pad(flat, (0, rows * 128 - flat.shape[0])).reshape(rows, 128)

    def unpack(flat, parts):
        out, off = {}, 0
        for key, like in parts.items():
            n = math.prod(like.shape)
            out[key], off = flat[..., off:off + n].reshape(flat.shape[:-1] + like.shape), off + n
        return out

    dabr, dabi, dbbr, dbbi, dcr, dci = ssm_table_grads(dab, dbd, dcd, g_n, p_n, h_n)
    early = dict(gmod=jnp.concatenate([dmod2, dgt2, dmod3, dgt3], axis=1), g_mix=dg_mix, g_ffn2=dg_ffn2,
                 g_final=dg_final, d_skip=dd_skip, abr=dabr, abi=dabi, bbr=dbbr, bbi=dbbi, c_re=dcr, c_im=dci,
                 conv_w=dconv_w, loss=loss_here)
    (da1, db1, sw1, df1, dgt1), _ = ffn_backward_hidden(dx1, a1, b1, f1, mod, gw["w2_a"], 0, nb,
                                                        "ffn_a_backward_hidden")
    (part["w2_a"],), (early_all,) = nn_matmul(sw1, df1, "grad_w2_a", carried=GatherStage([pack(early)]))
    (part["w1_a"],), got = nn_matmul(da1, h1, "grad_w1_a", carried=exchange_stage(("w2_a",)))
    received["w2_a"] = got[0]
    (part["w3_a"],), got = nn_matmul(db1, h1, "grad_w3_a", carried=exchange_stage(("w1_a",)))
    received["w1_a"] = got[0]
    (dx0, dmod1, dg_ffn1), got = ffn_backward_input(dx1, x0, da1, db1, mod, g_ffn1, gw["w1_a"], gw["w3_a"],
                                                    0, nb, "ffn_a_backward_input", carried=exchange_stage(("w3_a",)))
    received["w3_a"] = got[0]
    late = dict(gmod=jnp.concatenate([dmod1, dgt1], axis=1), g_ffn1=dg_ffn1)
    (late_all,) = run_stage(GatherStage([pack(late)]), "gather_ffn_a_small_grads")

    tot, per_dev = {}, {}
    for parts, gathered, name in ((early, early_all, "early"), (late, late_all, "late")):
        slots = gathered.reshape(N_DEV, -1, 128)
        total = sum_slots(slots, "sum_small_grads_" + name).reshape(-1)
        for key, val in unpack(total, parts).items():
            tot[name + key if key == "gmod" else key] = val
        per_dev[name] = unpack(slots.reshape(N_DEV, -1), parts)["gmod"]
    gmod_all = jnp.concatenate([per_dev["late"], per_dev["early"]], axis=2).reshape(N_DEV * nb, N_MOD * d)
    gmod_tot = jnp.concatenate([tot["lategmod"], tot["earlygmod"]], axis=1)
    g_a_re, g_a_im, g_b_re, g_b_im, g_log_dt = disc_vjp((tot["abr"], tot["abi"], tot["bbr"], tot["bbi"]))

    loss = tot["loss"].reshape(())
    grads = {}
    grads["b_ada"] = sum_rows(gmod_tot.reshape(nb, N_MOD * d))
    grads["g_ffn1"], grads["g_mix"], grads["g_ffn2"] = tot["g_ffn1"], tot["g_mix"], tot["g_ffn2"]
    grads["g_final"] = tot["g_final"].reshape(d)
    grads["d_skip"] = tot["d_skip"]
    grads["a_re"], grads["a_im"], grads["log_dt"] = g_a_re[None], g_a_im[None], g_log_dt[None]
    grads["b_re"], grads["b_im"] = g_b_re[None], g_b_im[None]
    grads["c_re"], grads["c_im"] = tot["c_re"][None], tot["c_im"][None]
    grads["conv_w"] = lax.dynamic_slice(tot["conv_w"], (0, me * conv_w.shape[2]), (CONV_K, conv_w.shape[2]))[None]

    delta, new_m, new_v = {}, {}, {}
    for name in big:
        wmv = (weights[name][0], mom1[name][0], mom2[name][0])
        if name in transposed:
            outs = sum_adamw_update(received[name], *[a.T for a in wmv], "adamw_" + name)
            gsum, dl, mm, vn = [a.T for a in outs]
        elif received[name].shape[1:] == weights[name].shape[1:]:
            gsum, dl, mm, vn = sum_adamw_update(received[name], *wmv, "adamw_" + name)
        else:
            gsum = sum_slots(received[name], "sum_" + name).reshape(weights[name].shape[1:])
            dl, mm, vn = adamw_update(*wmv[:1], gsum, *wmv[1:], "adamw_" + name)
        grads[name] = gsum[None]
        delta[name], new_m[name], new_v[name] = dl[None], mm[None], vn[None]

    gmod_cols = lax.dynamic_slice(gmod_all, (0, me * ada_cols), (N_DEV * nb, ada_cols))
    g_wada, d_wada, m_wada, v_wada = ada_backward_update(c_all, gmod_cols, w_ada[0], m_w_ada[0], v_w_ada[0])
    grads["w_ada"], delta["w_ada"], new_m["w_ada"], new_v["w_ada"] = g_wada[None], d_wada[None], m_wada[None], v_wada[None]

    small_names = [n for n in names if n not in big and n != "w_ada"]

    narrow = ("b_re", "b_im")

    def as2d(n, a):
        a = jnp.swapaxes(a.reshape(weights[n].shape), -1, -2) if n in narrow else a
        return a.reshape(-1, a.shape[-1])

    def from2d(n, a):
        shape = weights[n].shape
        return jnp.swapaxes(a.reshape(shape[:-2] + (shape[-1], shape[-2])), -1, -2) if n in narrow else a.reshape(shape)

    sw_, sg_, sm_, sv_ = ([as2d(n, src[n]) for n in small_names] for src in (weights, grads, mom1, mom2))
    sd, smo, svo = adamw_update_small(sw_, sg_, sm_, sv_)
    for n, dl, mm, vn in zip(small_names, sd, smo, svo):
        grads[n] = grads[n].reshape(weights[n].shape)
        delta[n], new_m[n], new_v[n] = from2d(n, dl), from2d(n, mm), from2d(n, vn)

    grad_x = dx0.reshape(nb, s, d)
    return (loss, grad_x, *[grads[n] for n in names], *[delta[n] for n in names],
            *[new_m[n] for n in names], *[new_v[n] for n in names])


def sum_rows(a):
    r, c = a.shape

    def body(a_ref, o_ref):
        acc = a_ref[0:1, :]
        for j in range(1, r):
            acc = acc + a_ref[j:j + 1, :]
        o_ref[...] = acc

    return pl.pallas_call(body, name="sum_rows", out_shape=jax.ShapeDtypeStruct((1, c), F32),
                          compiler_params=_params())(a)
```

```python
import functools
import math

import jax
import jax.numpy as jnp
from jax import lax
from jax.experimental import pallas as pl
from jax.experimental.pallas import tpu as pltpu

F32 = jnp.float32
BF16 = jnp.bfloat16
N_DEV = 8
N_MOD = 9
EPS = 1e-6
CONV_K = 3
ADAM_LR = 0.001
ADAM_B1 = 0.9
ADAM_B2 = 0.999
ADAM_EPS = 1e-08
ADAM_WD = 0.01
ADAM_STEP = 10
GELU_C0 = math.sqrt(2.0 / math.pi)
GELU_C1 = 0.044715
V7X_VMEM_LIMIT = 56 * 1024 * 1024
MESH_ID = pl.DeviceIdType.MESH
NT = (((1,), (1,)), ((), ()))
TN = (((0,), (0,)), ((), ()))


def _dot(a, b, dims=None):
    if dims is None:
        return jnp.dot(a, b, preferred_element_type=F32)
    return lax.dot_general(a, b, dims, preferred_element_type=F32)


def _params(sem=None, vmem=V7X_VMEM_LIMIT):
    return pltpu.CompilerParams(dimension_semantics=sem, vmem_limit_bytes=vmem)


def _full(shape):
    return pl.BlockSpec(shape, lambda *_: (0,) * len(shape))


def _const(shape):
    return pl.BlockSpec(shape, lambda *_: (0,) * len(shape), pipeline_mode=pl.Buffered(1))


def _tile(n, want):
    t = min(n, want)
    while n % t:
        t //= 2
    return t


class GatherStage:
    COPIES = 9

    def __init__(self, shards):
        n = len(shards)
        self.inputs = list(shards)
        self.out_shape = [jax.ShapeDtypeStruct((N_DEV * s.shape[0], s.shape[1]), s.dtype) for s in shards]
        self.scratch = [pltpu.SemaphoreType.DMA((self.COPIES * n,)), pltpu.SemaphoreType.DMA((self.COPIES * n,)),
                        pltpu.SemaphoreType.DMA((n,))]

    def _plan(self, ins, outs, sems):
        send_sems, recv_sems, local_sems = sems
        n = len(ins)
        x, y, c = lax.axis_index("x"), lax.axis_index("y"), lax.axis_index("c")
        me, sibling, xn, yn, dg = (x, y, c), (x, y, 1 - c), (1 - x, y, c), (x, 1 - y, c), (1 - x, 1 - y, c)

        def rows(k, block, half=None):
            r = ins[k].shape[0]
            px, py, pc = block
            base = (4 * px + 2 * py + pc) * r
            if half is None:
                return outs[k].at[pl.ds(base, r), :]
            return outs[k].at[pl.ds(base + half * (r // 2), r // 2), :]

        def copy(k, j, block, to, half=None, src=None):
            return pltpu.make_async_remote_copy(
                src_ref=rows(k, block, half) if src is None else src, dst_ref=rows(k, block, half),
                send_sem=send_sems.at[self.COPIES * k + j], recv_sem=recv_sems.at[self.COPIES * k + j],
                device_id=to, device_id_type=MESH_ID)

        sib = lambda b: (b[0], b[1], 1 - b[2])
        mine = [pltpu.make_async_copy(ins[k], rows(k, me), local_sems.at[k]) for k in range(n)]
        first = [(0, me, sibling, None, sibling), (1, me, xn, None, xn), (2, me, yn, None, yn)]
        second = [(3, xn, yn, 0, dg), (4, yn, xn, 1, dg), (5, xn, sibling, None, sib(xn)), (6, yn, sibling, None, sib(yn))]
        third = [(7, dg, sibling, 0, sib(dg)), (8, dg, sibling, 1, sib(dg))]
        return n, me, copy, mine, first, second, third

    def start(self, ins, outs, sems):
        n, me, copy, mine, first, _, _ = self._plan(ins, outs, sems)
        for cp in mine:
            cp.start()
        for k in range(n):
            for j, block, to, half, _ in first:
                copy(k, j, block, to, half, src=ins[k]).start()

    def advance(self, ins, outs, sems):
        n, me, copy, mine, first, second, third = self._plan(ins, outs, sems)
        for k in range(n):
            copy(k, 1, first[1][4], me).wait_recv()
            copy(k, 2, first[2][4], me).wait_recv()
            for j, block, to, half, _ in second:
                copy(k, j, block, to, half).start()

    def advance_again(self, ins, outs, sems):
        n, me, copy, mine, first, second, third = self._plan(ins, outs, sems)
        for k in range(n):
            copy(k, 3, second[0][4], me, 0).wait_recv()
            copy(k, 4, second[1][4], me, 1).wait_recv()
            for j, block, to, half, _ in third:
                copy(k, j, block, to, half).start()

    def finish(self, ins, outs, sems):
        n, me, copy, mine, first, second, third = self._plan(ins, outs, sems)
        arrived = lambda k, j, block, half: copy(k, j, block, me, half).wait_recv()
        for k in range(n):
            arrived(k, 0, first[0][4], None)
            arrived(k, 5, second[2][4], None)
            arrived(k, 6, second[3][4], None)
            arrived(k, 7, third[0][4], 0)
            arrived(k, 8, third[1][4], 1)
        for k in range(n):
            for j, block, to, half, _ in first:
                copy(k, j, block, to, half, src=ins[k]).wait_send()
            for j, block, to, half, _ in second + third:
                copy(k, j, block, to, half).wait_send()
        for cp in mine:
            cp.wait()


class ExchangeStage:
    def __init__(self, bufs):
        n = len(bufs)
        self.inputs = list(bufs)
        self.out_shape = [jax.ShapeDtypeStruct(b.shape, b.dtype) for b in bufs]
        self.scratch = [pltpu.SemaphoreType.DMA((7 * n,)), pltpu.SemaphoreType.DMA((7 * n,)),
                        pltpu.SemaphoreType.DMA((n,))]

    def _plan(self, ins, outs, sems):
        send_sems, recv_sems, local_sems = sems
        n = len(ins)
        x, y, c = lax.axis_index("x"), lax.axis_index("y"), lax.axis_index("c")
        me = 4 * x + 2 * y + c
        mine = [pltpu.make_async_copy(ins[k].at[me], outs[k].at[me], local_sems.at[k]) for k in range(n)]
        copies = []
        for mask in range(1, N_DEV):
            px, py, pc = x ^ (mask >> 2), y ^ ((mask >> 1) & 1), c ^ (mask & 1)
            for k in range(n):
                copies.append(pltpu.make_async_remote_copy(
                    src_ref=ins[k].at[4 * px + 2 * py + pc], dst_ref=outs[k].at[me],
                    send_sem=send_sems.at[7 * k + mask - 1], recv_sem=recv_sems.at[7 * k + mask - 1],
                    device_id=(px, py, pc), device_id_type=MESH_ID))
        return mine, copies

    def start(self, ins, outs, sems):
        mine, copies = self._plan(ins, outs, sems)
        for cp in mine + copies:
            cp.start()

    def advance(self, ins, outs, sems):
        pass

    advance_again = advance

    def finish(self, ins, outs, sems):
        mine, copies = self._plan(ins, outs, sems)
        for cp in copies:
            cp.wait_recv()
        for cp in copies:
            cp.wait_send()
        for cp in mine:
            cp.wait()


ANY_SPEC = pl.BlockSpec(memory_space=pl.ANY)
GATHER_ROWS = 16


class StageGroup:
    def __init__(self, stages):
        self.stages = list(stages)
        self.inputs = [a for s in stages for a in s.inputs]
        self.out_shape = [o for s in stages for o in s.out_shape]
        self.scratch = [t for s in stages for t in s.scratch]

    def _parts(self, ins, outs, sems):
        i = o = t = 0
        for s in self.stages:
            ni, no, nt = len(s.inputs), len(s.out_shape), len(s.scratch)
            yield s, ins[i:i + ni], outs[o:o + no], sems[t:t + nt]
            i, o, t = i + ni, o + no, t + nt

    def start(self, ins, outs, sems):
        for s, i_, o_, t_ in self._parts(ins, outs, sems):
            s.start(i_, o_, t_)

    def advance(self, ins, outs, sems):
        for s, i_, o_, t_ in self._parts(ins, outs, sems):
            s.advance(i_, o_, t_)

    def advance_again(self, ins, outs, sems):
        for s, i_, o_, t_ in self._parts(ins, outs, sems):
            s.advance_again(i_, o_, t_)

    def finish(self, ins, outs, sems):
        for s, i_, o_, t_ in self._parts(ins, outs, sems):
            s.finish(i_, o_, t_)


def run_stage(stage, name):
    ci, co = len(stage.inputs), len(stage.out_shape)

    def body(*refs):
        ins, outs, sems = refs[:ci], refs[ci:ci + co], refs[ci + co:]
        stage.start(ins, outs, sems)
        stage.advance(ins, outs, sems)
        stage.advance_again(ins, outs, sems)
        stage.finish(ins, outs, sems)

    return pl.pallas_call(body, name=name, out_shape=stage.out_shape, in_specs=[ANY_SPEC] * ci,
                          out_specs=[ANY_SPEC] * co, scratch_shapes=stage.scratch)(*stage.inputs)


def _call(body, *, name, grid, in_specs, out_specs, out_shape, args, scratch_shapes=(), carried=None):
    sem = ("arbitrary",) * len(grid)
    if carried is None:
        return pl.pallas_call(body, name=name, grid=grid, in_specs=list(in_specs), out_specs=list(out_specs),
                              out_shape=list(out_shape), scratch_shapes=list(scratch_shapes),
                              compiler_params=_params(sem))(*args), None
    ni, no, ns = len(in_specs), len(out_shape), len(scratch_shapes)
    ci, co = len(carried.inputs), len(carried.out_shape)
    n_steps = math.prod(grid)

    def wrapped(*refs):
        ins, refs = refs[:ni], refs[ni:]
        cins, refs = refs[:ci], refs[ci:]
        outs, refs = refs[:no], refs[no:]
        couts, refs = refs[:co], refs[co:]
        scr, csems = refs[:ns], refs[ns:]
        step = functools.reduce(lambda acc, ig: acc * ig[1] + ig[0],
                                zip([pl.program_id(k) for k in range(len(grid))], grid), 0)

        @pl.when(step == 0)
        def _():
            carried.start(cins, couts, csems)

        @pl.when(step == n_steps // 2)
        def _():
            carried.advance(cins, couts, csems)

        @pl.when(step == (3 * n_steps) // 4)
        def _():
            carried.advance_again(cins, couts, csems)

        body(*ins, *outs, *scr)

        @pl.when(step == n_steps - 1)
        def _():
            carried.finish(cins, couts, csems)

    res = pl.pallas_call(
        wrapped, name=name, grid=grid, in_specs=list(in_specs) + [ANY_SPEC] * ci,
        out_specs=list(out_specs) + [ANY_SPEC] * co, out_shape=list(out_shape) + carried.out_shape,
        scratch_shapes=list(scratch_shapes) + carried.scratch, compiler_params=_params(sem),
    )(*args, *carried.inputs)
    return res[:no], res[no:]


def _norm_mod(x, g, shift, scale):
    r = lax.rsqrt(jnp.mean(x * x, axis=-1, keepdims=True) + EPS)
    n = x * r
    return (n * g) * (1.0 + scale) + shift, n, r


def _norm_mod_bwd(dh, n, r, g, scale):
    dsh = jnp.sum(dh, axis=0, keepdims=True)
    dsc = jnp.sum(dh * (n * g), axis=0, keepdims=True)
    dg = jnp.sum(dh * (1.0 + scale) * n, axis=0, keepdims=True)
    dn = dh * ((1.0 + scale) * g)
    dx = r * (dn - n * jnp.mean(n * dn, axis=-1, keepdims=True))
    return dx, dsh, dsc, dg


def _mod_rows(mod_ref, sub):
    m = mod_ref[0]
    return m[3 * sub:3 * sub + 1], m[3 * sub + 1:3 * sub + 2], m[3 * sub + 2:3 * sub + 3]


def _gelu(x):
    t = jnp.tanh(GELU_C0 * (x + GELU_C1 * x * x * x))
    return 0.5 * x * (1.0 + t), t


def _gelu_grad(x, t):
    return 0.5 * (1.0 + t) + 0.5 * x * (1.0 - t * t) * (GELU_C0 * (1.0 + 3.0 * GELU_C1 * x * x))


def _zero_when(cond, *refs):
    @pl.when(cond)
    def _():
        for r in refs:
            r[...] = jnp.zeros_like(r)


def ada_forward(c_all, w_ada, b_ada_cols):
    def body(c_ref, w_ref, b_ref, o_ref):
        c = c_ref[...]
        cond = (c * jax.nn.sigmoid(c)).astype(BF16)
        o_ref[...] = _dot(cond, w_ref[...].astype(BF16)) + b_ref[...]

    nb, d = c_all.shape
    cols = w_ada.shape[1]
    tn = _tile(cols, 384)
    return pl.pallas_call(
        body, name="ada_forward", grid=(cols // tn,),
        out_shape=jax.ShapeDtypeStruct((nb, cols), F32),
        in_specs=[_full((nb, d)), pl.BlockSpec((d, tn), lambda j: (0, j)), pl.BlockSpec((1, tn), lambda j: (0, j))],
        out_specs=pl.BlockSpec((nb, tn), lambda j: (0, j)),
        compiler_params=_params(("arbitrary",)),
    )(c_all, w_ada, b_ada_cols)


def _adamw(w, g, m, v):
    m = ADAM_B1 * m + (1.0 - ADAM_B1) * g
    v = ADAM_B2 * v + (1.0 - ADAM_B2) * (g * g)
    m_hat = m / (1.0 - ADAM_B1 ** ADAM_STEP)
    v_hat = v / (1.0 - ADAM_B2 ** ADAM_STEP)
    delta = -ADAM_LR * (m_hat / (jnp.sqrt(v_hat) + ADAM_EPS) + ADAM_WD * w)
    return delta, m, v


def ada_backward_update(c_all, gmod_cols, w, m, v):
    def body(c_ref, g_ref, w_ref, m_ref, v_ref, go_ref, d_ref, mo_ref, vo_ref):
        c = c_ref[...]
        cond = (c * jax.nn.sigmoid(c)).astype(BF16)
        g = _dot(cond, g_ref[...].astype(BF16), TN)
        go_ref[...] = g
        d_ref[...], mo_ref[...], vo_ref[...] = _adamw(w_ref[...], g, m_ref[...], v_ref[...])

    nb, d = c_all.shape
    cols = w.shape[1]
    tn = _tile(cols, 128)
    col = pl.BlockSpec((d, tn), lambda j: (0, j))
    return pl.pallas_call(
        body, name="ada_backward_update", grid=(cols // tn,),
        out_shape=[jax.ShapeDtypeStruct(w.shape, F32)] * 4,
        in_specs=[_full((nb, d)), pl.BlockSpec((nb, tn), lambda j: (0, j)), col, col, col],
        out_specs=[col] * 4,
        compiler_params=_params(("arbitrary",)),
    )(c_all, gmod_cols, w, m, v)


def _row_spec(tm, width, tiles_per_seq):
    return pl.BlockSpec((tm, width), lambda b, i: (b * tiles_per_seq + i, 0))


def _mod_spec(d):
    return pl.BlockSpec((1, N_MOD, d), lambda b, i: (b, 0, 0))


def _col_spec(rows, tm, tiles_per_seq):
    return pl.BlockSpec((rows, tm), lambda b, i: (0, b * tiles_per_seq + i))


def _ffn_chunk(f):
    return f // 4 if f % 512 == 0 and f > 1536 else f


def ffn_forward(x, mod, g, w1t, w3t, w2, sub, nb, name, carried=None):
    t, d = x.shape
    f = w1t.shape[0]
    s = t // nb
    tm = _tile(s, 512)
    fc = _ffn_chunk(f)

    def body(x_ref, mod_ref, g_ref, w1_ref, w3_ref, w2_ref, xo_ref, a_ref, b_ref, f_ref, h_ref):
        xv = x_ref[...]
        sh, sc, gt = _mod_rows(mod_ref, sub)
        h, _, _ = _norm_mod(xv, g_ref[...], sh, sc)
        hb = h.astype(BF16)
        h_ref[...] = hb
        acc_t = jnp.zeros((d, tm), F32)
        for k in range(f // fc):
            rows = slice(k * fc, (k + 1) * fc)
            a = _dot(w1_ref[rows, :], hb, NT)
            b = _dot(w3_ref[rows, :], hb, NT)
            a_ref[rows, :] = a.astype(BF16)
            b_ref[rows, :] = b.astype(BF16)
            sw = (a * jax.nn.sigmoid(a)) * b
            acc_t = acc_t + _dot(w2_ref[rows, :], sw.astype(BF16), TN)
        acc = acc_t.T
        f_ref[...] = acc.astype(BF16)
        xo_ref[...] = xv + (0.5 * gt) * acc

    tps = s // tm
    rd, cf = _row_spec(tm, d, tps), _col_spec(f, tm, tps)
    return _call(
        body, name=name, grid=(nb, tps), carried=carried,
        out_shape=[jax.ShapeDtypeStruct((t, d), F32)] + [jax.ShapeDtypeStruct((f, t), BF16)] * 2
        + [jax.ShapeDtypeStruct((t, d), BF16)] * 2,
        in_specs=[rd, _mod_spec(d), _const((1, d)), _const((f, d)), _const((f, d)), _const((f, d))],
        out_specs=[rd, cf, cf, rd, rd],
        args=(x, mod, g, w1t, w3t, w2))


def ffn_backward_hidden(dxo, a_t, b_t, fo, mod, w2, sub, nb, name, carried=None):
    t, d = dxo.shape
    f = w2.shape[0]
    s = t // nb
    tm = _tile(s, 512)
    fc = _tile(f, 704) if f % 704 == 0 else _tile(f, 512)

    def body(dxo_ref, a_ref, b_ref, f_ref, mod_ref, w2_ref, da_ref, db_ref, s_ref, df_ref, dgt_ref):
        _zero_when(pl.program_id(1) == 0, dgt_ref)
        dxo = dxo_ref[...]
        _, _, gt = _mod_rows(mod_ref, sub)
        dfb = ((0.5 * gt) * dxo).astype(BF16)
        df_ref[...] = dfb
        dgt_ref[...] += 0.5 * jnp.sum(dxo * f_ref[...].astype(F32), axis=0, keepdims=True)[None]
        for k in range(f // fc):
            rows = slice(k * fc, (k + 1) * fc)
            ds = _dot(w2_ref[rows, :], dfb, NT).astype(BF16)
            av = a_ref[rows, :].astype(F32)
            bv = b_ref[rows, :]
            sig = jax.nn.sigmoid(av)
            sl = av * sig
            slb = sl.astype(BF16)
            da_ref[rows, :] = ds * bv * (sig + sl * (1.0 - sig)).astype(BF16)
            db_ref[rows, :] = ds * slb
            s_ref[rows, :] = slb * bv

    tps = s // tm
    rd, cf = _row_spec(tm, d, tps), _col_spec(f, tm, tps)
    return _call(
        body, name=name, grid=(nb, tps), carried=carried,
        out_shape=[jax.ShapeDtypeStruct((f, t), BF16)] * 3
        + [jax.ShapeDtypeStruct((t, d), BF16), jax.ShapeDtypeStruct((nb, 1, d), F32)],
        in_specs=[rd, cf, cf, rd, _mod_spec(d), _const((f, d))],
        out_specs=[cf, cf, cf, rd, pl.BlockSpec((1, 1, d), lambda b, i: (b, 0, 0))],
        args=(dxo, a_t, b_t, fo, mod, w2))


def ffn_backward_input(dxo, x, da_t, db_t, mod, g, w1t, w3t, sub, nb, name, carried=None):
    t, d = x.shape
    f = w1t.shape[0]
    s = t // nb
    tm = _tile(s, 512)

    def body(dxo_ref, x_ref, da_ref, db_ref, mod_ref, g_ref, w1_ref, w3_ref, dx_ref, dmod_ref, dg_ref):
        bi, i = pl.program_id(0), pl.program_id(1)
        _zero_when(i == 0, dmod_ref)
        _zero_when(jnp.logical_and(bi == 0, i == 0), dg_ref)
        sh, sc, _ = _mod_rows(mod_ref, sub)
        gv = g_ref[...]
        _, n, r = _norm_mod(x_ref[...], gv, sh, sc)
        dh_t = _dot(w1_ref[...], da_ref[...], TN) + _dot(w3_ref[...], db_ref[...], TN)
        dxn, dsh, dsc, dg = _norm_mod_bwd(dh_t.T, n, r, gv, sc)
        dx_ref[...] = dxo_ref[...] + dxn
        dmod_ref[...] += jnp.concatenate([dsh, dsc], axis=0)[None]
        dg_ref[...] += dg

    tps = s // tm
    rd, cf = _row_spec(tm, d, tps), _col_spec(f, tm, tps)
    return _call(
        body, name=name, grid=(nb, tps), carried=carried,
        out_shape=[jax.ShapeDtypeStruct((t, d), F32), jax.ShapeDtypeStruct((nb, 2, d), F32),
                   jax.ShapeDtypeStruct((1, d), F32)],
        in_specs=[rd, rd, cf, cf, _mod_spec(d), _const((1, d)), _const((f, d)), _const((f, d))],
        out_specs=[rd, pl.BlockSpec((1, 2, d), lambda b, i: (b, 0, 0)), _full((1, d))],
        args=(dxo, x, da_t, db_t, mod, g, w1t, w3t))


def nn_matmul(lhs_t, rhs, name, carried=None):
    m, t = lhs_t.shape
    n = rhs.shape[1]
    tk = _tile(t, 4096)
    tmm = m if m <= 1024 else m // 4
    nk = t // tk

    def body(a_ref, b_ref, o_ref, acc_ref):
        k = pl.program_id(1)
        _zero_when(k == 0, acc_ref)
        acc_ref[...] += _dot(a_ref[...], b_ref[...])

        @pl.when(k == nk - 1)
        def _():
            o_ref[...] = acc_ref[...].astype(BF16)

    return _call(
        body, name=name, grid=(m // tmm, nk), carried=carried,
        out_shape=[jax.ShapeDtypeStruct((m, n), BF16)],
        in_specs=[pl.BlockSpec((tmm, tk), lambda j, k: (j, k)), pl.BlockSpec((tk, n), lambda j, k: (k, 0))],
        out_specs=[pl.BlockSpec((tmm, n), lambda j, k: (j, 0))],
        scratch_shapes=[pltpu.VMEM((tmm, n), F32)],
        args=(lhs_t, rhs))


def tn_matmul(lhs, rhs, name):
    t, m = lhs.shape
    n = rhs.shape[1]
    tk = _tile(t, 4096)
    nk = t // tk

    def body(a_ref, b_ref, o_ref, acc_ref):
        k = pl.program_id(0)
        _zero_when(k == 0, acc_ref)
        acc_ref[...] += _dot(a_ref[...], b_ref[...], TN)

        @pl.when(k == nk - 1)
        def _():
            o_ref[...] = acc_ref[...].astype(BF16)

    return pl.pallas_call(
        body, name=name, grid=(nk,),
        out_shape=jax.ShapeDtypeStruct((m, n), BF16),
        in_specs=[pl.BlockSpec((tk, m), lambda k: (k, 0)), pl.BlockSpec((tk, n), lambda k: (k, 0))],
        out_specs=pl.BlockSpec((m, n), lambda k: (0, 0)),
        scratch_shapes=[pltpu.VMEM((m, n), F32)],
        compiler_params=_params(("arbitrary",)),
    )(lhs, rhs)


def mixer_proj_forward(x, mod, g, w_in_t, cw, sw, nb, carried=None):
    t, d = x.shape
    s = t // nb
    tm = _tile(s, 512)
    pieces = [(0, cw, "bf16"), (cw, cw, "bf16"), (2 * cw, cw, "bf16"), (3 * cw, sw, "f32"),
              (3 * cw + sw, d, "sig"), (3 * cw + sw + d, d, "sig")]

    def body(x_ref, mod_ref, g_ref, w_ref, *outs):
        h_ref = outs[-1]
        sh, sc, _ = _mod_rows(mod_ref, 1)
        h, _, _ = _norm_mod(x_ref[...], g_ref[...], sh, sc)
        hb = h.astype(BF16)
        h_ref[...] = hb
        for (off, width, kind), o_ref in zip(pieces, outs[:-1]):
            ck = _tile(width, 512)
            for j in range(width // ck):
                p = _dot(hb, w_ref[off + j * ck:off + (j + 1) * ck, :], NT)
                if kind == "sig":
                    p = jax.nn.sigmoid(p)
                o_ref[:, j * ck:(j + 1) * ck] = p.astype(o_ref.dtype)

    tps = s // tm
    widths = [(cw, BF16), (cw, BF16), (cw, BF16), (sw, F32), (d, BF16), (d, BF16), (d, BF16)]
    return _call(
        body, name="mixer_proj_forward", grid=(nb, tps), carried=carried,
        out_shape=[jax.ShapeDtypeStruct((t, w), dt) for w, dt in widths],
        in_specs=[_row_spec(tm, d, tps), _mod_spec(d), _const((1, d)), _const(w_in_t.shape)],
        out_specs=[_row_spec(tm, w, tps) for w, _ in widths],
        args=(x, mod, g, w_in_t))


def mixer_proj_backward(dgb, dgc, dv, du, dgla, dglb, dxo, x, mod, g, w_in_t, nb, carried=None):
    t, d = x.shape
    s = t // nb
    tm = _tile(s, 512)
    parts = [dgb, dgc, dv, du, dgla, dglb]
    offs = [0]
    for p in parts:
        offs.append(offs[-1] + p.shape[1])

    def body(*refs):
        p_refs = refs[:6]
        dxo_ref, x_ref, mod_ref, g_ref, w_ref, dx_ref, dmod_ref, dg_ref = refs[6:]
        bi, i = pl.program_id(0), pl.program_id(1)
        _zero_when(i == 0, dmod_ref)
        _zero_when(jnp.logical_and(bi == 0, i == 0), dg_ref)
        dh = jnp.zeros((tm, d), F32)
        for p_ref, off in zip(p_refs, offs):
            dh = dh + _dot(p_ref[...], w_ref[off:off + p_ref.shape[1], :])
        sh, sc, _ = _mod_rows(mod_ref, 1)
        gv = g_ref[...]
        _, n, r = _norm_mod(x_ref[...], gv, sh, sc)
        dxn, dsh, dsc, dg = _norm_mod_bwd(dh, n, r, gv, sc)
        dx_ref[...] = dxo_ref[...] + dxn
        dmod_ref[...] += jnp.concatenate([dsh, dsc], axis=0)[None]
        dg_ref[...] += dg

    tps = s // tm
    rd = _row_spec(tm, d, tps)
    return _call(
        body, name="mixer_proj_backward", grid=(nb, tps), carried=carried,
        out_shape=[jax.ShapeDtypeStruct((t, d), F32), jax.ShapeDtypeStruct((nb, 2, d), F32),
                   jax.ShapeDtypeStruct((1, d), F32)],
        in_specs=[_row_spec(tm, p.shape[1], tps) for p in parts]
        + [rd, rd, _mod_spec(d), _const((1, d)), _const(w_in_t.shape)],
        out_specs=[rd, pl.BlockSpec((1, 2, d), lambda b, i: (b, 0, 0)), _full((1, d))],
        args=(*parts, dxo, x, mod, g, w_in_t))


GROUPS_PER_BLOCK = 8
ROWS = 8
SCAN_LANES = 512
SCAN_UNROLL = 8


def _scan_rows(xr, xi, masks, shifts):
    for (mr, mi), sft in zip(masks, shifts):
        sr, si = pltpu.roll(xr, sft, 0), pltpu.roll(xi, sft, 0)
        xr, xi = xr + mr * sr - mi * si, xi + mr * si + mi * sr
    return xr, xi


def _cmul_add(ar, ai, cr, ci, br, bi):
    return ar * cr - ai * ci + br, ar * ci + ai * cr + bi


def _segment_rows(perm_ref, x):
    return _dot(perm_ref[0], x).astype(BF16)


def _time_rows(perm_ref, x):
    hi = x.astype(BF16)
    lo = (x - hi.astype(F32)).astype(BF16)
    return _dot(perm_ref[1], hi) + _dot(perm_ref[1], lo)


def segment_permutation(tc):
    r = jnp.arange(tc)
    p = (r[:, None] % ROWS * (tc // ROWS) + r[:, None] // ROWS == r[None, :]).astype(BF16)
    return jnp.stack([p, p.T])


def _scan_loop(n, step, init):
    def trip(j, carry):
        for r in range(SCAN_UNROLL):
            carry = step(j * SCAN_UNROLL + r, carry)
        return carry

    return lax.fori_loop(0, n // SCAN_UNROLL, trip, init)


def _rows_at(k, offset=0):
    return pl.ds(pl.multiple_of(k * ROWS + offset, ROWS), ROWS)


def ssm_forward(u, bd, cd, a1, ml, nb):
    t, w = u.shape
    s = t // nb
    tc = _tile(s, 256)
    seg = tc // ROWS
    nq, ub, lq = bd.shape[1], bd.shape[2], bd.shape[3]
    nl = nq * lq
    nch = s // tc
    lw = min(nl, SCAN_LANES)

    def body(u_ref, perm_ref, bd_ref, cd_ref, a1_ref, ml_ref, y_ref, st_ref, xr_s, xi_s, car_s):
        i = pl.program_id(1)

        @pl.when(i == 0)
        def _():
            car_s[...] = jnp.zeros_like(car_s)

        ub16 = _segment_rows(perm_ref, u_ref[...].astype(BF16))
        for q in range(nq):
            lanes = slice(q * lq, (q + 1) * lq)
            uq = ub16[:, q * ub:(q + 1) * ub]
            xr_s[:, lanes] = _dot(uq, bd_ref[0, q])
            xi_s[:, lanes] = _dot(uq, bd_ref[1, q])
        row_is_0 = lax.broadcasted_iota(jnp.int32, (ROWS, lw), 0) == 0
        zero = jnp.zeros((ROWS, lw), F32)
        for j in range(nl // lw):
            lanes = slice(j * lw, (j + 1) * lw)
            ar, ai = a1_ref[0, :, lanes], a1_ref[1, :, lanes]

            def local(k, c):
                return _cmul_add(ar, ai, c[0], c[1], xr_s[_rows_at(k), lanes], xi_s[_rows_at(k), lanes])

            er, ei = _scan_loop(seg, local, (zero, zero))
            masks = [(ml_ref[d, 0, :, lanes], ml_ref[d, 1, :, lanes]) for d in range(3)]
            cr, ci = _scan_rows(jnp.where(row_is_0, car_s[0, :, lanes], pltpu.roll(er, 1, 0)),
                                jnp.where(row_is_0, car_s[1, :, lanes], pltpu.roll(ei, 1, 0)), masks, (1, 2, 4))
            st_ref[0, 0, :, lanes] = cr
            st_ref[0, 1, :, lanes] = ci

            def full(k, c):
                xr, xi = _cmul_add(ar, ai, c[0], c[1], xr_s[_rows_at(k), lanes], xi_s[_rows_at(k), lanes])
                xr_s[_rows_at(k), lanes] = xr
                xi_s[_rows_at(k), lanes] = xi
                return xr, xi

            fr, fi = _scan_loop(seg, full, (cr, ci))
            car_s[0, :, lanes] = jnp.broadcast_to(fr[ROWS - 1:ROWS], fr.shape)
            car_s[1, :, lanes] = jnp.broadcast_to(fi[ROWS - 1:ROWS], fi.shape)
        y = jnp.concatenate([_dot(xr_s[:, q * lq:(q + 1) * lq].astype(BF16), cd_ref[0, q])
                             + _dot(xi_s[:, q * lq:(q + 1) * lq].astype(BF16), cd_ref[1, q]) for q in range(nq)],
                            axis=1)
        y_ref[...] = _time_rows(perm_ref, y)

    perm = segment_permutation(tc)
    return pl.pallas_call(
        body, name="ssm_forward", grid=(nb, nch),
        out_shape=[jax.ShapeDtypeStruct((t, w), F32), jax.ShapeDtypeStruct((nb * nch, 2, ROWS, nl), F32)],
        in_specs=[pl.BlockSpec((tc, w), lambda b, i: (b * nch + i, 0)), _const(perm.shape), _const(bd.shape),
                  _const(cd.shape), _const(a1.shape), _const(ml.shape)],
        out_specs=[pl.BlockSpec((tc, w), lambda b, i: (b * nch + i, 0)),
                   pl.BlockSpec((1, 2, ROWS, nl), lambda b, i: (b * nch + i, 0, 0, 0))],
        scratch_shapes=[pltpu.VMEM((tc, nl), F32), pltpu.VMEM((tc, nl), F32), pltpu.VMEM((2, ROWS, nl), F32)],
        compiler_params=_params(("arbitrary", "arbitrary")),
    )(u, perm, bd, cd, a1, ml)


def ssm_backward(u, dy, st, bd, cd, a1, mlb, dskip, nb, carried=None):
    t, w = u.shape
    s = t // nb
    tc = _tile(s, 256)
    seg = tc // ROWS
    nq, ub, lq = bd.shape[1], bd.shape[2], bd.shape[3]
    nl = nq * lq
    nch = s // tc
    lw = min(nl, SCAN_LANES)

    def body(u_ref, dy_ref, st_ref, perm_ref, bd_ref, cd_ref, a1_ref, mlb_ref, dsk_ref,
             du_ref, dab_ref, dbd_ref, dcd_ref, xr_s, xi_s, lr_s, li_s, car_s):
        bi, i = pl.program_id(0), pl.program_id(1)
        first = jnp.logical_and(bi == 0, i == 0)

        @pl.when(i == 0)
        def _():
            car_s[...] = jnp.zeros_like(car_s)

        @pl.when(first)
        def _():
            dab_ref[...] = jnp.zeros_like(dab_ref)
            dbd_ref[...] = jnp.zeros_like(dbd_ref)
            dcd_ref[...] = jnp.zeros_like(dcd_ref)

        ub16 = _segment_rows(perm_ref, u_ref[...].astype(BF16))
        dyb16 = _segment_rows(perm_ref, dy_ref[...].astype(BF16))
        xr_s[0:ROWS, :] = st_ref[0, 0]
        xi_s[0:ROWS, :] = st_ref[0, 1]
        for q in range(nq):
            lanes = slice(q * lq, (q + 1) * lq)
            uq = ub16[:, q * ub:(q + 1) * ub]
            dq = dyb16[:, q * ub:(q + 1) * ub]
            xr_s[ROWS:, lanes] = _dot(uq, bd_ref[0, q])
            xi_s[ROWS:, lanes] = _dot(uq, bd_ref[1, q])
            lr_s[:, lanes] = _dot(dq, cd_ref[0, q], NT)
            li_s[:, lanes] = _dot(dq, cd_ref[1, q], NT)
        row_is_7 = lax.broadcasted_iota(jnp.int32, (ROWS, lw), 0) == ROWS - 1
        zero = jnp.zeros((ROWS, lw), F32)
        for j in range(nl // lw):
            lanes = slice(j * lw, (j + 1) * lw)
            ar, ai = a1_ref[0, :, lanes], a1_ref[1, :, lanes]
            nai = -ai

            def states(k, c):
                xr, xi = _cmul_add(ar, ai, c[0], c[1], xr_s[_rows_at(k, ROWS), lanes], xi_s[_rows_at(k, ROWS), lanes])
                xr_s[_rows_at(k, ROWS), lanes] = xr
                xi_s[_rows_at(k, ROWS), lanes] = xi
                return xr, xi

            _scan_loop(seg, states, (st_ref[0, 0, :, lanes], st_ref[0, 1, :, lanes]))

            def local(kk, c):
                k = seg - 1 - kk
                return _cmul_add(ar, nai, c[0], c[1], lr_s[_rows_at(k), lanes], li_s[_rows_at(k), lanes])

            er, ei = _scan_loop(seg, local, (zero, zero))
            masks = [(mlb_ref[d, 0, :, lanes], mlb_ref[d, 1, :, lanes]) for d in range(3)]
            cr, ci = _scan_rows(jnp.where(row_is_7, car_s[0, :, lanes], pltpu.roll(er, ROWS - 1, 0)),
                                jnp.where(row_is_7, car_s[1, :, lanes], pltpu.roll(ei, ROWS - 1, 0)), masks, (7, 6, 4))

            def full(kk, c):
                cr_, ci_, accr, acci = c
                k = seg - 1 - kk
                lr, li = _cmul_add(ar, nai, cr_, ci_, lr_s[_rows_at(k), lanes], li_s[_rows_at(k), lanes])
                lr_s[_rows_at(k), lanes] = lr
                li_s[_rows_at(k), lanes] = li
                xpr, xpi = xr_s[_rows_at(k), lanes], xi_s[_rows_at(k), lanes]
                return lr, li, accr + lr * xpr + li * xpi, acci + li * xpr - lr * xpi

            lr0, li0, accr, acci = _scan_loop(seg, full, (cr, ci, zero, zero))
            car_s[0, :, lanes] = jnp.broadcast_to(lr0[0:1], lr0.shape)
            car_s[1, :, lanes] = jnp.broadcast_to(li0[0:1], li0.shape)
            dab_ref[0, :, lanes] += accr
            dab_ref[1, :, lanes] += acci
        du_parts = []
        for q in range(nq):
            lanes = slice(q * lq, (q + 1) * lq)
            cols = slice(q * ub, (q + 1) * ub)
            lrb, lib = lr_s[:, lanes].astype(BF16), li_s[:, lanes].astype(BF16)
            uq, dq = ub16[:, cols], dyb16[:, cols]
            du_parts.append(_dot(lrb, bd_ref[0, q], NT) + _dot(lib, bd_ref[1, q], NT))
            dbd_ref[0, q] += _dot(uq, lrb, TN)
            dbd_ref[1, q] += _dot(uq, lib, TN)
            dcd_ref[0, q] += _dot(xr_s[ROWS:, lanes].astype(BF16), dq, TN)
            dcd_ref[1, q] += _dot(xi_s[ROWS:, lanes].astype(BF16), dq, TN)
        du = _time_rows(perm_ref, jnp.concatenate(du_parts, axis=1)) + dsk_ref[...] * dy_ref[...]
        du_ref[...] = du.astype(BF16)

    rev = lambda b, i: (b * nch + nch - 1 - i, 0)
    perm = segment_permutation(tc)
    return _call(
        body, name="ssm_backward", grid=(nb, nch), carried=carried,
        out_shape=[jax.ShapeDtypeStruct((t, w), BF16), jax.ShapeDtypeStruct((2, ROWS, nl), F32),
                   jax.ShapeDtypeStruct(bd.shape, F32), jax.ShapeDtypeStruct(cd.shape, F32)],
        in_specs=[pl.BlockSpec((tc, w), rev), pl.BlockSpec((tc, w), rev),
                  pl.BlockSpec((1, 2, ROWS, nl), lambda b, i: (b * nch + nch - 1 - i, 0, 0, 0)),
                  _const(perm.shape), _const(bd.shape), _const(cd.shape), _const(a1.shape), _const(mlb.shape),
                  _const((1, w))],
        out_specs=[pl.BlockSpec((tc, w), rev), _full((2, ROWS, nl)), _full(bd.shape), _full(cd.shape)],
        scratch_shapes=[pltpu.VMEM((tc + ROWS, nl), F32), pltpu.VMEM((tc + ROWS, nl), F32),
                        pltpu.VMEM((tc, nl), F32), pltpu.VMEM((tc, nl), F32), pltpu.VMEM((2, ROWS, nl), F32)],
        args=(u, dy, st, perm, bd, cd, a1, mlb, dskip))


def ssm_discretise(a_re, a_im, b_re, b_im, log_dt):
    dt = jnp.exp(log_dt)[:, None]
    er = jnp.exp(a_re * dt)
    abr, abi = er * jnp.cos(a_im * dt), er * jnp.sin(a_im * dt)
    den = a_re * a_re + a_im * a_im
    nr, ni = abr - 1.0, abi
    fr = ((nr * a_re + ni * a_im) / den)[..., None]
    fi = ((ni * a_re - nr * a_im) / den)[..., None]
    return abr, abi, fr * b_re - fi * b_im, fr * b_im + fi * b_re


def _complex_square(zr, zi):
    return zr * zr - zi * zi, 2.0 * zr * zi


def ssm_tables(abr, abi, bbr, bbi, c_re, c_im, seg):
    g, p, h = bbr.shape
    nq = g // GROUPS_PER_BLOCK
    zr, zi = abr.reshape(1, -1), abi.reshape(1, -1)
    a1 = jnp.stack([jnp.broadcast_to(zr, (ROWS, g * p)), jnp.broadcast_to(zi, (ROWS, g * p))])
    for _ in range(seg.bit_length() - 1):
        zr, zi = _complex_square(zr, zi)
    row = jnp.arange(ROWS)[:, None]
    ml, mlb = [], []
    for d in (1, 2, 4):
        ml.append(jnp.stack([jnp.where(row >= d, zr, 0.0), jnp.where(row >= d, zi, 0.0)]))
        mlb.append(jnp.stack([jnp.where(row + d < ROWS, zr, 0.0), jnp.where(row + d < ROWS, -zi, 0.0)]))
        zr, zi = _complex_square(zr, zi)
    eye = jnp.eye(GROUPS_PER_BLOCK, dtype=F32)

    def block_diag_in(bb):
        bq = bb.reshape(nq, GROUPS_PER_BLOCK, p, h)
        return jnp.einsum("qaph,ab->qahbp", bq, eye).reshape(nq, GROUPS_PER_BLOCK * h, GROUPS_PER_BLOCK * p)

    def block_diag_out(cc):
        cq = cc.reshape(nq, GROUPS_PER_BLOCK, h, p)
        return jnp.einsum("qahp,ab->qapbh", cq, eye).reshape(nq, GROUPS_PER_BLOCK * p, GROUPS_PER_BLOCK * h)

    bd = jnp.stack([block_diag_in(bbr), block_diag_in(bbi)]).astype(BF16)
    cd = jnp.stack([block_diag_out(c_re), block_diag_out(-c_im)]).astype(BF16)
    return bd, cd, a1, jnp.stack(ml), jnp.stack(mlb)


def ssm_table_grads(dab, dbd, dcd, g, p, h):
    nq = g // GROUPS_PER_BLOCK
    dabr, dabi = dab[0].sum(0).reshape(g, p), dab[1].sum(0).reshape(g, p)
    b5 = dbd.reshape(2, nq, GROUPS_PER_BLOCK, h, GROUPS_PER_BLOCK, p)
    dbb = jnp.einsum("rqahap->rqaph", b5).reshape(2, g, p, h)
    c5 = dcd.reshape(2, nq, GROUPS_PER_BLOCK, p, GROUPS_PER_BLOCK, h)
    dcc = jnp.einsum("rqapah->rqahp", c5).reshape(2, g, h, p)
    return dabr, dabi, dbb[0], dbb[1], dcc[0], -dcc[1]


HALO = 16


def _conv_inputs(gc_ref, v_ref, gch_ref, vh_ref, cv_s, i, tm):
    cv = gc_ref[...].astype(F32) * v_ref[...].astype(F32)
    halo = gch_ref[...].astype(F32) * vh_ref[...].astype(F32)
    cv_s[0:HALO, :] = jnp.where(i == 0, 0.0, halo)
    cv_s[HALO:, :] = cv
    return cv, cv_s[HALO - 1:HALO - 1 + tm, :], cv_s[HALO - 2:HALO - 2 + tm, :]


def _halo_spec(tm, width, tiles_per_seq):
    per = tm // HALO
    return pl.BlockSpec((HALO, width), lambda b, i: (jnp.maximum((b * tiles_per_seq + i) * per - 1, 0), 0))


def mixer_merge_forward(x, gb, gc, v, sga, sgb, yssm, u, mod, conv_w, dskip, wco, wglu, wso_t, wout, nb):
    t, d = x.shape
    cw, sw = gb.shape[1], u.shape[1]
    s = t // nb
    tm = _tile(s, 512)
    tps = s // tm

    def body(x_ref, gb_ref, gc_ref, v_ref, gch_ref, vh_ref, sga_ref, sgb_ref, ys_ref, u_ref, mod_ref, cw_ref,
             dsk_ref, wco_ref, wglu_ref, wso_ref, wout_ref, xo_ref, ya_ref, yb_ref, mix_ref, cv_s):
        i = pl.program_id(1)
        cv, cv1, cv2 = _conv_inputs(gc_ref, v_ref, gch_ref, vh_ref, cv_s, i, tm)
        w = cw_ref[...]
        conv = w[0:1] * cv2 + w[1:2] * cv1 + w[2:3] * cv
        ya = _dot((gb_ref[...].astype(F32) * conv).astype(BF16), wco_ref[...])
        s0 = ys_ref[...] + dsk_ref[...] * u_ref[...]
        s1, _ = _gelu(s0)
        z = _dot(s1.astype(BF16), wglu_ref[...])
        s2 = s1 * jax.nn.sigmoid(z)
        yb = _dot(s2.astype(BF16), wso_ref[...], NT)
        merged = sga_ref[...].astype(F32) * ya + sgb_ref[...].astype(F32) * yb
        mix = _dot(merged.astype(BF16), wout_ref[...])
        _, _, gt = _mod_rows(mod_ref, 1)
        xo_ref[...] = x_ref[...] + gt * mix
        ya_ref[...] = ya.astype(BF16)
        yb_ref[...] = yb.astype(BF16)
        mix_ref[...] = mix.astype(BF16)

    rd, rc, rw = _row_spec(tm, d, tps), _row_spec(tm, cw, tps), _row_spec(tm, sw, tps)
    hc = _halo_spec(tm, cw, tps)
    return pl.pallas_call(
        body, name="mixer_merge_forward", grid=(nb, tps),
        out_shape=[jax.ShapeDtypeStruct((t, d), F32)] + [jax.ShapeDtypeStruct((t, d), BF16)] * 3,
        in_specs=[rd, rc, rc, rc, hc, hc, rd, rd, rw, rw, _mod_spec(d), _const(conv_w.shape), _const((1, sw)),
                  _const(wco.shape), _const(wglu.shape), _const(wso_t.shape), _const(wout.shape)],
        out_specs=[rd, rd, rd, rd],
        scratch_shapes=[pltpu.VMEM((tm + HALO, cw), F32)],
        compiler_params=_params(("arbitrary", "arbitrary")),
    )(x, gb, gc, v, gc, v, sga, sgb, yssm, u, mod, conv_w, dskip, wco, wglu, wso_t, wout)


def mixer_merge_backward(dxo, mix, ya, yb, gb, gc, v, sga, sgb, yssm, u, mod, conv_w, dskip,
                         wco, wglu, wso_t, wout, nb, carried=None):
    t, d = dxo.shape
    cw, sw = gb.shape[1], u.shape[1]
    s = t // nb
    tm = _tile(s, 256)
    tps = s // tm

    def body(dxo_ref, mix_ref, ya_ref, yb_ref, gb_ref, gc_ref, v_ref, gch_ref, vh_ref, sga_ref, sgb_ref, ys_ref,
             u_ref, mod_ref, cw_ref, dsk_ref, wco_ref, wglu_ref, wso_ref, wout_ref,
             dgla_ref, dglb_ref, dgb_ref, dconv_ref, ds0_ref, dgt_ref, ddsk_ref, dwout_ref, dwco_ref, dwso_ref,
             dwglu_ref, cv_s):
        bi, i = pl.program_id(0), pl.program_id(1)
        _zero_when(i == 0, dgt_ref)
        _zero_when(jnp.logical_and(bi == 0, i == 0), ddsk_ref, dwout_ref, dwco_ref, dwso_ref, dwglu_ref)
        dxo = dxo_ref[...]
        _, _, gt = _mod_rows(mod_ref, 1)
        dmix = (gt * dxo).astype(BF16)
        dmerged = _dot(dmix, wout_ref[...], NT)
        uv = u_ref[...]
        s0 = ys_ref[...] + dsk_ref[...] * uv
        s1, th = _gelu(s0)
        s1b = s1.astype(BF16)
        sz = jax.nn.sigmoid(_dot(s1b, wglu_ref[...]))
        s2b = (s1 * sz).astype(BF16)
        cv, cv1, cv2 = _conv_inputs(gc_ref, v_ref, gch_ref, vh_ref, cv_s, i, tm)
        w = cw_ref[...]
        conv = w[0:1] * cv2 + w[1:2] * cv1 + w[2:3] * cv
        gbv = gb_ref[...].astype(F32)
        ya, yb = ya_ref[...].astype(F32), yb_ref[...].astype(F32)
        sga, sgb = sga_ref[...].astype(F32), sgb_ref[...].astype(F32)
        merged = (sga * ya + sgb * yb).astype(BF16)
        dwout_ref[...] += _dot(merged, dmix, TN)
        dgt_ref[...] += jnp.sum(dxo * mix_ref[...].astype(F32), axis=0, keepdims=True)[None]
        dgla_ref[...] = (dmerged * ya * sga * (1.0 - sga)).astype(BF16)
        dglb_ref[...] = (dmerged * yb * sgb * (1.0 - sgb)).astype(BF16)
        dya = (dmerged * sga).astype(BF16)
        dyb = (dmerged * sgb).astype(BF16)
        dwco_ref[...] += _dot((gbv * conv).astype(BF16), dya, TN)
        dya_in = _dot(dya, wco_ref[...], NT)
        dgb_ref[...] = (dya_in * conv).astype(BF16)
        dconv_ref[...] = dya_in * gbv
        dwso_ref[...] += _dot(dyb, s2b, TN)
        ds2 = _dot(dyb, wso_ref[...])
        dz = (ds2 * s1 * sz * (1.0 - sz)).astype(BF16)
        dwglu_ref[...] += _dot(s1b, dz, TN)
        ds1 = ds2 * sz + _dot(dz, wglu_ref[...], NT)
        ds0 = ds1 * _gelu_grad(s0, th)
        ds0_ref[...] = ds0
        ddsk_ref[...] += jnp.sum(ds0 * uv, axis=0, keepdims=True)

    rd, rc, rw = _row_spec(tm, d, tps), _row_spec(tm, cw, tps), _row_spec(tm, sw, tps)
    hc = _halo_spec(tm, cw, tps)
    return _call(
        body, name="mixer_merge_backward", grid=(nb, tps), carried=carried,
        out_shape=[jax.ShapeDtypeStruct((t, d), BF16), jax.ShapeDtypeStruct((t, d), BF16),
                   jax.ShapeDtypeStruct((t, cw), BF16), jax.ShapeDtypeStruct((t, cw), F32),
                   jax.ShapeDtypeStruct((t, sw), F32), jax.ShapeDtypeStruct((nb, 1, d), F32),
                   jax.ShapeDtypeStruct((1, sw), F32), jax.ShapeDtypeStruct(wout.shape, F32),
                   jax.ShapeDtypeStruct(wco.shape, F32), jax.ShapeDtypeStruct(wso_t.shape, F32),
                   jax.ShapeDtypeStruct(wglu.shape, F32)],
        in_specs=[rd, rd, rd, rd, rc, rc, rc, hc, hc, rd, rd, rw, rw, _mod_spec(d), _const(conv_w.shape),
                  _const((1, sw)), _const(wco.shape), _const(wglu.shape), _const(wso_t.shape), _const(wout.shape)],
        out_specs=[rd, rd, rc, rc, rw, pl.BlockSpec((1, 1, d), lambda b, i: (b, 0, 0)), _full((1, sw)),
                   _full(wout.shape), _full(wco.shape), _full(wso_t.shape), _full(wglu.shape)],
        scratch_shapes=[pltpu.VMEM((tm + HALO, cw), F32)],
        args=(dxo, mix, ya, yb, gb, gc, v, gc, v, sga, sgb, yssm, u, mod, conv_w, dskip, wco, wglu, wso_t, wout))


def conv_backward(dconv, gc, v, conv_w, nb):
    t, cw = dconv.shape
    s = t // nb
    tm = _tile(s, 512)
    tps = s // tm
    per = tm // ROWS
    slab = _tile(tm, 32)

    def body(dc_ref, dcn_ref, gc_ref, v_ref, gch_ref, vh_ref, cw_ref, dgc_ref, dv_ref, dw_ref, cv_s, dc_s):
        bi, i = pl.program_id(0), pl.program_id(1)
        _zero_when(jnp.logical_and(bi == 0, i == 0), dw_ref)
        cv_s[0:HALO, :] = jnp.where(i == 0, 0.0, gch_ref[...].astype(F32) * vh_ref[...].astype(F32))
        dc_s[tm:, :] = jnp.where(i == tps - 1, 0.0, dcn_ref[...])
        for r in range(0, tm, slab):
            cv_s[HALO + r:HALO + r + slab, :] = (gc_ref[r:r + slab, :].astype(F32)
                                                 * v_ref[r:r + slab, :].astype(F32))
            dc_s[r:r + slab, :] = dc_ref[r:r + slab, :]
        w = cw_ref[...]
        sums = [jnp.zeros((ROWS, cw), F32)] * CONV_K
        for r in range(0, tm, slab):
            dc = dc_s[r:r + slab, :]
            dcv = w[2:3] * dc + w[1:2] * dc_s[r + 1:r + 1 + slab, :] + w[0:1] * dc_s[r + 2:r + 2 + slab, :]
            dgc_ref[r:r + slab, :] = (dcv * v_ref[r:r + slab, :].astype(F32)).astype(BF16)
            dv_ref[r:r + slab, :] = (dcv * gc_ref[r:r + slab, :].astype(F32)).astype(BF16)
            for k in range(CONV_K):
                lag = HALO + r - (CONV_K - 1 - k)
                prod = dc * cv_s[lag:lag + slab, :]
                sums[k] = sums[k] + jnp.sum(prod.reshape(slab // ROWS, ROWS, cw), axis=0)
        dw_ref[...] += jnp.concatenate([jnp.sum(a, axis=0, keepdims=True) for a in sums], axis=0)

    rc = _row_spec(tm, cw, tps)
    nxt = pl.BlockSpec((ROWS, cw), lambda b, i: (jnp.minimum((b * tps + i + 1) * per, nb * tps * per - 1), 0))
    hc = _halo_spec(tm, cw, tps)
    return pl.pallas_call(
        body, name="conv_backward", grid=(nb, tps),
        out_shape=[jax.ShapeDtypeStruct((t, cw), BF16), jax.ShapeDtypeStruct((t, cw), BF16),
                   jax.ShapeDtypeStruct(conv_w.shape, F32)],
        in_specs=[rc, nxt, rc, rc, hc, hc, _full(conv_w.shape)],
        out_specs=[rc, rc, _full(conv_w.shape)],
        scratch_shapes=[pltpu.VMEM((tm + HALO, cw), F32), pltpu.VMEM((tm + ROWS, cw), F32)],
        compiler_params=_params(("arbitrary", "arbitrary")),
    )(dconv, dconv, gc, v, gc, v, conv_w)


def loss_forward_backward(x, target, g):
    t, d = x.shape
    tm = _tile(t, 512)

    def body(x_ref, t_ref, g_ref, l_ref, dx_ref, dg_ref):
        _zero_when(pl.program_id(0) == 0, dg_ref)
        xv = x_ref[...]
        gv = g_ref[...]
        r = lax.rsqrt(jnp.mean(xv * xv, axis=-1, keepdims=True) + EPS)
        n = xv * r
        err = n * gv - t_ref[...]
        l_ref[...] = jnp.full(l_ref.shape, 0.5 * jnp.sum(jnp.mean(err * err, axis=-1)), F32)
        dy = err * (1.0 / d)
        dn = dy * gv
        dx_ref[...] = r * (dn - n * jnp.mean(n * dn, axis=-1, keepdims=True))
        dg_ref[...] += jnp.sum(dy * n, axis=0, keepdims=True)

    row = pl.BlockSpec((tm, d), lambda i: (i, 0))
    return pl.pallas_call(
        body, name="loss_forward_backward", grid=(t // tm,),
        out_shape=[jax.ShapeDtypeStruct((t // tm, 1, 128), F32), jax.ShapeDtypeStruct((t, d), F32),
                   jax.ShapeDtypeStruct((1, d), F32)],
        in_specs=[row, row, _full((1, d))],
        out_specs=[pl.BlockSpec((1, 1, 128), lambda i: (i, 0, 0)), row, _full((1, d))],
        compiler_params=_params(("arbitrary",)),
    )(x, target, g)


def sum_slots(slots, name):
    _, r, c = slots.shape
    tr = _tile(r, 352) if r % 352 == 0 else _tile(r, 256)
    if tr < 128:
        tr = r

    def body(s_ref, o_ref):
        acc = s_ref[0].astype(F32)
        for j in range(1, N_DEV):
            acc = acc + s_ref[j].astype(F32)
        o_ref[...] = acc

    return pl.pallas_call(
        body, name=name, grid=(r // tr,),
        out_shape=jax.ShapeDtypeStruct((r, c), F32),
        in_specs=[pl.BlockSpec((N_DEV, tr, c), lambda i: (0, i, 0))],
        out_specs=pl.BlockSpec((tr, c), lambda i: (i, 0)),
        compiler_params=_params(("arbitrary",)),
    )(slots)


def adamw_update(w, g, m, v, name):
    r, c = w.shape
    tr = _tile(r, 256) if r % 8 == 0 else r

    def body(w_ref, g_ref, m_ref, v_ref, d_ref, mo_ref, vo_ref):
        d_ref[...], mo_ref[...], vo_ref[...] = _adamw(w_ref[...], g_ref[...], m_ref[...], v_ref[...])

    spec = pl.BlockSpec((tr, c), lambda i: (i, 0))
    return pl.pallas_call(
        body, name=name, grid=(r // tr,),
        out_shape=[jax.ShapeDtypeStruct((r, c), F32)] * 3,
        in_specs=[spec] * 4, out_specs=[spec] * 3,
        compiler_params=_params(("arbitrary",)),
    )(w, g, m, v)


def sum_adamw_update(slots, w, m, v, name):
    _, r, c = slots.shape
    tr = _tile(r, 352) if r % 352 == 0 else _tile(r, 256)

    def body(s_ref, w_ref, m_ref, v_ref, g_ref, d_ref, mo_ref, vo_ref):
        g = s_ref[0].astype(F32)
        for j in range(1, N_DEV):
            g = g + s_ref[j].astype(F32)
        g_ref[...] = g
        d_ref[...], mo_ref[...], vo_ref[...] = _adamw(w_ref[...], g, m_ref[...], v_ref[...])

    spec = pl.BlockSpec((tr, c), lambda i: (i, 0))
    return pl.pallas_call(
        body, name=name, grid=(r // tr,),
        out_shape=[jax.ShapeDtypeStruct((r, c), F32)] * 4,
        in_specs=[pl.BlockSpec((N_DEV, tr, c), lambda i: (0, i, 0))] + [spec] * 3, out_specs=[spec] * 4,
        compiler_params=_params(("arbitrary",)),
    )(slots, w, m, v)


def adamw_update_small(ws, gs, ms, vs):
    n = len(ws)

    def body(*refs):
        w_r, g_r, m_r, v_r = refs[:n], refs[n:2 * n], refs[2 * n:3 * n], refs[3 * n:4 * n]
        d_r, mo_r, vo_r = refs[4 * n:5 * n], refs[5 * n:6 * n], refs[6 * n:7 * n]
        for k in range(n):
            d_r[k][...], mo_r[k][...], vo_r[k][...] = _adamw(w_r[k][...], g_r[k][...], m_r[k][...], v_r[k][...])

    shapes = [jax.ShapeDtypeStruct(w.shape, F32) for w in ws]
    out = pl.pallas_call(body, name="adamw_update_small", out_shape=shapes * 3,
                         compiler_params=_params())(*ws, *gs, *ms, *vs)
    return out[:n], out[n:2 * n], out[2 * n:]


def _slots(grad_t):
    return grad_t.reshape(N_DEV, grad_t.shape[0] // N_DEV, grad_t.shape[1])


def kernel(x, c, w_ada, b_ada, g_ffn1, w1_a, w3_a, w2_a, g_mix, w_in, conv_w, w_conv_out, a_re, a_im, b_re, b_im, c_re, c_im, log_dt, d_skip, w_glu, w_ssm_out, w_out, g_ffn2, w1_b, w3_b, w2_b, g_final, loss_target, m_w_ada, m_b_ada, m_g_ffn1, m_w1_a, m_w3_a, m_w2_a, m_g_mix, m_w_in, m_conv_w, m_w_conv_out, m_a_re, m_a_im, m_b_re, m_b_im, m_c_re, m_c_im, m_log_dt, m_d_skip, m_w_glu, m_w_ssm_out, m_w_out, m_g_ffn2, m_w1_b, m_w3_b, m_w2_b, m_g_final, v_w_ada, v_b_ada, v_g_ffn1, v_w1_a, v_w3_a, v_w2_a, v_g_mix, v_w_in, v_conv_w, v_w_conv_out, v_a_re, v_a_im, v_b_re, v_b_im, v_c_re, v_c_im, v_log_dt, v_d_skip, v_w_glu, v_w_ssm_out, v_w_out, v_g_ffn2, v_w1_b, v_w3_b, v_w2_b, v_g_final):
    nb, s, d = x.shape
    t = nb * s
    me = 4 * lax.axis_index("x") + 2 * lax.axis_index("y") + lax.axis_index("c")
    g_n, p_n, h_n = b_re.shape[1:]
    cw_n = w_conv_out.shape[1] * N_DEV
    sw_n = w_ssm_out.shape[1]
    glu_fold = d // w_glu.shape[2]

    weights = dict(w_ada=w_ada, b_ada=b_ada, g_ffn1=g_ffn1, w1_a=w1_a, w3_a=w3_a, w2_a=w2_a, g_mix=g_mix, w_in=w_in,
                   conv_w=conv_w, w_conv_out=w_conv_out, a_re=a_re, a_im=a_im, b_re=b_re, b_im=b_im, c_re=c_re,
                   c_im=c_im, log_dt=log_dt, d_skip=d_skip, w_glu=w_glu, w_ssm_out=w_ssm_out, w_out=w_out,
                   g_ffn2=g_ffn2, w1_b=w1_b, w3_b=w3_b, w2_b=w2_b, g_final=g_final)
    mom1 = dict(w_ada=m_w_ada, b_ada=m_b_ada, g_ffn1=m_g_ffn1, w1_a=m_w1_a, w3_a=m_w3_a, w2_a=m_w2_a, g_mix=m_g_mix,
                w_in=m_w_in, conv_w=m_conv_w, w_conv_out=m_w_conv_out, a_re=m_a_re, a_im=m_a_im, b_re=m_b_re,
                b_im=m_b_im, c_re=m_c_re, c_im=m_c_im, log_dt=m_log_dt, d_skip=m_d_skip, w_glu=m_w_glu,
                w_ssm_out=m_w_ssm_out, w_out=m_w_out, g_ffn2=m_g_ffn2, w1_b=m_w1_b, w3_b=m_w3_b, w2_b=m_w2_b,
                g_final=m_g_final)
    mom2 = dict(w_ada=v_w_ada, b_ada=v_b_ada, g_ffn1=v_g_ffn1, w1_a=v_w1_a, w3_a=v_w3_a, w2_a=v_w2_a, g_mix=v_g_mix,
                w_in=v_w_in, conv_w=v_conv_w, w_conv_out=v_w_conv_out, a_re=v_a_re, a_im=v_a_im, b_re=v_b_re,
                b_im=v_b_im, c_re=v_c_re, c_im=v_c_im, log_dt=v_log_dt, d_skip=v_d_skip, w_glu=v_w_glu,
                w_ssm_out=v_w_ssm_out, w_out=v_w_out, g_ffn2=v_g_ffn2, w1_b=v_w1_b, w3_b=v_w3_b, w2_b=v_w2_b,
                g_final=v_g_final)
    names = list(weights)
    transposed = ("w1_a", "w3_a", "w_in", "w_ssm_out", "w1_b", "w3_b")
    groups = dict(ffn_a=("w1_a", "w3_a", "w2_a"), mixer=("w_in", "w_conv_out", "w_glu", "w_ssm_out", "w_out"),
                  ffn_b=("w1_b", "w3_b", "w2_b"))
    big = groups["ffn_a"] + groups["mixer"] + groups["ffn_b"]

    def shard_rows(name):
        w = weights[name][0]
        if name in transposed:
            w = w.T
        if name == "w_glu":
            w = w.reshape(w.shape[0] // glu_fold, d)
        return w.astype(BF16)

    def gather_stage(group):
        return GatherStage([shard_rows(n) for n in groups[group]])

    gw = {}

    def keep_weights(group, outs):
        for n, w in zip(groups[group], outs):
            gw[n] = w.reshape(sw_n, sw_n) if n == "w_glu" else w

    pad_rows = lambda a: jnp.pad(a, ((0, -a.shape[0] % GATHER_ROWS), (0, 0)))
    c_all, conv_all, *ffn_a_weights = run_stage(
        StageGroup([GatherStage([pad_rows(c), pad_rows(conv_w[0])]), gather_stage("ffn_a")]), "gather_cond_ffn_a")
    keep_weights("ffn_a", ffn_a_weights)
    c_all = c_all.reshape(N_DEV, -1, d)[:, :nb].reshape(N_DEV * nb, d)
    conv_full = conv_all.reshape(N_DEV, GATHER_ROWS, -1)[:, :CONV_K].transpose(1, 0, 2).reshape(CONV_K, cw_n)
    ada_cols = w_ada.shape[2]
    b_cols = lax.dynamic_slice(b_ada, (0, me * ada_cols), (1, ada_cols))
    mod_cols = ada_forward(c_all, w_ada[0], b_cols)
    (mod_all,) = run_stage(GatherStage([mod_cols]), "gather_mod")
    mod_mine = lax.dynamic_slice(mod_all.reshape(N_DEV, N_DEV * nb, ada_cols), (0, me * nb, 0), (N_DEV, nb, ada_cols))
    mod = mod_mine.transpose(1, 0, 2).reshape(nb, N_MOD, d)

    disc_in = (a_re[0], a_im[0], b_re[0], b_im[0], log_dt[0])
    (abr, abi, bbr, bbi), disc_vjp = jax.vjp(ssm_discretise, *disc_in)
    bd, cd, abar8, ml, mlb = ssm_tables(abr, abi, bbr, bbi, c_re[0], c_im[0], _tile(s, 256) // ROWS)

    x0 = x.reshape(t, d)
    (x1, a1, b1, f1, h1), got = ffn_forward(x0, mod, g_ffn1, gw["w1_a"], gw["w3_a"], gw["w2_a"], 0, nb,
                                            "ffn_a_forward", carried=gather_stage("mixer"))
    keep_weights("mixer", got)
    (gb, gc, vv, u, sga, sgb, h2), got = mixer_proj_forward(x1, mod, g_mix, gw["w_in"], cw_n, sw_n, nb,
                                                            carried=gather_stage("ffn_b"))
    keep_weights("ffn_b", got)
    yssm, st = ssm_forward(u, bd, cd, abar8, ml, nb)
    x2, ya, yb, mix = mixer_merge_forward(x1, gb, gc, vv, sga, sgb, yssm, u, mod, conv_full, d_skip,
                                          gw["w_conv_out"], gw["w_glu"], gw["w_ssm_out"], gw["w_out"], nb)
    (x3, a3, b3, f3, h3), _ = ffn_forward(x2, mod, g_ffn2, gw["w1_b"], gw["w3_b"], gw["w2_b"], 2, nb,
                                          "ffn_b_forward")
    loss_parts, dx3, dg_final = loss_forward_backward(x3, loss_target.reshape(t, d), g_final.reshape(1, d))
    loss_here = jnp.sum(loss_parts[:, 0, 0]).reshape(1)

    part, received = {}, {}

    def exchange_stage(ns):
        return ExchangeStage([_slots(part[n]) for n in ns])

    (da3, db3, sw3, df3, dgt3), _ = ffn_backward_hidden(dx3, a3, b3, f3, mod, gw["w2_b"], 2, nb,
                                                        "ffn_b_backward_hidden")
    (dx2, dmod3, dg_ffn2), _ = ffn_backward_input(dx3, x2, da3, db3, mod, g_ffn2, gw["w1_b"], gw["w3_b"],
                                                  2, nb, "ffn_b_backward_input")
    (part["w1_b"],), _ = nn_matmul(da3, h3, "grad_w1_b")
    (part["w3_b"],), _ = nn_matmul(db3, h3, "grad_w3_b")
    (part["w2_b"],), _ = nn_matmul(sw3, df3, "grad_w2_b")
    (dgla, dglb, dgb, dconv, ds0, dgt2, dd_skip, dw_out, dw_co, dw_so_t, dw_glu), got = mixer_merge_backward(
        dx2, mix, ya, yb, gb, gc, vv, sga, sgb, yssm, u, mod, conv_full, d_skip,
        gw["w_conv_out"], gw["w_glu"], gw["w_ssm_out"], gw["w_out"], nb, carried=exchange_stage(("w1_b", "w3_b")))
    received.update(zip(("w1_b", "w3_b"), got))
    part["w_out"] = dw_out.astype(BF16)
    part["w_conv_out"] = dw_co.astype(BF16)
    part["w_ssm_out"] = dw_so_t.astype(BF16)
    part["w_glu"] = dw_glu.reshape(sw_n // glu_fold, d).astype(BF16)
    (du, dab, dbd, dcd), got = ssm_backward(u, ds0, st, bd, cd, abar8, mlb, d_skip, nb,
                                            carried=exchange_stage(("w2_b",) + groups["mixer"][1:]))
    received.update(zip(("w2_b",) + groups["mixer"][1:], got))
    dgc, dvv, dconv_w = conv_backward(dconv, gc, vv, conv_full, nb)
    part["w_in"] = jnp.concatenate([tn_matmul(p, h2, "grad_w_in_%d" % k)
                                    for k, p in enumerate((dgb, dgc, dvv, du, dgla, dglb))], axis=0)
    (dx1, dmod2, dg_mix), got = mixer_proj_backward(dgb, dgc, dvv, du, dgla, dglb, dx2, x1, mod, g_mix, gw["w_in"], nb,
                                                    carried=exchange_stage(("w_in",)))
    received["w_in"] = got[0]
    def pack(parts):
        flat = jnp.concatenate([a.reshape(-1) for a in parts.values()])
        rows = -(-flat.shape[0] // (128 * GATHER_ROWS)) * GATHER_ROWS
        return jnp.pad(flat, (0, rows * 128 - flat.shape[0])).reshape(rows, 128)

    def unpack(flat, parts):
        out, off = {}, 0
        for key, like in parts.items():
            n = math.prod(like.shape)
            out[key], off = flat[..., off:off + n].reshape(flat.shape[:-1] + like.shape), off + n
        return out

    dabr, dabi, dbbr, dbbi, dcr, dci = ssm_table_grads(dab, dbd, dcd, g_n, p_n, h_n)
    early = dict(gmod=jnp.concatenate([dmod2, dgt2, dmod3, dgt3], axis=1), g_mix=dg_mix, g_ffn2=dg_ffn2,
                 g_final=dg_final, d_skip=dd_skip, abr=dabr, abi=dabi, bbr=dbbr, bbi=dbbi, c_re=dcr, c_im=dci,
                 conv_w=dconv_w, loss=loss_here)
    (da1, db1, sw1, df1, dgt1), _ = ffn_backward_hidden(dx1, a1, b1, f1, mod, gw["w2_a"], 0, nb,
                                                        "ffn_a_backward_hidden")
    (part["w2_a"],), (early_all,) = nn_matmul(sw1, df1, "grad_w2_a", carried=GatherStage([pack(early)]))
    (part["w1_a"],), got = nn_matmul(da1, h1, "grad_w1_a", carried=exchange_stage(("w2_a",)))
    received["w2_a"] = got[0]
    (part["w3_a"],), got = nn_matmul(db1, h1, "grad_w3_a", carried=exchange_stage(("w1_a",)))
    received["w1_a"] = got[0]
    (dx0, dmod1, dg_ffn1), got = ffn_backward_input(dx1, x0, da1, db1, mod, g_ffn1, gw["w1_a"], gw["w3_a"],
                                                    0, nb, "ffn_a_backward_input", carried=exchange_stage(("w3_a",)))
    received["w3_a"] = got[0]
    late = dict(gmod=jnp.concatenate([dmod1, dgt1], axis=1), g_ffn1=dg_ffn1)
    (late_all,) = run_stage(GatherStage([pack(late)]), "gather_ffn_a_small_grads")

    tot, per_dev = {}, {}
    for parts, gathered, name in ((early, early_all, "early"), (late, late_all, "late")):
        slots = gathered.reshape(N_DEV, -1, 128)
        total = sum_slots(slots, "sum_small_grads_" + name).reshape(-1)
        for key, val in unpack(total, parts).items():
            tot[name + key if key == "gmod" else key] = val
        per_dev[name] = unpack(slots.reshape(N_DEV, -1), parts)["gmod"]
    gmod_all = jnp.concatenate([per_dev["late"], per_dev["early"]], axis=2).reshape(N_DEV * nb, N_MOD * d)
    gmod_tot = jnp.concatenate([tot["lategmod"], tot["earlygmod"]], axis=1)
    g_a_re, g_a_im, g_b_re, g_b_im, g_log_dt = disc_vjp((tot["abr"], tot["abi"], tot["bbr"], tot["bbi"]))

    loss = tot["loss"].reshape(())
    grads = {}
    grads["b_ada"] = sum_rows(gmod_tot.reshape(nb, N_MOD * d))
    grads["g_ffn1"], grads["g_mix"], grads["g_ffn2"] = tot["g_ffn1"], tot["g_mix"], tot["g_ffn2"]
    grads["g_final"] = tot["g_final"].reshape(d)
    grads["d_skip"] = tot["d_skip"]
    grads["a_re"], grads["a_im"], grads["log_dt"] = g_a_re[None], g_a_im[None], g_log_dt[None]
    grads["b_re"], grads["b_im"] = g_b_re[None], g_b_im[None]
    grads["c_re"], grads["c_im"] = tot["c_re"][None], tot["c_im"][None]
    grads["conv_w"] = lax.dynamic_slice(tot["conv_w"], (0, me * conv_w.shape[2]), (CONV_K, conv_w.shape[2]))[None]

    delta, new_m, new_v = {}, {}, {}
    for name in big:
        wmv = (weights[name][0], mom1[name][0], mom2[name][0])
        if name in transposed:
            outs = sum_adamw_update(received[name], *[a.T for a in wmv], "adamw_" + name)
            gsum, dl, mm, vn = [a.T for a in outs]
        elif received[name].shape[1:] == weights[name].shape[1:]:
            gsum, dl, mm, vn = sum_adamw_update(received[name], *wmv, "adamw_" + name)
        else:
            gsum = sum_slots(received[name], "sum_" + name).reshape(weights[name].shape[1:])
            dl, mm, vn = adamw_update(*wmv[:1], gsum, *wmv[1:], "adamw_" + name)
        grads[name] = gsum[None]
        delta[name], new_m[name], new_v[name] = dl[None], mm[None], vn[None]

    gmod_cols = lax.dynamic_slice(gmod_all, (0, me * ada_cols), (N_DEV * nb, ada_cols))
    g_wada, d_wada, m_wada, v_wada = ada_backward_update(c_all, gmod_cols, w_ada[0], m_w_ada[0], v_w_ada[0])
    grads["w_ada"], delta["w_ada"], new_m["w_ada"], new_v["w_ada"] = g_wada[None], d_wada[None], m_wada[None], v_wada[None]

    small_names = [n for n in names if n not in big and n != "w_ada"]

    narrow = ("b_re", "b_im")

    def as2d(n, a):
        a = jnp.swapaxes(a.reshape(weights[n].shape), -1, -2) if n in narrow else a
        return a.reshape(-1, a.shape[-1])

    def from2d(n, a):
        shape = weights[n].shape
        return jnp.swapaxes(a.reshape(shape[:-2] + (shape[-1], shape[-2])), -1, -2) if n in narrow else a.reshape(shape)

    sw_, sg_, sm_, sv_ = ([as2d(n, src[n]) for n in small_names] for src in (weights, grads, mom1, mom2))
    sd, smo, svo = adamw_update_small(sw_, sg_, sm_, sv_)
    for n, dl, mm, vn in zip(small_names, sd, smo, svo):
        grads[n] = grads[n].reshape(weights[n].shape)
        delta[n], new_m[n], new_v[n] = from2d(n, dl), from2d(n, mm), from2d(n, vn)

    grad_x = dx0.reshape(nb, s, d)
    return (loss, grad_x, *[grads[n] for n in names], *[delta[n] for n in names],
            *[new_m[n] for n in names], *[new_v[n] for n in names])


def sum_rows(a):
    r, c = a.shape

    def body(a_ref, o_ref):
        acc = a_ref[0:1, :]
        for j in range(1, r):
            acc = acc + a_ref[j:j + 1, :]
        o_ref[...] = acc

    return pl.pallas_call(body, name="sum_rows", out_shape=jax.ShapeDtypeStruct((1, c), F32),
                          compiler_params=_params())(a)
```

```python
import functools
import math

import jax
import jax.numpy as jnp
from jax import lax
from jax.experimental import pallas as pl
from jax.experimental.pallas import tpu as pltpu

F32 = jnp.float32
BF16 = jnp.bfloat16
N_DEV = 8
N_MOD = 9
EPS = 1e-6
CONV_K = 3
ADAM_LR = 0.001
ADAM_B1 = 0.9
ADAM_B2 = 0.999
ADAM_EPS = 1e-08
ADAM_WD = 0.01
ADAM_STEP = 10
GELU_C0 = math.sqrt(2.0 / math.pi)
GELU_C1 = 0.044715
V7X_VMEM_LIMIT = 56 * 1024 * 1024
MESH_ID = pl.DeviceIdType.MESH
NT = (((1,), (1,)), ((), ()))
TN = (((0,), (0,)), ((), ()))


def _dot(a, b, dims=None):
    if dims is None:
        return jnp.dot(a, b, preferred_element_type=F32)
    return lax.dot_general(a, b, dims, preferred_element_type=F32)


def _params(sem=None, vmem=V7X_VMEM_LIMIT):
    return pltpu.CompilerParams(dimension_semantics=sem, vmem_limit_bytes=vmem)


def _full(shape):
    return pl.BlockSpec(shape, lambda *_: (0,) * len(shape))


def _const(shape):
    return pl.BlockSpec(shape, lambda *_: (0,) * len(shape), pipeline_mode=pl.Buffered(1))


def _tile(n, want):
    t = min(n, want)
    while n % t:
        t //= 2
    return t


class GatherStage:
    COPIES = 9

    def __init__(self, shards):
        n = len(shards)
        self.inputs = list(shards)
        self.out_shape = [jax.ShapeDtypeStruct((N_DEV * s.shape[0], s.shape[1]), s.dtype) for s in shards]
        self.scratch = [pltpu.SemaphoreType.DMA((self.COPIES * n,)), pltpu.SemaphoreType.DMA((self.COPIES * n,)),
                        pltpu.SemaphoreType.DMA((n,))]

    def _plan(self, ins, outs, sems):
        send_sems, recv_sems, local_sems = sems
        n = len(ins)
        x, y, c = lax.axis_index("x"), lax.axis_index("y"), lax.axis_index("c")
        me, sibling, xn, yn, dg = (x, y, c), (x, y, 1 - c), (1 - x, y, c), (x, 1 - y, c), (1 - x, 1 - y, c)

        def rows(k, block, half=None):
            r = ins[k].shape[0]
            px, py, pc = block
            base = (4 * px + 2 * py + pc) * r
            if half is None:
                return outs[k].at[pl.ds(base, r), :]
            return outs[k].at[pl.ds(base + half * (r // 2), r // 2), :]

        def copy(k, j, block, to, half=None, src=None):
            return pltpu.make_async_remote_copy(
                src_ref=rows(k, block, half) if src is None else src, dst_ref=rows(k, block, half),
                send_sem=send_sems.at[self.COPIES * k + j], recv_sem=recv_sems.at[self.COPIES * k + j],
                device_id=to, device_id_type=MESH_ID)

        sib = lambda b: (b[0], b[1], 1 - b[2])
        mine = [pltpu.make_async_copy(ins[k], rows(k, me), local_sems.at[k]) for k in range(n)]
        first = [(0, me, sibling, None, sibling), (1, me, xn, None, xn), (2, me, yn, None, yn)]
        second = [(3, xn, yn, 0, dg), (4, yn, xn, 1, dg), (5, xn, sibling, None, sib(xn)), (6, yn, sibling, None, sib(yn))]
        third = [(7, dg, sibling, 0, sib(dg)), (8, dg, sibling, 1, sib(dg))]
        return n, me, copy, mine, first, second, third

    def start(self, ins, outs, sems):
        n, me, copy, mine, first, _, _ = self._plan(ins, outs, sems)
        for cp in mine:
            cp.start()
        for k in range(n):
            for j, block, to, half, _ in first:
                copy(k, j, block, to, half, src=ins[k]).start()

    def advance(self, ins, outs, sems):
        n, me, copy, mine, first, second, third = self._plan(ins, outs, sems)
        for k in range(n):
            copy(k, 1, first[1][4], me).wait_recv()
            copy(k, 2, first[2][4], me).wait_recv()
            for j, block, to, half, _ in second:
                copy(k, j, block, to, half).start()

    def advance_again(self, ins, outs, sems):
        n, me, copy, mine, first, second, third = self._plan(ins, outs, sems)
        for k in range(n):
            copy(k, 3, second[0][4], me, 0).wait_recv()
            copy(k, 4, second[1][4], me, 1).wait_recv()
            for j, block, to, half, _ in third:
                copy(k, j, block, to, half).start()

    def finish(self, ins, outs, sems):
        n, me, copy, mine, first, second, third = self._plan(ins, outs, sems)
        arrived = lambda k, j, block, half: copy(k, j, block, me, half).wait_recv()
        for k in range(n):
            arrived(k, 0, first[0][4], None)
            arrived(k, 5, second[2][4], None)
            arrived(k, 6, second[3][4], None)
            arrived(k, 7, third[0][4], 0)
            arrived(k, 8, third[1][4], 1)
        for k in range(n):
            for j, block, to, half, _ in first:
                copy(k, j, block, to, half, src=ins[k]).wait_send()
            for j, block, to, half, _ in second + third:
                copy(k, j, block, to, half).wait_send()
        for cp in mine:
            cp.wait()


class ExchangeStage:
    def __init__(self, bufs):
        n = len(bufs)
        self.inputs = list(bufs)
        self.out_shape = [jax.ShapeDtypeStruct(b.shape, b.dtype) for b in bufs]
        self.scratch = [pltpu.SemaphoreType.DMA((7 * n,)), pltpu.SemaphoreType.DMA((7 * n,)),
                        pltpu.SemaphoreType.DMA((n,))]

    def _plan(self, ins, outs, sems):
        send_sems, recv_sems, local_sems = sems
        n = len(ins)
        x, y, c = lax.axis_index("x"), lax.axis_index("y"), lax.axis_index("c")
        me = 4 * x + 2 * y + c
        mine = [pltpu.make_async_copy(ins[k].at[me], outs[k].at[me], local_sems.at[k]) for k in range(n)]
        copies = []
        for mask in range(1, N_DEV):
            px, py, pc = x ^ (mask >> 2), y ^ ((mask >> 1) & 1), c ^ (mask & 1)
            for k in range(n):
                copies.append(pltpu.make_async_remote_copy(
                    src_ref=ins[k].at[4 * px + 2 * py + pc], dst_ref=outs[k].at[me],
                    send_sem=send_sems.at[7 * k + mask - 1], recv_sem=recv_sems.at[7 * k + mask - 1],
                    device_id=(px, py, pc), device_id_type=MESH_ID))
        return mine, copies

    def start(self, ins, outs, sems):
        mine, copies = self._plan(ins, outs, sems)
        for cp in mine + copies:
            cp.start()

    def advance(self, ins, outs, sems):
        pass

    advance_again = advance

    def finish(self, ins, outs, sems):
        mine, copies = self._plan(ins, outs, sems)
        for cp in copies:
            cp.wait_recv()
        for cp in copies:
            cp.wait_send()
        for cp in mine:
            cp.wait()


ANY_SPEC = pl.BlockSpec(memory_space=pl.ANY)
GATHER_ROWS = 16


class StageGroup:
    def __init__(self, stages):
        self.stages = list(stages)
        self.inputs = [a for s in stages for a in s.inputs]
        self.out_shape = [o for s in stages for o in s.out_shape]
        self.scratch = [t for s in stages for t in s.scratch]

    def _parts(self, ins, outs, sems):
        i = o = t = 0
        for s in self.stages:
            ni, no, nt = len(s.inputs), len(s.out_shape), len(s.scratch)
            yield s, ins[i:i + ni], outs[o:o + no], sems[t:t + nt]
            i, o, t = i + ni, o + no, t + nt

    def start(self, ins, outs, sems):
        for s, i_, o_, t_ in self._parts(ins, outs, sems):
            s.start(i_, o_, t_)

    def advance(self, ins, outs, sems):
        for s, i_, o_, t_ in self._parts(ins, outs, sems):
            s.advance(i_, o_, t_)

    def advance_again(self, ins, outs, sems):
        for s, i_, o_, t_ in self._parts(ins, outs, sems):
            s.advance_again(i_, o_, t_)

    def finish(self, ins, outs, sems):
        for s, i_, o_, t_ in self._parts(ins, outs, sems):
            s.finish(i_, o_, t_)


def run_stage(stage, name):
    ci, co = len(stage.inputs), len(stage.out_shape)

    def body(*refs):
        ins, outs, sems = refs[:ci], refs[ci:ci + co], refs[ci + co:]
        stage.start(ins, outs, sems)
        stage.advance(ins, outs, sems)
        stage.advance_again(ins, outs, sems)
        stage.finish(ins, outs, sems)

    return pl.pallas_call(body, name=name, out_shape=stage.out_shape, in_specs=[ANY_SPEC] * ci,
                          out_specs=[ANY_SPEC] * co, scratch_shapes=stage.scratch)(*stage.inputs)


def _call(body, *, name, grid, in_specs, out_specs, out_shape, args, scratch_shapes=(), carried=None):
    sem = ("arbitrary",) * len(grid)
    if carried is None:
        return pl.pallas_call(body, name=name, grid=grid, in_specs=list(in_specs), out_specs=list(out_specs),
                              out_shape=list(out_shape), scratch_shapes=list(scratch_shapes),
                              compiler_params=_params(sem))(*args), None
    ni, no, ns = len(in_specs), len(out_shape), len(scratch_shapes)
    ci, co = len(carried.inputs), len(carried.out_shape)
    n_steps = math.prod(grid)

    def wrapped(*refs):
        ins, refs = refs[:ni], refs[ni:]
        cins, refs = refs[:ci], refs[ci:]
        outs, refs = refs[:no], refs[no:]
        couts, refs = refs[:co], refs[co:]
        scr, csems = refs[:ns], refs[ns:]
        step = functools.reduce(lambda acc, ig: acc * ig[1] + ig[0],
                                zip([pl.program_id(k) for k in range(len(grid))], grid), 0)

        @pl.when(step == 0)
        def _():
            carried.start(cins, couts, csems)

        @pl.when(step == n_steps // 2)
        def _():
            carried.advance(cins, couts, csems)

        @pl.when(step == (3 * n_steps) // 4)
        def _():
            carried.advance_again(cins, couts, csems)

        body(*ins, *outs, *scr)

        @pl.when(step == n_steps - 1)
        def _():
            carried.finish(cins, couts, csems)

    res = pl.pallas_call(
        wrapped, name=name, grid=grid, in_specs=list(in_specs) + [ANY_SPEC] * ci,
        out_specs=list(out_specs) + [ANY_SPEC] * co, out_shape=list(out_shape) + carried.out_shape,
        scratch_shapes=list(scratch_shapes) + carried.scratch, compiler_params=_params(sem),
    )(*args, *carried.inputs)
    return res[:no], res[no:]


def _norm_mod(x, g, shift, scale):
    r = lax.rsqrt(jnp.mean(x * x, axis=-1, keepdims=True) + EPS)
    n = x * r
    return (n * g) * (1.0 + scale) + shift, n, r


def _norm_mod_bwd(dh, n, r, g, scale):
    dsh = jnp.sum(dh, axis=0, keepdims=True)
    dsc = jnp.sum(dh * (n * g), axis=0, keepdims=True)
    dg = jnp.sum(dh * (1.0 + scale) * n, axis=0, keepdims=True)
    dn = dh * ((1.0 + scale) * g)
    dx = r * (dn - n * jnp.mean(n * dn, axis=-1, keepdims=True))
    return dx, dsh, dsc, dg


def _mod_rows(mod_ref, sub):
    m = mod_ref[0]
    return m[3 * sub:3 * sub + 1], m[3 * sub + 1:3 * sub + 2], m[3 * sub + 2:3 * sub + 3]


def _gelu(x):
    t = jnp.tanh(GELU_C0 * (x + GELU_C1 * x * x * x))
    return 0.5 * x * (1.0 + t), t


def _gelu_grad(x, t):
    return 0.5 * (1.0 + t) + 0.5 * x * (1.0 - t * t) * (GELU_C0 * (1.0 + 3.0 * GELU_C1 * x * x))


def _zero_when(cond, *refs):
    @pl.when(cond)
    def _():
        for r in refs:
            r[...] = jnp.zeros_like(r)


def ada_forward(c_all, w_ada, b_ada_cols):
    def body(c_ref, w_ref, b_ref, o_ref):
        c = c_ref[...]
        cond = (c * jax.nn.sigmoid(c)).astype(BF16)
        o_ref[...] = _dot(cond, w_ref[...].astype(BF16)) + b_ref[...]

    nb, d = c_all.shape
    cols = w_ada.shape[1]
    tn = _tile(cols, 384)
    return pl.pallas_call(
        body, name="ada_forward", grid=(cols // tn,),
        out_shape=jax.ShapeDtypeStruct((nb, cols), F32),
        in_specs=[_full((nb, d)), pl.BlockSpec((d, tn), lambda j: (0, j)), pl.BlockSpec((1, tn), lambda j: (0, j))],
        out_specs=pl.BlockSpec((nb, tn), lambda j: (0, j)),
        compiler_params=_params(("arbitrary",)),
    )(c_all, w_ada, b_ada_cols)


def _adamw(w, g, m, v):
    m = ADAM_B1 * m + (1.0 - ADAM_B1) * g
    v = ADAM_B2 * v + (1.0 - ADAM_B2) * (g * g)
    m_hat = m / (1.0 - ADAM_B1 ** ADAM_STEP)
    v_hat = v / (1.0 - ADAM_B2 ** ADAM_STEP)
    delta = -ADAM_LR * (m_hat / (jnp.sqrt(v_hat) + ADAM_EPS) + ADAM_WD * w)
    return delta, m, v


def ada_backward_update(c_all, gmod_cols, w, m, v):
    def body(c_ref, g_ref, w_ref, m_ref, v_ref, go_ref, d_ref, mo_ref, vo_ref):
        c = c_ref[...]
        cond = (c * jax.nn.sigmoid(c)).astype(BF16)
        g = _dot(cond, g_ref[...].astype(BF16), TN)
        go_ref[...] = g
        d_ref[...], mo_ref[...], vo_ref[...] = _adamw(w_ref[...], g, m_ref[...], v_ref[...])

    nb, d = c_all.shape
    cols = w.shape[1]
    tn = _tile(cols, 128)
    col = pl.BlockSpec((d, tn), lambda j: (0, j))
    return pl.pallas_call(
        body, name="ada_backward_update", grid=(cols // tn,),
        out_shape=[jax.ShapeDtypeStruct(w.shape, F32)] * 4,
        in_specs=[_full((nb, d)), pl.BlockSpec((nb, tn), lambda j: (0, j)), col, col, col],
        out_specs=[col] * 4,
        compiler_params=_params(("arbitrary",)),
    )(c_all, gmod_cols, w, m, v)


def _row_spec(tm, width, tiles_per_seq):
    return pl.BlockSpec((tm, width), lambda b, i: (b * tiles_per_seq + i, 0))


def _mod_spec(d):
    return pl.BlockSpec((1, N_MOD, d), lambda b, i: (b, 0, 0))


def _col_spec(rows, tm, tiles_per_seq):
    return pl.BlockSpec((rows, tm), lambda b, i: (0, b * tiles_per_seq + i))


def _ffn_chunk(f):
    return f // 4 if f % 512 == 0 and f > 1536 else f


def ffn_forward(x, mod, g, w1t, w3t, w2, sub, nb, name, carried=None):
    t, d = x.shape
    f = w1t.shape[0]
    s = t // nb
    tm = _tile(s, 512)
    fc = _ffn_chunk(f)

    def body(x_ref, mod_ref, g_ref, w1_ref, w3_ref, w2_ref, xo_ref, a_ref, b_ref, f_ref, h_ref):
        xv = x_ref[...]
        sh, sc, gt = _mod_rows(mod_ref, sub)
        h, _, _ = _norm_mod(xv, g_ref[...], sh, sc)
        hb = h.astype(BF16)
        h_ref[...] = hb
        acc_t = jnp.zeros((d, tm), F32)
        for k in range(f // fc):
            rows = slice(k * fc, (k + 1) * fc)
            a = _dot(w1_ref[rows, :], hb, NT)
            b = _dot(w3_ref[rows, :], hb, NT)
            a_ref[rows, :] = a.astype(BF16)
            b_ref[rows, :] = b.astype(BF16)
            sw = (a * jax.nn.sigmoid(a)) * b
            acc_t = acc_t + _dot(w2_ref[rows, :], sw.astype(BF16), TN)
        acc = acc_t.T
        f_ref[...] = acc.astype(BF16)
        xo_ref[...] = xv + (0.5 * gt) * acc

    tps = s // tm
    rd, cf = _row_spec(tm, d, tps), _col_spec(f, tm, tps)
    return _call(
        body, name=name, grid=(nb, tps), carried=carried,
        out_shape=[jax.ShapeDtypeStruct((t, d), F32)] + [jax.ShapeDtypeStruct((f, t), BF16)] * 2
        + [jax.ShapeDtypeStruct((t, d), BF16)] * 2,
        in_specs=[rd, _mod_spec(d), _const((1, d)), _const((f, d)), _const((f, d)), _const((f, d))],
        out_specs=[rd, cf, cf, rd, rd],
        args=(x, mod, g, w1t, w3t, w2))


def ffn_backward_hidden(dxo, a_t, b_t, fo, mod, w2, sub, nb, name, carried=None):
    t, d = dxo.shape
    f = w2.shape[0]
    s = t // nb
    tm = _tile(s, 512)
    fc = _tile(f, 704) if f % 704 == 0 else _tile(f, 512)

    def body(dxo_ref, a_ref, b_ref, f_ref, mod_ref, w2_ref, da_ref, db_ref, s_ref, df_ref, dgt_ref):
        _zero_when(pl.program_id(1) == 0, dgt_ref)
        dxo = dxo_ref[...]
        _, _, gt = _mod_rows(mod_ref, sub)
        dfb = ((0.5 * gt) * dxo).astype(BF16)
        df_ref[...] = dfb
        dgt_ref[...] += 0.5 * jnp.sum(dxo * f_ref[...].astype(F32), axis=0, keepdims=True)[None]
        for k in range(f // fc):
            rows = slice(k * fc, (k + 1) * fc)
            ds = _dot(w2_ref[rows, :], dfb, NT).astype(BF16)
            av = a_ref[rows, :].astype(F32)
            bv = b_ref[rows, :]
            sig = jax.nn.sigmoid(av)
            sl = av * sig
            slb = sl.astype(BF16)
            da_ref[rows, :] = ds * bv * (sig + sl * (1.0 - sig)).astype(BF16)
            db_ref[rows, :] = ds * slb
            s_ref[rows, :] = slb * bv

    tps = s // tm
    rd, cf = _row_spec(tm, d, tps), _col_spec(f, tm, tps)
    return _call(
        body, name=name, grid=(nb, tps), carried=carried,
        out_shape=[jax.ShapeDtypeStruct((f, t), BF16)] * 3
        + [jax.ShapeDtypeStruct((t, d), BF16), jax.ShapeDtypeStruct((nb, 1, d), F32)],
        in_specs=[rd, cf, cf, rd, _mod_spec(d), _const((f, d))],
        out_specs=[cf, cf, cf, rd, pl.BlockSpec((1, 1, d), lambda b, i: (b, 0, 0))],
        args=(dxo, a_t, b_t, fo, mod, w2))


def ffn_backward_input(dxo, x, da_t, db_t, mod, g, w1t, w3t, sub, nb, name, carried=None):
    t, d = x.shape
    f = w1t.shape[0]
    s = t // nb
    tm = _tile(s, 512)

    def body(dxo_ref, x_ref, da_ref, db_ref, mod_ref, g_ref, w1_ref, w3_ref, dx_ref, dmod_ref, dg_ref):
        bi, i = pl.program_id(0), pl.program_id(1)
        _zero_when(i == 0, dmod_ref)
        _zero_when(jnp.logical_and(bi == 0, i == 0), dg_ref)
        sh, sc, _ = _mod_rows(mod_ref, sub)
        gv = g_ref[...]
        _, n, r = _norm_mod(x_ref[...], gv, sh, sc)
        dh_t = _dot(w1_ref[...], da_ref[...], TN) + _dot(w3_ref[...], db_ref[...], TN)
        dxn, dsh, dsc, dg = _norm_mod_bwd(dh_t.T, n, r, gv, sc)
        dx_ref[...] = dxo_ref[...] + dxn
        dmod_ref[...] += jnp.concatenate([dsh, dsc], axis=0)[None]
        dg_ref[...] += dg

    tps = s // tm
    rd, cf = _row_spec(tm, d, tps), _col_spec(f, tm, tps)
    return _call(
        body, name=name, grid=(nb, tps), carried=carried,
        out_shape=[jax.ShapeDtypeStruct((t, d), F32), jax.ShapeDtypeStruct((nb, 2, d), F32),
                   jax.ShapeDtypeStruct((1, d), F32)],
        in_specs=[rd, rd, cf, cf, _mod_spec(d), _const((1, d)), _const((f, d)), _const((f, d))],
        out_specs=[rd, pl.BlockSpec((1, 2, d), lambda b, i: (b, 0, 0)), _full((1, d))],
        args=(dxo, x, da_t, db_t, mod, g, w1t, w3t))


def nn_matmul(lhs_t, rhs, name, carried=None):
    m, t = lhs_t.shape
    n = rhs.shape[1]
    tk = _tile(t, 2048)
    tmm = m if m <= 1536 else m // 2
    nk = t // tk

    def body(a_ref, b_ref, o_ref, acc_ref):
        k = pl.program_id(1)
        _zero_when(k == 0, acc_ref)
        acc_ref[...] += _dot(a_ref[...], b_ref[...])

        @pl.when(k == nk - 1)
        def _():
            o_ref[...] = acc_ref[...].astype(BF16)

    return _call(
        body, name=name, grid=(m // tmm, nk), carried=carried,
        out_shape=[jax.ShapeDtypeStruct((m, n), BF16)],
        in_specs=[pl.BlockSpec((tmm, tk), lambda j, k: (j, k)), pl.BlockSpec((tk, n), lambda j, k: (k, 0))],
        out_specs=[pl.BlockSpec((tmm, n), lambda j, k: (j, 0))],
        scratch_shapes=[pltpu.VMEM((tmm, n), F32)],
        args=(lhs_t, rhs))


def tn_matmul(lhs, rhs, name):
    t, m = lhs.shape
    n = rhs.shape[1]
    tk = _tile(t, 2048)
    nk = t // tk

    def body(a_ref, b_ref, o_ref, acc_ref):
        k = pl.program_id(0)
        _zero_when(k == 0, acc_ref)
        acc_ref[...] += _dot(a_ref[...], b_ref[...], TN)

        @pl.when(k == nk - 1)
        def _():
            o_ref[...] = acc_ref[...].astype(BF16)

    return pl.pallas_call(
        body, name=name, grid=(nk,),
        out_shape=jax.ShapeDtypeStruct((m, n), BF16),
        in_specs=[pl.BlockSpec((tk, m), lambda k: (k, 0)), pl.BlockSpec((tk, n), lambda k: (k, 0))],
        out_specs=pl.BlockSpec((m, n), lambda k: (0, 0)),
        scratch_shapes=[pltpu.VMEM((m, n), F32)],
        compiler_params=_params(("arbitrary",)),
    )(lhs, rhs)


def mixer_proj_forward(x, mod, g, w_in_t, cw, sw, nb, carried=None):
    t, d = x.shape
    s = t // nb
    tm = _tile(s, 512)
    pieces = [(0, cw, "bf16"), (cw, cw, "bf16"), (2 * cw, cw, "bf16"), (3 * cw, sw, "f32"),
              (3 * cw + sw, d, "sig"), (3 * cw + sw + d, d, "sig")]

    def body(x_ref, mod_ref, g_ref, w_ref, *outs):
        h_ref = outs[-1]
        sh, sc, _ = _mod_rows(mod_ref, 1)
        h, _, _ = _norm_mod(x_ref[...], g_ref[...], sh, sc)
        hb = h.astype(BF16)
        h_ref[...] = hb
        for (off, width, kind), o_ref in zip(pieces, outs[:-1]):
            ck = _tile(width, 512)
            for j in range(width // ck):
                p = _dot(hb, w_ref[off + j * ck:off + (j + 1) * ck, :], NT)
                if kind == "sig":
                    p = jax.nn.sigmoid(p)
                o_ref[:, j * ck:(j + 1) * ck] = p.astype(o_ref.dtype)

    tps = s // tm
    widths = [(cw, BF16), (cw, BF16), (cw, BF16), (sw, F32), (d, BF16), (d, BF16), (d, BF16)]
    return _call(
        body, name="mixer_proj_forward", grid=(nb, tps), carried=carried,
        out_shape=[jax.ShapeDtypeStruct((t, w), dt) for w, dt in widths],
        in_specs=[_row_spec(tm, d, tps), _mod_spec(d), _const((1, d)), _const(w_in_t.shape)],
        out_specs=[_row_spec(tm, w, tps) for w, _ in widths],
        args=(x, mod, g, w_in_t))


def mixer_proj_backward(dgb, dgc, dv, du, dgla, dglb, dxo, x, mod, g, w_in_t, nb, carried=None):
    t, d = x.shape
    s = t // nb
    tm = _tile(s, 512)
    parts = [dgb, dgc, dv, du, dgla, dglb]
    offs = [0]
    for p in parts:
        offs.append(offs[-1] + p.shape[1])

    def body(*refs):
        p_refs = refs[:6]
        dxo_ref, x_ref, mod_ref, g_ref, w_ref, dx_ref, dmod_ref, dg_ref = refs[6:]
        bi, i = pl.program_id(0), pl.program_id(1)
        _zero_when(i == 0, dmod_ref)
        _zero_when(jnp.logical_and(bi == 0, i == 0), dg_ref)
        dh = jnp.zeros((tm, d), F32)
        for p_ref, off in zip(p_refs, offs):
            dh = dh + _dot(p_ref[...], w_ref[off:off + p_ref.shape[1], :])
        sh, sc, _ = _mod_rows(mod_ref, 1)
        gv = g_ref[...]
        _, n, r = _norm_mod(x_ref[...], gv, sh, sc)
        dxn, dsh, dsc, dg = _norm_mod_bwd(dh, n, r, gv, sc)
        dx_ref[...] = dxo_ref[...] + dxn
        dmod_ref[...] += jnp.concatenate([dsh, dsc], axis=0)[None]
        dg_ref[...] += dg

    tps = s // tm
    rd = _row_spec(tm, d, tps)
    return _call(
        body, name="mixer_proj_backward", grid=(nb, tps), carried=carried,
        out_shape=[jax.ShapeDtypeStruct((t, d), F32), jax.ShapeDtypeStruct((nb, 2, d), F32),
                   jax.ShapeDtypeStruct((1, d), F32)],
        in_specs=[_row_spec(tm, p.shape[1], tps) for p in parts]
        + [rd, rd, _mod_spec(d), _const((1, d)), _const(w_in_t.shape)],
        out_specs=[rd, pl.BlockSpec((1, 2, d), lambda b, i: (b, 0, 0)), _full((1, d))],
        args=(*parts, dxo, x, mod, g, w_in_t))


GROUPS_PER_BLOCK = 8
ROWS = 8
SSM_CHUNK = 512
SCAN_LANES = 512
SCAN_UNROLL = 8


def _scan_rows(xr, xi, masks, shifts):
    for (mr, mi), sft in zip(masks, shifts):
        sr, si = pltpu.roll(xr, sft, 0), pltpu.roll(xi, sft, 0)
        xr, xi = xr + mr * sr - mi * si, xi + mr * si + mi * sr
    return xr, xi


def _cmul_add(ar, ai, cr, ci, br, bi):
    return ar * cr - ai * ci + br, ar * ci + ai * cr + bi


def _segment_rows(perm_ref, x):
    return _dot(perm_ref[0], x).astype(BF16)


def _time_rows(perm_ref, x):
    hi = x.astype(BF16)
    lo = (x - hi.astype(F32)).astype(BF16)
    return _dot(perm_ref[1], hi) + _dot(perm_ref[1], lo)


def segment_permutation(tc):
    r = jnp.arange(tc)
    p = (r[:, None] % ROWS * (tc // ROWS) + r[:, None] // ROWS == r[None, :]).astype(BF16)
    return jnp.stack([p, p.T])


def _scan_loop(n, step, init):
    def trip(j, carry):
        for r in range(SCAN_UNROLL):
            carry = step(j * SCAN_UNROLL + r, carry)
        return carry

    return lax.fori_loop(0, n // SCAN_UNROLL, trip, init)


def _rows_at(k, offset=0):
    return pl.ds(pl.multiple_of(k * ROWS + offset, ROWS), ROWS)


def ssm_forward(u, bd, cd, a1, ml, nb):
    t, w = u.shape
    s = t // nb
    tc = _tile(s, SSM_CHUNK)
    seg = tc // ROWS
    nq, ub, lq = bd.shape[1], bd.shape[2], bd.shape[3]
    nl = nq * lq
    nch = s // tc
    lw = min(nl, SCAN_LANES)

    def body(u_ref, perm_ref, bd_ref, cd_ref, a1_ref, ml_ref, y_ref, st_ref, xr_s, xi_s, car_s):
        i = pl.program_id(1)

        @pl.when(i == 0)
        def _():
            car_s[...] = jnp.zeros_like(car_s)

        ub16 = _segment_rows(perm_ref, u_ref[...].astype(BF16))
        for q in range(nq):
            lanes = slice(q * lq, (q + 1) * lq)
            uq = ub16[:, q * ub:(q + 1) * ub]
            xr_s[:, lanes] = _dot(uq, bd_ref[0, q])
            xi_s[:, lanes] = _dot(uq, bd_ref[1, q])
        row_is_0 = lax.broadcasted_iota(jnp.int32, (ROWS, lw), 0) == 0
        zero = jnp.zeros((ROWS, lw), F32)
        for j in range(nl // lw):
            lanes = slice(j * lw, (j + 1) * lw)
            ar, ai = a1_ref[0, :, lanes], a1_ref[1, :, lanes]

            def local(k, c):
                return _cmul_add(ar, ai, c[0], c[1], xr_s[_rows_at(k), lanes], xi_s[_rows_at(k), lanes])

            er, ei = _scan_loop(seg, local, (zero, zero))
            masks = [(ml_ref[d, 0, :, lanes], ml_ref[d, 1, :, lanes]) for d in range(3)]
            cr, ci = _scan_rows(jnp.where(row_is_0, car_s[0, :, lanes], pltpu.roll(er, 1, 0)),
                                jnp.where(row_is_0, car_s[1, :, lanes], pltpu.roll(ei, 1, 0)), masks, (1, 2, 4))
            st_ref[0, 0, :, lanes] = cr
            st_ref[0, 1, :, lanes] = ci

            def full(k, c):
                xr, xi = _cmul_add(ar, ai, c[0], c[1], xr_s[_rows_at(k), lanes], xi_s[_rows_at(k), lanes])
                xr_s[_rows_at(k), lanes] = xr
                xi_s[_rows_at(k), lanes] = xi
                return xr, xi

            fr, fi = _scan_loop(seg, full, (cr, ci))
            car_s[0, :, lanes] = jnp.broadcast_to(fr[ROWS - 1:ROWS], fr.shape)
            car_s[1, :, lanes] = jnp.broadcast_to(fi[ROWS - 1:ROWS], fi.shape)
        y = jnp.concatenate([_dot(xr_s[:, q * lq:(q + 1) * lq].astype(BF16), cd_ref[0, q])
                             + _dot(xi_s[:, q * lq:(q + 1) * lq].astype(BF16), cd_ref[1, q]) for q in range(nq)],
                            axis=1)
        y_ref[...] = _time_rows(perm_ref, y)

    perm = segment_permutation(tc)
    return pl.pallas_call(
        body, name="ssm_forward", grid=(nb, nch),
        out_shape=[jax.ShapeDtypeStruct((t, w), F32), jax.ShapeDtypeStruct((nb * nch, 2, ROWS, nl), F32)],
        in_specs=[pl.BlockSpec((tc, w), lambda b, i: (b * nch + i, 0)), _const(perm.shape), _const(bd.shape),
                  _const(cd.shape), _const(a1.shape), _const(ml.shape)],
        out_specs=[pl.BlockSpec((tc, w), lambda b, i: (b * nch + i, 0)),
                   pl.BlockSpec((1, 2, ROWS, nl), lambda b, i: (b * nch + i, 0, 0, 0))],
        scratch_shapes=[pltpu.VMEM((tc, nl), F32), pltpu.VMEM((tc, nl), F32), pltpu.VMEM((2, ROWS, nl), F32)],
        compiler_params=_params(("arbitrary", "arbitrary")),
    )(u, perm, bd, cd, a1, ml)


def ssm_backward(u, dy, st, bd, cd, a1, mlb, dskip, nb, carried=None):
    t, w = u.shape
    s = t // nb
    tc = _tile(s, SSM_CHUNK)
    seg = tc // ROWS
    nq, ub, lq = bd.shape[1], bd.shape[2], bd.shape[3]
    nl = nq * lq
    nch = s // tc
    lw = min(nl, SCAN_LANES)

    def body(u_ref, dy_ref, st_ref, perm_ref, bd_ref, cd_ref, a1_ref, mlb_ref, dsk_ref,
             du_ref, dab_ref, dbd_ref, dcd_ref, xr_s, xi_s, lr_s, li_s, car_s):
        bi, i = pl.program_id(0), pl.program_id(1)
        first = jnp.logical_and(bi == 0, i == 0)

        @pl.when(i == 0)
        def _():
            car_s[...] = jnp.zeros_like(car_s)

        @pl.when(first)
        def _():
            dab_ref[...] = jnp.zeros_like(dab_ref)
            dbd_ref[...] = jnp.zeros_like(dbd_ref)
            dcd_ref[...] = jnp.zeros_like(dcd_ref)

        ub16 = _segment_rows(perm_ref, u_ref[...].astype(BF16))
        dyb16 = _segment_rows(perm_ref, dy_ref[...].astype(BF16))
        xr_s[0:ROWS, :] = st_ref[0, 0]
        xi_s[0:ROWS, :] = st_ref[0, 1]
        for q in range(nq):
            lanes = slice(q * lq, (q + 1) * lq)
            uq = ub16[:, q * ub:(q + 1) * ub]
            dq = dyb16[:, q * ub:(q + 1) * ub]
            xr_s[ROWS:, lanes] = _dot(uq, bd_ref[0, q])
            xi_s[ROWS:, lanes] = _dot(uq, bd_ref[1, q])
            lr_s[:, lanes] = _dot(dq, cd_ref[0, q], NT)
            li_s[:, lanes] = _dot(dq, cd_ref[1, q], NT)
        row_is_7 = lax.broadcasted_iota(jnp.int32, (ROWS, lw), 0) == ROWS - 1
        zero = jnp.zeros((ROWS, lw), F32)
        for j in range(nl // lw):
            lanes = slice(j * lw, (j + 1) * lw)
            ar, ai = a1_ref[0, :, lanes], a1_ref[1, :, lanes]
            nai = -ai

            def states(k, c):
                xr, xi = _cmul_add(ar, ai, c[0], c[1], xr_s[_rows_at(k, ROWS), lanes], xi_s[_rows_at(k, ROWS), lanes])
                xr_s[_rows_at(k, ROWS), lanes] = xr
                xi_s[_rows_at(k, ROWS), lanes] = xi
                return xr, xi

            _scan_loop(seg, states, (st_ref[0, 0, :, lanes], st_ref[0, 1, :, lanes]))

            def local(kk, c):
                k = seg - 1 - kk
                return _cmul_add(ar, nai, c[0], c[1], lr_s[_rows_at(k), lanes], li_s[_rows_at(k), lanes])

            er, ei = _scan_loop(seg, local, (zero, zero))
            masks = [(mlb_ref[d, 0, :, lanes], mlb_ref[d, 1, :, lanes]) for d in range(3)]
            cr, ci = _scan_rows(jnp.where(row_is_7, car_s[0, :, lanes], pltpu.roll(er, ROWS - 1, 0)),
                                jnp.where(row_is_7, car_s[1, :, lanes], pltpu.roll(ei, ROWS - 1, 0)), masks, (7, 6, 4))

            def full(kk, c):
                cr_, ci_, accr, acci = c
                k = seg - 1 - kk
                lr, li = _cmul_add(ar, nai, cr_, ci_, lr_s[_rows_at(k), lanes], li_s[_rows_at(k), lanes])
                lr_s[_rows_at(k), lanes] = lr
                li_s[_rows_at(k), lanes] = li
                xpr, xpi = xr_s[_rows_at(k), lanes], xi_s[_rows_at(k), lanes]
                return lr, li, accr + lr * xpr + li * xpi, acci + li * xpr - lr * xpi

            lr0, li0, accr, acci = _scan_loop(seg, full, (cr, ci, zero, zero))
            car_s[0, :, lanes] = jnp.broadcast_to(lr0[0:1], lr0.shape)
            car_s[1, :, lanes] = jnp.broadcast_to(li0[0:1], li0.shape)
            dab_ref[0, :, lanes] += accr
            dab_ref[1, :, lanes] += acci
        du_parts = []
        for q in range(nq):
            lanes = slice(q * lq, (q + 1) * lq)
            cols = slice(q * ub, (q + 1) * ub)
            lrb, lib = lr_s[:, lanes].astype(BF16), li_s[:, lanes].astype(BF16)
            uq, dq = ub16[:, cols], dyb16[:, cols]
            du_parts.append(_dot(lrb, bd_ref[0, q], NT) + _dot(lib, bd_ref[1, q], NT))
            dbd_ref[0, q] += _dot(uq, lrb, TN)
            dbd_ref[1, q] += _dot(uq, lib, TN)
            dcd_ref[0, q] += _dot(xr_s[ROWS:, lanes].astype(BF16), dq, TN)
            dcd_ref[1, q] += _dot(xi_s[ROWS:, lanes].astype(BF16), dq, TN)
        du = _time_rows(perm_ref, jnp.concatenate(du_parts, axis=1)) + dsk_ref[...] * dy_ref[...]
        du_ref[...] = du.astype(BF16)

    rev = lambda b, i: (b * nch + nch - 1 - i, 0)
    perm = segment_permutation(tc)
    return _call(
        body, name="ssm_backward", grid=(nb, nch), carried=carried,
        out_shape=[jax.ShapeDtypeStruct((t, w), BF16), jax.ShapeDtypeStruct((2, ROWS, nl), F32),
                   jax.ShapeDtypeStruct(bd.shape, F32), jax.ShapeDtypeStruct(cd.shape, F32)],
        in_specs=[pl.BlockSpec((tc, w), rev), pl.BlockSpec((tc, w), rev),
                  pl.BlockSpec((1, 2, ROWS, nl), lambda b, i: (b * nch + nch - 1 - i, 0, 0, 0)),
                  _const(perm.shape), _const(bd.shape), _const(cd.shape), _const(a1.shape), _const(mlb.shape),
                  _const((1, w))],
        out_specs=[pl.BlockSpec((tc, w), rev), _full((2, ROWS, nl)), _full(bd.shape), _full(cd.shape)],
        scratch_shapes=[pltpu.VMEM((tc + ROWS, nl), F32), pltpu.VMEM((tc + ROWS, nl), F32),
                        pltpu.VMEM((tc, nl), F32), pltpu.VMEM((tc, nl), F32), pltpu.VMEM((2, ROWS, nl), F32)],
        args=(u, dy, st, perm, bd, cd, a1, mlb, dskip))


def ssm_discretise(a_re, a_im, b_re, b_im, log_dt):
    dt = jnp.exp(log_dt)[:, None]
    er = jnp.exp(a_re * dt)
    abr, abi = er * jnp.cos(a_im * dt), er * jnp.sin(a_im * dt)
    den = a_re * a_re + a_im * a_im
    nr, ni = abr - 1.0, abi
    fr = ((nr * a_re + ni * a_im) / den)[..., None]
    fi = ((ni * a_re - nr * a_im) / den)[..., None]
    return abr, abi, fr * b_re - fi * b_im, fr * b_im + fi * b_re


def _complex_square(zr, zi):
    return zr * zr - zi * zi, 2.0 * zr * zi


def ssm_tables(abr, abi, bbr, bbi, c_re, c_im, seg):
    g, p, h = bbr.shape
    nq = g // GROUPS_PER_BLOCK
    zr, zi = abr.reshape(1, -1), abi.reshape(1, -1)
    a1 = jnp.stack([jnp.broadcast_to(zr, (ROWS, g * p)), jnp.broadcast_to(zi, (ROWS, g * p))])
    for _ in range(seg.bit_length() - 1):
        zr, zi = _complex_square(zr, zi)
    row = jnp.arange(ROWS)[:, None]
    ml, mlb = [], []
    for d in (1, 2, 4):
        ml.append(jnp.stack([jnp.where(row >= d, zr, 0.0), jnp.where(row >= d, zi, 0.0)]))
        mlb.append(jnp.stack([jnp.where(row + d < ROWS, zr, 0.0), jnp.where(row + d < ROWS, -zi, 0.0)]))
        zr, zi = _complex_square(zr, zi)
    eye = jnp.eye(GROUPS_PER_BLOCK, dtype=F32)

    def block_diag_in(bb):
        bq = bb.reshape(nq, GROUPS_PER_BLOCK, p, h)
        return jnp.einsum("qaph,ab->qahbp", bq, eye).reshape(nq, GROUPS_PER_BLOCK * h, GROUPS_PER_BLOCK * p)

    def block_diag_out(cc):
        cq = cc.reshape(nq, GROUPS_PER_BLOCK, h, p)
        return jnp.einsum("qahp,ab->qapbh", cq, eye).reshape(nq, GROUPS_PER_BLOCK * p, GROUPS_PER_BLOCK * h)

    bd = jnp.stack([block_diag_in(bbr), block_diag_in(bbi)]).astype(BF16)
    cd = jnp.stack([block_diag_out(c_re), block_diag_out(-c_im)]).astype(BF16)
    return bd, cd, a1, jnp.stack(ml), jnp.stack(mlb)


def ssm_table_grads(dab, dbd, dcd, g, p, h):
    nq = g // GROUPS_PER_BLOCK
    dabr, dabi = dab[0].sum(0).reshape(g, p), dab[1].sum(0).reshape(g, p)
    b5 = dbd.reshape(2, nq, GROUPS_PER_BLOCK, h, GROUPS_PER_BLOCK, p)
    dbb = jnp.einsum("rqahap->rqaph", b5).reshape(2, g, p, h)
    c5 = dcd.reshape(2, nq, GROUPS_PER_BLOCK, p, GROUPS_PER_BLOCK, h)
    dcc = jnp.einsum("rqapah->rqahp", c5).reshape(2, g, h, p)
    return dabr, dabi, dbb[0], dbb[1], dcc[0], -dcc[1]


HALO = 16


def _conv_inputs(gc_ref, v_ref, gch_ref, vh_ref, cv_s, i, tm):
    cv = gc_ref[...].astype(F32) * v_ref[...].astype(F32)
    halo = gch_ref[...].astype(F32) * vh_ref[...].astype(F32)
    cv_s[0:HALO, :] = jnp.where(i == 0, 0.0, halo)
    cv_s[HALO:, :] = cv
    return cv, cv_s[HALO - 1:HALO - 1 + tm, :], cv_s[HALO - 2:HALO - 2 + tm, :]


def _halo_spec(tm, width, tiles_per_seq):
    per = tm // HALO
    return pl.BlockSpec((HALO, width), lambda b, i: (jnp.maximum((b * tiles_per_seq + i) * per - 1, 0), 0))


def mixer_merge_forward(x, gb, gc, v, sga, sgb, yssm, u, mod, conv_w, dskip, wco, wglu, wso_t, wout, nb):
    t, d = x.shape
    cw, sw = gb.shape[1], u.shape[1]
    s = t // nb
    tm = _tile(s, 512)
    tps = s // tm

    def body(x_ref, gb_ref, gc_ref, v_ref, gch_ref, vh_ref, sga_ref, sgb_ref, ys_ref, u_ref, mod_ref, cw_ref,
             dsk_ref, wco_ref, wglu_ref, wso_ref, wout_ref, xo_ref, ya_ref, yb_ref, mix_ref, cv_s):
        i = pl.program_id(1)
        cv, cv1, cv2 = _conv_inputs(gc_ref, v_ref, gch_ref, vh_ref, cv_s, i, tm)
        w = cw_ref[...]
        conv = w[0:1] * cv2 + w[1:2] * cv1 + w[2:3] * cv
        ya = _dot((gb_ref[...].astype(F32) * conv).astype(BF16), wco_ref[...])
        s0 = ys_ref[...] + dsk_ref[...] * u_ref[...]
        s1, _ = _gelu(s0)
        z = _dot(s1.astype(BF16), wglu_ref[...])
        s2 = s1 * jax.nn.sigmoid(z)
        yb = _dot(s2.astype(BF16), wso_ref[...], NT)
        merged = sga_ref[...].astype(F32) * ya + sgb_ref[...].astype(F32) * yb
        mix = _dot(merged.astype(BF16), wout_ref[...])
        _, _, gt = _mod_rows(mod_ref, 1)
        xo_ref[...] = x_ref[...] + gt * mix
        ya_ref[...] = ya.astype(BF16)
        yb_ref[...] = yb.astype(BF16)
        mix_ref[...] = mix.astype(BF16)

    rd, rc, rw = _row_spec(tm, d, tps), _row_spec(tm, cw, tps), _row_spec(tm, sw, tps)
    hc = _halo_spec(tm, cw, tps)
    return pl.pallas_call(
        body, name="mixer_merge_forward", grid=(nb, tps),
        out_shape=[jax.ShapeDtypeStruct((t, d), F32)] + [jax.ShapeDtypeStruct((t, d), BF16)] * 3,
        in_specs=[rd, rc, rc, rc, hc, hc, rd, rd, rw, rw, _mod_spec(d), _const(conv_w.shape), _const((1, sw)),
                  _const(wco.shape), _const(wglu.shape), _const(wso_t.shape), _const(wout.shape)],
        out_specs=[rd, rd, rd, rd],
        scratch_shapes=[pltpu.VMEM((tm + HALO, cw), F32)],
        compiler_params=_params(("arbitrary", "arbitrary")),
    )(x, gb, gc, v, gc, v, sga, sgb, yssm, u, mod, conv_w, dskip, wco, wglu, wso_t, wout)


def mixer_merge_backward(dxo, mix, ya, yb, gb, gc, v, sga, sgb, yssm, u, mod, conv_w, dskip,
                         wco, wglu, wso_t, wout, nb, carried=None):
    t, d = dxo.shape
    cw, sw = gb.shape[1], u.shape[1]
    s = t // nb
    tm = _tile(s, 256)
    tps = s // tm

    def body(dxo_ref, mix_ref, ya_ref, yb_ref, gb_ref, gc_ref, v_ref, gch_ref, vh_ref, sga_ref, sgb_ref, ys_ref,
             u_ref, mod_ref, cw_ref, dsk_ref, wco_ref, wglu_ref, wso_ref, wout_ref,
             dgla_ref, dglb_ref, dgb_ref, dconv_ref, ds0_ref, dgt_ref, ddsk_ref, dwout_ref, dwco_ref, dwso_ref,
             dwglu_ref, cv_s):
        bi, i = pl.program_id(0), pl.program_id(1)
        _zero_when(i == 0, dgt_ref)
        _zero_when(jnp.logical_and(bi == 0, i == 0), ddsk_ref, dwout_ref, dwco_ref, dwso_ref, dwglu_ref)
        dxo = dxo_ref[...]
        _, _, gt = _mod_rows(mod_ref, 1)
        dmix = (gt * dxo).astype(BF16)
        dmerged = _dot(dmix, wout_ref[...], NT)
        uv = u_ref[...]
        s0 = ys_ref[...] + dsk_ref[...] * uv
        s1, th = _gelu(s0)
        s1b = s1.astype(BF16)
        sz = jax.nn.sigmoid(_dot(s1b, wglu_ref[...]))
        s2b = (s1 * sz).astype(BF16)
        cv, cv1, cv2 = _conv_inputs(gc_ref, v_ref, gch_ref, vh_ref, cv_s, i, tm)
        w = cw_ref[...]
        conv = w[0:1] * cv2 + w[1:2] * cv1 + w[2:3] * cv
        gbv = gb_ref[...].astype(F32)
        ya, yb = ya_ref[...].astype(F32), yb_ref[...].astype(F32)
        sga, sgb = sga_ref[...].astype(F32), sgb_ref[...].astype(F32)
        merged = (sga * ya + sgb * yb).astype(BF16)
        dwout_ref[...] += _dot(merged, dmix, TN)
        dgt_ref[...] += jnp.sum(dxo * mix_ref[...].astype(F32), axis=0, keepdims=True)[None]
        dgla_ref[...] = (dmerged * ya * sga * (1.0 - sga)).astype(BF16)
        dglb_ref[...] = (dmerged * yb * sgb * (1.0 - sgb)).astype(BF16)
        dya = (dmerged * sga).astype(BF16)
        dyb = (dmerged * sgb).astype(BF16)
        dwco_ref[...] += _dot((gbv * conv).astype(BF16), dya, TN)
        dya_in = _dot(dya, wco_ref[...], NT)
        dgb_ref[...] = (dya_in * conv).astype(BF16)
        dconv_ref[...] = dya_in * gbv
        dwso_ref[...] += _dot(dyb, s2b, TN)
        ds2 = _dot(dyb, wso_ref[...])
        dz = (ds2 * s1 * sz * (1.0 - sz)).astype(BF16)
        dwglu_ref[...] += _dot(s1b, dz, TN)
        ds1 = ds2 * sz + _dot(dz, wglu_ref[...], NT)
        ds0 = ds1 * _gelu_grad(s0, th)
        ds0_ref[...] = ds0
        ddsk_ref[...] += jnp.sum(ds0 * uv, axis=0, keepdims=True)

    rd, rc, rw = _row_spec(tm, d, tps), _row_spec(tm, cw, tps), _row_spec(tm, sw, tps)
    hc = _halo_spec(tm, cw, tps)
    return _call(
        body, name="mixer_merge_backward", grid=(nb, tps), carried=carried,
        out_shape=[jax.ShapeDtypeStruct((t, d), BF16), jax.ShapeDtypeStruct((t, d), BF16),
                   jax.ShapeDtypeStruct((t, cw), BF16), jax.ShapeDtypeStruct((t, cw), F32),
                   jax.ShapeDtypeStruct((t, sw), F32), jax.ShapeDtypeStruct((nb, 1, d), F32),
                   jax.ShapeDtypeStruct((1, sw), F32), jax.ShapeDtypeStruct(wout.shape, F32),
                   jax.ShapeDtypeStruct(wco.shape, F32), jax.ShapeDtypeStruct(wso_t.shape, F32),
                   jax.ShapeDtypeStruct(wglu.shape, F32)],
        in_specs=[rd, rd, rd, rd, rc, rc, rc, hc, hc, rd, rd, rw, rw, _mod_spec(d), _const(conv_w.shape),
                  _const((1, sw)), _const(wco.shape), _const(wglu.shape), _const(wso_t.shape), _const(wout.shape)],
        out_specs=[rd, rd, rc, rc, rw, pl.BlockSpec((1, 1, d), lambda b, i: (b, 0, 0)), _full((1, sw)),
                   _full(wout.shape), _full(wco.shape), _full(wso_t.shape), _full(wglu.shape)],
        scratch_shapes=[pltpu.VMEM((tm + HALO, cw), F32)],
        args=(dxo, mix, ya, yb, gb, gc, v, gc, v, sga, sgb, yssm, u, mod, conv_w, dskip, wco, wglu, wso_t, wout))


def conv_backward(dconv, gc, v, conv_w, nb):
    t, cw = dconv.shape
    s = t // nb
    tm = _tile(s, 512)
    tps = s // tm
    per = tm // ROWS
    slab = _tile(tm, 32)

    def body(dc_ref, dcn_ref, gc_ref, v_ref, gch_ref, vh_ref, cw_ref, dgc_ref, dv_ref, dw_ref, cv_s, dc_s):
        bi, i = pl.program_id(0), pl.program_id(1)
        _zero_when(jnp.logical_and(bi == 0, i == 0), dw_ref)
        cv_s[0:HALO, :] = jnp.where(i == 0, 0.0, gch_ref[...].astype(F32) * vh_ref[...].astype(F32))
        dc_s[tm:, :] = jnp.where(i == tps - 1, 0.0, dcn_ref[...])
        for r in range(0, tm, slab):
            cv_s[HALO + r:HALO + r + slab, :] = (gc_ref[r:r + slab, :].astype(F32)
                                                 * v_ref[r:r + slab, :].astype(F32))
            dc_s[r:r + slab, :] = dc_ref[r:r + slab, :]
        w = cw_ref[...]
        sums = [jnp.zeros((ROWS, cw), F32)] * CONV_K
        for r in range(0, tm, slab):
            dc = dc_s[r:r + slab, :]
            dcv = w[2:3] * dc + w[1:2] * dc_s[r + 1:r + 1 + slab, :] + w[0:1] * dc_s[r + 2:r + 2 + slab, :]
            dgc_ref[r:r + slab, :] = (dcv * v_ref[r:r + slab, :].astype(F32)).astype(BF16)
            dv_ref[r:r + slab, :] = (dcv * gc_ref[r:r + slab, :].astype(F32)).astype(BF16)
            for k in range(CONV_K):
                lag = HALO + r - (CONV_K - 1 - k)
                prod = dc * cv_s[lag:lag + slab, :]
                sums[k] = sums[k] + jnp.sum(prod.reshape(slab // ROWS, ROWS, cw), axis=0)
        dw_ref[...] += jnp.concatenate([jnp.sum(a, axis=0, keepdims=True) for a in sums], axis=0)

    rc = _row_spec(tm, cw, tps)
    nxt = pl.BlockSpec((ROWS, cw), lambda b, i: (jnp.minimum((b * tps + i + 1) * per, nb * tps * per - 1), 0))
    hc = _halo_spec(tm, cw, tps)
    return pl.pallas_call(
        body, name="conv_backward", grid=(nb, tps),
        out_shape=[jax.ShapeDtypeStruct((t, cw), BF16), jax.ShapeDtypeStruct((t, cw), BF16),
                   jax.ShapeDtypeStruct(conv_w.shape, F32)],
        in_specs=[rc, nxt, rc, rc, hc, hc, _full(conv_w.shape)],
        out_specs=[rc, rc, _full(conv_w.shape)],
        scratch_shapes=[pltpu.VMEM((tm + HALO, cw), F32), pltpu.VMEM((tm + ROWS, cw), F32)],
        compiler_params=_params(("arbitrary", "arbitrary")),
    )(dconv, dconv, gc, v, gc, v, conv_w)


def loss_forward_backward(x, target, g):
    t, d = x.shape
    tm = _tile(t, 512)

    def body(x_ref, t_ref, g_ref, l_ref, dx_ref, dg_ref):
        _zero_when(pl.program_id(0) == 0, dg_ref)
        xv = x_ref[...]
        gv = g_ref[...]
        r = lax.rsqrt(jnp.mean(xv * xv, axis=-1, keepdims=True) + EPS)
        n = xv * r
        err = n * gv - t_ref[...]
        l_ref[...] = jnp.full(l_ref.shape, 0.5 * jnp.sum(jnp.mean(err * err, axis=-1)), F32)
        dy = err * (1.0 / d)
        dn = dy * gv
        dx_ref[...] = r * (dn - n * jnp.mean(n * dn, axis=-1, keepdims=True))
        dg_ref[...] += jnp.sum(dy * n, axis=0, keepdims=True)

    row = pl.BlockSpec((tm, d), lambda i: (i, 0))
    return pl.pallas_call(
        body, name="loss_forward_backward", grid=(t // tm,),
        out_shape=[jax.ShapeDtypeStruct((t // tm, 1, 128), F32), jax.ShapeDtypeStruct((t, d), F32),
                   jax.ShapeDtypeStruct((1, d), F32)],
        in_specs=[row, row, _full((1, d))],
        out_specs=[pl.BlockSpec((1, 1, 128), lambda i: (i, 0, 0)), row, _full((1, d))],
        compiler_params=_params(("arbitrary",)),
    )(x, target, g)


def sum_slots(slots, name):
    _, r, c = slots.shape
    tr = _tile(r, 352) if r % 352 == 0 else _tile(r, 256)
    if tr < 128:
        tr = r

    def body(s_ref, o_ref):
        acc = s_ref[0].astype(F32)
        for j in range(1, N_DEV):
            acc = acc + s_ref[j].astype(F32)
        o_ref[...] = acc

    return pl.pallas_call(
        body, name=name, grid=(r // tr,),
        out_shape=jax.ShapeDtypeStruct((r, c), F32),
        in_specs=[pl.BlockSpec((N_DEV, tr, c), lambda i: (0, i, 0))],
        out_specs=pl.BlockSpec((tr, c), lambda i: (i, 0)),
        compiler_params=_params(("arbitrary",)),
    )(slots)


def adamw_update(w, g, m, v, name):
    r, c = w.shape
    tr = _tile(r, 256) if r % 8 == 0 else r

    def body(w_ref, g_ref, m_ref, v_ref, d_ref, mo_ref, vo_ref):
        d_ref[...], mo_ref[...], vo_ref[...] = _adamw(w_ref[...], g_ref[...], m_ref[...], v_ref[...])

    spec = pl.BlockSpec((tr, c), lambda i: (i, 0))
    return pl.pallas_call(
        body, name=name, grid=(r // tr,),
        out_shape=[jax.ShapeDtypeStruct((r, c), F32)] * 3,
        in_specs=[spec] * 4, out_specs=[spec] * 3,
        compiler_params=_params(("arbitrary",)),
    )(w, g, m, v)


def sum_adamw_update(slots, w, m, v, name):
    _, r, c = slots.shape
    tr = _tile(r, 352) if r % 352 == 0 else _tile(r, 256)

    def body(s_ref, w_ref, m_ref, v_ref, g_ref, d_ref, mo_ref, vo_ref):
        g = s_ref[0].astype(F32)
        for j in range(1, N_DEV):
            g = g + s_ref[j].astype(F32)
        g_ref[...] = g
        d_ref[...], mo_ref[...], vo_ref[...] = _adamw(w_ref[...], g, m_ref[...], v_ref[...])

    spec = pl.BlockSpec((tr, c), lambda i: (i, 0))
    return pl.pallas_call(
        body, name=name, grid=(r // tr,),
        out_shape=[jax.ShapeDtypeStruct((r, c), F32)] * 4,
        in_specs=[pl.BlockSpec((N_DEV, tr, c), lambda i: (0, i, 0))] + [spec] * 3, out_specs=[spec] * 4,
        compiler_params=_params(("arbitrary",)),
    )(slots, w, m, v)


def adamw_update_small(ws, gs, ms, vs):
    n = len(ws)

    def body(*refs):
        w_r, g_r, m_r, v_r = refs[:n], refs[n:2 * n], refs[2 * n:3 * n], refs[3 * n:4 * n]
        d_r, mo_r, vo_r = refs[4 * n:5 * n], refs[5 * n:6 * n], refs[6 * n:7 * n]
        for k in range(n):
            d_r[k][...], mo_r[k][...], vo_r[k][...] = _adamw(w_r[k][...], g_r[k][...], m_r[k][...], v_r[k][...])

    shapes = [jax.ShapeDtypeStruct(w.shape, F32) for w in ws]
    out = pl.pallas_call(body, name="adamw_update_small", out_shape=shapes * 3,
                         compiler_params=_params())(*ws, *gs, *ms, *vs)
    return out[:n], out[n:2 * n], out[2 * n:]


def _slots(grad_t):
    return grad_t.reshape(N_DEV, grad_t.shape[0] // N_DEV, grad_t.shape[1])


def kernel(x, c, w_ada, b_ada, g_ffn1, w1_a, w3_a, w2_a, g_mix, w_in, conv_w, w_conv_out, a_re, a_im, b_re, b_im, c_re, c_im, log_dt, d_skip, w_glu, w_ssm_out, w_out, g_ffn2, w1_b, w3_b, w2_b, g_final, loss_target, m_w_ada, m_b_ada, m_g_ffn1, m_w1_a, m_w3_a, m_w2_a, m_g_mix, m_w_in, m_conv_w, m_w_conv_out, m_a_re, m_a_im, m_b_re, m_b_im, m_c_re, m_c_im, m_log_dt, m_d_skip, m_w_glu, m_w_ssm_out, m_w_out, m_g_ffn2, m_w1_b, m_w3_b, m_w2_b, m_g_final, v_w_ada, v_b_ada, v_g_ffn1, v_w1_a, v_w3_a, v_w2_a, v_g_mix, v_w_in, v_conv_w, v_w_conv_out, v_a_re, v_a_im, v_b_re, v_b_im, v_c_re, v_c_im, v_log_dt, v_d_skip, v_w_glu, v_w_ssm_out, v_w_out, v_g_ffn2, v_w1_b, v_w3_b, v_w2_b, v_g_final):
    nb, s, d = x.shape
    t = nb * s
    me = 4 * lax.axis_index("x") + 2 * lax.axis_index("y") + lax.axis_index("c")
    g_n, p_n, h_n = b_re.shape[1:]
    cw_n = w_conv_out.shape[1] * N_DEV
    sw_n = w_ssm_out.shape[1]
    glu_fold = d // w_glu.shape[2]

    weights = dict(w_ada=w_ada, b_ada=b_ada, g_ffn1=g_ffn1, w1_a=w1_a, w3_a=w3_a, w2_a=w2_a, g_mix=g_mix, w_in=w_in,
                   conv_w=conv_w, w_conv_out=w_conv_out, a_re=a_re, a_im=a_im, b_re=b_re, b_im=b_im, c_re=c_re,
                   c_im=c_im, log_dt=log_dt, d_skip=d_skip, w_glu=w_glu, w_ssm_out=w_ssm_out, w_out=w_out,
                   g_ffn2=g_ffn2, w1_b=w1_b, w3_b=w3_b, w2_b=w2_b, g_final=g_final)
    mom1 = dict(w_ada=m_w_ada, b_ada=m_b_ada, g_ffn1=m_g_ffn1, w1_a=m_w1_a, w3_a=m_w3_a, w2_a=m_w2_a, g_mix=m_g_mix,
                w_in=m_w_in, conv_w=m_conv_w, w_conv_out=m_w_conv_out, a_re=m_a_re, a_im=m_a_im, b_re=m_b_re,
                b_im=m_b_im, c_re=m_c_re, c_im=m_c_im, log_dt=m_log_dt, d_skip=m_d_skip, w_glu=m_w_glu,
                w_ssm_out=m_w_ssm_out, w_out=m_w_out, g_ffn2=m_g_ffn2, w1_b=m_w1_b, w3_b=m_w3_b, w2_b=m_w2_b,
                g_final=m_g_final)
    mom2 = dict(w_ada=v_w_ada, b_ada=v_b_ada, g_ffn1=v_g_ffn1, w1_a=v_w1_a, w3_a=v_w3_a, w2_a=v_w2_a, g_mix=v_g_mix,
                w_in=v_w_in, conv_w=v_conv_w, w_conv_out=v_w_conv_out, a_re=v_a_re, a_im=v_a_im, b_re=v_b_re,
                b_im=v_b_im, c_re=v_c_re, c_im=v_c_im, log_dt=v_log_dt, d_skip=v_d_skip, w_glu=v_w_glu,
                w_ssm_out=v_w_ssm_out, w_out=v_w_out, g_ffn2=v_g_ffn2, w1_b=v_w1_b, w3_b=v_w3_b, w2_b=v_w2_b,
                g_final=v_g_final)
    names = list(weights)
    transposed = ("w1_a", "w3_a", "w_in", "w_ssm_out", "w1_b", "w3_b")
    groups = dict(ffn_a=("w1_a", "w3_a", "w2_a"), mixer=("w_in", "w_conv_out", "w_glu", "w_ssm_out", "w_out"),
                  ffn_b=("w1_b", "w3_b", "w2_b"))
    big = groups["ffn_a"] + groups["mixer"] + groups["ffn_b"]

    def shard_rows(name):
        w = weights[name][0]
        if name in transposed:
            w = w.T
        if name == "w_glu":
            w = w.reshape(w.shape[0] // glu_fold, d)
        return w.astype(BF16)

    def gather_stage(group):
        return GatherStage([shard_rows(n) for n in groups[group]])

    gw = {}

    def keep_weights(group, outs):
        for n, w in zip(groups[group], outs):
            gw[n] = w.reshape(sw_n, sw_n) if n == "w_glu" else w

    pad_rows = lambda a: jnp.pad(a, ((0, -a.shape[0] % GATHER_ROWS), (0, 0)))
    c_all, conv_all, *ffn_a_weights = run_stage(
        StageGroup([GatherStage([pad_rows(c), pad_rows(conv_w[0])]), gather_stage("ffn_a")]), "gather_cond_ffn_a")
    keep_weights("ffn_a", ffn_a_weights)
    c_all = c_all.reshape(N_DEV, -1, d)[:, :nb].reshape(N_DEV * nb, d)
    conv_full = conv_all.reshape(N_DEV, GATHER_ROWS, -1)[:, :CONV_K].transpose(1, 0, 2).reshape(CONV_K, cw_n)
    ada_cols = w_ada.shape[2]
    b_cols = lax.dynamic_slice(b_ada, (0, me * ada_cols), (1, ada_cols))
    mod_cols = ada_forward(c_all, w_ada[0], b_cols)
    (mod_all,) = run_stage(GatherStage([mod_cols]), "gather_mod")
    mod_mine = lax.dynamic_slice(mod_all.reshape(N_DEV, N_DEV * nb, ada_cols), (0, me * nb, 0), (N_DEV, nb, ada_cols))
    mod = mod_mine.transpose(1, 0, 2).reshape(nb, N_MOD, d)

    disc_in = (a_re[0], a_im[0], b_re[0], b_im[0], log_dt[0])
    (abr, abi, bbr, bbi), disc_vjp = jax.vjp(ssm_discretise, *disc_in)
    bd, cd, abar8, ml, mlb = ssm_tables(abr, abi, bbr, bbi, c_re[0], c_im[0], _tile(s, SSM_CHUNK) // ROWS)

    x0 = x.reshape(t, d)
    (x1, a1, b1, f1, h1), got = ffn_forward(x0, mod, g_ffn1, gw["w1_a"], gw["w3_a"], gw["w2_a"], 0, nb,
                                            "ffn_a_forward", carried=gather_stage("mixer"))
    keep_weights("mixer", got)
    (gb, gc, vv, u, sga, sgb, h2), got = mixer_proj_forward(x1, mod, g_mix, gw["w_in"], cw_n, sw_n, nb,
                                                            carried=gather_stage("ffn_b"))
    keep_weights("ffn_b", got)
    yssm, st = ssm_forward(u, bd, cd, abar8, ml, nb)
    x2, ya, yb, mix = mixer_merge_forward(x1, gb, gc, vv, sga, sgb, yssm, u, mod, conv_full, d_skip,
                                          gw["w_conv_out"], gw["w_glu"], gw["w_ssm_out"], gw["w_out"], nb)
    (x3, a3, b3, f3, h3), _ = ffn_forward(x2, mod, g_ffn2, gw["w1_b"], gw["w3_b"], gw["w2_b"], 2, nb,
                                          "ffn_b_forward")
    loss_parts, dx3, dg_final = loss_forward_backward(x3, loss_target.reshape(t, d), g_final.reshape(1, d))
    loss_here = jnp.sum(loss_parts[:, 0, 0]).reshape(1)

    part, received = {}, {}

    def exchange_stage(ns):
        return ExchangeStage([_slots(part[n]) for n in ns])

    (da3, db3, sw3, df3, dgt3), _ = ffn_backward_hidden(dx3, a3, b3, f3, mod, gw["w2_b"], 2, nb,
                                                        "ffn_b_backward_hidden")
    (dx2, dmod3, dg_ffn2), _ = ffn_backward_input(dx3, x2, da3, db3, mod, g_ffn2, gw["w1_b"], gw["w3_b"],
                                                  2, nb, "ffn_b_backward_input")
    (part["w1_b"],), _ = nn_matmul(da3, h3, "grad_w1_b")
    (part["w3_b"],), _ = nn_matmul(db3, h3, "grad_w3_b")
    (part["w2_b"],), _ = nn_matmul(sw3, df3, "grad_w2_b")
    (dgla, dglb, dgb, dconv, ds0, dgt2, dd_skip, dw_out, dw_co, dw_so_t, dw_glu), got = mixer_merge_backward(
        dx2, mix, ya, yb, gb, gc, vv, sga, sgb, yssm, u, mod, conv_full, d_skip,
        gw["w_conv_out"], gw["w_glu"], gw["w_ssm_out"], gw["w_out"], nb, carried=exchange_stage(("w1_b", "w3_b")))
    received.update(zip(("w1_b", "w3_b"), got))
    part["w_out"] = dw_out.astype(BF16)
    part["w_conv_out"] = dw_co.astype(BF16)
    part["w_ssm_out"] = dw_so_t.astype(BF16)
    part["w_glu"] = dw_glu.reshape(sw_n // glu_fold, d).astype(BF16)
    (du, dab, dbd, dcd), got = ssm_backward(u, ds0, st, bd, cd, abar8, mlb, d_skip, nb,
                                            carried=exchange_stage(("w2_b",) + groups["mixer"][1:]))
    received.update(zip(("w2_b",) + groups["mixer"][1:], got))
    dgc, dvv, dconv_w = conv_backward(dconv, gc, vv, conv_full, nb)
    part["w_in"] = jnp.concatenate([tn_matmul(p, h2, "grad_w_in_%d" % k)
                                    for k, p in enumerate((dgb, dgc, dvv, du, dgla, dglb))], axis=0)
    (dx1, dmod2, dg_mix), got = mixer_proj_backward(dgb, dgc, dvv, du, dgla, dglb, dx2, x1, mod, g_mix, gw["w_in"], nb,
                                                    carried=exchange_stage(("w_in",)))
    received["w_in"] = got[0]
    def pack(parts):
        flat = jnp.concatenate([a.reshape(-1) for a in parts.values()])
        rows = -(-flat.shape[0] // (128 * GATHER_ROWS)) * GATHER_ROWS
        return jnp.pad(flat, (0, rows * 128 - flat.shape[0])).reshape(rows, 128)

    def unpack(flat, parts):
        out, off = {}, 0
        for key, like in parts.items():
            n = math.prod(like.shape)
            out[key], off = flat[..., off:off + n].reshape(flat.shape[:-1] + like.shape), off + n
        return out

    dabr, dabi, dbbr, dbbi, dcr, dci = ssm_table_grads(dab, dbd, dcd, g_n, p_n, h_n)
    early = dict(gmod=jnp.concatenate([dmod2, dgt2, dmod3, dgt3], axis=1), g_mix=dg_mix, g_ffn2=dg_ffn2,
                 g_final=dg_final, d_skip=dd_skip, abr=dabr, abi=dabi, bbr=dbbr, bbi=dbbi, c_re=dcr, c_im=dci,
                 conv_w=dconv_w, loss=loss_here)
    (da1, db1, sw1, df1, dgt1), _ = ffn_backward_hidden(dx1, a1, b1, f1, mod, gw["w2_a"], 0, nb,
                                                        "ffn_a_backward_hidden")
    (part["w2_a"],), (early_all,) = nn_matmul(sw1, df1, "grad_w2_a", carried=GatherStage([pack(early)]))
    (part["w1_a"],), got = nn_matmul(da1, h1, "grad_w1_a", carried=exchange_stage(("w2_a",)))
    received["w2_a"] = got[0]
    (part["w3_a"],), got = nn_matmul(db1, h1, "grad_w3_a", carried=exchange_stage(("w1_a",)))
    received["w1_a"] = got[0]
    (dx0, dmod1, dg_ffn1), got = ffn_backward_input(dx1, x0, da1, db1, mod, g_ffn1, gw["w1_a"], gw["w3_a"],
                                                    0, nb, "ffn_a_backward_input", carried=exchange_stage(("w3_a",)))
    received["w3_a"] = got[0]
    late = dict(gmod=jnp.concatenate([dmod1, dgt1], axis=1), g_ffn1=dg_ffn1)
    (late_all,) = run_stage(GatherStage([pack(late)]), "gather_ffn_a_small_grads")

    tot, per_dev = {}, {}
    for parts, gathered, name in ((early, early_all, "early"), (late, late_all, "late")):
        slots = gathered.reshape(N_DEV, -1, 128)
        total = sum_slots(slots, "sum_small_grads_" + name).reshape(-1)
        for key, val in unpack(total, parts).items():
            tot[name + key if key == "gmod" else key] = val
        per_dev[name] = unpack(slots.reshape(N_DEV, -1), parts)["gmod"]
    gmod_all = jnp.concatenate([per_dev["late"], per_dev["early"]], axis=2).reshape(N_DEV * nb, N_MOD * d)
    gmod_tot = jnp.concatenate([tot["lategmod"], tot["earlygmod"]], axis=1)
    g_a_re, g_a_im, g_b_re, g_b_im, g_log_dt = disc_vjp((tot["abr"], tot["abi"], tot["bbr"], tot["bbi"]))

    loss = tot["loss"].reshape(())
    grads = {}
    grads["b_ada"] = sum_rows(gmod_tot.reshape(nb, N_MOD * d))
    grads["g_ffn1"], grads["g_mix"], grads["g_ffn2"] = tot["g_ffn1"], tot["g_mix"], tot["g_ffn2"]
    grads["g_final"] = tot["g_final"].reshape(d)
    grads["d_skip"] = tot["d_skip"]
    grads["a_re"], grads["a_im"], grads["log_dt"] = g_a_re[None], g_a_im[None], g_log_dt[None]
    grads["b_re"], grads["b_im"] = g_b_re[None], g_b_im[None]
    grads["c_re"], grads["c_im"] = tot["c_re"][None], tot["c_im"][None]
    grads["conv_w"] = lax.dynamic_slice(tot["conv_w"], (0, me * conv_w.shape[2]), (CONV_K, conv_w.shape[2]))[None]

    delta, new_m, new_v = {}, {}, {}
    for name in big:
        wmv = (weights[name][0], mom1[name][0], mom2[name][0])
        if name in transposed:
            outs = sum_adamw_update(received[name], *[a.T for a in wmv], "adamw_" + name)
            gsum, dl, mm, vn = [a.T for a in outs]
        elif received[name].shape[1:] == weights[name].shape[1:]:
            gsum, dl, mm, vn = sum_adamw_update(received[name], *wmv, "adamw_" + name)
        else:
            gsum = sum_slots(received[name], "sum_" + name).reshape(weights[name].shape[1:])
            dl, mm, vn = adamw_update(*wmv[:1], gsum, *wmv[1:], "adamw_" + name)
        grads[name] = gsum[None]
        delta[name], new_m[name], new_v[name] = dl[None], mm[None], vn[None]

    gmod_cols = lax.dynamic_slice(gmod_all, (0, me * ada_cols), (N_DEV * nb, ada_cols))
    g_wada, d_wada, m_wada, v_wada = ada_backward_update(c_all, gmod_cols, w_ada[0], m_w_ada[0], v_w_ada[0])
    grads["w_ada"], delta["w_ada"], new_m["w_ada"], new_v["w_ada"] = g_wada[None], d_wada[None], m_wada[None], v_wada[None]

    small_names = [n for n in names if n not in big and n != "w_ada"]

    narrow = ("b_re", "b_im")

    def as2d(n, a):
        a = jnp.swapaxes(a.reshape(weights[n].shape), -1, -2) if n in narrow else a
        return a.reshape(-1, a.shape[-1])

    def from2d(n, a):
        shape = weights[n].shape
        return jnp.swapaxes(a.reshape(shape[:-2] + (shape[-1], shape[-2])), -1, -2) if n in narrow else a.reshape(shape)

    sw_, sg_, sm_, sv_ = ([as2d(n, src[n]) for n in small_names] for src in (weights, grads, mom1, mom2))
    sd, smo, svo = adamw_update_small(sw_, sg_, sm_, sv_)
    for n, dl, mm, vn in zip(small_names, sd, smo, svo):
        grads[n] = grads[n].reshape(weights[n].shape)
        delta[n], new_m[n], new_v[n] = from2d(n, dl), from2d(n, mm), from2d(n, vn)

    grad_x = dx0.reshape(nb, s, d)
    return (loss, grad_x, *[grads[n] for n in names], *[delta[n] for n in names],
            *[new_m[n] for n in names], *[new_v[n] for n in names])


def sum_rows(a):
    r, c = a.shape

    def body(a_ref, o_ref):
        acc = a_ref[0:1, :]
        for j in range(1, r):
            acc = acc + a_ref[j:j + 1, :]
        o_ref[...] = acc

    return pl.pallas_call(body, name="sum_rows", out_shape=jax.ShapeDtypeStruct((1, c), F32),
                          compiler_params=_params())(a)
```

```python
import functools
import math

import jax
import jax.numpy as jnp
from jax import lax
from jax.experimental import pallas as pl
from jax.experimental.pallas import tpu as pltpu

F32 = jnp.float32
BF16 = jnp.bfloat16
N_DEV = 8
N_MOD = 9
EPS = 1e-6
CONV_K = 3
ADAM_LR = 0.001
ADAM_B1 = 0.9
ADAM_B2 = 0.999
ADAM_EPS = 1e-08
ADAM_WD = 0.01
ADAM_STEP = 10
GELU_C0 = math.sqrt(2.0 / math.pi)
GELU_C1 = 0.044715
V7X_VMEM_LIMIT = 56 * 1024 * 1024
MESH_ID = pl.DeviceIdType.MESH
NT = (((1,), (1,)), ((), ()))
TN = (((0,), (0,)), ((), ()))


def _dot(a, b, dims=None):
    if dims is None:
        return jnp.dot(a, b, preferred_element_type=F32)
    return lax.dot_general(a, b, dims, preferred_element_type=F32)


def _params(sem=None, vmem=V7X_VMEM_LIMIT):
    return pltpu.CompilerParams(dimension_semantics=sem, vmem_limit_bytes=vmem)


def _full(shape):
    return pl.BlockSpec(shape, lambda *_: (0,) * len(shape))


def _const(shape):
    return pl.BlockSpec(shape, lambda *_: (0,) * len(shape), pipeline_mode=pl.Buffered(1))


def _tile(n, want):
    t = min(n, want)
    while n % t:
        t //= 2
    return t


class GatherStage:
    COPIES = 9

    def __init__(self, shards):
        n = len(shards)
        self.inputs = list(shards)
        self.out_shape = [jax.ShapeDtypeStruct((N_DEV * s.shape[0], s.shape[1]), s.dtype) for s in shards]
        self.scratch = [pltpu.SemaphoreType.DMA((self.COPIES * n,)), pltpu.SemaphoreType.DMA((self.COPIES * n,)),
                        pltpu.SemaphoreType.DMA((n,))]

    def _plan(self, ins, outs, sems):
        send_sems, recv_sems, local_sems = sems
        n = len(ins)
        x, y, c = lax.axis_index("x"), lax.axis_index("y"), lax.axis_index("c")
        me, sibling, xn, yn, dg = (x, y, c), (x, y, 1 - c), (1 - x, y, c), (x, 1 - y, c), (1 - x, 1 - y, c)

        def rows(k, block, half=None):
            r = ins[k].shape[0]
            px, py, pc = block
            base = (4 * px + 2 * py + pc) * r
            if half is None:
                return outs[k].at[pl.ds(base, r), :]
            return outs[k].at[pl.ds(base + half * (r // 2), r // 2), :]

        def copy(k, j, block, to, half=None, src=None):
            return pltpu.make_async_remote_copy(
                src_ref=rows(k, block, half) if src is None else src, dst_ref=rows(k, block, half),
                send_sem=send_sems.at[self.COPIES * k + j], recv_sem=recv_sems.at[self.COPIES * k + j],
                device_id=to, device_id_type=MESH_ID)

        sib = lambda b: (b[0], b[1], 1 - b[2])
        mine = [pltpu.make_async_copy(ins[k], rows(k, me), local_sems.at[k]) for k in range(n)]
        first = [(0, me, sibling, None, sibling), (1, me, xn, None, xn), (2, me, yn, None, yn)]
        second = [(3, xn, yn, 0, dg), (4, yn, xn, 1, dg), (5, xn, sibling, None, sib(xn)), (6, yn, sibling, None, sib(yn))]
        third = [(7, dg, sibling, 0, sib(dg)), (8, dg, sibling, 1, sib(dg))]
        return n, me, copy, mine, first, second, third

    def start(self, ins, outs, sems):
        n, me, copy, mine, first, _, _ = self._plan(ins, outs, sems)
        for cp in mine:
            cp.start()
        for k in range(n):
            for j, block, to, half, _ in first:
                copy(k, j, block, to, half, src=ins[k]).start()

    def advance(self, ins, outs, sems):
        n, me, copy, mine, first, second, third = self._plan(ins, outs, sems)
        for k in range(n):
            copy(k, 1, first[1][4], me).wait_recv()
            copy(k, 2, first[2][4], me).wait_recv()
            for j, block, to, half, _ in second:
                copy(k, j, block, to, half).start()

    def advance_again(self, ins, outs, sems):
        n, me, copy, mine, first, second, third = self._plan(ins, outs, sems)
        for k in range(n):
            copy(k, 3, second[0][4], me, 0).wait_recv()
            copy(k, 4, second[1][4], me, 1).wait_recv()
            for j, block, to, half, _ in third:
                copy(k, j, block, to, half).start()

    def finish(self, ins, outs, sems):
        n, me, copy, mine, first, second, third = self._plan(ins, outs, sems)
        arrived = lambda k, j, block, half: copy(k, j, block, me, half).wait_recv()
        for k in range(n):
            arrived(k, 0, first[0][4], None)
            arrived(k, 5, second[2][4], None)
            arrived(k, 6, second[3][4], None)
            arrived(k, 7, third[0][4], 0)
            arrived(k, 8, third[1][4], 1)
        for k in range(n):
            for j, block, to, half, _ in first:
                copy(k, j, block, to, half, src=ins[k]).wait_send()
            for j, block, to, half, _ in second + third:
                copy(k, j, block, to, half).wait_send()
        for cp in mine:
            cp.wait()


class ExchangeStage:
    def __init__(self, bufs):
        n = len(bufs)
        self.inputs = list(bufs)
        self.out_shape = [jax.ShapeDtypeStruct(b.shape, b.dtype) for b in bufs]
        self.scratch = [pltpu.SemaphoreType.DMA((7 * n,)), pltpu.SemaphoreType.DMA((7 * n,)),
                        pltpu.SemaphoreType.DMA((n,))]

    def _plan(self, ins, outs, sems):
        send_sems, recv_sems, local_sems = sems
        n = len(ins)
        x, y, c = lax.axis_index("x"), lax.axis_index("y"), lax.axis_index("c")
        me = 4 * x + 2 * y + c
        mine = [pltpu.make_async_copy(ins[k].at[me], outs[k].at[me], local_sems.at[k]) for k in range(n)]
        copies = []
        for mask in range(1, N_DEV):
            px, py, pc = x ^ (mask >> 2), y ^ ((mask >> 1) & 1), c ^ (mask & 1)
            for k in range(n):
                copies.append(pltpu.make_async_remote_copy(
                    src_ref=ins[k].at[4 * px + 2 * py + pc], dst_ref=outs[k].at[me],
                    send_sem=send_sems.at[7 * k + mask - 1], recv_sem=recv_sems.at[7 * k + mask - 1],
                    device_id=(px, py, pc), device_id_type=MESH_ID))
        return mine, copies

    def start(self, ins, outs, sems):
        mine, copies = self._plan(ins, outs, sems)
        for cp in mine + copies:
            cp.start()

    def advance(self, ins, outs, sems):
        pass

    advance_again = advance

    def finish(self, ins, outs, sems):
        mine, copies = self._plan(ins, outs, sems)
        for cp in copies:
            cp.wait_recv()
        for cp in copies:
            cp.wait_send()
        for cp in mine:
            cp.wait()


ANY_SPEC = pl.BlockSpec(memory_space=pl.ANY)
GATHER_ROWS = 16


class StageGroup:
    def __init__(self, stages):
        self.stages = list(stages)
        self.inputs = [a for s in stages for a in s.inputs]
        self.out_shape = [o for s in stages for o in s.out_shape]
        self.scratch = [t for s in stages for t in s.scratch]

    def _parts(self, ins, outs, sems):
        i = o = t = 0
        for s in self.stages:
            ni, no, nt = len(s.inputs), len(s.out_shape), len(s.scratch)
            yield s, ins[i:i + ni], outs[o:o + no], sems[t:t + nt]
            i, o, t = i + ni, o + no, t + nt

    def start(self, ins, outs, sems):
        for s, i_, o_, t_ in self._parts(ins, outs, sems):
            s.start(i_, o_, t_)

    def advance(self, ins, outs, sems):
        for s, i_, o_, t_ in self._parts(ins, outs, sems):
            s.advance(i_, o_, t_)

    def advance_again(self, ins, outs, sems):
        for s, i_, o_, t_ in self._parts(ins, outs, sems):
            s.advance_again(i_, o_, t_)

    def finish(self, ins, outs, sems):
        for s, i_, o_, t_ in self._parts(ins, outs, sems):
            s.finish(i_, o_, t_)


def run_stage(stage, name):
    ci, co = len(stage.inputs), len(stage.out_shape)

    def body(*refs):
        ins, outs, sems = refs[:ci], refs[ci:ci + co], refs[ci + co:]
        stage.start(ins, outs, sems)
        stage.advance(ins, outs, sems)
        stage.advance_again(ins, outs, sems)
        stage.finish(ins, outs, sems)

    return pl.pallas_call(body, name=name, out_shape=stage.out_shape, in_specs=[ANY_SPEC] * ci,
                          out_specs=[ANY_SPEC] * co, scratch_shapes=stage.scratch)(*stage.inputs)


def _call(body, *, name, grid, in_specs, out_specs, out_shape, args, scratch_shapes=(), carried=None):
    sem = ("arbitrary",) * len(grid)
    if carried is None:
        return pl.pallas_call(body, name=name, grid=grid, in_specs=list(in_specs), out_specs=list(out_specs),
                              out_shape=list(out_shape), scratch_shapes=list(scratch_shapes),
                              compiler_params=_params(sem))(*args), None
    ni, no, ns = len(in_specs), len(out_shape), len(scratch_shapes)
    ci, co = len(carried.inputs), len(carried.out_shape)
    n_steps = math.prod(grid)

    def wrapped(*refs):
        ins, refs = refs[:ni], refs[ni:]
        cins, refs = refs[:ci], refs[ci:]
        outs, refs = refs[:no], refs[no:]
        couts, refs = refs[:co], refs[co:]
        scr, csems = refs[:ns], refs[ns:]
        step = functools.reduce(lambda acc, ig: acc * ig[1] + ig[0],
                                zip([pl.program_id(k) for k in range(len(grid))], grid), 0)

        @pl.when(step == 0)
        def _():
            carried.start(cins, couts, csems)

        @pl.when(step == n_steps // 2)
        def _():
            carried.advance(cins, couts, csems)

        @pl.when(step == (3 * n_steps) // 4)
        def _():
            carried.advance_again(cins, couts, csems)

        body(*ins, *outs, *scr)

        @pl.when(step == n_steps - 1)
        def _():
            carried.finish(cins, couts, csems)

    res = pl.pallas_call(
        wrapped, name=name, grid=grid, in_specs=list(in_specs) + [ANY_SPEC] * ci,
        out_specs=list(out_specs) + [ANY_SPEC] * co, out_shape=list(out_shape) + carried.out_shape,
        scratch_shapes=list(scratch_shapes) + carried.scratch, compiler_params=_params(sem),
    )(*args, *carried.inputs)
    return res[:no], res[no:]


def _norm_mod(x, g, shift, scale):
    r = lax.rsqrt(jnp.mean(x * x, axis=-1, keepdims=True) + EPS)
    n = x * r
    return (n * g) * (1.0 + scale) + shift, n, r


def _norm_mod_bwd(dh, n, r, g, scale):
    dsh = jnp.sum(dh, axis=0, keepdims=True)
    dsc = jnp.sum(dh * (n * g), axis=0, keepdims=True)
    dg = jnp.sum(dh * (1.0 + scale) * n, axis=0, keepdims=True)
    dn = dh * ((1.0 + scale) * g)
    dx = r * (dn - n * jnp.mean(n * dn, axis=-1, keepdims=True))
    return dx, dsh, dsc, dg


def _mod_rows(mod_ref, sub):
    m = mod_ref[0]
    return m[3 * sub:3 * sub + 1], m[3 * sub + 1:3 * sub + 2], m[3 * sub + 2:3 * sub + 3]


def _gelu(x):
    t = jnp.tanh(GELU_C0 * (x + GELU_C1 * x * x * x))
    return 0.5 * x * (1.0 + t), t


def _gelu_grad(x, t):
    return 0.5 * (1.0 + t) + 0.5 * x * (1.0 - t * t) * (GELU_C0 * (1.0 + 3.0 * GELU_C1 * x * x))


def _zero_when(cond, *refs):
    @pl.when(cond)
    def _():
        for r in refs:
            r[...] = jnp.zeros_like(r)


def ada_forward(c_all, w_ada, b_ada_cols):
    def body(c_ref, w_ref, b_ref, o_ref):
        c = c_ref[...]
        cond = (c * jax.nn.sigmoid(c)).astype(BF16)
        o_ref[...] = _dot(cond, w_ref[...].astype(BF16)) + b_ref[...]

    nb, d = c_all.shape
    cols = w_ada.shape[1]
    tn = _tile(cols, 384)
    return pl.pallas_call(
        body, name="ada_forward", grid=(cols // tn,),
        out_shape=jax.ShapeDtypeStruct((nb, cols), F32),
        in_specs=[_full((nb, d)), pl.BlockSpec((d, tn), lambda j: (0, j)), pl.BlockSpec((1, tn), lambda j: (0, j))],
        out_specs=pl.BlockSpec((nb, tn), lambda j: (0, j)),
        compiler_params=_params(("arbitrary",)),
    )(c_all, w_ada, b_ada_cols)


def _adamw(w, g, m, v):
    m = ADAM_B1 * m + (1.0 - ADAM_B1) * g
    v = ADAM_B2 * v + (1.0 - ADAM_B2) * (g * g)
    m_hat = m / (1.0 - ADAM_B1 ** ADAM_STEP)
    v_hat = v / (1.0 - ADAM_B2 ** ADAM_STEP)
    delta = -ADAM_LR * (m_hat / (jnp.sqrt(v_hat) + ADAM_EPS) + ADAM_WD * w)
    return delta, m, v


def ada_backward_update(c_all, gmod_cols, w, m, v):
    def body(c_ref, g_ref, w_ref, m_ref, v_ref, go_ref, d_ref, mo_ref, vo_ref):
        c = c_ref[...]
        cond = (c * jax.nn.sigmoid(c)).astype(BF16)
        g = _dot(cond, g_ref[...].astype(BF16), TN)
        go_ref[...] = g
        d_ref[...], mo_ref[...], vo_ref[...] = _adamw(w_ref[...], g, m_ref[...], v_ref[...])

    nb, d = c_all.shape
    cols = w.shape[1]
    tn = _tile(cols, 128)
    col = pl.BlockSpec((d, tn), lambda j: (0, j))
    return pl.pallas_call(
        body, name="ada_backward_update", grid=(cols // tn,),
        out_shape=[jax.ShapeDtypeStruct(w.shape, F32)] * 4,
        in_specs=[_full((nb, d)), pl.BlockSpec((nb, tn), lambda j: (0, j)), col, col, col],
        out_specs=[col] * 4,
        compiler_params=_params(("arbitrary",)),
    )(c_all, gmod_cols, w, m, v)


def _row_spec(tm, width, tiles_per_seq):
    return pl.BlockSpec((tm, width), lambda b, i: (b * tiles_per_seq + i, 0))


def _mod_spec(d):
    return pl.BlockSpec((1, N_MOD, d), lambda b, i: (b, 0, 0))


def _col_spec(rows, tm, tiles_per_seq):
    return pl.BlockSpec((rows, tm), lambda b, i: (0, b * tiles_per_seq + i))


def _ffn_chunk(f):
    return f // 4 if f % 512 == 0 and f > 1536 else f


def ffn_forward(x, mod, g, w1t, w3t, w2, sub, nb, name, carried=None):
    t, d = x.shape
    f = w1t.shape[0]
    s = t // nb
    tm = _tile(s, 512)
    fc = _ffn_chunk(f)

    def body(x_ref, mod_ref, g_ref, w1_ref, w3_ref, w2_ref, xo_ref, a_ref, b_ref, f_ref, h_ref):
        xv = x_ref[...]
        sh, sc, gt = _mod_rows(mod_ref, sub)
        h, _, _ = _norm_mod(xv, g_ref[...], sh, sc)
        hb = h.astype(BF16)
        h_ref[...] = hb
        acc_t = jnp.zeros((d, tm), F32)
        for k in range(f // fc):
            rows = slice(k * fc, (k + 1) * fc)
            a = _dot(w1_ref[rows, :], hb, NT)
            b = _dot(w3_ref[rows, :], hb, NT)
            a_ref[rows, :] = a.astype(BF16)
            b_ref[rows, :] = b.astype(BF16)
            sw = (a * jax.nn.sigmoid(a)) * b
            acc_t = acc_t + _dot(w2_ref[rows, :], sw.astype(BF16), TN)
        acc = acc_t.T
        f_ref[...] = acc.astype(BF16)
        xo_ref[...] = xv + (0.5 * gt) * acc

    tps = s // tm
    rd, cf = _row_spec(tm, d, tps), _col_spec(f, tm, tps)
    return _call(
        body, name=name, grid=(nb, tps), carried=carried,
        out_shape=[jax.ShapeDtypeStruct((t, d), F32)] + [jax.ShapeDtypeStruct((f, t), BF16)] * 2
        + [jax.ShapeDtypeStruct((t, d), BF16)] * 2,
        in_specs=[rd, _mod_spec(d), _const((1, d)), _const((f, d)), _const((f, d)), _const((f, d))],
        out_specs=[rd, cf, cf, rd, rd],
        args=(x, mod, g, w1t, w3t, w2))


def ffn_backward_hidden(dxo, a_t, b_t, fo, mod, w2, sub, nb, name, carried=None):
    t, d = dxo.shape
    f = w2.shape[0]
    s = t // nb
    tm = _tile(s, 512)
    fc = _tile(f, 704) if f % 704 == 0 else _tile(f, 512)

    def body(dxo_ref, a_ref, b_ref, f_ref, mod_ref, w2_ref, da_ref, db_ref, s_ref, df_ref, dgt_ref):
        _zero_when(pl.program_id(1) == 0, dgt_ref)
        dxo = dxo_ref[...]
        _, _, gt = _mod_rows(mod_ref, sub)
        dfb = ((0.5 * gt) * dxo).astype(BF16)
        df_ref[...] = dfb
        dgt_ref[...] += 0.5 * jnp.sum(dxo * f_ref[...].astype(F32), axis=0, keepdims=True)[None]
        for k in range(f // fc):
            rows = slice(k * fc, (k + 1) * fc)
            ds = _dot(w2_ref[rows, :], dfb, NT).astype(BF16)
            av = a_ref[rows, :].astype(F32)
            bv = b_ref[rows, :]
            sig = jax.nn.sigmoid(av)
            sl = av * sig
            slb = sl.astype(BF16)
            da_ref[rows, :] = ds * bv * (sig + sl * (1.0 - sig)).astype(BF16)
            db_ref[rows, :] = ds * slb
            s_ref[rows, :] = slb * bv

    tps = s // tm
    rd, cf = _row_spec(tm, d, tps), _col_spec(f, tm, tps)
    return _call(
        body, name=name, grid=(nb, tps), carried=carried,
        out_shape=[jax.ShapeDtypeStruct((f, t), BF16)] * 3
        + [jax.ShapeDtypeStruct((t, d), BF16), jax.ShapeDtypeStruct((nb, 1, d), F32)],
        in_specs=[rd, cf, cf, rd, _mod_spec(d), _const((f, d))],
        out_specs=[cf, cf, cf, rd, pl.BlockSpec((1, 1, d), lambda b, i: (b, 0, 0))],
        args=(dxo, a_t, b_t, fo, mod, w2))


def ffn_backward_input(dxo, x, da_t, db_t, mod, g, w1t, w3t, sub, nb, name, carried=None):
    t, d = x.shape
    f = w1t.shape[0]
    s = t // nb
    tm = _tile(s, 512)

    def body(dxo_ref, x_ref, da_ref, db_ref, mod_ref, g_ref, w1_ref, w3_ref, dx_ref, dmod_ref, dg_ref):
        bi, i = pl.program_id(0), pl.program_id(1)
        _zero_when(i == 0, dmod_ref)
        _zero_when(jnp.logical_and(bi == 0, i == 0), dg_ref)
        sh, sc, _ = _mod_rows(mod_ref, sub)
        gv = g_ref[...]
        _, n, r = _norm_mod(x_ref[...], gv, sh, sc)
        dh_t = _dot(w1_ref[...], da_ref[...], TN) + _dot(w3_ref[...], db_ref[...], TN)
        dxn, dsh, dsc, dg = _norm_mod_bwd(dh_t.T, n, r, gv, sc)
        dx_ref[...] = dxo_ref[...] + dxn
        dmod_ref[...] += jnp.concatenate([dsh, dsc], axis=0)[None]
        dg_ref[...] += dg

    tps = s // tm
    rd, cf = _row_spec(tm, d, tps), _col_spec(f, tm, tps)
    return _call(
        body, name=name, grid=(nb, tps), carried=carried,
        out_shape=[jax.ShapeDtypeStruct((t, d), F32), jax.ShapeDtypeStruct((nb, 2, d), F32),
                   jax.ShapeDtypeStruct((1, d), F32)],
        in_specs=[rd, rd, cf, cf, _mod_spec(d), _const((1, d)), _const((f, d)), _const((f, d))],
        out_specs=[rd, pl.BlockSpec((1, 2, d), lambda b, i: (b, 0, 0)), _full((1, d))],
        args=(dxo, x, da_t, db_t, mod, g, w1t, w3t))


def nn_matmul(lhs_t, rhs, name, carried=None):
    m, t = lhs_t.shape
    n = rhs.shape[1]
    tk = _tile(t, 2048)
    tmm = m if m <= 1536 else m // 2
    nk = t // tk

    def body(a_ref, b_ref, o_ref, acc_ref):
        k = pl.program_id(1)
        _zero_when(k == 0, acc_ref)
        acc_ref[...] += _dot(a_ref[...], b_ref[...])

        @pl.when(k == nk - 1)
        def _():
            o_ref[...] = acc_ref[...].astype(BF16)

    return _call(
        body, name=name, grid=(m // tmm, nk), carried=carried,
        out_shape=[jax.ShapeDtypeStruct((m, n), BF16)],
        in_specs=[pl.BlockSpec((tmm, tk), lambda j, k: (j, k)), pl.BlockSpec((tk, n), lambda j, k: (k, 0))],
        out_specs=[pl.BlockSpec((tmm, n), lambda j, k: (j, 0))],
        scratch_shapes=[pltpu.VMEM((tmm, n), F32)],
        args=(lhs_t, rhs))


def tn_matmul(lhs, rhs, name):
    t, m = lhs.shape
    n = rhs.shape[1]
    tk = _tile(t, 2048)
    nk = t // tk

    def body(a_ref, b_ref, o_ref, acc_ref):
        k = pl.program_id(0)
        _zero_when(k == 0, acc_ref)
        acc_ref[...] += _dot(a_ref[...], b_ref[...], TN)

        @pl.when(k == nk - 1)
        def _():
            o_ref[...] = acc_ref[...].astype(BF16)

    return pl.pallas_call(
        body, name=name, grid=(nk,),
        out_shape=jax.ShapeDtypeStruct((m, n), BF16),
        in_specs=[pl.BlockSpec((tk, m), lambda k: (k, 0)), pl.BlockSpec((tk, n), lambda k: (k, 0))],
        out_specs=pl.BlockSpec((m, n), lambda k: (0, 0)),
        scratch_shapes=[pltpu.VMEM((m, n), F32)],
        compiler_params=_params(("arbitrary",)),
    )(lhs, rhs)


def mixer_proj_forward(x, mod, g, w_in_t, cw, sw, nb, carried=None):
    t, d = x.shape
    s = t // nb
    tm = _tile(s, 512)
    pieces = [(0, cw, "bf16"), (cw, cw, "bf16"), (2 * cw, cw, "bf16"), (3 * cw, sw, "f32"),
              (3 * cw + sw, d, "sig"), (3 * cw + sw + d, d, "sig")]

    def body(x_ref, mod_ref, g_ref, w_ref, *outs):
        h_ref = outs[-1]
        sh, sc, _ = _mod_rows(mod_ref, 1)
        h, _, _ = _norm_mod(x_ref[...], g_ref[...], sh, sc)
        hb = h.astype(BF16)
        h_ref[...] = hb
        for (off, width, kind), o_ref in zip(pieces, outs[:-1]):
            ck = _tile(width, 512)
            for j in range(width // ck):
                p = _dot(hb, w_ref[off + j * ck:off + (j + 1) * ck, :], NT)
                if kind == "sig":
                    p = jax.nn.sigmoid(p)
                o_ref[:, j * ck:(j + 1) * ck] = p.astype(o_ref.dtype)

    tps = s // tm
    widths = [(cw, BF16), (cw, BF16), (cw, BF16), (sw, F32), (d, BF16), (d, BF16), (d, BF16)]
    return _call(
        body, name="mixer_proj_forward", grid=(nb, tps), carried=carried,
        out_shape=[jax.ShapeDtypeStruct((t, w), dt) for w, dt in widths],
        in_specs=[_row_spec(tm, d, tps), _mod_spec(d), _const((1, d)), _const(w_in_t.shape)],
        out_specs=[_row_spec(tm, w, tps) for w, _ in widths],
        args=(x, mod, g, w_in_t))


def mixer_proj_backward(dgb, dgc, dv, du, dgla, dglb, dxo, x, mod, g, w_in_t, nb, carried=None):
    t, d = x.shape
    s = t // nb
    tm = _tile(s, 512)
    parts = [dgb, dgc, dv, du, dgla, dglb]
    offs = [0]
    for p in parts:
        offs.append(offs[-1] + p.shape[1])

    def body(*refs):
        p_refs = refs[:6]
        dxo_ref, x_ref, mod_ref, g_ref, w_ref, dx_ref, dmod_ref, dg_ref = refs[6:]
        bi, i = pl.program_id(0), pl.program_id(1)
        _zero_when(i == 0, dmod_ref)
        _zero_when(jnp.logical_and(bi == 0, i == 0), dg_ref)
        dh = jnp.zeros((tm, d), F32)
        for p_ref, off in zip(p_refs, offs):
            dh = dh + _dot(p_ref[...], w_ref[off:off + p_ref.shape[1], :])
        sh, sc, _ = _mod_rows(mod_ref, 1)
        gv = g_ref[...]
        _, n, r = _norm_mod(x_ref[...], gv, sh, sc)
        dxn, dsh, dsc, dg = _norm_mod_bwd(dh, n, r, gv, sc)
        dx_ref[...] = dxo_ref[...] + dxn
        dmod_ref[...] += jnp.concatenate([dsh, dsc], axis=0)[None]
        dg_ref[...] += dg

    tps = s // tm
    rd = _row_spec(tm, d, tps)
    return _call(
        body, name="mixer_proj_backward", grid=(nb, tps), carried=carried,
        out_shape=[jax.ShapeDtypeStruct((t, d), F32), jax.ShapeDtypeStruct((nb, 2, d), F32),
                   jax.ShapeDtypeStruct((1, d), F32)],
        in_specs=[_row_spec(tm, p.shape[1], tps) for p in parts]
        + [rd, rd, _mod_spec(d), _const((1, d)), _const(w_in_t.shape)],
        out_specs=[rd, pl.BlockSpec((1, 2, d), lambda b, i: (b, 0, 0)), _full((1, d))],
        args=(*parts, dxo, x, mod, g, w_in_t))


GROUPS_PER_BLOCK = 8
ROWS = 8
SSM_CHUNK = 512
SCAN_LANES = 512
SCAN_UNROLL = 8


def _scan_rows(xr, xi, masks, shifts):
    for (mr, mi), sft in zip(masks, shifts):
        sr, si = pltpu.roll(xr, sft, 0), pltpu.roll(xi, sft, 0)
        xr, xi = xr + mr * sr - mi * si, xi + mr * si + mi * sr
    return xr, xi


def _cmul_add(ar, ai, cr, ci, br, bi):
    return ar * cr - ai * ci + br, ar * ci + ai * cr + bi


def _segment_rows(perm_ref, x):
    return _dot(perm_ref[0], x).astype(BF16)


def _time_rows(perm_ref, x):
    hi = x.astype(BF16)
    lo = (x - hi.astype(F32)).astype(BF16)
    return _dot(perm_ref[1], hi) + _dot(perm_ref[1], lo)


def segment_permutation(tc):
    r = jnp.arange(tc)
    p = (r[:, None] % ROWS * (tc // ROWS) + r[:, None] // ROWS == r[None, :]).astype(BF16)
    return jnp.stack([p, p.T])


def _scan_loop(n, step, init):
    def trip(j, carry):
        for r in range(SCAN_UNROLL):
            carry = step(j * SCAN_UNROLL + r, carry)
        return carry

    return lax.fori_loop(0, n // SCAN_UNROLL, trip, init)


def _rows_at(k, offset=0):
    return pl.ds(pl.multiple_of(k * ROWS + offset, ROWS), ROWS)


def ssm_forward(u, bd, cd, a1, ml, nb, carried=None):
    t, w = u.shape
    s = t // nb
    tc = _tile(s, SSM_CHUNK)
    seg = tc // ROWS
    nq, ub, lq = bd.shape[1], bd.shape[2], bd.shape[3]
    nl = nq * lq
    nch = s // tc
    lw = min(nl, SCAN_LANES)

    def body(u_ref, perm_ref, bd_ref, cd_ref, a1_ref, ml_ref, y_ref, st_ref, xr_s, xi_s, car_s):
        i = pl.program_id(1)

        @pl.when(i == 0)
        def _():
            car_s[...] = jnp.zeros_like(car_s)

        ub16 = _segment_rows(perm_ref, u_ref[...].astype(BF16))
        for q in range(nq):
            lanes = slice(q * lq, (q + 1) * lq)
            uq = ub16[:, q * ub:(q + 1) * ub]
            xr_s[:, lanes] = _dot(uq, bd_ref[0, q])
            xi_s[:, lanes] = _dot(uq, bd_ref[1, q])
        row_is_0 = lax.broadcasted_iota(jnp.int32, (ROWS, lw), 0) == 0
        zero = jnp.zeros((ROWS, lw), F32)
        for j in range(nl // lw):
            lanes = slice(j * lw, (j + 1) * lw)
            ar, ai = a1_ref[0, :, lanes], a1_ref[1, :, lanes]

            def local(k, c):
                return _cmul_add(ar, ai, c[0], c[1], xr_s[_rows_at(k), lanes], xi_s[_rows_at(k), lanes])

            er, ei = _scan_loop(seg, local, (zero, zero))
            masks = [(ml_ref[d, 0, :, lanes], ml_ref[d, 1, :, lanes]) for d in range(3)]
            cr, ci = _scan_rows(jnp.where(row_is_0, car_s[0, :, lanes], pltpu.roll(er, 1, 0)),
                                jnp.where(row_is_0, car_s[1, :, lanes], pltpu.roll(ei, 1, 0)), masks, (1, 2, 4))
            st_ref[0, 0, :, lanes] = cr
            st_ref[0, 1, :, lanes] = ci

            def full(k, c):
                xr, xi = _cmul_add(ar, ai, c[0], c[1], xr_s[_rows_at(k), lanes], xi_s[_rows_at(k), lanes])
                xr_s[_rows_at(k), lanes] = xr
                xi_s[_rows_at(k), lanes] = xi
                return xr, xi

            fr, fi = _scan_loop(seg, full, (cr, ci))
            car_s[0, :, lanes] = jnp.broadcast_to(fr[ROWS - 1:ROWS], fr.shape)
            car_s[1, :, lanes] = jnp.broadcast_to(fi[ROWS - 1:ROWS], fi.shape)
        y = jnp.concatenate([_dot(xr_s[:, q * lq:(q + 1) * lq].astype(BF16), cd_ref[0, q])
                             + _dot(xi_s[:, q * lq:(q + 1) * lq].astype(BF16), cd_ref[1, q]) for q in range(nq)],
                            axis=1)
        y_ref[...] = _time_rows(perm_ref, y)

    perm = segment_permutation(tc)
    return _call(
        body, name="ssm_forward", grid=(nb, nch), carried=carried,
        out_shape=[jax.ShapeDtypeStruct((t, w), F32), jax.ShapeDtypeStruct((nb * nch, 2, ROWS, nl), F32)],
        in_specs=[pl.BlockSpec((tc, w), lambda b, i: (b * nch + i, 0)), _const(perm.shape), _const(bd.shape),
                  _const(cd.shape), _const(a1.shape), _const(ml.shape)],
        out_specs=[pl.BlockSpec((tc, w), lambda b, i: (b * nch + i, 0)),
                   pl.BlockSpec((1, 2, ROWS, nl), lambda b, i: (b * nch + i, 0, 0, 0))],
        scratch_shapes=[pltpu.VMEM((tc, nl), F32), pltpu.VMEM((tc, nl), F32), pltpu.VMEM((2, ROWS, nl), F32)],
        args=(u, perm, bd, cd, a1, ml))


def ssm_backward(u, dy, st, bd, cd, a1, mlb, dskip, nb, carried=None):
    t, w = u.shape
    s = t // nb
    tc = _tile(s, SSM_CHUNK)
    seg = tc // ROWS
    nq, ub, lq = bd.shape[1], bd.shape[2], bd.shape[3]
    nl = nq * lq
    nch = s // tc
    lw = min(nl, SCAN_LANES)

    def body(u_ref, dy_ref, st_ref, perm_ref, bd_ref, cd_ref, a1_ref, mlb_ref, dsk_ref,
             du_ref, dab_ref, dbd_ref, dcd_ref, xr_s, xi_s, lr_s, li_s, car_s):
        bi, i = pl.program_id(0), pl.program_id(1)
        first = jnp.logical_and(bi == 0, i == 0)

        @pl.when(i == 0)
        def _():
            car_s[...] = jnp.zeros_like(car_s)

        @pl.when(first)
        def _():
            dab_ref[...] = jnp.zeros_like(dab_ref)
            dbd_ref[...] = jnp.zeros_like(dbd_ref)
            dcd_ref[...] = jnp.zeros_like(dcd_ref)

        ub16 = _segment_rows(perm_ref, u_ref[...].astype(BF16))
        dyb16 = _segment_rows(perm_ref, dy_ref[...].astype(BF16))
        xr_s[0:ROWS, :] = st_ref[0, 0]
        xi_s[0:ROWS, :] = st_ref[0, 1]
        for q in range(nq):
            lanes = slice(q * lq, (q + 1) * lq)
            uq = ub16[:, q * ub:(q + 1) * ub]
            dq = dyb16[:, q * ub:(q + 1) * ub]
            xr_s[ROWS:, lanes] = _dot(uq, bd_ref[0, q])
            xi_s[ROWS:, lanes] = _dot(uq, bd_ref[1, q])
            lr_s[:, lanes] = _dot(dq, cd_ref[0, q], NT)
            li_s[:, lanes] = _dot(dq, cd_ref[1, q], NT)
        row_is_7 = lax.broadcasted_iota(jnp.int32, (ROWS, lw), 0) == ROWS - 1
        zero = jnp.zeros((ROWS, lw), F32)
        for j in range(nl // lw):
            lanes = slice(j * lw, (j + 1) * lw)
            ar, ai = a1_ref[0, :, lanes], a1_ref[1, :, lanes]
            nai = -ai

            def states(k, c):
                xr, xi = _cmul_add(ar, ai, c[0], c[1], xr_s[_rows_at(k, ROWS), lanes], xi_s[_rows_at(k, ROWS), lanes])
                xr_s[_rows_at(k, ROWS), lanes] = xr
                xi_s[_rows_at(k, ROWS), lanes] = xi
                return xr, xi

            _scan_loop(seg, states, (st_ref[0, 0, :, lanes], st_ref[0, 1, :, lanes]))

            def local(kk, c):
                k = seg - 1 - kk
                return _cmul_add(ar, nai, c[0], c[1], lr_s[_rows_at(k), lanes], li_s[_rows_at(k), lanes])

            er, ei = _scan_loop(seg, local, (zero, zero))
            masks = [(mlb_ref[d, 0, :, lanes], mlb_ref[d, 1, :, lanes]) for d in range(3)]
            cr, ci = _scan_rows(jnp.where(row_is_7, car_s[0, :, lanes], pltpu.roll(er, ROWS - 1, 0)),
                                jnp.where(row_is_7, car_s[1, :, lanes], pltpu.roll(ei, ROWS - 1, 0)), masks, (7, 6, 4))

            def full(kk, c):
                cr_, ci_, accr, acci = c
                k = seg - 1 - kk
                lr, li = _cmul_add(ar, nai, cr_, ci_, lr_s[_rows_at(k), lanes], li_s[_rows_at(k), lanes])
                lr_s[_rows_at(k), lanes] = lr
                li_s[_rows_at(k), lanes] = li
                xpr, xpi = xr_s[_rows_at(k), lanes], xi_s[_rows_at(k), lanes]
                return lr, li, accr + lr * xpr + li * xpi, acci + li * xpr - lr * xpi

            lr0, li0, accr, acci = _scan_loop(seg, full, (cr, ci, zero, zero))
            car_s[0, :, lanes] = jnp.broadcast_to(lr0[0:1], lr0.shape)
            car_s[1, :, lanes] = jnp.broadcast_to(li0[0:1], li0.shape)
            dab_ref[0, :, lanes] += accr
            dab_ref[1, :, lanes] += acci
        du_parts = []
        for q in range(nq):
            lanes = slice(q * lq, (q + 1) * lq)
            cols = slice(q * ub, (q + 1) * ub)
            lrb, lib = lr_s[:, lanes].astype(BF16), li_s[:, lanes].astype(BF16)
            uq, dq = ub16[:, cols], dyb16[:, cols]
            du_parts.append(_dot(lrb, bd_ref[0, q], NT) + _dot(lib, bd_ref[1, q], NT))
            dbd_ref[0, q] += _dot(uq, lrb, TN)
            dbd_ref[1, q] += _dot(uq, lib, TN)
            dcd_ref[0, q] += _dot(xr_s[ROWS:, lanes].astype(BF16), dq, TN)
            dcd_ref[1, q] += _dot(xi_s[ROWS:, lanes].astype(BF16), dq, TN)
        du = _time_rows(perm_ref, jnp.concatenate(du_parts, axis=1)) + dsk_ref[...] * dy_ref[...]
        du_ref[...] = du.astype(BF16)

    rev = lambda b, i: (b * nch + nch - 1 - i, 0)
    perm = segment_permutation(tc)
    return _call(
        body, name="ssm_backward", grid=(nb, nch), carried=carried,
        out_shape=[jax.ShapeDtypeStruct((t, w), BF16), jax.ShapeDtypeStruct((2, ROWS, nl), F32),
                   jax.ShapeDtypeStruct(bd.shape, F32), jax.ShapeDtypeStruct(cd.shape, F32)],
        in_specs=[pl.BlockSpec((tc, w), rev), pl.BlockSpec((tc, w), rev),
                  pl.BlockSpec((1, 2, ROWS, nl), lambda b, i: (b * nch + nch - 1 - i, 0, 0, 0)),
                  _const(perm.shape), _const(bd.shape), _const(cd.shape), _const(a1.shape), _const(mlb.shape),
                  _const((1, w))],
        out_specs=[pl.BlockSpec((tc, w), rev), _full((2, ROWS, nl)), _full(bd.shape), _full(cd.shape)],
        scratch_shapes=[pltpu.VMEM((tc + ROWS, nl), F32), pltpu.VMEM((tc + ROWS, nl), F32),
                        pltpu.VMEM((tc, nl), F32), pltpu.VMEM((tc, nl), F32), pltpu.VMEM((2, ROWS, nl), F32)],
        args=(u, dy, st, perm, bd, cd, a1, mlb, dskip))


def ssm_discretise(a_re, a_im, b_re, b_im, log_dt):
    dt = jnp.exp(log_dt)[:, None]
    er = jnp.exp(a_re * dt)
    abr, abi = er * jnp.cos(a_im * dt), er * jnp.sin(a_im * dt)
    den = a_re * a_re + a_im * a_im
    nr, ni = abr - 1.0, abi
    fr = ((nr * a_re + ni * a_im) / den)[..., None]
    fi = ((ni * a_re - nr * a_im) / den)[..., None]
    return abr, abi, fr * b_re - fi * b_im, fr * b_im + fi * b_re


def _complex_square(zr, zi):
    return zr * zr - zi * zi, 2.0 * zr * zi


def ssm_tables(abr, abi, bbr, bbi, c_re, c_im, seg):
    g, p, h = bbr.shape
    nq = g // GROUPS_PER_BLOCK
    zr, zi = abr.reshape(1, -1), abi.reshape(1, -1)
    a1 = jnp.stack([jnp.broadcast_to(zr, (ROWS, g * p)), jnp.broadcast_to(zi, (ROWS, g * p))])
    for _ in range(seg.bit_length() - 1):
        zr, zi = _complex_square(zr, zi)
    row = jnp.arange(ROWS)[:, None]
    ml, mlb = [], []
    for d in (1, 2, 4):
        ml.append(jnp.stack([jnp.where(row >= d, zr, 0.0), jnp.where(row >= d, zi, 0.0)]))
        mlb.append(jnp.stack([jnp.where(row + d < ROWS, zr, 0.0), jnp.where(row + d < ROWS, -zi, 0.0)]))
        zr, zi = _complex_square(zr, zi)
    eye = jnp.eye(GROUPS_PER_BLOCK, dtype=F32)

    def block_diag_in(bb):
        bq = bb.reshape(nq, GROUPS_PER_BLOCK, p, h)
        return jnp.einsum("qaph,ab->qahbp", bq, eye).reshape(nq, GROUPS_PER_BLOCK * h, GROUPS_PER_BLOCK * p)

    def block_diag_out(cc):
        cq = cc.reshape(nq, GROUPS_PER_BLOCK, h, p)
        return jnp.einsum("qahp,ab->qapbh", cq, eye).reshape(nq, GROUPS_PER_BLOCK * p, GROUPS_PER_BLOCK * h)

    bd = jnp.stack([block_diag_in(bbr), block_diag_in(bbi)]).astype(BF16)
    cd = jnp.stack([block_diag_out(c_re), block_diag_out(-c_im)]).astype(BF16)
    return bd, cd, a1, jnp.stack(ml), jnp.stack(mlb)


def ssm_table_grads(dab, dbd, dcd, g, p, h):
    nq = g // GROUPS_PER_BLOCK
    dabr, dabi = dab[0].sum(0).reshape(g, p), dab[1].sum(0).reshape(g, p)
    b5 = dbd.reshape(2, nq, GROUPS_PER_BLOCK, h, GROUPS_PER_BLOCK, p)
    dbb = jnp.einsum("rqahap->rqaph", b5).reshape(2, g, p, h)
    c5 = dcd.reshape(2, nq, GROUPS_PER_BLOCK, p, GROUPS_PER_BLOCK, h)
    dcc = jnp.einsum("rqapah->rqahp", c5).reshape(2, g, h, p)
    return dabr, dabi, dbb[0], dbb[1], dcc[0], -dcc[1]


HALO = 16


def _conv_inputs(gc_ref, v_ref, gch_ref, vh_ref, cv_s, i, tm):
    cv = gc_ref[...].astype(F32) * v_ref[...].astype(F32)
    halo = gch_ref[...].astype(F32) * vh_ref[...].astype(F32)
    cv_s[0:HALO, :] = jnp.where(i == 0, 0.0, halo)
    cv_s[HALO:, :] = cv
    return cv, cv_s[HALO - 1:HALO - 1 + tm, :], cv_s[HALO - 2:HALO - 2 + tm, :]


def _halo_spec(tm, width, tiles_per_seq):
    per = tm // HALO
    return pl.BlockSpec((HALO, width), lambda b, i: (jnp.maximum((b * tiles_per_seq + i) * per - 1, 0), 0))


def mixer_merge_forward(x, gb, gc, v, sga, sgb, yssm, u, mod, conv_w, dskip, wco, wglu, wso_t, wout, nb):
    t, d = x.shape
    cw, sw = gb.shape[1], u.shape[1]
    s = t // nb
    tm = _tile(s, 512)
    tps = s // tm

    def body(x_ref, gb_ref, gc_ref, v_ref, gch_ref, vh_ref, sga_ref, sgb_ref, ys_ref, u_ref, mod_ref, cw_ref,
             dsk_ref, wco_ref, wglu_ref, wso_ref, wout_ref, xo_ref, ya_ref, yb_ref, mix_ref, cv_s):
        i = pl.program_id(1)
        cv, cv1, cv2 = _conv_inputs(gc_ref, v_ref, gch_ref, vh_ref, cv_s, i, tm)
        w = cw_ref[...]
        conv = w[0:1] * cv2 + w[1:2] * cv1 + w[2:3] * cv
        ya = _dot((gb_ref[...].astype(F32) * conv).astype(BF16), wco_ref[...])
        s0 = ys_ref[...] + dsk_ref[...] * u_ref[...]
        s1, _ = _gelu(s0)
        z = _dot(s1.astype(BF16), wglu_ref[...])
        s2 = s1 * jax.nn.sigmoid(z)
        yb = _dot(s2.astype(BF16), wso_ref[...], NT)
        merged = sga_ref[...].astype(F32) * ya + sgb_ref[...].astype(F32) * yb
        mix = _dot(merged.astype(BF16), wout_ref[...])
        _, _, gt = _mod_rows(mod_ref, 1)
        xo_ref[...] = x_ref[...] + gt * mix
        ya_ref[...] = ya.astype(BF16)
        yb_ref[...] = yb.astype(BF16)
        mix_ref[...] = mix.astype(BF16)

    rd, rc, rw = _row_spec(tm, d, tps), _row_spec(tm, cw, tps), _row_spec(tm, sw, tps)
    hc = _halo_spec(tm, cw, tps)
    return pl.pallas_call(
        body, name="mixer_merge_forward", grid=(nb, tps),
        out_shape=[jax.ShapeDtypeStruct((t, d), F32)] + [jax.ShapeDtypeStruct((t, d), BF16)] * 3,
        in_specs=[rd, rc, rc, rc, hc, hc, rd, rd, rw, rw, _mod_spec(d), _const(conv_w.shape), _const((1, sw)),
                  _const(wco.shape), _const(wglu.shape), _const(wso_t.shape), _const(wout.shape)],
        out_specs=[rd, rd, rd, rd],
        scratch_shapes=[pltpu.VMEM((tm + HALO, cw), F32)],
        compiler_params=_params(("arbitrary", "arbitrary")),
    )(x, gb, gc, v, gc, v, sga, sgb, yssm, u, mod, conv_w, dskip, wco, wglu, wso_t, wout)


def mixer_merge_backward(dxo, mix, ya, yb, gb, gc, v, sga, sgb, yssm, u, mod, conv_w, dskip,
                         wco, wglu, wso_t, wout, nb, carried=None):
    t, d = dxo.shape
    cw, sw = gb.shape[1], u.shape[1]
    s = t // nb
    tm = _tile(s, 256)
    tps = s // tm

    def body(dxo_ref, mix_ref, ya_ref, yb_ref, gb_ref, gc_ref, v_ref, gch_ref, vh_ref, sga_ref, sgb_ref, ys_ref,
             u_ref, mod_ref, cw_ref, dsk_ref, wco_ref, wglu_ref, wso_ref, wout_ref,
             dgla_ref, dglb_ref, dgb_ref, dconv_ref, ds0_ref, dgt_ref, ddsk_ref, dwout_ref, dwco_ref, dwso_ref,
             dwglu_ref, cv_s):
        bi, i = pl.program_id(0), pl.program_id(1)
        _zero_when(i == 0, dgt_ref)
        _zero_when(jnp.logical_and(bi == 0, i == 0), ddsk_ref, dwout_ref, dwco_ref, dwso_ref, dwglu_ref)
        dxo = dxo_ref[...]
        _, _, gt = _mod_rows(mod_ref, 1)
        dmix = (gt * dxo).astype(BF16)
        dmerged = _dot(dmix, wout_ref[...], NT)
        uv = u_ref[...]
        s0 = ys_ref[...] + dsk_ref[...] * uv
        s1, th = _gelu(s0)
        s1b = s1.astype(BF16)
        sz = jax.nn.sigmoid(_dot(s1b, wglu_ref[...]))
        s2b = (s1 * sz).astype(BF16)
        cv, cv1, cv2 = _conv_inputs(gc_ref, v_ref, gch_ref, vh_ref, cv_s, i, tm)
        w = cw_ref[...]
        conv = w[0:1] * cv2 + w[1:2] * cv1 + w[2:3] * cv
        gbv = gb_ref[...].astype(F32)
        ya, yb = ya_ref[...].astype(F32), yb_ref[...].astype(F32)
        sga, sgb = sga_ref[...].astype(F32), sgb_ref[...].astype(F32)
        merged = (sga * ya + sgb * yb).astype(BF16)
        dwout_ref[...] += _dot(merged, dmix, TN)
        dgt_ref[...] += jnp.sum(dxo * mix_ref[...].astype(F32), axis=0, keepdims=True)[None]
        dgla_ref[...] = (dmerged * ya * sga * (1.0 - sga)).astype(BF16)
        dglb_ref[...] = (dmerged * yb * sgb * (1.0 - sgb)).astype(BF16)
        dya = (dmerged * sga).astype(BF16)
        dyb = (dmerged * sgb).astype(BF16)
        dwco_ref[...] += _dot((gbv * conv).astype(BF16), dya, TN)
        dya_in = _dot(dya, wco_ref[...], NT)
        dgb_ref[...] = (dya_in * conv).astype(BF16)
        dconv_ref[...] = dya_in * gbv
        dwso_ref[...] += _dot(dyb, s2b, TN)
        ds2 = _dot(dyb, wso_ref[...])
        dz = (ds2 * s1 * sz * (1.0 - sz)).astype(BF16)
        dwglu_ref[...] += _dot(s1b, dz, TN)
        ds1 = ds2 * sz + _dot(dz, wglu_ref[...], NT)
        ds0 = ds1 * _gelu_grad(s0, th)
        ds0_ref[...] = ds0
        ddsk_ref[...] += jnp.sum(ds0 * uv, axis=0, keepdims=True)

    rd, rc, rw = _row_spec(tm, d, tps), _row_spec(tm, cw, tps), _row_spec(tm, sw, tps)
    hc = _halo_spec(tm, cw, tps)
    return _call(
        body, name="mixer_merge_backward", grid=(nb, tps), carried=carried,
        out_shape=[jax.ShapeDtypeStruct((t, d), BF16), jax.ShapeDtypeStruct((t, d), BF16),
                   jax.ShapeDtypeStruct((t, cw), BF16), jax.ShapeDtypeStruct((t, cw), F32),
                   jax.ShapeDtypeStruct((t, sw), F32), jax.ShapeDtypeStruct((nb, 1, d), F32),
                   jax.ShapeDtypeStruct((1, sw), F32), jax.ShapeDtypeStruct(wout.shape, F32),
                   jax.ShapeDtypeStruct(wco.shape, F32), jax.ShapeDtypeStruct(wso_t.shape, F32),
                   jax.ShapeDtypeStruct(wglu.shape, F32)],
        in_specs=[rd, rd, rd, rd, rc, rc, rc, hc, hc, rd, rd, rw, rw, _mod_spec(d), _const(conv_w.shape),
                  _const((1, sw)), _const(wco.shape), _const(wglu.shape), _const(wso_t.shape), _const(wout.shape)],
        out_specs=[rd, rd, rc, rc, rw, pl.BlockSpec((1, 1, d), lambda b, i: (b, 0, 0)), _full((1, sw)),
                   _full(wout.shape), _full(wco.shape), _full(wso_t.shape), _full(wglu.shape)],
        scratch_shapes=[pltpu.VMEM((tm + HALO, cw), F32)],
        args=(dxo, mix, ya, yb, gb, gc, v, gc, v, sga, sgb, yssm, u, mod, conv_w, dskip, wco, wglu, wso_t, wout))


def conv_backward(dconv, gc, v, conv_w, nb):
    t, cw = dconv.shape
    s = t // nb
    tm = _tile(s, 512)
    tps = s // tm
    per = tm // ROWS
    slab = _tile(tm, 32)

    def body(dc_ref, dcn_ref, gc_ref, v_ref, gch_ref, vh_ref, cw_ref, dgc_ref, dv_ref, dw_ref, cv_s, dc_s):
        bi, i = pl.program_id(0), pl.program_id(1)
        _zero_when(jnp.logical_and(bi == 0, i == 0), dw_ref)
        cv_s[0:HALO, :] = jnp.where(i == 0, 0.0, gch_ref[...].astype(F32) * vh_ref[...].astype(F32))
        dc_s[tm:, :] = jnp.where(i == tps - 1, 0.0, dcn_ref[...])
        for r in range(0, tm, slab):
            cv_s[HALO + r:HALO + r + slab, :] = (gc_ref[r:r + slab, :].astype(F32)
                                                 * v_ref[r:r + slab, :].astype(F32))
            dc_s[r:r + slab, :] = dc_ref[r:r + slab, :]
        w = cw_ref[...]
        sums = [jnp.zeros((ROWS, cw), F32)] * CONV_K
        for r in range(0, tm, slab):
            dc = dc_s[r:r + slab, :]
            dcv = w[2:3] * dc + w[1:2] * dc_s[r + 1:r + 1 + slab, :] + w[0:1] * dc_s[r + 2:r + 2 + slab, :]
            dgc_ref[r:r + slab, :] = (dcv * v_ref[r:r + slab, :].astype(F32)).astype(BF16)
            dv_ref[r:r + slab, :] = (dcv * gc_ref[r:r + slab, :].astype(F32)).astype(BF16)
            for k in range(CONV_K):
                lag = HALO + r - (CONV_K - 1 - k)
                prod = dc * cv_s[lag:lag + slab, :]
                sums[k] = sums[k] + jnp.sum(prod.reshape(slab // ROWS, ROWS, cw), axis=0)
        dw_ref[...] += jnp.concatenate([jnp.sum(a, axis=0, keepdims=True) for a in sums], axis=0)

    rc = _row_spec(tm, cw, tps)
    nxt = pl.BlockSpec((ROWS, cw), lambda b, i: (jnp.minimum((b * tps + i + 1) * per, nb * tps * per - 1), 0))
    hc = _halo_spec(tm, cw, tps)
    return pl.pallas_call(
        body, name="conv_backward", grid=(nb, tps),
        out_shape=[jax.ShapeDtypeStruct((t, cw), BF16), jax.ShapeDtypeStruct((t, cw), BF16),
                   jax.ShapeDtypeStruct(conv_w.shape, F32)],
        in_specs=[rc, nxt, rc, rc, hc, hc, _full(conv_w.shape)],
        out_specs=[rc, rc, _full(conv_w.shape)],
        scratch_shapes=[pltpu.VMEM((tm + HALO, cw), F32), pltpu.VMEM((tm + ROWS, cw), F32)],
        compiler_params=_params(("arbitrary", "arbitrary")),
    )(dconv, dconv, gc, v, gc, v, conv_w)


def loss_forward_backward(x, target, g):
    t, d = x.shape
    tm = _tile(t, 512)

    def body(x_ref, t_ref, g_ref, l_ref, dx_ref, dg_ref):
        _zero_when(pl.program_id(0) == 0, dg_ref)
        xv = x_ref[...]
        gv = g_ref[...]
        r = lax.rsqrt(jnp.mean(xv * xv, axis=-1, keepdims=True) + EPS)
        n = xv * r
        err = n * gv - t_ref[...]
        l_ref[...] = jnp.full(l_ref.shape, 0.5 * jnp.sum(jnp.mean(err * err, axis=-1)), F32)
        dy = err * (1.0 / d)
        dn = dy * gv
        dx_ref[...] = r * (dn - n * jnp.mean(n * dn, axis=-1, keepdims=True))
        dg_ref[...] += jnp.sum(dy * n, axis=0, keepdims=True)

    row = pl.BlockSpec((tm, d), lambda i: (i, 0))
    return pl.pallas_call(
        body, name="loss_forward_backward", grid=(t // tm,),
        out_shape=[jax.ShapeDtypeStruct((t // tm, 1, 128), F32), jax.ShapeDtypeStruct((t, d), F32),
                   jax.ShapeDtypeStruct((1, d), F32)],
        in_specs=[row, row, _full((1, d))],
        out_specs=[pl.BlockSpec((1, 1, 128), lambda i: (i, 0, 0)), row, _full((1, d))],
        compiler_params=_params(("arbitrary",)),
    )(x, target, g)


def sum_slots(slots, name):
    _, r, c = slots.shape
    tr = _tile(r, 352) if r % 352 == 0 else _tile(r, 256)
    if tr < 128:
        tr = r

    def body(s_ref, o_ref):
        acc = s_ref[0].astype(F32)
        for j in range(1, N_DEV):
            acc = acc + s_ref[j].astype(F32)
        o_ref[...] = acc

    return pl.pallas_call(
        body, name=name, grid=(r // tr,),
        out_shape=jax.ShapeDtypeStruct((r, c), F32),
        in_specs=[pl.BlockSpec((N_DEV, tr, c), lambda i: (0, i, 0))],
        out_specs=pl.BlockSpec((tr, c), lambda i: (i, 0)),
        compiler_params=_params(("arbitrary",)),
    )(slots)


def adamw_update(w, g, m, v, name):
    r, c = w.shape
    tr = _tile(r, 256) if r % 8 == 0 else r

    def body(w_ref, g_ref, m_ref, v_ref, d_ref, mo_ref, vo_ref):
        d_ref[...], mo_ref[...], vo_ref[...] = _adamw(w_ref[...], g_ref[...], m_ref[...], v_ref[...])

    spec = pl.BlockSpec((tr, c), lambda i: (i, 0))
    return pl.pallas_call(
        body, name=name, grid=(r // tr,),
        out_shape=[jax.ShapeDtypeStruct((r, c), F32)] * 3,
        in_specs=[spec] * 4, out_specs=[spec] * 3,
        compiler_params=_params(("arbitrary",)),
    )(w, g, m, v)


def sum_adamw_update(slots, w, m, v, name):
    _, r, c = slots.shape
    tr = _tile(r, 352) if r % 352 == 0 else _tile(r, 256)

    def body(s_ref, w_ref, m_ref, v_ref, g_ref, d_ref, mo_ref, vo_ref):
        g = s_ref[0].astype(F32)
        for j in range(1, N_DEV):
            g = g + s_ref[j].astype(F32)
        g_ref[...] = g
        d_ref[...], mo_ref[...], vo_ref[...] = _adamw(w_ref[...], g, m_ref[...], v_ref[...])

    spec = pl.BlockSpec((tr, c), lambda i: (i, 0))
    return pl.pallas_call(
        body, name=name, grid=(r // tr,),
        out_shape=[jax.ShapeDtypeStruct((r, c), F32)] * 4,
        in_specs=[pl.BlockSpec((N_DEV, tr, c), lambda i: (0, i, 0))] + [spec] * 3, out_specs=[spec] * 4,
        compiler_params=_params(("arbitrary",)),
    )(slots, w, m, v)


def adamw_update_small(ws, gs, ms, vs):
    n = len(ws)

    def body(*refs):
        w_r, g_r, m_r, v_r = refs[:n], refs[n:2 * n], refs[2 * n:3 * n], refs[3 * n:4 * n]
        d_r, mo_r, vo_r = refs[4 * n:5 * n], refs[5 * n:6 * n], refs[6 * n:7 * n]
        for k in range(n):
            d_r[k][...], mo_r[k][...], vo_r[k][...] = _adamw(w_r[k][...], g_r[k][...], m_r[k][...], v_r[k][...])

    shapes = [jax.ShapeDtypeStruct(w.shape, F32) for w in ws]
    out = pl.pallas_call(body, name="adamw_update_small", out_shape=shapes * 3,
                         compiler_params=_params())(*ws, *gs, *ms, *vs)
    return out[:n], out[n:2 * n], out[2 * n:]


def _slots(grad_t):
    return grad_t.reshape(N_DEV, grad_t.shape[0] // N_DEV, grad_t.shape[1])


def kernel(x, c, w_ada, b_ada, g_ffn1, w1_a, w3_a, w2_a, g_mix, w_in, conv_w, w_conv_out, a_re, a_im, b_re, b_im, c_re, c_im, log_dt, d_skip, w_glu, w_ssm_out, w_out, g_ffn2, w1_b, w3_b, w2_b, g_final, loss_target, m_w_ada, m_b_ada, m_g_ffn1, m_w1_a, m_w3_a, m_w2_a, m_g_mix, m_w_in, m_conv_w, m_w_conv_out, m_a_re, m_a_im, m_b_re, m_b_im, m_c_re, m_c_im, m_log_dt, m_d_skip, m_w_glu, m_w_ssm_out, m_w_out, m_g_ffn2, m_w1_b, m_w3_b, m_w2_b, m_g_final, v_w_ada, v_b_ada, v_g_ffn1, v_w1_a, v_w3_a, v_w2_a, v_g_mix, v_w_in, v_conv_w, v_w_conv_out, v_a_re, v_a_im, v_b_re, v_b_im, v_c_re, v_c_im, v_log_dt, v_d_skip, v_w_glu, v_w_ssm_out, v_w_out, v_g_ffn2, v_w1_b, v_w3_b, v_w2_b, v_g_final):
    nb, s, d = x.shape
    t = nb * s
    me = 4 * lax.axis_index("x") + 2 * lax.axis_index("y") + lax.axis_index("c")
    g_n, p_n, h_n = b_re.shape[1:]
    cw_n = w_conv_out.shape[1] * N_DEV
    sw_n = w_ssm_out.shape[1]
    glu_fold = d // w_glu.shape[2]

    weights = dict(w_ada=w_ada, b_ada=b_ada, g_ffn1=g_ffn1, w1_a=w1_a, w3_a=w3_a, w2_a=w2_a, g_mix=g_mix, w_in=w_in,
                   conv_w=conv_w, w_conv_out=w_conv_out, a_re=a_re, a_im=a_im, b_re=b_re, b_im=b_im, c_re=c_re,
                   c_im=c_im, log_dt=log_dt, d_skip=d_skip, w_glu=w_glu, w_ssm_out=w_ssm_out, w_out=w_out,
                   g_ffn2=g_ffn2, w1_b=w1_b, w3_b=w3_b, w2_b=w2_b, g_final=g_final)
    mom1 = dict(w_ada=m_w_ada, b_ada=m_b_ada, g_ffn1=m_g_ffn1, w1_a=m_w1_a, w3_a=m_w3_a, w2_a=m_w2_a, g_mix=m_g_mix,
                w_in=m_w_in, conv_w=m_conv_w, w_conv_out=m_w_conv_out, a_re=m_a_re, a_im=m_a_im, b_re=m_b_re,
                b_im=m_b_im, c_re=m_c_re, c_im=m_c_im, log_dt=m_log_dt, d_skip=m_d_skip, w_glu=m_w_glu,
                w_ssm_out=m_w_ssm_out, w_out=m_w_out, g_ffn2=m_g_ffn2, w1_b=m_w1_b, w3_b=m_w3_b, w2_b=m_w2_b,
                g_final=m_g_final)
    mom2 = dict(w_ada=v_w_ada, b_ada=v_b_ada, g_ffn1=v_g_ffn1, w1_a=v_w1_a, w3_a=v_w3_a, w2_a=v_w2_a, g_mix=v_g_mix,
                w_in=v_w_in, conv_w=v_conv_w, w_conv_out=v_w_conv_out, a_re=v_a_re, a_im=v_a_im, b_re=v_b_re,
                b_im=v_b_im, c_re=v_c_re, c_im=v_c_im, log_dt=v_log_dt, d_skip=v_d_skip, w_glu=v_w_glu,
                w_ssm_out=v_w_ssm_out, w_out=v_w_out, g_ffn2=v_g_ffn2, w1_b=v_w1_b, w3_b=v_w3_b, w2_b=v_w2_b,
                g_final=v_g_final)
    names = list(weights)
    transposed = ("w1_a", "w3_a", "w_in", "w_ssm_out", "w1_b", "w3_b")
    groups = dict(ffn_a=("w1_a", "w3_a", "w2_a"), mixer=("w_in", "w_conv_out", "w_glu", "w_ssm_out", "w_out"),
                  ffn_b=("w1_b", "w3_b", "w2_b"))
    big = groups["ffn_a"] + groups["mixer"] + groups["ffn_b"]

    def shard_rows(name):
        w = weights[name][0]
        if name in transposed:
            w = w.T
        if name == "w_glu":
            w = w.reshape(w.shape[0] // glu_fold, d)
        return w.astype(BF16)

    def gather_stage(group):
        return GatherStage([shard_rows(n) for n in groups[group]])

    gw = {}

    def keep_weights(group, outs):
        for n, w in zip(groups[group], outs):
            gw[n] = w.reshape(sw_n, sw_n) if n == "w_glu" else w

    pad_rows = lambda a: jnp.pad(a, ((0, -a.shape[0] % GATHER_ROWS), (0, 0)))
    c_all, conv_all, *ffn_a_weights = run_stage(
        StageGroup([GatherStage([pad_rows(c), pad_rows(conv_w[0])]), gather_stage("ffn_a")]), "gather_cond_ffn_a")
    keep_weights("ffn_a", ffn_a_weights)
    c_all = c_all.reshape(N_DEV, -1, d)[:, :nb].reshape(N_DEV * nb, d)
    conv_full = conv_all.reshape(N_DEV, GATHER_ROWS, -1)[:, :CONV_K].transpose(1, 0, 2).reshape(CONV_K, cw_n)
    ada_cols = w_ada.shape[2]
    b_cols = lax.dynamic_slice(b_ada, (0, me * ada_cols), (1, ada_cols))
    mod_cols = ada_forward(c_all, w_ada[0], b_cols)
    (mod_all,) = run_stage(GatherStage([mod_cols]), "gather_mod")
    mod_mine = lax.dynamic_slice(mod_all.reshape(N_DEV, N_DEV * nb, ada_cols), (0, me * nb, 0), (N_DEV, nb, ada_cols))
    mod = mod_mine.transpose(1, 0, 2).reshape(nb, N_MOD, d)

    disc_in = (a_re[0], a_im[0], b_re[0], b_im[0], log_dt[0])
    (abr, abi, bbr, bbi), disc_vjp = jax.vjp(ssm_discretise, *disc_in)
    bd, cd, abar8, ml, mlb = ssm_tables(abr, abi, bbr, bbi, c_re[0], c_im[0], _tile(s, SSM_CHUNK) // ROWS)

    x0 = x.reshape(t, d)
    (x1, a1, b1, f1, h1), got = ffn_forward(x0, mod, g_ffn1, gw["w1_a"], gw["w3_a"], gw["w2_a"], 0, nb,
                                            "ffn_a_forward", carried=gather_stage("mixer"))
    keep_weights("mixer", got)
    (gb, gc, vv, u, sga, sgb, h2), _ = mixer_proj_forward(x1, mod, g_mix, gw["w_in"], cw_n, sw_n, nb)
    (yssm, st), got = ssm_forward(u, bd, cd, abar8, ml, nb, carried=gather_stage("ffn_b"))
    keep_weights("ffn_b", got)
    x2, ya, yb, mix = mixer_merge_forward(x1, gb, gc, vv, sga, sgb, yssm, u, mod, conv_full, d_skip,
                                          gw["w_conv_out"], gw["w_glu"], gw["w_ssm_out"], gw["w_out"], nb)
    (x3, a3, b3, f3, h3), _ = ffn_forward(x2, mod, g_ffn2, gw["w1_b"], gw["w3_b"], gw["w2_b"], 2, nb,
                                          "ffn_b_forward")
    loss_parts, dx3, dg_final = loss_forward_backward(x3, loss_target.reshape(t, d), g_final.reshape(1, d))
    loss_here = jnp.sum(loss_parts[:, 0, 0]).reshape(1)

    part, received = {}, {}

    def exchange_stage(ns):
        return ExchangeStage([_slots(part[n]) for n in ns])

    (da3, db3, sw3, df3, dgt3), _ = ffn_backward_hidden(dx3, a3, b3, f3, mod, gw["w2_b"], 2, nb,
                                                        "ffn_b_backward_hidden")
    (dx2, dmod3, dg_ffn2), _ = ffn_backward_input(dx3, x2, da3, db3, mod, g_ffn2, gw["w1_b"], gw["w3_b"],
                                                  2, nb, "ffn_b_backward_input")
    (part["w1_b"],), _ = nn_matmul(da3, h3, "grad_w1_b")
    (part["w3_b"],), _ = nn_matmul(db3, h3, "grad_w3_b")
    (part["w2_b"],), _ = nn_matmul(sw3, df3, "grad_w2_b")
    (dgla, dglb, dgb, dconv, ds0, dgt2, dd_skip, dw_out, dw_co, dw_so_t, dw_glu), got = mixer_merge_backward(
        dx2, mix, ya, yb, gb, gc, vv, sga, sgb, yssm, u, mod, conv_full, d_skip,
        gw["w_conv_out"], gw["w_glu"], gw["w_ssm_out"], gw["w_out"], nb, carried=exchange_stage(("w1_b", "w3_b")))
    received.update(zip(("w1_b", "w3_b"), got))
    part["w_out"] = dw_out.astype(BF16)
    part["w_conv_out"] = dw_co.astype(BF16)
    part["w_ssm_out"] = dw_so_t.astype(BF16)
    part["w_glu"] = dw_glu.reshape(sw_n // glu_fold, d).astype(BF16)
    (du, dab, dbd, dcd), got = ssm_backward(u, ds0, st, bd, cd, abar8, mlb, d_skip, nb,
                                            carried=exchange_stage(("w2_b",) + groups["mixer"][1:]))
    received.update(zip(("w2_b",) + groups["mixer"][1:], got))
    dgc, dvv, dconv_w = conv_backward(dconv, gc, vv, conv_full, nb)
    part["w_in"] = jnp.concatenate([tn_matmul(p, h2, "grad_w_in_%d" % k)
                                    for k, p in enumerate((dgb, dgc, dvv, du, dgla, dglb))], axis=0)
    (dx1, dmod2, dg_mix), got = mixer_proj_backward(dgb, dgc, dvv, du, dgla, dglb, dx2, x1, mod, g_mix, gw["w_in"], nb,
                                                    carried=exchange_stage(("w_in",)))
    received["w_in"] = got[0]
    def pack(parts):
        flat = jnp.concatenate([a.reshape(-1) for a in parts.values()])
        rows = -(-flat.shape[0] // (128 * GATHER_ROWS)) * GATHER_ROWS
        return jnp.pad(flat, (0, rows * 128 - flat.shape[0])).reshape(rows, 128)

    def unpack(flat, parts):
        out, off = {}, 0
        for key, like in parts.items():
            n = math.prod(like.shape)
            out[key], off = flat[..., off:off + n].reshape(flat.shape[:-1] + like.shape), off + n
        return out

    dabr, dabi, dbbr, dbbi, dcr, dci = ssm_table_grads(dab, dbd, dcd, g_n, p_n, h_n)
    early = dict(gmod=jnp.concatenate([dmod2, dgt2, dmod3, dgt3], axis=1), g_mix=dg_mix, g_ffn2=dg_ffn2,
                 g_final=dg_final, d_skip=dd_skip, abr=dabr, abi=dabi, bbr=dbbr, bbi=dbbi, c_re=dcr, c_im=dci,
                 conv_w=dconv_w, loss=loss_here)
    (da1, db1, sw1, df1, dgt1), (early_all,) = ffn_backward_hidden(
        dx1, a1, b1, f1, mod, gw["w2_a"], 0, nb, "ffn_a_backward_hidden", carried=GatherStage([pack(early)]))
    (part["w2_a"],), _ = nn_matmul(sw1, df1, "grad_w2_a")
    (part["w1_a"],), got = nn_matmul(da1, h1, "grad_w1_a", carried=exchange_stage(("w2_a",)))
    received["w2_a"] = got[0]
    (part["w3_a"],), got = nn_matmul(db1, h1, "grad_w3_a", carried=exchange_stage(("w1_a",)))
    received["w1_a"] = got[0]
    (dx0, dmod1, dg_ffn1), got = ffn_backward_input(dx1, x0, da1, db1, mod, g_ffn1, gw["w1_a"], gw["w3_a"],
                                                    0, nb, "ffn_a_backward_input", carried=exchange_stage(("w3_a",)))
    received["w3_a"] = got[0]
    late = dict(gmod=jnp.concatenate([dmod1, dgt1], axis=1), g_ffn1=dg_ffn1)
    (late_all,) = run_stage(GatherStage([pack(late)]), "gather_ffn_a_small_grads")

    tot, per_dev = {}, {}
    for parts, gathered, name in ((early, early_all, "early"), (late, late_all, "late")):
        slots = gathered.reshape(N_DEV, -1, 128)
        total = sum_slots(slots, "sum_small_grads_" + name).reshape(-1)
        for key, val in unpack(total, parts).items():
            tot[name + key if key == "gmod" else key] = val
        per_dev[name] = unpack(slots.reshape(N_DEV, -1), parts)["gmod"]
    gmod_all = jnp.concatenate([per_dev["late"], per_dev["early"]], axis=2).reshape(N_DEV * nb, N_MOD * d)
    gmod_tot = jnp.concatenate([tot["lategmod"], tot["earlygmod"]], axis=1)
    g_a_re, g_a_im, g_b_re, g_b_im, g_log_dt = disc_vjp((tot["abr"], tot["abi"], tot["bbr"], tot["bbi"]))

    loss = tot["loss"].reshape(())
    grads = {}
    grads["b_ada"] = sum_rows(gmod_tot.reshape(nb, N_MOD * d))
    grads["g_ffn1"], grads["g_mix"], grads["g_ffn2"] = tot["g_ffn1"], tot["g_mix"], tot["g_ffn2"]
    grads["g_final"] = tot["g_final"].reshape(d)
    grads["d_skip"] = tot["d_skip"]
    grads["a_re"], grads["a_im"], grads["log_dt"] = g_a_re[None], g_a_im[None], g_log_dt[None]
    grads["b_re"], grads["b_im"] = g_b_re[None], g_b_im[None]
    grads["c_re"], grads["c_im"] = tot["c_re"][None], tot["c_im"][None]
    grads["conv_w"] = lax.dynamic_slice(tot["conv_w"], (0, me * conv_w.shape[2]), (CONV_K, conv_w.shape[2]))[None]

    delta, new_m, new_v = {}, {}, {}
    for name in big:
        wmv = (weights[name][0], mom1[name][0], mom2[name][0])
        if name in transposed:
            outs = sum_adamw_update(received[name], *[a.T for a in wmv], "adamw_" + name)
            gsum, dl, mm, vn = [a.T for a in outs]
        elif received[name].shape[1:] == weights[name].shape[1:]:
            gsum, dl, mm, vn = sum_adamw_update(received[name], *wmv, "adamw_" + name)
        else:
            gsum = sum_slots(received[name], "sum_" + name).reshape(weights[name].shape[1:])
            dl, mm, vn = adamw_update(*wmv[:1], gsum, *wmv[1:], "adamw_" + name)
        grads[name] = gsum[None]
        delta[name], new_m[name], new_v[name] = dl[None], mm[None], vn[None]

    gmod_cols = lax.dynamic_slice(gmod_all, (0, me * ada_cols), (N_DEV * nb, ada_cols))
    g_wada, d_wada, m_wada, v_wada = ada_backward_update(c_all, gmod_cols, w_ada[0], m_w_ada[0], v_w_ada[0])
    grads["w_ada"], delta["w_ada"], new_m["w_ada"], new_v["w_ada"] = g_wada[None], d_wada[None], m_wada[None], v_wada[None]

    small_names = [n for n in names if n not in big and n != "w_ada"]

    narrow = ("b_re", "b_im")

    def as2d(n, a):
        a = jnp.swapaxes(a.reshape(weights[n].shape), -1, -2) if n in narrow else a
        return a.reshape(-1, a.shape[-1])

    def from2d(n, a):
        shape = weights[n].shape
        return jnp.swapaxes(a.reshape(shape[:-2] + (shape[-1], shape[-2])), -1, -2) if n in narrow else a.reshape(shape)

    sw_, sg_, sm_, sv_ = ([as2d(n, src[n]) for n in small_names] for src in (weights, grads, mom1, mom2))
    sd, smo, svo = adamw_update_small(sw_, sg_, sm_, sv_)
    for n, dl, mm, vn in zip(small_names, sd, smo, svo):
        grads[n] = grads[n].reshape(weights[n].shape)
        delta[n], new_m[n], new_v[n] = from2d(n, dl), from2d(n, mm), from2d(n, vn)

    grad_x = dx0.reshape(nb, s, d)
    return (loss, grad_x, *[grads[n] for n in names], *[delta[n] for n in names],
            *[new_m[n] for n in names], *[new_v[n] for n in names])


def sum_rows(a):
    r, c = a.shape

    def body(a_ref, o_ref):
        acc = a_ref[0:1, :]
        for j in range(1, r):
            acc = acc + a_ref[j:j + 1, :]
        o_ref[...] = acc

    return pl.pallas_call(body, name="sum_rows", out_shape=jax.ShapeDtypeStruct((1, c), F32),
                          compiler_params=_params())(a)
```

```python
import functools
import math

import jax
import jax.numpy as jnp
from jax import lax
from jax.experimental import pallas as pl
from jax.experimental.pallas import tpu as pltpu

F32 = jnp.float32
BF16 = jnp.bfloat16
N_DEV = 8
N_MOD = 9
EPS = 1e-6
CONV_K = 3
ADAM_LR = 0.001
ADAM_B1 = 0.9
ADAM_B2 = 0.999
ADAM_EPS = 1e-08
ADAM_WD = 0.01
ADAM_STEP = 10
GELU_C0 = math.sqrt(2.0 / math.pi)
GELU_C1 = 0.044715
V7X_VMEM_LIMIT = 56 * 1024 * 1024
MESH_ID = pl.DeviceIdType.MESH
NT = (((1,), (1,)), ((), ()))
TN = (((0,), (0,)), ((), ()))


def _dot(a, b, dims=None):
    if dims is None:
        return jnp.dot(a, b, preferred_element_type=F32)
    return lax.dot_general(a, b, dims, preferred_element_type=F32)


def _params(sem=None, vmem=V7X_VMEM_LIMIT):
    return pltpu.CompilerParams(dimension_semantics=sem, vmem_limit_bytes=vmem)


def _full(shape):
    return pl.BlockSpec(shape, lambda *_: (0,) * len(shape))


def _const(shape):
    return pl.BlockSpec(shape, lambda *_: (0,) * len(shape), pipeline_mode=pl.Buffered(1))


def _tile(n, want):
    t = min(n, want)
    while n % t:
        t //= 2
    return t


class GatherStage:
    COPIES = 9

    def __init__(self, shards):
        n = len(shards)
        self.inputs = list(shards)
        self.out_shape = [jax.ShapeDtypeStruct((N_DEV * s.shape[0], s.shape[1]), s.dtype) for s in shards]
        self.scratch = [pltpu.SemaphoreType.DMA((self.COPIES * n,)), pltpu.SemaphoreType.DMA((self.COPIES * n,)),
                        pltpu.SemaphoreType.DMA((n,))]

    def _plan(self, ins, outs, sems):
        send_sems, recv_sems, local_sems = sems
        n = len(ins)
        x, y, c = lax.axis_index("x"), lax.axis_index("y"), lax.axis_index("c")
        me, sibling, xn, yn, dg = (x, y, c), (x, y, 1 - c), (1 - x, y, c), (x, 1 - y, c), (1 - x, 1 - y, c)

        def rows(k, block, half=None):
            r = ins[k].shape[0]
            px, py, pc = block
            base = (4 * px + 2 * py + pc) * r
            if half is None:
                return outs[k].at[pl.ds(base, r), :]
            return outs[k].at[pl.ds(base + half * (r // 2), r // 2), :]

        def copy(k, j, block, to, half=None, src=None):
            return pltpu.make_async_remote_copy(
                src_ref=rows(k, block, half) if src is None else src, dst_ref=rows(k, block, half),
                send_sem=send_sems.at[self.COPIES * k + j], recv_sem=recv_sems.at[self.COPIES * k + j],
                device_id=to, device_id_type=MESH_ID)

        sib = lambda b: (b[0], b[1], 1 - b[2])
        mine = [pltpu.make_async_copy(ins[k], rows(k, me), local_sems.at[k]) for k in range(n)]
        first = [(0, me, sibling, None, sibling), (1, me, xn, None, xn), (2, me, yn, None, yn)]
        second = [(3, xn, yn, 0, dg), (4, yn, xn, 1, dg), (5, xn, sibling, None, sib(xn)), (6, yn, sibling, None, sib(yn))]
        third = [(7, dg, sibling, 0, sib(dg)), (8, dg, sibling, 1, sib(dg))]
        return n, me, copy, mine, first, second, third

    def start(self, ins, outs, sems):
        n, me, copy, mine, first, _, _ = self._plan(ins, outs, sems)
        for cp in mine:
            cp.start()
        for k in range(n):
            for j, block, to, half, _ in first:
                copy(k, j, block, to, half, src=ins[k]).start()

    def advance(self, ins, outs, sems):
        n, me, copy, mine, first, second, third = self._plan(ins, outs, sems)
        for k in range(n):
            copy(k, 1, first[1][4], me).wait_recv()
            copy(k, 2, first[2][4], me).wait_recv()
            for j, block, to, half, _ in second:
                copy(k, j, block, to, half).start()

    def advance_again(self, ins, outs, sems):
        n, me, copy, mine, first, second, third = self._plan(ins, outs, sems)
        for k in range(n):
            copy(k, 3, second[0][4], me, 0).wait_recv()
            copy(k, 4, second[1][4], me, 1).wait_recv()
            for j, block, to, half, _ in third:
                copy(k, j, block, to, half).start()

    def finish(self, ins, outs, sems):
        n, me, copy, mine, first, second, third = self._plan(ins, outs, sems)
        arrived = lambda k, j, block, half: copy(k, j, block, me, half).wait_recv()
        for k in range(n):
            arrived(k, 0, first[0][4], None)
            arrived(k, 5, second[2][4], None)
            arrived(k, 6, second[3][4], None)
            arrived(k, 7, third[0][4], 0)
            arrived(k, 8, third[1][4], 1)
        for k in range(n):
            for j, block, to, half, _ in first:
                copy(k, j, block, to, half, src=ins[k]).wait_send()
            for j, block, to, half, _ in second + third:
                copy(k, j, block, to, half).wait_send()
        for cp in mine:
            cp.wait()


class ExchangeStage:
    def __init__(self, bufs):
        n = len(bufs)
        self.inputs = list(bufs)
        self.out_shape = [jax.ShapeDtypeStruct(b.shape, b.dtype) for b in bufs]
        self.scratch = [pltpu.SemaphoreType.DMA((7 * n,)), pltpu.SemaphoreType.DMA((7 * n,)),
                        pltpu.SemaphoreType.DMA((n,))]

    def _plan(self, ins, outs, sems):
        send_sems, recv_sems, local_sems = sems
        n = len(ins)
        x, y, c = lax.axis_index("x"), lax.axis_index("y"), lax.axis_index("c")
        me = 4 * x + 2 * y + c
        mine = [pltpu.make_async_copy(ins[k].at[me], outs[k].at[me], local_sems.at[k]) for k in range(n)]
        copies = []
        for mask in range(1, N_DEV):
            px, py, pc = x ^ (mask >> 2), y ^ ((mask >> 1) & 1), c ^ (mask & 1)
            for k in range(n):
                copies.append(pltpu.make_async_remote_copy(
                    src_ref=ins[k].at[4 * px + 2 * py + pc], dst_ref=outs[k].at[me],
                    send_sem=send_sems.at[7 * k + mask - 1], recv_sem=recv_sems.at[7 * k + mask - 1],
                    device_id=(px, py, pc), device_id_type=MESH_ID))
        return mine, copies

    def start(self, ins, outs, sems):
        mine, copies = self._plan(ins, outs, sems)
        for cp in mine + copies:
            cp.start()

    def advance(self, ins, outs, sems):
        pass

    advance_again = advance

    def finish(self, ins, outs, sems):
        mine, copies = self._plan(ins, outs, sems)
        for cp in copies:
            cp.wait_recv()
        for cp in copies:
            cp.wait_send()
        for cp in mine:
            cp.wait()


ANY_SPEC = pl.BlockSpec(memory_space=pl.ANY)
GATHER_ROWS = 16


class StageGroup:
    def __init__(self, stages):
        self.stages = list(stages)
        self.inputs = [a for s in stages for a in s.inputs]
        self.out_shape = [o for s in stages for o in s.out_shape]
        self.scratch = [t for s in stages for t in s.scratch]

    def _parts(self, ins, outs, sems):
        i = o = t = 0
        for s in self.stages:
            ni, no, nt = len(s.inputs), len(s.out_shape), len(s.scratch)
            yield s, ins[i:i + ni], outs[o:o + no], sems[t:t + nt]
            i, o, t = i + ni, o + no, t + nt

    def start(self, ins, outs, sems):
        for s, i_, o_, t_ in self._parts(ins, outs, sems):
            s.start(i_, o_, t_)

    def advance(self, ins, outs, sems):
        for s, i_, o_, t_ in self._parts(ins, outs, sems):
            s.advance(i_, o_, t_)

    def advance_again(self, ins, outs, sems):
        for s, i_, o_, t_ in self._parts(ins, outs, sems):
            s.advance_again(i_, o_, t_)

    def finish(self, ins, outs, sems):
        for s, i_, o_, t_ in self._parts(ins, outs, sems):
            s.finish(i_, o_, t_)


def run_stage(stage, name):
    ci, co = len(stage.inputs), len(stage.out_shape)

    def body(*refs):
        ins, outs, sems = refs[:ci], refs[ci:ci + co], refs[ci + co:]
        stage.start(ins, outs, sems)
        stage.advance(ins, outs, sems)
        stage.advance_again(ins, outs, sems)
        stage.finish(ins, outs, sems)

    return pl.pallas_call(body, name=name, out_shape=stage.out_shape, in_specs=[ANY_SPEC] * ci,
                          out_specs=[ANY_SPEC] * co, scratch_shapes=stage.scratch)(*stage.inputs)


def _call(body, *, name, grid, in_specs, out_specs, out_shape, args, scratch_shapes=(), carried=None):
    sem = ("arbitrary",) * len(grid)
    if carried is None:
        return pl.pallas_call(body, name=name, grid=grid, in_specs=list(in_specs), out_specs=list(out_specs),
                              out_shape=list(out_shape), scratch_shapes=list(scratch_shapes),
                              compiler_params=_params(sem))(*args), None
    ni, no, ns = len(in_specs), len(out_shape), len(scratch_shapes)
    ci, co = len(carried.inputs), len(carried.out_shape)
    n_steps = math.prod(grid)

    def wrapped(*refs):
        ins, refs = refs[:ni], refs[ni:]
        cins, refs = refs[:ci], refs[ci:]
        outs, refs = refs[:no], refs[no:]
        couts, refs = refs[:co], refs[co:]
        scr, csems = refs[:ns], refs[ns:]
        step = functools.reduce(lambda acc, ig: acc * ig[1] + ig[0],
                                zip([pl.program_id(k) for k in range(len(grid))], grid), 0)

        @pl.when(step == 0)
        def _():
            carried.start(cins, couts, csems)

        @pl.when(step == n_steps // 2)
        def _():
            carried.advance(cins, couts, csems)

        @pl.when(step == (3 * n_steps) // 4)
        def _():
            carried.advance_again(cins, couts, csems)

        body(*ins, *outs, *scr)

        @pl.when(step == n_steps - 1)
        def _():
            carried.finish(cins, couts, csems)

    res = pl.pallas_call(
        wrapped, name=name, grid=grid, in_specs=list(in_specs) + [ANY_SPEC] * ci,
        out_specs=list(out_specs) + [ANY_SPEC] * co, out_shape=list(out_shape) + carried.out_shape,
        scratch_shapes=list(scratch_shapes) + carried.scratch, compiler_params=_params(sem),
    )(*args, *carried.inputs)
    return res[:no], res[no:]


def _norm_mod(x, g, shift, scale):
    r = lax.rsqrt(jnp.mean(x * x, axis=-1, keepdims=True) + EPS)
    n = x * r
    return (n * g) * (1.0 + scale) + shift, n, r


def _norm_mod_bwd(dh, n, r, g, scale):
    dsh = jnp.sum(dh, axis=0, keepdims=True)
    dsc = jnp.sum(dh * (n * g), axis=0, keepdims=True)
    dg = jnp.sum(dh * (1.0 + scale) * n, axis=0, keepdims=True)
    dn = dh * ((1.0 + scale) * g)
    dx = r * (dn - n * jnp.mean(n * dn, axis=-1, keepdims=True))
    return dx, dsh, dsc, dg


def _mod_rows(mod_ref, sub):
    m = mod_ref[0]
    return m[3 * sub:3 * sub + 1], m[3 * sub + 1:3 * sub + 2], m[3 * sub + 2:3 * sub + 3]


def _gelu(x):
    t = jnp.tanh(GELU_C0 * (x + GELU_C1 * x * x * x))
    return 0.5 * x * (1.0 + t), t


def _gelu_grad(x, t):
    return 0.5 * (1.0 + t) + 0.5 * x * (1.0 - t * t) * (GELU_C0 * (1.0 + 3.0 * GELU_C1 * x * x))


def _zero_when(cond, *refs):
    @pl.when(cond)
    def _():
        for r in refs:
            r[...] = jnp.zeros_like(r)


def ada_forward(c_all, w_ada, b_ada_cols):
    def body(c_ref, w_ref, b_ref, o_ref):
        c = c_ref[...]
        cond = (c * jax.nn.sigmoid(c)).astype(BF16)
        o_ref[...] = _dot(cond, w_ref[...].astype(BF16)) + b_ref[...]

    nb, d = c_all.shape
    cols = w_ada.shape[1]
    tn = _tile(cols, 384)
    return pl.pallas_call(
        body, name="ada_forward", grid=(cols // tn,),
        out_shape=jax.ShapeDtypeStruct((nb, cols), F32),
        in_specs=[_full((nb, d)), pl.BlockSpec((d, tn), lambda j: (0, j)), pl.BlockSpec((1, tn), lambda j: (0, j))],
        out_specs=pl.BlockSpec((nb, tn), lambda j: (0, j)),
        compiler_params=_params(("arbitrary",)),
    )(c_all, w_ada, b_ada_cols)


def _adamw(w, g, m, v):
    m = ADAM_B1 * m + (1.0 - ADAM_B1) * g
    v = ADAM_B2 * v + (1.0 - ADAM_B2) * (g * g)
    m_hat = m / (1.0 - ADAM_B1 ** ADAM_STEP)
    v_hat = v / (1.0 - ADAM_B2 ** ADAM_STEP)
    delta = -ADAM_LR * (m_hat / (jnp.sqrt(v_hat) + ADAM_EPS) + ADAM_WD * w)
    return delta, m, v


def ada_backward_update(c_all, gmod_cols, w, m, v):
    def body(c_ref, g_ref, w_ref, m_ref, v_ref, go_ref, d_ref, mo_ref, vo_ref):
        c = c_ref[...]
        cond = (c * jax.nn.sigmoid(c)).astype(BF16)
        g = _dot(cond, g_ref[...].astype(BF16), TN)
        go_ref[...] = g
        d_ref[...], mo_ref[...], vo_ref[...] = _adamw(w_ref[...], g, m_ref[...], v_ref[...])

    nb, d = c_all.shape
    cols = w.shape[1]
    tn = _tile(cols, 128)
    col = pl.BlockSpec((d, tn), lambda j: (0, j))
    return pl.pallas_call(
        body, name="ada_backward_update", grid=(cols // tn,),
        out_shape=[jax.ShapeDtypeStruct(w.shape, F32)] * 4,
        in_specs=[_full((nb, d)), pl.BlockSpec((nb, tn), lambda j: (0, j)), col, col, col],
        out_specs=[col] * 4,
        compiler_params=_params(("arbitrary",)),
    )(c_all, gmod_cols, w, m, v)


def _row_spec(tm, width, tiles_per_seq):
    return pl.BlockSpec((tm, width), lambda b, i: (b * tiles_per_seq + i, 0))


def _mod_spec(d):
    return pl.BlockSpec((1, N_MOD, d), lambda b, i: (b, 0, 0))


def _col_spec(rows, tm, tiles_per_seq):
    return pl.BlockSpec((rows, tm), lambda b, i: (0, b * tiles_per_seq + i))


def _ffn_chunk(f):
    return f // 4 if f % 512 == 0 and f > 1536 else f


def ffn_forward(x, mod, g, w1t, w3t, w2, sub, nb, name, carried=None):
    t, d = x.shape
    f = w1t.shape[0]
    s = t // nb
    tm = _tile(s, 512)
    fc = _ffn_chunk(f)

    def body(x_ref, mod_ref, g_ref, w1_ref, w3_ref, w2_ref, xo_ref, a_ref, b_ref, f_ref, h_ref):
        xv = x_ref[...]
        sh, sc, gt = _mod_rows(mod_ref, sub)
        h, _, _ = _norm_mod(xv, g_ref[...], sh, sc)
        hb = h.astype(BF16)
        h_ref[...] = hb
        acc_t = jnp.zeros((d, tm), F32)
        for k in range(f // fc):
            rows = slice(k * fc, (k + 1) * fc)
            a = _dot(w1_ref[rows, :], hb, NT)
            b = _dot(w3_ref[rows, :], hb, NT)
            a_ref[rows, :] = a.astype(BF16)
            b_ref[rows, :] = b.astype(BF16)
            sw = (a * jax.nn.sigmoid(a)) * b
            acc_t = acc_t + _dot(w2_ref[rows, :], sw.astype(BF16), TN)
        acc = acc_t.T
        f_ref[...] = acc.astype(BF16)
        xo_ref[...] = xv + (0.5 * gt) * acc

    tps = s // tm
    rd, cf = _row_spec(tm, d, tps), _col_spec(f, tm, tps)
    return _call(
        body, name=name, grid=(nb, tps), carried=carried,
        out_shape=[jax.ShapeDtypeStruct((t, d), F32)] + [jax.ShapeDtypeStruct((f, t), BF16)] * 2
        + [jax.ShapeDtypeStruct((t, d), BF16)] * 2,
        in_specs=[rd, _mod_spec(d), _const((1, d)), _const((f, d)), _const((f, d)), _const((f, d))],
        out_specs=[rd, cf, cf, rd, rd],
        args=(x, mod, g, w1t, w3t, w2))


def ffn_backward_hidden(dxo, a_t, b_t, fo, mod, w2, sub, nb, name, carried=None):
    t, d = dxo.shape
    f = w2.shape[0]
    s = t // nb
    tm = _tile(s, 512)
    fc = _tile(f, 704) if f % 704 == 0 else _tile(f, 512)

    def body(dxo_ref, a_ref, b_ref, f_ref, mod_ref, w2_ref, da_ref, db_ref, s_ref, df_ref, dgt_ref):
        _zero_when(pl.program_id(1) == 0, dgt_ref)
        dxo = dxo_ref[...]
        _, _, gt = _mod_rows(mod_ref, sub)
        dfb = ((0.5 * gt) * dxo).astype(BF16)
        df_ref[...] = dfb
        dgt_ref[...] += 0.5 * jnp.sum(dxo * f_ref[...].astype(F32), axis=0, keepdims=True)[None]
        for k in range(f // fc):
            rows = slice(k * fc, (k + 1) * fc)
            ds = _dot(w2_ref[rows, :], dfb, NT).astype(BF16)
            av = a_ref[rows, :].astype(F32)
            bv = b_ref[rows, :]
            sig = jax.nn.sigmoid(av)
            sl = av * sig
            slb = sl.astype(BF16)
            da_ref[rows, :] = ds * bv * (sig + sl * (1.0 - sig)).astype(BF16)
            db_ref[rows, :] = ds * slb
            s_ref[rows, :] = slb * bv

    tps = s // tm
    rd, cf = _row_spec(tm, d, tps), _col_spec(f, tm, tps)
    return _call(
        body, name=name, grid=(nb, tps), carried=carried,
        out_shape=[jax.ShapeDtypeStruct((f, t), BF16)] * 3
        + [jax.ShapeDtypeStruct((t, d), BF16), jax.ShapeDtypeStruct((nb, 1, d), F32)],
        in_specs=[rd, cf, cf, rd, _mod_spec(d), _const((f, d))],
        out_specs=[cf, cf, cf, rd, pl.BlockSpec((1, 1, d), lambda b, i: (b, 0, 0))],
        args=(dxo, a_t, b_t, fo, mod, w2))


def ffn_backward_input(dxo, x, da_t, db_t, mod, g, w1t, w3t, sub, nb, name, carried=None):
    t, d = x.shape
    f = w1t.shape[0]
    s = t // nb
    tm = _tile(s, 512)

    def body(dxo_ref, x_ref, da_ref, db_ref, mod_ref, g_ref, w1_ref, w3_ref, dx_ref, dmod_ref, dg_ref):
        bi, i = pl.program_id(0), pl.program_id(1)
        _zero_when(i == 0, dmod_ref)
        _zero_when(jnp.logical_and(bi == 0, i == 0), dg_ref)
        sh, sc, _ = _mod_rows(mod_ref, sub)
        gv = g_ref[...]
        _, n, r = _norm_mod(x_ref[...], gv, sh, sc)
        dh_t = _dot(w1_ref[...], da_ref[...], TN) + _dot(w3_ref[...], db_ref[...], TN)
        dxn, dsh, dsc, dg = _norm_mod_bwd(dh_t.T, n, r, gv, sc)
        dx_ref[...] = dxo_ref[...] + dxn
        dmod_ref[...] += jnp.concatenate([dsh, dsc], axis=0)[None]
        dg_ref[...] += dg

    tps = s // tm
    rd, cf = _row_spec(tm, d, tps), _col_spec(f, tm, tps)
    return _call(
        body, name=name, grid=(nb, tps), carried=carried,
        out_shape=[jax.ShapeDtypeStruct((t, d), F32), jax.ShapeDtypeStruct((nb, 2, d), F32),
                   jax.ShapeDtypeStruct((1, d), F32)],
        in_specs=[rd, rd, cf, cf, _mod_spec(d), _const((1, d)), _const((f, d)), _const((f, d))],
        out_specs=[rd, pl.BlockSpec((1, 2, d), lambda b, i: (b, 0, 0)), _full((1, d))],
        args=(dxo, x, da_t, db_t, mod, g, w1t, w3t))


def nn_matmul(lhs_t, rhs, name, carried=None):
    m, t = lhs_t.shape
    n = rhs.shape[1]
    tk = _tile(t, 2048)
    tmm = m if m <= 1536 else m // 2
    nk = t // tk

    def body(a_ref, b_ref, o_ref, acc_ref):
        k = pl.program_id(1)
        _zero_when(k == 0, acc_ref)
        acc_ref[...] += _dot(a_ref[...], b_ref[...])

        @pl.when(k == nk - 1)
        def _():
            o_ref[...] = acc_ref[...].astype(BF16)

    return _call(
        body, name=name, grid=(m // tmm, nk), carried=carried,
        out_shape=[jax.ShapeDtypeStruct((m, n), BF16)],
        in_specs=[pl.BlockSpec((tmm, tk), lambda j, k: (j, k)), pl.BlockSpec((tk, n), lambda j, k: (k, 0))],
        out_specs=[pl.BlockSpec((tmm, n), lambda j, k: (j, 0))],
        scratch_shapes=[pltpu.VMEM((tmm, n), F32)],
        args=(lhs_t, rhs))


def tn_matmul(lhs, rhs, name):
    t, m = lhs.shape
    n = rhs.shape[1]
    tk = _tile(t, 2048)
    nk = t // tk

    def body(a_ref, b_ref, o_ref, acc_ref):
        k = pl.program_id(0)
        _zero_when(k == 0, acc_ref)
        acc_ref[...] += _dot(a_ref[...], b_ref[...], TN)

        @pl.when(k == nk - 1)
        def _():
            o_ref[...] = acc_ref[...].astype(BF16)

    return pl.pallas_call(
        body, name=name, grid=(nk,),
        out_shape=jax.ShapeDtypeStruct((m, n), BF16),
        in_specs=[pl.BlockSpec((tk, m), lambda k: (k, 0)), pl.BlockSpec((tk, n), lambda k: (k, 0))],
        out_specs=pl.BlockSpec((m, n), lambda k: (0, 0)),
        scratch_shapes=[pltpu.VMEM((m, n), F32)],
        compiler_params=_params(("arbitrary",)),
    )(lhs, rhs)


def mixer_proj_forward(x, mod, g, w_in_t, cw, sw, nb, carried=None):
    t, d = x.shape
    s = t // nb
    tm = _tile(s, 512)
    pieces = [(0, cw, "bf16"), (cw, cw, "bf16"), (2 * cw, cw, "bf16"), (3 * cw, sw, "f32"),
              (3 * cw + sw, d, "sig"), (3 * cw + sw + d, d, "sig")]

    def body(x_ref, mod_ref, g_ref, w_ref, *outs):
        h_ref = outs[-1]
        sh, sc, _ = _mod_rows(mod_ref, 1)
        h, _, _ = _norm_mod(x_ref[...], g_ref[...], sh, sc)
        hb = h.astype(BF16)
        h_ref[...] = hb
        for (off, width, kind), o_ref in zip(pieces, outs[:-1]):
            ck = _tile(width, 512)
            for j in range(width // ck):
                p = _dot(hb, w_ref[off + j * ck:off + (j + 1) * ck, :], NT)
                if kind == "sig":
                    p = jax.nn.sigmoid(p)
                o_ref[:, j * ck:(j + 1) * ck] = p.astype(o_ref.dtype)

    tps = s // tm
    widths = [(cw, BF16), (cw, BF16), (cw, BF16), (sw, F32), (d, BF16), (d, BF16), (d, BF16)]
    return _call(
        body, name="mixer_proj_forward", grid=(nb, tps), carried=carried,
        out_shape=[jax.ShapeDtypeStruct((t, w), dt) for w, dt in widths],
        in_specs=[_row_spec(tm, d, tps), _mod_spec(d), _const((1, d)), _const(w_in_t.shape)],
        out_specs=[_row_spec(tm, w, tps) for w, _ in widths],
        args=(x, mod, g, w_in_t))


def mixer_proj_backward(dgb, dgc, dv, du, dgla, dglb, dxo, x, mod, g, w_in_t, nb, carried=None):
    t, d = x.shape
    s = t // nb
    tm = _tile(s, 512)
    parts = [dgb, dgc, dv, du, dgla, dglb]
    offs = [0]
    for p in parts:
        offs.append(offs[-1] + p.shape[1])

    def body(*refs):
        p_refs = refs[:6]
        dxo_ref, x_ref, mod_ref, g_ref, w_ref, dx_ref, dmod_ref, dg_ref = refs[6:]
        bi, i = pl.program_id(0), pl.program_id(1)
        _zero_when(i == 0, dmod_ref)
        _zero_when(jnp.logical_and(bi == 0, i == 0), dg_ref)
        dh = jnp.zeros((tm, d), F32)
        for p_ref, off in zip(p_refs, offs):
            dh = dh + _dot(p_ref[...], w_ref[off:off + p_ref.shape[1], :])
        sh, sc, _ = _mod_rows(mod_ref, 1)
        gv = g_ref[...]
        _, n, r = _norm_mod(x_ref[...], gv, sh, sc)
        dxn, dsh, dsc, dg = _norm_mod_bwd(dh, n, r, gv, sc)
        dx_ref[...] = dxo_ref[...] + dxn
        dmod_ref[...] += jnp.concatenate([dsh, dsc], axis=0)[None]
        dg_ref[...] += dg

    tps = s // tm
    rd = _row_spec(tm, d, tps)
    return _call(
        body, name="mixer_proj_backward", grid=(nb, tps), carried=carried,
        out_shape=[jax.ShapeDtypeStruct((t, d), F32), jax.ShapeDtypeStruct((nb, 2, d), F32),
                   jax.ShapeDtypeStruct((1, d), F32)],
        in_specs=[_row_spec(tm, p.shape[1], tps) for p in parts]
        + [rd, rd, _mod_spec(d), _const((1, d)), _const(w_in_t.shape)],
        out_specs=[rd, pl.BlockSpec((1, 2, d), lambda b, i: (b, 0, 0)), _full((1, d))],
        args=(*parts, dxo, x, mod, g, w_in_t))


GROUPS_PER_BLOCK = 8
ROWS = 8
SSM_CHUNK = 512
SCAN_LANES = 512
SCAN_UNROLL = 8


def _scan_rows(xr, xi, masks, shifts):
    for (mr, mi), sft in zip(masks, shifts):
        sr, si = pltpu.roll(xr, sft, 0), pltpu.roll(xi, sft, 0)
        xr, xi = xr + mr * sr - mi * si, xi + mr * si + mi * sr
    return xr, xi


def _cmul_add(ar, ai, cr, ci, br, bi):
    return ar * cr - ai * ci + br, ar * ci + ai * cr + bi


def _segment_rows(perm_ref, x):
    return _dot(perm_ref[0], x).astype(BF16)


def _time_rows(perm_ref, x):
    hi = x.astype(BF16)
    lo = (x - hi.astype(F32)).astype(BF16)
    return _dot(perm_ref[1], hi) + _dot(perm_ref[1], lo)


def segment_permutation(tc):
    r = jnp.arange(tc)
    p = (r[:, None] % ROWS * (tc // ROWS) + r[:, None] // ROWS == r[None, :]).astype(BF16)
    return jnp.stack([p, p.T])


def _scan_loop(n, step, init):
    def trip(j, carry):
        for r in range(SCAN_UNROLL):
            carry = step(j * SCAN_UNROLL + r, carry)
        return carry

    return lax.fori_loop(0, n // SCAN_UNROLL, trip, init)


def _rows_at(k, offset=0):
    return pl.ds(pl.multiple_of(k * ROWS + offset, ROWS), ROWS)


def ssm_forward(u, bd, cd, a1, ml, nb):
    t, w = u.shape
    s = t // nb
    tc = _tile(s, SSM_CHUNK)
    seg = tc // ROWS
    nq, ub, lq = bd.shape[1], bd.shape[2], bd.shape[3]
    nl = nq * lq
    nch = s // tc
    lw = min(nl, SCAN_LANES)

    def body(u_ref, perm_ref, bd_ref, cd_ref, a1_ref, ml_ref, y_ref, st_ref, xr_s, xi_s, car_s):
        i = pl.program_id(1)

        @pl.when(i == 0)
        def _():
            car_s[...] = jnp.zeros_like(car_s)

        ub16 = _segment_rows(perm_ref, u_ref[...].astype(BF16))
        for q in range(nq):
            lanes = slice(q * lq, (q + 1) * lq)
            uq = ub16[:, q * ub:(q + 1) * ub]
            xr_s[:, lanes] = _dot(uq, bd_ref[0, q])
            xi_s[:, lanes] = _dot(uq, bd_ref[1, q])
        row_is_0 = lax.broadcasted_iota(jnp.int32, (ROWS, lw), 0) == 0
        zero = jnp.zeros((ROWS, lw), F32)
        for j in range(nl // lw):
            lanes = slice(j * lw, (j + 1) * lw)
            ar, ai = a1_ref[0, :, lanes], a1_ref[1, :, lanes]

            def local(k, c):
                return _cmul_add(ar, ai, c[0], c[1], xr_s[_rows_at(k), lanes], xi_s[_rows_at(k), lanes])

            er, ei = _scan_loop(seg, local, (zero, zero))
            masks = [(ml_ref[d, 0, :, lanes], ml_ref[d, 1, :, lanes]) for d in range(3)]
            cr, ci = _scan_rows(jnp.where(row_is_0, car_s[0, :, lanes], pltpu.roll(er, 1, 0)),
                                jnp.where(row_is_0, car_s[1, :, lanes], pltpu.roll(ei, 1, 0)), masks, (1, 2, 4))
            st_ref[0, 0, :, lanes] = cr
            st_ref[0, 1, :, lanes] = ci

            def full(k, c):
                xr, xi = _cmul_add(ar, ai, c[0], c[1], xr_s[_rows_at(k), lanes], xi_s[_rows_at(k), lanes])
                xr_s[_rows_at(k), lanes] = xr
                xi_s[_rows_at(k), lanes] = xi
                return xr, xi

            fr, fi = _scan_loop(seg, full, (cr, ci))
            car_s[0, :, lanes] = jnp.broadcast_to(fr[ROWS - 1:ROWS], fr.shape)
            car_s[1, :, lanes] = jnp.broadcast_to(fi[ROWS - 1:ROWS], fi.shape)
        y = jnp.concatenate([_dot(xr_s[:, q * lq:(q + 1) * lq].astype(BF16), cd_ref[0, q])
                             + _dot(xi_s[:, q * lq:(q + 1) * lq].astype(BF16), cd_ref[1, q]) for q in range(nq)],
                            axis=1)
        y_ref[...] = _time_rows(perm_ref, y)

    perm = segment_permutation(tc)
    return pl.pallas_call(
        body, name="ssm_forward", grid=(nb, nch),
        out_shape=[jax.ShapeDtypeStruct((t, w), F32), jax.ShapeDtypeStruct((nb * nch, 2, ROWS, nl), F32)],
        in_specs=[pl.BlockSpec((tc, w), lambda b, i: (b * nch + i, 0)), _const(perm.shape), _const(bd.shape),
                  _const(cd.shape), _const(a1.shape), _const(ml.shape)],
        out_specs=[pl.BlockSpec((tc, w), lambda b, i: (b * nch + i, 0)),
                   pl.BlockSpec((1, 2, ROWS, nl), lambda b, i: (b * nch + i, 0, 0, 0))],
        scratch_shapes=[pltpu.VMEM((tc, nl), F32), pltpu.VMEM((tc, nl), F32), pltpu.VMEM((2, ROWS, nl), F32)],
        compiler_params=_params(("arbitrary", "arbitrary")),
    )(u, perm, bd, cd, a1, ml)


def ssm_backward(u, dy, st, bd, cd, a1, mlb, dskip, nb, carried=None):
    t, w = u.shape
    s = t // nb
    tc = _tile(s, SSM_CHUNK)
    seg = tc // ROWS
    nq, ub, lq = bd.shape[1], bd.shape[2], bd.shape[3]
    nl = nq * lq
    nch = s // tc
    lw = min(nl, SCAN_LANES)

    def body(u_ref, dy_ref, st_ref, perm_ref, bd_ref, cd_ref, a1_ref, mlb_ref, dsk_ref,
             du_ref, dab_ref, dbd_ref, dcd_ref, xr_s, xi_s, lr_s, li_s, car_s):
        bi, i = pl.program_id(0), pl.program_id(1)
        first = jnp.logical_and(bi == 0, i == 0)

        @pl.when(i == 0)
        def _():
            car_s[...] = jnp.zeros_like(car_s)

        @pl.when(first)
        def _():
            dab_ref[...] = jnp.zeros_like(dab_ref)
            dbd_ref[...] = jnp.zeros_like(dbd_ref)
            dcd_ref[...] = jnp.zeros_like(dcd_ref)

        ub16 = _segment_rows(perm_ref, u_ref[...].astype(BF16))
        dyb16 = _segment_rows(perm_ref, dy_ref[...].astype(BF16))
        xr_s[0:ROWS, :] = st_ref[0, 0]
        xi_s[0:ROWS, :] = st_ref[0, 1]
        for q in range(nq):
            lanes = slice(q * lq, (q + 1) * lq)
            uq = ub16[:, q * ub:(q + 1) * ub]
            dq = dyb16[:, q * ub:(q + 1) * ub]
            xr_s[ROWS:, lanes] = _dot(uq, bd_ref[0, q])
            xi_s[ROWS:, lanes] = _dot(uq, bd_ref[1, q])
            lr_s[:, lanes] = _dot(dq, cd_ref[0, q], NT)
            li_s[:, lanes] = _dot(dq, cd_ref[1, q], NT)
        row_is_7 = lax.broadcasted_iota(jnp.int32, (ROWS, lw), 0) == ROWS - 1
        zero = jnp.zeros((ROWS, lw), F32)
        for j in range(nl // lw):
            lanes = slice(j * lw, (j + 1) * lw)
            ar, ai = a1_ref[0, :, lanes], a1_ref[1, :, lanes]
            nai = -ai

            def states(k, c):
                xr, xi = _cmul_add(ar, ai, c[0], c[1], xr_s[_rows_at(k, ROWS), lanes], xi_s[_rows_at(k, ROWS), lanes])
                xr_s[_rows_at(k, ROWS), lanes] = xr
                xi_s[_rows_at(k, ROWS), lanes] = xi
                return xr, xi

            _scan_loop(seg, states, (st_ref[0, 0, :, lanes], st_ref[0, 1, :, lanes]))

            def local(kk, c):
                k = seg - 1 - kk
                return _cmul_add(ar, nai, c[0], c[1], lr_s[_rows_at(k), lanes], li_s[_rows_at(k), lanes])

            er, ei = _scan_loop(seg, local, (zero, zero))
            masks = [(mlb_ref[d, 0, :, lanes], mlb_ref[d, 1, :, lanes]) for d in range(3)]
            cr, ci = _scan_rows(jnp.where(row_is_7, car_s[0, :, lanes], pltpu.roll(er, ROWS - 1, 0)),
                                jnp.where(row_is_7, car_s[1, :, lanes], pltpu.roll(ei, ROWS - 1, 0)), masks, (7, 6, 4))

            def full(kk, c):
                cr_, ci_, accr, acci = c
                k = seg - 1 - kk
                lr, li = _cmul_add(ar, nai, cr_, ci_, lr_s[_rows_at(k), lanes], li_s[_rows_at(k), lanes])
                lr_s[_rows_at(k), lanes] = lr
                li_s[_rows_at(k), lanes] = li
                xpr, xpi = xr_s[_rows_at(k), lanes], xi_s[_rows_at(k), lanes]
                return lr, li, accr + lr * xpr + li * xpi, acci + li * xpr - lr * xpi

            lr0, li0, accr, acci = _scan_loop(seg, full, (cr, ci, zero, zero))
            car_s[0, :, lanes] = jnp.broadcast_to(lr0[0:1], lr0.shape)
            car_s[1, :, lanes] = jnp.broadcast_to(li0[0:1], li0.shape)
            dab_ref[0, :, lanes] += accr
            dab_ref[1, :, lanes] += acci
        du_parts = []
        for q in range(nq):
            lanes = slice(q * lq, (q + 1) * lq)
            cols = slice(q * ub, (q + 1) * ub)
            lrb, lib = lr_s[:, lanes].astype(BF16), li_s[:, lanes].astype(BF16)
            uq, dq = ub16[:, cols], dyb16[:, cols]
            du_parts.append(_dot(lrb, bd_ref[0, q], NT) + _dot(lib, bd_ref[1, q], NT))
            dbd_ref[0, q] += _dot(uq, lrb, TN)
            dbd_ref[1, q] += _dot(uq, lib, TN)
            dcd_ref[0, q] += _dot(xr_s[ROWS:, lanes].astype(BF16), dq, TN)
            dcd_ref[1, q] += _dot(xi_s[ROWS:, lanes].astype(BF16), dq, TN)
        du = _time_rows(perm_ref, jnp.concatenate(du_parts, axis=1)) + dsk_ref[...] * dy_ref[...]
        du_ref[...] = du.astype(BF16)

    rev = lambda b, i: (b * nch + nch - 1 - i, 0)
    perm = segment_permutation(tc)
    return _call(
        body, name="ssm_backward", grid=(nb, nch), carried=carried,
        out_shape=[jax.ShapeDtypeStruct((t, w), BF16), jax.ShapeDtypeStruct((2, ROWS, nl), F32),
                   jax.ShapeDtypeStruct(bd.shape, F32), jax.ShapeDtypeStruct(cd.shape, F32)],
        in_specs=[pl.BlockSpec((tc, w), rev), pl.BlockSpec((tc, w), rev),
                  pl.BlockSpec((1, 2, ROWS, nl), lambda b, i: (b * nch + nch - 1 - i, 0, 0, 0)),
                  _const(perm.shape), _const(bd.shape), _const(cd.shape), _const(a1.shape), _const(mlb.shape),
                  _const((1, w))],
        out_specs=[pl.BlockSpec((tc, w), rev), _full((2, ROWS, nl)), _full(bd.shape), _full(cd.shape)],
        scratch_shapes=[pltpu.VMEM((tc + ROWS, nl), F32), pltpu.VMEM((tc + ROWS, nl), F32),
                        pltpu.VMEM((tc, nl), F32), pltpu.VMEM((tc, nl), F32), pltpu.VMEM((2, ROWS, nl), F32)],
        args=(u, dy, st, perm, bd, cd, a1, mlb, dskip))


def ssm_discretise(a_re, a_im, b_re, b_im, log_dt):
    dt = jnp.exp(log_dt)[:, None]
    er = jnp.exp(a_re * dt)
    abr, abi = er * jnp.cos(a_im * dt), er * jnp.sin(a_im * dt)
    den = a_re * a_re + a_im * a_im
    nr, ni = abr - 1.0, abi
    fr = ((nr * a_re + ni * a_im) / den)[..., None]
    fi = ((ni * a_re - nr * a_im) / den)[..., None]
    return abr, abi, fr * b_re - fi * b_im, fr * b_im + fi * b_re


def _complex_square(zr, zi):
    return zr * zr - zi * zi, 2.0 * zr * zi


def ssm_tables(abr, abi, bbr, bbi, c_re, c_im, seg):
    g, p, h = bbr.shape
    nq = g // GROUPS_PER_BLOCK
    zr, zi = abr.reshape(1, -1), abi.reshape(1, -1)
    a1 = jnp.stack([jnp.broadcast_to(zr, (ROWS, g * p)), jnp.broadcast_to(zi, (ROWS, g * p))])
    for _ in range(seg.bit_length() - 1):
        zr, zi = _complex_square(zr, zi)
    row = jnp.arange(ROWS)[:, None]
    ml, mlb = [], []
    for d in (1, 2, 4):
        ml.append(jnp.stack([jnp.where(row >= d, zr, 0.0), jnp.where(row >= d, zi, 0.0)]))
        mlb.append(jnp.stack([jnp.where(row + d < ROWS, zr, 0.0), jnp.where(row + d < ROWS, -zi, 0.0)]))
        zr, zi = _complex_square(zr, zi)
    eye = jnp.eye(GROUPS_PER_BLOCK, dtype=F32)

    def block_diag_in(bb):
        bq = bb.reshape(nq, GROUPS_PER_BLOCK, p, h)
        return jnp.einsum("qaph,ab->qahbp", bq, eye).reshape(nq, GROUPS_PER_BLOCK * h, GROUPS_PER_BLOCK * p)

    def block_diag_out(cc):
        cq = cc.reshape(nq, GROUPS_PER_BLOCK, h, p)
        return jnp.einsum("qahp,ab->qapbh", cq, eye).reshape(nq, GROUPS_PER_BLOCK * p, GROUPS_PER_BLOCK * h)

    bd = jnp.stack([block_diag_in(bbr), block_diag_in(bbi)]).astype(BF16)
    cd = jnp.stack([block_diag_out(c_re), block_diag_out(-c_im)]).astype(BF16)
    return bd, cd, a1, jnp.stack(ml), jnp.stack(mlb)


def ssm_table_grads(dab, dbd, dcd, g, p, h):
    nq = g // GROUPS_PER_BLOCK
    dabr, dabi = dab[0].sum(0).reshape(g, p), dab[1].sum(0).reshape(g, p)
    b5 = dbd.reshape(2, nq, GROUPS_PER_BLOCK, h, GROUPS_PER_BLOCK, p)
    dbb = jnp.einsum("rqahap->rqaph", b5).reshape(2, g, p, h)
    c5 = dcd.reshape(2, nq, GROUPS_PER_BLOCK, p, GROUPS_PER_BLOCK, h)
    dcc = jnp.einsum("rqapah->rqahp", c5).reshape(2, g, h, p)
    return dabr, dabi, dbb[0], dbb[1], dcc[0], -dcc[1]


HALO = 16


def _conv_inputs(gc_ref, v_ref, gch_ref, vh_ref, cv_s, i, tm):
    cv = gc_ref[...].astype(F32) * v_ref[...].astype(F32)
    halo = gch_ref[...].astype(F32) * vh_ref[...].astype(F32)
    cv_s[0:HALO, :] = jnp.where(i == 0, 0.0, halo)
    cv_s[HALO:, :] = cv
    return cv, cv_s[HALO - 1:HALO - 1 + tm, :], cv_s[HALO - 2:HALO - 2 + tm, :]


def _halo_spec(tm, width, tiles_per_seq):
    per = tm // HALO
    return pl.BlockSpec((HALO, width), lambda b, i: (jnp.maximum((b * tiles_per_seq + i) * per - 1, 0), 0))


def mixer_merge_forward(x, gb, gc, v, sga, sgb, yssm, u, mod, conv_w, dskip, wco, wglu, wso_t, wout, nb):
    t, d = x.shape
    cw, sw = gb.shape[1], u.shape[1]
    s = t // nb
    tm = _tile(s, 512)
    tps = s // tm

    def body(x_ref, gb_ref, gc_ref, v_ref, gch_ref, vh_ref, sga_ref, sgb_ref, ys_ref, u_ref, mod_ref, cw_ref,
             dsk_ref, wco_ref, wglu_ref, wso_ref, wout_ref, xo_ref, ya_ref, yb_ref, mix_ref, cv_s):
        i = pl.program_id(1)
        cv, cv1, cv2 = _conv_inputs(gc_ref, v_ref, gch_ref, vh_ref, cv_s, i, tm)
        w = cw_ref[...]
        conv = w[0:1] * cv2 + w[1:2] * cv1 + w[2:3] * cv
        ya = _dot((gb_ref[...].astype(F32) * conv).astype(BF16), wco_ref[...])
        s0 = ys_ref[...] + dsk_ref[...] * u_ref[...]
        s1, _ = _gelu(s0)
        z = _dot(s1.astype(BF16), wglu_ref[...])
        s2 = s1 * jax.nn.sigmoid(z)
        yb = _dot(s2.astype(BF16), wso_ref[...], NT)
        merged = sga_ref[...].astype(F32) * ya + sgb_ref[...].astype(F32) * yb
        mix = _dot(merged.astype(BF16), wout_ref[...])
        _, _, gt = _mod_rows(mod_ref, 1)
        xo_ref[...] = x_ref[...] + gt * mix
        ya_ref[...] = ya.astype(BF16)
        yb_ref[...] = yb.astype(BF16)
        mix_ref[...] = mix.astype(BF16)

    rd, rc, rw = _row_spec(tm, d, tps), _row_spec(tm, cw, tps), _row_spec(tm, sw, tps)
    hc = _halo_spec(tm, cw, tps)
    return pl.pallas_call(
        body, name="mixer_merge_forward", grid=(nb, tps),
        out_shape=[jax.ShapeDtypeStruct((t, d), F32)] + [jax.ShapeDtypeStruct((t, d), BF16)] * 3,
        in_specs=[rd, rc, rc, rc, hc, hc, rd, rd, rw, rw, _mod_spec(d), _const(conv_w.shape), _const((1, sw)),
                  _const(wco.shape), _const(wglu.shape), _const(wso_t.shape), _const(wout.shape)],
        out_specs=[rd, rd, rd, rd],
        scratch_shapes=[pltpu.VMEM((tm + HALO, cw), F32)],
        compiler_params=_params(("arbitrary", "arbitrary")),
    )(x, gb, gc, v, gc, v, sga, sgb, yssm, u, mod, conv_w, dskip, wco, wglu, wso_t, wout)


def mixer_merge_backward(dxo, mix, ya, yb, gb, gc, v, sga, sgb, yssm, u, mod, conv_w, dskip,
                         wco, wglu, wso_t, wout, nb, carried=None):
    t, d = dxo.shape
    cw, sw = gb.shape[1], u.shape[1]
    s = t // nb
    tm = _tile(s, 256)
    tps = s // tm

    def body(dxo_ref, mix_ref, ya_ref, yb_ref, gb_ref, gc_ref, v_ref, gch_ref, vh_ref, sga_ref, sgb_ref, ys_ref,
             u_ref, mod_ref, cw_ref, dsk_ref, wco_ref, wglu_ref, wso_ref, wout_ref,
             dgla_ref, dglb_ref, dgb_ref, dconv_ref, ds0_ref, dgt_ref, ddsk_ref, dwout_ref, dwco_ref, dwso_ref,
             dwglu_ref, cv_s):
        bi, i = pl.program_id(0), pl.program_id(1)
        _zero_when(i == 0, dgt_ref)
        _zero_when(jnp.logical_and(bi == 0, i == 0), ddsk_ref, dwout_ref, dwco_ref, dwso_ref, dwglu_ref)
        dxo = dxo_ref[...]
        _, _, gt = _mod_rows(mod_ref, 1)
        dmix = (gt * dxo).astype(BF16)
        dmerged = _dot(dmix, wout_ref[...], NT)
        uv = u_ref[...]
        s0 = ys_ref[...] + dsk_ref[...] * uv
        s1, th = _gelu(s0)
        s1b = s1.astype(BF16)
        sz = jax.nn.sigmoid(_dot(s1b, wglu_ref[...]))
        s2b = (s1 * sz).astype(BF16)
        cv, cv1, cv2 = _conv_inputs(gc_ref, v_ref, gch_ref, vh_ref, cv_s, i, tm)
        w = cw_ref[...]
        conv = w[0:1] * cv2 + w[1:2] * cv1 + w[2:3] * cv
        gbv = gb_ref[...].astype(F32)
        ya, yb = ya_ref[...].astype(F32), yb_ref[...].astype(F32)
        sga, sgb = sga_ref[...].astype(F32), sgb_ref[...].astype(F32)
        merged = (sga * ya + sgb * yb).astype(BF16)
        dwout_ref[...] += _dot(merged, dmix, TN)
        dgt_ref[...] += jnp.sum(dxo * mix_ref[...].astype(F32), axis=0, keepdims=True)[None]
        dgla_ref[...] = (dmerged * ya * sga * (1.0 - sga)).astype(BF16)
        dglb_ref[...] = (dmerged * yb * sgb * (1.0 - sgb)).astype(BF16)
        dya = (dmerged * sga).astype(BF16)
        dyb = (dmerged * sgb).astype(BF16)
        dwco_ref[...] += _dot((gbv * conv).astype(BF16), dya, TN)
        dya_in = _dot(dya, wco_ref[...], NT)
        dgb_ref[...] = (dya_in * conv).astype(BF16)
        dconv_ref[...] = dya_in * gbv
        dwso_ref[...] += _dot(dyb, s2b, TN)
        ds2 = _dot(dyb, wso_ref[...])
        dz = (ds2 * s1 * sz * (1.0 - sz)).astype(BF16)
        dwglu_ref[...] += _dot(s1b, dz, TN)
        ds1 = ds2 * sz + _dot(dz, wglu_ref[...], NT)
        ds0 = ds1 * _gelu_grad(s0, th)
        ds0_ref[...] = ds0
        ddsk_ref[...] += jnp.sum(ds0 * uv, axis=0, keepdims=True)

    rd, rc, rw = _row_spec(tm, d, tps), _row_spec(tm, cw, tps), _row_spec(tm, sw, tps)
    hc = _halo_spec(tm, cw, tps)
    return _call(
        body, name="mixer_merge_backward", grid=(nb, tps), carried=carried,
        out_shape=[jax.ShapeDtypeStruct((t, d), BF16), jax.ShapeDtypeStruct((t, d), BF16),
                   jax.ShapeDtypeStruct((t, cw), BF16), jax.ShapeDtypeStruct((t, cw), F32),
                   jax.ShapeDtypeStruct((t, sw), F32), jax.ShapeDtypeStruct((nb, 1, d), F32),
                   jax.ShapeDtypeStruct((1, sw), F32), jax.ShapeDtypeStruct(wout.shape, F32),
                   jax.ShapeDtypeStruct(wco.shape, F32), jax.ShapeDtypeStruct(wso_t.shape, F32),
                   jax.ShapeDtypeStruct(wglu.shape, F32)],
        in_specs=[rd, rd, rd, rd, rc, rc, rc, hc, hc, rd, rd, rw, rw, _mod_spec(d), _const(conv_w.shape),
                  _const((1, sw)), _const(wco.shape), _const(wglu.shape), _const(wso_t.shape), _const(wout.shape)],
        out_specs=[rd, rd, rc, rc, rw, pl.BlockSpec((1, 1, d), lambda b, i: (b, 0, 0)), _full((1, sw)),
                   _full(wout.shape), _full(wco.shape), _full(wso_t.shape), _full(wglu.shape)],
        scratch_shapes=[pltpu.VMEM((tm + HALO, cw), F32)],
        args=(dxo, mix, ya, yb, gb, gc, v, gc, v, sga, sgb, yssm, u, mod, conv_w, dskip, wco, wglu, wso_t, wout))


def conv_backward(dconv, gc, v, conv_w, nb):
    t, cw = dconv.shape
    s = t // nb
    tm = _tile(s, 512)
    tps = s // tm
    per = tm // ROWS
    slab = _tile(tm, 32)

    def body(dc_ref, dcn_ref, gc_ref, v_ref, gch_ref, vh_ref, cw_ref, dgc_ref, dv_ref, dw_ref, cv_s, dc_s):
        bi, i = pl.program_id(0), pl.program_id(1)
        _zero_when(jnp.logical_and(bi == 0, i == 0), dw_ref)
        cv_s[0:HALO, :] = jnp.where(i == 0, 0.0, gch_ref[...].astype(F32) * vh_ref[...].astype(F32))
        dc_s[tm:, :] = jnp.where(i == tps - 1, 0.0, dcn_ref[...])
        for r in range(0, tm, slab):
            cv_s[HALO + r:HALO + r + slab, :] = (gc_ref[r:r + slab, :].astype(F32)
                                                 * v_ref[r:r + slab, :].astype(F32))
            dc_s[r:r + slab, :] = dc_ref[r:r + slab, :]
        w = cw_ref[...]
        sums = [jnp.zeros((ROWS, cw), F32)] * CONV_K
        for r in range(0, tm, slab):
            dc = dc_s[r:r + slab, :]
            dcv = w[2:3] * dc + w[1:2] * dc_s[r + 1:r + 1 + slab, :] + w[0:1] * dc_s[r + 2:r + 2 + slab, :]
            dgc_ref[r:r + slab, :] = (dcv * v_ref[r:r + slab, :].astype(F32)).astype(BF16)
            dv_ref[r:r + slab, :] = (dcv * gc_ref[r:r + slab, :].astype(F32)).astype(BF16)
            for k in range(CONV_K):
                lag = HALO + r - (CONV_K - 1 - k)
                prod = dc * cv_s[lag:lag + slab, :]
                sums[k] = sums[k] + jnp.sum(prod.reshape(slab // ROWS, ROWS, cw), axis=0)
        dw_ref[...] += jnp.concatenate([jnp.sum(a, axis=0, keepdims=True) for a in sums], axis=0)

    rc = _row_spec(tm, cw, tps)
    nxt = pl.BlockSpec((ROWS, cw), lambda b, i: (jnp.minimum((b * tps + i + 1) * per, nb * tps * per - 1), 0))
    hc = _halo_spec(tm, cw, tps)
    return pl.pallas_call(
        body, name="conv_backward", grid=(nb, tps),
        out_shape=[jax.ShapeDtypeStruct((t, cw), BF16), jax.ShapeDtypeStruct((t, cw), BF16),
                   jax.ShapeDtypeStruct(conv_w.shape, F32)],
        in_specs=[rc, nxt, rc, rc, hc, hc, _full(conv_w.shape)],
        out_specs=[rc, rc, _full(conv_w.shape)],
        scratch_shapes=[pltpu.VMEM((tm + HALO, cw), F32), pltpu.VMEM((tm + ROWS, cw), F32)],
        compiler_params=_params(("arbitrary", "arbitrary")),
    )(dconv, dconv, gc, v, gc, v, conv_w)


def loss_forward_backward(x, target, g):
    t, d = x.shape
    tm = _tile(t, 512)
    steps = t // tm
    ring = 3

    def body(x_hbm, t_hbm, g_ref, l_ref, dx_ref, dg_ref, x_buf, t_buf, sems):
        i = pl.program_id(0)

        def copies(step, slot):
            rows = pl.ds(pl.multiple_of(step * tm, tm), tm)
            return (pltpu.make_async_copy(x_hbm.at[rows, :], x_buf.at[slot], sems.at[0, slot]),
                    pltpu.make_async_copy(t_hbm.at[rows, :], t_buf.at[slot], sems.at[1, slot]))

        @pl.when(i == 0)
        def _():
            dg_ref[...] = jnp.zeros_like(dg_ref)
            for s in range(min(ring, steps)):
                for cp in copies(s, s):
                    cp.start()

        slot = i % ring
        for cp in copies(i, slot):
            cp.wait()
        xv = x_buf[slot]
        gv = g_ref[...]
        r = lax.rsqrt(jnp.mean(xv * xv, axis=-1, keepdims=True) + EPS)
        n = xv * r
        err = n * gv - t_buf[slot]
        l_ref[...] = jnp.full(l_ref.shape, 0.5 * jnp.sum(jnp.mean(err * err, axis=-1)), F32)
        dy = err * (1.0 / d)
        dn = dy * gv
        dx_ref[...] = r * (dn - n * jnp.mean(n * dn, axis=-1, keepdims=True))
        dg_ref[...] += jnp.sum(dy * n, axis=0, keepdims=True)

        @pl.when(i + ring < steps)
        def _():
            for cp in copies(i + ring, slot):
                cp.start()

    row = pl.BlockSpec((tm, d), lambda i: (i, 0))
    return pl.pallas_call(
        body, name="loss_forward_backward", grid=(steps,),
        out_shape=[jax.ShapeDtypeStruct((steps, 1, 128), F32), jax.ShapeDtypeStruct((t, d), F32),
                   jax.ShapeDtypeStruct((1, d), F32)],
        in_specs=[ANY_SPEC, ANY_SPEC, _full((1, d))],
        out_specs=[pl.BlockSpec((1, 1, 128), lambda i: (i, 0, 0)), row, _full((1, d))],
        scratch_shapes=[pltpu.VMEM((ring, tm, d), F32), pltpu.VMEM((ring, tm, d), F32),
                        pltpu.SemaphoreType.DMA((2, ring))],
        compiler_params=_params(("arbitrary",)),
    )(x, target, g)


def sum_slots(slots, name):
    _, r, c = slots.shape
    tr = _tile(r, 352) if r % 352 == 0 else _tile(r, 256)
    if tr < 128:
        tr = r

    def body(s_ref, o_ref):
        acc = s_ref[0].astype(F32)
        for j in range(1, N_DEV):
            acc = acc + s_ref[j].astype(F32)
        o_ref[...] = acc

    return pl.pallas_call(
        body, name=name, grid=(r // tr,),
        out_shape=jax.ShapeDtypeStruct((r, c), F32),
        in_specs=[pl.BlockSpec((N_DEV, tr, c), lambda i: (0, i, 0))],
        out_specs=pl.BlockSpec((tr, c), lambda i: (i, 0)),
        compiler_params=_params(("arbitrary",)),
    )(slots)


def adamw_update(w, g, m, v, name):
    r, c = w.shape
    tr = _tile(r, 256) if r % 8 == 0 else r

    def body(w_ref, g_ref, m_ref, v_ref, d_ref, mo_ref, vo_ref):
        d_ref[...], mo_ref[...], vo_ref[...] = _adamw(w_ref[...], g_ref[...], m_ref[...], v_ref[...])

    spec = pl.BlockSpec((tr, c), lambda i: (i, 0))
    return pl.pallas_call(
        body, name=name, grid=(r // tr,),
        out_shape=[jax.ShapeDtypeStruct((r, c), F32)] * 3,
        in_specs=[spec] * 4, out_specs=[spec] * 3,
        compiler_params=_params(("arbitrary",)),
    )(w, g, m, v)


def sum_adamw_update(slots, w, m, v, name):
    _, r, c = slots.shape
    tr = _tile(r, 352) if r % 352 == 0 else _tile(r, 256)

    def body(s_ref, w_ref, m_ref, v_ref, g_ref, d_ref, mo_ref, vo_ref):
        g = s_ref[0].astype(F32)
        for j in range(1, N_DEV):
            g = g + s_ref[j].astype(F32)
        g_ref[...] = g
        d_ref[...], mo_ref[...], vo_ref[...] = _adamw(w_ref[...], g, m_ref[...], v_ref[...])

    spec = pl.BlockSpec((tr, c), lambda i: (i, 0))
    return pl.pallas_call(
        body, name=name, grid=(r // tr,),
        out_shape=[jax.ShapeDtypeStruct((r, c), F32)] * 4,
        in_specs=[pl.BlockSpec((N_DEV, tr, c), lambda i: (0, i, 0))] + [spec] * 3, out_specs=[spec] * 4,
        compiler_params=_params(("arbitrary",)),
    )(slots, w, m, v)


def adamw_update_small(ws, gs, ms, vs):
    n = len(ws)

    def body(*refs):
        w_r, g_r, m_r, v_r = refs[:n], refs[n:2 * n], refs[2 * n:3 * n], refs[3 * n:4 * n]
        d_r, mo_r, vo_r = refs[4 * n:5 * n], refs[5 * n:6 * n], refs[6 * n:7 * n]
        for k in range(n):
            d_r[k][...], mo_r[k][...], vo_r[k][...] = _adamw(w_r[k][...], g_r[k][...], m_r[k][...], v_r[k][...])

    shapes = [jax.ShapeDtypeStruct(w.shape, F32) for w in ws]
    out = pl.pallas_call(body, name="adamw_update_small", out_shape=shapes * 3,
                         compiler_params=_params())(*ws, *gs, *ms, *vs)
    return out[:n], out[n:2 * n], out[2 * n:]


def _slots(grad_t):
    return grad_t.reshape(N_DEV, grad_t.shape[0] // N_DEV, grad_t.shape[1])


def kernel(x, c, w_ada, b_ada, g_ffn1, w1_a, w3_a, w2_a, g_mix, w_in, conv_w, w_conv_out, a_re, a_im, b_re, b_im, c_re, c_im, log_dt, d_skip, w_glu, w_ssm_out, w_out, g_ffn2, w1_b, w3_b, w2_b, g_final, loss_target, m_w_ada, m_b_ada, m_g_ffn1, m_w1_a, m_w3_a, m_w2_a, m_g_mix, m_w_in, m_conv_w, m_w_conv_out, m_a_re, m_a_im, m_b_re, m_b_im, m_c_re, m_c_im, m_log_dt, m_d_skip, m_w_glu, m_w_ssm_out, m_w_out, m_g_ffn2, m_w1_b, m_w3_b, m_w2_b, m_g_final, v_w_ada, v_b_ada, v_g_ffn1, v_w1_a, v_w3_a, v_w2_a, v_g_mix, v_w_in, v_conv_w, v_w_conv_out, v_a_re, v_a_im, v_b_re, v_b_im, v_c_re, v_c_im, v_log_dt, v_d_skip, v_w_glu, v_w_ssm_out, v_w_out, v_g_ffn2, v_w1_b, v_w3_b, v_w2_b, v_g_final):
    nb, s, d = x.shape
    t = nb * s
    me = 4 * lax.axis_index("x") + 2 * lax.axis_index("y") + lax.axis_index("c")
    g_n, p_n, h_n = b_re.shape[1:]
    cw_n = w_conv_out.shape[1] * N_DEV
    sw_n = w_ssm_out.shape[1]
    glu_fold = d // w_glu.shape[2]

    weights = dict(w_ada=w_ada, b_ada=b_ada, g_ffn1=g_ffn1, w1_a=w1_a, w3_a=w3_a, w2_a=w2_a, g_mix=g_mix, w_in=w_in,
                   conv_w=conv_w, w_conv_out=w_conv_out, a_re=a_re, a_im=a_im, b_re=b_re, b_im=b_im, c_re=c_re,
                   c_im=c_im, log_dt=log_dt, d_skip=d_skip, w_glu=w_glu, w_ssm_out=w_ssm_out, w_out=w_out,
                   g_ffn2=g_ffn2, w1_b=w1_b, w3_b=w3_b, w2_b=w2_b, g_final=g_final)
    mom1 = dict(w_ada=m_w_ada, b_ada=m_b_ada, g_ffn1=m_g_ffn1, w1_a=m_w1_a, w3_a=m_w3_a, w2_a=m_w2_a, g_mix=m_g_mix,
                w_in=m_w_in, conv_w=m_conv_w, w_conv_out=m_w_conv_out, a_re=m_a_re, a_im=m_a_im, b_re=m_b_re,
                b_im=m_b_im, c_re=m_c_re, c_im=m_c_im, log_dt=m_log_dt, d_skip=m_d_skip, w_glu=m_w_glu,
                w_ssm_out=m_w_ssm_out, w_out=m_w_out, g_ffn2=m_g_ffn2, w1_b=m_w1_b, w3_b=m_w3_b, w2_b=m_w2_b,
                g_final=m_g_final)
    mom2 = dict(w_ada=v_w_ada, b_ada=v_b_ada, g_ffn1=v_g_ffn1, w1_a=v_w1_a, w3_a=v_w3_a, w2_a=v_w2_a, g_mix=v_g_mix,
                w_in=v_w_in, conv_w=v_conv_w, w_conv_out=v_w_conv_out, a_re=v_a_re, a_im=v_a_im, b_re=v_b_re,
                b_im=v_b_im, c_re=v_c_re, c_im=v_c_im, log_dt=v_log_dt, d_skip=v_d_skip, w_glu=v_w_glu,
                w_ssm_out=v_w_ssm_out, w_out=v_w_out, g_ffn2=v_g_ffn2, w1_b=v_w1_b, w3_b=v_w3_b, w2_b=v_w2_b,
                g_final=v_g_final)
    names = list(weights)
    transposed = ("w1_a", "w3_a", "w_in", "w_ssm_out", "w1_b", "w3_b")
    groups = dict(ffn_a=("w1_a", "w3_a", "w2_a"), mixer=("w_in", "w_conv_out", "w_glu", "w_ssm_out", "w_out"),
                  ffn_b=("w1_b", "w3_b", "w2_b"))
    big = groups["ffn_a"] + groups["mixer"] + groups["ffn_b"]

    def shard_rows(name):
        w = weights[name][0]
        if name in transposed:
            w = w.T
        if name == "w_glu":
            w = w.reshape(w.shape[0] // glu_fold, d)
        return w.astype(BF16)

    def gather_stage(group):
        return GatherStage([shard_rows(n) for n in groups[group]])

    gw = {}

    def keep_weights(group, outs):
        for n, w in zip(groups[group], outs):
            gw[n] = w.reshape(sw_n, sw_n) if n == "w_glu" else w

    pad_rows = lambda a: jnp.pad(a, ((0, -a.shape[0] % GATHER_ROWS), (0, 0)))
    c_all, conv_all, *ffn_a_weights = run_stage(
        StageGroup([GatherStage([pad_rows(c), pad_rows(conv_w[0])]), gather_stage("ffn_a")]), "gather_cond_ffn_a")
    keep_weights("ffn_a", ffn_a_weights)
    c_all = c_all.reshape(N_DEV, -1, d)[:, :nb].reshape(N_DEV * nb, d)
    conv_full = conv_all.reshape(N_DEV, GATHER_ROWS, -1)[:, :CONV_K].transpose(1, 0, 2).reshape(CONV_K, cw_n)
    ada_cols = w_ada.shape[2]
    b_cols = lax.dynamic_slice(b_ada, (0, me * ada_cols), (1, ada_cols))
    mod_cols = ada_forward(c_all, w_ada[0], b_cols)
    (mod_all,) = run_stage(GatherStage([mod_cols]), "gather_mod")
    mod_mine = lax.dynamic_slice(mod_all.reshape(N_DEV, N_DEV * nb, ada_cols), (0, me * nb, 0), (N_DEV, nb, ada_cols))
    mod = mod_mine.transpose(1, 0, 2).reshape(nb, N_MOD, d)

    disc_in = (a_re[0], a_im[0], b_re[0], b_im[0], log_dt[0])
    (abr, abi, bbr, bbi), disc_vjp = jax.vjp(ssm_discretise, *disc_in)
    bd, cd, abar8, ml, mlb = ssm_tables(abr, abi, bbr, bbi, c_re[0], c_im[0], _tile(s, SSM_CHUNK) // ROWS)

    x0 = x.reshape(t, d)
    (x1, a1, b1, f1, h1), got = ffn_forward(x0, mod, g_ffn1, gw["w1_a"], gw["w3_a"], gw["w2_a"], 0, nb,
                                            "ffn_a_forward", carried=gather_stage("mixer"))
    keep_weights("mixer", got)
    (gb, gc, vv, u, sga, sgb, h2), got = mixer_proj_forward(x1, mod, g_mix, gw["w_in"], cw_n, sw_n, nb,
                                                            carried=gather_stage("ffn_b"))
    keep_weights("ffn_b", got)
    yssm, st = ssm_forward(u, bd, cd, abar8, ml, nb)
    x2, ya, yb, mix = mixer_merge_forward(x1, gb, gc, vv, sga, sgb, yssm, u, mod, conv_full, d_skip,
                                          gw["w_conv_out"], gw["w_glu"], gw["w_ssm_out"], gw["w_out"], nb)
    (x3, a3, b3, f3, h3), _ = ffn_forward(x2, mod, g_ffn2, gw["w1_b"], gw["w3_b"], gw["w2_b"], 2, nb,
                                          "ffn_b_forward")
    loss_parts, dx3, dg_final = loss_forward_backward(x3, loss_target.reshape(t, d), g_final.reshape(1, d))
    loss_here = jnp.sum(loss_parts[:, 0, 0]).reshape(1)

    part, received = {}, {}

    def exchange_stage(ns):
        return ExchangeStage([_slots(part[n]) for n in ns])

    (da3, db3, sw3, df3, dgt3), _ = ffn_backward_hidden(dx3, a3, b3, f3, mod, gw["w2_b"], 2, nb,
                                                        "ffn_b_backward_hidden")
    (dx2, dmod3, dg_ffn2), _ = ffn_backward_input(dx3, x2, da3, db3, mod, g_ffn2, gw["w1_b"], gw["w3_b"],
                                                  2, nb, "ffn_b_backward_input")
    (part["w1_b"],), _ = nn_matmul(da3, h3, "grad_w1_b")
    (part["w3_b"],), _ = nn_matmul(db3, h3, "grad_w3_b")
    (part["w2_b"],), _ = nn_matmul(sw3, df3, "grad_w2_b")
    (dgla, dglb, dgb, dconv, ds0, dgt2, dd_skip, dw_out, dw_co, dw_so_t, dw_glu), got = mixer_merge_backward(
        dx2, mix, ya, yb, gb, gc, vv, sga, sgb, yssm, u, mod, conv_full, d_skip,
        gw["w_conv_out"], gw["w_glu"], gw["w_ssm_out"], gw["w_out"], nb, carried=exchange_stage(("w1_b", "w3_b")))
    received.update(zip(("w1_b", "w3_b"), got))
    part["w_out"] = dw_out.astype(BF16)
    part["w_conv_out"] = dw_co.astype(BF16)
    part["w_ssm_out"] = dw_so_t.astype(BF16)
    part["w_glu"] = dw_glu.reshape(sw_n // glu_fold, d).astype(BF16)
    (du, dab, dbd, dcd), got = ssm_backward(u, ds0, st, bd, cd, abar8, mlb, d_skip, nb,
                                            carried=exchange_stage(("w2_b",) + groups["mixer"][1:]))
    received.update(zip(("w2_b",) + groups["mixer"][1:], got))
    dgc, dvv, dconv_w = conv_backward(dconv, gc, vv, conv_full, nb)
    part["w_in"] = jnp.concatenate([tn_matmul(p, h2, "grad_w_in_%d" % k)
                                    for k, p in enumerate((dgb, dgc, dvv, du, dgla, dglb))], axis=0)
    (dx1, dmod2, dg_mix), got = mixer_proj_backward(dgb, dgc, dvv, du, dgla, dglb, dx2, x1, mod, g_mix, gw["w_in"], nb,
                                                    carried=exchange_stage(("w_in",)))
    received["w_in"] = got[0]
    def pack(parts):
        flat = jnp.concatenate([a.reshape(-1) for a in parts.values()])
        rows = -(-flat.shape[0] // (128 * GATHER_ROWS)) * GATHER_ROWS
        return jnp.pad(flat, (0, rows * 128 - flat.shape[0])).reshape(rows, 128)

    def unpack(flat, parts):
        out, off = {}, 0
        for key, like in parts.items():
            n = math.prod(like.shape)
            out[key], off = flat[..., off:off + n].reshape(flat.shape[:-1] + like.shape), off + n
        return out

    dabr, dabi, dbbr, dbbi, dcr, dci = ssm_table_grads(dab, dbd, dcd, g_n, p_n, h_n)
    early = dict(gmod=jnp.concatenate([dmod2, dgt2, dmod3, dgt3], axis=1), g_mix=dg_mix, g_ffn2=dg_ffn2,
                 g_final=dg_final, d_skip=dd_skip, abr=dabr, abi=dabi, bbr=dbbr, bbi=dbbi, c_re=dcr, c_im=dci,
                 conv_w=dconv_w, loss=loss_here)
    (da1, db1, sw1, df1, dgt1), _ = ffn_backward_hidden(dx1, a1, b1, f1, mod, gw["w2_a"], 0, nb,
                                                        "ffn_a_backward_hidden")
    (part["w2_a"],), (early_all,) = nn_matmul(sw1, df1, "grad_w2_a", carried=GatherStage([pack(early)]))
    (part["w1_a"],), got = nn_matmul(da1, h1, "grad_w1_a", carried=exchange_stage(("w2_a",)))
    received["w2_a"] = got[0]
    (part["w3_a"],), got = nn_matmul(db1, h1, "grad_w3_a", carried=exchange_stage(("w1_a",)))
    received["w1_a"] = got[0]
    (dx0, dmod1, dg_ffn1), got = ffn_backward_input(dx1, x0, da1, db1, mod, g_ffn1, gw["w1_a"], gw["w3_a"],
                                                    0, nb, "ffn_a_backward_input", carried=exchange_stage(("w3_a",)))
    received["w3_a"] = got[0]
    late = dict(gmod=jnp.concatenate([dmod1, dgt1], axis=1), g_ffn1=dg_ffn1)
    (late_all,) = run_stage(GatherStage([pack(late)]), "gather_ffn_a_small_grads")

    tot, per_dev = {}, {}
    for parts, gathered, name in ((early, early_all, "early"), (late, late_all, "late")):
        slots = gathered.reshape(N_DEV, -1, 128)
        total = sum_slots(slots, "sum_small_grads_" + name).reshape(-1)
        for key, val in unpack(total, parts).items():
            tot[name + key if key == "gmod" else key] = val
        per_dev[name] = unpack(slots.reshape(N_DEV, -1), parts)["gmod"]
    gmod_all = jnp.concatenate([per_dev["late"], per_dev["early"]], axis=2).reshape(N_DEV * nb, N_MOD * d)
    gmod_tot = jnp.concatenate([tot["lategmod"], tot["earlygmod"]], axis=1)
    g_a_re, g_a_im, g_b_re, g_b_im, g_log_dt = disc_vjp((tot["abr"], tot["abi"], tot["bbr"], tot["bbi"]))

    loss = tot["loss"].reshape(())
    grads = {}
    grads["b_ada"] = sum_rows(gmod_tot.reshape(nb, N_MOD * d))
    grads["g_ffn1"], grads["g_mix"], grads["g_ffn2"] = tot["g_ffn1"], tot["g_mix"], tot["g_ffn2"]
    grads["g_final"] = tot["g_final"].reshape(d)
    grads["d_skip"] = tot["d_skip"]
    grads["a_re"], grads["a_im"], grads["log_dt"] = g_a_re[None], g_a_im[None], g_log_dt[None]
    grads["b_re"], grads["b_im"] = g_b_re[None], g_b_im[None]
    grads["c_re"], grads["c_im"] = tot["c_re"][None], tot["c_im"][None]
    grads["conv_w"] = lax.dynamic_slice(tot["conv_w"], (0, me * conv_w.shape[2]), (CONV_K, conv_w.shape[2]))[None]

    delta, new_m, new_v = {}, {}, {}
    for name in big:
        wmv = (weights[name][0], mom1[name][0], mom2[name][0])
        if name in transposed:
            outs = sum_adamw_update(received[name], *[a.T for a in wmv], "adamw_" + name)
            gsum, dl, mm, vn = [a.T for a in outs]
        elif received[name].shape[1:] == weights[name].shape[1:]:
            gsum, dl, mm, vn = sum_adamw_update(received[name], *wmv, "adamw_" + name)
        else:
            gsum = sum_slots(received[name], "sum_" + name).reshape(weights[name].shape[1:])
            dl, mm, vn = adamw_update(*wmv[:1], gsum, *wmv[1:], "adamw_" + name)
        grads[name] = gsum[None]
        delta[name], new_m[name], new_v[name] = dl[None], mm[None], vn[None]

    gmod_cols = lax.dynamic_slice(gmod_all, (0, me * ada_cols), (N_DEV * nb, ada_cols))
    g_wada, d_wada, m_wada, v_wada = ada_backward_update(c_all, gmod_cols, w_ada[0], m_w_ada[0], v_w_ada[0])
    grads["w_ada"], delta["w_ada"], new_m["w_ada"], new_v["w_ada"] = g_wada[None], d_wada[None], m_wada[None], v_wada[None]

    small_names = [n for n in names if n not in big and n != "w_ada"]

    narrow = ("b_re", "b_im")

    def as2d(n, a):
        a = jnp.swapaxes(a.reshape(weights[n].shape), -1, -2) if n in narrow else a
        return a.reshape(-1, a.shape[-1])

    def from2d(n, a):
        shape = weights[n].shape
        return jnp.swapaxes(a.reshape(shape[:-2] + (shape[-1], shape[-2])), -1, -2) if n in narrow else a.reshape(shape)

    sw_, sg_, sm_, sv_ = ([as2d(n, src[n]) for n in small_names] for src in (weights, grads, mom1, mom2))
    sd, smo, svo = adamw_update_small(sw_, sg_, sm_, sv_)
    for n, dl, mm, vn in zip(small_names, sd, smo, svo):
        grads[n] = grads[n].reshape(weights[n].shape)
        delta[n], new_m[n], new_v[n] = from2d(n, dl), from2d(n, mm), from2d(n, vn)

    grad_x = dx0.reshape(nb, s, d)
    return (loss, grad_x, *[grads[n] for n in names], *[delta[n] for n in names],
            *[new_m[n] for n in names], *[new_v[n] for n in names])


def sum_rows(a):
    r, c = a.shape

    def body(a_ref, o_ref):
        acc = a_ref[0:1, :]
        for j in range(1, r):
            acc = acc + a_ref[j:j + 1, :]
        o_ref[...] = acc

    return pl.pallas_call(body, name="sum_rows", out_shape=jax.ShapeDtypeStruct((1, c), F32),
                          compiler_params=_params())(a)
```
